```python
import jax, jax.numpy as jnp
from jax import lax
import numpy as np

D_MODEL = 1024
BATCH = 8
SEQ = 2048
DEPTH = 4

ATTN_HEADS = 8
ATTN_KV_HEADS = 2
HEAD_DIM = 64
WINDOW = 128
BLOCK = 128
DN_HEADS = 8
DN_DK = 64
DN_DV = 64
DN_CONV = 4
CHUNK = 64
CONV_DIM = D_MODEL
CONV_WIDTH = 31
D_FF = 2816
EPS = 1e-6

Q_A = ATTN_HEADS * HEAD_DIM
KV_A = ATTN_KV_HEADS * HEAD_DIM
QK_B = DN_HEADS * DN_DK
V_B = DN_HEADS * DN_DV
QKV_B = 2 * QK_B + V_B
IN_SPLIT_SIZES = (Q_A, KV_A, KV_A, QKV_B, V_B, DN_HEADS, DN_HEADS)
IN_COLS = sum(IN_SPLIT_SIZES)
MIX_WIDTH = Q_A + V_B
N_EVEN = (DEPTH + 1) // 2
N_ODD = DEPTH // 2

kernel_name = "hybrid_swa_deltanet_conformer_macaron"


def rmsnorm(x, w):
    xf = x.astype(jnp.float32)
    y = xf * lax.rsqrt(jnp.mean(xf * xf, axis=-1, keepdims=True) + EPS)
    return (y * w.astype(jnp.float32)).astype(x.dtype)


def layernorm(x, w, b):
    xf = x.astype(jnp.float32)
    mu = jnp.mean(xf, axis=-1, keepdims=True)
    xc = xf - mu
    y = xc * lax.rsqrt(jnp.mean(xc * xc, axis=-1, keepdims=True) + EPS)
    return (y * w.astype(jnp.float32) + b.astype(jnp.float32)).astype(x.dtype)


def l2norm(x):
    xf = x.astype(jnp.float32)
    return xf * lax.rsqrt(jnp.sum(xf * xf, axis=-1, keepdims=True) + EPS)


def causal_depthwise_conv(x, w):
    k_width = w.shape[0]
    return lax.conv_general_dilated(
        x, w[:, None, :].astype(x.dtype), window_strides=(1,), padding=[(k_width - 1, 0)],
        dimension_numbers=('NWC', 'WIO', 'NWC'), feature_group_count=x.shape[-1])


def swiglu(x, w_gate, w_up, w_down):
    return (jax.nn.silu(x @ w_gate) * (x @ w_up)) @ w_down


def alibi_slopes(n_heads):
    return jnp.asarray(2.0 ** (-8.0 * np.arange(1, n_heads + 1) / n_heads), dtype=jnp.float32)


def sliding_window_attention(q, k, v, sinks):
    B, T, Hq, d = q.shape
    Hkv = k.shape[2]
    G = Hq // Hkv
    N = T // BLOCK
    qb = q.reshape(B, N, BLOCK, Hkv, G, d)

    def with_prev(t):
        tb = t.reshape(B, N, BLOCK, Hkv, d)
        prev = jnp.pad(tb, ((0, 0), (1, 0), (0, 0), (0, 0), (0, 0)))[:, :-1]
        return jnp.concatenate([prev, tb], axis=2)

    kb, vb = with_prev(k), with_prev(v)
    s = jnp.einsum('bnikgd,bnjkd->bkgnij', qb, kb).astype(jnp.float32) * (d ** -0.5)
    i = jnp.arange(BLOCK)[:, None]
    j = jnp.arange(2 * BLOCK)[None, :]
    dist = i + BLOCK - j
    blk = jnp.arange(N)[:, None, None]
    valid = (dist >= 0) & (dist < WINDOW) & ((blk > 0) | (j >= BLOCK))
    slopes = alibi_slopes(Hq).reshape(Hkv, G)[:, :, None, None, None]
    s = s - slopes * dist.astype(jnp.float32)
    s = jnp.where(valid, s, -1e30)
    sink = sinks.astype(jnp.float32).reshape(Hkv, G)[:, :, None, None, None]
    m = jnp.maximum(jnp.max(s, axis=-1, keepdims=True), sink)
    e = jnp.exp(s - m)
    p = e / (jnp.sum(e, axis=-1, keepdims=True) + jnp.exp(sink - m))
    o = jnp.einsum('bkgnij,bnjkd->bnikgd', p.astype(v.dtype), vb)
    return o.reshape(B, T, Hq * d)


def gated_delta_rule_chunked(q, k, v, g, beta):
    B, T, H, dk = q.shape
    dv = v.shape[-1]
    N = T // CHUNK
    f32 = jnp.float32

    def chunks(t):
        return t.astype(f32).reshape(B, N, CHUNK, H, -1).transpose(0, 3, 1, 2, 4)

    def chunks_s(t):
        return t.astype(f32).reshape(B, N, CHUNK, H).transpose(0, 3, 1, 2)

    q = chunks(q) * (dk ** -0.5)
    k, v = chunks(k), chunks(v)
    g, beta = chunks_s(g), chunks_s(beta)
    gc = jnp.cumsum(g, axis=-1)
    causal = jnp.tril(jnp.ones((CHUNK, CHUNK), dtype=bool))
    strict = jnp.tril(jnp.ones((CHUNK, CHUNK), dtype=bool), -1)
    diff = gc[..., :, None] - gc[..., None, :]
    decay = jnp.where(causal, jnp.exp(jnp.where(causal, diff, 0.0)), 0.0)
    kb = k * beta[..., None]
    low = jnp.where(strict, jnp.einsum('bhnid,bhnjd->bhnij', kb, k) * decay, 0.0)
    rhs = jnp.concatenate([v * beta[..., None], kb * jnp.exp(gc)[..., None]], axis=-1)
    sol = lax.linalg.triangular_solve(low, rhs, left_side=True, lower=True, unit_diagonal=True)
    u, w = sol[..., :dv], sol[..., dv:]
    attn = jnp.einsum('bhnid,bhnjd->bhnij', q, k) * decay
    q_dec = q * jnp.exp(gc)[..., None]
    k_dec = k * jnp.exp(gc[..., -1:] - gc)[..., None]
    g_last = jnp.exp(gc[..., -1])

    def step(S, xs):
        u_n, w_n, attn_n, qd_n, kd_n, gl_n = xs
        v_new = u_n - jnp.einsum('bhcd,bhde->bhce', w_n, S)
        o_n = jnp.einsum('bhcd,bhde->bhce', qd_n, S) + jnp.einsum('bhij,bhje->bhie', attn_n, v_new)
        S = S * gl_n[..., None, None] + jnp.einsum('bhcd,bhce->bhde', kd_n, v_new)
        return S, o_n

    xs = tuple(jnp.moveaxis(t, 2, 0) for t in (u, w, attn, q_dec, k_dec, g_last))
    S0 = jnp.zeros((B, H, dk, dv), f32)
    _, o = lax.scan(step, S0, xs)
    return o.transpose(1, 0, 3, 2, 4).reshape(B, T, H, dv)


def attn_deltanet_mixer(h, w_in, dn_conv_w, attn_sinks, dn_a_log, dn_dt_bias, dn_norm_w, w_out):
    B, T, _ = h.shape
    proj = h @ w_in
    split_idx = list(np.cumsum(IN_SPLIT_SIZES)[:-1])
    qa, ka, va, qkv_b, z, b_raw, a_raw = jnp.split(proj, split_idx, axis=-1)
    att = sliding_window_attention(qa.reshape(B, T, ATTN_HEADS, HEAD_DIM),
                                   ka.reshape(B, T, ATTN_KV_HEADS, HEAD_DIM),
                                   va.reshape(B, T, ATTN_KV_HEADS, HEAD_DIM), attn_sinks)
    qkv_b = jax.nn.silu(causal_depthwise_conv(qkv_b, dn_conv_w))
    qb, kb, vb = jnp.split(qkv_b, [QK_B, 2 * QK_B], axis=-1)
    qb = l2norm(qb.reshape(B, T, DN_HEADS, DN_DK))
    kb = l2norm(kb.reshape(B, T, DN_HEADS, DN_DK))
    vb = vb.reshape(B, T, DN_HEADS, DN_DV)
    beta = jax.nn.sigmoid(b_raw.astype(jnp.float32))
    g = -jnp.exp(dn_a_log.astype(jnp.float32)) * jax.nn.softplus(
        a_raw.astype(jnp.float32) + dn_dt_bias.astype(jnp.float32))
    o = gated_delta_rule_chunked(qb, kb, vb, g, beta)
    o = rmsnorm(o, dn_norm_w) * jax.nn.silu(z.reshape(B, T, DN_HEADS, DN_DV).astype(jnp.float32))
    mix = jnp.concatenate([att, o.reshape(B, T, V_B).astype(h.dtype)], axis=-1)
    return mix @ w_out


def conformer_conv_module(h, w_pw1, b_pw1, w_dw, b_dw, ln_w, ln_b, w_pw2, b_pw2):
    u = h @ w_pw1 + b_pw1
    u = u[..., :CONV_DIM] * jax.nn.sigmoid(u[..., CONV_DIM:])
    u = causal_depthwise_conv(u, w_dw) + b_dw
    u = jax.nn.silu(layernorm(u, ln_w, ln_b))
    return u @ w_pw2 + b_pw2


def _fwd_setup_inputs(seed: int = 0) -> dict:
    key = jax.random.key(seed)
    ks = jax.random.split(key, 24)
    f32 = jnp.float32

    def nrm(k, shape, scale):
        return jax.random.normal(k, shape, f32) * scale

    dt = jnp.exp(jax.random.uniform(ks[10], (N_EVEN, DN_HEADS), f32,
                                    np.log(1e-3), np.log(1e-1)))
    return {
        "x": nrm(ks[0], (BATCH, SEQ, D_MODEL), 1.0),
        "norm_w": 1.0 + nrm(ks[1], (DEPTH, 3, D_MODEL), 0.02),
        "ffn_w_gate": nrm(ks[2], (DEPTH, 2, D_MODEL, D_FF), D_MODEL ** -0.5),
        "ffn_w_up": nrm(ks[3], (DEPTH, 2, D_MODEL, D_FF), D_MODEL ** -0.5),
        "ffn_w_down": nrm(ks[4], (DEPTH, 2, D_FF, D_MODEL), D_FF ** -0.5),
        "mix_w_in": nrm(ks[5], (N_EVEN, D_MODEL, IN_COLS), D_MODEL ** -0.5),
        "dn_conv_w": nrm(ks[6], (N_EVEN, DN_CONV, QKV_B), DN_CONV ** -0.5),
        "attn_sinks": nrm(ks[7], (N_EVEN, ATTN_HEADS), 0.5),
        "dn_a_log": jnp.log(jax.random.uniform(ks[8], (N_EVEN, DN_HEADS), f32, 1.0, 16.0)),
        "dn_dt_bias": dt + jnp.log(-jnp.expm1(-dt)),
        "dn_norm_w": 1.0 + nrm(ks[9], (N_EVEN, DN_DV), 0.02),
        "mix_w_out": nrm(ks[11], (N_EVEN, MIX_WIDTH, D_MODEL), MIX_WIDTH ** -0.5),
        "conv_w_pw1": nrm(ks[12], (N_ODD, D_MODEL, 2 * CONV_DIM), D_MODEL ** -0.5),
        "conv_b_pw1": nrm(ks[13], (N_ODD, 2 * CONV_DIM), 0.02),
        "conv_w_dw": nrm(ks[14], (N_ODD, CONV_WIDTH, CONV_DIM), CONV_WIDTH ** -0.5),
        "conv_b_dw": nrm(ks[15], (N_ODD, CONV_DIM), 0.02),
        "conv_ln_w": 1.0 + nrm(ks[16], (N_ODD, CONV_DIM), 0.02),
        "conv_ln_b": nrm(ks[17], (N_ODD, CONV_DIM), 0.02),
        "conv_w_pw2": nrm(ks[18], (N_ODD, CONV_DIM, D_MODEL), CONV_DIM ** -0.5),
        "conv_b_pw2": nrm(ks[19], (N_ODD, D_MODEL), 0.02),
        "final_norm_w": 1.0 + nrm(ks[20], (D_MODEL,), 0.02),
    }


def _fwd_reference(x, norm_w, ffn_w_gate, ffn_w_up, ffn_w_down, mix_w_in, dn_conv_w, attn_sinks,
              dn_a_log, dn_dt_bias, dn_norm_w, mix_w_out, conv_w_pw1, conv_b_pw1, conv_w_dw,
              conv_b_dw, conv_ln_w, conv_ln_b, conv_w_pw2, conv_b_pw2, final_norm_w):
    for layer in range(DEPTH):
        x = x + 0.5 * swiglu(rmsnorm(x, norm_w[layer, 0]),
                             ffn_w_gate[layer, 0], ffn_w_up[layer, 0], ffn_w_down[layer, 0])
        h = rmsnorm(x, norm_w[layer, 1])
        if layer % 2 == 0:
            e = layer // 2
            x = x + attn_deltanet_mixer(h, mix_w_in[e], dn_conv_w[e], attn_sinks[e], dn_a_log[e],
                                        dn_dt_bias[e], dn_norm_w[e], mix_w_out[e])
        else:
            c = layer // 2
            x = x + conformer_conv_module(h, conv_w_pw1[c], conv_b_pw1[c], conv_w_dw[c], conv_b_dw[c],
                                          conv_ln_w[c], conv_ln_b[c], conv_w_pw2[c], conv_b_pw2[c])
        x = x + 0.5 * swiglu(rmsnorm(x, norm_w[layer, 2]),
                             ffn_w_gate[layer, 1], ffn_w_up[layer, 1], ffn_w_down[layer, 1])
    return rmsnorm(x, final_norm_w)


import jax as _jax
import jax.numpy as _jnp

TWIN_FORMAT = 'train_step'
FWD_PARAMS = ['x', 'norm_w', 'ffn_w_gate', 'ffn_w_up', 'ffn_w_down', 'mix_w_in', 'dn_conv_w', 'attn_sinks', 'dn_a_log', 'dn_dt_bias', 'dn_norm_w', 'mix_w_out', 'conv_w_pw1', 'conv_b_pw1', 'conv_w_dw', 'conv_b_dw', 'conv_ln_w', 'conv_ln_b', 'conv_w_pw2', 'conv_b_pw2', 'final_norm_w']
TWIN_WEIGHTS = ['norm_w', 'ffn_w_gate', 'ffn_w_up', 'ffn_w_down', 'mix_w_in', 'dn_conv_w', 'attn_sinks', 'dn_a_log', 'dn_dt_bias', 'dn_norm_w', 'mix_w_out', 'conv_w_pw1', 'conv_b_pw1', 'conv_w_dw', 'conv_b_dw', 'conv_ln_w', 'conv_ln_b', 'conv_w_pw2', 'conv_b_pw2', 'final_norm_w']
TWIN_DIFF_INPUT = 'x'
TWIN_INPUTS = ['x', 'norm_w', 'ffn_w_gate', 'ffn_w_up', 'ffn_w_down', 'mix_w_in', 'dn_conv_w', 'attn_sinks', 'dn_a_log', 'dn_dt_bias', 'dn_norm_w', 'mix_w_out', 'conv_w_pw1', 'conv_b_pw1', 'conv_w_dw', 'conv_b_dw', 'conv_ln_w', 'conv_ln_b', 'conv_w_pw2', 'conv_b_pw2', 'final_norm_w', 'loss_target', 'm_norm_w', 'm_ffn_w_gate', 'm_ffn_w_up', 'm_ffn_w_down', 'm_mix_w_in', 'm_dn_conv_w', 'm_attn_sinks', 'm_dn_a_log', 'm_dn_dt_bias', 'm_dn_norm_w', 'm_mix_w_out', 'm_conv_w_pw1', 'm_conv_b_pw1', 'm_conv_w_dw', 'm_conv_b_dw', 'm_conv_ln_w', 'm_conv_ln_b', 'm_conv_w_pw2', 'm_conv_b_pw2', 'm_final_norm_w', 'v_norm_w', 'v_ffn_w_gate', 'v_ffn_w_up', 'v_ffn_w_down', 'v_mix_w_in', 'v_dn_conv_w', 'v_attn_sinks', 'v_dn_a_log', 'v_dn_dt_bias', 'v_dn_norm_w', 'v_mix_w_out', 'v_conv_w_pw1', 'v_conv_b_pw1', 'v_conv_w_dw', 'v_conv_b_dw', 'v_conv_ln_w', 'v_conv_ln_b', 'v_conv_w_pw2', 'v_conv_b_pw2', 'v_final_norm_w']
TWIN_OUTPUTS = ['loss', 'grad_x', 'grad_norm_w', 'grad_ffn_w_gate', 'grad_ffn_w_up', 'grad_ffn_w_down', 'grad_mix_w_in', 'grad_dn_conv_w', 'grad_attn_sinks', 'grad_dn_a_log', 'grad_dn_dt_bias', 'grad_dn_norm_w', 'grad_mix_w_out', 'grad_conv_w_pw1', 'grad_conv_b_pw1', 'grad_conv_w_dw', 'grad_conv_b_dw', 'grad_conv_ln_w', 'grad_conv_ln_b', 'grad_conv_w_pw2', 'grad_conv_b_pw2', 'grad_final_norm_w', 'delta_norm_w', 'delta_ffn_w_gate', 'delta_ffn_w_up', 'delta_ffn_w_down', 'delta_mix_w_in', 'delta_dn_conv_w', 'delta_attn_sinks', 'delta_dn_a_log', 'delta_dn_dt_bias', 'delta_dn_norm_w', 'delta_mix_w_out', 'delta_conv_w_pw1', 'delta_conv_b_pw1', 'delta_conv_w_dw', 'delta_conv_b_dw', 'delta_conv_ln_w', 'delta_conv_ln_b', 'delta_conv_w_pw2', 'delta_conv_b_pw2', 'delta_final_norm_w', 'new_m_norm_w', 'new_m_ffn_w_gate', 'new_m_ffn_w_up', 'new_m_ffn_w_down', 'new_m_mix_w_in', 'new_m_dn_conv_w', 'new_m_attn_sinks', 'new_m_dn_a_log', 'new_m_dn_dt_bias', 'new_m_dn_norm_w', 'new_m_mix_w_out', 'new_m_conv_w_pw1', 'new_m_conv_b_pw1', 'new_m_conv_w_dw', 'new_m_conv_b_dw', 'new_m_conv_ln_w', 'new_m_conv_ln_b', 'new_m_conv_w_pw2', 'new_m_conv_b_pw2', 'new_m_final_norm_w', 'new_v_norm_w', 'new_v_ffn_w_gate', 'new_v_ffn_w_up', 'new_v_ffn_w_down', 'new_v_mix_w_in', 'new_v_dn_conv_w', 'new_v_attn_sinks', 'new_v_dn_a_log', 'new_v_dn_dt_bias', 'new_v_dn_norm_w', 'new_v_mix_w_out', 'new_v_conv_w_pw1', 'new_v_conv_b_pw1', 'new_v_conv_w_dw', 'new_v_conv_b_dw', 'new_v_conv_ln_w', 'new_v_conv_ln_b', 'new_v_conv_w_pw2', 'new_v_conv_b_pw2', 'new_v_final_norm_w']
TWIN_LEAF_KINDS = {'loss': 'loss', 'grad_x': 'grad_x', 'grad_norm_w': 'grad_w', 'grad_ffn_w_gate': 'grad_w', 'grad_ffn_w_up': 'grad_w', 'grad_ffn_w_down': 'grad_w', 'grad_mix_w_in': 'grad_w', 'grad_dn_conv_w': 'grad_w', 'grad_attn_sinks': 'grad_w', 'grad_dn_a_log': 'grad_w', 'grad_dn_dt_bias': 'grad_w', 'grad_dn_norm_w': 'grad_w', 'grad_mix_w_out': 'grad_w', 'grad_conv_w_pw1': 'grad_w', 'grad_conv_b_pw1': 'grad_w', 'grad_conv_w_dw': 'grad_w', 'grad_conv_b_dw': 'grad_w', 'grad_conv_ln_w': 'grad_w', 'grad_conv_ln_b': 'grad_w', 'grad_conv_w_pw2': 'grad_w', 'grad_conv_b_pw2': 'grad_w', 'grad_final_norm_w': 'grad_w', 'delta_norm_w': 'delta_w', 'delta_ffn_w_gate': 'delta_w', 'delta_ffn_w_up': 'delta_w', 'delta_ffn_w_down': 'delta_w', 'delta_mix_w_in': 'delta_w', 'delta_dn_conv_w': 'delta_w', 'delta_attn_sinks': 'delta_w', 'delta_dn_a_log': 'delta_w', 'delta_dn_dt_bias': 'delta_w', 'delta_dn_norm_w': 'delta_w', 'delta_mix_w_out': 'delta_w', 'delta_conv_w_pw1': 'delta_w', 'delta_conv_b_pw1': 'delta_w', 'delta_conv_w_dw': 'delta_w', 'delta_conv_b_dw': 'delta_w', 'delta_conv_ln_w': 'delta_w', 'delta_conv_ln_b': 'delta_w', 'delta_conv_w_pw2': 'delta_w', 'delta_conv_b_pw2': 'delta_w', 'delta_final_norm_w': 'delta_w', 'new_m_norm_w': 'new_m', 'new_m_ffn_w_gate': 'new_m', 'new_m_ffn_w_up': 'new_m', 'new_m_ffn_w_down': 'new_m', 'new_m_mix_w_in': 'new_m', 'new_m_dn_conv_w': 'new_m', 'new_m_attn_sinks': 'new_m', 'new_m_dn_a_log': 'new_m', 'new_m_dn_dt_bias': 'new_m', 'new_m_dn_norm_w': 'new_m', 'new_m_mix_w_out': 'new_m', 'new_m_conv_w_pw1': 'new_m', 'new_m_conv_b_pw1': 'new_m', 'new_m_conv_w_dw': 'new_m', 'new_m_conv_b_dw': 'new_m', 'new_m_conv_ln_w': 'new_m', 'new_m_conv_ln_b': 'new_m', 'new_m_conv_w_pw2': 'new_m', 'new_m_conv_b_pw2': 'new_m', 'new_m_final_norm_w': 'new_m', 'new_v_norm_w': 'new_v', 'new_v_ffn_w_gate': 'new_v', 'new_v_ffn_w_up': 'new_v', 'new_v_ffn_w_down': 'new_v', 'new_v_mix_w_in': 'new_v', 'new_v_dn_conv_w': 'new_v', 'new_v_attn_sinks': 'new_v', 'new_v_dn_a_log': 'new_v', 'new_v_dn_dt_bias': 'new_v', 'new_v_dn_norm_w': 'new_v', 'new_v_mix_w_out': 'new_v', 'new_v_conv_w_pw1': 'new_v', 'new_v_conv_b_pw1': 'new_v', 'new_v_conv_w_dw': 'new_v', 'new_v_conv_b_dw': 'new_v', 'new_v_conv_ln_w': 'new_v', 'new_v_conv_ln_b': 'new_v', 'new_v_conv_w_pw2': 'new_v', 'new_v_conv_b_pw2': 'new_v', 'new_v_final_norm_w': 'new_v'}


def _forward(args):
    return _fwd_reference(*[args[k] for k in FWD_PARAMS])


def _output_shape():
    out = _jax.eval_shape(lambda: _forward(_fwd_setup_inputs(0)))
    return out.shape, out.dtype

N_MICROBATCH = 1
ADAM_LR = 0.001
ADAM_B1 = 0.9
ADAM_B2 = 0.999
ADAM_EPS = 1e-08
ADAM_WD = 0.01
ADAM_STEP = 10
PER_EXAMPLE_BATCH_AXIS = {'x': 0, 'loss_target': 0}
SHARED_INPUTS = []
_WEIGHT_DTYPES = {'norm_w': _jnp.float32, 'ffn_w_gate': _jnp.float32, 'ffn_w_up': _jnp.float32, 'ffn_w_down': _jnp.float32, 'mix_w_in': _jnp.float32, 'dn_conv_w': _jnp.float32, 'attn_sinks': _jnp.float32, 'dn_a_log': _jnp.float32, 'dn_dt_bias': _jnp.float32, 'dn_norm_w': _jnp.float32, 'mix_w_out': _jnp.float32, 'conv_w_pw1': _jnp.float32, 'conv_b_pw1': _jnp.float32, 'conv_w_dw': _jnp.float32, 'conv_b_dw': _jnp.float32, 'conv_ln_w': _jnp.float32, 'conv_ln_b': _jnp.float32, 'conv_w_pw2': _jnp.float32, 'conv_b_pw2': _jnp.float32, 'final_norm_w': _jnp.float32}
MOMENT_SCALE = {'norm_w': 6.730166e-02, 'ffn_w_gate': 2.252258e-02, 'ffn_w_up': 2.181803e-02, 'ffn_w_down': 3.618490e-02, 'mix_w_in': 6.415313e-02, 'dn_conv_w': 6.820658e-02, 'attn_sinks': 3.747776e-02, 'dn_a_log': 2.762196e-01, 'dn_dt_bias': 2.744661e-01, 'dn_norm_w': 2.362999e-01, 'mix_w_out': 5.777235e-02, 'conv_w_pw1': 4.620938e-02, 'conv_b_pw1': 5.880292e-02, 'conv_w_dw': 6.105286e-02, 'conv_b_dw': 1.443081e-01, 'conv_ln_w': 7.526853e-02, 'conv_ln_b': 7.010403e-02, 'conv_w_pw2': 6.009449e-02, 'conv_b_pw2': 1.134621e-01, 'final_norm_w': 1.603820e+01}


def _to_microbatches(a, axis):
    t = _jnp.moveaxis(a, axis, 0)
    t = t.reshape((N_MICROBATCH, t.shape[0] // N_MICROBATCH) + t.shape[1:])
    return _jnp.moveaxis(t, 1, axis + 1)


def setup_inputs(seed: int = 0) -> dict:
    inp = _fwd_setup_inputs(seed)
    key = _jax.random.fold_in(_jax.random.key(seed), 7919)
    shape, _ = _output_shape()
    out = dict(inp)
    out["loss_target"] = _jax.random.normal(_jax.random.fold_in(key, 0), shape, _jnp.float32)
    for i, name in enumerate(TWIN_WEIGHTS):
        w = inp[name].astype(_jnp.float32)
        if MOMENT_SCALE is None:
            s = _jnp.sqrt(_jnp.mean(_jnp.square(w)) + 1e-30)
        else:
            s = MOMENT_SCALE[name]
        km, kv = _jax.random.split(_jax.random.fold_in(key, i + 1))
        out[name] = w
        out["m_" + name] = s * _jax.random.normal(km, w.shape, _jnp.float32)
        out["v_" + name] = (s * s) * _jax.random.uniform(kv, w.shape, _jnp.float32, 0.5, 1.5)
    if N_MICROBATCH > 1:
        for name, axis in PER_EXAMPLE_BATCH_AXIS.items():
            out[name] = _to_microbatches(out[name], axis)
    return {'x': out['x'], 'norm_w': out['norm_w'], 'ffn_w_gate': out['ffn_w_gate'], 'ffn_w_up': out['ffn_w_up'], 'ffn_w_down': out['ffn_w_down'], 'mix_w_in': out['mix_w_in'], 'dn_conv_w': out['dn_conv_w'], 'attn_sinks': out['attn_sinks'], 'dn_a_log': out['dn_a_log'], 'dn_dt_bias': out['dn_dt_bias'], 'dn_norm_w': out['dn_norm_w'], 'mix_w_out': out['mix_w_out'], 'conv_w_pw1': out['conv_w_pw1'], 'conv_b_pw1': out['conv_b_pw1'], 'conv_w_dw': out['conv_w_dw'], 'conv_b_dw': out['conv_b_dw'], 'conv_ln_w': out['conv_ln_w'], 'conv_ln_b': out['conv_ln_b'], 'conv_w_pw2': out['conv_w_pw2'], 'conv_b_pw2': out['conv_b_pw2'], 'final_norm_w': out['final_norm_w'], 'loss_target': out['loss_target'], 'm_norm_w': out['m_norm_w'], 'm_ffn_w_gate': out['m_ffn_w_gate'], 'm_ffn_w_up': out['m_ffn_w_up'], 'm_ffn_w_down': out['m_ffn_w_down'], 'm_mix_w_in': out['m_mix_w_in'], 'm_dn_conv_w': out['m_dn_conv_w'], 'm_attn_sinks': out['m_attn_sinks'], 'm_dn_a_log': out['m_dn_a_log'], 'm_dn_dt_bias': out['m_dn_dt_bias'], 'm_dn_norm_w': out['m_dn_norm_w'], 'm_mix_w_out': out['m_mix_w_out'], 'm_conv_w_pw1': out['m_conv_w_pw1'], 'm_conv_b_pw1': out['m_conv_b_pw1'], 'm_conv_w_dw': out['m_conv_w_dw'], 'm_conv_b_dw': out['m_conv_b_dw'], 'm_conv_ln_w': out['m_conv_ln_w'], 'm_conv_ln_b': out['m_conv_ln_b'], 'm_conv_w_pw2': out['m_conv_w_pw2'], 'm_conv_b_pw2': out['m_conv_b_pw2'], 'm_final_norm_w': out['m_final_norm_w'], 'v_norm_w': out['v_norm_w'], 'v_ffn_w_gate': out['v_ffn_w_gate'], 'v_ffn_w_up': out['v_ffn_w_up'], 'v_ffn_w_down': out['v_ffn_w_down'], 'v_mix_w_in': out['v_mix_w_in'], 'v_dn_conv_w': out['v_dn_conv_w'], 'v_attn_sinks': out['v_attn_sinks'], 'v_dn_a_log': out['v_dn_a_log'], 'v_dn_dt_bias': out['v_dn_dt_bias'], 'v_dn_norm_w': out['v_dn_norm_w'], 'v_mix_w_out': out['v_mix_w_out'], 'v_conv_w_pw1': out['v_conv_w_pw1'], 'v_conv_b_pw1': out['v_conv_b_pw1'], 'v_conv_w_dw': out['v_conv_w_dw'], 'v_conv_b_dw': out['v_conv_b_dw'], 'v_conv_ln_w': out['v_conv_ln_w'], 'v_conv_ln_b': out['v_conv_ln_b'], 'v_conv_w_pw2': out['v_conv_w_pw2'], 'v_conv_b_pw2': out['v_conv_b_pw2'], 'v_final_norm_w': out['v_final_norm_w']}


def _loss(weights, diff, rest, loss_target):
    with _jax.named_scope("forward"):
        args = {**rest, TWIN_DIFF_INPUT: diff, **{k: w.astype(_WEIGHT_DTYPES[k]) for k, w in weights.items()}}
        y = _forward(args)
    with _jax.named_scope("loss_head"):
        err = _jnp.square(y.astype(_jnp.float32) - loss_target)
        return 0.5 * _jnp.sum(_jnp.mean(err, axis=-1)) if err.ndim else 0.5 * err


def _adamw(w, g, m, v):
    m = ADAM_B1 * m + (1.0 - ADAM_B1) * g
    v = ADAM_B2 * v + (1.0 - ADAM_B2) * _jnp.square(g)
    m_hat = m / (1.0 - ADAM_B1 ** ADAM_STEP)
    v_hat = v / (1.0 - ADAM_B2 ** ADAM_STEP)
    delta = -ADAM_LR * (m_hat / (_jnp.sqrt(v_hat) + ADAM_EPS) + ADAM_WD * w)
    return delta, m, v


def reference(x, norm_w, ffn_w_gate, ffn_w_up, ffn_w_down, mix_w_in, dn_conv_w, attn_sinks, dn_a_log, dn_dt_bias, dn_norm_w, mix_w_out, conv_w_pw1, conv_b_pw1, conv_w_dw, conv_b_dw, conv_ln_w, conv_ln_b, conv_w_pw2, conv_b_pw2, final_norm_w, loss_target, m_norm_w, m_ffn_w_gate, m_ffn_w_up, m_ffn_w_down, m_mix_w_in, m_dn_conv_w, m_attn_sinks, m_dn_a_log, m_dn_dt_bias, m_dn_norm_w, m_mix_w_out, m_conv_w_pw1, m_conv_b_pw1, m_conv_w_dw, m_conv_b_dw, m_conv_ln_w, m_conv_ln_b, m_conv_w_pw2, m_conv_b_pw2, m_final_norm_w, v_norm_w, v_ffn_w_gate, v_ffn_w_up, v_ffn_w_down, v_mix_w_in, v_dn_conv_w, v_attn_sinks, v_dn_a_log, v_dn_dt_bias, v_dn_norm_w, v_mix_w_out, v_conv_w_pw1, v_conv_b_pw1, v_conv_w_dw, v_conv_b_dw, v_conv_ln_w, v_conv_ln_b, v_conv_w_pw2, v_conv_b_pw2, v_final_norm_w):
    given = dict(x=x, norm_w=norm_w, ffn_w_gate=ffn_w_gate, ffn_w_up=ffn_w_up, ffn_w_down=ffn_w_down, mix_w_in=mix_w_in, dn_conv_w=dn_conv_w, attn_sinks=attn_sinks, dn_a_log=dn_a_log, dn_dt_bias=dn_dt_bias, dn_norm_w=dn_norm_w, mix_w_out=mix_w_out, conv_w_pw1=conv_w_pw1, conv_b_pw1=conv_b_pw1, conv_w_dw=conv_w_dw, conv_b_dw=conv_b_dw, conv_ln_w=conv_ln_w, conv_ln_b=conv_ln_b, conv_w_pw2=conv_w_pw2, conv_b_pw2=conv_b_pw2, final_norm_w=final_norm_w, loss_target=loss_target, m_norm_w=m_norm_w, m_ffn_w_gate=m_ffn_w_gate, m_ffn_w_up=m_ffn_w_up, m_ffn_w_down=m_ffn_w_down, m_mix_w_in=m_mix_w_in, m_dn_conv_w=m_dn_conv_w, m_attn_sinks=m_attn_sinks, m_dn_a_log=m_dn_a_log, m_dn_dt_bias=m_dn_dt_bias, m_dn_norm_w=m_dn_norm_w, m_mix_w_out=m_mix_w_out, m_conv_w_pw1=m_conv_w_pw1, m_conv_b_pw1=m_conv_b_pw1, m_conv_w_dw=m_conv_w_dw, m_conv_b_dw=m_conv_b_dw, m_conv_ln_w=m_conv_ln_w, m_conv_ln_b=m_conv_ln_b, m_conv_w_pw2=m_conv_w_pw2, m_conv_b_pw2=m_conv_b_pw2, m_final_norm_w=m_final_norm_w, v_norm_w=v_norm_w, v_ffn_w_gate=v_ffn_w_gate, v_ffn_w_up=v_ffn_w_up, v_ffn_w_down=v_ffn_w_down, v_mix_w_in=v_mix_w_in, v_dn_conv_w=v_dn_conv_w, v_attn_sinks=v_attn_sinks, v_dn_a_log=v_dn_a_log, v_dn_dt_bias=v_dn_dt_bias, v_dn_norm_w=v_dn_norm_w, v_mix_w_out=v_mix_w_out, v_conv_w_pw1=v_conv_w_pw1, v_conv_b_pw1=v_conv_b_pw1, v_conv_w_dw=v_conv_w_dw, v_conv_b_dw=v_conv_b_dw, v_conv_ln_w=v_conv_ln_w, v_conv_ln_b=v_conv_ln_b, v_conv_w_pw2=v_conv_w_pw2, v_conv_b_pw2=v_conv_b_pw2, v_final_norm_w=v_final_norm_w)
    weights = {n: given[n] for n in TWIN_WEIGHTS}
    shared = {n: given[n] for n in SHARED_INPUTS}
    per_example = {n: given[n] for n in ['x']}
    grad_fn = _jax.value_and_grad(_loss, argnums=(0, 1))

    def one_microbatch(ex, loss_target):
        ex = dict(ex)
        diff = ex.pop(TWIN_DIFF_INPUT)
        return grad_fn(weights, diff, {**shared, **ex}, loss_target)

    if N_MICROBATCH == 1:
        loss, (grad_w, grad_x) = one_microbatch(per_example, given["loss_target"])
    else:
        def body(carry, xs):
            loss_sum, grad_sum = carry
            l_k, (gw_k, gx_k) = one_microbatch(xs[0], xs[1])
            with _jax.named_scope("update"):
                return (loss_sum + l_k, _jax.tree.map(_jnp.add, grad_sum, gw_k)), gx_k

        init = (_jnp.zeros((), _jnp.float32), _jax.tree.map(_jnp.zeros_like, weights))
        (loss, grad_w), grad_x = _jax.lax.scan(body, init, (per_example, given["loss_target"]))
    with _jax.named_scope("update"):
        delta_w, new_m, new_v = {}, {}, {}
        for n in TWIN_WEIGHTS:
            delta_w[n], new_m[n], new_v[n] = _adamw(weights[n], grad_w[n], given["m_" + n], given["v_" + n])
    return (loss, grad_x, *[grad_w[n] for n in TWIN_WEIGHTS], *[delta_w[n] for n in TWIN_WEIGHTS],
            *[new_m[n] for n in TWIN_WEIGHTS], *[new_v[n] for n in TWIN_WEIGHTS])
```

```python
import functools

import jax
import jax.numpy as jnp
from jax import lax
from jax.experimental import pallas as pl
from jax.experimental.pallas import tpu as pltpu

F32, BF16 = jnp.float32, jnp.bfloat16
HIGHEST = lax.Precision.HIGHEST
MESH = pl.DeviceIdType.MESH
ANY = pl.BlockSpec(memory_space=pl.ANY)

T, D, F = 2048, 1024, 2816
DEPTH = 4
EPS = 1e-6
HEADS, HDIM, KV_HEADS, GROUP = 8, 64, 2, 4
WINDOW = BLOCK = 128
CHUNK = 64
NCHUNK = T // CHUNK
DN_CONV, CONV_WIDTH = 4, 31
Q_A, KV_A, QKV_B, V_B = 512, 128, 1536, 512
IN_COLS = 2832
IN_SPLITS = (0, 512, 640, 768, 2304, 2816, 2832)
NCHIP, NDEV = 4, 8
FS = F // NCHIP
LR, B1, B2, AEPS, WD, STEP = 0.001, 0.9, 0.999, 1e-08, 0.01, 10
V7X_VMEM_BYTES = 64 * 1024 * 1024
VMEM_LIMIT = V7X_VMEM_BYTES * 7 // 8
LANES = 128


def _cp(*sem):
    return pltpu.CompilerParams(dimension_semantics=sem, vmem_limit_bytes=VMEM_LIMIT)


def _sds(shape, dtype=F32):
    return jax.ShapeDtypeStruct(tuple(shape), dtype)


def _full(shape):
    nd = len(shape)
    return pl.BlockSpec(tuple(shape), lambda *_: (0,) * nd)


def _dg(a, b, ca, cb, hi=False):
    dims = (((ca,), (cb,)), ((), ()))
    if hi:
        return lax.dot_general(a.astype(F32), b.astype(F32), dims, precision=HIGHEST, preferred_element_type=F32)
    return lax.dot_general(a.astype(BF16), b.astype(BF16), dims, preferred_element_type=F32)


def _make_mm(hi):
    @jax.custom_vjp
    def nn(a, b):
        return _dg(a, b, 1, 0, hi)

    @jax.custom_vjp
    def nt(a, b):
        return _dg(a, b, 1, 1, hi)

    @jax.custom_vjp
    def tn(a, b):
        return _dg(a, b, 0, 0, hi)

    nn.defvjp(lambda a, b: (_dg(a, b, 1, 0, hi), (a, b)),
              lambda r, g: (_dg(g, r[1], 1, 1, hi).astype(r[0].dtype), _dg(r[0], g, 0, 0, hi).astype(r[1].dtype)))
    nt.defvjp(lambda a, b: (_dg(a, b, 1, 1, hi), (a, b)),
              lambda r, g: (_dg(g, r[1], 1, 0, hi).astype(r[0].dtype), _dg(g, r[0], 0, 0, hi).astype(r[1].dtype)))
    tn.defvjp(lambda a, b: (_dg(a, b, 0, 0, hi), (a, b)),
              lambda r, g: (_dg(r[1], g, 1, 1, hi).astype(r[0].dtype), _dg(r[0], g, 1, 0, hi).astype(r[1].dtype)))
    return nn, nt, tn


_nn, _nt, _tn = _make_mm(False)
_nn_hi, _nt_hi, _tn_hi = _make_mm(True)


def _rms(x, w):
    return x * lax.rsqrt(jnp.mean(x * x, axis=-1, keepdims=True) + EPS) * w


def _layernorm(x, w, b):
    xc = x - jnp.mean(x, axis=-1, keepdims=True)
    return xc * lax.rsqrt(jnp.mean(xc * xc, axis=-1, keepdims=True) + EPS) * w + b


def _silu(x):
    return x * jax.nn.sigmoid(x)


def _iota2(shape, dim):
    return lax.broadcasted_iota(jnp.int32, shape, dim)


def _blk_fwd(name, pre, lhs_idx, post, toks, smalls, weights, outs, tm=512):
    nt_, ns, nw = len(toks), len(smalls), len(weights)

    def body(*refs):
        tv = [r[...] for r in refs[:nt_]]
        sv = [r[...] for r in refs[nt_:nt_ + ns]]
        wr = refs[nt_ + ns:nt_ + ns + nw]
        orf = refs[nt_ + ns + nw:]
        lhs = pre(tv, sv)
        ys = [_dg(lhs[i], w[...], 1, 0) for i, w in zip(lhs_idx, wr)]
        for o_ref, o in zip(orf, post(ys, tv, sv)):
            o_ref[...] = o.astype(o_ref.dtype)

    in_specs = ([pl.BlockSpec((tm, a.shape[1]), lambda i: (i, 0)) for a in toks]
                + [_full(a.shape) for a in smalls] + [_full(w.shape) for w in weights])
    out_specs = [pl.BlockSpec((tm, w_), lambda i: (i, 0)) for w_, _ in outs]
    return pl.pallas_call(
        body, grid=(T // tm,), in_specs=in_specs, out_specs=out_specs,
        out_shape=[_sds((T, w_), dt) for w_, dt in outs], name=name, compiler_params=_cp("parallel"),
    )(*toks, *smalls, *weights)


def _blk_bwd(name, pre, lhs_idx, post, toks, smalls, weights, ct_groups, res=None, tm=256, wchunk=512):
    nt_, ns, nw = len(toks), len(smalls), len(weights)
    cts = [a for g in ct_groups for a in g]
    nc = len(cts)
    widths = [sum(a.shape[1] for a in g) for g in ct_groups]
    has_res = res is not None

    def body(*refs):
        p = 0
        tr = refs[p:p + nt_]; p += nt_
        sr = refs[p:p + ns]; p += ns
        wr = refs[p:p + nw]; p += nw
        cr = refs[p:p + nc]; p += nc
        rr = refs[p:p + has_res]; p += has_res
        dtr = refs[p:p + nt_]; p += nt_
        dsr = refs[p:p + ns]; p += ns
        dwr = refs[p:p + nw]; p += nw
        scr = refs[p:]
        i = pl.program_id(0)

        @pl.when(i == 0)
        def _():
            for r in list(dsr) + list(dwr):
                r[...] = jnp.zeros_like(r)

        tv = [r[...] for r in tr]
        sv = [r[...] for r in sr]
        ctv, q, si = [], 0, 0
        for g in ct_groups:
            if len(g) == 1:
                ctv.append(cr[q][...].astype(F32))
            else:
                off = 0
                for j, a in enumerate(g):
                    scr[si][:, off:off + a.shape[1]] = cr[q + j][...].astype(F32)
                    off += a.shape[1]
                ctv.append(scr[si][...])
                si += 1
            q += len(g)

        lhs, vjp_pre = jax.vjp(lambda *a: tuple(pre(list(a[:nt_]), list(a[nt_:]))), *tv, *sv)
        lhs_b = [l.astype(BF16) for l in lhs]
        ys = [_dg(lhs_b[k], w[...], 1, 0) for k, w in zip(lhs_idx, wr)]
        _, vjp_post = jax.vjp(lambda *a: tuple(post(list(a[:nw]), list(a[nw:nw + nt_]), list(a[nw + nt_:]))),
                              *ys, *tv, *sv)
        gp = vjp_post(tuple(ctv))
        dys, dt_post, ds_post = gp[:nw], gp[nw:nw + nt_], gp[nw + nt_:]
        dlhs = [None] * len(lhs)
        for k, w, dy, dw in zip(lhs_idx, wr, dys, dwr):
            dyb = dy.astype(BF16)
            n = w.shape[1]
            for c0 in range(0, n, wchunk):
                c1 = min(n, c0 + wchunk)
                dw[:, c0:c1] += _dg(lhs_b[k], dyb[:, c0:c1], 0, 0)
            d = _dg(dyb, w[...], 1, 1)
            dlhs[k] = d if dlhs[k] is None else dlhs[k] + d
        gq = vjp_pre(tuple(d.astype(l.dtype) for d, l in zip(dlhs, lhs)))
        dt_pre, ds_pre = gq[:nt_], gq[nt_:]
        for j in range(nt_):
            d = dt_post[j] + dt_pre[j]
            if j == 0 and has_res:
                d = d + rr[0][...]
            dtr[j][...] = d
        for j in range(ns):
            dsr[j][...] += ds_post[j] + ds_pre[j]

    tok_spec = lambda a: pl.BlockSpec((tm, a.shape[1]), lambda i: (i, 0))
    in_specs = ([tok_spec(a) for a in toks] + [_full(a.shape) for a in smalls] + [_full(w.shape) for w in weights]
                + [tok_spec(a) for a in cts] + ([tok_spec(res)] if has_res else []))
    out_specs = [tok_spec(a) for a in toks] + [_full(a.shape) for a in smalls] + [_full(w.shape) for w in weights]
    out_shape = ([_sds(a.shape) for a in toks] + [_sds(a.shape) for a in smalls] + [_sds(w.shape) for w in weights])
    scratch = [pltpu.VMEM((tm, wd), F32) for g, wd in zip(ct_groups, widths) if len(g) > 1]
    outs = pl.pallas_call(
        body, grid=(T // tm,), in_specs=in_specs, out_specs=out_specs, out_shape=out_shape,
        scratch_shapes=scratch, name=name, compiler_params=_cp("arbitrary"),
    )(*toks, *smalls, *weights, *cts, *([res] if has_res else []))
    return outs[:nt_], outs[nt_:nt_ + ns], outs[nt_ + ns:]


def _ffn_fwd(name, x, nw, wg, wu, wd, idx, tm=512):
    def body(x_ref, nw_ref, wg_ref, wu_ref, wd_ref, o_ref, h_scr):
        s = pl.program_id(1)

        @pl.when(s == 0)
        def _():
            xv = x_ref[...]
            h_scr[...] = _rms(xv, nw_ref[...]).astype(BF16)
            o_ref[...] = xv

        h = h_scr[...]
        a = _dg(h, wg_ref[...], 1, 0)
        b = _dg(h, wu_ref[...], 1, 0)
        o_ref[...] += 0.5 * _dg(_silu(a) * b, wd_ref[...], 1, 0)

    wspec = lambda r, c: pl.BlockSpec((None, None, r, c), lambda i, s: (s, idx, 0, 0))
    return pl.pallas_call(
        body, grid=(T // tm, NCHIP),
        in_specs=[pl.BlockSpec((tm, D), lambda i, s: (i, 0)), _full((1, D)), wspec(D, FS), wspec(D, FS), wspec(FS, D)],
        out_specs=pl.BlockSpec((tm, D), lambda i, s: (i, 0)), out_shape=_sds((T, D)),
        scratch_shapes=[pltpu.VMEM((tm, D), BF16)], name=name, compiler_params=_cp("parallel", "arbitrary"),
    )(x, nw, wg, wu, wd)


def _ffn_bwd(name, x, nw, wg, wu, wd, idx, dy, tm=256):
    ni = T // tm

    def body(x_ref, dy_ref, nw_ref, wg_ref, wu_ref, wd_ref, dx_ref, dnw_ref, dwg_ref, dwu_ref, dwd_ref,
             dh_acc, ag, au, ad):
        s, i = pl.program_id(0), pl.program_id(1)
        rows = pl.ds(pl.multiple_of(i * tm, tm), tm)

        @pl.when((s == 0) & (i == 0))
        def _():
            dnw_ref[...] = jnp.zeros_like(dnw_ref)

        @pl.when(i == 0)
        def _():
            ag[...] = jnp.zeros_like(ag)
            au[...] = jnp.zeros_like(au)
            ad[...] = jnp.zeros_like(ad)

        xv, nwv, dyv = x_ref[...], nw_ref[...], dy_ref[...]
        h, vjp_rms = jax.vjp(_rms, xv, nwv)
        hb = h.astype(BF16)
        a = _dg(hb, wg_ref[...], 1, 0)
        b = _dg(hb, wu_ref[...], 1, 0)
        sa = jax.nn.sigmoid(a)
        act = a * sa
        dyb = (0.5 * dyv).astype(BF16)
        ad[...] += _dg(act * b, dyb, 0, 0)
        dact = _dg(dyb, wd_ref[...], 1, 1)
        da = (dact * b * (sa * (1.0 + a * (1.0 - sa)))).astype(BF16)
        db = (dact * act).astype(BF16)
        ag[...] += _dg(hb, da, 0, 0)
        au[...] += _dg(hb, db, 0, 0)
        dh = _dg(da, wg_ref[...], 1, 1) + _dg(db, wu_ref[...], 1, 1)

        @pl.when(s == 0)
        def _():
            dh_acc[rows, :] = dh

        @pl.when(s > 0)
        def _():
            dh_acc[rows, :] += dh

        @pl.when(s == NCHIP - 1)
        def _():
            dx, dnw = vjp_rms(dh_acc[rows, :])
            dx_ref[...] = dyv + dx
            dnw_ref[...] += dnw

        @pl.when(i == ni - 1)
        def _():
            dwg_ref[...] = ag[...].astype(BF16)
            dwu_ref[...] = au[...].astype(BF16)
            dwd_ref[...] = ad[...].astype(BF16)

    wspec = lambda r, c: pl.BlockSpec((None, None, r, c), lambda s, i: (s, idx, 0, 0))
    gspec = lambda r, c: pl.BlockSpec((None, r, c), lambda s, i: (s, 0, 0))
    last = lambda s, i: (jnp.where(s == NCHIP - 1, i, 0), 0)
    return pl.pallas_call(
        body, grid=(NCHIP, ni),
        in_specs=[pl.BlockSpec((tm, D), lambda s, i: (i, 0)), pl.BlockSpec((tm, D), lambda s, i: (i, 0)),
                  _full((1, D)), wspec(D, FS), wspec(D, FS), wspec(FS, D)],
        out_specs=[pl.BlockSpec((tm, D), last), _full((1, D)), gspec(D, FS), gspec(D, FS), gspec(FS, D)],
        out_shape=[_sds((T, D)), _sds((1, D)), _sds((NCHIP, D, FS), BF16), _sds((NCHIP, D, FS), BF16),
                   _sds((NCHIP, FS, D), BF16)],
        scratch_shapes=[pltpu.VMEM((T, D), F32), pltpu.VMEM((D, FS), F32), pltpu.VMEM((D, FS), F32),
                        pltpu.VMEM((FS, D), F32)],
        name=name, compiler_params=_cp("arbitrary", "arbitrary"),
    )(x, dy, nw, wg, wu, wd)


CONV_ROWS = 256


def _conv_pad(k):
    return 8 * ((k - 1 + 7) // 8)


def _conv_fwd(name, x, w, b, act):
    k_w, c = w.shape
    tc = 256 if c % 256 == 0 else LANES
    pad = _conv_pad(k_w)
    has_b = b is not None

    def body(*refs):
        x_ref, w_ref = refs[0], refs[1]
        b_ref = refs[2] if has_b else None
        y_ref, xp = refs[2 + has_b], refs[3 + has_b]
        xp[0:pad, :] = jnp.zeros((pad, tc), F32)
        xp[pad:, :] = x_ref[...]

        def step(t, carry):
            base = pl.multiple_of(t * CONV_ROWS, CONV_ROWS)
            win = xp[pl.ds(base, CONV_ROWS + pad), :]
            acc = jnp.zeros((CONV_ROWS, tc), F32)
            for k in range(k_w):
                o = pad - (k_w - 1) + k
                acc = acc + w_ref[k:k + 1, :] * win[o:o + CONV_ROWS, :]
            if has_b:
                acc = acc + b_ref[...]
            y_ref[pl.ds(base, CONV_ROWS), :] = _silu(acc) if act else acc
            return carry

        lax.fori_loop(0, T // CONV_ROWS, step, 0)

    col = lambda r: pl.BlockSpec((r, tc), lambda j: (0, j))
    ins = [x, w] + ([b] if has_b else [])
    return pl.pallas_call(
        body, grid=(c // tc,), in_specs=[col(T), col(k_w)] + ([col(1)] if has_b else []), out_specs=col(T),
        out_shape=_sds((T, c)), scratch_shapes=[pltpu.VMEM((T + pad, tc), F32)], name=name,
        compiler_params=_cp("parallel"),
    )(*ins)


def _conv_bwd(name, x, w, b, act, dy):
    k_w, c = w.shape
    tc = 256 if c % 256 == 0 else LANES
    pad = _conv_pad(k_w)
    has_b = b is not None

    def body(*refs):
        x_ref, w_ref, dy_ref = refs[0], refs[1], refs[2]
        b_ref = refs[3] if has_b else None
        dx_ref, dw_ref, db_ref, xp, dp = refs[3 + has_b:]
        xp[0:pad, :] = jnp.zeros((pad, tc), F32)
        xp[pad:, :] = x_ref[...]
        dp[T:, :] = jnp.zeros((pad, tc), F32)
        dw_ref[...] = jnp.zeros_like(dw_ref)
        db_ref[...] = jnp.zeros_like(db_ref)

        def step1(t, carry):
            base = pl.multiple_of(t * CONV_ROWS, CONV_ROWS)
            d = dy_ref[pl.ds(base, CONV_ROWS), :]
            win = xp[pl.ds(base, CONV_ROWS + pad), :]
            offs = [pad - (k_w - 1) + k for k in range(k_w)]
            if act:
                acc = jnp.zeros((CONV_ROWS, tc), F32)
                for k, o in enumerate(offs):
                    acc = acc + w_ref[k:k + 1, :] * win[o:o + CONV_ROWS, :]
                if has_b:
                    acc = acc + b_ref[...]
                sg = jax.nn.sigmoid(acc)
                d = d * (sg * (1.0 + acc * (1.0 - sg)))
            dp[pl.ds(base, CONV_ROWS), :] = d
            for k, o in enumerate(offs):
                dw_ref[k:k + 1, :] += jnp.sum(d * win[o:o + CONV_ROWS, :], axis=0, keepdims=True)
            db_ref[...] += jnp.sum(d, axis=0, keepdims=True)
            return carry

        lax.fori_loop(0, T // CONV_ROWS, step1, 0)

        def step2(t, carry):
            base = pl.multiple_of(t * CONV_ROWS, CONV_ROWS)
            win = dp[pl.ds(base, CONV_ROWS + pad), :]
            acc = jnp.zeros((CONV_ROWS, tc), F32)
            for k in range(k_w):
                o = (k_w - 1) - k
                acc = acc + w_ref[k:k + 1, :] * win[o:o + CONV_ROWS, :]
            dx_ref[pl.ds(base, CONV_ROWS), :] = acc
            return carry

        lax.fori_loop(0, T // CONV_ROWS, step2, 0)

    col = lambda r: pl.BlockSpec((r, tc), lambda j: (0, j))
    ins = [x, w, dy] + ([b] if has_b else [])
    return pl.pallas_call(
        body, grid=(c // tc,), in_specs=[col(T), col(k_w), col(T)] + ([col(1)] if has_b else []),
        out_specs=[col(T), col(k_w), col(1)], out_shape=[_sds((T, c)), _sds((k_w, c)), _sds((1, c))],
        scratch_shapes=[pltpu.VMEM((T + pad, tc), F32), pltpu.VMEM((T + pad, tc), F32)], name=name,
        compiler_params=_cp("parallel"),
    )(*ins)


def _attn_consts(n):
    i = _iota2((BLOCK, 2 * BLOCK), 0)
    j = _iota2((BLOCK, 2 * BLOCK), 1)
    dist = i + BLOCK - j
    valid = (dist >= 0) & (dist < WINDOW) & ((n > 0) | (j >= BLOCK))
    return dist.astype(F32), valid


def _attn_block(q4, kk, vv, sinks, dist, valid, kv):
    outs = []
    lane = _iota2((1, HEADS), 1)
    for g in range(GROUP):
        h = kv * GROUP + g
        slope = 2.0 ** (-8.0 * (h + 1) / HEADS)
        s = _nt(q4[:, g * HDIM:(g + 1) * HDIM], kk) * (HDIM ** -0.5)
        s = jnp.where(valid, s - slope * dist, -1e30)
        sink = jnp.sum(jnp.where(lane == h, sinks, 0.0), axis=1, keepdims=True)
        m = jnp.maximum(jnp.max(s, axis=-1, keepdims=True), sink)
        e = jnp.exp(s - m)
        p = e / (jnp.sum(e, axis=-1, keepdims=True) + jnp.exp(sink - m))
        outs.append(_nn(p, vv))
    return tuple(outs)


def _attn_fwd(name, qa, ka, va, sinks):
    def body(q_ref, k_ref, v_ref, s_ref, o_ref, kp, vp):
        kp[0:BLOCK, :] = jnp.zeros((BLOCK, KV_A), F32)
        vp[0:BLOCK, :] = jnp.zeros((BLOCK, KV_A), F32)
        kp[BLOCK:, :] = k_ref[...]
        vp[BLOCK:, :] = v_ref[...]
        sinks_v = s_ref[...]

        def step(n, carry):
            r = pl.multiple_of(n * BLOCK, BLOCK)
            dist, valid = _attn_consts(n)
            k2 = kp[pl.ds(r, 2 * BLOCK), :]
            v2 = vp[pl.ds(r, 2 * BLOCK), :]
            for kv in range(KV_HEADS):
                q4 = q_ref[pl.ds(r, BLOCK), kv * GROUP * HDIM:(kv + 1) * GROUP * HDIM]
                og = _attn_block(q4, k2[:, kv * HDIM:(kv + 1) * HDIM], v2[:, kv * HDIM:(kv + 1) * HDIM], sinks_v,
                                 dist, valid, kv)
                for g in range(GROUP):
                    h = kv * GROUP + g
                    o_ref[pl.ds(r, BLOCK), h * HDIM:(h + 1) * HDIM] = og[g]
            return carry

        lax.fori_loop(0, T // BLOCK, step, 0)

    return pl.pallas_call(
        body, out_shape=_sds((T, Q_A)),
        scratch_shapes=[pltpu.VMEM((T + BLOCK, KV_A), F32), pltpu.VMEM((T + BLOCK, KV_A), F32)], name=name,
        compiler_params=pltpu.CompilerParams(vmem_limit_bytes=VMEM_LIMIT),
    )(qa, ka, va, sinks)


def _attn_bwd(name, qa, ka, va, sinks, do):
    def body(q_ref, k_ref, v_ref, s_ref, do_ref, dq_ref, dk_ref, dv_ref, ds_ref, kp, vp, dkp, dvp):
        kp[0:BLOCK, :] = jnp.zeros((BLOCK, KV_A), F32)
        vp[0:BLOCK, :] = jnp.zeros((BLOCK, KV_A), F32)
        kp[BLOCK:, :] = k_ref[...]
        vp[BLOCK:, :] = v_ref[...]
        dkp[...] = jnp.zeros_like(dkp)
        dvp[...] = jnp.zeros_like(dvp)
        ds_ref[...] = jnp.zeros_like(ds_ref)
        sinks_v = s_ref[...]

        def step(n, carry):
            r = pl.multiple_of(n * BLOCK, BLOCK)
            dist, valid = _attn_consts(n)
            k2 = kp[pl.ds(r, 2 * BLOCK), :]
            v2 = vp[pl.ds(r, 2 * BLOCK), :]
            for kv in range(KV_HEADS):
                cols = slice(kv * HDIM, (kv + 1) * HDIM)
                q4 = q_ref[pl.ds(r, BLOCK), kv * GROUP * HDIM:(kv + 1) * GROUP * HDIM]
                _, vjp = jax.vjp(lambda q, k, v, s: _attn_block(q, k, v, s, dist, valid, kv),
                                 q4, k2[:, cols], v2[:, cols], sinks_v)
                cts = tuple(do_ref[pl.ds(r, BLOCK), (kv * GROUP + g) * HDIM:(kv * GROUP + g + 1) * HDIM]
                            for g in range(GROUP))
                dq4, dkk, dvv, dsk = vjp(cts)
                dq_ref[pl.ds(r, BLOCK), kv * GROUP * HDIM:(kv + 1) * GROUP * HDIM] = dq4
                dkp[pl.ds(r, 2 * BLOCK), cols] += dkk
                dvp[pl.ds(r, 2 * BLOCK), cols] += dvv
                ds_ref[...] += dsk
            return carry

        lax.fori_loop(0, T // BLOCK, step, 0)
        dk_ref[...] = dkp[BLOCK:, :]
        dv_ref[...] = dvp[BLOCK:, :]

    pad = lambda: pltpu.VMEM((T + BLOCK, KV_A), F32)
    return pl.pallas_call(
        body, out_shape=[_sds((T, Q_A)), _sds((T, KV_A)), _sds((T, KV_A)), _sds((1, HEADS))],
        scratch_shapes=[pad(), pad(), pad(), pad()], name=name,
        compiler_params=pltpu.CompilerParams(vmem_limit_bytes=VMEM_LIMIT),
    )(qa, ka, va, sinks, do)


def _dn_consts():
    i = _iota2((CHUNK, CHUNK), 0)
    j = _iota2((CHUNK, CHUNK), 1)
    return dict(causal=i >= j, strict=i > j, eye=(i == j).astype(F32), ltri=(i >= j).astype(F32),
                ones=jnp.ones((CHUNK, CHUNK), F32), last=(_iota2((CHUNK, 1), 0) == CHUNK - 1).astype(F32))


def _l2norm(x):
    return x * lax.rsqrt(jnp.sum(x * x, axis=-1, keepdims=True) + EPS)


def _dn_local(qh, kh, vh, braw, araw, alog, dtb, cs):
    q = _l2norm(qh) * (HDIM ** -0.5)
    k = _l2norm(kh)
    beta = jax.nn.sigmoid(braw)
    g = -jnp.exp(alog) * jax.nn.softplus(araw + dtb)
    gc = _nn_hi(cs["ltri"], g)
    a = jnp.broadcast_to(gc, (CHUNK, CHUNK))
    diff = a - _nn_hi(cs["ones"], cs["eye"] * a)
    decay = jnp.where(cs["causal"], jnp.exp(jnp.where(cs["causal"], diff, 0.0)), 0.0)
    kb = k * beta
    low = jnp.where(cs["strict"], _nt(kb, k) * decay, 0.0)
    egc = jnp.exp(gc)
    inv = cs["eye"] - low
    pw = low
    for _ in range(5):
        pw = _nn_hi(pw, pw)
        inv = inv + _nn_hi(inv, pw)
    u = _nn_hi(inv, vh * beta)
    w = _nn_hi(inv, kb * egc)
    attn = _nt(q, k) * decay
    gc_last = jnp.sum(gc * cs["last"], axis=0, keepdims=True)
    return u, w, attn, q * egc, k * jnp.exp(gc_last - gc), egc


def _dn_local_fwd(name, qkv, ba, alog, dtb):
    def body(qkv_ref, ba_ref, al_ref, dt_ref, u_ref, w_ref, at_ref, qd_ref, kd_ref, eg_ref):
        cs = _dn_consts()
        bav, alv, dtv = ba_ref[...], al_ref[...], dt_ref[...]
        for h in range(HEADS):
            c = slice(h * HDIM, (h + 1) * HDIM)
            outs = _dn_local(qkv_ref[:, h * HDIM:(h + 1) * HDIM], qkv_ref[:, 512 + h * HDIM:512 + (h + 1) * HDIM],
                             qkv_ref[:, 1024 + h * HDIM:1024 + (h + 1) * HDIM], bav[:, h:h + 1],
                             bav[:, HEADS + h:HEADS + h + 1], alv[:, h:h + 1], dtv[:, h:h + 1], cs)
            for r, o in zip((u_ref, w_ref, at_ref, qd_ref, kd_ref), outs[:5]):
                r[:, c] = o
            eg_ref[:, h:h + 1] = outs[5]

    row = lambda w_: pl.BlockSpec((CHUNK, w_), lambda n: (n, 0))
    return pl.pallas_call(
        body, grid=(NCHUNK,), in_specs=[row(QKV_B), row(2 * HEADS), _full((1, HEADS)), _full((1, HEADS))],
        out_specs=[row(V_B)] * 5 + [row(HEADS)], out_shape=[_sds((T, V_B))] * 5 + [_sds((T, HEADS))], name=name,
        compiler_params=_cp("parallel"),
    )(qkv, ba, alog, dtb)


def _dn_local_bwd(name, qkv, ba, alog, dtb, cts):
    def body(qkv_ref, ba_ref, al_ref, dt_ref, du_ref, dw_ref, dat_ref, dqd_ref, dkd_ref, deg_ref,
             dqkv_ref, dba_ref, dal_ref, ddt_ref):
        @pl.when(pl.program_id(0) == 0)
        def _():
            dal_ref[...] = jnp.zeros_like(dal_ref)
            ddt_ref[...] = jnp.zeros_like(ddt_ref)

        cs = _dn_consts()
        bav, alv, dtv, degv = ba_ref[...], al_ref[...], dt_ref[...], deg_ref[...]
        lane16 = _iota2((1, 2 * HEADS), 1)
        lane8 = _iota2((1, HEADS), 1)
        dba = jnp.zeros((CHUNK, 2 * HEADS), F32)
        dal = jnp.zeros((1, HEADS), F32)
        ddt = jnp.zeros((1, HEADS), F32)
        for h in range(HEADS):
            c = slice(h * HDIM, (h + 1) * HDIM)
            _, vjp = jax.vjp(lambda *a: _dn_local(*a, cs), qkv_ref[:, h * HDIM:(h + 1) * HDIM],
                             qkv_ref[:, 512 + h * HDIM:512 + (h + 1) * HDIM],
                             qkv_ref[:, 1024 + h * HDIM:1024 + (h + 1) * HDIM], bav[:, h:h + 1],
                             bav[:, HEADS + h:HEADS + h + 1], alv[:, h:h + 1], dtv[:, h:h + 1])
            dq, dk, dv, dbr, dar, da1, dd1 = vjp((du_ref[:, c], dw_ref[:, c], dat_ref[:, c], dqd_ref[:, c],
                                                  dkd_ref[:, c], degv[:, h:h + 1]))
            dqkv_ref[:, h * HDIM:(h + 1) * HDIM] = dq
            dqkv_ref[:, 512 + h * HDIM:512 + (h + 1) * HDIM] = dk
            dqkv_ref[:, 1024 + h * HDIM:1024 + (h + 1) * HDIM] = dv
            dba = dba + jnp.where(lane16 == h, dbr, 0.0) + jnp.where(lane16 == HEADS + h, dar, 0.0)
            dal = dal + jnp.where(lane8 == h, da1, 0.0)
            ddt = ddt + jnp.where(lane8 == h, dd1, 0.0)
        dba_ref[...] = dba
        dal_ref[...] += dal
        ddt_ref[...] += ddt

    row = lambda w_: pl.BlockSpec((CHUNK, w_), lambda n: (n, 0))
    return pl.pallas_call(
        body, grid=(NCHUNK,),
        in_specs=[row(QKV_B), row(2 * HEADS), _full((1, HEADS)), _full((1, HEADS))] + [row(V_B)] * 5 + [row(HEADS)],
        out_specs=[row(QKV_B), row(2 * HEADS), _full((1, HEADS)), _full((1, HEADS))],
        out_shape=[_sds((T, QKV_B)), _sds((T, 2 * HEADS)), _sds((1, HEADS)), _sds((1, HEADS))], name=name,
        compiler_params=_cp("arbitrary"),
    )(qkv, ba, alog, dtb, *cts)


def _dn_step(s, u, w, attn, qd, kd, egc, z, nw, sel):
    gl = jnp.sum(jnp.sum(egc * sel, axis=1, keepdims=True), axis=0, keepdims=True)
    v_new = u - _nn(w, s)
    o = _nn(qd, s) + _nn(attn, v_new)
    s_new = s * gl + _tn(kd, v_new)
    return s_new, _rms(o, nw) * _silu(z)


def _dn_sel(h):
    return ((_iota2((CHUNK, HEADS), 0) == CHUNK - 1) & (_iota2((CHUNK, HEADS), 1) == h)).astype(F32)


def _dn_rec_fwd(name, u, w, attn, qd, kd, egc, z, nw):
    def body(u_ref, w_ref, at_ref, qd_ref, kd_ref, eg_ref, z_ref, nw_ref, o_ref, ss_ref, s_scr):
        @pl.when(pl.program_id(0) == 0)
        def _():
            s_scr[...] = jnp.zeros_like(s_scr)

        egv, nwv = eg_ref[...], nw_ref[...]
        for h in range(HEADS):
            c = slice(h * HDIM, (h + 1) * HDIM)
            s = s_scr[h]
            ss_ref[h] = s
            s_new, on = _dn_step(s, u_ref[:, c], w_ref[:, c], at_ref[:, c], qd_ref[:, c], kd_ref[:, c], egv,
                                 z_ref[:, c], nwv, _dn_sel(h))
            s_scr[h] = s_new
            o_ref[:, c] = on

    row = lambda w_: pl.BlockSpec((CHUNK, w_), lambda n: (n, 0))
    return pl.pallas_call(
        body, grid=(NCHUNK,), in_specs=[row(V_B)] * 5 + [row(HEADS), row(V_B), _full((1, HDIM))],
        out_specs=[row(V_B), pl.BlockSpec((None, HEADS, HDIM, HDIM), lambda n: (n, 0, 0, 0))],
        out_shape=[_sds((T, V_B)), _sds((NCHUNK, HEADS, HDIM, HDIM))],
        scratch_shapes=[pltpu.VMEM((HEADS, HDIM, HDIM), F32)], name=name, compiler_params=_cp("arbitrary"),
    )(u, w, attn, qd, kd, egc, z, nw)


def _dn_rec_bwd(name, u, w, attn, qd, kd, egc, z, nw, ss, do):
    def body(u_ref, w_ref, at_ref, qd_ref, kd_ref, eg_ref, z_ref, nw_ref, ss_ref, do_ref,
             du_ref, dw_ref, dat_ref, dqd_ref, dkd_ref, deg_ref, dz_ref, dnw_ref, ds_scr):
        @pl.when(pl.program_id(0) == 0)
        def _():
            ds_scr[...] = jnp.zeros_like(ds_scr)
            dnw_ref[...] = jnp.zeros_like(dnw_ref)

        egv, nwv = eg_ref[...], nw_ref[...]
        deg = jnp.zeros((CHUNK, HEADS), F32)
        dnw = jnp.zeros((1, HDIM), F32)
        for h in range(HEADS):
            c = slice(h * HDIM, (h + 1) * HDIM)
            sel = _dn_sel(h)
            _, vjp = jax.vjp(lambda *a: _dn_step(*a, sel), ss_ref[h], u_ref[:, c], w_ref[:, c], at_ref[:, c],
                             qd_ref[:, c], kd_ref[:, c], egv, z_ref[:, c], nwv)
            ds, du, dw, dat, dqd, dkd, de, dz, dn = vjp((ds_scr[h], do_ref[:, c]))
            ds_scr[h] = ds
            for r, v in zip((du_ref, dw_ref, dat_ref, dqd_ref, dkd_ref, dz_ref), (du, dw, dat, dqd, dkd, dz)):
                r[:, c] = v
            deg = deg + de
            dnw = dnw + dn
        deg_ref[...] = deg
        dnw_ref[...] += dnw

    row = lambda w_: pl.BlockSpec((CHUNK, w_), lambda n: (NCHUNK - 1 - n, 0))
    return pl.pallas_call(
        body, grid=(NCHUNK,),
        in_specs=[row(V_B)] * 5 + [row(HEADS), row(V_B), _full((1, HDIM)),
                                   pl.BlockSpec((None, HEADS, HDIM, HDIM), lambda n: (NCHUNK - 1 - n, 0, 0, 0)),
                                   row(V_B)],
        out_specs=[row(V_B)] * 5 + [row(HEADS), row(V_B), _full((1, HDIM))],
        out_shape=[_sds((T, V_B))] * 5 + [_sds((T, HEADS)), _sds((T, V_B)), _sds((1, HDIM))],
        scratch_shapes=[pltpu.VMEM((HEADS, HDIM, HDIM), F32)], name=name, compiler_params=_cp("arbitrary"),
    )(u, w, attn, qd, kd, egc, z, nw, ss, do)


def _final(name, x, fw, target, tm=512):
    def body(x_ref, fw_ref, t_ref, l_ref, dx_ref, dfw_ref):
        @pl.when(pl.program_id(0) == 0)
        def _():
            l_ref[...] = jnp.zeros_like(l_ref)
            dfw_ref[...] = jnp.zeros_like(dfw_ref)

        tv = t_ref[...]

        def f(xv, fwv):
            err = _rms(xv, fwv) - tv
            per_tok = jnp.mean(err * err, axis=-1, keepdims=True)
            return 0.5 * jnp.sum(per_tok, axis=0, keepdims=True)

        loss, vjp = jax.vjp(f, x_ref[...], fw_ref[...])
        dx, dfw = vjp(jnp.ones((1, 1), F32))
        l_ref[...] += loss
        dx_ref[...] = dx
        dfw_ref[...] += dfw

    tok = pl.BlockSpec((tm, D), lambda i: (i, 0))
    return pl.pallas_call(
        body, grid=(T // tm,), in_specs=[tok, _full((1, D)), tok], out_specs=[_full((1, 1)), tok, _full((1, D))],
        out_shape=[_sds((1, 1)), _sds((T, D)), _sds((1, D))], name=name, compiler_params=_cp("arbitrary"),
    )(x, fw, target)


def _m1_pre(tv, sv):
    return [_rms(tv[0], sv[0])]


def _m1_post(ys, tv, sv):
    return (ys[0],)


def _m1_post_split(ys, tv, sv):
    return tuple(ys[0][:, a:b] for a, b in zip(IN_SPLITS[:-1], IN_SPLITS[1:]))


def _m5_pre(tv, sv):
    return [tv[1], tv[2]]


def _m5_post(ys, tv, sv):
    return (tv[0] + ys[0] + ys[1],)


def _c1_pre(tv, sv):
    return [_rms(tv[0], sv[0])]


def _c1_post(ys, tv, sv):
    return ((ys[0] + sv[1]) * jax.nn.sigmoid(ys[1] + sv[2]),)


def _c3_pre(tv, sv):
    return [_silu(_layernorm(tv[0], sv[0], sv[1]))]


def _c3_post(ys, tv, sv):
    return (tv[1] + ys[0] + sv[2],)


def _row(v):
    return v.reshape(1, -1)


def _mixer_fwd(tag, x, p):
    parts = _blk_fwd(f"m1_fwd_{tag}", _m1_pre, [0], _m1_post_split, [x], [p["nw"]], [p["w_in"]],
                     [(b - a, F32) for a, b in zip(IN_SPLITS[:-1], IN_SPLITS[1:])])
    qa, ka, va, qkvb, z, ba = parts
    att = _attn_fwd(f"attn_fwd_{tag}", qa, ka, va, p["sinks"])
    qkvc = _conv_fwd(f"dnconv_fwd_{tag}", qkvb, p["dn_conv_w"], None, True)
    loc = _dn_local_fwd(f"dnloc_fwd_{tag}", qkvc, ba, p["a_log"], p["dt_bias"])
    og, ss = _dn_rec_fwd(f"dnrec_fwd_{tag}", *loc, z, p["dn_norm_w"])
    (out,) = _blk_fwd(f"m5_fwd_{tag}", _m5_pre, [0, 1], _m5_post, [x, att, og], [], [p["wo_a"], p["wo_b"]],
                      [(D, F32)])
    return out, dict(x=x, qa=qa, ka=ka, va=va, qkvb=qkvb, z=z, ba=ba, att=att, qkvc=qkvc, loc=loc, og=og, ss=ss)


def _mixer_bwd(tag, dy, p, s):
    (dxa, datt, dog), _, (dwo_a, dwo_b) = _blk_bwd(f"m5_bwd_{tag}", _m5_pre, [0, 1], _m5_post,
                                                   [s["x"], s["att"], s["og"]], [], [p["wo_a"], p["wo_b"]], [[dy]])
    rec = _dn_rec_bwd(f"dnrec_bwd_{tag}", *s["loc"], s["z"], p["dn_norm_w"], s["ss"], dog)
    dz, dnw_dn = rec[6], rec[7]
    dqkvc, dba, dalog, ddtb = _dn_local_bwd(f"dnloc_bwd_{tag}", s["qkvc"], s["ba"], p["a_log"], p["dt_bias"],
                                            rec[:6])
    dqkvb, dconvw, _ = _conv_bwd(f"dnconv_bwd_{tag}", s["qkvb"], p["dn_conv_w"], None, True, dqkvc)
    dqa, dka, dva, dsinks = _attn_bwd(f"attn_bwd_{tag}", s["qa"], s["ka"], s["va"], p["sinks"], datt)
    (dx,), (dnw,), (dw_in,) = _blk_bwd(f"m1_bwd_{tag}", _m1_pre, [0], _m1_post, [s["x"]], [p["nw"]], [p["w_in"]],
                                       [[dqa, dka, dva, dqkvb, dz, dba]], res=dxa)
    return dx, dict(nw=dnw, w_in=dw_in, wo_a=dwo_a, wo_b=dwo_b, dn_conv_w=dconvw, sinks=dsinks, a_log=dalog,
                    dt_bias=ddtb, dn_norm_w=dnw_dn)


def _conformer_fwd(tag, x, p):
    (glu,) = _blk_fwd(f"c1_fwd_{tag}", _c1_pre, [0, 0], _c1_post, [x], [p["nw"], p["b1a"], p["b1b"]],
                      [p["w1a"], p["w1b"]], [(D, F32)])
    cc = _conv_fwd(f"dwconv_fwd_{tag}", glu, p["w_dw"], p["b_dw"], False)
    (out,) = _blk_fwd(f"c3_fwd_{tag}", _c3_pre, [0], _c3_post, [cc, x], [p["ln_w"], p["ln_b"], p["b2"]], [p["w2"]],
                      [(D, F32)])
    return out, dict(x=x, glu=glu, cc=cc)


def _conformer_bwd(tag, dy, p, s):
    (dcc, dxa), (dlnw, dlnb, db2), (dw2,) = _blk_bwd(f"c3_bwd_{tag}", _c3_pre, [0], _c3_post, [s["cc"], s["x"]],
                                                     [p["ln_w"], p["ln_b"], p["b2"]], [p["w2"]], [[dy]])
    dglu, dwdw, dbdw = _conv_bwd(f"dwconv_bwd_{tag}", s["glu"], p["w_dw"], p["b_dw"], False, dcc)
    (dx,), (dnw, db1a, db1b), (dw1a, dw1b) = _blk_bwd(f"c1_bwd_{tag}", _c1_pre, [0, 0], _c1_post, [s["x"]],
                                                      [p["nw"], p["b1a"], p["b1b"]], [p["w1a"], p["w1b"]], [[dglu]],
                                                      res=dxa)
    return dx, dict(nw=dnw, b1a=db1a, b1b=db1b, w1a=dw1a, w1b=dw1b, w_dw=dwdw, b_dw=dbdw, ln_w=dlnw, ln_b=dlnb,
                    b2=db2, w2=dw2)


def _local_step(x, target, norm_w, gate, up, down, mixers, confs, final_w):
    saved = []
    for l in range(DEPTH):
        x0 = x
        x1 = _ffn_fwd(f"ffn_fwd_{l}a", x0, _row(norm_w[l, 0]), gate, up, down, 2 * l)
        p = dict(mixers[l // 2] if l % 2 == 0 else confs[l // 2], nw=_row(norm_w[l, 1]))
        x2, sv = (_mixer_fwd if l % 2 == 0 else _conformer_fwd)(str(l), x1, p)
        x = _ffn_fwd(f"ffn_fwd_{l}b", x2, _row(norm_w[l, 2]), gate, up, down, 2 * l + 1)
        saved.append((x0, x2, p, sv))
    loss, dx, dfw = _final("final", x, _row(final_w), target)
    dnorm = [[None] * 3 for _ in range(DEPTH)]
    dffn = [None] * (2 * DEPTH)
    dmix = [None] * (DEPTH // 2)
    dconf = [None] * (DEPTH // 2)
    for l in reversed(range(DEPTH)):
        x0, x2, p, sv = saved[l]
        dx, dnorm[l][2], *dffn[2 * l + 1] = _ffn_bwd(f"ffn_bwd_{l}b", x2, _row(norm_w[l, 2]), gate, up, down,
                                                     2 * l + 1, dx)
        if l % 2 == 0:
            dx, dmix[l // 2] = _mixer_bwd(str(l), dx, p, sv)
            dnorm[l][1] = dmix[l // 2].pop("nw")
        else:
            dx, dconf[l // 2] = _conformer_bwd(str(l), dx, p, sv)
            dnorm[l][1] = dconf[l // 2].pop("nw")
        dx, dnorm[l][0], *dffn[2 * l] = _ffn_bwd(f"ffn_bwd_{l}a", x0, _row(norm_w[l, 0]), gate, up, down, 2 * l, dx)
    dnorm_w = jnp.stack([jnp.concatenate(r, axis=0) for r in dnorm])
    return loss, dx, dict(norm_w=dnorm_w, ffn=dffn, mixers=dmix, confs=dconf, final_w=dfw)


def _place():
    x, y, c = lax.axis_index("x"), lax.axis_index("y"), lax.axis_index("c")
    chips = [(1 - x, y), (x, 1 - y), (1 - x, 1 - y)]
    return x, y, c, 2 * x + y, chips, [2 * px + py for px, py in chips]


def _all_gather(bigs, small):
    nb = len(bigs)
    n_in = nb + 1

    def body(*refs):
        ins, outs = refs[:n_in], refs[n_in:2 * n_in]
        send, recv, fsend, frecv, lsem = refs[2 * n_in:]
        x, y, c, me, chips, cidx = _place()
        sib = (x, y, 1 - c)
        local = [pltpu.make_async_copy(ins[a], outs[a].at[me], lsem.at[a]) for a in range(n_in)]
        for cp in local:
            cp.start()

        def half(a, who):
            h = bigs[a].shape[0] // 2
            return pl.ds(who * h, h)

        def ici(a, j):
            k = a * 3 + j
            if a == nb:
                return pltpu.make_async_remote_copy(ins[a], outs[a].at[me], send.at[k], recv.at[k],
                                                    device_id=(*chips[j], c), device_id_type=MESH)
            return pltpu.make_async_remote_copy(ins[a].at[half(a, c)], outs[a].at[me, half(a, c)], send.at[k],
                                                recv.at[k], device_id=(*chips[j], c), device_id_type=MESH)

        def landed(a, j):
            k = a * 3 + j
            dst = outs[a].at[cidx[j]] if a == nb else outs[a].at[cidx[j], half(a, c)]
            return pltpu.make_async_remote_copy(dst, dst, send.at[k], recv.at[k], device_id=(*chips[j], c),
                                                device_id_type=MESH)

        def passed(a, j, who):
            k = a * 3 + j
            part = outs[a].at[cidx[j], half(a, who)]
            return pltpu.make_async_remote_copy(part, part, fsend.at[k], frecv.at[k], device_id=sib,
                                                device_id_type=MESH)

        sends = [ici(a, j) for a in range(n_in) for j in range(3)]
        for cp in sends:
            cp.start()
        fwd = []
        for a in range(nb):
            for j in range(3):
                landed(a, j).wait_recv()
                cp = passed(a, j, c)
                cp.start()
                fwd.append(cp)
        for j in range(3):
            landed(nb, j).wait_recv()
        for a in range(nb):
            for j in range(3):
                passed(a, j, 1 - c).wait_recv()
        for cp in sends + fwd:
            cp.wait_send()
        for cp in local:
            cp.wait()

    arrs = list(bigs) + [small]
    return pl.pallas_call(
        body, in_specs=[ANY] * n_in, out_specs=[ANY] * n_in,
        out_shape=[_sds((NCHIP,) + a.shape, a.dtype) for a in arrs],
        scratch_shapes=[pltpu.SemaphoreType.DMA((3 * n_in,)), pltpu.SemaphoreType.DMA((3 * n_in,)),
                        pltpu.SemaphoreType.DMA((3 * nb,)), pltpu.SemaphoreType.DMA((3 * nb,)),
                        pltpu.SemaphoreType.DMA((n_in,))],
        name="all_gather_weights",
    )(*arrs)


def _swap_halves(grads):
    n = len(grads)

    def body(*refs):
        ins, outs, send, recv = refs[:n], refs[n:2 * n], refs[2 * n], refs[2 * n + 1]
        x, y, c, _, _, _ = _place()
        cps = []
        for a in range(n):
            h = grads[a].shape[1] // 2
            cps.append(pltpu.make_async_remote_copy(ins[a].at[:, pl.ds((1 - c) * h, h)], outs[a], send.at[a],
                                                    recv.at[a], device_id=(x, y, 1 - c), device_id_type=MESH))
        for cp in cps:
            cp.start()
        for cp in cps:
            cp.wait()

    return pl.pallas_call(
        body, in_specs=[ANY] * n, out_specs=[ANY] * n,
        out_shape=[_sds((NCHIP, g.shape[1] // 2) + g.shape[2:], g.dtype) for g in grads],
        scratch_shapes=[pltpu.SemaphoreType.DMA((n,)), pltpu.SemaphoreType.DMA((n,))], name="swap_grad_halves",
    )(*grads)


def _row_tile(r, cap=256):
    return max(t for t in range(8, cap + 1, 8) if r % t == 0)


def _add_half(name, g, r, c_arr):
    _, l, rows, cols = g.shape
    h = l // 2
    tr = _row_tile(rows)

    def body(c_ref, g_ref, r_ref, o_ref):
        o_ref[...] = (g_ref[...].astype(F32) + r_ref[...].astype(F32)).astype(BF16)

    blk = (None, None, tr, cols)
    return pl.pallas_call(
        body,
        grid_spec=pltpu.PrefetchScalarGridSpec(
            num_scalar_prefetch=1, grid=(NCHIP, h, rows // tr),
            in_specs=[pl.BlockSpec(blk, lambda j, i, t, c_ref: (j, c_ref[0] * h + i, t, 0)),
                      pl.BlockSpec(blk, lambda j, i, t, c_ref: (j, i, t, 0))],
            out_specs=pl.BlockSpec(blk, lambda j, i, t, c_ref: (j, i, t, 0))),
        out_shape=_sds((NCHIP, h, rows, cols), BF16), name=name,
        compiler_params=_cp("parallel", "parallel", "parallel"),
    )(c_arr, g, r)


def _reduce_scatter(parts, small, rep):
    nb = len(parts)

    def body(*refs):
        ins, small_in, rep_in = refs[:nb], refs[nb], refs[nb + 1]
        outs, small_out, rep_out = refs[nb + 2:2 * nb + 2], refs[2 * nb + 2], refs[2 * nb + 3]
        send, recv, fsend, frecv, lsem, ssend, srecv = refs[2 * nb + 4:]
        x, y, c, me, chips, cidx = _place()
        sib = (x, y, 1 - c)
        dev = 4 * x + 2 * y + c
        local = [pltpu.make_async_copy(ins[a].at[me], outs[a].at[c, me], lsem.at[a]) for a in range(nb)]
        local.append(pltpu.make_async_copy(small_in.at[me], small_out.at[dev], lsem.at[nb]))
        local.append(pltpu.make_async_copy(rep_in, rep_out.at[dev], lsem.at[nb + 1]))
        for cp in local:
            cp.start()

        def ici(a, j):
            return pltpu.make_async_remote_copy(ins[a].at[cidx[j]], outs[a].at[c, me], send.at[a * 3 + j],
                                                recv.at[a * 3 + j], device_id=(*chips[j], c), device_id_type=MESH)

        def landed(a, j):
            dst = outs[a].at[c, cidx[j]]
            return pltpu.make_async_remote_copy(dst, dst, send.at[a * 3 + j], recv.at[a * 3 + j],
                                                device_id=(*chips[j], c), device_id_type=MESH)

        def passed(a, j, who):
            chip = me if j == 3 else cidx[j]
            dst = outs[a].at[who, chip]
            src = ins[a].at[me] if j == 3 else dst
            return pltpu.make_async_remote_copy(src, dst, fsend.at[a * 4 + j], frecv.at[a * 4 + j], device_id=sib,
                                                device_id_type=MESH)

        def peer(r):
            return (1 - x if r & 4 else x), (1 - y if r & 2 else y), (1 - c if r & 1 else c)

        def tiny(r, which):
            px, py, pc = peer(r)
            k = (r - 1) * 2 + which
            if which == 0:
                return pltpu.make_async_remote_copy(small_in.at[2 * px + py], small_out.at[dev], ssend.at[k],
                                                    srecv.at[k], device_id=(px, py, pc), device_id_type=MESH)
            return pltpu.make_async_remote_copy(rep_in, rep_out.at[dev], ssend.at[k], srecv.at[k],
                                                device_id=(px, py, pc), device_id_type=MESH)

        def tiny_landed(r, which):
            px, py, pc = peer(r)
            k = (r - 1) * 2 + which
            dst = (small_out if which == 0 else rep_out).at[4 * px + 2 * py + pc]
            return pltpu.make_async_remote_copy(dst, dst, ssend.at[k], srecv.at[k], device_id=(px, py, pc),
                                                device_id_type=MESH)

        sends = [ici(a, j) for a in range(nb) for j in range(3)]
        sends += [passed(a, 3, c) for a in range(nb)]
        sends += [tiny(r, w) for r in range(1, NDEV) for w in range(2)]
        for cp in sends:
            cp.start()
        for a in range(nb):
            for j in range(3):
                landed(a, j).wait_recv()
                cp = passed(a, j, c)
                cp.start()
                sends.append(cp)
        for a in range(nb):
            for j in range(4):
                passed(a, j, 1 - c).wait_recv()
        for r in range(1, NDEV):
            for w in range(2):
                tiny_landed(r, w).wait_recv()
        for cp in sends:
            cp.wait_send()
        for cp in local:
            cp.wait()

    n_in = nb + 2
    out_shape = [_sds((2,) + p.shape, p.dtype) for p in parts]
    out_shape += [_sds((NDEV,) + small.shape[1:], F32), _sds((NDEV,) + rep.shape, F32)]
    return pl.pallas_call(
        body, in_specs=[ANY] * n_in, out_specs=[ANY] * n_in, out_shape=out_shape,
        scratch_shapes=[pltpu.SemaphoreType.DMA((3 * nb,)), pltpu.SemaphoreType.DMA((3 * nb,)),
                        pltpu.SemaphoreType.DMA((4 * nb,)), pltpu.SemaphoreType.DMA((4 * nb,)),
                        pltpu.SemaphoreType.DMA((n_in,)), pltpu.SemaphoreType.DMA((2 * (NDEV - 1),)),
                        pltpu.SemaphoreType.DMA((2 * (NDEV - 1),))],
        name="reduce_scatter_grads",
    )(*parts, small, rep)


def _adamw_math(w, g, m, v):
    m = B1 * m + (1.0 - B1) * g
    v = B2 * v + (1.0 - B2) * (g * g)
    m_hat = m / (1.0 - B1 ** STEP)
    v_hat = v / (1.0 - B2 ** STEP)
    return -LR * (m_hat / (jnp.sqrt(v_hat) + AEPS) + WD * w), m, v


def _adamw_big(name, w, m, v, parts):
    l, rows, cols = w.shape
    h = l // 2
    tr = _row_tile(rows)

    def body(w_ref, m_ref, v_ref, p_ref, g_ref, d_ref, nm_ref, nv_ref):
        g = p_ref[0].astype(F32)
        for q in range(1, NCHIP):
            g = g + p_ref[q].astype(F32)
        d, nm, nv = _adamw_math(w_ref[...], g, m_ref[...], v_ref[...])
        g_ref[...], d_ref[...], nm_ref[...], nv_ref[...] = g, d, nm, nv

    spec = pl.BlockSpec((None, tr, cols), lambda p, i, t: (p * h + i, t, 0))
    return pl.pallas_call(
        body, grid=(2, h, rows // tr),
        in_specs=[spec, spec, spec, pl.BlockSpec((None, NCHIP, None, tr, cols), lambda p, i, t: (p, 0, i, t, 0))],
        out_specs=[spec] * 4, out_shape=[_sds(w.shape)] * 4, name=name,
        compiler_params=_cp("parallel", "parallel", "parallel"),
    )(w, m, v, parts)


def _adamw_small(name, w, m, v, parts):
    def body(w_ref, m_ref, v_ref, p_ref, g_ref, d_ref, nm_ref, nv_ref):
        g = p_ref[0]
        for q in range(1, NDEV):
            g = g + p_ref[q]
        d, nm, nv = _adamw_math(w_ref[...], g, m_ref[...], v_ref[...])
        g_ref[...], d_ref[...], nm_ref[...], nv_ref[...] = g, d, nm, nv

    return pl.pallas_call(body, out_shape=[_sds(w.shape)] * 4, name=name)(w, m, v, parts)


def _pack(arrs, rows):
    flat = jnp.concatenate([a.reshape(-1) for a in arrs])
    return jnp.pad(flat, (0, rows * LANES - flat.shape[0])).reshape(rows, LANES)


def _unpack(packed, shapes):
    flat, out, o = packed.reshape(-1), [], 0
    for s in shapes:
        n = 1
        for d in s:
            n *= d
        out.append(flat[o:o + n].reshape(s))
        o += n
    return out


SMALL_ROWS, REP_ROWS = 200, 16


def kernel(x, norm_w, ffn_w_gate, ffn_w_up, ffn_w_down, mix_w_in, dn_conv_w, attn_sinks, dn_a_log, dn_dt_bias, dn_norm_w, mix_w_out, conv_w_pw1, conv_b_pw1, conv_w_dw, conv_b_dw, conv_ln_w, conv_ln_b, conv_w_pw2, conv_b_pw2, final_norm_w, loss_target, m_norm_w, m_ffn_w_gate, m_ffn_w_up, m_ffn_w_down, m_mix_w_in, m_dn_conv_w, m_attn_sinks, m_dn_a_log, m_dn_dt_bias, m_dn_norm_w, m_mix_w_out, m_conv_w_pw1, m_conv_b_pw1, m_conv_w_dw, m_conv_b_dw, m_conv_ln_w, m_conv_ln_b, m_conv_w_pw2, m_conv_b_pw2, m_final_norm_w, v_norm_w, v_ffn_w_gate, v_ffn_w_up, v_ffn_w_down, v_mix_w_in, v_dn_conv_w, v_attn_sinks, v_dn_a_log, v_dn_dt_bias, v_dn_norm_w, v_mix_w_out, v_conv_w_pw1, v_conv_b_pw1, v_conv_w_dw, v_conv_b_dw, v_conv_ln_w, v_conv_ln_b, v_conv_w_pw2, v_conv_b_pw2, v_final_norm_w):
    big_names = ["ffn_w_gate", "ffn_w_up", "ffn_w_down", "mix_w_in", "mix_w_out", "conv_w_pw1", "conv_w_pw2"]
    small_names = ["norm_w", "dn_conv_w", "conv_b_pw1", "conv_w_dw", "conv_b_dw", "conv_ln_w", "conv_ln_b",
                   "conv_b_pw2"]
    rep_names = ["attn_sinks", "dn_a_log", "dn_dt_bias", "dn_norm_w", "final_norm_w"]
    w = dict(norm_w=norm_w, ffn_w_gate=ffn_w_gate, ffn_w_up=ffn_w_up, ffn_w_down=ffn_w_down, mix_w_in=mix_w_in, dn_conv_w=dn_conv_w, attn_sinks=attn_sinks, dn_a_log=dn_a_log, dn_dt_bias=dn_dt_bias, dn_norm_w=dn_norm_w, mix_w_out=mix_w_out, conv_w_pw1=conv_w_pw1, conv_b_pw1=conv_b_pw1, conv_w_dw=conv_w_dw, conv_b_dw=conv_b_dw, conv_ln_w=conv_ln_w, conv_ln_b=conv_ln_b, conv_w_pw2=conv_w_pw2, conv_b_pw2=conv_b_pw2, final_norm_w=final_norm_w)
    m = dict(norm_w=m_norm_w, ffn_w_gate=m_ffn_w_gate, ffn_w_up=m_ffn_w_up, ffn_w_down=m_ffn_w_down, mix_w_in=m_mix_w_in, dn_conv_w=m_dn_conv_w, attn_sinks=m_attn_sinks, dn_a_log=m_dn_a_log, dn_dt_bias=m_dn_dt_bias, dn_norm_w=m_dn_norm_w, mix_w_out=m_mix_w_out, conv_w_pw1=m_conv_w_pw1, conv_b_pw1=m_conv_b_pw1, conv_w_dw=m_conv_w_dw, conv_b_dw=m_conv_b_dw, conv_ln_w=m_conv_ln_w, conv_ln_b=m_conv_ln_b, conv_w_pw2=m_conv_w_pw2, conv_b_pw2=m_conv_b_pw2, final_norm_w=m_final_norm_w)
    v = dict(norm_w=v_norm_w, ffn_w_gate=v_ffn_w_gate, ffn_w_up=v_ffn_w_up, ffn_w_down=v_ffn_w_down, mix_w_in=v_mix_w_in, dn_conv_w=v_dn_conv_w, attn_sinks=v_attn_sinks, dn_a_log=v_dn_a_log, dn_dt_bias=v_dn_dt_bias, dn_norm_w=v_dn_norm_w, mix_w_out=v_mix_w_out, conv_w_pw1=v_conv_w_pw1, conv_b_pw1=v_conv_b_pw1, conv_w_dw=v_conv_w_dw, conv_b_dw=v_conv_b_dw, conv_ln_w=v_conv_ln_w, conv_ln_b=v_conv_ln_b, conv_w_pw2=v_conv_w_pw2, conv_b_pw2=v_conv_b_pw2, final_norm_w=v_final_norm_w)
    order = ["norm_w", "ffn_w_gate", "ffn_w_up", "ffn_w_down", "mix_w_in", "dn_conv_w", "attn_sinks", "dn_a_log",
             "dn_dt_bias", "dn_norm_w", "mix_w_out", "conv_w_pw1", "conv_b_pw1", "conv_w_dw", "conv_b_dw",
             "conv_ln_w", "conv_ln_b", "conv_w_pw2", "conv_b_pw2", "final_norm_w"]

    def view3(a):
        return a.reshape((-1,) + a.shape[-2:])

    small_shapes = [w[n].shape for n in small_names]
    rep_shapes = [w[n].shape for n in rep_names]

    gathered = _all_gather([view3(w[n]).astype(BF16) for n in big_names],
                           _pack([w[n] for n in small_names], SMALL_ROWS))
    gate, up, down, win_all, wout_all, pw1_all, pw2_all, small_all = gathered
    per_chip = [_unpack(small_all[q], small_shapes) for q in range(NCHIP)]
    sm = {n: jnp.concatenate([per_chip[q][i] for q in range(NCHIP)], axis=-1) for i, n in enumerate(small_names)}
    w_in = win_all.transpose(1, 2, 0, 3).reshape(2, D, IN_COLS)
    w_out = wout_all.transpose(1, 0, 2, 3).reshape(2, D, D)
    pw1 = pw1_all.transpose(1, 2, 0, 3).reshape(2, D, 2 * D)
    pw2 = pw2_all.transpose(1, 0, 2, 3).reshape(2, D, D)
    mixers = [dict(w_in=w_in[e], dn_conv_w=sm["dn_conv_w"][e], sinks=_row(attn_sinks[e]), a_log=_row(dn_a_log[e]),
                   dt_bias=_row(dn_dt_bias[e]), dn_norm_w=_row(dn_norm_w[e]), wo_a=w_out[e, :Q_A],
                   wo_b=w_out[e, Q_A:]) for e in range(2)]
    confs = [dict(b1a=_row(sm["conv_b_pw1"][e, :D]), b1b=_row(sm["conv_b_pw1"][e, D:]), w1a=pw1[e, :, :D],
                  w1b=pw1[e, :, D:], w_dw=sm["conv_w_dw"][e], b_dw=_row(sm["conv_b_dw"][e]),
                  ln_w=_row(sm["conv_ln_w"][e]), ln_b=_row(sm["conv_ln_b"][e]), b2=_row(sm["conv_b_pw2"][e]),
                  w2=pw2[e]) for e in range(2)]

    loss, dx, g = _local_step(x[0], loss_target[0], sm["norm_w"], gate, up, down, mixers, confs, final_norm_w)

    def chip_cols(a):
        return a.reshape(2, a.shape[1], NCHIP, -1).transpose(2, 0, 1, 3)

    def chip_rows(a):
        return a.reshape(2, NCHIP, -1, a.shape[2]).transpose(1, 0, 2, 3)

    gm, gc = g["mixers"], g["confs"]
    big_g = [jnp.stack([g["ffn"][i][k] for i in range(2 * DEPTH)], axis=1) for k in range(3)]
    big_g.append(chip_cols(jnp.stack([gm[e]["w_in"] for e in range(2)])).astype(BF16))
    big_g.append(chip_rows(jnp.stack([jnp.concatenate([gm[e]["wo_a"], gm[e]["wo_b"]], axis=0)
                                      for e in range(2)])).astype(BF16))
    big_g.append(chip_cols(jnp.stack([jnp.concatenate([gc[e]["w1a"], gc[e]["w1b"]], axis=1)
                                      for e in range(2)])).astype(BF16))
    big_g.append(chip_rows(jnp.stack([gc[e]["w2"] for e in range(2)])).astype(BF16))
    small_g = dict(
        norm_w=g["norm_w"], dn_conv_w=jnp.stack([gm[e]["dn_conv_w"] for e in range(2)]),
        conv_b_pw1=jnp.stack([jnp.concatenate([gc[e]["b1a"], gc[e]["b1b"]], axis=1)[0] for e in range(2)]),
        conv_w_dw=jnp.stack([gc[e]["w_dw"] for e in range(2)]),
        conv_b_dw=jnp.stack([gc[e]["b_dw"][0] for e in range(2)]),
        conv_ln_w=jnp.stack([gc[e]["ln_w"][0] for e in range(2)]),
        conv_ln_b=jnp.stack([gc[e]["ln_b"][0] for e in range(2)]),
        conv_b_pw2=jnp.stack([gc[e]["b2"][0] for e in range(2)]))
    small_by_chip = jnp.stack([_pack([jnp.split(small_g[n], NCHIP, axis=-1)[q] for n in small_names], SMALL_ROWS)
                               for q in range(NCHIP)])
    rep_g = _pack([jnp.stack([gm[e]["sinks"][0] for e in range(2)]), jnp.stack([gm[e]["a_log"][0] for e in range(2)]),
                   jnp.stack([gm[e]["dt_bias"][0] for e in range(2)]),
                   jnp.stack([gm[e]["dn_norm_w"][0] for e in range(2)]), g["final_w"][0]], REP_ROWS)

    from_sibling = _swap_halves(big_g)
    c_arr = lax.axis_index("c").astype(jnp.int32).reshape(1)
    partial = [_add_half(f"add_half_{n}", gg, rr, c_arr) for n, gg, rr in zip(big_names, big_g, from_sibling)]
    reduced = _reduce_scatter(partial, small_by_chip, rep_g)

    res = {}
    for n, parts in zip(big_names, reduced[:len(big_names)]):
        outs = _adamw_big(f"adamw_{n}", view3(w[n]), view3(m[n]), view3(v[n]), parts)
        res[n] = [o.reshape(w[n].shape) for o in outs]
    outs = _adamw_small("adamw_small", *[_pack([d[n] for n in small_names], SMALL_ROWS) for d in (w, m, v)],
                        reduced[-2])
    for i, n in enumerate(small_names):
        res[n] = [_unpack(o, small_shapes)[i] for o in outs]
    outs = _adamw_small("adamw_replicated", *[_pack([d[n] for n in rep_names], REP_ROWS) for d in (w, m, v)],
                        reduced[-1])
    for i, n in enumerate(rep_names):
        res[n] = [_unpack(o, rep_shapes)[i] for o in outs]

    total = lax.psum(loss[0, 0], ("x", "y", "c"))
    return (total, dx[None], *[res[n][0] for n in order], *[res[n][1] for n in order],
            *[res[n][2] for n in order], *[res[n][3] for n in order])
```

```python
import jax
import jax.numpy as jnp
from jax import lax
from jax.experimental import pallas as pl
from jax.experimental.pallas import tpu as pltpu

F32, BF16 = jnp.float32, jnp.bfloat16
MESH = pl.DeviceIdType.MESH
ANY = pl.BlockSpec(memory_space=pl.ANY)

T, D, F = 2048, 1024, 2816
DEPTH = 4
EPS = 1e-6
HEADS, HDIM, KV_HEADS, GROUP = 8, 64, 2, 4
WINDOW = BLOCK = 128
CHUNK = 64
NCHUNK = T // CHUNK
DN_CONV, CONV_WIDTH = 4, 31
Q_A, KV_A, QKV_B, V_B = 512, 128, 1536, 512
IN_COLS = 2832
IN_SPLITS = (0, 512, 640, 768, 2304, 2816, 2832)
NCHIP, NDEV = 4, 8
FS = F // NCHIP
LR, B1, B2, AEPS, WD, STEP = 0.001, 0.9, 0.999, 1e-08, 0.01, 10
V7X_VMEM_BYTES = 64 * 1024 * 1024
VMEM_LIMIT = V7X_VMEM_BYTES * 7 // 8
LANES = 128


def _cp(*sem):
    return pltpu.CompilerParams(dimension_semantics=sem, vmem_limit_bytes=VMEM_LIMIT)


def _sds(shape, dtype=F32):
    return jax.ShapeDtypeStruct(tuple(shape), dtype)


def _full(shape):
    nd = len(shape)
    return pl.BlockSpec(tuple(shape), lambda *_: (0,) * nd)


def _split_bf16(a):
    hi = a.astype(BF16)
    return hi, (a - hi.astype(F32)).astype(BF16)


def _dg(a, b, ca, cb, hi=False):
    if a.ndim == 3 and b.ndim == 3:
        dims = (((ca + 1,), (cb + 1,)), ((0,), (0,)))
    else:
        dims = (((ca,), (cb,)), ((), ()))
    dot = lambda p, q: lax.dot_general(p, q, dims, preferred_element_type=F32)
    if hi:
        a_hi, a_lo = _split_bf16(a.astype(F32))
        b_hi, b_lo = _split_bf16(b.astype(F32))
        return dot(a_hi, b_hi) + (dot(a_hi, b_lo) + dot(a_lo, b_hi))
    return dot(a.astype(BF16), b.astype(BF16))


def _make_mm(hi):
    @jax.custom_vjp
    def nn(a, b):
        return _dg(a, b, 1, 0, hi)

    @jax.custom_vjp
    def nt(a, b):
        return _dg(a, b, 1, 1, hi)

    @jax.custom_vjp
    def tn(a, b):
        return _dg(a, b, 0, 0, hi)

    nn.defvjp(lambda a, b: (_dg(a, b, 1, 0, hi), (a, b)),
              lambda r, g: (_dg(g, r[1], 1, 1, hi).astype(r[0].dtype), _dg(r[0], g, 0, 0, hi).astype(r[1].dtype)))
    nt.defvjp(lambda a, b: (_dg(a, b, 1, 1, hi), (a, b)),
              lambda r, g: (_dg(g, r[1], 1, 0, hi).astype(r[0].dtype), _dg(g, r[0], 0, 0, hi).astype(r[1].dtype)))
    tn.defvjp(lambda a, b: (_dg(a, b, 0, 0, hi), (a, b)),
              lambda r, g: (_dg(r[1], g, 1, 1, hi).astype(r[0].dtype), _dg(r[0], g, 1, 0, hi).astype(r[1].dtype)))
    return nn, nt, tn


_nn, _nt, _tn = _make_mm(False)
_nn_hi, _nt_hi, _tn_hi = _make_mm(True)


def _rms(x, w):
    return x * lax.rsqrt(jnp.mean(x * x, axis=-1, keepdims=True) + EPS) * w


def _layernorm(x, w, b):
    xc = x - jnp.mean(x, axis=-1, keepdims=True)
    return xc * lax.rsqrt(jnp.mean(xc * xc, axis=-1, keepdims=True) + EPS) * w + b


def _silu(x):
    return x * jax.nn.sigmoid(x)


def _iota2(shape, dim):
    return lax.broadcasted_iota(jnp.int32, shape, dim)


def _blk_fwd(name, pre, lhs_idx, post, toks, smalls, weights, outs, tm=512):
    nt_, ns, nw = len(toks), len(smalls), len(weights)

    def body(*refs):
        tv = [r[...] for r in refs[:nt_]]
        sv = [r[...] for r in refs[nt_:nt_ + ns]]
        wr = refs[nt_ + ns:nt_ + ns + nw]
        orf = refs[nt_ + ns + nw:]
        lhs = pre(tv, sv)
        ys = [_dg(lhs[i], w[...], 1, 0) for i, w in zip(lhs_idx, wr)]
        for o_ref, o in zip(orf, post(ys, tv, sv)):
            o_ref[...] = o.astype(o_ref.dtype)

    in_specs = ([pl.BlockSpec((tm, a.shape[1]), lambda i: (i, 0)) for a in toks]
                + [_full(a.shape) for a in smalls] + [_full(w.shape) for w in weights])
    out_specs = [pl.BlockSpec((tm, w_), lambda i: (i, 0)) for w_, _ in outs]
    return pl.pallas_call(
        body, grid=(T // tm,), in_specs=in_specs, out_specs=out_specs,
        out_shape=[_sds((T, w_), dt) for w_, dt in outs], name=name, compiler_params=_cp("parallel"),
    )(*toks, *smalls, *weights)


def _blk_bwd(name, pre, lhs_idx, post, toks, smalls, weights, ct_groups, res=None, tm=256, wchunk=512):
    nt_, ns, nw = len(toks), len(smalls), len(weights)
    cts = [a for g in ct_groups for a in g]
    nc = len(cts)
    widths = [sum(a.shape[1] for a in g) for g in ct_groups]
    has_res = res is not None

    def body(*refs):
        p = 0
        tr = refs[p:p + nt_]; p += nt_
        sr = refs[p:p + ns]; p += ns
        wr = refs[p:p + nw]; p += nw
        cr = refs[p:p + nc]; p += nc
        rr = refs[p:p + has_res]; p += has_res
        dtr = refs[p:p + nt_]; p += nt_
        dsr = refs[p:p + ns]; p += ns
        dwr = refs[p:p + nw]; p += nw
        scr = refs[p:]
        i = pl.program_id(0)

        @pl.when(i == 0)
        def _():
            for r in list(dsr) + list(dwr):
                r[...] = jnp.zeros_like(r)

        tv = [r[...] for r in tr]
        sv = [r[...] for r in sr]
        ctv, q, si = [], 0, 0
        for g in ct_groups:
            if len(g) == 1:
                ctv.append(cr[q][...].astype(F32))
            else:
                off = 0
                for j, a in enumerate(g):
                    scr[si][:, off:off + a.shape[1]] = cr[q + j][...].astype(F32)
                    off += a.shape[1]
                ctv.append(scr[si][...])
                si += 1
            q += len(g)

        lhs, vjp_pre = jax.vjp(lambda *a: tuple(pre(list(a[:nt_]), list(a[nt_:]))), *tv, *sv)
        lhs_b = [l.astype(BF16) for l in lhs]
        ys = [_dg(lhs_b[k], w[...], 1, 0) for k, w in zip(lhs_idx, wr)]
        _, vjp_post = jax.vjp(lambda *a: tuple(post(list(a[:nw]), list(a[nw:nw + nt_]), list(a[nw + nt_:]))),
                              *ys, *tv, *sv)
        gp = vjp_post(tuple(ctv))
        dys, dt_post, ds_post = gp[:nw], gp[nw:nw + nt_], gp[nw + nt_:]
        dlhs = [None] * len(lhs)
        for k, w, dy, dw in zip(lhs_idx, wr, dys, dwr):
            dyb = dy.astype(BF16)
            n = w.shape[1]
            for c0 in range(0, n, wchunk):
                c1 = min(n, c0 + wchunk)
                dw[:, c0:c1] += _dg(lhs_b[k], dyb[:, c0:c1], 0, 0)
            d = _dg(dyb, w[...], 1, 1)
            dlhs[k] = d if dlhs[k] is None else dlhs[k] + d
        gq = vjp_pre(tuple(d.astype(l.dtype) for d, l in zip(dlhs, lhs)))
        dt_pre, ds_pre = gq[:nt_], gq[nt_:]
        for j in range(nt_):
            d = dt_post[j] + dt_pre[j]
            if j == 0 and has_res:
                d = d + rr[0][...]
            dtr[j][...] = d
        for j in range(ns):
            dsr[j][...] += ds_post[j] + ds_pre[j]

    tok_spec = lambda a: pl.BlockSpec((tm, a.shape[1]), lambda i: (i, 0))
    in_specs = ([tok_spec(a) for a in toks] + [_full(a.shape) for a in smalls] + [_full(w.shape) for w in weights]
                + [tok_spec(a) for a in cts] + ([tok_spec(res)] if has_res else []))
    out_specs = [tok_spec(a) for a in toks] + [_full(a.shape) for a in smalls] + [_full(w.shape) for w in weights]
    out_shape = ([_sds(a.shape) for a in toks] + [_sds(a.shape) for a in smalls] + [_sds(w.shape) for w in weights])
    scratch = [pltpu.VMEM((tm, wd), F32) for g, wd in zip(ct_groups, widths) if len(g) > 1]
    outs = pl.pallas_call(
        body, grid=(T // tm,), in_specs=in_specs, out_specs=out_specs, out_shape=out_shape,
        scratch_shapes=scratch, name=name, compiler_params=_cp("arbitrary"),
    )(*toks, *smalls, *weights, *cts, *([res] if has_res else []))
    return outs[:nt_], outs[nt_:nt_ + ns], outs[nt_ + ns:]


def _ffn_fwd(name, x, nw, wg, wu, wd, idx, tm=512):
    def body(x_ref, nw_ref, wg_ref, wu_ref, wd_ref, o_ref, h_scr):
        s = pl.program_id(1)

        @pl.when(s == 0)
        def _():
            xv = x_ref[...]
            h_scr[...] = _rms(xv, nw_ref[...]).astype(BF16)
            o_ref[...] = xv

        h = h_scr[...]
        a = _dg(h, wg_ref[...], 1, 0)
        b = _dg(h, wu_ref[...], 1, 0)
        o_ref[...] += 0.5 * _dg(_silu(a) * b, wd_ref[...], 1, 0)

    wspec = lambda r, c: pl.BlockSpec((None, None, r, c), lambda i, s: (s, idx, 0, 0))
    return pl.pallas_call(
        body, grid=(T // tm, NCHIP),
        in_specs=[pl.BlockSpec((tm, D), lambda i, s: (i, 0)), _full((1, D)), wspec(D, FS), wspec(D, FS), wspec(FS, D)],
        out_specs=pl.BlockSpec((tm, D), lambda i, s: (i, 0)), out_shape=_sds((T, D)),
        scratch_shapes=[pltpu.VMEM((tm, D), BF16)], name=name, compiler_params=_cp("parallel", "arbitrary"),
    )(x, nw, wg, wu, wd)


def _ffn_bwd(name, x, nw, wg, wu, wd, idx, dy, gbufs=None, tm=512):
    ni = T // tm

    def body(x_ref, dy_ref, nw_ref, wg_ref, wu_ref, wd_ref, dx_ref, dnw_ref, dwg_ref, dwu_ref, dwd_ref,
             dh_acc, ag, au, ad):
        s, i = pl.program_id(0), pl.program_id(1)
        rows = pl.ds(pl.multiple_of(i * tm, tm), tm)

        @pl.when((s == 0) & (i == 0))
        def _():
            dnw_ref[...] = jnp.zeros_like(dnw_ref)

        @pl.when(i == 0)
        def _():
            ag[...] = jnp.zeros_like(ag)
            au[...] = jnp.zeros_like(au)
            ad[...] = jnp.zeros_like(ad)

        xv, nwv, dyv = x_ref[...], nw_ref[...], dy_ref[...]
        h, vjp_rms = jax.vjp(_rms, xv, nwv)
        hb = h.astype(BF16)
        a = _dg(hb, wg_ref[...], 1, 0)
        b = _dg(hb, wu_ref[...], 1, 0)
        sa = jax.nn.sigmoid(a)
        act = a * sa
        dyb = (0.5 * dyv).astype(BF16)
        ad[...] += _dg(act * b, dyb, 0, 0)
        dact = _dg(dyb, wd_ref[...], 1, 1)
        da = (dact * b * (sa * (1.0 + a * (1.0 - sa)))).astype(BF16)
        db = (dact * act).astype(BF16)
        ag[...] += _dg(hb, da, 0, 0)
        au[...] += _dg(hb, db, 0, 0)
        dh = _dg(da, wg_ref[...], 1, 1) + _dg(db, wu_ref[...], 1, 1)

        @pl.when(s == 0)
        def _():
            dh_acc[rows, :] = dh

        @pl.when(s > 0)
        def _():
            dh_acc[rows, :] += dh

        @pl.when(s == NCHIP - 1)
        def _():
            dx, dnw = vjp_rms(dh_acc[rows, :])
            dx_ref[...] = dyv + dx
            dnw_ref[...] += dnw

        @pl.when(i == ni - 1)
        def _():
            dwg_ref[...] = ag[...].astype(BF16)
            dwu_ref[...] = au[...].astype(BF16)
            dwd_ref[...] = ad[...].astype(BF16)

    wspec = lambda r, c: pl.BlockSpec((None, None, r, c), lambda s, i: (s, idx, 0, 0), pipeline_mode=pl.Buffered(1))
    last = lambda s, i: (jnp.where(s == NCHIP - 1, i, 0), 0)
    nb = 0 if gbufs is None else 3
    return pl.pallas_call(
        lambda *refs: body(*refs[:6], *refs[6 + nb:]), grid=(NCHIP, ni),
        in_specs=[pl.BlockSpec((tm, D), lambda s, i: (i, 0)), pl.BlockSpec((tm, D), lambda s, i: (i, 0)),
                  _full((1, D)), wspec(D, FS), wspec(D, FS), wspec(FS, D)] + [ANY] * nb,
        out_specs=[pl.BlockSpec((tm, D), last), _full((1, D)), wspec(D, FS), wspec(D, FS), wspec(FS, D)],
        out_shape=[_sds((T, D)), _sds((1, D)), _sds(wg.shape, BF16), _sds(wu.shape, BF16), _sds(wd.shape, BF16)],
        input_output_aliases={6 + k: 2 + k for k in range(nb)},
        scratch_shapes=[pltpu.VMEM((T, D), F32), pltpu.VMEM((D, FS), F32), pltpu.VMEM((D, FS), F32),
                        pltpu.VMEM((FS, D), F32)],
        name=name, compiler_params=_cp("arbitrary", "arbitrary"),
    )(x, dy, nw, wg, wu, wd, *(gbufs or ()))


CONV_ROWS = 256


def _conv_pad(k):
    return 8 * ((k - 1 + 7) // 8)


def _conv_fwd(name, x, w, b, act):
    k_w, c = w.shape
    tc = 256 if c % 256 == 0 else LANES
    pad = _conv_pad(k_w)
    has_b = b is not None

    def body(*refs):
        x_ref, w_ref = refs[0], refs[1]
        b_ref = refs[2] if has_b else None
        y_ref, xp = refs[2 + has_b], refs[3 + has_b]
        xp[0:pad, :] = jnp.zeros((pad, tc), F32)
        xp[pad:, :] = x_ref[...]

        def step(t, carry):
            base = pl.multiple_of(t * CONV_ROWS, CONV_ROWS)
            win = xp[pl.ds(base, CONV_ROWS + pad), :]
            acc = jnp.zeros((CONV_ROWS, tc), F32)
            for k in range(k_w):
                o = pad - (k_w - 1) + k
                acc = acc + w_ref[k:k + 1, :] * win[o:o + CONV_ROWS, :]
            if has_b:
                acc = acc + b_ref[...]
            y_ref[pl.ds(base, CONV_ROWS), :] = _silu(acc) if act else acc
            return carry

        lax.fori_loop(0, T // CONV_ROWS, step, 0)

    col = lambda r: pl.BlockSpec((r, tc), lambda j: (0, j))
    ins = [x, w] + ([b] if has_b else [])
    return pl.pallas_call(
        body, grid=(c // tc,), in_specs=[col(T), col(k_w)] + ([col(1)] if has_b else []), out_specs=col(T),
        out_shape=_sds((T, c)), scratch_shapes=[pltpu.VMEM((T + pad, tc), F32)], name=name,
        compiler_params=_cp("parallel"),
    )(*ins)


def _conv_bwd(name, x, w, b, act, dy):
    k_w, c = w.shape
    tc = 256 if c % 256 == 0 else LANES
    pad = _conv_pad(k_w)
    has_b = b is not None

    def body(*refs):
        x_ref, w_ref, dy_ref = refs[0], refs[1], refs[2]
        b_ref = refs[3] if has_b else None
        dx_ref, dw_ref, db_ref, xp, dp = refs[3 + has_b:]
        xp[0:pad, :] = jnp.zeros((pad, tc), F32)
        xp[pad:, :] = x_ref[...]
        dp[T:, :] = jnp.zeros((pad, tc), F32)
        dw_ref[...] = jnp.zeros_like(dw_ref)
        db_ref[...] = jnp.zeros_like(db_ref)

        def step1(t, carry):
            base = pl.multiple_of(t * CONV_ROWS, CONV_ROWS)
            d = dy_ref[pl.ds(base, CONV_ROWS), :]
            win = xp[pl.ds(base, CONV_ROWS + pad), :]
            offs = [pad - (k_w - 1) + k for k in range(k_w)]
            if act:
                acc = jnp.zeros((CONV_ROWS, tc), F32)
                for k, o in enumerate(offs):
                    acc = acc + w_ref[k:k + 1, :] * win[o:o + CONV_ROWS, :]
                if has_b:
                    acc = acc + b_ref[...]
                sg = jax.nn.sigmoid(acc)
                d = d * (sg * (1.0 + acc * (1.0 - sg)))
            dp[pl.ds(base, CONV_ROWS), :] = d
            for k, o in enumerate(offs):
                dw_ref[k:k + 1, :] += jnp.sum(d * win[o:o + CONV_ROWS, :], axis=0, keepdims=True)
            db_ref[...] += jnp.sum(d, axis=0, keepdims=True)
            return carry

        lax.fori_loop(0, T // CONV_ROWS, step1, 0)

        def step2(t, carry):
            base = pl.multiple_of(t * CONV_ROWS, CONV_ROWS)
            win = dp[pl.ds(base, CONV_ROWS + pad), :]
            acc = jnp.zeros((CONV_ROWS, tc), F32)
            for k in range(k_w):
                o = (k_w - 1) - k
                acc = acc + w_ref[k:k + 1, :] * win[o:o + CONV_ROWS, :]
            dx_ref[pl.ds(base, CONV_ROWS), :] = acc
            return carry

        lax.fori_loop(0, T // CONV_ROWS, step2, 0)

    col = lambda r: pl.BlockSpec((r, tc), lambda j: (0, j))
    ins = [x, w, dy] + ([b] if has_b else [])
    return pl.pallas_call(
        body, grid=(c // tc,), in_specs=[col(T), col(k_w), col(T)] + ([col(1)] if has_b else []),
        out_specs=[col(T), col(k_w), col(1)], out_shape=[_sds((T, c)), _sds((k_w, c)), _sds((1, c))],
        scratch_shapes=[pltpu.VMEM((T + pad, tc), F32), pltpu.VMEM((T + pad, tc), F32)], name=name,
        compiler_params=_cp("parallel"),
    )(*ins)


def _attn_consts(n):
    i = _iota2((BLOCK, 2 * BLOCK), 0)
    j = _iota2((BLOCK, 2 * BLOCK), 1)
    dist = i + BLOCK - j
    valid = (dist >= 0) & (dist < WINDOW) & ((n > 0) | (j >= BLOCK))
    return dist.astype(F32), valid


def _attn_block(q4, kk, vv, sinks, dist, valid, kv):
    outs = []
    lane = _iota2((1, HEADS), 1)
    for g in range(GROUP):
        h = kv * GROUP + g
        slope = 2.0 ** (-8.0 * (h + 1) / HEADS)
        s = _nt(q4[:, g * HDIM:(g + 1) * HDIM], kk) * (HDIM ** -0.5)
        s = jnp.where(valid, s - slope * dist, -1e30)
        sink = jnp.sum(jnp.where(lane == h, sinks, 0.0), axis=1, keepdims=True)
        m = jnp.maximum(jnp.max(s, axis=-1, keepdims=True), sink)
        e = jnp.exp(s - m)
        p = e / (jnp.sum(e, axis=-1, keepdims=True) + jnp.exp(sink - m))
        outs.append(_nn(p, vv))
    return tuple(outs)


def _attn_fwd(name, qa, ka, va, sinks):
    def body(q_ref, k_ref, v_ref, s_ref, o_ref, kp, vp):
        kp[0:BLOCK, :] = jnp.zeros((BLOCK, KV_A), F32)
        vp[0:BLOCK, :] = jnp.zeros((BLOCK, KV_A), F32)
        kp[BLOCK:, :] = k_ref[...]
        vp[BLOCK:, :] = v_ref[...]
        sinks_v = s_ref[...]

        def step(n, carry):
            r = pl.multiple_of(n * BLOCK, BLOCK)
            dist, valid = _attn_consts(n)
            k2 = kp[pl.ds(r, 2 * BLOCK), :]
            v2 = vp[pl.ds(r, 2 * BLOCK), :]
            for kv in range(KV_HEADS):
                q4 = q_ref[pl.ds(r, BLOCK), kv * GROUP * HDIM:(kv + 1) * GROUP * HDIM]
                og = _attn_block(q4, k2[:, kv * HDIM:(kv + 1) * HDIM], v2[:, kv * HDIM:(kv + 1) * HDIM], sinks_v,
                                 dist, valid, kv)
                for g in range(GROUP):
                    h = kv * GROUP + g
                    o_ref[pl.ds(r, BLOCK), h * HDIM:(h + 1) * HDIM] = og[g]
            return carry

        lax.fori_loop(0, T // BLOCK, step, 0)

    return pl.pallas_call(
        body, out_shape=_sds((T, Q_A)),
        scratch_shapes=[pltpu.VMEM((T + BLOCK, KV_A), F32), pltpu.VMEM((T + BLOCK, KV_A), F32)], name=name,
        compiler_params=pltpu.CompilerParams(vmem_limit_bytes=VMEM_LIMIT),
    )(qa, ka, va, sinks)


def _attn_bwd(name, qa, ka, va, sinks, do):
    def body(q_ref, k_ref, v_ref, s_ref, do_ref, dq_ref, dk_ref, dv_ref, ds_ref, kp, vp, dkp, dvp):
        kp[0:BLOCK, :] = jnp.zeros((BLOCK, KV_A), F32)
        vp[0:BLOCK, :] = jnp.zeros((BLOCK, KV_A), F32)
        kp[BLOCK:, :] = k_ref[...]
        vp[BLOCK:, :] = v_ref[...]
        dkp[...] = jnp.zeros_like(dkp)
        dvp[...] = jnp.zeros_like(dvp)
        ds_ref[...] = jnp.zeros_like(ds_ref)
        sinks_v = s_ref[...]

        def step(n, carry):
            r = pl.multiple_of(n * BLOCK, BLOCK)
            dist, valid = _attn_consts(n)
            k2 = kp[pl.ds(r, 2 * BLOCK), :]
            v2 = vp[pl.ds(r, 2 * BLOCK), :]
            for kv in range(KV_HEADS):
                cols = slice(kv * HDIM, (kv + 1) * HDIM)
                q4 = q_ref[pl.ds(r, BLOCK), kv * GROUP * HDIM:(kv + 1) * GROUP * HDIM]
                _, vjp = jax.vjp(lambda q, k, v, s: _attn_block(q, k, v, s, dist, valid, kv),
                                 q4, k2[:, cols], v2[:, cols], sinks_v)
                cts = tuple(do_ref[pl.ds(r, BLOCK), (kv * GROUP + g) * HDIM:(kv * GROUP + g + 1) * HDIM]
                            for g in range(GROUP))
                dq4, dkk, dvv, dsk = vjp(cts)
                dq_ref[pl.ds(r, BLOCK), kv * GROUP * HDIM:(kv + 1) * GROUP * HDIM] = dq4
                dkp[pl.ds(r, 2 * BLOCK), cols] += dkk
                dvp[pl.ds(r, 2 * BLOCK), cols] += dvv
                ds_ref[...] += dsk
            return carry

        lax.fori_loop(0, T // BLOCK, step, 0)
        dk_ref[...] = dkp[BLOCK:, :]
        dv_ref[...] = dvp[BLOCK:, :]

    pad = lambda: pltpu.VMEM((T + BLOCK, KV_A), F32)
    return pl.pallas_call(
        body, out_shape=[_sds((T, Q_A)), _sds((T, KV_A)), _sds((T, KV_A)), _sds((1, HEADS))],
        scratch_shapes=[pad(), pad(), pad(), pad()], name=name,
        compiler_params=pltpu.CompilerParams(vmem_limit_bytes=VMEM_LIMIT),
    )(qa, ka, va, sinks, do)


def _dn_consts():
    i = _iota2((CHUNK, CHUNK), 0)
    j = _iota2((CHUNK, CHUNK), 1)
    return dict(causal=i >= j, strict=i > j, eye=(i == j).astype(F32), ltri=(i >= j).astype(F32),
                ones=jnp.ones((CHUNK, CHUNK), F32), last=(_iota2((CHUNK, 1), 0) == CHUNK - 1).astype(F32))


def _l2norm(x):
    return x * lax.rsqrt(jnp.sum(x * x, axis=-1, keepdims=True) + EPS)


def _head_cols(m):
    lane = _iota2((1, HEADS), 1)
    return jnp.concatenate([jnp.sum(jnp.where(lane == h, m, 0.0), axis=1, keepdims=True)[None]
                            for h in range(HEADS)], axis=0)


def _dn_local(q3, k3, v3, braw, araw, alog, dtb, cs):
    q = _l2norm(q3) * (HDIM ** -0.5)
    k = _l2norm(k3)
    g = -jnp.exp(alog) * jax.nn.softplus(araw + dtb)
    gc_all = _nn_hi(cs["ltri"], g)
    egc_all = jnp.exp(gc_all)
    beta, gc, egc = _head_cols(jax.nn.sigmoid(braw)), _head_cols(gc_all), _head_cols(egc_all)
    a = jnp.broadcast_to(gc, (HEADS, CHUNK, CHUNK))
    diff = a - jnp.swapaxes(a, 1, 2)
    decay = jnp.where(cs["causal"], jnp.exp(jnp.where(cs["causal"], diff, 0.0)), 0.0)
    kb = k * beta
    low = jnp.where(cs["strict"], _nt(kb, k) * decay, 0.0)
    inv = cs["eye"] - low
    pw = low
    for _ in range(5):
        pw = _nn_hi(pw, pw)
        inv = inv + _nn_hi(inv, pw)
    u = _nn_hi(inv, v3 * beta)
    w = _nn_hi(inv, kb * egc)
    attn = _nt(q, k) * decay
    gc_last = jnp.sum(gc * cs["last"], axis=1, keepdims=True)
    return u, w, attn, q * egc, k * jnp.exp(gc_last - gc), egc_all


def _heads3(ref, off=0):
    return jnp.concatenate([ref[:, off + h * HDIM:off + (h + 1) * HDIM][None] for h in range(HEADS)], axis=0)


def _dn_local_fwd(name, qkv, ba, alog, dtb):
    def body(qkv_ref, ba_ref, al_ref, dt_ref, u_ref, w_ref, at_ref, qd_ref, kd_ref, eg_ref):
        bav = ba_ref[...]
        outs = _dn_local(_heads3(qkv_ref), _heads3(qkv_ref, 512), _heads3(qkv_ref, 1024), bav[:, :HEADS],
                         bav[:, HEADS:], al_ref[...], dt_ref[...], _dn_consts())
        for r, o in zip((u_ref, w_ref, at_ref, qd_ref, kd_ref), outs[:5]):
            for h in range(HEADS):
                r[:, h * HDIM:(h + 1) * HDIM] = o[h]
        eg_ref[...] = outs[5]

    row = lambda w_: pl.BlockSpec((CHUNK, w_), lambda n: (n, 0))
    return pl.pallas_call(
        body, grid=(NCHUNK,), in_specs=[row(QKV_B), row(2 * HEADS), _full((1, HEADS)), _full((1, HEADS))],
        out_specs=[row(V_B)] * 5 + [row(HEADS)], out_shape=[_sds((T, V_B))] * 5 + [_sds((T, HEADS))], name=name,
        compiler_params=_cp("parallel"),
    )(qkv, ba, alog, dtb)


def _dn_local_bwd(name, qkv, ba, alog, dtb, cts):
    def body(qkv_ref, ba_ref, al_ref, dt_ref, du_ref, dw_ref, dat_ref, dqd_ref, dkd_ref, deg_ref,
             dqkv_ref, dba_ref, dal_ref, ddt_ref):
        @pl.when(pl.program_id(0) == 0)
        def _():
            dal_ref[...] = jnp.zeros_like(dal_ref)
            ddt_ref[...] = jnp.zeros_like(ddt_ref)

        cs = _dn_consts()
        bav = ba_ref[...]
        _, vjp = jax.vjp(lambda *a: _dn_local(*a, cs), _heads3(qkv_ref), _heads3(qkv_ref, 512),
                         _heads3(qkv_ref, 1024), bav[:, :HEADS], bav[:, HEADS:], al_ref[...], dt_ref[...])
        dq, dk, dv, dbr, dar, dal, ddt = vjp((_heads3(du_ref), _heads3(dw_ref), _heads3(dat_ref), _heads3(dqd_ref),
                                              _heads3(dkd_ref), deg_ref[...]))
        for h in range(HEADS):
            dqkv_ref[:, h * HDIM:(h + 1) * HDIM] = dq[h]
            dqkv_ref[:, 512 + h * HDIM:512 + (h + 1) * HDIM] = dk[h]
            dqkv_ref[:, 1024 + h * HDIM:1024 + (h + 1) * HDIM] = dv[h]
        dba_ref[:, :HEADS] = dbr
        dba_ref[:, HEADS:] = dar
        dal_ref[...] += dal
        ddt_ref[...] += ddt

    row = lambda w_: pl.BlockSpec((CHUNK, w_), lambda n: (n, 0))
    return pl.pallas_call(
        body, grid=(NCHUNK,),
        in_specs=[row(QKV_B), row(2 * HEADS), _full((1, HEADS)), _full((1, HEADS))] + [row(V_B)] * 5 + [row(HEADS)],
        out_specs=[row(QKV_B), row(2 * HEADS), _full((1, HEADS)), _full((1, HEADS))],
        out_shape=[_sds((T, QKV_B)), _sds((T, 2 * HEADS)), _sds((1, HEADS)), _sds((1, HEADS))], name=name,
        compiler_params=_cp("arbitrary"),
    )(qkv, ba, alog, dtb, *cts)


def _dn_step(s, u, w, attn, qd, kd, egc, z, nw):
    last = (_iota2((CHUNK, 1), 0) == CHUNK - 1).astype(F32)
    gl = jnp.sum(_head_cols(egc) * last, axis=1, keepdims=True)
    v_new = u - _nn(w, s)
    o = _nn(qd, s) + _nn(attn, v_new)
    s_new = s * gl + _tn(kd, v_new)
    return s_new, _rms(o, nw) * _silu(z)


def _unheads(ref, v3):
    for h in range(HEADS):
        ref[:, h * HDIM:(h + 1) * HDIM] = v3[h]


def _dn_rec_fwd(name, u, w, attn, qd, kd, egc, z, nw):
    def body(u_ref, w_ref, at_ref, qd_ref, kd_ref, eg_ref, z_ref, nw_ref, o_ref, ss_ref, s_scr):
        @pl.when(pl.program_id(0) == 0)
        def _():
            s_scr[...] = jnp.zeros_like(s_scr)

        s = s_scr[...]
        ss_ref[...] = s
        s_new, on = _dn_step(s, _heads3(u_ref), _heads3(w_ref), _heads3(at_ref), _heads3(qd_ref), _heads3(kd_ref),
                             eg_ref[...], _heads3(z_ref), nw_ref[...])
        s_scr[...] = s_new
        _unheads(o_ref, on)

    row = lambda w_: pl.BlockSpec((CHUNK, w_), lambda n: (n, 0))
    return pl.pallas_call(
        body, grid=(NCHUNK,), in_specs=[row(V_B)] * 5 + [row(HEADS), row(V_B), _full((1, HDIM))],
        out_specs=[row(V_B), pl.BlockSpec((None, HEADS, HDIM, HDIM), lambda n: (n, 0, 0, 0))],
        out_shape=[_sds((T, V_B)), _sds((NCHUNK, HEADS, HDIM, HDIM))],
        scratch_shapes=[pltpu.VMEM((HEADS, HDIM, HDIM), F32)], name=name, compiler_params=_cp("arbitrary"),
    )(u, w, attn, qd, kd, egc, z, nw)


def _dn_rec_bwd(name, u, w, attn, qd, kd, egc, z, nw, ss, do):
    def body(u_ref, w_ref, at_ref, qd_ref, kd_ref, eg_ref, z_ref, nw_ref, ss_ref, do_ref,
             du_ref, dw_ref, dat_ref, dqd_ref, dkd_ref, deg_ref, dz_ref, dnw_ref, ds_scr):
        @pl.when(pl.program_id(0) == 0)
        def _():
            ds_scr[...] = jnp.zeros_like(ds_scr)
            dnw_ref[...] = jnp.zeros_like(dnw_ref)

        _, vjp = jax.vjp(_dn_step, ss_ref[...], _heads3(u_ref), _heads3(w_ref), _heads3(at_ref), _heads3(qd_ref),
                         _heads3(kd_ref), eg_ref[...], _heads3(z_ref), nw_ref[...])
        ds, du, dw, dat, dqd, dkd, deg, dz, dnw = vjp((ds_scr[...], _heads3(do_ref)))
        ds_scr[...] = ds
        for r, v in zip((du_ref, dw_ref, dat_ref, dqd_ref, dkd_ref, dz_ref), (du, dw, dat, dqd, dkd, dz)):
            _unheads(r, v)
        deg_ref[...] = deg
        dnw_ref[...] += dnw

    row = lambda w_: pl.BlockSpec((CHUNK, w_), lambda n: (NCHUNK - 1 - n, 0))
    return pl.pallas_call(
        body, grid=(NCHUNK,),
        in_specs=[row(V_B)] * 5 + [row(HEADS), row(V_B), _full((1, HDIM)),
                                   pl.BlockSpec((None, HEADS, HDIM, HDIM), lambda n: (NCHUNK - 1 - n, 0, 0, 0)),
                                   row(V_B)],
        out_specs=[row(V_B)] * 5 + [row(HEADS), row(V_B), _full((1, HDIM))],
        out_shape=[_sds((T, V_B))] * 5 + [_sds((T, HEADS)), _sds((T, V_B)), _sds((1, HDIM))],
        scratch_shapes=[pltpu.VMEM((HEADS, HDIM, HDIM), F32)], name=name, compiler_params=_cp("arbitrary"),
    )(u, w, attn, qd, kd, egc, z, nw, ss, do)


def _final(name, x, fw, target, tm=512):
    def body(x_ref, fw_ref, t_ref, l_ref, dx_ref, dfw_ref):
        @pl.when(pl.program_id(0) == 0)
        def _():
            l_ref[...] = jnp.zeros_like(l_ref)
            dfw_ref[...] = jnp.zeros_like(dfw_ref)

        tv = t_ref[...]

        def f(xv, fwv):
            err = _rms(xv, fwv) - tv
            per_tok = jnp.mean(err * err, axis=-1, keepdims=True)
            return 0.5 * jnp.sum(per_tok, axis=0, keepdims=True)

        loss, vjp = jax.vjp(f, x_ref[...], fw_ref[...])
        dx, dfw = vjp(jnp.ones((1, 1), F32))
        l_ref[...] += loss
        dx_ref[...] = dx
        dfw_ref[...] += dfw

    tok = pl.BlockSpec((tm, D), lambda i: (i, 0))
    return pl.pallas_call(
        body, grid=(T // tm,), in_specs=[tok, _full((1, D)), tok], out_specs=[_full((1, 1)), tok, _full((1, D))],
        out_shape=[_sds((1, 1)), _sds((T, D)), _sds((1, D))], name=name, compiler_params=_cp("arbitrary"),
    )(x, fw, target)


def _m1_pre(tv, sv):
    return [_rms(tv[0], sv[0])]


def _m1_post(ys, tv, sv):
    return (ys[0],)


def _m1_post_split(ys, tv, sv):
    return tuple(ys[0][:, a:b] for a, b in zip(IN_SPLITS[:-1], IN_SPLITS[1:]))


def _m5_pre(tv, sv):
    return [tv[1], tv[2]]


def _m5_post(ys, tv, sv):
    return (tv[0] + ys[0] + ys[1],)


def _c1_pre(tv, sv):
    return [_rms(tv[0], sv[0])]


def _c1_post(ys, tv, sv):
    return ((ys[0] + sv[1]) * jax.nn.sigmoid(ys[1] + sv[2]),)


def _c3_pre(tv, sv):
    return [_silu(_layernorm(tv[0], sv[0], sv[1]))]


def _c3_post(ys, tv, sv):
    return (tv[1] + ys[0] + sv[2],)


def _row(v):
    return v.reshape(1, -1)


def _mixer_fwd(tag, x, p):
    parts = _blk_fwd(f"m1_fwd_{tag}", _m1_pre, [0], _m1_post_split, [x], [p["nw"]], [p["w_in"]],
                     [(b - a, F32) for a, b in zip(IN_SPLITS[:-1], IN_SPLITS[1:])])
    qa, ka, va, qkvb, z, ba = parts
    att = _attn_fwd(f"attn_fwd_{tag}", qa, ka, va, p["sinks"])
    qkvc = _conv_fwd(f"dnconv_fwd_{tag}", qkvb, p["dn_conv_w"], None, True)
    loc = _dn_local_fwd(f"dnloc_fwd_{tag}", qkvc, ba, p["a_log"], p["dt_bias"])
    og, ss = _dn_rec_fwd(f"dnrec_fwd_{tag}", *loc, z, p["dn_norm_w"])
    (out,) = _blk_fwd(f"m5_fwd_{tag}", _m5_pre, [0, 1], _m5_post, [x, att, og], [], [p["wo_a"], p["wo_b"]],
                      [(D, F32)])
    return out, dict(x=x, qa=qa, ka=ka, va=va, qkvb=qkvb, z=z, ba=ba, att=att, qkvc=qkvc, loc=loc, og=og, ss=ss)


def _mixer_bwd(tag, dy, p, s):
    (dxa, datt, dog), _, (dwo_a, dwo_b) = _blk_bwd(f"m5_bwd_{tag}", _m5_pre, [0, 1], _m5_post,
                                                   [s["x"], s["att"], s["og"]], [], [p["wo_a"], p["wo_b"]], [[dy]])
    rec = _dn_rec_bwd(f"dnrec_bwd_{tag}", *s["loc"], s["z"], p["dn_norm_w"], s["ss"], dog)
    dz, dnw_dn = rec[6], rec[7]
    dqkvc, dba, dalog, ddtb = _dn_local_bwd(f"dnloc_bwd_{tag}", s["qkvc"], s["ba"], p["a_log"], p["dt_bias"],
                                            rec[:6])
    dqkvb, dconvw, _ = _conv_bwd(f"dnconv_bwd_{tag}", s["qkvb"], p["dn_conv_w"], None, True, dqkvc)
    dqa, dka, dva, dsinks = _attn_bwd(f"attn_bwd_{tag}", s["qa"], s["ka"], s["va"], p["sinks"], datt)
    (dx,), (dnw,), (dw_in,) = _blk_bwd(f"m1_bwd_{tag}", _m1_pre, [0], _m1_post, [s["x"]], [p["nw"]], [p["w_in"]],
                                       [[dqa, dka, dva, dqkvb, dz, dba]], res=dxa)
    return dx, dict(nw=dnw, w_in=dw_in, wo_a=dwo_a, wo_b=dwo_b, dn_conv_w=dconvw, sinks=dsinks, a_log=dalog,
                    dt_bias=ddtb, dn_norm_w=dnw_dn)


def _conformer_fwd(tag, x, p):
    (glu,) = _blk_fwd(f"c1_fwd_{tag}", _c1_pre, [0, 0], _c1_post, [x], [p["nw"], p["b1a"], p["b1b"]],
                      [p["w1a"], p["w1b"]], [(D, F32)])
    cc = _conv_fwd(f"dwconv_fwd_{tag}", glu, p["w_dw"], p["b_dw"], False)
    (out,) = _blk_fwd(f"c3_fwd_{tag}", _c3_pre, [0], _c3_post, [cc, x], [p["ln_w"], p["ln_b"], p["b2"]], [p["w2"]],
                      [(D, F32)])
    return out, dict(x=x, glu=glu, cc=cc)


def _conformer_bwd(tag, dy, p, s):
    (dcc, dxa), (dlnw, dlnb, db2), (dw2,) = _blk_bwd(f"c3_bwd_{tag}", _c3_pre, [0], _c3_post, [s["cc"], s["x"]],
                                                     [p["ln_w"], p["ln_b"], p["b2"]], [p["w2"]], [[dy]])
    dglu, dwdw, dbdw = _conv_bwd(f"dwconv_bwd_{tag}", s["glu"], p["w_dw"], p["b_dw"], False, dcc)
    (dx,), (dnw, db1a, db1b), (dw1a, dw1b) = _blk_bwd(f"c1_bwd_{tag}", _c1_pre, [0, 0], _c1_post, [s["x"]],
                                                      [p["nw"], p["b1a"], p["b1b"]], [p["w1a"], p["w1b"]], [[dglu]],
                                                      res=dxa)
    return dx, dict(nw=dnw, b1a=db1a, b1b=db1b, w1a=dw1a, w1b=dw1b, w_dw=dwdw, b_dw=dbdw, ln_w=dlnw, ln_b=dlnb,
                    b2=db2, w2=dw2)


def _local_step(x, target, norm_w, gate, up, down, mixers, confs, final_w):
    saved = []
    for l in range(DEPTH):
        x0 = x
        x1 = _ffn_fwd(f"ffn_fwd_{l}a", x0, _row(norm_w[l, 0]), gate, up, down, 2 * l)
        p = dict(mixers[l // 2] if l % 2 == 0 else confs[l // 2], nw=_row(norm_w[l, 1]))
        x2, sv = (_mixer_fwd if l % 2 == 0 else _conformer_fwd)(str(l), x1, p)
        x = _ffn_fwd(f"ffn_fwd_{l}b", x2, _row(norm_w[l, 2]), gate, up, down, 2 * l + 1)
        saved.append((x0, x2, p, sv))
    loss, dx, dfw = _final("final", x, _row(final_w), target)
    dnorm = [[None] * 3 for _ in range(DEPTH)]
    dffn = None
    dmix = [None] * (DEPTH // 2)
    dconf = [None] * (DEPTH // 2)
    for l in reversed(range(DEPTH)):
        x0, x2, p, sv = saved[l]
        dx, dnorm[l][2], *dffn = _ffn_bwd(f"ffn_bwd_{l}b", x2, _row(norm_w[l, 2]), gate, up, down, 2 * l + 1, dx,
                                          dffn)
        if l % 2 == 0:
            dx, dmix[l // 2] = _mixer_bwd(str(l), dx, p, sv)
            dnorm[l][1] = dmix[l // 2].pop("nw")
        else:
            dx, dconf[l // 2] = _conformer_bwd(str(l), dx, p, sv)
            dnorm[l][1] = dconf[l // 2].pop("nw")
        dx, dnorm[l][0], *dffn = _ffn_bwd(f"ffn_bwd_{l}a", x0, _row(norm_w[l, 0]), gate, up, down, 2 * l, dx, dffn)
    dnorm_w = jnp.stack([jnp.concatenate(r, axis=0) for r in dnorm])
    return loss, dx, dict(norm_w=dnorm_w, ffn=dffn, mixers=dmix, confs=dconf, final_w=dfw)


def _place():
    x, y, c = lax.axis_index("x"), lax.axis_index("y"), lax.axis_index("c")
    chips = [(1 - x, y), (x, 1 - y), (1 - x, 1 - y)]
    return x, y, c, 2 * x + y, chips, [2 * px + py for px, py in chips]


def _all_gather(bigs, small):
    nb = len(bigs)
    n_in = nb + 1

    def body(*refs):
        ins, outs = refs[:n_in], refs[n_in:2 * n_in]
        send, recv, fsend, frecv, lsem = refs[2 * n_in:]
        x, y, c, me, chips, cidx = _place()
        sib = (x, y, 1 - c)
        local = [pltpu.make_async_copy(ins[a], outs[a].at[me], lsem.at[a]) for a in range(n_in)]
        for cp in local:
            cp.start()

        def half(a, who):
            h = bigs[a].shape[0] // 2
            return pl.ds(who * h, h)

        def ici(a, j):
            k = a * 3 + j
            if a == nb:
                return pltpu.make_async_remote_copy(ins[a], outs[a].at[me], send.at[k], recv.at[k],
                                                    device_id=(*chips[j], c), device_id_type=MESH)
            return pltpu.make_async_remote_copy(ins[a].at[half(a, c)], outs[a].at[me, half(a, c)], send.at[k],
                                                recv.at[k], device_id=(*chips[j], c), device_id_type=MESH)

        def landed(a, j):
            k = a * 3 + j
            dst = outs[a].at[cidx[j]] if a == nb else outs[a].at[cidx[j], half(a, c)]
            return pltpu.make_async_remote_copy(dst, dst, send.at[k], recv.at[k], device_id=(*chips[j], c),
                                                device_id_type=MESH)

        def passed(a, j, who):
            k = a * 3 + j
            part = outs[a].at[cidx[j], half(a, who)]
            return pltpu.make_async_remote_copy(part, part, fsend.at[k], frecv.at[k], device_id=sib,
                                                device_id_type=MESH)

        sends = [ici(a, j) for a in range(n_in) for j in range(3)]
        for cp in sends:
            cp.start()
        fwd = []
        for a in range(nb):
            for j in range(3):
                landed(a, j).wait_recv()
                cp = passed(a, j, c)
                cp.start()
                fwd.append(cp)
        for j in range(3):
            landed(nb, j).wait_recv()
        for a in range(nb):
            for j in range(3):
                passed(a, j, 1 - c).wait_recv()
        for cp in sends + fwd:
            cp.wait_send()
        for cp in local:
            cp.wait()

    arrs = list(bigs) + [small]
    return pl.pallas_call(
        body, in_specs=[ANY] * n_in, out_specs=[ANY] * n_in,
        out_shape=[_sds((NCHIP,) + a.shape, a.dtype) for a in arrs],
        scratch_shapes=[pltpu.SemaphoreType.DMA((3 * n_in,)), pltpu.SemaphoreType.DMA((3 * n_in,)),
                        pltpu.SemaphoreType.DMA((3 * nb,)), pltpu.SemaphoreType.DMA((3 * nb,)),
                        pltpu.SemaphoreType.DMA((n_in,))],
        name="all_gather_weights",
    )(*arrs)


def _swap_halves(grads):
    n = len(grads)

    def body(*refs):
        ins, outs, send, recv = refs[:n], refs[n:2 * n], refs[2 * n], refs[2 * n + 1]
        x, y, c, _, _, _ = _place()
        cps = []
        for a in range(n):
            h = grads[a].shape[1] // 2
            cps.append(pltpu.make_async_remote_copy(ins[a].at[:, pl.ds((1 - c) * h, h)], outs[a], send.at[a],
                                                    recv.at[a], device_id=(x, y, 1 - c), device_id_type=MESH))
        for cp in cps:
            cp.start()
        for cp in cps:
            cp.wait()

    return pl.pallas_call(
        body, in_specs=[ANY] * n, out_specs=[ANY] * n,
        out_shape=[_sds((NCHIP, g.shape[1] // 2) + g.shape[2:], g.dtype) for g in grads],
        scratch_shapes=[pltpu.SemaphoreType.DMA((n,)), pltpu.SemaphoreType.DMA((n,))], name="swap_grad_halves",
    )(*grads)


def _row_tile(r, cap=256):
    return max(t for t in range(8, cap + 1, 8) if r % t == 0)


def _add_half(name, g, r, c_arr):
    _, l, rows, cols = g.shape
    h = l // 2
    tr = _row_tile(rows)

    def body(c_ref, g_ref, r_ref, o_ref):
        o_ref[...] = (g_ref[...].astype(F32) + r_ref[...].astype(F32)).astype(BF16)

    blk = (None, None, tr, cols)
    return pl.pallas_call(
        body,
        grid_spec=pltpu.PrefetchScalarGridSpec(
            num_scalar_prefetch=1, grid=(NCHIP, h, rows // tr),
            in_specs=[pl.BlockSpec(blk, lambda j, i, t, c_ref: (j, c_ref[0] * h + i, t, 0)),
                      pl.BlockSpec(blk, lambda j, i, t, c_ref: (j, i, t, 0))],
            out_specs=pl.BlockSpec(blk, lambda j, i, t, c_ref: (j, i, t, 0))),
        out_shape=_sds((NCHIP, h, rows, cols), BF16), name=name,
        compiler_params=_cp("parallel", "parallel", "parallel"),
    )(c_arr, g, r)


def _reduce_scatter(parts, small, rep):
    nb = len(parts)

    def body(*refs):
        ins, small_in, rep_in = refs[:nb], refs[nb], refs[nb + 1]
        outs, small_out, rep_out = refs[nb + 2:2 * nb + 2], refs[2 * nb + 2], refs[2 * nb + 3]
        send, recv, fsend, frecv, lsem, ssend, srecv = refs[2 * nb + 4:]
        x, y, c, me, chips, cidx = _place()
        sib = (x, y, 1 - c)
        dev = 4 * x + 2 * y + c
        local = [pltpu.make_async_copy(ins[a].at[me], outs[a].at[c, me], lsem.at[a]) for a in range(nb)]
        local.append(pltpu.make_async_copy(small_in.at[me], small_out.at[dev], lsem.at[nb]))
        local.append(pltpu.make_async_copy(rep_in, rep_out.at[dev], lsem.at[nb + 1]))
        for cp in local:
            cp.start()

        def ici(a, j):
            return pltpu.make_async_remote_copy(ins[a].at[cidx[j]], outs[a].at[c, me], send.at[a * 3 + j],
                                                recv.at[a * 3 + j], device_id=(*chips[j], c), device_id_type=MESH)

        def landed(a, j):
            dst = outs[a].at[c, cidx[j]]
            return pltpu.make_async_remote_copy(dst, dst, send.at[a * 3 + j], recv.at[a * 3 + j],
                                                device_id=(*chips[j], c), device_id_type=MESH)

        def passed(a, j, who):
            chip = me if j == 3 else cidx[j]
            dst = outs[a].at[who, chip]
            src = ins[a].at[me] if j == 3 else dst
            return pltpu.make_async_remote_copy(src, dst, fsend.at[a * 4 + j], frecv.at[a * 4 + j], device_id=sib,
                                                device_id_type=MESH)

        def peer(r):
            return (1 - x if r & 4 else x), (1 - y if r & 2 else y), (1 - c if r & 1 else c)

        def tiny(r, which):
            px, py, pc = peer(r)
            k = (r - 1) * 2 + which
            if which == 0:
                return pltpu.make_async_remote_copy(small_in.at[2 * px + py], small_out.at[dev], ssend.at[k],
                                                    srecv.at[k], device_id=(px, py, pc), device_id_type=MESH)
            return pltpu.make_async_remote_copy(rep_in, rep_out.at[dev], ssend.at[k], srecv.at[k],
                                                device_id=(px, py, pc), device_id_type=MESH)

        def tiny_landed(r, which):
            px, py, pc = peer(r)
            k = (r - 1) * 2 + which
            dst = (small_out if which == 0 else rep_out).at[4 * px + 2 * py + pc]
            return pltpu.make_async_remote_copy(dst, dst, ssend.at[k], srecv.at[k], device_id=(px, py, pc),
                                                device_id_type=MESH)

        sends = [ici(a, j) for a in range(nb) for j in range(3)]
        sends += [passed(a, 3, c) for a in range(nb)]
        sends += [tiny(r, w) for r in range(1, NDEV) for w in range(2)]
        for cp in sends:
            cp.start()
        for a in range(nb):
            for j in range(3):
                landed(a, j).wait_recv()
                cp = passed(a, j, c)
                cp.start()
                sends.append(cp)
        for a in range(nb):
            for j in range(4):
                passed(a, j, 1 - c).wait_recv()
        for r in range(1, NDEV):
            for w in range(2):
                tiny_landed(r, w).wait_recv()
        for cp in sends:
            cp.wait_send()
        for cp in local:
            cp.wait()

    n_in = nb + 2
    out_shape = [_sds((2,) + p.shape, p.dtype) for p in parts]
    out_shape += [_sds((NDEV,) + small.shape[1:], F32), _sds((NDEV,) + rep.shape, F32)]
    return pl.pallas_call(
        body, in_specs=[ANY] * n_in, out_specs=[ANY] * n_in, out_shape=out_shape,
        scratch_shapes=[pltpu.SemaphoreType.DMA((3 * nb,)), pltpu.SemaphoreType.DMA((3 * nb,)),
                        pltpu.SemaphoreType.DMA((4 * nb,)), pltpu.SemaphoreType.DMA((4 * nb,)),
                        pltpu.SemaphoreType.DMA((n_in,)), pltpu.SemaphoreType.DMA((2 * (NDEV - 1),)),
                        pltpu.SemaphoreType.DMA((2 * (NDEV - 1),))],
        name="reduce_scatter_grads",
    )(*parts, small, rep)


def _adamw_math(w, g, m, v):
    m = B1 * m + (1.0 - B1) * g
    v = B2 * v + (1.0 - B2) * (g * g)
    m_hat = m / (1.0 - B1 ** STEP)
    v_hat = v / (1.0 - B2 ** STEP)
    return -LR * (m_hat / (jnp.sqrt(v_hat) + AEPS) + WD * w), m, v


def _adamw_big(name, w, m, v, parts):
    l, rows, cols = w.shape
    h = l // 2
    tr = _row_tile(rows)

    def body(w_ref, m_ref, v_ref, p_ref, g_ref, d_ref, nm_ref, nv_ref):
        g = p_ref[0].astype(F32)
        for q in range(1, NCHIP):
            g = g + p_ref[q].astype(F32)
        d, nm, nv = _adamw_math(w_ref[...], g, m_ref[...], v_ref[...])
        g_ref[...], d_ref[...], nm_ref[...], nv_ref[...] = g, d, nm, nv

    spec = pl.BlockSpec((None, tr, cols), lambda p, i, t: (p * h + i, t, 0))
    return pl.pallas_call(
        body, grid=(2, h, rows // tr),
        in_specs=[spec, spec, spec, pl.BlockSpec((None, NCHIP, None, tr, cols), lambda p, i, t: (p, 0, i, t, 0))],
        out_specs=[spec] * 4, out_shape=[_sds(w.shape)] * 4, name=name,
        compiler_params=_cp("parallel", "parallel", "parallel"),
    )(w, m, v, parts)


def _adamw_small(name, w, m, v, parts):
    def body(w_ref, m_ref, v_ref, p_ref, g_ref, d_ref, nm_ref, nv_ref):
        g = p_ref[0]
        for q in range(1, NDEV):
            g = g + p_ref[q]
        d, nm, nv = _adamw_math(w_ref[...], g, m_ref[...], v_ref[...])
        g_ref[...], d_ref[...], nm_ref[...], nv_ref[...] = g, d, nm, nv

    return pl.pallas_call(body, out_shape=[_sds(w.shape)] * 4, name=name)(w, m, v, parts)


def _pack(arrs, rows):
    flat = jnp.concatenate([a.reshape(-1) for a in arrs])
    return jnp.pad(flat, (0, rows * LANES - flat.shape[0])).reshape(rows, LANES)


def _unpack(packed, shapes):
    flat, out, o = packed.reshape(-1), [], 0
    for s in shapes:
        n = 1
        for d in s:
            n *= d
        out.append(flat[o:o + n].reshape(s))
        o += n
    return out


SMALL_ROWS, REP_ROWS = 200, 16


def kernel(x, norm_w, ffn_w_gate, ffn_w_up, ffn_w_down, mix_w_in, dn_conv_w, attn_sinks, dn_a_log, dn_dt_bias, dn_norm_w, mix_w_out, conv_w_pw1, conv_b_pw1, conv_w_dw, conv_b_dw, conv_ln_w, conv_ln_b, conv_w_pw2, conv_b_pw2, final_norm_w, loss_target, m_norm_w, m_ffn_w_gate, m_ffn_w_up, m_ffn_w_down, m_mix_w_in, m_dn_conv_w, m_attn_sinks, m_dn_a_log, m_dn_dt_bias, m_dn_norm_w, m_mix_w_out, m_conv_w_pw1, m_conv_b_pw1, m_conv_w_dw, m_conv_b_dw, m_conv_ln_w, m_conv_ln_b, m_conv_w_pw2, m_conv_b_pw2, m_final_norm_w, v_norm_w, v_ffn_w_gate, v_ffn_w_up, v_ffn_w_down, v_mix_w_in, v_dn_conv_w, v_attn_sinks, v_dn_a_log, v_dn_dt_bias, v_dn_norm_w, v_mix_w_out, v_conv_w_pw1, v_conv_b_pw1, v_conv_w_dw, v_conv_b_dw, v_conv_ln_w, v_conv_ln_b, v_conv_w_pw2, v_conv_b_pw2, v_final_norm_w):
    big_names = ["ffn_w_gate", "ffn_w_up", "ffn_w_down", "mix_w_in", "mix_w_out", "conv_w_pw1", "conv_w_pw2"]
    small_names = ["norm_w", "dn_conv_w", "conv_b_pw1", "conv_w_dw", "conv_b_dw", "conv_ln_w", "conv_ln_b",
                   "conv_b_pw2"]
    rep_names = ["attn_sinks", "dn_a_log", "dn_dt_bias", "dn_norm_w", "final_norm_w"]
    w = dict(norm_w=norm_w, ffn_w_gate=ffn_w_gate, ffn_w_up=ffn_w_up, ffn_w_down=ffn_w_down, mix_w_in=mix_w_in, dn_conv_w=dn_conv_w, attn_sinks=attn_sinks, dn_a_log=dn_a_log, dn_dt_bias=dn_dt_bias, dn_norm_w=dn_norm_w, mix_w_out=mix_w_out, conv_w_pw1=conv_w_pw1, conv_b_pw1=conv_b_pw1, conv_w_dw=conv_w_dw, conv_b_dw=conv_b_dw, conv_ln_w=conv_ln_w, conv_ln_b=conv_ln_b, conv_w_pw2=conv_w_pw2, conv_b_pw2=conv_b_pw2, final_norm_w=final_norm_w)
    m = dict(norm_w=m_norm_w, ffn_w_gate=m_ffn_w_gate, ffn_w_up=m_ffn_w_up, ffn_w_down=m_ffn_w_down, mix_w_in=m_mix_w_in, dn_conv_w=m_dn_conv_w, attn_sinks=m_attn_sinks, dn_a_log=m_dn_a_log, dn_dt_bias=m_dn_dt_bias, dn_norm_w=m_dn_norm_w, mix_w_out=m_mix_w_out, conv_w_pw1=m_conv_w_pw1, conv_b_pw1=m_conv_b_pw1, conv_w_dw=m_conv_w_dw, conv_b_dw=m_conv_b_dw, conv_ln_w=m_conv_ln_w, conv_ln_b=m_conv_ln_b, conv_w_pw2=m_conv_w_pw2, conv_b_pw2=m_conv_b_pw2, final_norm_w=m_final_norm_w)
    v = dict(norm_w=v_norm_w, ffn_w_gate=v_ffn_w_gate, ffn_w_up=v_ffn_w_up, ffn_w_down=v_ffn_w_down, mix_w_in=v_mix_w_in, dn_conv_w=v_dn_conv_w, attn_sinks=v_attn_sinks, dn_a_log=v_dn_a_log, dn_dt_bias=v_dn_dt_bias, dn_norm_w=v_dn_norm_w, mix_w_out=v_mix_w_out, conv_w_pw1=v_conv_w_pw1, conv_b_pw1=v_conv_b_pw1, conv_w_dw=v_conv_w_dw, conv_b_dw=v_conv_b_dw, conv_ln_w=v_conv_ln_w, conv_ln_b=v_conv_ln_b, conv_w_pw2=v_conv_w_pw2, conv_b_pw2=v_conv_b_pw2, final_norm_w=v_final_norm_w)
    order = ["norm_w", "ffn_w_gate", "ffn_w_up", "ffn_w_down", "mix_w_in", "dn_conv_w", "attn_sinks", "dn_a_log",
             "dn_dt_bias", "dn_norm_w", "mix_w_out", "conv_w_pw1", "conv_b_pw1", "conv_w_dw", "conv_b_dw",
             "conv_ln_w", "conv_ln_b", "conv_w_pw2", "conv_b_pw2", "final_norm_w"]

    def view3(a):
        return a.reshape((-1,) + a.shape[-2:])

    small_shapes = [w[n].shape for n in small_names]
    rep_shapes = [w[n].shape for n in rep_names]

    gathered = _all_gather([view3(w[n]).astype(BF16) for n in big_names],
                           _pack([w[n] for n in small_names], SMALL_ROWS))
    gate, up, down, win_all, wout_all, pw1_all, pw2_all, small_all = gathered
    per_chip = [_unpack(small_all[q], small_shapes) for q in range(NCHIP)]
    sm = {n: jnp.concatenate([per_chip[q][i] for q in range(NCHIP)], axis=-1) for i, n in enumerate(small_names)}
    w_in = win_all.transpose(1, 2, 0, 3).reshape(2, D, IN_COLS)
    w_out = wout_all.transpose(1, 0, 2, 3).reshape(2, D, D)
    pw1 = pw1_all.transpose(1, 2, 0, 3).reshape(2, D, 2 * D)
    pw2 = pw2_all.transpose(1, 0, 2, 3).reshape(2, D, D)
    mixers = [dict(w_in=w_in[e], dn_conv_w=sm["dn_conv_w"][e], sinks=_row(attn_sinks[e]), a_log=_row(dn_a_log[e]),
                   dt_bias=_row(dn_dt_bias[e]), dn_norm_w=_row(dn_norm_w[e]), wo_a=w_out[e, :Q_A],
                   wo_b=w_out[e, Q_A:]) for e in range(2)]
    confs = [dict(b1a=_row(sm["conv_b_pw1"][e, :D]), b1b=_row(sm["conv_b_pw1"][e, D:]), w1a=pw1[e, :, :D],
                  w1b=pw1[e, :, D:], w_dw=sm["conv_w_dw"][e], b_dw=_row(sm["conv_b_dw"][e]),
                  ln_w=_row(sm["conv_ln_w"][e]), ln_b=_row(sm["conv_ln_b"][e]), b2=_row(sm["conv_b_pw2"][e]),
                  w2=pw2[e]) for e in range(2)]

    loss, dx, g = _local_step(x[0], loss_target[0], sm["norm_w"], gate, up, down, mixers, confs, final_norm_w)

    def chip_cols(a):
        return a.reshape(2, a.shape[1], NCHIP, -1).transpose(2, 0, 1, 3)

    def chip_rows(a):
        return a.reshape(2, NCHIP, -1, a.shape[2]).transpose(1, 0, 2, 3)

    gm, gc = g["mixers"], g["confs"]
    big_g = list(g["ffn"])
    big_g.append(chip_cols(jnp.stack([gm[e]["w_in"] for e in range(2)])).astype(BF16))
    big_g.append(chip_rows(jnp.stack([jnp.concatenate([gm[e]["wo_a"], gm[e]["wo_b"]], axis=0)
                                      for e in range(2)])).astype(BF16))
    big_g.append(chip_cols(jnp.stack([jnp.concatenate([gc[e]["w1a"], gc[e]["w1b"]], axis=1)
                                      for e in range(2)])).astype(BF16))
    big_g.append(chip_rows(jnp.stack([gc[e]["w2"] for e in range(2)])).astype(BF16))
    small_g = dict(
        norm_w=g["norm_w"], dn_conv_w=jnp.stack([gm[e]["dn_conv_w"] for e in range(2)]),
        conv_b_pw1=jnp.stack([jnp.concatenate([gc[e]["b1a"], gc[e]["b1b"]], axis=1)[0] for e in range(2)]),
        conv_w_dw=jnp.stack([gc[e]["w_dw"] for e in range(2)]),
        conv_b_dw=jnp.stack([gc[e]["b_dw"][0] for e in range(2)]),
        conv_ln_w=jnp.stack([gc[e]["ln_w"][0] for e in range(2)]),
        conv_ln_b=jnp.stack([gc[e]["ln_b"][0] for e in range(2)]),
        conv_b_pw2=jnp.stack([gc[e]["b2"][0] for e in range(2)]))
    small_by_chip = jnp.stack([_pack([jnp.split(small_g[n], NCHIP, axis=-1)[q] for n in small_names], SMALL_ROWS)
                               for q in range(NCHIP)])
    rep_g = _pack([jnp.stack([gm[e]["sinks"][0] for e in range(2)]), jnp.stack([gm[e]["a_log"][0] for e in range(2)]),
                   jnp.stack([gm[e]["dt_bias"][0] for e in range(2)]),
                   jnp.stack([gm[e]["dn_norm_w"][0] for e in range(2)]), g["final_w"][0]], REP_ROWS)

    from_sibling = _swap_halves(big_g)
    c_arr = lax.axis_index("c").astype(jnp.int32).reshape(1)
    partial = [_add_half(f"add_half_{n}", gg, rr, c_arr) for n, gg, rr in zip(big_names, big_g, from_sibling)]
    reduced = _reduce_scatter(partial, small_by_chip, rep_g)

    res = {}
    for n, parts in zip(big_names, reduced[:len(big_names)]):
        outs = _adamw_big(f"adamw_{n}", view3(w[n]), view3(m[n]), view3(v[n]), parts)
        res[n] = [o.reshape(w[n].shape) for o in outs]
    outs = _adamw_small("adamw_small", *[_pack([d[n] for n in small_names], SMALL_ROWS) for d in (w, m, v)],
                        reduced[-2])
    for i, n in enumerate(small_names):
        res[n] = [_unpack(o, small_shapes)[i] for o in outs]
    outs = _adamw_small("adamw_replicated", *[_pack([d[n] for n in rep_names], REP_ROWS) for d in (w, m, v)],
                        reduced[-1])
    for i, n in enumerate(rep_names):
        res[n] = [_unpack(o, rep_shapes)[i] for o in outs]

    total = lax.psum(loss[0, 0], ("x", "y", "c"))
    return (total, dx[None], *[res[n][0] for n in order], *[res[n][1] for n in order],
            *[res[n][2] for n in order], *[res[n][3] for n in order])
```

```python
import jax
import jax.numpy as jnp
from jax import lax
from jax.experimental import pallas as pl
from jax.experimental.pallas import tpu as pltpu
from jax.experimental.pallas import tpu_sc as plsc

F32, BF16 = jnp.float32, jnp.bfloat16
MESH = pl.DeviceIdType.MESH
ANY = pl.BlockSpec(memory_space=pl.ANY)

T, D, F = 2048, 1024, 2816
DEPTH = 4
EPS = 1e-6
HEADS, HDIM, KV_HEADS, GROUP = 8, 64, 2, 4
WINDOW = BLOCK = 128
CHUNK = 64
NCHUNK = T // CHUNK
DN_CONV, CONV_WIDTH = 4, 31
Q_A, KV_A, QKV_B, V_B = 512, 128, 1536, 512
IN_COLS = 2832
IN_SPLITS = (0, 512, 640, 768, 2304, 2816, 2832)
NCHIP, NDEV = 4, 8
FS = F // NCHIP
LR, B1, B2, AEPS, WD, STEP = 0.001, 0.9, 0.999, 1e-08, 0.01, 10
V7X_VMEM_BYTES = 64 * 1024 * 1024
VMEM_LIMIT = V7X_VMEM_BYTES * 7 // 8
LANES = 128


def _cp(*sem):
    return pltpu.CompilerParams(dimension_semantics=sem, vmem_limit_bytes=VMEM_LIMIT)


def _sds(shape, dtype=F32):
    return jax.ShapeDtypeStruct(tuple(shape), dtype)


def _full(shape):
    nd = len(shape)
    return pl.BlockSpec(tuple(shape), lambda *_: (0,) * nd)


def _split_bf16(a):
    hi = a.astype(BF16)
    return hi, (a - hi.astype(F32)).astype(BF16)


def _dg(a, b, ca, cb, hi=False):
    if a.ndim == 3 and b.ndim == 3:
        dims = (((ca + 1,), (cb + 1,)), ((0,), (0,)))
    else:
        dims = (((ca,), (cb,)), ((), ()))
    dot = lambda p, q: lax.dot_general(p, q, dims, preferred_element_type=F32)
    if hi:
        a_hi, a_lo = _split_bf16(a.astype(F32))
        b_hi, b_lo = _split_bf16(b.astype(F32))
        return dot(a_hi, b_hi) + (dot(a_hi, b_lo) + dot(a_lo, b_hi))
    return dot(a.astype(BF16), b.astype(BF16))


def _make_mm(hi):
    @jax.custom_vjp
    def nn(a, b):
        return _dg(a, b, 1, 0, hi)

    @jax.custom_vjp
    def nt(a, b):
        return _dg(a, b, 1, 1, hi)

    @jax.custom_vjp
    def tn(a, b):
        return _dg(a, b, 0, 0, hi)

    nn.defvjp(lambda a, b: (_dg(a, b, 1, 0, hi), (a, b)),
              lambda r, g: (_dg(g, r[1], 1, 1, hi).astype(r[0].dtype), _dg(r[0], g, 0, 0, hi).astype(r[1].dtype)))
    nt.defvjp(lambda a, b: (_dg(a, b, 1, 1, hi), (a, b)),
              lambda r, g: (_dg(g, r[1], 1, 0, hi).astype(r[0].dtype), _dg(g, r[0], 0, 0, hi).astype(r[1].dtype)))
    tn.defvjp(lambda a, b: (_dg(a, b, 0, 0, hi), (a, b)),
              lambda r, g: (_dg(r[1], g, 1, 1, hi).astype(r[0].dtype), _dg(r[0], g, 1, 0, hi).astype(r[1].dtype)))
    return nn, nt, tn


_nn, _nt, _tn = _make_mm(False)
_nn_hi, _nt_hi, _tn_hi = _make_mm(True)


def _rms(x, w):
    return x * lax.rsqrt(jnp.mean(x * x, axis=-1, keepdims=True) + EPS) * w


def _layernorm(x, w, b):
    xc = x - jnp.mean(x, axis=-1, keepdims=True)
    return xc * lax.rsqrt(jnp.mean(xc * xc, axis=-1, keepdims=True) + EPS) * w + b


def _silu(x):
    return x * jax.nn.sigmoid(x)


def _iota2(shape, dim):
    return lax.broadcasted_iota(jnp.int32, shape, dim)


def _blk_fwd(name, pre, lhs_idx, post, toks, smalls, weights, outs, tm=512):
    nt_, ns, nw = len(toks), len(smalls), len(weights)

    def body(*refs):
        tv = [r[...] for r in refs[:nt_]]
        sv = [r[...] for r in refs[nt_:nt_ + ns]]
        wr = refs[nt_ + ns:nt_ + ns + nw]
        orf = refs[nt_ + ns + nw:]
        lhs = pre(tv, sv)
        ys = [_dg(lhs[i], w[...], 1, 0) for i, w in zip(lhs_idx, wr)]
        for o_ref, o in zip(orf, post(ys, tv, sv)):
            o_ref[...] = o.astype(o_ref.dtype)

    in_specs = ([pl.BlockSpec((tm, a.shape[1]), lambda i: (i, 0)) for a in toks]
                + [_full(a.shape) for a in smalls] + [_full(w.shape) for w in weights])
    out_specs = [pl.BlockSpec((tm, w_), lambda i: (i, 0)) for w_, _ in outs]
    return pl.pallas_call(
        body, grid=(T // tm,), in_specs=in_specs, out_specs=out_specs,
        out_shape=[_sds((T, w_), dt) for w_, dt in outs], name=name, compiler_params=_cp("parallel"),
    )(*toks, *smalls, *weights)


def _blk_bwd(name, pre, lhs_idx, post, toks, smalls, weights, ct_groups, res=None, tm=256, wchunk=512):
    nt_, ns, nw = len(toks), len(smalls), len(weights)
    cts = [a for g in ct_groups for a in g]
    nc = len(cts)
    widths = [sum(a.shape[1] for a in g) for g in ct_groups]
    has_res = res is not None

    def body(*refs):
        p = 0
        tr = refs[p:p + nt_]; p += nt_
        sr = refs[p:p + ns]; p += ns
        wr = refs[p:p + nw]; p += nw
        cr = refs[p:p + nc]; p += nc
        rr = refs[p:p + has_res]; p += has_res
        dtr = refs[p:p + nt_]; p += nt_
        dsr = refs[p:p + ns]; p += ns
        dwr = refs[p:p + nw]; p += nw
        scr = refs[p:]
        i = pl.program_id(0)

        @pl.when(i == 0)
        def _():
            for r in list(dsr) + list(dwr):
                r[...] = jnp.zeros_like(r)

        tv = [r[...] for r in tr]
        sv = [r[...] for r in sr]
        ctv, q, si = [], 0, 0
        for g in ct_groups:
            if len(g) == 1:
                ctv.append(cr[q][...].astype(F32))
            else:
                off = 0
                for j, a in enumerate(g):
                    scr[si][:, off:off + a.shape[1]] = cr[q + j][...].astype(F32)
                    off += a.shape[1]
                ctv.append(scr[si][...])
                si += 1
            q += len(g)

        lhs, vjp_pre = jax.vjp(lambda *a: tuple(pre(list(a[:nt_]), list(a[nt_:]))), *tv, *sv)
        lhs_b = [l.astype(BF16) for l in lhs]
        ys = [_dg(lhs_b[k], w[...], 1, 0) for k, w in zip(lhs_idx, wr)]
        _, vjp_post = jax.vjp(lambda *a: tuple(post(list(a[:nw]), list(a[nw:nw + nt_]), list(a[nw + nt_:]))),
                              *ys, *tv, *sv)
        gp = vjp_post(tuple(ctv))
        dys, dt_post, ds_post = gp[:nw], gp[nw:nw + nt_], gp[nw + nt_:]
        dlhs = [None] * len(lhs)
        for k, w, dy, dw in zip(lhs_idx, wr, dys, dwr):
            dyb = dy.astype(BF16)
            n = w.shape[1]
            for c0 in range(0, n, wchunk):
                c1 = min(n, c0 + wchunk)
                dw[:, c0:c1] += _dg(lhs_b[k], dyb[:, c0:c1], 0, 0)
            d = _dg(dyb, w[...], 1, 1)
            dlhs[k] = d if dlhs[k] is None else dlhs[k] + d
        gq = vjp_pre(tuple(d.astype(l.dtype) for d, l in zip(dlhs, lhs)))
        dt_pre, ds_pre = gq[:nt_], gq[nt_:]
        for j in range(nt_):
            d = dt_post[j] + dt_pre[j]
            if j == 0 and has_res:
                d = d + rr[0][...]
            dtr[j][...] = d
        for j in range(ns):
            dsr[j][...] += ds_post[j] + ds_pre[j]

    tok_spec = lambda a: pl.BlockSpec((tm, a.shape[1]), lambda i: (i, 0))
    in_specs = ([tok_spec(a) for a in toks] + [_full(a.shape) for a in smalls] + [_full(w.shape) for w in weights]
                + [tok_spec(a) for a in cts] + ([tok_spec(res)] if has_res else []))
    out_specs = [tok_spec(a) for a in toks] + [_full(a.shape) for a in smalls] + [_full(w.shape) for w in weights]
    out_shape = ([_sds(a.shape) for a in toks] + [_sds(a.shape) for a in smalls] + [_sds(w.shape) for w in weights])
    scratch = [pltpu.VMEM((tm, wd), F32) for g, wd in zip(ct_groups, widths) if len(g) > 1]
    outs = pl.pallas_call(
        body, grid=(T // tm,), in_specs=in_specs, out_specs=out_specs, out_shape=out_shape,
        scratch_shapes=scratch, name=name, compiler_params=_cp("arbitrary"),
    )(*toks, *smalls, *weights, *cts, *([res] if has_res else []))
    return outs[:nt_], outs[nt_:nt_ + ns], outs[nt_ + ns:]


def _ffn_fwd(name, x, nw, gu, wd, idx, tm=512):
    def body(x_ref, nw_ref, wg_ref, wu_ref, wd_ref, o_ref, h_scr):
        s = pl.program_id(1)

        @pl.when(s == 0)
        def _():
            xv = x_ref[...]
            h_scr[...] = _rms(xv, nw_ref[...]).astype(BF16)
            o_ref[...] = xv

        h = h_scr[...]
        a = _dg(h, wg_ref[...], 1, 0)
        b = _dg(h, wu_ref[...], 1, 0)
        o_ref[...] += 0.5 * _dg(_silu(a) * b, wd_ref[...], 1, 0)

    wspec = lambda r, c, k: pl.BlockSpec((None, None, r, c), lambda i, s: (s, idx, k, 0))
    return pl.pallas_call(
        body, grid=(T // tm, NCHIP),
        in_specs=[pl.BlockSpec((tm, D), lambda i, s: (i, 0)), _full((1, D)), wspec(D, FS, 0), wspec(D, FS, 1),
                  wspec(FS, D, 0)],
        out_specs=pl.BlockSpec((tm, D), lambda i, s: (i, 0)), out_shape=_sds((T, D)),
        scratch_shapes=[pltpu.VMEM((tm, D), BF16)], name=name, compiler_params=_cp("parallel", "arbitrary"),
    )(x, nw, gu, gu, wd)


def _ffn_bwd(name, x, nw, gu, wd, idx, dy, gbufs=None, tm=512):
    ni = T // tm

    def body(x_ref, dy_ref, nw_ref, wg_ref, wu_ref, wd_ref, dx_ref, dnw_ref, dgu_ref, dwd_ref,
             dh_acc, ag, au, ad):
        s, i = pl.program_id(0), pl.program_id(1)
        rows = pl.ds(pl.multiple_of(i * tm, tm), tm)

        @pl.when((s == 0) & (i == 0))
        def _():
            dnw_ref[...] = jnp.zeros_like(dnw_ref)

        @pl.when(i == 0)
        def _():
            ag[...] = jnp.zeros_like(ag)
            au[...] = jnp.zeros_like(au)
            ad[...] = jnp.zeros_like(ad)

        xv, nwv, dyv = x_ref[...], nw_ref[...], dy_ref[...]
        h, vjp_rms = jax.vjp(_rms, xv, nwv)
        hb = h.astype(BF16)
        a = _dg(hb, wg_ref[...], 1, 0)
        b = _dg(hb, wu_ref[...], 1, 0)
        sa = jax.nn.sigmoid(a)
        act = a * sa
        dyb = (0.5 * dyv).astype(BF16)
        ad[...] += _dg(act * b, dyb, 0, 0)
        dact = _dg(dyb, wd_ref[...], 1, 1)
        da = (dact * b * (sa * (1.0 + a * (1.0 - sa)))).astype(BF16)
        db = (dact * act).astype(BF16)
        ag[...] += _dg(hb, da, 0, 0)
        au[...] += _dg(hb, db, 0, 0)
        dh = _dg(da, wg_ref[...], 1, 1) + _dg(db, wu_ref[...], 1, 1)

        @pl.when(s == 0)
        def _():
            dh_acc[rows, :] = dh

        @pl.when(s > 0)
        def _():
            dh_acc[rows, :] += dh

        @pl.when(s == NCHIP - 1)
        def _():
            dx, dnw = vjp_rms(dh_acc[rows, :])
            dx_ref[...] = dyv + dx
            dnw_ref[...] += dnw

        @pl.when(i == ni - 1)
        def _():
            dgu_ref[0:D, :] = ag[...].astype(BF16)
            dgu_ref[D:, :] = au[...].astype(BF16)
            dwd_ref[...] = ad[...].astype(BF16)

    wspec = lambda r, c, k: pl.BlockSpec((None, None, r, c), lambda s, i: (s, idx, k, 0),
                                         pipeline_mode=pl.Buffered(1))
    last = lambda s, i: (jnp.where(s == NCHIP - 1, i, 0), 0)
    nb = 0 if gbufs is None else 2
    return pl.pallas_call(
        lambda *refs: body(*refs[:6], *refs[6 + nb:]), grid=(NCHIP, ni),
        in_specs=[pl.BlockSpec((tm, D), lambda s, i: (i, 0)), pl.BlockSpec((tm, D), lambda s, i: (i, 0)),
                  _full((1, D)), wspec(D, FS, 0), wspec(D, FS, 1), wspec(FS, D, 0)] + [ANY] * nb,
        out_specs=[pl.BlockSpec((tm, D), last), _full((1, D)), wspec(2 * D, FS, 0), wspec(FS, D, 0)],
        out_shape=[_sds((T, D)), _sds((1, D)), _sds(gu.shape, BF16), _sds(wd.shape, BF16)],
        input_output_aliases={6 + k: 2 + k for k in range(nb)},
        scratch_shapes=[pltpu.VMEM((T, D), F32), pltpu.VMEM((D, FS), F32), pltpu.VMEM((D, FS), F32),
                        pltpu.VMEM((FS, D), F32)],
        name=name, compiler_params=_cp("arbitrary", "arbitrary"),
    )(x, dy, nw, gu, gu, wd, *(gbufs or ()))


CONV_ROWS = 256


def _conv_pad(k):
    return 8 * ((k - 1 + 7) // 8)


def _conv_fwd(name, x, w, b, act):
    k_w, c = w.shape
    tc = 256 if c % 256 == 0 else LANES
    pad = _conv_pad(k_w)
    has_b = b is not None

    def body(*refs):
        x_ref, w_ref = refs[0], refs[1]
        b_ref = refs[2] if has_b else None
        y_ref, xp = refs[2 + has_b], refs[3 + has_b]
        xp[0:pad, :] = jnp.zeros((pad, tc), F32)
        xp[pad:, :] = x_ref[...]

        def step(t, carry):
            base = pl.multiple_of(t * CONV_ROWS, CONV_ROWS)
            win = xp[pl.ds(base, CONV_ROWS + pad), :]
            acc = jnp.zeros((CONV_ROWS, tc), F32)
            for k in range(k_w):
                o = pad - (k_w - 1) + k
                acc = acc + w_ref[k:k + 1, :] * win[o:o + CONV_ROWS, :]
            if has_b:
                acc = acc + b_ref[...]
            y_ref[pl.ds(base, CONV_ROWS), :] = _silu(acc) if act else acc
            return carry

        lax.fori_loop(0, T // CONV_ROWS, step, 0)

    col = lambda r: pl.BlockSpec((r, tc), lambda j: (0, j))
    ins = [x, w] + ([b] if has_b else [])
    return pl.pallas_call(
        body, grid=(c // tc,), in_specs=[col(T), col(k_w)] + ([col(1)] if has_b else []), out_specs=col(T),
        out_shape=_sds((T, c)), scratch_shapes=[pltpu.VMEM((T + pad, tc), F32)], name=name,
        compiler_params=_cp("parallel"),
    )(*ins)


def _conv_bwd(name, x, w, b, act, dy):
    k_w, c = w.shape
    tc = 256 if c % 256 == 0 else LANES
    pad = _conv_pad(k_w)
    has_b = b is not None

    def body(*refs):
        x_ref, w_ref, dy_ref = refs[0], refs[1], refs[2]
        b_ref = refs[3] if has_b else None
        dx_ref, dw_ref, db_ref, xp, dp = refs[3 + has_b:]
        xp[0:pad, :] = jnp.zeros((pad, tc), F32)
        xp[pad:, :] = x_ref[...]
        dp[T:, :] = jnp.zeros((pad, tc), F32)
        dw_ref[...] = jnp.zeros_like(dw_ref)
        db_ref[...] = jnp.zeros_like(db_ref)

        def step1(t, carry):
            base = pl.multiple_of(t * CONV_ROWS, CONV_ROWS)
            d = dy_ref[pl.ds(base, CONV_ROWS), :]
            win = xp[pl.ds(base, CONV_ROWS + pad), :]
            offs = [pad - (k_w - 1) + k for k in range(k_w)]
            if act:
                acc = jnp.zeros((CONV_ROWS, tc), F32)
                for k, o in enumerate(offs):
                    acc = acc + w_ref[k:k + 1, :] * win[o:o + CONV_ROWS, :]
                if has_b:
                    acc = acc + b_ref[...]
                sg = jax.nn.sigmoid(acc)
                d = d * (sg * (1.0 + acc * (1.0 - sg)))
            dp[pl.ds(base, CONV_ROWS), :] = d
            for k, o in enumerate(offs):
                dw_ref[k:k + 1, :] += jnp.sum(d * win[o:o + CONV_ROWS, :], axis=0, keepdims=True)
            db_ref[...] += jnp.sum(d, axis=0, keepdims=True)
            return carry

        lax.fori_loop(0, T // CONV_ROWS, step1, 0)

        def step2(t, carry):
            base = pl.multiple_of(t * CONV_ROWS, CONV_ROWS)
            win = dp[pl.ds(base, CONV_ROWS + pad), :]
            acc = jnp.zeros((CONV_ROWS, tc), F32)
            for k in range(k_w):
                o = (k_w - 1) - k
                acc = acc + w_ref[k:k + 1, :] * win[o:o + CONV_ROWS, :]
            dx_ref[pl.ds(base, CONV_ROWS), :] = acc
            return carry

        lax.fori_loop(0, T // CONV_ROWS, step2, 0)

    col = lambda r: pl.BlockSpec((r, tc), lambda j: (0, j))
    ins = [x, w, dy] + ([b] if has_b else [])
    return pl.pallas_call(
        body, grid=(c // tc,), in_specs=[col(T), col(k_w), col(T)] + ([col(1)] if has_b else []),
        out_specs=[col(T), col(k_w), col(1)], out_shape=[_sds((T, c)), _sds((k_w, c)), _sds((1, c))],
        scratch_shapes=[pltpu.VMEM((T + pad, tc), F32), pltpu.VMEM((T + pad, tc), F32)], name=name,
        compiler_params=_cp("parallel"),
    )(*ins)


def _attn_consts(n):
    i = _iota2((BLOCK, 2 * BLOCK), 0)
    j = _iota2((BLOCK, 2 * BLOCK), 1)
    dist = i + BLOCK - j
    valid = (dist >= 0) & (dist < WINDOW) & ((n > 0) | (j >= BLOCK))
    return dist.astype(F32), valid


def _attn_block(q4, kk, vv, sinks, dist, valid, kv):
    outs = []
    lane = _iota2((1, HEADS), 1)
    for g in range(GROUP):
        h = kv * GROUP + g
        slope = 2.0 ** (-8.0 * (h + 1) / HEADS)
        s = _nt(q4[:, g * HDIM:(g + 1) * HDIM], kk) * (HDIM ** -0.5)
        s = jnp.where(valid, s - slope * dist, -1e30)
        sink = jnp.sum(jnp.where(lane == h, sinks, 0.0), axis=1, keepdims=True)
        m = jnp.maximum(jnp.max(s, axis=-1, keepdims=True), sink)
        e = jnp.exp(s - m)
        p = e / (jnp.sum(e, axis=-1, keepdims=True) + jnp.exp(sink - m))
        outs.append(_nn(p, vv))
    return tuple(outs)


def _attn_fwd(name, qa, ka, va, sinks):
    def body(q_ref, k_ref, v_ref, s_ref, o_ref, kp, vp):
        kp[0:BLOCK, :] = jnp.zeros((BLOCK, KV_A), F32)
        vp[0:BLOCK, :] = jnp.zeros((BLOCK, KV_A), F32)
        kp[BLOCK:, :] = k_ref[...]
        vp[BLOCK:, :] = v_ref[...]
        sinks_v = s_ref[...]

        def step(n, carry):
            r = pl.multiple_of(n * BLOCK, BLOCK)
            dist, valid = _attn_consts(n)
            k2 = kp[pl.ds(r, 2 * BLOCK), :]
            v2 = vp[pl.ds(r, 2 * BLOCK), :]
            for kv in range(KV_HEADS):
                q4 = q_ref[pl.ds(r, BLOCK), kv * GROUP * HDIM:(kv + 1) * GROUP * HDIM]
                og = _attn_block(q4, k2[:, kv * HDIM:(kv + 1) * HDIM], v2[:, kv * HDIM:(kv + 1) * HDIM], sinks_v,
                                 dist, valid, kv)
                for g in range(GROUP):
                    h = kv * GROUP + g
                    o_ref[pl.ds(r, BLOCK), h * HDIM:(h + 1) * HDIM] = og[g]
            return carry

        lax.fori_loop(0, T // BLOCK, step, 0)

    return pl.pallas_call(
        body, out_shape=_sds((T, Q_A)),
        scratch_shapes=[pltpu.VMEM((T + BLOCK, KV_A), F32), pltpu.VMEM((T + BLOCK, KV_A), F32)], name=name,
        compiler_params=pltpu.CompilerParams(vmem_limit_bytes=VMEM_LIMIT),
    )(qa, ka, va, sinks)


def _attn_bwd(name, qa, ka, va, sinks, do):
    def body(q_ref, k_ref, v_ref, s_ref, do_ref, dq_ref, dk_ref, dv_ref, ds_ref, kp, vp, dkp, dvp):
        kp[0:BLOCK, :] = jnp.zeros((BLOCK, KV_A), F32)
        vp[0:BLOCK, :] = jnp.zeros((BLOCK, KV_A), F32)
        kp[BLOCK:, :] = k_ref[...]
        vp[BLOCK:, :] = v_ref[...]
        dkp[...] = jnp.zeros_like(dkp)
        dvp[...] = jnp.zeros_like(dvp)
        ds_ref[...] = jnp.zeros_like(ds_ref)
        sinks_v = s_ref[...]

        def step(n, carry):
            r = pl.multiple_of(n * BLOCK, BLOCK)
            dist, valid = _attn_consts(n)
            k2 = kp[pl.ds(r, 2 * BLOCK), :]
            v2 = vp[pl.ds(r, 2 * BLOCK), :]
            for kv in range(KV_HEADS):
                cols = slice(kv * HDIM, (kv + 1) * HDIM)
                q4 = q_ref[pl.ds(r, BLOCK), kv * GROUP * HDIM:(kv + 1) * GROUP * HDIM]
                _, vjp = jax.vjp(lambda q, k, v, s: _attn_block(q, k, v, s, dist, valid, kv),
                                 q4, k2[:, cols], v2[:, cols], sinks_v)
                cts = tuple(do_ref[pl.ds(r, BLOCK), (kv * GROUP + g) * HDIM:(kv * GROUP + g + 1) * HDIM]
                            for g in range(GROUP))
                dq4, dkk, dvv, dsk = vjp(cts)
                dq_ref[pl.ds(r, BLOCK), kv * GROUP * HDIM:(kv + 1) * GROUP * HDIM] = dq4
                dkp[pl.ds(r, 2 * BLOCK), cols] += dkk
                dvp[pl.ds(r, 2 * BLOCK), cols] += dvv
                ds_ref[...] += dsk
            return carry

        lax.fori_loop(0, T // BLOCK, step, 0)
        dk_ref[...] = dkp[BLOCK:, :]
        dv_ref[...] = dvp[BLOCK:, :]

    pad = lambda: pltpu.VMEM((T + BLOCK, KV_A), F32)
    return pl.pallas_call(
        body, out_shape=[_sds((T, Q_A)), _sds((T, KV_A)), _sds((T, KV_A)), _sds((1, HEADS))],
        scratch_shapes=[pad(), pad(), pad(), pad()], name=name,
        compiler_params=pltpu.CompilerParams(vmem_limit_bytes=VMEM_LIMIT),
    )(qa, ka, va, sinks, do)


def _dn_consts():
    i = _iota2((CHUNK, CHUNK), 0)
    j = _iota2((CHUNK, CHUNK), 1)
    return dict(causal=i >= j, strict=i > j, eye=(i == j).astype(F32), ltri=(i >= j).astype(F32),
                ones=jnp.ones((CHUNK, CHUNK), F32), last=(_iota2((CHUNK, 1), 0) == CHUNK - 1).astype(F32))


def _l2norm(x):
    return x * lax.rsqrt(jnp.sum(x * x, axis=-1, keepdims=True) + EPS)


def _head_cols(m):
    lane = _iota2((1, HEADS), 1)
    return jnp.concatenate([jnp.sum(jnp.where(lane == h, m, 0.0), axis=1, keepdims=True)[None]
                            for h in range(HEADS)], axis=0)


def _dn_local(q3, k3, v3, braw, araw, alog, dtb, cs):
    q = _l2norm(q3) * (HDIM ** -0.5)
    k = _l2norm(k3)
    g = -jnp.exp(alog) * jax.nn.softplus(araw + dtb)
    gc_all = _nn_hi(cs["ltri"], g)
    egc_all = jnp.exp(gc_all)
    beta, gc, egc = _head_cols(jax.nn.sigmoid(braw)), _head_cols(gc_all), _head_cols(egc_all)
    a = jnp.broadcast_to(gc, (HEADS, CHUNK, CHUNK))
    diff = a - jnp.swapaxes(a, 1, 2)
    decay = jnp.where(cs["causal"], jnp.exp(jnp.where(cs["causal"], diff, 0.0)), 0.0)
    kb = k * beta
    low = jnp.where(cs["strict"], _nt(kb, k) * decay, 0.0)
    inv = cs["eye"] - low
    pw = low
    for _ in range(5):
        pw = _nn_hi(pw, pw)
        inv = inv + _nn_hi(inv, pw)
    u = _nn_hi(inv, v3 * beta)
    w = _nn_hi(inv, kb * egc)
    attn = _nt(q, k) * decay
    gc_last = jnp.sum(gc * cs["last"], axis=1, keepdims=True)
    return u, w, attn, q * egc, k * jnp.exp(gc_last - gc), egc_all


def _heads3(ref, off=0):
    return jnp.concatenate([ref[:, off + h * HDIM:off + (h + 1) * HDIM][None] for h in range(HEADS)], axis=0)


def _dn_local_fwd(name, qkv, ba, alog, dtb):
    def body(qkv_ref, ba_ref, al_ref, dt_ref, u_ref, w_ref, at_ref, qd_ref, kd_ref, eg_ref):
        bav = ba_ref[...]
        outs = _dn_local(_heads3(qkv_ref), _heads3(qkv_ref, 512), _heads3(qkv_ref, 1024), bav[:, :HEADS],
                         bav[:, HEADS:], al_ref[...], dt_ref[...], _dn_consts())
        for r, o in zip((u_ref, w_ref, at_ref, qd_ref, kd_ref), outs[:5]):
            for h in range(HEADS):
                r[:, h * HDIM:(h + 1) * HDIM] = o[h]
        eg_ref[...] = outs[5]

    row = lambda w_: pl.BlockSpec((CHUNK, w_), lambda n: (n, 0))
    return pl.pallas_call(
        body, grid=(NCHUNK,), in_specs=[row(QKV_B), row(2 * HEADS), _full((1, HEADS)), _full((1, HEADS))],
        out_specs=[row(V_B)] * 5 + [row(HEADS)], out_shape=[_sds((T, V_B))] * 5 + [_sds((T, HEADS))], name=name,
        compiler_params=_cp("parallel"),
    )(qkv, ba, alog, dtb)


def _dn_local_bwd(name, qkv, ba, alog, dtb, cts):
    def body(qkv_ref, ba_ref, al_ref, dt_ref, du_ref, dw_ref, dat_ref, dqd_ref, dkd_ref, deg_ref,
             dqkv_ref, dba_ref, dal_ref, ddt_ref):
        @pl.when(pl.program_id(0) == 0)
        def _():
            dal_ref[...] = jnp.zeros_like(dal_ref)
            ddt_ref[...] = jnp.zeros_like(ddt_ref)

        cs = _dn_consts()
        bav = ba_ref[...]
        _, vjp = jax.vjp(lambda *a: _dn_local(*a, cs), _heads3(qkv_ref), _heads3(qkv_ref, 512),
                         _heads3(qkv_ref, 1024), bav[:, :HEADS], bav[:, HEADS:], al_ref[...], dt_ref[...])
        dq, dk, dv, dbr, dar, dal, ddt = vjp((_heads3(du_ref), _heads3(dw_ref), _heads3(dat_ref), _heads3(dqd_ref),
                                              _heads3(dkd_ref), deg_ref[...]))
        for h in range(HEADS):
            dqkv_ref[:, h * HDIM:(h + 1) * HDIM] = dq[h]
            dqkv_ref[:, 512 + h * HDIM:512 + (h + 1) * HDIM] = dk[h]
            dqkv_ref[:, 1024 + h * HDIM:1024 + (h + 1) * HDIM] = dv[h]
        dba_ref[:, :HEADS] = dbr
        dba_ref[:, HEADS:] = dar
        dal_ref[...] += dal
        ddt_ref[...] += ddt

    row = lambda w_: pl.BlockSpec((CHUNK, w_), lambda n: (n, 0))
    return pl.pallas_call(
        body, grid=(NCHUNK,),
        in_specs=[row(QKV_B), row(2 * HEADS), _full((1, HEADS)), _full((1, HEADS))] + [row(V_B)] * 5 + [row(HEADS)],
        out_specs=[row(QKV_B), row(2 * HEADS), _full((1, HEADS)), _full((1, HEADS))],
        out_shape=[_sds((T, QKV_B)), _sds((T, 2 * HEADS)), _sds((1, HEADS)), _sds((1, HEADS))], name=name,
        compiler_params=_cp("arbitrary"),
    )(qkv, ba, alog, dtb, *cts)


def _dn_step(s, u, w, attn, qd, kd, egc, z, nw):
    last = (_iota2((CHUNK, 1), 0) == CHUNK - 1).astype(F32)
    gl = jnp.sum(_head_cols(egc) * last, axis=1, keepdims=True)
    v_new = u - _nn(w, s)
    o = _nn(qd, s) + _nn(attn, v_new)
    s_new = s * gl + _tn(kd, v_new)
    return s_new, _rms(o, nw) * _silu(z)


def _unheads(ref, v3):
    for h in range(HEADS):
        ref[:, h * HDIM:(h + 1) * HDIM] = v3[h]


def _dn_rec_fwd(name, u, w, attn, qd, kd, egc, z, nw):
    def body(u_ref, w_ref, at_ref, qd_ref, kd_ref, eg_ref, z_ref, nw_ref, o_ref, ss_ref, s_scr):
        @pl.when(pl.program_id(0) == 0)
        def _():
            s_scr[...] = jnp.zeros_like(s_scr)

        s = s_scr[...]
        ss_ref[...] = s
        s_new, on = _dn_step(s, _heads3(u_ref), _heads3(w_ref), _heads3(at_ref), _heads3(qd_ref), _heads3(kd_ref),
                             eg_ref[...], _heads3(z_ref), nw_ref[...])
        s_scr[...] = s_new
        _unheads(o_ref, on)

    row = lambda w_: pl.BlockSpec((CHUNK, w_), lambda n: (n, 0))
    return pl.pallas_call(
        body, grid=(NCHUNK,), in_specs=[row(V_B)] * 5 + [row(HEADS), row(V_B), _full((1, HDIM))],
        out_specs=[row(V_B), pl.BlockSpec((None, HEADS, HDIM, HDIM), lambda n: (n, 0, 0, 0))],
        out_shape=[_sds((T, V_B)), _sds((NCHUNK, HEADS, HDIM, HDIM))],
        scratch_shapes=[pltpu.VMEM((HEADS, HDIM, HDIM), F32)], name=name, compiler_params=_cp("arbitrary"),
    )(u, w, attn, qd, kd, egc, z, nw)


def _dn_rec_bwd(name, u, w, attn, qd, kd, egc, z, nw, ss, do):
    def body(u_ref, w_ref, at_ref, qd_ref, kd_ref, eg_ref, z_ref, nw_ref, ss_ref, do_ref,
             du_ref, dw_ref, dat_ref, dqd_ref, dkd_ref, deg_ref, dz_ref, dnw_ref, ds_scr):
        @pl.when(pl.program_id(0) == 0)
        def _():
            ds_scr[...] = jnp.zeros_like(ds_scr)
            dnw_ref[...] = jnp.zeros_like(dnw_ref)

        _, vjp = jax.vjp(_dn_step, ss_ref[...], _heads3(u_ref), _heads3(w_ref), _heads3(at_ref), _heads3(qd_ref),
                         _heads3(kd_ref), eg_ref[...], _heads3(z_ref), nw_ref[...])
        ds, du, dw, dat, dqd, dkd, deg, dz, dnw = vjp((ds_scr[...], _heads3(do_ref)))
        ds_scr[...] = ds
        for r, v in zip((du_ref, dw_ref, dat_ref, dqd_ref, dkd_ref, dz_ref), (du, dw, dat, dqd, dkd, dz)):
            _unheads(r, v)
        deg_ref[...] = deg
        dnw_ref[...] += dnw

    row = lambda w_: pl.BlockSpec((CHUNK, w_), lambda n: (NCHUNK - 1 - n, 0))
    return pl.pallas_call(
        body, grid=(NCHUNK,),
        in_specs=[row(V_B)] * 5 + [row(HEADS), row(V_B), _full((1, HDIM)),
                                   pl.BlockSpec((None, HEADS, HDIM, HDIM), lambda n: (NCHUNK - 1 - n, 0, 0, 0)),
                                   row(V_B)],
        out_specs=[row(V_B)] * 5 + [row(HEADS), row(V_B), _full((1, HDIM))],
        out_shape=[_sds((T, V_B))] * 5 + [_sds((T, HEADS)), _sds((T, V_B)), _sds((1, HDIM))],
        scratch_shapes=[pltpu.VMEM((HEADS, HDIM, HDIM), F32)], name=name, compiler_params=_cp("arbitrary"),
    )(u, w, attn, qd, kd, egc, z, nw, ss, do)


def _final(name, x, fw, target, tm=512):
    def body(x_ref, fw_ref, t_ref, l_ref, dx_ref, dfw_ref):
        @pl.when(pl.program_id(0) == 0)
        def _():
            l_ref[...] = jnp.zeros_like(l_ref)
            dfw_ref[...] = jnp.zeros_like(dfw_ref)

        tv = t_ref[...]

        def f(xv, fwv):
            err = _rms(xv, fwv) - tv
            per_tok = jnp.mean(err * err, axis=-1, keepdims=True)
            return 0.5 * jnp.sum(per_tok, axis=0, keepdims=True)

        loss, vjp = jax.vjp(f, x_ref[...], fw_ref[...])
        dx, dfw = vjp(jnp.ones((1, 1), F32))
        l_ref[...] += loss
        dx_ref[...] = dx
        dfw_ref[...] += dfw

    tok = pl.BlockSpec((tm, D), lambda i: (i, 0))
    return pl.pallas_call(
        body, grid=(T // tm,), in_specs=[tok, _full((1, D)), tok], out_specs=[_full((1, 1)), tok, _full((1, D))],
        out_shape=[_sds((1, 1)), _sds((T, D)), _sds((1, D))], name=name, compiler_params=_cp("arbitrary"),
    )(x, fw, target)


def _m1_pre(tv, sv):
    return [_rms(tv[0], sv[0])]


def _m1_post(ys, tv, sv):
    return (ys[0],)


def _m1_post_split(ys, tv, sv):
    return tuple(ys[0][:, a:b] for a, b in zip(IN_SPLITS[:-1], IN_SPLITS[1:]))


def _m5_pre(tv, sv):
    return [tv[1], tv[2]]


def _m5_post(ys, tv, sv):
    return (tv[0] + ys[0] + ys[1],)


def _c1_pre(tv, sv):
    return [_rms(tv[0], sv[0])]


def _c1_post(ys, tv, sv):
    return ((ys[0] + sv[1]) * jax.nn.sigmoid(ys[1] + sv[2]),)


def _c3_pre(tv, sv):
    return [_silu(_layernorm(tv[0], sv[0], sv[1]))]


def _c3_post(ys, tv, sv):
    return (tv[1] + ys[0] + sv[2],)


def _row(v):
    return v.reshape(1, -1)


def _mixer_fwd(tag, x, p):
    parts = _blk_fwd(f"m1_fwd_{tag}", _m1_pre, [0], _m1_post_split, [x], [p["nw"]], [p["w_in"]],
                     [(b - a, F32) for a, b in zip(IN_SPLITS[:-1], IN_SPLITS[1:])])
    qa, ka, va, qkvb, z, ba = parts
    att = _attn_fwd(f"attn_fwd_{tag}", qa, ka, va, p["sinks"])
    qkvc = _conv_fwd(f"dnconv_fwd_{tag}", qkvb, p["dn_conv_w"], None, True)
    loc = _dn_local_fwd(f"dnloc_fwd_{tag}", qkvc, ba, p["a_log"], p["dt_bias"])
    og, ss = _dn_rec_fwd(f"dnrec_fwd_{tag}", *loc, z, p["dn_norm_w"])
    (out,) = _blk_fwd(f"m5_fwd_{tag}", _m5_pre, [0, 1], _m5_post, [x, att, og], [], [p["wo_a"], p["wo_b"]],
                      [(D, F32)])
    return out, dict(x=x, qa=qa, ka=ka, va=va, qkvb=qkvb, z=z, ba=ba, att=att, qkvc=qkvc, loc=loc, og=og, ss=ss)


def _mixer_bwd(tag, dy, p, s):
    (dxa, datt, dog), _, (dwo_a, dwo_b) = _blk_bwd(f"m5_bwd_{tag}", _m5_pre, [0, 1], _m5_post,
                                                   [s["x"], s["att"], s["og"]], [], [p["wo_a"], p["wo_b"]], [[dy]])
    rec = _dn_rec_bwd(f"dnrec_bwd_{tag}", *s["loc"], s["z"], p["dn_norm_w"], s["ss"], dog)
    dz, dnw_dn = rec[6], rec[7]
    dqkvc, dba, dalog, ddtb = _dn_local_bwd(f"dnloc_bwd_{tag}", s["qkvc"], s["ba"], p["a_log"], p["dt_bias"],
                                            rec[:6])
    dqkvb, dconvw, _ = _conv_bwd(f"dnconv_bwd_{tag}", s["qkvb"], p["dn_conv_w"], None, True, dqkvc)
    dqa, dka, dva, dsinks = _attn_bwd(f"attn_bwd_{tag}", s["qa"], s["ka"], s["va"], p["sinks"], datt)
    (dx,), (dnw,), (dw_in,) = _blk_bwd(f"m1_bwd_{tag}", _m1_pre, [0], _m1_post, [s["x"]], [p["nw"]], [p["w_in"]],
                                       [[dqa, dka, dva, dqkvb, dz, dba]], res=dxa)
    return dx, dict(nw=dnw, w_in=dw_in, wo_a=dwo_a, wo_b=dwo_b, dn_conv_w=dconvw, sinks=dsinks, a_log=dalog,
                    dt_bias=ddtb, dn_norm_w=dnw_dn)


def _conformer_fwd(tag, x, p):
    (glu,) = _blk_fwd(f"c1_fwd_{tag}", _c1_pre, [0, 0], _c1_post, [x], [p["nw"], p["b1a"], p["b1b"]],
                      [p["w1a"], p["w1b"]], [(D, F32)])
    cc = _conv_fwd(f"dwconv_fwd_{tag}", glu, p["w_dw"], p["b_dw"], False)
    (out,) = _blk_fwd(f"c3_fwd_{tag}", _c3_pre, [0], _c3_post, [cc, x], [p["ln_w"], p["ln_b"], p["b2"]], [p["w2"]],
                      [(D, F32)])
    return out, dict(x=x, glu=glu, cc=cc)


def _conformer_bwd(tag, dy, p, s):
    (dcc, dxa), (dlnw, dlnb, db2), (dw2,) = _blk_bwd(f"c3_bwd_{tag}", _c3_pre, [0], _c3_post, [s["cc"], s["x"]],
                                                     [p["ln_w"], p["ln_b"], p["b2"]], [p["w2"]], [[dy]])
    dglu, dwdw, dbdw = _conv_bwd(f"dwconv_bwd_{tag}", s["glu"], p["w_dw"], p["b_dw"], False, dcc)
    (dx,), (dnw, db1a, db1b), (dw1a, dw1b) = _blk_bwd(f"c1_bwd_{tag}", _c1_pre, [0, 0], _c1_post, [s["x"]],
                                                      [p["nw"], p["b1a"], p["b1b"]], [p["w1a"], p["w1b"]], [[dglu]],
                                                      res=dxa)
    return dx, dict(nw=dnw, b1a=db1a, b1b=db1b, w1a=dw1a, w1b=dw1b, w_dw=dwdw, b_dw=dbdw, ln_w=dlnw, ln_b=dlnb,
                    b2=db2, w2=dw2)


def _layer_fwd(l, x, nw, ffn, p):
    x1 = _ffn_fwd(f"ffn_fwd_{l}a", x, _row(nw[0]), *ffn, 0)
    p = dict(p, nw=_row(nw[1]))
    x2, sv = (_mixer_fwd if l % 2 == 0 else _conformer_fwd)(str(l), x1, p)
    return _ffn_fwd(f"ffn_fwd_{l}b", x2, _row(nw[2]), *ffn, 1), (x, x2, p, sv)


def _layer_bwd(l, dx, nw, ffn, saved):
    x0, x2, p, sv = saved
    dx, dn2, *dffn = _ffn_bwd(f"ffn_bwd_{l}b", x2, _row(nw[2]), *ffn, 1, dx)
    dx, dmix = (_mixer_bwd if l % 2 == 0 else _conformer_bwd)(str(l), dx, p, sv)
    dx, dn0, *dffn = _ffn_bwd(f"ffn_bwd_{l}a", x0, _row(nw[0]), *ffn, 0, dx, dffn)
    return dx, jnp.concatenate([dn0, dmix.pop("nw"), dn2], axis=0), dffn, dmix


def _place():
    x, y, c = lax.axis_index("x"), lax.axis_index("y"), lax.axis_index("c")
    chips = [(1 - x, y), (x, 1 - y), (1 - x, 1 - y)]
    return x, y, c, 2 * x + y, chips, [2 * px + py for px, py in chips]


def _handshake(peers):
    barrier = pltpu.get_barrier_semaphore()
    for p in peers:
        pl.semaphore_signal(barrier, inc=1, device_id=p, device_id_type=MESH)
    pl.semaphore_wait(barrier, len(peers))


def _chip_peers():
    x, y, c, _, chips, _ = _place()
    return [(*chip, c) for chip in chips] + [(x, y, 1 - c)]


def _gather_copies(ins, outs, nb, send, recv, fsend, frecv, lsem):
    n_in = len(ins)
    x, y, c, me, chips, cidx = _place()
    sib = (x, y, 1 - c)
    local = [pltpu.make_async_copy(ins[a], outs[a].at[me], lsem.at[a]) for a in range(n_in)]
    for cp in local:
        cp.start()

    def ici(a, j):
        k = a * 3 + j
        src, dst = (ins[a], outs[a].at[me]) if a >= nb else (ins[a].at[pl.ds(c, 1)], outs[a].at[me, pl.ds(c, 1)])
        return pltpu.make_async_remote_copy(src, dst, send.at[k], recv.at[k], device_id=(*chips[j], c),
                                            device_id_type=MESH)

    def landed(a, j):
        k = a * 3 + j
        dst = outs[a].at[cidx[j]] if a >= nb else outs[a].at[cidx[j], pl.ds(c, 1)]
        return pltpu.make_async_remote_copy(dst, dst, send.at[k], recv.at[k], device_id=(*chips[j], c),
                                            device_id_type=MESH)

    def passed(a, j, who):
        k = a * 3 + j
        part = outs[a].at[cidx[j], pl.ds(who, 1)]
        return pltpu.make_async_remote_copy(part, part, fsend.at[k], frecv.at[k], device_id=sib, device_id_type=MESH)

    sends = [ici(a, j) for a in range(n_in) for j in range(3)]
    for cp in sends:
        cp.start()
    for a in range(nb):
        for j in range(3):
            landed(a, j).wait_recv()
            cp = passed(a, j, c)
            cp.start()
            sends.append(cp)
    for a in range(nb, n_in):
        for j in range(3):
            landed(a, j).wait_recv()
    for a in range(nb):
        for j in range(3):
            passed(a, j, 1 - c).wait_recv()
    for cp in sends:
        cp.wait_send()
    for cp in local:
        cp.wait()


def _gather_sems(n_in, nb):
    dma = pltpu.SemaphoreType.DMA
    return [dma((3 * n_in,)), dma((3 * n_in,)), dma((3 * nb,)), dma((3 * nb,)), dma((n_in,))]


def _gather_first(halved, whole, later):
    nb, n_in, n_l = len(halved), len(halved) + len(whole), len(later)

    def body(*refs):
        ins, outs = refs[:n_in], refs[n_in + n_l:2 * n_in + n_l]
        _gather_copies(ins, outs, nb, *refs[2 * (n_in + n_l):])

    arrs = list(halved) + list(whole)
    outs = pl.pallas_call(
        body, in_specs=[ANY] * (n_in + n_l), out_specs=[ANY] * (n_in + n_l),
        out_shape=[_sds((NCHIP,) + a.shape, a.dtype) for a in arrs] + [_sds(a.shape, a.dtype) for a in later],
        input_output_aliases={n_in + k: n_in + k for k in range(n_l)},
        scratch_shapes=_gather_sems(n_in, nb), name="gather_layer0",
    )(*arrs, *later)
    return outs[:n_in], outs[n_in:]


def _gather_async(name, halved):
    nb = len(halved)
    hbm = pltpu.MemorySpace.HBM
    ins = [jax.new_ref(a, memory_space=hbm) for a in halved]
    outs = [jax.empty_ref(_sds((NCHIP,) + a.shape, a.dtype), memory_space=hbm) for a in halved]

    @pl.kernel(mesh=plsc.ScalarSubcoreMesh(axis_name="seq", num_cores=1), name=name,
               scratch_types=tuple(_gather_sems(nb, nb)), compiler_params=pltpu.CompilerParams(collective_id=2))
    def launch(send, recv, fsend, frecv, lsem):
        _handshake(_chip_peers())
        _gather_copies(ins, outs, nb, send, recv, fsend, frecv, lsem)

    launch()
    return [o[...] for o in outs]


def _swap_halves(name, grads):
    n = len(grads)
    hbm = pltpu.MemorySpace.HBM
    ins = [jax.new_ref(g, memory_space=hbm) for g in grads]
    outs = [jax.empty_ref(_sds((NCHIP, g.shape[1] // 2) + g.shape[2:], g.dtype), memory_space=hbm) for g in grads]

    @pl.kernel(mesh=plsc.ScalarSubcoreMesh(axis_name="seq", num_cores=1), name=name,
               scratch_types=(pltpu.SemaphoreType.DMA((n,)), pltpu.SemaphoreType.DMA((n,))),
               compiler_params=pltpu.CompilerParams(collective_id=1))
    def launch(send, recv):
        x, y, c, _, _, _ = _place()
        sib = (x, y, 1 - c)
        _handshake([sib])
        cps = []
        for a in range(n):
            h = grads[a].shape[1] // 2
            cps.append(pltpu.make_async_remote_copy(ins[a].at[:, pl.ds((1 - c) * h, h)], outs[a], send.at[a],
                                                    recv.at[a], device_id=sib, device_id_type=MESH))
        for cp in cps:
            cp.start()
        for cp in cps:
            cp.wait()

    launch()
    return [o[...] for o in outs]


def _row_tile(r, cap=256):
    return max(t for t in range(8, cap + 1, 8) if r % t == 0)


def _add_half(name, g, r, c_arr):
    _, l, rows, cols = g.shape
    h = l // 2
    tr = _row_tile(rows)

    def body(c_ref, g_ref, r_ref, o_ref):
        o_ref[...] = (g_ref[...].astype(F32) + r_ref[...].astype(F32)).astype(BF16)

    blk = (None, None, tr, cols)
    return pl.pallas_call(
        body,
        grid_spec=pltpu.PrefetchScalarGridSpec(
            num_scalar_prefetch=1, grid=(NCHIP, h, rows // tr),
            in_specs=[pl.BlockSpec(blk, lambda j, i, t, c_ref: (j, c_ref[0] * h + i, t, 0)),
                      pl.BlockSpec(blk, lambda j, i, t, c_ref: (j, i, t, 0))],
            out_specs=pl.BlockSpec(blk, lambda j, i, t, c_ref: (j, i, t, 0))),
        out_shape=_sds((NCHIP, h, rows, cols), BF16), name=name,
        compiler_params=_cp("parallel", "parallel", "parallel"),
    )(c_arr, g, r)


def _scatter_async(name, parts, sums, where):
    nb = len(parts)
    ins = [jax.new_ref(p, memory_space=pltpu.MemorySpace.HBM) for p in parts]
    dma = pltpu.SemaphoreType.DMA

    @pl.kernel(mesh=plsc.ScalarSubcoreMesh(axis_name="seq", num_cores=1), name=name,
               scratch_types=(dma((3 * nb,)), dma((3 * nb,)), dma((4 * nb,)), dma((4 * nb,)), dma((nb,))),
               compiler_params=pltpu.CompilerParams(collective_id=3))
    def launch(send, recv, fsend, frecv, lsem):
        _handshake(_chip_peers())
        x, y, c, me, chips, cidx = _place()
        sib = (x, y, 1 - c)

        def slot(a, half, chip):
            return sums[a].at[half, chip, pl.ds(where[a], 1)]

        local = [pltpu.make_async_copy(ins[a].at[me], slot(a, c, me), lsem.at[a]) for a in range(nb)]
        for cp in local:
            cp.start()

        def ici(a, j):
            return pltpu.make_async_remote_copy(ins[a].at[cidx[j]], slot(a, c, me), send.at[a * 3 + j],
                                                recv.at[a * 3 + j], device_id=(*chips[j], c), device_id_type=MESH)

        def landed(a, j):
            dst = slot(a, c, cidx[j])
            return pltpu.make_async_remote_copy(dst, dst, send.at[a * 3 + j], recv.at[a * 3 + j],
                                                device_id=(*chips[j], c), device_id_type=MESH)

        def passed(a, j, who):
            dst = slot(a, who, me if j == 3 else cidx[j])
            src = ins[a].at[me] if j == 3 else dst
            return pltpu.make_async_remote_copy(src, dst, fsend.at[a * 4 + j], frecv.at[a * 4 + j], device_id=sib,
                                                device_id_type=MESH)

        sends = [ici(a, j) for a in range(nb) for j in range(3)] + [passed(a, 3, c) for a in range(nb)]
        for cp in sends:
            cp.start()
        for a in range(nb):
            for j in range(3):
                landed(a, j).wait_recv()
                cp = passed(a, j, c)
                cp.start()
                sends.append(cp)
        for a in range(nb):
            for j in range(4):
                passed(a, j, 1 - c).wait_recv()
        for cp in sends:
            cp.wait_send()
        for cp in local:
            cp.wait()

    launch()


def _exchange_small(small, rep):
    def body(small_in, rep_in, small_out, rep_out, lsem, ssend, srecv):
        x, y, c, me, _, _ = _place()
        dev = 4 * x + 2 * y + c
        local = [pltpu.make_async_copy(small_in.at[me], small_out.at[dev], lsem.at[0]),
                 pltpu.make_async_copy(rep_in, rep_out.at[dev], lsem.at[1])]
        for cp in local:
            cp.start()

        def peer(r):
            return (1 - x if r & 4 else x), (1 - y if r & 2 else y), (1 - c if r & 1 else c)

        def tiny(r, which):
            px, py, pc = peer(r)
            k = (r - 1) * 2 + which
            if which == 0:
                return pltpu.make_async_remote_copy(small_in.at[2 * px + py], small_out.at[dev], ssend.at[k],
                                                    srecv.at[k], device_id=(px, py, pc), device_id_type=MESH)
            return pltpu.make_async_remote_copy(rep_in, rep_out.at[dev], ssend.at[k], srecv.at[k],
                                                device_id=(px, py, pc), device_id_type=MESH)

        def tiny_landed(r, which):
            px, py, pc = peer(r)
            k = (r - 1) * 2 + which
            dst = (small_out if which == 0 else rep_out).at[4 * px + 2 * py + pc]
            return pltpu.make_async_remote_copy(dst, dst, ssend.at[k], srecv.at[k], device_id=(px, py, pc),
                                                device_id_type=MESH)

        sends = [tiny(r, w) for r in range(1, NDEV) for w in range(2)]
        for cp in sends:
            cp.start()
        for r in range(1, NDEV):
            for w in range(2):
                tiny_landed(r, w).wait_recv()
        for cp in sends:
            cp.wait_send()
        for cp in local:
            cp.wait()

    dma = pltpu.SemaphoreType.DMA
    return pl.pallas_call(
        body, in_specs=[ANY] * 2, out_specs=[ANY] * 2,
        out_shape=[_sds((NDEV,) + small.shape[1:], F32), _sds((NDEV,) + rep.shape, F32)],
        scratch_shapes=[dma((2,)), dma((2 * (NDEV - 1),)), dma((2 * (NDEV - 1),))], name="exchange_small_grads",
    )(small, rep)


def _adamw_math(w, g, m, v):
    m = B1 * m + (1.0 - B1) * g
    v = B2 * v + (1.0 - B2) * (g * g)
    m_hat = m / (1.0 - B1 ** STEP)
    v_hat = v / (1.0 - B2 ** STEP)
    return -LR * (m_hat / (jnp.sqrt(v_hat) + AEPS) + WD * w), m, v


def _adamw_big(name, w, m, v, parts, row0=0):
    n, _, rows, cols = w.shape
    tr = _row_tile(rows)
    t0 = row0 // tr

    def body(w_ref, m_ref, v_ref, p_ref, g_ref, d_ref, nm_ref, nv_ref):
        g = p_ref[0].astype(F32)
        for q in range(1, NCHIP):
            g = g + p_ref[q].astype(F32)
        d, nm, nv = _adamw_math(w_ref[...], g, m_ref[...], v_ref[...])
        g_ref[...], d_ref[...], nm_ref[...], nv_ref[...] = g, d, nm, nv

    spec = pl.BlockSpec((None, None, tr, cols), lambda i, p, t: (i, p, t, 0))
    return pl.pallas_call(
        body, grid=(n, 2, rows // tr),
        in_specs=[spec, spec, spec,
                  pl.BlockSpec((None, NCHIP, None, tr, cols), lambda i, p, t: (p, 0, i, t0 + t, 0))],
        out_specs=[spec] * 4, out_shape=[_sds(w.shape)] * 4, name=name,
        compiler_params=_cp("parallel", "parallel", "parallel"),
    )(w, m, v, parts)


def _adamw_small(name, w, m, v, parts):
    def body(w_ref, m_ref, v_ref, p_ref, g_ref, d_ref, nm_ref, nv_ref):
        g = p_ref[0]
        for q in range(1, NDEV):
            g = g + p_ref[q]
        d, nm, nv = _adamw_math(w_ref[...], g, m_ref[...], v_ref[...])
        g_ref[...], d_ref[...], nm_ref[...], nv_ref[...] = g, d, nm, nv

    return pl.pallas_call(body, out_shape=[_sds(w.shape)] * 4, name=name)(w, m, v, parts)


def _pack(arrs, rows):
    flat = jnp.concatenate([a.reshape(-1) for a in arrs])
    return jnp.pad(flat, (0, rows * LANES - flat.shape[0])).reshape(rows, LANES)


def _unpack(packed, shapes):
    flat, out, o = packed.reshape(-1), [], 0
    for s in shapes:
        n = 1
        for d in s:
            n *= d
        out.append(flat[o:o + n].reshape(s))
        o += n
    return out


SMALL_ROWS, REP_ROWS = 200, 16


def kernel(x, norm_w, ffn_w_gate, ffn_w_up, ffn_w_down, mix_w_in, dn_conv_w, attn_sinks, dn_a_log, dn_dt_bias, dn_norm_w, mix_w_out, conv_w_pw1, conv_b_pw1, conv_w_dw, conv_b_dw, conv_ln_w, conv_ln_b, conv_w_pw2, conv_b_pw2, final_norm_w, loss_target, m_norm_w, m_ffn_w_gate, m_ffn_w_up, m_ffn_w_down, m_mix_w_in, m_dn_conv_w, m_attn_sinks, m_dn_a_log, m_dn_dt_bias, m_dn_norm_w, m_mix_w_out, m_conv_w_pw1, m_conv_b_pw1, m_conv_w_dw, m_conv_b_dw, m_conv_ln_w, m_conv_ln_b, m_conv_w_pw2, m_conv_b_pw2, m_final_norm_w, v_norm_w, v_ffn_w_gate, v_ffn_w_up, v_ffn_w_down, v_mix_w_in, v_dn_conv_w, v_attn_sinks, v_dn_a_log, v_dn_dt_bias, v_dn_norm_w, v_mix_w_out, v_conv_w_pw1, v_conv_b_pw1, v_conv_w_dw, v_conv_b_dw, v_conv_ln_w, v_conv_ln_b, v_conv_w_pw2, v_conv_b_pw2, v_final_norm_w):
    small_names = ["norm_w", "dn_conv_w", "conv_b_pw1", "conv_w_dw", "conv_b_dw", "conv_ln_w", "conv_ln_b",
                   "conv_b_pw2"]
    rep_names = ["attn_sinks", "dn_a_log", "dn_dt_bias", "dn_norm_w", "final_norm_w"]
    w = dict(norm_w=norm_w, ffn_w_gate=ffn_w_gate, ffn_w_up=ffn_w_up, ffn_w_down=ffn_w_down, mix_w_in=mix_w_in, dn_conv_w=dn_conv_w, attn_sinks=attn_sinks, dn_a_log=dn_a_log, dn_dt_bias=dn_dt_bias, dn_norm_w=dn_norm_w, mix_w_out=mix_w_out, conv_w_pw1=conv_w_pw1, conv_b_pw1=conv_b_pw1, conv_w_dw=conv_w_dw, conv_b_dw=conv_b_dw, conv_ln_w=conv_ln_w, conv_ln_b=conv_ln_b, conv_w_pw2=conv_w_pw2, conv_b_pw2=conv_b_pw2, final_norm_w=final_norm_w)
    m = dict(norm_w=m_norm_w, ffn_w_gate=m_ffn_w_gate, ffn_w_up=m_ffn_w_up, ffn_w_down=m_ffn_w_down, mix_w_in=m_mix_w_in, dn_conv_w=m_dn_conv_w, attn_sinks=m_attn_sinks, dn_a_log=m_dn_a_log, dn_dt_bias=m_dn_dt_bias, dn_norm_w=m_dn_norm_w, mix_w_out=m_mix_w_out, conv_w_pw1=m_conv_w_pw1, conv_b_pw1=m_conv_b_pw1, conv_w_dw=m_conv_w_dw, conv_b_dw=m_conv_b_dw, conv_ln_w=m_conv_ln_w, conv_ln_b=m_conv_ln_b, conv_w_pw2=m_conv_w_pw2, conv_b_pw2=m_conv_b_pw2, final_norm_w=m_final_norm_w)
    v = dict(norm_w=v_norm_w, ffn_w_gate=v_ffn_w_gate, ffn_w_up=v_ffn_w_up, ffn_w_down=v_ffn_w_down, mix_w_in=v_mix_w_in, dn_conv_w=v_dn_conv_w, attn_sinks=v_attn_sinks, dn_a_log=v_dn_a_log, dn_dt_bias=v_dn_dt_bias, dn_norm_w=v_dn_norm_w, mix_w_out=v_mix_w_out, conv_w_pw1=v_conv_w_pw1, conv_b_pw1=v_conv_b_pw1, conv_w_dw=v_conv_w_dw, conv_b_dw=v_conv_b_dw, conv_ln_w=v_conv_ln_w, conv_ln_b=v_conv_ln_b, conv_w_pw2=v_conv_w_pw2, conv_b_pw2=v_conv_b_pw2, final_norm_w=v_final_norm_w)
    order = ["norm_w", "ffn_w_gate", "ffn_w_up", "ffn_w_down", "mix_w_in", "dn_conv_w", "attn_sinks", "dn_a_log",
             "dn_dt_bias", "dn_norm_w", "mix_w_out", "conv_w_pw1", "conv_b_pw1", "conv_w_dw", "conv_b_dw",
             "conv_ln_w", "conv_ln_b", "conv_w_pw2", "conv_b_pw2", "final_norm_w"]

    small_shapes = [w[n].shape for n in small_names]
    rep_shapes = [w[n].shape for n in rep_names]

    def halves(a):
        return a.reshape(a.shape[:-2] + (2, a.shape[-2] // 2, a.shape[-1]))

    def layer_shards(l):
        mix_in, mix_out = (mix_w_in, mix_w_out) if l % 2 == 0 else (conv_w_pw1, conv_w_pw2)
        return [t.astype(BF16) for t in (jnp.concatenate([ffn_w_gate[l], ffn_w_up[l]], axis=1), ffn_w_down[l],
                                         halves(mix_in[l // 2]), halves(mix_out[l // 2]))]

    shards = [layer_shards(l) for l in range(DEPTH)]
    first, later = _gather_first(shards[0], [_pack([w[n] for n in small_names], SMALL_ROWS)],
                                 [t for l in range(1, DEPTH) for t in shards[l]])
    gathered = [first[:4]] + [_gather_async(f"gather_layer{l}", later[4 * (l - 1):4 * l]) for l in range(1, DEPTH)]
    per_chip = [_unpack(first[4][q], small_shapes) for q in range(NCHIP)]
    sm = {n: jnp.concatenate([per_chip[q][i] for q in range(NCHIP)], axis=-1) for i, n in enumerate(small_names)}

    def mixer_params(l):
        e = l // 2
        w_a = gathered[l][2].transpose(1, 2, 0, 3).reshape(D, -1)
        w_b = gathered[l][3].reshape(D, D)
        if l % 2 == 0:
            return dict(w_in=w_a, dn_conv_w=sm["dn_conv_w"][e], sinks=_row(attn_sinks[e]), a_log=_row(dn_a_log[e]),
                        dt_bias=_row(dn_dt_bias[e]), dn_norm_w=_row(dn_norm_w[e]), wo_a=w_b[:Q_A], wo_b=w_b[Q_A:])
        return dict(b1a=_row(sm["conv_b_pw1"][e, :D]), b1b=_row(sm["conv_b_pw1"][e, D:]), w1a=w_a[:, :D],
                    w1b=w_a[:, D:], w_dw=sm["conv_w_dw"][e], b_dw=_row(sm["conv_b_dw"][e]),
                    ln_w=_row(sm["conv_ln_w"][e]), ln_b=_row(sm["conv_ln_b"][e]), b2=_row(sm["conv_b_pw2"][e]),
                    w2=w_b)

    xs, saved = x[0], []
    for l in range(DEPTH):
        xs, sv = _layer_fwd(l, xs, sm["norm_w"][l], gathered[l][:2], mixer_params(l))
        saved.append(sv)
    loss, dx, dfw = _final("final", xs, _row(final_norm_w), loss_target[0])

    hbm = pltpu.MemorySpace.HBM
    sum_shapes = dict(gu=(DEPTH, 2 * D, FS), down=(DEPTH, FS, D), w_in=(2, D // 2, IN_COLS // NCHIP),
                      w_out=(2, D // 8, D), pw1=(2, D // 2, D // 2), pw2=(2, D // 8, D))
    sums = {k: jax.empty_ref(_sds((2, NCHIP) + s, BF16), memory_space=hbm) for k, s in sum_shapes.items()}
    c_arr = lax.axis_index("c").astype(jnp.int32).reshape(1)
    dnorm, gmix = [None] * DEPTH, [None] * DEPTH
    for l in reversed(range(DEPTH)):
        dx, dnorm[l], dffn, gmix[l] = _layer_bwd(l, dx, sm["norm_w"][l], gathered[l][:2], saved[l])
        if l % 2 == 0:
            g_a, g_b = gmix[l]["w_in"], jnp.concatenate([gmix[l]["wo_a"], gmix[l]["wo_b"]], axis=0)
        else:
            g_a, g_b = jnp.concatenate([gmix[l]["w1a"], gmix[l]["w1b"]], axis=1), gmix[l]["w2"]
        g_a = g_a.reshape(2, D // 2, NCHIP, -1).transpose(2, 0, 1, 3).astype(BF16)
        g_b = g_b.reshape(NCHIP, 2, D // 8, D).astype(BF16)
        grads = [dffn[0], dffn[1], g_a, g_b]
        other = _swap_halves(f"swap_grads_{l}", grads)
        parts = [_add_half(f"add_half_{l}_{k}", gg, rr, c_arr) for k, (gg, rr) in enumerate(zip(grads, other))]
        keys = ("gu", "down", "w_in", "w_out") if l % 2 == 0 else ("gu", "down", "pw1", "pw2")
        _scatter_async(f"scatter_grads_{l}", parts, [sums[k] for k in keys], [l, l, l // 2, l // 2])
    gm, gc = [gmix[0], gmix[2]], [gmix[1], gmix[3]]
    small_g = dict(
        norm_w=jnp.stack(dnorm), dn_conv_w=jnp.stack([gm[e]["dn_conv_w"] for e in range(2)]),
        conv_b_pw1=jnp.stack([jnp.concatenate([gc[e]["b1a"], gc[e]["b1b"]], axis=1)[0] for e in range(2)]),
        conv_w_dw=jnp.stack([gc[e]["w_dw"] for e in range(2)]),
        conv_b_dw=jnp.stack([gc[e]["b_dw"][0] for e in range(2)]),
        conv_ln_w=jnp.stack([gc[e]["ln_w"][0] for e in range(2)]),
        conv_ln_b=jnp.stack([gc[e]["ln_b"][0] for e in range(2)]),
        conv_b_pw2=jnp.stack([gc[e]["b2"][0] for e in range(2)]))
    small_by_chip = jnp.stack([_pack([jnp.split(small_g[n], NCHIP, axis=-1)[q] for n in small_names], SMALL_ROWS)
                               for q in range(NCHIP)])
    rep_g = _pack([jnp.stack([gm[e]["sinks"][0] for e in range(2)]), jnp.stack([gm[e]["a_log"][0] for e in range(2)]),
                   jnp.stack([gm[e]["dt_bias"][0] for e in range(2)]),
                   jnp.stack([gm[e]["dn_norm_w"][0] for e in range(2)]), dfw[0]], REP_ROWS)
    small_sum, rep_sum = _exchange_small(small_by_chip, rep_g)

    res = {}
    partial_sums = {k: r[...] for k, r in sums.items()}
    for n, key, row0 in (("ffn_w_gate", "gu", 0), ("ffn_w_up", "gu", D), ("ffn_w_down", "down", 0),
                         ("mix_w_in", "w_in", 0), ("mix_w_out", "w_out", 0), ("conv_w_pw1", "pw1", 0),
                         ("conv_w_pw2", "pw2", 0)):
        view = (lambda a: a) if w[n].ndim == 4 else halves
        outs = _adamw_big(f"adamw_{n}", view(w[n]), view(m[n]), view(v[n]), partial_sums[key], row0)
        res[n] = [o.reshape(w[n].shape) for o in outs]
    outs = _adamw_small("adamw_small", *[_pack([d[n] for n in small_names], SMALL_ROWS) for d in (w, m, v)],
                        small_sum)
    for i, n in enumerate(small_names):
        res[n] = [_unpack(o, small_shapes)[i] for o in outs]
    outs = _adamw_small("adamw_replicated", *[_pack([d[n] for n in rep_names], REP_ROWS) for d in (w, m, v)],
                        rep_sum)
    for i, n in enumerate(rep_names):
        res[n] = [_unpack(o, rep_shapes)[i] for o in outs]

    total = lax.psum(loss[0, 0], ("x", "y", "c"))
    return (total, dx[None], *[res[n][0] for n in order], *[res[n][1] for n in order],
            *[res[n][2] for n in order], *[res[n][3] for n in order])
```

```python
import jax
import jax.numpy as jnp
from jax import lax
from jax.experimental import pallas as pl
from jax.experimental.pallas import tpu as pltpu
from jax.experimental.pallas import tpu_sc as plsc

F32, BF16 = jnp.float32, jnp.bfloat16
MESH = pl.DeviceIdType.MESH
ANY = pl.BlockSpec(memory_space=pl.ANY)

T, D, F = 2048, 1024, 2816
DEPTH = 4
EPS = 1e-6
HEADS, HDIM, KV_HEADS, GROUP = 8, 64, 2, 4
WINDOW = BLOCK = 128
CHUNK = 64
NCHUNK = T // CHUNK
DN_CONV, CONV_WIDTH = 4, 31
Q_A, KV_A, QKV_B, V_B = 512, 128, 1536, 512
IN_COLS = 2832
IN_SPLITS = (0, 512, 640, 768, 2304, 2816, 2832)
NCHIP, NDEV = 4, 8
FS = F // NCHIP
LR, B1, B2, AEPS, WD, STEP = 0.001, 0.9, 0.999, 1e-08, 0.01, 10
V7X_VMEM_BYTES = 64 * 1024 * 1024
VMEM_LIMIT = V7X_VMEM_BYTES * 7 // 8
LANES = 128


def _cp(*sem):
    return pltpu.CompilerParams(dimension_semantics=sem, vmem_limit_bytes=VMEM_LIMIT)


def _sds(shape, dtype=F32):
    return jax.ShapeDtypeStruct(tuple(shape), dtype)


def _full(shape):
    nd = len(shape)
    return pl.BlockSpec(tuple(shape), lambda *_: (0,) * nd)


def _split_bf16(a):
    hi = a.astype(BF16)
    return hi, (a - hi.astype(F32)).astype(BF16)


def _dg(a, b, ca, cb, hi=False):
    if a.ndim == 3 and b.ndim == 3:
        dims = (((ca + 1,), (cb + 1,)), ((0,), (0,)))
    else:
        dims = (((ca,), (cb,)), ((), ()))
    dot = lambda p, q: lax.dot_general(p, q, dims, preferred_element_type=F32)
    if hi:
        a_hi, a_lo = _split_bf16(a.astype(F32))
        b_hi, b_lo = _split_bf16(b.astype(F32))
        return dot(a_hi, b_hi) + (dot(a_hi, b_lo) + dot(a_lo, b_hi))
    return dot(a.astype(BF16), b.astype(BF16))


def _make_mm(hi):
    @jax.custom_vjp
    def nn(a, b):
        return _dg(a, b, 1, 0, hi)

    @jax.custom_vjp
    def nt(a, b):
        return _dg(a, b, 1, 1, hi)

    @jax.custom_vjp
    def tn(a, b):
        return _dg(a, b, 0, 0, hi)

    nn.defvjp(lambda a, b: (_dg(a, b, 1, 0, hi), (a, b)),
              lambda r, g: (_dg(g, r[1], 1, 1, hi).astype(r[0].dtype), _dg(r[0], g, 0, 0, hi).astype(r[1].dtype)))
    nt.defvjp(lambda a, b: (_dg(a, b, 1, 1, hi), (a, b)),
              lambda r, g: (_dg(g, r[1], 1, 0, hi).astype(r[0].dtype), _dg(g, r[0], 0, 0, hi).astype(r[1].dtype)))
    tn.defvjp(lambda a, b: (_dg(a, b, 0, 0, hi), (a, b)),
              lambda r, g: (_dg(r[1], g, 1, 1, hi).astype(r[0].dtype), _dg(r[0], g, 1, 0, hi).astype(r[1].dtype)))
    return nn, nt, tn


_nn, _nt, _tn = _make_mm(False)
_nn_hi, _nt_hi, _tn_hi = _make_mm(True)


def _rms(x, w):
    return x * lax.rsqrt(jnp.mean(x * x, axis=-1, keepdims=True) + EPS) * w


def _layernorm(x, w, b):
    xc = x - jnp.mean(x, axis=-1, keepdims=True)
    return xc * lax.rsqrt(jnp.mean(xc * xc, axis=-1, keepdims=True) + EPS) * w + b


def _silu(x):
    return x * jax.nn.sigmoid(x)


def _iota2(shape, dim):
    return lax.broadcasted_iota(jnp.int32, shape, dim)


def _blk_fwd(name, pre, lhs_idx, post, toks, smalls, weights, outs, tm=512):
    nt_, ns, nw = len(toks), len(smalls), len(weights)

    def body(*refs):
        tv = [r[...] for r in refs[:nt_]]
        sv = [r[...] for r in refs[nt_:nt_ + ns]]
        wr = refs[nt_ + ns:nt_ + ns + nw]
        orf = refs[nt_ + ns + nw:]
        lhs = pre(tv, sv)
        ys = [_dg(lhs[i], w[...], 1, 0) for i, w in zip(lhs_idx, wr)]
        for o_ref, o in zip(orf, post(ys, tv, sv)):
            o_ref[...] = o.astype(o_ref.dtype)

    in_specs = ([pl.BlockSpec((tm, a.shape[1]), lambda i: (i, 0)) for a in toks]
                + [_full(a.shape) for a in smalls] + [_full(w.shape) for w in weights])
    out_specs = [pl.BlockSpec((tm, w_), lambda i: (i, 0)) for w_, _ in outs]
    return pl.pallas_call(
        body, grid=(T // tm,), in_specs=in_specs, out_specs=out_specs,
        out_shape=[_sds((T, w_), dt) for w_, dt in outs], name=name, compiler_params=_cp("parallel"),
    )(*toks, *smalls, *weights)


def _blk_bwd(name, pre, lhs_idx, post, toks, smalls, weights, ct_groups, res=None, tm=256, wchunk=512):
    nt_, ns, nw = len(toks), len(smalls), len(weights)
    cts = [a for g in ct_groups for a in g]
    nc = len(cts)
    widths = [sum(a.shape[1] for a in g) for g in ct_groups]
    has_res = res is not None

    def body(*refs):
        p = 0
        tr = refs[p:p + nt_]; p += nt_
        sr = refs[p:p + ns]; p += ns
        wr = refs[p:p + nw]; p += nw
        cr = refs[p:p + nc]; p += nc
        rr = refs[p:p + has_res]; p += has_res
        dtr = refs[p:p + nt_]; p += nt_
        dsr = refs[p:p + ns]; p += ns
        dwr = refs[p:p + nw]; p += nw
        scr = refs[p:]
        i = pl.program_id(0)

        @pl.when(i == 0)
        def _():
            for r in list(dsr) + list(dwr):
                r[...] = jnp.zeros_like(r)

        tv = [r[...] for r in tr]
        sv = [r[...] for r in sr]
        ctv, q, si = [], 0, 0
        for g in ct_groups:
            if len(g) == 1:
                ctv.append(cr[q][...].astype(F32))
            else:
                off = 0
                for j, a in enumerate(g):
                    scr[si][:, off:off + a.shape[1]] = cr[q + j][...].astype(F32)
                    off += a.shape[1]
                ctv.append(scr[si][...])
                si += 1
            q += len(g)

        lhs, vjp_pre = jax.vjp(lambda *a: tuple(pre(list(a[:nt_]), list(a[nt_:]))), *tv, *sv)
        lhs_b = [l.astype(BF16) for l in lhs]
        ys = [_dg(lhs_b[k], w[...], 1, 0) for k, w in zip(lhs_idx, wr)]
        _, vjp_post = jax.vjp(lambda *a: tuple(post(list(a[:nw]), list(a[nw:nw + nt_]), list(a[nw + nt_:]))),
                              *ys, *tv, *sv)
        gp = vjp_post(tuple(ctv))
        dys, dt_post, ds_post = gp[:nw], gp[nw:nw + nt_], gp[nw + nt_:]
        dlhs = [None] * len(lhs)
        for k, w, dy, dw in zip(lhs_idx, wr, dys, dwr):
            dyb = dy.astype(BF16)
            n = w.shape[1]
            for c0 in range(0, n, wchunk):
                c1 = min(n, c0 + wchunk)
                dw[:, c0:c1] += _dg(lhs_b[k], dyb[:, c0:c1], 0, 0)
            d = _dg(dyb, w[...], 1, 1)
            dlhs[k] = d if dlhs[k] is None else dlhs[k] + d
        gq = vjp_pre(tuple(d.astype(l.dtype) for d, l in zip(dlhs, lhs)))
        dt_pre, ds_pre = gq[:nt_], gq[nt_:]
        for j in range(nt_):
            d = dt_post[j] + dt_pre[j]
            if j == 0 and has_res:
                d = d + rr[0][...]
            dtr[j][...] = d
        for j in range(ns):
            dsr[j][...] += ds_post[j] + ds_pre[j]

    tok_spec = lambda a: pl.BlockSpec((tm, a.shape[1]), lambda i: (i, 0))
    in_specs = ([tok_spec(a) for a in toks] + [_full(a.shape) for a in smalls] + [_full(w.shape) for w in weights]
                + [tok_spec(a) for a in cts] + ([tok_spec(res)] if has_res else []))
    out_specs = [tok_spec(a) for a in toks] + [_full(a.shape) for a in smalls] + [_full(w.shape) for w in weights]
    out_shape = ([_sds(a.shape) for a in toks] + [_sds(a.shape) for a in smalls] + [_sds(w.shape) for w in weights])
    scratch = [pltpu.VMEM((tm, wd), F32) for g, wd in zip(ct_groups, widths) if len(g) > 1]
    outs = pl.pallas_call(
        body, grid=(T // tm,), in_specs=in_specs, out_specs=out_specs, out_shape=out_shape,
        scratch_shapes=scratch, name=name, compiler_params=_cp("arbitrary"),
    )(*toks, *smalls, *weights, *cts, *([res] if has_res else []))
    return outs[:nt_], outs[nt_:nt_ + ns], outs[nt_ + ns:]


def _ffn_fwd(name, x, nw, gu, wd, idx, tm=512):
    def body(x_ref, nw_ref, wg_ref, wu_ref, wd_ref, o_ref, h_scr):
        s = pl.program_id(1)

        @pl.when(s == 0)
        def _():
            xv = x_ref[...]
            h_scr[...] = _rms(xv, nw_ref[...]).astype(BF16)
            o_ref[...] = xv

        h = h_scr[...]
        a = _dg(h, wg_ref[...], 1, 0)
        b = _dg(h, wu_ref[...], 1, 0)
        o_ref[...] += 0.5 * _dg(_silu(a) * b, wd_ref[...], 1, 0)

    wspec = lambda r, c, k: pl.BlockSpec((None, None, r, c), lambda i, s: (s, idx, k, 0))
    return pl.pallas_call(
        body, grid=(T // tm, NCHIP),
        in_specs=[pl.BlockSpec((tm, D), lambda i, s: (i, 0)), _full((1, D)), wspec(D, FS, 0), wspec(D, FS, 1),
                  wspec(FS, D, 0)],
        out_specs=pl.BlockSpec((tm, D), lambda i, s: (i, 0)), out_shape=_sds((T, D)),
        scratch_shapes=[pltpu.VMEM((tm, D), BF16)], name=name, compiler_params=_cp("parallel", "arbitrary"),
    )(x, nw, gu, gu, wd)


def _ffn_bwd(name, x, nw, gu, wd, idx, dy, gbufs=None, tm=512):
    ni = T // tm

    def body(x_ref, dy_ref, nw_ref, wg_ref, wu_ref, wd_ref, dx_ref, dnw_ref, dgu_ref, dwd_ref,
             dh_acc, ag, au, ad):
        s, i = pl.program_id(0), pl.program_id(1)
        rows = pl.ds(pl.multiple_of(i * tm, tm), tm)

        @pl.when((s == 0) & (i == 0))
        def _():
            dnw_ref[...] = jnp.zeros_like(dnw_ref)

        @pl.when(i == 0)
        def _():
            ag[...] = jnp.zeros_like(ag)
            au[...] = jnp.zeros_like(au)
            ad[...] = jnp.zeros_like(ad)

        xv, nwv, dyv = x_ref[...], nw_ref[...], dy_ref[...]
        h, vjp_rms = jax.vjp(_rms, xv, nwv)
        hb = h.astype(BF16)
        a = _dg(hb, wg_ref[...], 1, 0)
        b = _dg(hb, wu_ref[...], 1, 0)
        sa = jax.nn.sigmoid(a)
        act = a * sa
        dyb = (0.5 * dyv).astype(BF16)
        ad[...] += _dg(act * b, dyb, 0, 0)
        dact = _dg(dyb, wd_ref[...], 1, 1)
        da = (dact * b * (sa * (1.0 + a * (1.0 - sa)))).astype(BF16)
        db = (dact * act).astype(BF16)
        ag[...] += _dg(hb, da, 0, 0)
        au[...] += _dg(hb, db, 0, 0)
        dh = _dg(da, wg_ref[...], 1, 1) + _dg(db, wu_ref[...], 1, 1)

        @pl.when(s == 0)
        def _():
            dh_acc[rows, :] = dh

        @pl.when(s > 0)
        def _():
            dh_acc[rows, :] += dh

        @pl.when(s == NCHIP - 1)
        def _():
            dx, dnw = vjp_rms(dh_acc[rows, :])
            dx_ref[...] = dyv + dx
            dnw_ref[...] += dnw

        @pl.when(i == ni - 1)
        def _():
            dgu_ref[0:D, :] = ag[...].astype(BF16)
            dgu_ref[D:, :] = au[...].astype(BF16)
            dwd_ref[...] = ad[...].astype(BF16)

    wspec = lambda r, c, k: pl.BlockSpec((None, None, r, c), lambda s, i: (s, idx, k, 0),
                                         pipeline_mode=pl.Buffered(1))
    last = lambda s, i: (jnp.where(s == NCHIP - 1, i, 0), 0)
    nb = 0 if gbufs is None else 2
    return pl.pallas_call(
        lambda *refs: body(*refs[:6], *refs[6 + nb:]), grid=(NCHIP, ni),
        in_specs=[pl.BlockSpec((tm, D), lambda s, i: (i, 0)), pl.BlockSpec((tm, D), lambda s, i: (i, 0)),
                  _full((1, D)), wspec(D, FS, 0), wspec(D, FS, 1), wspec(FS, D, 0)] + [ANY] * nb,
        out_specs=[pl.BlockSpec((tm, D), last), _full((1, D)), wspec(2 * D, FS, 0), wspec(FS, D, 0)],
        out_shape=[_sds((T, D)), _sds((1, D)), _sds(gu.shape, BF16), _sds(wd.shape, BF16)],
        input_output_aliases={6 + k: 2 + k for k in range(nb)},
        scratch_shapes=[pltpu.VMEM((T, D), F32), pltpu.VMEM((D, FS), F32), pltpu.VMEM((D, FS), F32),
                        pltpu.VMEM((FS, D), F32)],
        name=name, compiler_params=_cp("arbitrary", "arbitrary"),
    )(x, dy, nw, gu, gu, wd, *(gbufs or ()))


CONV_ROWS = 256


def _conv_pad(k):
    return 8 * ((k - 1 + 7) // 8)


def _conv_fwd(name, x, w, b, act):
    k_w, c = w.shape
    tc = 256 if c % 256 == 0 else LANES
    pad = _conv_pad(k_w)
    has_b = b is not None

    def body(*refs):
        x_ref, w_ref = refs[0], refs[1]
        b_ref = refs[2] if has_b else None
        y_ref, xp = refs[2 + has_b], refs[3 + has_b]
        xp[0:pad, :] = jnp.zeros((pad, tc), F32)
        xp[pad:, :] = x_ref[...]

        def step(t, carry):
            base = pl.multiple_of(t * CONV_ROWS, CONV_ROWS)
            win = xp[pl.ds(base, CONV_ROWS + pad), :]
            acc = jnp.zeros((CONV_ROWS, tc), F32)
            for k in range(k_w):
                o = pad - (k_w - 1) + k
                acc = acc + w_ref[k:k + 1, :] * win[o:o + CONV_ROWS, :]
            if has_b:
                acc = acc + b_ref[...]
            y_ref[pl.ds(base, CONV_ROWS), :] = _silu(acc) if act else acc
            return carry

        lax.fori_loop(0, T // CONV_ROWS, step, 0)

    col = lambda r: pl.BlockSpec((r, tc), lambda j: (0, j))
    ins = [x, w] + ([b] if has_b else [])
    return pl.pallas_call(
        body, grid=(c // tc,), in_specs=[col(T), col(k_w)] + ([col(1)] if has_b else []), out_specs=col(T),
        out_shape=_sds((T, c)), scratch_shapes=[pltpu.VMEM((T + pad, tc), F32)], name=name,
        compiler_params=_cp("parallel"),
    )(*ins)


def _conv_bwd(name, x, w, b, act, dy):
    k_w, c = w.shape
    tc = 256 if c % 256 == 0 else LANES
    pad = _conv_pad(k_w)
    has_b = b is not None

    def body(*refs):
        x_ref, w_ref, dy_ref = refs[0], refs[1], refs[2]
        b_ref = refs[3] if has_b else None
        dx_ref, dw_ref, db_ref, xp, dp = refs[3 + has_b:]
        xp[0:pad, :] = jnp.zeros((pad, tc), F32)
        xp[pad:, :] = x_ref[...]
        dp[T:, :] = jnp.zeros((pad, tc), F32)
        dw_ref[...] = jnp.zeros_like(dw_ref)
        db_ref[...] = jnp.zeros_like(db_ref)

        def step1(t, carry):
            base = pl.multiple_of(t * CONV_ROWS, CONV_ROWS)
            d = dy_ref[pl.ds(base, CONV_ROWS), :]
            win = xp[pl.ds(base, CONV_ROWS + pad), :]
            offs = [pad - (k_w - 1) + k for k in range(k_w)]
            if act:
                acc = jnp.zeros((CONV_ROWS, tc), F32)
                for k, o in enumerate(offs):
                    acc = acc + w_ref[k:k + 1, :] * win[o:o + CONV_ROWS, :]
                if has_b:
                    acc = acc + b_ref[...]
                sg = jax.nn.sigmoid(acc)
                d = d * (sg * (1.0 + acc * (1.0 - sg)))
            dp[pl.ds(base, CONV_ROWS), :] = d
            for k, o in enumerate(offs):
                dw_ref[k:k + 1, :] += jnp.sum(d * win[o:o + CONV_ROWS, :], axis=0, keepdims=True)
            db_ref[...] += jnp.sum(d, axis=0, keepdims=True)
            return carry

        lax.fori_loop(0, T // CONV_ROWS, step1, 0)

        def step2(t, carry):
            base = pl.multiple_of(t * CONV_ROWS, CONV_ROWS)
            win = dp[pl.ds(base, CONV_ROWS + pad), :]
            acc = jnp.zeros((CONV_ROWS, tc), F32)
            for k in range(k_w):
                o = (k_w - 1) - k
                acc = acc + w_ref[k:k + 1, :] * win[o:o + CONV_ROWS, :]
            dx_ref[pl.ds(base, CONV_ROWS), :] = acc
            return carry

        lax.fori_loop(0, T // CONV_ROWS, step2, 0)

    col = lambda r: pl.BlockSpec((r, tc), lambda j: (0, j))
    ins = [x, w, dy] + ([b] if has_b else [])
    return pl.pallas_call(
        body, grid=(c // tc,), in_specs=[col(T), col(k_w), col(T)] + ([col(1)] if has_b else []),
        out_specs=[col(T), col(k_w), col(1)], out_shape=[_sds((T, c)), _sds((k_w, c)), _sds((1, c))],
        scratch_shapes=[pltpu.VMEM((T + pad, tc), F32), pltpu.VMEM((T + pad, tc), F32)], name=name,
        compiler_params=_cp("parallel"),
    )(*ins)


def _attn_consts(n):
    i = _iota2((BLOCK, 2 * BLOCK), 0)
    j = _iota2((BLOCK, 2 * BLOCK), 1)
    dist = i + BLOCK - j
    valid = (dist >= 0) & (dist < WINDOW) & ((n > 0) | (j >= BLOCK))
    return dist.astype(F32), valid


def _attn_block(q4, kk, vv, sinks, dist, valid, kv):
    outs = []
    lane = _iota2((1, HEADS), 1)
    for g in range(GROUP):
        h = kv * GROUP + g
        slope = 2.0 ** (-8.0 * (h + 1) / HEADS)
        s = _nt(q4[:, g * HDIM:(g + 1) * HDIM], kk) * (HDIM ** -0.5)
        s = jnp.where(valid, s - slope * dist, -1e30)
        sink = jnp.sum(jnp.where(lane == h, sinks, 0.0), axis=1, keepdims=True)
        m = jnp.maximum(jnp.max(s, axis=-1, keepdims=True), sink)
        e = jnp.exp(s - m)
        p = e / (jnp.sum(e, axis=-1, keepdims=True) + jnp.exp(sink - m))
        outs.append(_nn(p, vv))
    return tuple(outs)


def _attn_fwd(name, qa, ka, va, sinks):
    def body(q_ref, k_ref, v_ref, s_ref, o_ref, kp, vp):
        kp[0:BLOCK, :] = jnp.zeros((BLOCK, KV_A), F32)
        vp[0:BLOCK, :] = jnp.zeros((BLOCK, KV_A), F32)
        kp[BLOCK:, :] = k_ref[...]
        vp[BLOCK:, :] = v_ref[...]
        sinks_v = s_ref[...]

        def step(n, carry):
            r = pl.multiple_of(n * BLOCK, BLOCK)
            dist, valid = _attn_consts(n)
            k2 = kp[pl.ds(r, 2 * BLOCK), :]
            v2 = vp[pl.ds(r, 2 * BLOCK), :]
            for kv in range(KV_HEADS):
                q4 = q_ref[pl.ds(r, BLOCK), kv * GROUP * HDIM:(kv + 1) * GROUP * HDIM]
                og = _attn_block(q4, k2[:, kv * HDIM:(kv + 1) * HDIM], v2[:, kv * HDIM:(kv + 1) * HDIM], sinks_v,
                                 dist, valid, kv)
                for g in range(GROUP):
                    h = kv * GROUP + g
                    o_ref[pl.ds(r, BLOCK), h * HDIM:(h + 1) * HDIM] = og[g]
            return carry

        lax.fori_loop(0, T // BLOCK, step, 0)

    return pl.pallas_call(
        body, out_shape=_sds((T, Q_A)),
        scratch_shapes=[pltpu.VMEM((T + BLOCK, KV_A), F32), pltpu.VMEM((T + BLOCK, KV_A), F32)], name=name,
        compiler_params=pltpu.CompilerParams(vmem_limit_bytes=VMEM_LIMIT),
    )(qa, ka, va, sinks)


def _attn_bwd(name, qa, ka, va, sinks, do):
    def body(q_ref, k_ref, v_ref, s_ref, do_ref, dq_ref, dk_ref, dv_ref, ds_ref, kp, vp, dkp, dvp):
        kp[0:BLOCK, :] = jnp.zeros((BLOCK, KV_A), F32)
        vp[0:BLOCK, :] = jnp.zeros((BLOCK, KV_A), F32)
        kp[BLOCK:, :] = k_ref[...]
        vp[BLOCK:, :] = v_ref[...]
        dkp[...] = jnp.zeros_like(dkp)
        dvp[...] = jnp.zeros_like(dvp)
        ds_ref[...] = jnp.zeros_like(ds_ref)
        sinks_v = s_ref[...]

        def step(n, carry):
            r = pl.multiple_of(n * BLOCK, BLOCK)
            dist, valid = _attn_consts(n)
            k2 = kp[pl.ds(r, 2 * BLOCK), :]
            v2 = vp[pl.ds(r, 2 * BLOCK), :]
            for kv in range(KV_HEADS):
                cols = slice(kv * HDIM, (kv + 1) * HDIM)
                q4 = q_ref[pl.ds(r, BLOCK), kv * GROUP * HDIM:(kv + 1) * GROUP * HDIM]
                _, vjp = jax.vjp(lambda q, k, v, s: _attn_block(q, k, v, s, dist, valid, kv),
                                 q4, k2[:, cols], v2[:, cols], sinks_v)
                cts = tuple(do_ref[pl.ds(r, BLOCK), (kv * GROUP + g) * HDIM:(kv * GROUP + g + 1) * HDIM]
                            for g in range(GROUP))
                dq4, dkk, dvv, dsk = vjp(cts)
                dq_ref[pl.ds(r, BLOCK), kv * GROUP * HDIM:(kv + 1) * GROUP * HDIM] = dq4
                dkp[pl.ds(r, 2 * BLOCK), cols] += dkk
                dvp[pl.ds(r, 2 * BLOCK), cols] += dvv
                ds_ref[...] += dsk
            return carry

        lax.fori_loop(0, T // BLOCK, step, 0)
        dk_ref[...] = dkp[BLOCK:, :]
        dv_ref[...] = dvp[BLOCK:, :]

    pad = lambda: pltpu.VMEM((T + BLOCK, KV_A), F32)
    return pl.pallas_call(
        body, out_shape=[_sds((T, Q_A)), _sds((T, KV_A)), _sds((T, KV_A)), _sds((1, HEADS))],
        scratch_shapes=[pad(), pad(), pad(), pad()], name=name,
        compiler_params=pltpu.CompilerParams(vmem_limit_bytes=VMEM_LIMIT),
    )(qa, ka, va, sinks, do)


def _dn_consts():
    i = _iota2((CHUNK, CHUNK), 0)
    j = _iota2((CHUNK, CHUNK), 1)
    return dict(causal=i >= j, strict=i > j, eye=(i == j).astype(F32), ltri=(i >= j).astype(F32),
                ones=jnp.ones((CHUNK, CHUNK), F32), last=(_iota2((CHUNK, 1), 0) == CHUNK - 1).astype(F32))


def _l2norm(x):
    return x * lax.rsqrt(jnp.sum(x * x, axis=-1, keepdims=True) + EPS)


def _head_cols(m):
    lane = _iota2((1, HEADS), 1)
    return jnp.concatenate([jnp.sum(jnp.where(lane == h, m, 0.0), axis=1, keepdims=True)[None]
                            for h in range(HEADS)], axis=0)


def _dn_local(q3, k3, v3, braw, araw, alog, dtb, cs):
    q = _l2norm(q3) * (HDIM ** -0.5)
    k = _l2norm(k3)
    g = -jnp.exp(alog) * jax.nn.softplus(araw + dtb)
    gc_all = _nn_hi(cs["ltri"], g)
    egc_all = jnp.exp(gc_all)
    beta, gc, egc = _head_cols(jax.nn.sigmoid(braw)), _head_cols(gc_all), _head_cols(egc_all)
    a = jnp.broadcast_to(gc, (HEADS, CHUNK, CHUNK))
    diff = a - jnp.swapaxes(a, 1, 2)
    decay = jnp.where(cs["causal"], jnp.exp(jnp.where(cs["causal"], diff, 0.0)), 0.0)
    kb = k * beta
    low = jnp.where(cs["strict"], _nt(kb, k) * decay, 0.0)
    inv = cs["eye"] - low
    pw = low
    for _ in range(5):
        pw = _nn_hi(pw, pw)
        inv = inv + _nn_hi(inv, pw)
    u = _nn_hi(inv, v3 * beta)
    w = _nn_hi(inv, kb * egc)
    attn = _nt(q, k) * decay
    gc_last = jnp.sum(gc * cs["last"], axis=1, keepdims=True)
    return u, w, attn, q * egc, k * jnp.exp(gc_last - gc), egc_all


def _heads3(ref, off=0):
    return jnp.concatenate([ref[:, off + h * HDIM:off + (h + 1) * HDIM][None] for h in range(HEADS)], axis=0)


def _dn_local_fwd(name, qkv, ba, alog, dtb):
    def body(qkv_ref, ba_ref, al_ref, dt_ref, u_ref, w_ref, at_ref, qd_ref, kd_ref, eg_ref):
        bav = ba_ref[...]
        outs = _dn_local(_heads3(qkv_ref), _heads3(qkv_ref, 512), _heads3(qkv_ref, 1024), bav[:, :HEADS],
                         bav[:, HEADS:], al_ref[...], dt_ref[...], _dn_consts())
        for r, o in zip((u_ref, w_ref, at_ref, qd_ref, kd_ref), outs[:5]):
            for h in range(HEADS):
                r[:, h * HDIM:(h + 1) * HDIM] = o[h]
        eg_ref[...] = outs[5]

    row = lambda w_: pl.BlockSpec((CHUNK, w_), lambda n: (n, 0))
    return pl.pallas_call(
        body, grid=(NCHUNK,), in_specs=[row(QKV_B), row(2 * HEADS), _full((1, HEADS)), _full((1, HEADS))],
        out_specs=[row(V_B)] * 5 + [row(HEADS)], out_shape=[_sds((T, V_B))] * 5 + [_sds((T, HEADS))], name=name,
        compiler_params=_cp("parallel"),
    )(qkv, ba, alog, dtb)


def _dn_local_bwd(name, qkv, ba, alog, dtb, cts):
    def body(qkv_ref, ba_ref, al_ref, dt_ref, du_ref, dw_ref, dat_ref, dqd_ref, dkd_ref, deg_ref,
             dqkv_ref, dba_ref, dal_ref, ddt_ref):
        @pl.when(pl.program_id(0) == 0)
        def _():
            dal_ref[...] = jnp.zeros_like(dal_ref)
            ddt_ref[...] = jnp.zeros_like(ddt_ref)

        cs = _dn_consts()
        bav = ba_ref[...]
        _, vjp = jax.vjp(lambda *a: _dn_local(*a, cs), _heads3(qkv_ref), _heads3(qkv_ref, 512),
                         _heads3(qkv_ref, 1024), bav[:, :HEADS], bav[:, HEADS:], al_ref[...], dt_ref[...])
        dq, dk, dv, dbr, dar, dal, ddt = vjp((_heads3(du_ref), _heads3(dw_ref), _heads3(dat_ref), _heads3(dqd_ref),
                                              _heads3(dkd_ref), deg_ref[...]))
        for h in range(HEADS):
            dqkv_ref[:, h * HDIM:(h + 1) * HDIM] = dq[h]
            dqkv_ref[:, 512 + h * HDIM:512 + (h + 1) * HDIM] = dk[h]
            dqkv_ref[:, 1024 + h * HDIM:1024 + (h + 1) * HDIM] = dv[h]
        dba_ref[:, :HEADS] = dbr
        dba_ref[:, HEADS:] = dar
        dal_ref[...] += dal
        ddt_ref[...] += ddt

    row = lambda w_: pl.BlockSpec((CHUNK, w_), lambda n: (n, 0))
    return pl.pallas_call(
        body, grid=(NCHUNK,),
        in_specs=[row(QKV_B), row(2 * HEADS), _full((1, HEADS)), _full((1, HEADS))] + [row(V_B)] * 5 + [row(HEADS)],
        out_specs=[row(QKV_B), row(2 * HEADS), _full((1, HEADS)), _full((1, HEADS))],
        out_shape=[_sds((T, QKV_B)), _sds((T, 2 * HEADS)), _sds((1, HEADS)), _sds((1, HEADS))], name=name,
        compiler_params=_cp("arbitrary"),
    )(qkv, ba, alog, dtb, *cts)


def _dn_step(s, u, w, attn, qd, kd, egc, z, nw):
    last = (_iota2((CHUNK, 1), 0) == CHUNK - 1).astype(F32)
    gl = jnp.sum(_head_cols(egc) * last, axis=1, keepdims=True)
    v_new = u - _nn(w, s)
    o = _nn(qd, s) + _nn(attn, v_new)
    s_new = s * gl + _tn(kd, v_new)
    return s_new, _rms(o, nw) * _silu(z)


def _unheads(ref, v3):
    for h in range(HEADS):
        ref[:, h * HDIM:(h + 1) * HDIM] = v3[h]


def _dn_rec_fwd(name, u, w, attn, qd, kd, egc, z, nw):
    def body(u_ref, w_ref, at_ref, qd_ref, kd_ref, eg_ref, z_ref, nw_ref, o_ref, ss_ref, s_scr):
        @pl.when(pl.program_id(0) == 0)
        def _():
            s_scr[...] = jnp.zeros_like(s_scr)

        s = s_scr[...]
        ss_ref[...] = s
        s_new, on = _dn_step(s, _heads3(u_ref), _heads3(w_ref), _heads3(at_ref), _heads3(qd_ref), _heads3(kd_ref),
                             eg_ref[...], _heads3(z_ref), nw_ref[...])
        s_scr[...] = s_new
        _unheads(o_ref, on)

    row = lambda w_: pl.BlockSpec((CHUNK, w_), lambda n: (n, 0))
    return pl.pallas_call(
        body, grid=(NCHUNK,), in_specs=[row(V_B)] * 5 + [row(HEADS), row(V_B), _full((1, HDIM))],
        out_specs=[row(V_B), pl.BlockSpec((None, HEADS, HDIM, HDIM), lambda n: (n, 0, 0, 0))],
        out_shape=[_sds((T, V_B)), _sds((NCHUNK, HEADS, HDIM, HDIM))],
        scratch_shapes=[pltpu.VMEM((HEADS, HDIM, HDIM), F32)], name=name, compiler_params=_cp("arbitrary"),
    )(u, w, attn, qd, kd, egc, z, nw)


def _dn_rec_bwd(name, u, w, attn, qd, kd, egc, z, nw, ss, do):
    def body(u_ref, w_ref, at_ref, qd_ref, kd_ref, eg_ref, z_ref, nw_ref, ss_ref, do_ref,
             du_ref, dw_ref, dat_ref, dqd_ref, dkd_ref, deg_ref, dz_ref, dnw_ref, ds_scr):
        @pl.when(pl.program_id(0) == 0)
        def _():
            ds_scr[...] = jnp.zeros_like(ds_scr)
            dnw_ref[...] = jnp.zeros_like(dnw_ref)

        _, vjp = jax.vjp(_dn_step, ss_ref[...], _heads3(u_ref), _heads3(w_ref), _heads3(at_ref), _heads3(qd_ref),
                         _heads3(kd_ref), eg_ref[...], _heads3(z_ref), nw_ref[...])
        ds, du, dw, dat, dqd, dkd, deg, dz, dnw = vjp((ds_scr[...], _heads3(do_ref)))
        ds_scr[...] = ds
        for r, v in zip((du_ref, dw_ref, dat_ref, dqd_ref, dkd_ref, dz_ref), (du, dw, dat, dqd, dkd, dz)):
            _unheads(r, v)
        deg_ref[...] = deg
        dnw_ref[...] += dnw

    row = lambda w_: pl.BlockSpec((CHUNK, w_), lambda n: (NCHUNK - 1 - n, 0))
    return pl.pallas_call(
        body, grid=(NCHUNK,),
        in_specs=[row(V_B)] * 5 + [row(HEADS), row(V_B), _full((1, HDIM)),
                                   pl.BlockSpec((None, HEADS, HDIM, HDIM), lambda n: (NCHUNK - 1 - n, 0, 0, 0)),
                                   row(V_B)],
        out_specs=[row(V_B)] * 5 + [row(HEADS), row(V_B), _full((1, HDIM))],
        out_shape=[_sds((T, V_B))] * 5 + [_sds((T, HEADS)), _sds((T, V_B)), _sds((1, HDIM))],
        scratch_shapes=[pltpu.VMEM((HEADS, HDIM, HDIM), F32)], name=name, compiler_params=_cp("arbitrary"),
    )(u, w, attn, qd, kd, egc, z, nw, ss, do)


def _final(name, x, fw, target, tm=512):
    def body(x_ref, fw_ref, t_ref, l_ref, dx_ref, dfw_ref):
        @pl.when(pl.program_id(0) == 0)
        def _():
            l_ref[...] = jnp.zeros_like(l_ref)
            dfw_ref[...] = jnp.zeros_like(dfw_ref)

        tv = t_ref[...]

        def f(xv, fwv):
            err = _rms(xv, fwv) - tv
            per_tok = jnp.mean(err * err, axis=-1, keepdims=True)
            return 0.5 * jnp.sum(per_tok, axis=0, keepdims=True)

        loss, vjp = jax.vjp(f, x_ref[...], fw_ref[...])
        dx, dfw = vjp(jnp.ones((1, 1), F32))
        l_ref[...] += loss
        dx_ref[...] = dx
        dfw_ref[...] += dfw

    tok = pl.BlockSpec((tm, D), lambda i: (i, 0))
    return pl.pallas_call(
        body, grid=(T // tm,), in_specs=[tok, _full((1, D)), tok], out_specs=[_full((1, 1)), tok, _full((1, D))],
        out_shape=[_sds((1, 1)), _sds((T, D)), _sds((1, D))], name=name, compiler_params=_cp("arbitrary"),
    )(x, fw, target)


def _m1_pre(tv, sv):
    return [_rms(tv[0], sv[0])]


def _m1_post(ys, tv, sv):
    return (ys[0],)


def _m1_post_split(ys, tv, sv):
    return tuple(ys[0][:, a:b] for a, b in zip(IN_SPLITS[:-1], IN_SPLITS[1:]))


def _m5_pre(tv, sv):
    return [tv[1], tv[2]]


def _m5_post(ys, tv, sv):
    return (tv[0] + ys[0] + ys[1],)


def _c1_pre(tv, sv):
    return [_rms(tv[0], sv[0])]


def _c1_post(ys, tv, sv):
    return ((ys[0] + sv[1]) * jax.nn.sigmoid(ys[1] + sv[2]),)


def _c3_pre(tv, sv):
    return [_silu(_layernorm(tv[0], sv[0], sv[1]))]


def _c3_post(ys, tv, sv):
    return (tv[1] + ys[0] + sv[2],)


def _row(v):
    return v.reshape(1, -1)


def _mixer_fwd(tag, x, p):
    parts = _blk_fwd(f"m1_fwd_{tag}", _m1_pre, [0], _m1_post_split, [x], [p["nw"]], [p["w_in"]],
                     [(b - a, F32) for a, b in zip(IN_SPLITS[:-1], IN_SPLITS[1:])])
    qa, ka, va, qkvb, z, ba = parts
    att = _attn_fwd(f"attn_fwd_{tag}", qa, ka, va, p["sinks"])
    qkvc = _conv_fwd(f"dnconv_fwd_{tag}", qkvb, p["dn_conv_w"], None, True)
    loc = _dn_local_fwd(f"dnloc_fwd_{tag}", qkvc, ba, p["a_log"], p["dt_bias"])
    og, ss = _dn_rec_fwd(f"dnrec_fwd_{tag}", *loc, z, p["dn_norm_w"])
    (out,) = _blk_fwd(f"m5_fwd_{tag}", _m5_pre, [0, 1], _m5_post, [x, att, og], [], [p["wo_a"], p["wo_b"]],
                      [(D, F32)])
    return out, dict(x=x, qa=qa, ka=ka, va=va, qkvb=qkvb, z=z, ba=ba, att=att, qkvc=qkvc, loc=loc, og=og, ss=ss)


def _mixer_bwd(tag, dy, p, s):
    (dxa, datt, dog), _, (dwo_a, dwo_b) = _blk_bwd(f"m5_bwd_{tag}", _m5_pre, [0, 1], _m5_post,
                                                   [s["x"], s["att"], s["og"]], [], [p["wo_a"], p["wo_b"]], [[dy]])
    rec = _dn_rec_bwd(f"dnrec_bwd_{tag}", *s["loc"], s["z"], p["dn_norm_w"], s["ss"], dog)
    dz, dnw_dn = rec[6], rec[7]
    dqkvc, dba, dalog, ddtb = _dn_local_bwd(f"dnloc_bwd_{tag}", s["qkvc"], s["ba"], p["a_log"], p["dt_bias"],
                                            rec[:6])
    dqkvb, dconvw, _ = _conv_bwd(f"dnconv_bwd_{tag}", s["qkvb"], p["dn_conv_w"], None, True, dqkvc)
    dqa, dka, dva, dsinks = _attn_bwd(f"attn_bwd_{tag}", s["qa"], s["ka"], s["va"], p["sinks"], datt)
    (dx,), (dnw,), (dw_in,) = _blk_bwd(f"m1_bwd_{tag}", _m1_pre, [0], _m1_post, [s["x"]], [p["nw"]], [p["w_in"]],
                                       [[dqa, dka, dva, dqkvb, dz, dba]], res=dxa)
    return dx, dict(nw=dnw, w_in=dw_in, wo_a=dwo_a, wo_b=dwo_b, dn_conv_w=dconvw, sinks=dsinks, a_log=dalog,
                    dt_bias=ddtb, dn_norm_w=dnw_dn)


def _conformer_fwd(tag, x, p):
    (glu,) = _blk_fwd(f"c1_fwd_{tag}", _c1_pre, [0, 0], _c1_post, [x], [p["nw"], p["b1a"], p["b1b"]],
                      [p["w1a"], p["w1b"]], [(D, F32)])
    cc = _conv_fwd(f"dwconv_fwd_{tag}", glu, p["w_dw"], p["b_dw"], False)
    (out,) = _blk_fwd(f"c3_fwd_{tag}", _c3_pre, [0], _c3_post, [cc, x], [p["ln_w"], p["ln_b"], p["b2"]], [p["w2"]],
                      [(D, F32)])
    return out, dict(x=x, glu=glu, cc=cc)


def _conformer_bwd(tag, dy, p, s):
    (dcc, dxa), (dlnw, dlnb, db2), (dw2,) = _blk_bwd(f"c3_bwd_{tag}", _c3_pre, [0], _c3_post, [s["cc"], s["x"]],
                                                     [p["ln_w"], p["ln_b"], p["b2"]], [p["w2"]], [[dy]])
    dglu, dwdw, dbdw = _conv_bwd(f"dwconv_bwd_{tag}", s["glu"], p["w_dw"], p["b_dw"], False, dcc)
    (dx,), (dnw, db1a, db1b), (dw1a, dw1b) = _blk_bwd(f"c1_bwd_{tag}", _c1_pre, [0, 0], _c1_post, [s["x"]],
                                                      [p["nw"], p["b1a"], p["b1b"]], [p["w1a"], p["w1b"]], [[dglu]],
                                                      res=dxa)
    return dx, dict(nw=dnw, b1a=db1a, b1b=db1b, w1a=dw1a, w1b=dw1b, w_dw=dwdw, b_dw=dbdw, ln_w=dlnw, ln_b=dlnb,
                    b2=db2, w2=dw2)


def _layer_fwd(l, x, nw, ffn, p):
    x1 = _ffn_fwd(f"ffn_fwd_{l}a", x, _row(nw[0]), *ffn, 0)
    p = dict(p, nw=_row(nw[1]))
    x2, sv = (_mixer_fwd if l % 2 == 0 else _conformer_fwd)(str(l), x1, p)
    return _ffn_fwd(f"ffn_fwd_{l}b", x2, _row(nw[2]), *ffn, 1), (x, x2, p, sv)


def _layer_bwd(l, dx, nw, ffn, saved, after_first=lambda dx: dx):
    x0, x2, p, sv = saved
    dx, dn2, *dffn = _ffn_bwd(f"ffn_bwd_{l}b", x2, _row(nw[2]), *ffn, 1, dx)
    dx = after_first(dx)
    dx, dmix = (_mixer_bwd if l % 2 == 0 else _conformer_bwd)(str(l), dx, p, sv)
    dx, dn0, *dffn = _ffn_bwd(f"ffn_bwd_{l}a", x0, _row(nw[0]), *ffn, 0, dx, dffn)
    return dx, jnp.concatenate([dn0, dmix.pop("nw"), dn2], axis=0), dffn, dmix


def _place():
    x, y, c = lax.axis_index("x"), lax.axis_index("y"), lax.axis_index("c")
    chips = [(1 - x, y), (x, 1 - y), (1 - x, 1 - y)]
    return x, y, c, 2 * x + y, chips, [2 * px + py for px, py in chips]


def _handshake(peers):
    barrier = pltpu.get_barrier_semaphore()
    for p in peers:
        pl.semaphore_signal(barrier, inc=1, device_id=p, device_id_type=MESH)
    pl.semaphore_wait(barrier, len(peers))


def _chip_peers():
    x, y, c, _, chips, _ = _place()
    return [(*chip, c) for chip in chips] + [(x, y, 1 - c)]


def _gather_copies(ins, outs, nb, send, recv, fsend, frecv, lsem):
    n_in = len(ins)
    x, y, c, me, chips, cidx = _place()
    sib = (x, y, 1 - c)
    local = [pltpu.make_async_copy(ins[a], outs[a].at[me], lsem.at[a]) for a in range(n_in)]
    for cp in local:
        cp.start()

    def ici(a, j):
        k = a * 3 + j
        src, dst = (ins[a], outs[a].at[me]) if a >= nb else (ins[a].at[pl.ds(c, 1)], outs[a].at[me, pl.ds(c, 1)])
        return pltpu.make_async_remote_copy(src, dst, send.at[k], recv.at[k], device_id=(*chips[j], c),
                                            device_id_type=MESH)

    def landed(a, j):
        k = a * 3 + j
        dst = outs[a].at[cidx[j]] if a >= nb else outs[a].at[cidx[j], pl.ds(c, 1)]
        return pltpu.make_async_remote_copy(dst, dst, send.at[k], recv.at[k], device_id=(*chips[j], c),
                                            device_id_type=MESH)

    def passed(a, j, who):
        k = a * 3 + j
        part = outs[a].at[cidx[j], pl.ds(who, 1)]
        return pltpu.make_async_remote_copy(part, part, fsend.at[k], frecv.at[k], device_id=sib, device_id_type=MESH)

    sends = [ici(a, j) for a in range(n_in) for j in range(3)]
    for cp in sends:
        cp.start()
    for a in range(nb):
        for j in range(3):
            landed(a, j).wait_recv()
            cp = passed(a, j, c)
            cp.start()
            sends.append(cp)
    for a in range(nb, n_in):
        for j in range(3):
            landed(a, j).wait_recv()
    for a in range(nb):
        for j in range(3):
            passed(a, j, 1 - c).wait_recv()
    for cp in sends:
        cp.wait_send()
    for cp in local:
        cp.wait()


def _gather_sems(n_in, nb):
    dma = pltpu.SemaphoreType.DMA
    return [dma((3 * n_in,)), dma((3 * n_in,)), dma((3 * nb,)), dma((3 * nb,)), dma((n_in,))]


def _gather_async(name, halved, whole=()):
    nb, arrs = len(halved), list(halved) + list(whole)
    hbm = pltpu.MemorySpace.HBM
    ins = [jax.new_ref(a, memory_space=hbm) for a in arrs]
    outs = [jax.empty_ref(_sds((NCHIP,) + a.shape, a.dtype), memory_space=hbm) for a in arrs]

    @pl.kernel(mesh=plsc.ScalarSubcoreMesh(axis_name="seq", num_cores=1), name=name,
               scratch_types=tuple(_gather_sems(len(arrs), nb)),
               compiler_params=pltpu.CompilerParams(collective_id=2))
    def launch(send, recv, fsend, frecv, lsem):
        _handshake(_chip_peers())
        _gather_copies(ins, outs, nb, send, recv, fsend, frecv, lsem)

    launch()
    return outs


def _swap_halves(name, grads):
    n = len(grads)
    hbm = pltpu.MemorySpace.HBM
    ins = [jax.new_ref(g, memory_space=hbm) for g in grads]
    outs = [jax.empty_ref(_sds((NCHIP, g.shape[1] // 2) + g.shape[2:], g.dtype), memory_space=hbm) for g in grads]

    @pl.kernel(mesh=plsc.ScalarSubcoreMesh(axis_name="seq", num_cores=1), name=name,
               scratch_types=(pltpu.SemaphoreType.DMA((n,)), pltpu.SemaphoreType.DMA((n,))),
               compiler_params=pltpu.CompilerParams(collective_id=1))
    def launch(send, recv):
        x, y, c, _, _, _ = _place()
        sib = (x, y, 1 - c)
        _handshake([sib])
        cps = []
        for a in range(n):
            h = grads[a].shape[1] // 2
            cps.append(pltpu.make_async_remote_copy(ins[a].at[:, pl.ds((1 - c) * h, h)], outs[a], send.at[a],
                                                    recv.at[a], device_id=sib, device_id_type=MESH))
        for cp in cps:
            cp.start()
        for cp in cps:
            cp.wait()

    launch()
    return outs


def _row_tile(r, cap=256):
    return max(t for t in range(8, cap + 1, 8) if r % t == 0)


def _add_half(name, g, r, c_arr):
    _, l, rows, cols = g.shape
    h = l // 2
    tr = _row_tile(rows)

    def body(c_ref, g_ref, r_ref, o_ref):
        o_ref[...] = (g_ref[...].astype(F32) + r_ref[...].astype(F32)).astype(BF16)

    blk = (None, None, tr, cols)
    return pl.pallas_call(
        body,
        grid_spec=pltpu.PrefetchScalarGridSpec(
            num_scalar_prefetch=1, grid=(NCHIP, h, rows // tr),
            in_specs=[pl.BlockSpec(blk, lambda j, i, t, c_ref: (j, c_ref[0] * h + i, t, 0)),
                      pl.BlockSpec(blk, lambda j, i, t, c_ref: (j, i, t, 0))],
            out_specs=pl.BlockSpec(blk, lambda j, i, t, c_ref: (j, i, t, 0))),
        out_shape=_sds((NCHIP, h, rows, cols), BF16), name=name,
        compiler_params=_cp("parallel", "parallel", "parallel"),
    )(c_arr, g, r)


def _scatter_async(name, parts, sums, where):
    nb = len(parts)
    ins = [jax.new_ref(p, memory_space=pltpu.MemorySpace.HBM) for p in parts]
    dma = pltpu.SemaphoreType.DMA

    @pl.kernel(mesh=plsc.ScalarSubcoreMesh(axis_name="seq", num_cores=1), name=name,
               scratch_types=(dma((3 * nb,)), dma((3 * nb,)), dma((4 * nb,)), dma((4 * nb,)), dma((nb,))),
               compiler_params=pltpu.CompilerParams(collective_id=3))
    def launch(send, recv, fsend, frecv, lsem):
        _handshake(_chip_peers())
        x, y, c, me, chips, cidx = _place()
        sib = (x, y, 1 - c)

        def slot(a, half, chip):
            return sums[a].at[half, chip, pl.ds(where[a], 1)]

        local = [pltpu.make_async_copy(ins[a].at[me], slot(a, c, me), lsem.at[a]) for a in range(nb)]
        for cp in local:
            cp.start()

        def ici(a, j):
            return pltpu.make_async_remote_copy(ins[a].at[cidx[j]], slot(a, c, me), send.at[a * 3 + j],
                                                recv.at[a * 3 + j], device_id=(*chips[j], c), device_id_type=MESH)

        def landed(a, j):
            dst = slot(a, c, cidx[j])
            return pltpu.make_async_remote_copy(dst, dst, send.at[a * 3 + j], recv.at[a * 3 + j],
                                                device_id=(*chips[j], c), device_id_type=MESH)

        def passed(a, j, who):
            dst = slot(a, who, me if j == 3 else cidx[j])
            src = ins[a].at[me] if j == 3 else dst
            return pltpu.make_async_remote_copy(src, dst, fsend.at[a * 4 + j], frecv.at[a * 4 + j], device_id=sib,
                                                device_id_type=MESH)

        sends = [ici(a, j) for a in range(nb) for j in range(3)] + [passed(a, 3, c) for a in range(nb)]
        for cp in sends:
            cp.start()
        for a in range(nb):
            for j in range(3):
                landed(a, j).wait_recv()
                cp = passed(a, j, c)
                cp.start()
                sends.append(cp)
        for a in range(nb):
            for j in range(4):
                passed(a, j, 1 - c).wait_recv()
        for cp in sends:
            cp.wait_send()
        for cp in local:
            cp.wait()

    launch()


def _exchange_small(small, rep):
    def body(small_in, rep_in, small_out, rep_out, lsem, ssend, srecv):
        x, y, c, me, _, _ = _place()
        dev = 4 * x + 2 * y + c
        local = [pltpu.make_async_copy(small_in.at[me], small_out.at[dev], lsem.at[0]),
                 pltpu.make_async_copy(rep_in, rep_out.at[dev], lsem.at[1])]
        for cp in local:
            cp.start()

        def peer(r):
            return (1 - x if r & 4 else x), (1 - y if r & 2 else y), (1 - c if r & 1 else c)

        def tiny(r, which):
            px, py, pc = peer(r)
            k = (r - 1) * 2 + which
            if which == 0:
                return pltpu.make_async_remote_copy(small_in.at[2 * px + py], small_out.at[dev], ssend.at[k],
                                                    srecv.at[k], device_id=(px, py, pc), device_id_type=MESH)
            return pltpu.make_async_remote_copy(rep_in, rep_out.at[dev], ssend.at[k], srecv.at[k],
                                                device_id=(px, py, pc), device_id_type=MESH)

        def tiny_landed(r, which):
            px, py, pc = peer(r)
            k = (r - 1) * 2 + which
            dst = (small_out if which == 0 else rep_out).at[4 * px + 2 * py + pc]
            return pltpu.make_async_remote_copy(dst, dst, ssend.at[k], srecv.at[k], device_id=(px, py, pc),
                                                device_id_type=MESH)

        sends = [tiny(r, w) for r in range(1, NDEV) for w in range(2)]
        for cp in sends:
            cp.start()
        for r in range(1, NDEV):
            for w in range(2):
                tiny_landed(r, w).wait_recv()
        for cp in sends:
            cp.wait_send()
        for cp in local:
            cp.wait()

    dma = pltpu.SemaphoreType.DMA
    return pl.pallas_call(
        body, in_specs=[ANY] * 2, out_specs=[ANY] * 2,
        out_shape=[_sds((NDEV,) + small.shape[1:], F32), _sds((NDEV,) + rep.shape, F32)],
        scratch_shapes=[dma((2,)), dma((2 * (NDEV - 1),)), dma((2 * (NDEV - 1),))], name="exchange_small_grads",
    )(small, rep)


def _adamw_math(w, g, m, v):
    m = B1 * m + (1.0 - B1) * g
    v = B2 * v + (1.0 - B2) * (g * g)
    m_hat = m / (1.0 - B1 ** STEP)
    v_hat = v / (1.0 - B2 ** STEP)
    return -LR * (m_hat / (jnp.sqrt(v_hat) + AEPS) + WD * w), m, v


def _adamw_big(name, w, m, v, parts, row0=0):
    n, _, rows, cols = w.shape
    tr = _row_tile(rows)
    t0 = row0 // tr

    def body(w_ref, m_ref, v_ref, p_ref, g_ref, d_ref, nm_ref, nv_ref):
        g = p_ref[0].astype(F32)
        for q in range(1, NCHIP):
            g = g + p_ref[q].astype(F32)
        d, nm, nv = _adamw_math(w_ref[...], g, m_ref[...], v_ref[...])
        g_ref[...], d_ref[...], nm_ref[...], nv_ref[...] = g, d, nm, nv

    spec = pl.BlockSpec((None, None, tr, cols), lambda i, p, t: (i, p, t, 0))
    return pl.pallas_call(
        body, grid=(n, 2, rows // tr),
        in_specs=[spec, spec, spec,
                  pl.BlockSpec((None, NCHIP, None, tr, cols), lambda i, p, t: (p, 0, i, t0 + t, 0))],
        out_specs=[spec] * 4, out_shape=[_sds(w.shape)] * 4, name=name,
        compiler_params=_cp("parallel", "parallel", "parallel"),
    )(w, m, v, parts)


def _adamw_small(name, w, m, v, parts):
    def body(w_ref, m_ref, v_ref, p_ref, g_ref, d_ref, nm_ref, nv_ref):
        g = p_ref[0]
        for q in range(1, NDEV):
            g = g + p_ref[q]
        d, nm, nv = _adamw_math(w_ref[...], g, m_ref[...], v_ref[...])
        g_ref[...], d_ref[...], nm_ref[...], nv_ref[...] = g, d, nm, nv

    return pl.pallas_call(body, out_shape=[_sds(w.shape)] * 4, name=name)(w, m, v, parts)


def _pack(arrs, rows):
    flat = jnp.concatenate([a.reshape(-1) for a in arrs])
    return jnp.pad(flat, (0, rows * LANES - flat.shape[0])).reshape(rows, LANES)


def _unpack(packed, shapes):
    flat, out, o = packed.reshape(-1), [], 0
    for s in shapes:
        n = 1
        for d in s:
            n *= d
        out.append(flat[o:o + n].reshape(s))
        o += n
    return out


SMALL_ROWS, REP_ROWS = 200, 16


def kernel(x, norm_w, ffn_w_gate, ffn_w_up, ffn_w_down, mix_w_in, dn_conv_w, attn_sinks, dn_a_log, dn_dt_bias, dn_norm_w, mix_w_out, conv_w_pw1, conv_b_pw1, conv_w_dw, conv_b_dw, conv_ln_w, conv_ln_b, conv_w_pw2, conv_b_pw2, final_norm_w, loss_target, m_norm_w, m_ffn_w_gate, m_ffn_w_up, m_ffn_w_down, m_mix_w_in, m_dn_conv_w, m_attn_sinks, m_dn_a_log, m_dn_dt_bias, m_dn_norm_w, m_mix_w_out, m_conv_w_pw1, m_conv_b_pw1, m_conv_w_dw, m_conv_b_dw, m_conv_ln_w, m_conv_ln_b, m_conv_w_pw2, m_conv_b_pw2, m_final_norm_w, v_norm_w, v_ffn_w_gate, v_ffn_w_up, v_ffn_w_down, v_mix_w_in, v_dn_conv_w, v_attn_sinks, v_dn_a_log, v_dn_dt_bias, v_dn_norm_w, v_mix_w_out, v_conv_w_pw1, v_conv_b_pw1, v_conv_w_dw, v_conv_b_dw, v_conv_ln_w, v_conv_ln_b, v_conv_w_pw2, v_conv_b_pw2, v_final_norm_w):
    small_names = ["norm_w", "dn_conv_w", "conv_b_pw1", "conv_w_dw", "conv_b_dw", "conv_ln_w", "conv_ln_b",
                   "conv_b_pw2"]
    rep_names = ["attn_sinks", "dn_a_log", "dn_dt_bias", "dn_norm_w", "final_norm_w"]
    w = dict(norm_w=norm_w, ffn_w_gate=ffn_w_gate, ffn_w_up=ffn_w_up, ffn_w_down=ffn_w_down, mix_w_in=mix_w_in, dn_conv_w=dn_conv_w, attn_sinks=attn_sinks, dn_a_log=dn_a_log, dn_dt_bias=dn_dt_bias, dn_norm_w=dn_norm_w, mix_w_out=mix_w_out, conv_w_pw1=conv_w_pw1, conv_b_pw1=conv_b_pw1, conv_w_dw=conv_w_dw, conv_b_dw=conv_b_dw, conv_ln_w=conv_ln_w, conv_ln_b=conv_ln_b, conv_w_pw2=conv_w_pw2, conv_b_pw2=conv_b_pw2, final_norm_w=final_norm_w)
    m = dict(norm_w=m_norm_w, ffn_w_gate=m_ffn_w_gate, ffn_w_up=m_ffn_w_up, ffn_w_down=m_ffn_w_down, mix_w_in=m_mix_w_in, dn_conv_w=m_dn_conv_w, attn_sinks=m_attn_sinks, dn_a_log=m_dn_a_log, dn_dt_bias=m_dn_dt_bias, dn_norm_w=m_dn_norm_w, mix_w_out=m_mix_w_out, conv_w_pw1=m_conv_w_pw1, conv_b_pw1=m_conv_b_pw1, conv_w_dw=m_conv_w_dw, conv_b_dw=m_conv_b_dw, conv_ln_w=m_conv_ln_w, conv_ln_b=m_conv_ln_b, conv_w_pw2=m_conv_w_pw2, conv_b_pw2=m_conv_b_pw2, final_norm_w=m_final_norm_w)
    v = dict(norm_w=v_norm_w, ffn_w_gate=v_ffn_w_gate, ffn_w_up=v_ffn_w_up, ffn_w_down=v_ffn_w_down, mix_w_in=v_mix_w_in, dn_conv_w=v_dn_conv_w, attn_sinks=v_attn_sinks, dn_a_log=v_dn_a_log, dn_dt_bias=v_dn_dt_bias, dn_norm_w=v_dn_norm_w, mix_w_out=v_mix_w_out, conv_w_pw1=v_conv_w_pw1, conv_b_pw1=v_conv_b_pw1, conv_w_dw=v_conv_w_dw, conv_b_dw=v_conv_b_dw, conv_ln_w=v_conv_ln_w, conv_ln_b=v_conv_ln_b, conv_w_pw2=v_conv_w_pw2, conv_b_pw2=v_conv_b_pw2, final_norm_w=v_final_norm_w)
    order = ["norm_w", "ffn_w_gate", "ffn_w_up", "ffn_w_down", "mix_w_in", "dn_conv_w", "attn_sinks", "dn_a_log",
             "dn_dt_bias", "dn_norm_w", "mix_w_out", "conv_w_pw1", "conv_b_pw1", "conv_w_dw", "conv_b_dw",
             "conv_ln_w", "conv_ln_b", "conv_w_pw2", "conv_b_pw2", "final_norm_w"]

    small_shapes = [w[n].shape for n in small_names]
    rep_shapes = [w[n].shape for n in rep_names]

    def halves(a):
        return a.reshape(a.shape[:-2] + (2, a.shape[-2] // 2, a.shape[-1]))

    def layer_shards(l):
        mix_in, mix_out = (mix_w_in, mix_w_out) if l % 2 == 0 else (conv_w_pw1, conv_w_pw2)
        return [t.astype(BF16) for t in (jnp.concatenate([ffn_w_gate[l], ffn_w_up[l]], axis=1), ffn_w_down[l],
                                         halves(mix_in[l // 2]), halves(mix_out[l // 2]))]

    gathering = [_gather_async(f"gather_layer{l}", layer_shards(l),
                               [_pack([w[n] for n in small_names], SMALL_ROWS)] if l == 0 else [])
                 for l in range(DEPTH)]

    def mixer_params(l, w_a, w_b):
        e = l // 2
        w_a = w_a.transpose(1, 2, 0, 3).reshape(D, -1)
        w_b = w_b.reshape(D, D)
        if l % 2 == 0:
            return dict(w_in=w_a, dn_conv_w=sm["dn_conv_w"][e], sinks=_row(attn_sinks[e]), a_log=_row(dn_a_log[e]),
                        dt_bias=_row(dn_dt_bias[e]), dn_norm_w=_row(dn_norm_w[e]), wo_a=w_b[:Q_A], wo_b=w_b[Q_A:])
        return dict(b1a=_row(sm["conv_b_pw1"][e, :D]), b1b=_row(sm["conv_b_pw1"][e, D:]), w1a=w_a[:, :D],
                    w1b=w_a[:, D:], w_dw=sm["conv_w_dw"][e], b_dw=_row(sm["conv_b_dw"][e]),
                    ln_w=_row(sm["conv_ln_w"][e]), ln_b=_row(sm["conv_ln_b"][e]), b2=_row(sm["conv_b_pw2"][e]),
                    w2=w_b)

    xs, saved, ffn_w = x[0], [], []
    for l in range(DEPTH):
        got = [r[...] for r in gathering[l]]
        if l == 0:
            per_chip = [_unpack(got[4][q], small_shapes) for q in range(NCHIP)]
            sm = {n: jnp.concatenate([per_chip[q][i] for q in range(NCHIP)], axis=-1)
                  for i, n in enumerate(small_names)}
        else:
            xs, got = lax.optimization_barrier((xs, got))
        ffn_w.append(got[:2])
        xs, sv = _layer_fwd(l, xs, sm["norm_w"][l], got[:2], mixer_params(l, got[2], got[3]))
        saved.append(sv)
    loss, dx, dfw = _final("final", xs, _row(final_norm_w), loss_target[0])

    hbm = pltpu.MemorySpace.HBM
    sum_shapes = dict(gu=(DEPTH, 2 * D, FS), down=(DEPTH, FS, D), w_in=(2, D // 2, IN_COLS // NCHIP),
                      w_out=(2, D // 8, D), pw1=(2, D // 2, D // 2), pw2=(2, D // 8, D))
    sums = {k: jax.empty_ref(_sds((2, NCHIP) + s, BF16), memory_space=hbm) for k, s in sum_shapes.items()}
    c_arr = lax.axis_index("c").astype(jnp.int32).reshape(1)
    dnorm, gmix = [None] * DEPTH, [None] * DEPTH

    def hand_on(l, grads, swapped):
        def run(dx):
            dx, other = lax.optimization_barrier((dx, [r[...] for r in swapped]))
            parts = [_add_half(f"add_half_{l}_{k}", gg, rr, c_arr) for k, (gg, rr) in enumerate(zip(grads, other))]
            dx, parts = lax.optimization_barrier((dx, parts))
            keys = ("gu", "down", "w_in", "w_out") if l % 2 == 0 else ("gu", "down", "pw1", "pw2")
            _scatter_async(f"scatter_grads_{l}", parts, [sums[k] for k in keys], [l, l, l // 2, l // 2])
            return dx
        return run

    pending = lambda dx: dx
    for l in reversed(range(DEPTH)):
        dx, dnorm[l], dffn, gmix[l] = _layer_bwd(l, dx, sm["norm_w"][l], ffn_w[l], saved[l], pending)
        if l % 2 == 0:
            g_a, g_b = gmix[l]["w_in"], jnp.concatenate([gmix[l]["wo_a"], gmix[l]["wo_b"]], axis=0)
        else:
            g_a, g_b = jnp.concatenate([gmix[l]["w1a"], gmix[l]["w1b"]], axis=1), gmix[l]["w2"]
        g_a = g_a.reshape(2, D // 2, NCHIP, -1).transpose(2, 0, 1, 3).astype(BF16)
        g_b = g_b.reshape(NCHIP, 2, D // 8, D).astype(BF16)
        dx, grads = lax.optimization_barrier((dx, [dffn[0], dffn[1], g_a, g_b]))
        pending = hand_on(l, grads, _swap_halves(f"swap_grads_{l}", grads))
    dx = pending(dx)
    gm, gc = [gmix[0], gmix[2]], [gmix[1], gmix[3]]
    small_g = dict(
        norm_w=jnp.stack(dnorm), dn_conv_w=jnp.stack([gm[e]["dn_conv_w"] for e in range(2)]),
        conv_b_pw1=jnp.stack([jnp.concatenate([gc[e]["b1a"], gc[e]["b1b"]], axis=1)[0] for e in range(2)]),
        conv_w_dw=jnp.stack([gc[e]["w_dw"] for e in range(2)]),
        conv_b_dw=jnp.stack([gc[e]["b_dw"][0] for e in range(2)]),
        conv_ln_w=jnp.stack([gc[e]["ln_w"][0] for e in range(2)]),
        conv_ln_b=jnp.stack([gc[e]["ln_b"][0] for e in range(2)]),
        conv_b_pw2=jnp.stack([gc[e]["b2"][0] for e in range(2)]))
    small_by_chip = jnp.stack([_pack([jnp.split(small_g[n], NCHIP, axis=-1)[q] for n in small_names], SMALL_ROWS)
                               for q in range(NCHIP)])
    rep_g = _pack([jnp.stack([gm[e]["sinks"][0] for e in range(2)]), jnp.stack([gm[e]["a_log"][0] for e in range(2)]),
                   jnp.stack([gm[e]["dt_bias"][0] for e in range(2)]),
                   jnp.stack([gm[e]["dn_norm_w"][0] for e in range(2)]), dfw[0]], REP_ROWS)
    small_sum, rep_sum = _exchange_small(small_by_chip, rep_g)

    res = {}
    partial_sums = {k: r[...] for k, r in sums.items()}
    for n, key, row0 in (("ffn_w_gate", "gu", 0), ("ffn_w_up", "gu", D), ("ffn_w_down", "down", 0),
                         ("mix_w_in", "w_in", 0), ("mix_w_out", "w_out", 0), ("conv_w_pw1", "pw1", 0),
                         ("conv_w_pw2", "pw2", 0)):
        view = (lambda a: a) if w[n].ndim == 4 else halves
        outs = _adamw_big(f"adamw_{n}", view(w[n]), view(m[n]), view(v[n]), partial_sums[key], row0)
        res[n] = [o.reshape(w[n].shape) for o in outs]
    outs = _adamw_small("adamw_small", *[_pack([d[n] for n in small_names], SMALL_ROWS) for d in (w, m, v)],
                        small_sum)
    for i, n in enumerate(small_names):
        res[n] = [_unpack(o, small_shapes)[i] for o in outs]
    outs = _adamw_small("adamw_replicated", *[_pack([d[n] for n in rep_names], REP_ROWS) for d in (w, m, v)],
                        rep_sum)
    for i, n in enumerate(rep_names):
        res[n] = [_unpack(o, rep_shapes)[i] for o in outs]

    total = lax.psum(loss[0, 0], ("x", "y", "c"))
    return (total, dx[None], *[res[n][0] for n in order], *[res[n][1] for n in order],
            *[res[n][2] for n in order], *[res[n][3] for n in order])
```

```python
import jax
import jax.numpy as jnp
from jax import lax
from jax.experimental import pallas as pl
from jax.experimental.pallas import tpu as pltpu
from jax.experimental.pallas import tpu_sc as plsc

F32, BF16 = jnp.float32, jnp.bfloat16
MESH = pl.DeviceIdType.MESH
ANY = pl.BlockSpec(memory_space=pl.ANY)

T, D, F = 2048, 1024, 2816
DEPTH = 4
EPS = 1e-6
HEADS, HDIM, KV_HEADS, GROUP = 8, 64, 2, 4
WINDOW = BLOCK = 128
CHUNK = 64
NCHUNK = T // CHUNK
DN_CONV, CONV_WIDTH = 4, 31
Q_A, KV_A, QKV_B, V_B = 512, 128, 1536, 512
IN_COLS = 2832
IN_SPLITS = (0, 512, 640, 768, 2304, 2816, 2832)
NCHIP, NDEV = 4, 8
FS = F // NCHIP
LR, B1, B2, AEPS, WD, STEP = 0.001, 0.9, 0.999, 1e-08, 0.01, 10
V7X_VMEM_BYTES = 64 * 1024 * 1024
VMEM_LIMIT = V7X_VMEM_BYTES * 7 // 8
LANES = 128


def _cp(*sem):
    return pltpu.CompilerParams(dimension_semantics=sem, vmem_limit_bytes=VMEM_LIMIT)


def _sds(shape, dtype=F32):
    return jax.ShapeDtypeStruct(tuple(shape), dtype)


def _full(shape):
    nd = len(shape)
    return pl.BlockSpec(tuple(shape), lambda *_: (0,) * nd)


def _split_bf16(a):
    hi = a.astype(BF16)
    return hi, (a - hi.astype(F32)).astype(BF16)


def _dg(a, b, ca, cb, hi=False):
    if a.ndim == 3 and b.ndim == 3:
        dims = (((ca + 1,), (cb + 1,)), ((0,), (0,)))
    else:
        dims = (((ca,), (cb,)), ((), ()))
    dot = lambda p, q: lax.dot_general(p, q, dims, preferred_element_type=F32)
    if hi:
        a_hi, a_lo = _split_bf16(a.astype(F32))
        b_hi, b_lo = _split_bf16(b.astype(F32))
        return dot(a_hi, b_hi) + (dot(a_hi, b_lo) + dot(a_lo, b_hi))
    return dot(a.astype(BF16), b.astype(BF16))


def _make_mm(hi):
    @jax.custom_vjp
    def nn(a, b):
        return _dg(a, b, 1, 0, hi)

    @jax.custom_vjp
    def nt(a, b):
        return _dg(a, b, 1, 1, hi)

    @jax.custom_vjp
    def tn(a, b):
        return _dg(a, b, 0, 0, hi)

    nn.defvjp(lambda a, b: (_dg(a, b, 1, 0, hi), (a, b)),
              lambda r, g: (_dg(g, r[1], 1, 1, hi).astype(r[0].dtype), _dg(r[0], g, 0, 0, hi).astype(r[1].dtype)))
    nt.defvjp(lambda a, b: (_dg(a, b, 1, 1, hi), (a, b)),
              lambda r, g: (_dg(g, r[1], 1, 0, hi).astype(r[0].dtype), _dg(g, r[0], 0, 0, hi).astype(r[1].dtype)))
    tn.defvjp(lambda a, b: (_dg(a, b, 0, 0, hi), (a, b)),
              lambda r, g: (_dg(r[1], g, 1, 1, hi).astype(r[0].dtype), _dg(r[0], g, 1, 0, hi).astype(r[1].dtype)))
    return nn, nt, tn


_nn, _nt, _tn = _make_mm(False)
_nn_hi, _nt_hi, _tn_hi = _make_mm(True)


def _rms(x, w):
    return x * lax.rsqrt(jnp.mean(x * x, axis=-1, keepdims=True) + EPS) * w


def _layernorm(x, w, b):
    xc = x - jnp.mean(x, axis=-1, keepdims=True)
    return xc * lax.rsqrt(jnp.mean(xc * xc, axis=-1, keepdims=True) + EPS) * w + b


def _silu(x):
    return x * jax.nn.sigmoid(x)


def _iota2(shape, dim):
    return lax.broadcasted_iota(jnp.int32, shape, dim)


def _flat_weights(lhs_idx, weights):
    specs, ops, lhs_of, where = [], [], [], []
    for a, (k, w) in enumerate(zip(lhs_idx, weights)):
        for q in range(1 if w.ndim == 2 else w.shape[0]):
            specs.append(_full(w.shape) if w.ndim == 2
                         else pl.BlockSpec((None,) + w.shape[1:], lambda i, q=q: (q, 0, 0)))
            ops.append(w)
            lhs_of.append(k)
            where.append((a, None if w.ndim == 2 else q))
    return specs, ops, lhs_of, where


def _blk_fwd(name, pre, lhs_idx, post, toks, smalls, weights, outs, tm=512):
    wspecs, wops, lhs_of, _ = _flat_weights(lhs_idx, weights)
    nt_, ns, nw = len(toks), len(smalls), len(wops)

    def body(*refs):
        tv = [r[...] for r in refs[:nt_]]
        sv = [r[...] for r in refs[nt_:nt_ + ns]]
        wr = refs[nt_ + ns:nt_ + ns + nw]
        orf = refs[nt_ + ns + nw:]
        lhs = pre(tv, sv)
        ys = [_dg(lhs[i], w[...], 1, 0) for i, w in zip(lhs_of, wr)]
        for o_ref, o in zip(orf, post(ys, tv, sv)):
            o_ref[...] = o.astype(o_ref.dtype)

    in_specs = ([pl.BlockSpec((tm, a.shape[1]), lambda i: (i, 0)) for a in toks]
                + [_full(a.shape) for a in smalls] + wspecs)
    out_specs = [pl.BlockSpec((tm, w_), lambda i: (i, 0)) for w_, _ in outs]
    return pl.pallas_call(
        body, grid=(T // tm,), in_specs=in_specs, out_specs=out_specs,
        out_shape=[_sds((T, w_), dt) for w_, dt in outs], name=name, compiler_params=_cp("parallel"),
    )(*toks, *smalls, *wops)


def _blk_bwd(name, pre, lhs_idx, post, toks, smalls, weights, ct_groups, res=None, tm=256, wchunk=512):
    wspecs, wops, lhs_of, where = _flat_weights(lhs_idx, weights)
    nt_, ns, nw, na = len(toks), len(smalls), len(wops), len(weights)
    cts = [a for g in ct_groups for a in g]
    nc = len(cts)
    widths = [sum(a.shape[1] for a in g) for g in ct_groups]
    has_res = res is not None

    def body(*refs):
        p = 0
        tr = refs[p:p + nt_]; p += nt_
        sr = refs[p:p + ns]; p += ns
        wr = refs[p:p + nw]; p += nw
        cr = refs[p:p + nc]; p += nc
        rr = refs[p:p + has_res]; p += has_res
        dtr = refs[p:p + nt_]; p += nt_
        dsr = refs[p:p + ns]; p += ns
        dwr = refs[p:p + na]; p += na
        scr = refs[p:]
        i = pl.program_id(0)

        @pl.when(i == 0)
        def _():
            for r in list(dsr) + list(dwr):
                r[...] = jnp.zeros_like(r)

        tv = [r[...] for r in tr]
        sv = [r[...] for r in sr]
        ctv, q, si = [], 0, 0
        for g in ct_groups:
            if len(g) == 1:
                ctv.append(cr[q][...].astype(F32))
            else:
                off = 0
                for j, a in enumerate(g):
                    scr[si][:, off:off + a.shape[1]] = cr[q + j][...].astype(F32)
                    off += a.shape[1]
                ctv.append(scr[si][...])
                si += 1
            q += len(g)

        lhs, vjp_pre = jax.vjp(lambda *a: tuple(pre(list(a[:nt_]), list(a[nt_:]))), *tv, *sv)
        lhs_b = [l.astype(BF16) for l in lhs]
        ys = [_dg(lhs_b[k], w[...], 1, 0) for k, w in zip(lhs_of, wr)]
        _, vjp_post = jax.vjp(lambda *a: tuple(post(list(a[:nw]), list(a[nw:nw + nt_]), list(a[nw + nt_:]))),
                              *ys, *tv, *sv)
        gp = vjp_post(tuple(ctv))
        dys, dt_post, ds_post = gp[:nw], gp[nw:nw + nt_], gp[nw + nt_:]
        dlhs = [None] * len(lhs)
        for k, w, dy, (a, q) in zip(lhs_of, wr, dys, where):
            dyb = dy.astype(BF16)
            n = w.shape[1]
            for c0 in range(0, n, wchunk):
                c1 = min(n, c0 + wchunk)
                part = _dg(lhs_b[k], dyb[:, c0:c1], 0, 0)
                if q is None:
                    dwr[a][:, c0:c1] += part
                else:
                    dwr[a][q, :, c0:c1] += part
            d = _dg(dyb, w[...], 1, 1)
            dlhs[k] = d if dlhs[k] is None else dlhs[k] + d
        gq = vjp_pre(tuple(d.astype(l.dtype) for d, l in zip(dlhs, lhs)))
        dt_pre, ds_pre = gq[:nt_], gq[nt_:]
        for j in range(nt_):
            d = dt_post[j] + dt_pre[j]
            if j == 0 and has_res:
                d = d + rr[0][...]
            dtr[j][...] = d
        for j in range(ns):
            dsr[j][...] += ds_post[j] + ds_pre[j]

    tok_spec = lambda a: pl.BlockSpec((tm, a.shape[1]), lambda i: (i, 0))
    in_specs = ([tok_spec(a) for a in toks] + [_full(a.shape) for a in smalls] + wspecs
                + [tok_spec(a) for a in cts] + ([tok_spec(res)] if has_res else []))
    out_specs = [tok_spec(a) for a in toks] + [_full(a.shape) for a in smalls] + [_full(w.shape) for w in weights]
    out_shape = ([_sds(a.shape) for a in toks] + [_sds(a.shape) for a in smalls] + [_sds(w.shape) for w in weights])
    scratch = [pltpu.VMEM((tm, wd), F32) for g, wd in zip(ct_groups, widths) if len(g) > 1]
    outs = pl.pallas_call(
        body, grid=(T // tm,), in_specs=in_specs, out_specs=out_specs, out_shape=out_shape,
        scratch_shapes=scratch, name=name, compiler_params=_cp("arbitrary"),
    )(*toks, *smalls, *wops, *cts, *([res] if has_res else []))
    return outs[:nt_], outs[nt_:nt_ + ns], outs[nt_ + ns:]


def _ffn_fwd(name, x, nw, gu, wd, idx, tm=512):
    def body(x_ref, nw_ref, wg_ref, wu_ref, wd_ref, o_ref, h_scr):
        s = pl.program_id(1)

        @pl.when(s == 0)
        def _():
            xv = x_ref[...]
            h_scr[...] = _rms(xv, nw_ref[...]).astype(BF16)
            o_ref[...] = xv

        h = h_scr[...]
        a = _dg(h, wg_ref[...], 1, 0)
        b = _dg(h, wu_ref[...], 1, 0)
        o_ref[...] += 0.5 * _dg(_silu(a) * b, wd_ref[...], 1, 0)

    wspec = lambda r, c, k: pl.BlockSpec((None, None, r, c), lambda i, s: (s, idx, k, 0))
    return pl.pallas_call(
        body, grid=(T // tm, NCHIP),
        in_specs=[pl.BlockSpec((tm, D), lambda i, s: (i, 0)), _full((1, D)), wspec(D, FS, 0), wspec(D, FS, 1),
                  wspec(FS, D, 0)],
        out_specs=pl.BlockSpec((tm, D), lambda i, s: (i, 0)), out_shape=_sds((T, D)),
        scratch_shapes=[pltpu.VMEM((tm, D), BF16)], name=name, compiler_params=_cp("parallel", "arbitrary"),
    )(x, nw, gu, gu, wd)


def _ffn_bwd(name, x, nw, gu, wd, idx, dy, gbufs=None, tm=512):
    ni = T // tm

    def body(x_ref, dy_ref, nw_ref, wg_ref, wu_ref, wd_ref, dx_ref, dnw_ref, dgu_ref, dwd_ref,
             dh_acc, ag, au, ad):
        s, i = pl.program_id(0), pl.program_id(1)
        rows = pl.ds(pl.multiple_of(i * tm, tm), tm)

        @pl.when((s == 0) & (i == 0))
        def _():
            dnw_ref[...] = jnp.zeros_like(dnw_ref)

        @pl.when(i == 0)
        def _():
            ag[...] = jnp.zeros_like(ag)
            au[...] = jnp.zeros_like(au)
            ad[...] = jnp.zeros_like(ad)

        xv, nwv, dyv = x_ref[...], nw_ref[...], dy_ref[...]
        h, vjp_rms = jax.vjp(_rms, xv, nwv)
        hb = h.astype(BF16)
        a = _dg(hb, wg_ref[...], 1, 0)
        b = _dg(hb, wu_ref[...], 1, 0)
        sa = jax.nn.sigmoid(a)
        act = a * sa
        dyb = (0.5 * dyv).astype(BF16)
        ad[...] += _dg(act * b, dyb, 0, 0)
        dact = _dg(dyb, wd_ref[...], 1, 1)
        da = (dact * b * (sa * (1.0 + a * (1.0 - sa)))).astype(BF16)
        db = (dact * act).astype(BF16)
        ag[...] += _dg(hb, da, 0, 0)
        au[...] += _dg(hb, db, 0, 0)
        dh = _dg(da, wg_ref[...], 1, 1) + _dg(db, wu_ref[...], 1, 1)

        @pl.when(s == 0)
        def _():
            dh_acc[rows, :] = dh

        @pl.when(s > 0)
        def _():
            dh_acc[rows, :] += dh

        @pl.when(s == NCHIP - 1)
        def _():
            dx, dnw = vjp_rms(dh_acc[rows, :])
            dx_ref[...] = dyv + dx
            dnw_ref[...] += dnw

        @pl.when(i == ni - 1)
        def _():
            dgu_ref[0:D, :] = ag[...].astype(BF16)
            dgu_ref[D:, :] = au[...].astype(BF16)
            dwd_ref[...] = ad[...].astype(BF16)

    wspec = lambda r, c, k: pl.BlockSpec((None, None, r, c), lambda s, i: (s, idx, k, 0),
                                         pipeline_mode=pl.Buffered(1))
    last = lambda s, i: (jnp.where(s == NCHIP - 1, i, 0), 0)
    nb = 0 if gbufs is None else 2
    return pl.pallas_call(
        lambda *refs: body(*refs[:6], *refs[6 + nb:]), grid=(NCHIP, ni),
        in_specs=[pl.BlockSpec((tm, D), lambda s, i: (i, 0)), pl.BlockSpec((tm, D), lambda s, i: (i, 0)),
                  _full((1, D)), wspec(D, FS, 0), wspec(D, FS, 1), wspec(FS, D, 0)] + [ANY] * nb,
        out_specs=[pl.BlockSpec((tm, D), last), _full((1, D)), wspec(2 * D, FS, 0), wspec(FS, D, 0)],
        out_shape=[_sds((T, D)), _sds((1, D)), _sds(gu.shape, BF16), _sds(wd.shape, BF16)],
        input_output_aliases={6 + k: 2 + k for k in range(nb)},
        scratch_shapes=[pltpu.VMEM((T, D), F32), pltpu.VMEM((D, FS), F32), pltpu.VMEM((D, FS), F32),
                        pltpu.VMEM((FS, D), F32)],
        name=name, compiler_params=_cp("arbitrary", "arbitrary"),
    )(x, dy, nw, gu, gu, wd, *(gbufs or ()))


CONV_ROWS = 256


def _conv_pad(k):
    return 8 * ((k - 1 + 7) // 8)


def _conv_fwd(name, x, w, b, act):
    k_w, c = w.shape
    tc = 256 if c % 256 == 0 else LANES
    pad = _conv_pad(k_w)
    has_b = b is not None

    def body(*refs):
        x_ref, w_ref = refs[0], refs[1]
        b_ref = refs[2] if has_b else None
        y_ref, xp = refs[2 + has_b], refs[3 + has_b]
        xp[0:pad, :] = jnp.zeros((pad, tc), F32)
        xp[pad:, :] = x_ref[...]

        def step(t, carry):
            base = pl.multiple_of(t * CONV_ROWS, CONV_ROWS)
            win = xp[pl.ds(base, CONV_ROWS + pad), :]
            acc = jnp.zeros((CONV_ROWS, tc), F32)
            for k in range(k_w):
                o = pad - (k_w - 1) + k
                acc = acc + w_ref[k:k + 1, :] * win[o:o + CONV_ROWS, :]
            if has_b:
                acc = acc + b_ref[...]
            y_ref[pl.ds(base, CONV_ROWS), :] = _silu(acc) if act else acc
            return carry

        lax.fori_loop(0, T // CONV_ROWS, step, 0)

    col = lambda r: pl.BlockSpec((r, tc), lambda j: (0, j))
    ins = [x, w] + ([b] if has_b else [])
    return pl.pallas_call(
        body, grid=(c // tc,), in_specs=[col(T), col(k_w)] + ([col(1)] if has_b else []), out_specs=col(T),
        out_shape=_sds((T, c)), scratch_shapes=[pltpu.VMEM((T + pad, tc), F32)], name=name,
        compiler_params=_cp("parallel"),
    )(*ins)


def _conv_bwd(name, x, w, b, act, dy):
    k_w, c = w.shape
    tc = 256 if c % 256 == 0 else LANES
    pad = _conv_pad(k_w)
    has_b = b is not None

    def body(*refs):
        x_ref, w_ref, dy_ref = refs[0], refs[1], refs[2]
        b_ref = refs[3] if has_b else None
        dx_ref, dw_ref, db_ref, xp, dp = refs[3 + has_b:]
        xp[0:pad, :] = jnp.zeros((pad, tc), F32)
        xp[pad:, :] = x_ref[...]
        dp[T:, :] = jnp.zeros((pad, tc), F32)
        dw_ref[...] = jnp.zeros_like(dw_ref)
        db_ref[...] = jnp.zeros_like(db_ref)

        def step1(t, carry):
            base = pl.multiple_of(t * CONV_ROWS, CONV_ROWS)
            d = dy_ref[pl.ds(base, CONV_ROWS), :]
            win = xp[pl.ds(base, CONV_ROWS + pad), :]
            offs = [pad - (k_w - 1) + k for k in range(k_w)]
            if act:
                acc = jnp.zeros((CONV_ROWS, tc), F32)
                for k, o in enumerate(offs):
                    acc = acc + w_ref[k:k + 1, :] * win[o:o + CONV_ROWS, :]
                if has_b:
                    acc = acc + b_ref[...]
                sg = jax.nn.sigmoid(acc)
                d = d * (sg * (1.0 + acc * (1.0 - sg)))
            dp[pl.ds(base, CONV_ROWS), :] = d
            for k, o in enumerate(offs):
                dw_ref[k:k + 1, :] += jnp.sum(d * win[o:o + CONV_ROWS, :], axis=0, keepdims=True)
            db_ref[...] += jnp.sum(d, axis=0, keepdims=True)
            return carry

        lax.fori_loop(0, T // CONV_ROWS, step1, 0)

        def step2(t, carry):
            base = pl.multiple_of(t * CONV_ROWS, CONV_ROWS)
            win = dp[pl.ds(base, CONV_ROWS + pad), :]
            acc = jnp.zeros((CONV_ROWS, tc), F32)
            for k in range(k_w):
                o = (k_w - 1) - k
                acc = acc + w_ref[k:k + 1, :] * win[o:o + CONV_ROWS, :]
            dx_ref[pl.ds(base, CONV_ROWS), :] = acc
            return carry

        lax.fori_loop(0, T // CONV_ROWS, step2, 0)

    col = lambda r: pl.BlockSpec((r, tc), lambda j: (0, j))
    ins = [x, w, dy] + ([b] if has_b else [])
    return pl.pallas_call(
        body, grid=(c // tc,), in_specs=[col(T), col(k_w), col(T)] + ([col(1)] if has_b else []),
        out_specs=[col(T), col(k_w), col(1)], out_shape=[_sds((T, c)), _sds((k_w, c)), _sds((1, c))],
        scratch_shapes=[pltpu.VMEM((T + pad, tc), F32), pltpu.VMEM((T + pad, tc), F32)], name=name,
        compiler_params=_cp("parallel"),
    )(*ins)


def _attn_consts(n):
    i = _iota2((BLOCK, 2 * BLOCK), 0)
    j = _iota2((BLOCK, 2 * BLOCK), 1)
    dist = i + BLOCK - j
    valid = (dist >= 0) & (dist < WINDOW) & ((n > 0) | (j >= BLOCK))
    return dist.astype(F32), valid


def _attn_block(q4, kk, vv, sinks, dist, valid, kv):
    outs = []
    lane = _iota2((1, HEADS), 1)
    for g in range(GROUP):
        h = kv * GROUP + g
        slope = 2.0 ** (-8.0 * (h + 1) / HEADS)
        s = _nt(q4[:, g * HDIM:(g + 1) * HDIM], kk) * (HDIM ** -0.5)
        s = jnp.where(valid, s - slope * dist, -1e30)
        sink = jnp.sum(jnp.where(lane == h, sinks, 0.0), axis=1, keepdims=True)
        m = jnp.maximum(jnp.max(s, axis=-1, keepdims=True), sink)
        e = jnp.exp(s - m)
        p = e / (jnp.sum(e, axis=-1, keepdims=True) + jnp.exp(sink - m))
        outs.append(_nn(p, vv))
    return tuple(outs)


def _attn_fwd(name, qa, ka, va, sinks):
    def body(q_ref, k_ref, v_ref, s_ref, o_ref, kp, vp):
        kp[0:BLOCK, :] = jnp.zeros((BLOCK, KV_A), F32)
        vp[0:BLOCK, :] = jnp.zeros((BLOCK, KV_A), F32)
        kp[BLOCK:, :] = k_ref[...]
        vp[BLOCK:, :] = v_ref[...]
        sinks_v = s_ref[...]

        def step(n, carry):
            r = pl.multiple_of(n * BLOCK, BLOCK)
            dist, valid = _attn_consts(n)
            k2 = kp[pl.ds(r, 2 * BLOCK), :]
            v2 = vp[pl.ds(r, 2 * BLOCK), :]
            for kv in range(KV_HEADS):
                q4 = q_ref[pl.ds(r, BLOCK), kv * GROUP * HDIM:(kv + 1) * GROUP * HDIM]
                og = _attn_block(q4, k2[:, kv * HDIM:(kv + 1) * HDIM], v2[:, kv * HDIM:(kv + 1) * HDIM], sinks_v,
                                 dist, valid, kv)
                for g in range(GROUP):
                    h = kv * GROUP + g
                    o_ref[pl.ds(r, BLOCK), h * HDIM:(h + 1) * HDIM] = og[g]
            return carry

        lax.fori_loop(0, T // BLOCK, step, 0)

    return pl.pallas_call(
        body, out_shape=_sds((T, Q_A)),
        scratch_shapes=[pltpu.VMEM((T + BLOCK, KV_A), F32), pltpu.VMEM((T + BLOCK, KV_A), F32)], name=name,
        compiler_params=pltpu.CompilerParams(vmem_limit_bytes=VMEM_LIMIT),
    )(qa, ka, va, sinks)


def _attn_bwd(name, qa, ka, va, sinks, do):
    def body(q_ref, k_ref, v_ref, s_ref, do_ref, dq_ref, dk_ref, dv_ref, ds_ref, kp, vp, dkp, dvp):
        kp[0:BLOCK, :] = jnp.zeros((BLOCK, KV_A), F32)
        vp[0:BLOCK, :] = jnp.zeros((BLOCK, KV_A), F32)
        kp[BLOCK:, :] = k_ref[...]
        vp[BLOCK:, :] = v_ref[...]
        dkp[...] = jnp.zeros_like(dkp)
        dvp[...] = jnp.zeros_like(dvp)
        ds_ref[...] = jnp.zeros_like(ds_ref)
        sinks_v = s_ref[...]

        def step(n, carry):
            r = pl.multiple_of(n * BLOCK, BLOCK)
            dist, valid = _attn_consts(n)
            k2 = kp[pl.ds(r, 2 * BLOCK), :]
            v2 = vp[pl.ds(r, 2 * BLOCK), :]
            for kv in range(KV_HEADS):
                cols = slice(kv * HDIM, (kv + 1) * HDIM)
                q4 = q_ref[pl.ds(r, BLOCK), kv * GROUP * HDIM:(kv + 1) * GROUP * HDIM]
                _, vjp = jax.vjp(lambda q, k, v, s: _attn_block(q, k, v, s, dist, valid, kv),
                                 q4, k2[:, cols], v2[:, cols], sinks_v)
                cts = tuple(do_ref[pl.ds(r, BLOCK), (kv * GROUP + g) * HDIM:(kv * GROUP + g + 1) * HDIM]
                            for g in range(GROUP))
                dq4, dkk, dvv, dsk = vjp(cts)
                dq_ref[pl.ds(r, BLOCK), kv * GROUP * HDIM:(kv + 1) * GROUP * HDIM] = dq4
                dkp[pl.ds(r, 2 * BLOCK), cols] += dkk
                dvp[pl.ds(r, 2 * BLOCK), cols] += dvv
                ds_ref[...] += dsk
            return carry

        lax.fori_loop(0, T // BLOCK, step, 0)
        dk_ref[...] = dkp[BLOCK:, :]
        dv_ref[...] = dvp[BLOCK:, :]

    pad = lambda: pltpu.VMEM((T + BLOCK, KV_A), F32)
    return pl.pallas_call(
        body, out_shape=[_sds((T, Q_A)), _sds((T, KV_A)), _sds((T, KV_A)), _sds((1, HEADS))],
        scratch_shapes=[pad(), pad(), pad(), pad()], name=name,
        compiler_params=pltpu.CompilerParams(vmem_limit_bytes=VMEM_LIMIT),
    )(qa, ka, va, sinks, do)


def _dn_consts():
    i = _iota2((CHUNK, CHUNK), 0)
    j = _iota2((CHUNK, CHUNK), 1)
    return dict(causal=i >= j, strict=i > j, eye=(i == j).astype(F32), ltri=(i >= j).astype(F32),
                ones=jnp.ones((CHUNK, CHUNK), F32), last=(_iota2((CHUNK, 1), 0) == CHUNK - 1).astype(F32))


def _l2norm(x):
    return x * lax.rsqrt(jnp.sum(x * x, axis=-1, keepdims=True) + EPS)


def _head_cols(m):
    lane = _iota2((1, HEADS), 1)
    return jnp.concatenate([jnp.sum(jnp.where(lane == h, m, 0.0), axis=1, keepdims=True)[None]
                            for h in range(HEADS)], axis=0)


def _dn_local(q3, k3, v3, braw, araw, alog, dtb, cs):
    q = _l2norm(q3) * (HDIM ** -0.5)
    k = _l2norm(k3)
    g = -jnp.exp(alog) * jax.nn.softplus(araw + dtb)
    gc_all = _nn_hi(cs["ltri"], g)
    egc_all = jnp.exp(gc_all)
    beta, gc, egc = _head_cols(jax.nn.sigmoid(braw)), _head_cols(gc_all), _head_cols(egc_all)
    a = jnp.broadcast_to(gc, (HEADS, CHUNK, CHUNK))
    diff = a - jnp.swapaxes(a, 1, 2)
    decay = jnp.where(cs["causal"], jnp.exp(jnp.where(cs["causal"], diff, 0.0)), 0.0)
    kb = k * beta
    low = jnp.where(cs["strict"], _nt(kb, k) * decay, 0.0)
    inv = cs["eye"] - low
    pw = low
    for _ in range(5):
        pw = _nn_hi(pw, pw)
        inv = inv + _nn_hi(inv, pw)
    u = _nn_hi(inv, v3 * beta)
    w = _nn_hi(inv, kb * egc)
    attn = _nt(q, k) * decay
    gc_last = jnp.sum(gc * cs["last"], axis=1, keepdims=True)
    return u, w, attn, q * egc, k * jnp.exp(gc_last - gc), egc_all


def _heads3(ref, off=0):
    return jnp.concatenate([ref[:, off + h * HDIM:off + (h + 1) * HDIM][None] for h in range(HEADS)], axis=0)


def _dn_local_fwd(name, qkv, ba, alog, dtb):
    def body(qkv_ref, ba_ref, al_ref, dt_ref, u_ref, w_ref, at_ref, qd_ref, kd_ref, eg_ref):
        bav = ba_ref[...]
        outs = _dn_local(_heads3(qkv_ref), _heads3(qkv_ref, 512), _heads3(qkv_ref, 1024), bav[:, :HEADS],
                         bav[:, HEADS:], al_ref[...], dt_ref[...], _dn_consts())
        for r, o in zip((u_ref, w_ref, at_ref, qd_ref, kd_ref), outs[:5]):
            for h in range(HEADS):
                r[:, h * HDIM:(h + 1) * HDIM] = o[h]
        eg_ref[...] = outs[5]

    row = lambda w_: pl.BlockSpec((CHUNK, w_), lambda n: (n, 0))
    return pl.pallas_call(
        body, grid=(NCHUNK,), in_specs=[row(QKV_B), row(2 * HEADS), _full((1, HEADS)), _full((1, HEADS))],
        out_specs=[row(V_B)] * 5 + [row(HEADS)], out_shape=[_sds((T, V_B))] * 5 + [_sds((T, HEADS))], name=name,
        compiler_params=_cp("parallel"),
    )(qkv, ba, alog, dtb)


def _dn_local_bwd(name, qkv, ba, alog, dtb, cts):
    def body(qkv_ref, ba_ref, al_ref, dt_ref, du_ref, dw_ref, dat_ref, dqd_ref, dkd_ref, deg_ref,
             dqkv_ref, dba_ref, dal_ref, ddt_ref):
        @pl.when(pl.program_id(0) == 0)
        def _():
            dal_ref[...] = jnp.zeros_like(dal_ref)
            ddt_ref[...] = jnp.zeros_like(ddt_ref)

        cs = _dn_consts()
        bav = ba_ref[...]
        _, vjp = jax.vjp(lambda *a: _dn_local(*a, cs), _heads3(qkv_ref), _heads3(qkv_ref, 512),
                         _heads3(qkv_ref, 1024), bav[:, :HEADS], bav[:, HEADS:], al_ref[...], dt_ref[...])
        dq, dk, dv, dbr, dar, dal, ddt = vjp((_heads3(du_ref), _heads3(dw_ref), _heads3(dat_ref), _heads3(dqd_ref),
                                              _heads3(dkd_ref), deg_ref[...]))
        for h in range(HEADS):
            dqkv_ref[:, h * HDIM:(h + 1) * HDIM] = dq[h]
            dqkv_ref[:, 512 + h * HDIM:512 + (h + 1) * HDIM] = dk[h]
            dqkv_ref[:, 1024 + h * HDIM:1024 + (h + 1) * HDIM] = dv[h]
        dba_ref[:, :HEADS] = dbr
        dba_ref[:, HEADS:] = dar
        dal_ref[...] += dal
        ddt_ref[...] += ddt

    row = lambda w_: pl.BlockSpec((CHUNK, w_), lambda n: (n, 0))
    return pl.pallas_call(
        body, grid=(NCHUNK,),
        in_specs=[row(QKV_B), row(2 * HEADS), _full((1, HEADS)), _full((1, HEADS))] + [row(V_B)] * 5 + [row(HEADS)],
        out_specs=[row(QKV_B), row(2 * HEADS), _full((1, HEADS)), _full((1, HEADS))],
        out_shape=[_sds((T, QKV_B)), _sds((T, 2 * HEADS)), _sds((1, HEADS)), _sds((1, HEADS))], name=name,
        compiler_params=_cp("arbitrary"),
    )(qkv, ba, alog, dtb, *cts)


def _dn_step(s, u, w, attn, qd, kd, egc, z, nw):
    last = (_iota2((CHUNK, 1), 0) == CHUNK - 1).astype(F32)
    gl = jnp.sum(_head_cols(egc) * last, axis=1, keepdims=True)
    v_new = u - _nn(w, s)
    o = _nn(qd, s) + _nn(attn, v_new)
    s_new = s * gl + _tn(kd, v_new)
    return s_new, _rms(o, nw) * _silu(z)


def _unheads(ref, v3):
    for h in range(HEADS):
        ref[:, h * HDIM:(h + 1) * HDIM] = v3[h]


def _dn_rec_fwd(name, u, w, attn, qd, kd, egc, z, nw):
    def body(u_ref, w_ref, at_ref, qd_ref, kd_ref, eg_ref, z_ref, nw_ref, o_ref, ss_ref, s_scr):
        @pl.when(pl.program_id(0) == 0)
        def _():
            s_scr[...] = jnp.zeros_like(s_scr)

        s = s_scr[...]
        ss_ref[...] = s
        s_new, on = _dn_step(s, _heads3(u_ref), _heads3(w_ref), _heads3(at_ref), _heads3(qd_ref), _heads3(kd_ref),
                             eg_ref[...], _heads3(z_ref), nw_ref[...])
        s_scr[...] = s_new
        _unheads(o_ref, on)

    row = lambda w_: pl.BlockSpec((CHUNK, w_), lambda n: (n, 0))
    return pl.pallas_call(
        body, grid=(NCHUNK,), in_specs=[row(V_B)] * 5 + [row(HEADS), row(V_B), _full((1, HDIM))],
        out_specs=[row(V_B), pl.BlockSpec((None, HEADS, HDIM, HDIM), lambda n: (n, 0, 0, 0))],
        out_shape=[_sds((T, V_B)), _sds((NCHUNK, HEADS, HDIM, HDIM))],
        scratch_shapes=[pltpu.VMEM((HEADS, HDIM, HDIM), F32)], name=name, compiler_params=_cp("arbitrary"),
    )(u, w, attn, qd, kd, egc, z, nw)


def _dn_rec_bwd(name, u, w, attn, qd, kd, egc, z, nw, ss, do):
    def body(u_ref, w_ref, at_ref, qd_ref, kd_ref, eg_ref, z_ref, nw_ref, ss_ref, do_ref,
             du_ref, dw_ref, dat_ref, dqd_ref, dkd_ref, deg_ref, dz_ref, dnw_ref, ds_scr):
        @pl.when(pl.program_id(0) == 0)
        def _():
            ds_scr[...] = jnp.zeros_like(ds_scr)
            dnw_ref[...] = jnp.zeros_like(dnw_ref)

        _, vjp = jax.vjp(_dn_step, ss_ref[...], _heads3(u_ref), _heads3(w_ref), _heads3(at_ref), _heads3(qd_ref),
                         _heads3(kd_ref), eg_ref[...], _heads3(z_ref), nw_ref[...])
        ds, du, dw, dat, dqd, dkd, deg, dz, dnw = vjp((ds_scr[...], _heads3(do_ref)))
        ds_scr[...] = ds
        for r, v in zip((du_ref, dw_ref, dat_ref, dqd_ref, dkd_ref, dz_ref), (du, dw, dat, dqd, dkd, dz)):
            _unheads(r, v)
        deg_ref[...] = deg
        dnw_ref[...] += dnw

    row = lambda w_: pl.BlockSpec((CHUNK, w_), lambda n: (NCHUNK - 1 - n, 0))
    return pl.pallas_call(
        body, grid=(NCHUNK,),
        in_specs=[row(V_B)] * 5 + [row(HEADS), row(V_B), _full((1, HDIM)),
                                   pl.BlockSpec((None, HEADS, HDIM, HDIM), lambda n: (NCHUNK - 1 - n, 0, 0, 0)),
                                   row(V_B)],
        out_specs=[row(V_B)] * 5 + [row(HEADS), row(V_B), _full((1, HDIM))],
        out_shape=[_sds((T, V_B))] * 5 + [_sds((T, HEADS)), _sds((T, V_B)), _sds((1, HDIM))],
        scratch_shapes=[pltpu.VMEM((HEADS, HDIM, HDIM), F32)], name=name, compiler_params=_cp("arbitrary"),
    )(u, w, attn, qd, kd, egc, z, nw, ss, do)


def _final(name, x, fw, target, tm=512):
    def body(x_ref, fw_ref, t_ref, l_ref, dx_ref, dfw_ref):
        @pl.when(pl.program_id(0) == 0)
        def _():
            l_ref[...] = jnp.zeros_like(l_ref)
            dfw_ref[...] = jnp.zeros_like(dfw_ref)

        tv = t_ref[...]

        def f(xv, fwv):
            err = _rms(xv, fwv) - tv
            per_tok = jnp.mean(err * err, axis=-1, keepdims=True)
            return 0.5 * jnp.sum(per_tok, axis=0, keepdims=True)

        loss, vjp = jax.vjp(f, x_ref[...], fw_ref[...])
        dx, dfw = vjp(jnp.ones((1, 1), F32))
        l_ref[...] += loss
        dx_ref[...] = dx
        dfw_ref[...] += dfw

    tok = pl.BlockSpec((tm, D), lambda i: (i, 0))
    return pl.pallas_call(
        body, grid=(T // tm,), in_specs=[tok, _full((1, D)), tok], out_specs=[_full((1, 1)), tok, _full((1, D))],
        out_shape=[_sds((1, 1)), _sds((T, D)), _sds((1, D))], name=name, compiler_params=_cp("arbitrary"),
    )(x, fw, target)


def _m1_pre(tv, sv):
    return [_rms(tv[0], sv[0])]


def _m1_post(ys, tv, sv):
    return (jnp.concatenate(ys, axis=1),)


def _m1_post_split(ys, tv, sv):
    proj = jnp.concatenate(ys, axis=1)
    return tuple(proj[:, a:b] for a, b in zip(IN_SPLITS[:-1], IN_SPLITS[1:]))


def _m5_pre(tv, sv):
    return [tv[1], tv[2]]


def _m5_post(ys, tv, sv):
    return (tv[0] + ys[0] + ys[1],)


def _c1_pre(tv, sv):
    return [_rms(tv[0], sv[0])]


def _c1_post(ys, tv, sv):
    return ((jnp.concatenate(ys[:2], axis=1) + sv[1]) * jax.nn.sigmoid(jnp.concatenate(ys[2:], axis=1) + sv[2]),)


def _c3_pre(tv, sv):
    return [_silu(_layernorm(tv[0], sv[0], sv[1]))]


def _c3_post(ys, tv, sv):
    return (tv[1] + ys[0] + sv[2],)


def _row(v):
    return v.reshape(1, -1)


def _mixer_fwd(tag, x, p):
    parts = _blk_fwd(f"m1_fwd_{tag}", _m1_pre, [0], _m1_post_split, [x], [p["nw"]], [p["w_in"]],
                     [(b - a, F32) for a, b in zip(IN_SPLITS[:-1], IN_SPLITS[1:])])
    qa, ka, va, qkvb, z, ba = parts
    att = _attn_fwd(f"attn_fwd_{tag}", qa, ka, va, p["sinks"])
    qkvc = _conv_fwd(f"dnconv_fwd_{tag}", qkvb, p["dn_conv_w"], None, True)
    loc = _dn_local_fwd(f"dnloc_fwd_{tag}", qkvc, ba, p["a_log"], p["dt_bias"])
    og, ss = _dn_rec_fwd(f"dnrec_fwd_{tag}", *loc, z, p["dn_norm_w"])
    (out,) = _blk_fwd(f"m5_fwd_{tag}", _m5_pre, [0, 1], _m5_post, [x, att, og], [], [p["wo_a"], p["wo_b"]],
                      [(D, F32)])
    return out, dict(x=x, qa=qa, ka=ka, va=va, qkvb=qkvb, z=z, ba=ba, att=att, qkvc=qkvc, loc=loc, og=og, ss=ss)


def _mixer_bwd(tag, dy, p, s):
    (dxa, datt, dog), _, (dwo_a, dwo_b) = _blk_bwd(f"m5_bwd_{tag}", _m5_pre, [0, 1], _m5_post,
                                                   [s["x"], s["att"], s["og"]], [], [p["wo_a"], p["wo_b"]], [[dy]])
    rec = _dn_rec_bwd(f"dnrec_bwd_{tag}", *s["loc"], s["z"], p["dn_norm_w"], s["ss"], dog)
    dz, dnw_dn = rec[6], rec[7]
    dqkvc, dba, dalog, ddtb = _dn_local_bwd(f"dnloc_bwd_{tag}", s["qkvc"], s["ba"], p["a_log"], p["dt_bias"],
                                            rec[:6])
    dqkvb, dconvw, _ = _conv_bwd(f"dnconv_bwd_{tag}", s["qkvb"], p["dn_conv_w"], None, True, dqkvc)
    dqa, dka, dva, dsinks = _attn_bwd(f"attn_bwd_{tag}", s["qa"], s["ka"], s["va"], p["sinks"], datt)
    (dx,), (dnw,), (dw_in,) = _blk_bwd(f"m1_bwd_{tag}", _m1_pre, [0], _m1_post, [s["x"]], [p["nw"]], [p["w_in"]],
                                       [[dqa, dka, dva, dqkvb, dz, dba]], res=dxa)
    return dx, dict(nw=dnw, w_in=dw_in, wo_a=dwo_a, wo_b=dwo_b, dn_conv_w=dconvw, sinks=dsinks, a_log=dalog,
                    dt_bias=ddtb, dn_norm_w=dnw_dn)


def _conformer_fwd(tag, x, p):
    (glu,) = _blk_fwd(f"c1_fwd_{tag}", _c1_pre, [0], _c1_post, [x], [p["nw"], p["b1a"], p["b1b"]], [p["w1"]],
                      [(D, F32)])
    cc = _conv_fwd(f"dwconv_fwd_{tag}", glu, p["w_dw"], p["b_dw"], False)
    (out,) = _blk_fwd(f"c3_fwd_{tag}", _c3_pre, [0], _c3_post, [cc, x], [p["ln_w"], p["ln_b"], p["b2"]], [p["w2"]],
                      [(D, F32)])
    return out, dict(x=x, glu=glu, cc=cc)


def _conformer_bwd(tag, dy, p, s):
    (dcc, dxa), (dlnw, dlnb, db2), (dw2,) = _blk_bwd(f"c3_bwd_{tag}", _c3_pre, [0], _c3_post, [s["cc"], s["x"]],
                                                     [p["ln_w"], p["ln_b"], p["b2"]], [p["w2"]], [[dy]])
    dglu, dwdw, dbdw = _conv_bwd(f"dwconv_bwd_{tag}", s["glu"], p["w_dw"], p["b_dw"], False, dcc)
    (dx,), (dnw, db1a, db1b), (dw1,) = _blk_bwd(f"c1_bwd_{tag}", _c1_pre, [0], _c1_post, [s["x"]],
                                                [p["nw"], p["b1a"], p["b1b"]], [p["w1"]], [[dglu]], res=dxa)
    return dx, dict(nw=dnw, b1a=db1a, b1b=db1b, w1=dw1, w_dw=dwdw, b_dw=dbdw, ln_w=dlnw, ln_b=dlnb, b2=db2, w2=dw2)


def _layer_fwd(l, x, nw, ffn, p):
    x1 = _ffn_fwd(f"ffn_fwd_{l}a", x, _row(nw[0]), *ffn, 0)
    p = dict(p, nw=_row(nw[1]))
    x2, sv = (_mixer_fwd if l % 2 == 0 else _conformer_fwd)(str(l), x1, p)
    return _ffn_fwd(f"ffn_fwd_{l}b", x2, _row(nw[2]), *ffn, 1), (x, x2, p, sv)


def _layer_bwd(l, dx, nw, ffn, saved, after_first=lambda dx: dx):
    x0, x2, p, sv = saved
    dx, dn2, *dffn = _ffn_bwd(f"ffn_bwd_{l}b", x2, _row(nw[2]), *ffn, 1, dx)
    dx = after_first(dx)
    dx, dmix = (_mixer_bwd if l % 2 == 0 else _conformer_bwd)(str(l), dx, p, sv)
    dx, dn0, *dffn = _ffn_bwd(f"ffn_bwd_{l}a", x0, _row(nw[0]), *ffn, 0, dx, dffn)
    return dx, jnp.concatenate([dn0, dmix.pop("nw"), dn2], axis=0), dffn, dmix


def _place():
    x, y, c = lax.axis_index("x"), lax.axis_index("y"), lax.axis_index("c")
    chips = [(1 - x, y), (x, 1 - y), (1 - x, 1 - y)]
    return x, y, c, 2 * x + y, chips, [2 * px + py for px, py in chips]


def _handshake(peers):
    barrier = pltpu.get_barrier_semaphore()
    for p in peers:
        pl.semaphore_signal(barrier, inc=1, device_id=p, device_id_type=MESH)
    pl.semaphore_wait(barrier, len(peers))


def _chip_peers():
    x, y, c, _, chips, _ = _place()
    return [(*chip, c) for chip in chips] + [(x, y, 1 - c)]


def _gather_copies(ins, outs, nb, send, recv, fsend, frecv, lsem):
    n_in = len(ins)
    x, y, c, me, chips, cidx = _place()
    sib = (x, y, 1 - c)
    local = [pltpu.make_async_copy(ins[a], outs[a].at[me], lsem.at[a]) for a in range(n_in)]
    for cp in local:
        cp.start()

    def ici(a, j):
        k = a * 3 + j
        src, dst = (ins[a], outs[a].at[me]) if a >= nb else (ins[a].at[pl.ds(c, 1)], outs[a].at[me, pl.ds(c, 1)])
        return pltpu.make_async_remote_copy(src, dst, send.at[k], recv.at[k], device_id=(*chips[j], c),
                                            device_id_type=MESH)

    def landed(a, j):
        k = a * 3 + j
        dst = outs[a].at[cidx[j]] if a >= nb else outs[a].at[cidx[j], pl.ds(c, 1)]
        return pltpu.make_async_remote_copy(dst, dst, send.at[k], recv.at[k], device_id=(*chips[j], c),
                                            device_id_type=MESH)

    def passed(a, j, who):
        k = a * 3 + j
        part = outs[a].at[cidx[j], pl.ds(who, 1)]
        return pltpu.make_async_remote_copy(part, part, fsend.at[k], frecv.at[k], device_id=sib, device_id_type=MESH)

    sends = [ici(a, j) for a in range(n_in) for j in range(3)]
    for cp in sends:
        cp.start()
    for a in range(nb):
        for j in range(3):
            landed(a, j).wait_recv()
            cp = passed(a, j, c)
            cp.start()
            sends.append(cp)
    for a in range(nb, n_in):
        for j in range(3):
            landed(a, j).wait_recv()
    for a in range(nb):
        for j in range(3):
            passed(a, j, 1 - c).wait_recv()
    for cp in sends:
        cp.wait_send()
    for cp in local:
        cp.wait()


def _gather_sems(n_in, nb):
    dma = pltpu.SemaphoreType.DMA
    return [dma((3 * n_in,)), dma((3 * n_in,)), dma((3 * nb,)), dma((3 * nb,)), dma((n_in,))]


def _gather_async(name, halved, whole=()):
    nb, arrs = len(halved), list(halved) + list(whole)
    hbm = pltpu.MemorySpace.HBM
    ins = [jax.new_ref(a, memory_space=hbm) for a in arrs]
    outs = [jax.empty_ref(_sds((NCHIP,) + a.shape, a.dtype), memory_space=hbm) for a in arrs]

    @pl.kernel(mesh=plsc.ScalarSubcoreMesh(axis_name="seq", num_cores=1), name=name,
               scratch_types=tuple(_gather_sems(len(arrs), nb)),
               compiler_params=pltpu.CompilerParams(collective_id=2))
    def launch(send, recv, fsend, frecv, lsem):
        _handshake(_chip_peers())
        _gather_copies(ins, outs, nb, send, recv, fsend, frecv, lsem)

    launch()
    return outs


def _swap_halves(name, grads):
    n = len(grads)
    hbm = pltpu.MemorySpace.HBM
    ins = [jax.new_ref(g, memory_space=hbm) for g in grads]
    outs = [jax.empty_ref(_sds((NCHIP, g.shape[1] // 2) + g.shape[2:], g.dtype), memory_space=hbm) for g in grads]

    @pl.kernel(mesh=plsc.ScalarSubcoreMesh(axis_name="seq", num_cores=1), name=name,
               scratch_types=(pltpu.SemaphoreType.DMA((n,)), pltpu.SemaphoreType.DMA((n,))),
               compiler_params=pltpu.CompilerParams(collective_id=1))
    def launch(send, recv):
        x, y, c, _, _, _ = _place()
        sib = (x, y, 1 - c)
        _handshake([sib])
        cps = []
        for a in range(n):
            h = grads[a].shape[1] // 2
            cps.append(pltpu.make_async_remote_copy(ins[a].at[:, pl.ds((1 - c) * h, h)], outs[a], send.at[a],
                                                    recv.at[a], device_id=sib, device_id_type=MESH))
        for cp in cps:
            cp.start()
        for cp in cps:
            cp.wait()

    launch()
    return outs


def _row_tile(r, cap=256):
    return max(t for t in range(8, cap + 1, 8) if r % t == 0)


def _add_half(name, g, r, c_arr):
    _, l, rows, cols = g.shape
    h = l // 2
    tr = _row_tile(rows)

    def body(c_ref, g_ref, r_ref, o_ref):
        o_ref[...] = (g_ref[...].astype(F32) + r_ref[...].astype(F32)).astype(BF16)

    blk = (None, None, tr, cols)
    return pl.pallas_call(
        body,
        grid_spec=pltpu.PrefetchScalarGridSpec(
            num_scalar_prefetch=1, grid=(NCHIP, h, rows // tr),
            in_specs=[pl.BlockSpec(blk, lambda j, i, t, c_ref: (j, c_ref[0] * h + i, t, 0)),
                      pl.BlockSpec(blk, lambda j, i, t, c_ref: (j, i, t, 0))],
            out_specs=pl.BlockSpec(blk, lambda j, i, t, c_ref: (j, i, t, 0))),
        out_shape=_sds((NCHIP, h, rows, cols), BF16), name=name,
        compiler_params=_cp("parallel", "parallel", "parallel"),
    )(c_arr, g, r)


def _scatter_async(name, parts, sums, where):
    nb = len(parts)
    ins = [jax.new_ref(p, memory_space=pltpu.MemorySpace.HBM) for p in parts]
    dma = pltpu.SemaphoreType.DMA

    @pl.kernel(mesh=plsc.ScalarSubcoreMesh(axis_name="seq", num_cores=1), name=name,
               scratch_types=(dma((3 * nb,)), dma((3 * nb,)), dma((4 * nb,)), dma((4 * nb,)), dma((nb,))),
               compiler_params=pltpu.CompilerParams(collective_id=3))
    def launch(send, recv, fsend, frecv, lsem):
        _handshake(_chip_peers())
        x, y, c, me, chips, cidx = _place()
        sib = (x, y, 1 - c)

        def slot(a, half, chip):
            return sums[a].at[half, chip, pl.ds(where[a], 1)]

        local = [pltpu.make_async_copy(ins[a].at[me], slot(a, c, me), lsem.at[a]) for a in range(nb)]
        for cp in local:
            cp.start()

        def ici(a, j):
            return pltpu.make_async_remote_copy(ins[a].at[cidx[j]], slot(a, c, me), send.at[a * 3 + j],
                                                recv.at[a * 3 + j], device_id=(*chips[j], c), device_id_type=MESH)

        def landed(a, j):
            dst = slot(a, c, cidx[j])
            return pltpu.make_async_remote_copy(dst, dst, send.at[a * 3 + j], recv.at[a * 3 + j],
                                                device_id=(*chips[j], c), device_id_type=MESH)

        def passed(a, j, who):
            dst = slot(a, who, me if j == 3 else cidx[j])
            src = ins[a].at[me] if j == 3 else dst
            return pltpu.make_async_remote_copy(src, dst, fsend.at[a * 4 + j], frecv.at[a * 4 + j], device_id=sib,
                                                device_id_type=MESH)

        sends = [ici(a, j) for a in range(nb) for j in range(3)] + [passed(a, 3, c) for a in range(nb)]
        for cp in sends:
            cp.start()
        for a in range(nb):
            for j in range(3):
                landed(a, j).wait_recv()
                cp = passed(a, j, c)
                cp.start()
                sends.append(cp)
        for a in range(nb):
            for j in range(4):
                passed(a, j, 1 - c).wait_recv()
        for cp in sends:
            cp.wait_send()
        for cp in local:
            cp.wait()

    launch()


def _exchange_small(small, rep):
    def body(small_in, rep_in, small_out, rep_out, lsem, ssend, srecv):
        x, y, c, me, _, _ = _place()
        dev = 4 * x + 2 * y + c
        local = [pltpu.make_async_copy(small_in.at[me], small_out.at[dev], lsem.at[0]),
                 pltpu.make_async_copy(rep_in, rep_out.at[dev], lsem.at[1])]
        for cp in local:
            cp.start()

        def peer(r):
            return (1 - x if r & 4 else x), (1 - y if r & 2 else y), (1 - c if r & 1 else c)

        def tiny(r, which):
            px, py, pc = peer(r)
            k = (r - 1) * 2 + which
            if which == 0:
                return pltpu.make_async_remote_copy(small_in.at[2 * px + py], small_out.at[dev], ssend.at[k],
                                                    srecv.at[k], device_id=(px, py, pc), device_id_type=MESH)
            return pltpu.make_async_remote_copy(rep_in, rep_out.at[dev], ssend.at[k], srecv.at[k],
                                                device_id=(px, py, pc), device_id_type=MESH)

        def tiny_landed(r, which):
            px, py, pc = peer(r)
            k = (r - 1) * 2 + which
            dst = (small_out if which == 0 else rep_out).at[4 * px + 2 * py + pc]
            return pltpu.make_async_remote_copy(dst, dst, ssend.at[k], srecv.at[k], device_id=(px, py, pc),
                                                device_id_type=MESH)

        sends = [tiny(r, w) for r in range(1, NDEV) for w in range(2)]
        for cp in sends:
            cp.start()
        for r in range(1, NDEV):
            for w in range(2):
                tiny_landed(r, w).wait_recv()
        for cp in sends:
            cp.wait_send()
        for cp in local:
            cp.wait()

    dma = pltpu.SemaphoreType.DMA
    return pl.pallas_call(
        body, in_specs=[ANY] * 2, out_specs=[ANY] * 2,
        out_shape=[_sds((NDEV,) + small.shape[1:], F32), _sds((NDEV,) + rep.shape, F32)],
        scratch_shapes=[dma((2,)), dma((2 * (NDEV - 1),)), dma((2 * (NDEV - 1),))], name="exchange_small_grads",
    )(small, rep)


def _adamw_math(w, g, m, v):
    m = B1 * m + (1.0 - B1) * g
    v = B2 * v + (1.0 - B2) * (g * g)
    m_hat = m / (1.0 - B1 ** STEP)
    v_hat = v / (1.0 - B2 ** STEP)
    return -LR * (m_hat / (jnp.sqrt(v_hat) + AEPS) + WD * w), m, v


def _adamw_big(name, w, m, v, parts, row0=0):
    n, _, rows, cols = w.shape
    tr = _row_tile(rows)
    t0 = row0 // tr

    def body(w_ref, m_ref, v_ref, p_ref, g_ref, d_ref, nm_ref, nv_ref):
        g = p_ref[0].astype(F32)
        for q in range(1, NCHIP):
            g = g + p_ref[q].astype(F32)
        d, nm, nv = _adamw_math(w_ref[...], g, m_ref[...], v_ref[...])
        g_ref[...], d_ref[...], nm_ref[...], nv_ref[...] = g, d, nm, nv

    spec = pl.BlockSpec((None, None, tr, cols), lambda i, p, t: (i, p, t, 0))
    return pl.pallas_call(
        body, grid=(n, 2, rows // tr),
        in_specs=[spec, spec, spec,
                  pl.BlockSpec((None, NCHIP, None, tr, cols), lambda i, p, t: (p, 0, i, t0 + t, 0))],
        out_specs=[spec] * 4, out_shape=[_sds(w.shape)] * 4, name=name,
        compiler_params=_cp("parallel", "parallel", "parallel"),
    )(w, m, v, parts)


def _adamw_small(name, w, m, v, parts):
    def body(w_ref, m_ref, v_ref, p_ref, g_ref, d_ref, nm_ref, nv_ref):
        g = p_ref[0]
        for q in range(1, NDEV):
            g = g + p_ref[q]
        d, nm, nv = _adamw_math(w_ref[...], g, m_ref[...], v_ref[...])
        g_ref[...], d_ref[...], nm_ref[...], nv_ref[...] = g, d, nm, nv

    return pl.pallas_call(body, out_shape=[_sds(w.shape)] * 4, name=name)(w, m, v, parts)


def _pack(arrs, rows):
    flat = jnp.concatenate([a.reshape(-1) for a in arrs])
    return jnp.pad(flat, (0, rows * LANES - flat.shape[0])).reshape(rows, LANES)


def _unpack(packed, shapes):
    flat, out, o = packed.reshape(-1), [], 0
    for s in shapes:
        n = 1
        for d in s:
            n *= d
        out.append(flat[o:o + n].reshape(s))
        o += n
    return out


SMALL_ROWS, REP_ROWS = 200, 16


def kernel(x, norm_w, ffn_w_gate, ffn_w_up, ffn_w_down, mix_w_in, dn_conv_w, attn_sinks, dn_a_log, dn_dt_bias, dn_norm_w, mix_w_out, conv_w_pw1, conv_b_pw1, conv_w_dw, conv_b_dw, conv_ln_w, conv_ln_b, conv_w_pw2, conv_b_pw2, final_norm_w, loss_target, m_norm_w, m_ffn_w_gate, m_ffn_w_up, m_ffn_w_down, m_mix_w_in, m_dn_conv_w, m_attn_sinks, m_dn_a_log, m_dn_dt_bias, m_dn_norm_w, m_mix_w_out, m_conv_w_pw1, m_conv_b_pw1, m_conv_w_dw, m_conv_b_dw, m_conv_ln_w, m_conv_ln_b, m_conv_w_pw2, m_conv_b_pw2, m_final_norm_w, v_norm_w, v_ffn_w_gate, v_ffn_w_up, v_ffn_w_down, v_mix_w_in, v_dn_conv_w, v_attn_sinks, v_dn_a_log, v_dn_dt_bias, v_dn_norm_w, v_mix_w_out, v_conv_w_pw1, v_conv_b_pw1, v_conv_w_dw, v_conv_b_dw, v_conv_ln_w, v_conv_ln_b, v_conv_w_pw2, v_conv_b_pw2, v_final_norm_w):
    small_names = ["norm_w", "dn_conv_w", "conv_b_pw1", "conv_w_dw", "conv_b_dw", "conv_ln_w", "conv_ln_b",
                   "conv_b_pw2"]
    rep_names = ["attn_sinks", "dn_a_log", "dn_dt_bias", "dn_norm_w", "final_norm_w"]
    w = dict(norm_w=norm_w, ffn_w_gate=ffn_w_gate, ffn_w_up=ffn_w_up, ffn_w_down=ffn_w_down, mix_w_in=mix_w_in, dn_conv_w=dn_conv_w, attn_sinks=attn_sinks, dn_a_log=dn_a_log, dn_dt_bias=dn_dt_bias, dn_norm_w=dn_norm_w, mix_w_out=mix_w_out, conv_w_pw1=conv_w_pw1, conv_b_pw1=conv_b_pw1, conv_w_dw=conv_w_dw, conv_b_dw=conv_b_dw, conv_ln_w=conv_ln_w, conv_ln_b=conv_ln_b, conv_w_pw2=conv_w_pw2, conv_b_pw2=conv_b_pw2, final_norm_w=final_norm_w)
    m = dict(norm_w=m_norm_w, ffn_w_gate=m_ffn_w_gate, ffn_w_up=m_ffn_w_up, ffn_w_down=m_ffn_w_down, mix_w_in=m_mix_w_in, dn_conv_w=m_dn_conv_w, attn_sinks=m_attn_sinks, dn_a_log=m_dn_a_log, dn_dt_bias=m_dn_dt_bias, dn_norm_w=m_dn_norm_w, mix_w_out=m_mix_w_out, conv_w_pw1=m_conv_w_pw1, conv_b_pw1=m_conv_b_pw1, conv_w_dw=m_conv_w_dw, conv_b_dw=m_conv_b_dw, conv_ln_w=m_conv_ln_w, conv_ln_b=m_conv_ln_b, conv_w_pw2=m_conv_w_pw2, conv_b_pw2=m_conv_b_pw2, final_norm_w=m_final_norm_w)
    v = dict(norm_w=v_norm_w, ffn_w_gate=v_ffn_w_gate, ffn_w_up=v_ffn_w_up, ffn_w_down=v_ffn_w_down, mix_w_in=v_mix_w_in, dn_conv_w=v_dn_conv_w, attn_sinks=v_attn_sinks, dn_a_log=v_dn_a_log, dn_dt_bias=v_dn_dt_bias, dn_norm_w=v_dn_norm_w, mix_w_out=v_mix_w_out, conv_w_pw1=v_conv_w_pw1, conv_b_pw1=v_conv_b_pw1, conv_w_dw=v_conv_w_dw, conv_b_dw=v_conv_b_dw, conv_ln_w=v_conv_ln_w, conv_ln_b=v_conv_ln_b, conv_w_pw2=v_conv_w_pw2, conv_b_pw2=v_conv_b_pw2, final_norm_w=v_final_norm_w)
    order = ["norm_w", "ffn_w_gate", "ffn_w_up", "ffn_w_down", "mix_w_in", "dn_conv_w", "attn_sinks", "dn_a_log",
             "dn_dt_bias", "dn_norm_w", "mix_w_out", "conv_w_pw1", "conv_b_pw1", "conv_w_dw", "conv_b_dw",
             "conv_ln_w", "conv_ln_b", "conv_w_pw2", "conv_b_pw2", "final_norm_w"]

    small_shapes = [w[n].shape for n in small_names]
    rep_shapes = [w[n].shape for n in rep_names]

    def halves(a):
        return a.reshape(a.shape[:-2] + (2, a.shape[-2] // 2, a.shape[-1]))

    def layer_shards(l):
        mix_in, mix_out = (mix_w_in, mix_w_out) if l % 2 == 0 else (conv_w_pw1, conv_w_pw2)
        return [t.astype(BF16) for t in (jnp.concatenate([ffn_w_gate[l], ffn_w_up[l]], axis=1), ffn_w_down[l],
                                         halves(mix_in[l // 2]), halves(mix_out[l // 2]))]

    gathering = [_gather_async(f"gather_layer{l}", layer_shards(l),
                               [_pack([w[n] for n in small_names], SMALL_ROWS)] if l == 0 else [])
                 for l in range(DEPTH)]

    def mixer_params(l, w_a, w_b):
        e = l // 2
        w_a = w_a.reshape(NCHIP, D, -1)
        w_b = w_b.reshape(D, D)
        if l % 2 == 0:
            return dict(w_in=w_a, dn_conv_w=sm["dn_conv_w"][e], sinks=_row(attn_sinks[e]), a_log=_row(dn_a_log[e]),
                        dt_bias=_row(dn_dt_bias[e]), dn_norm_w=_row(dn_norm_w[e]), wo_a=w_b[:Q_A], wo_b=w_b[Q_A:])
        return dict(b1a=_row(sm["conv_b_pw1"][e, :D]), b1b=_row(sm["conv_b_pw1"][e, D:]), w1=w_a,
                    w_dw=sm["conv_w_dw"][e], b_dw=_row(sm["conv_b_dw"][e]), ln_w=_row(sm["conv_ln_w"][e]),
                    ln_b=_row(sm["conv_ln_b"][e]), b2=_row(sm["conv_b_pw2"][e]), w2=w_b)

    xs, saved, ffn_w = x[0], [], []
    for l in range(DEPTH):
        got = [r[...] for r in gathering[l]]
        if l == 0:
            per_chip = [_unpack(got[4][q], small_shapes) for q in range(NCHIP)]
            sm = {n: jnp.concatenate([per_chip[q][i] for q in range(NCHIP)], axis=-1)
                  for i, n in enumerate(small_names)}
        else:
            xs, got = lax.optimization_barrier((xs, got))
        ffn_w.append(got[:2])
        xs, sv = _layer_fwd(l, xs, sm["norm_w"][l], got[:2], mixer_params(l, got[2], got[3]))
        saved.append(sv)
    loss, dx, dfw = _final("final", xs, _row(final_norm_w), loss_target[0])

    hbm = pltpu.MemorySpace.HBM
    sum_shapes = dict(gu=(DEPTH, 2 * D, FS), down=(DEPTH, FS, D), w_in=(2, D // 2, IN_COLS // NCHIP),
                      w_out=(2, D // 8, D), pw1=(2, D // 2, D // 2), pw2=(2, D // 8, D))
    sums = {k: jax.empty_ref(_sds((2, NCHIP) + s, BF16), memory_space=hbm) for k, s in sum_shapes.items()}
    c_arr = lax.axis_index("c").astype(jnp.int32).reshape(1)
    dnorm, gmix = [None] * DEPTH, [None] * DEPTH

    def hand_on(l, grads, swapped):
        def run(dx):
            dx, other = lax.optimization_barrier((dx, [r[...] for r in swapped]))
            parts = [_add_half(f"add_half_{l}_{k}", gg, rr, c_arr) for k, (gg, rr) in enumerate(zip(grads, other))]
            dx, parts = lax.optimization_barrier((dx, parts))
            keys = ("gu", "down", "w_in", "w_out") if l % 2 == 0 else ("gu", "down", "pw1", "pw2")
            _scatter_async(f"scatter_grads_{l}", parts, [sums[k] for k in keys], [l, l, l // 2, l // 2])
            return dx
        return run

    pending = lambda dx: dx
    for l in reversed(range(DEPTH)):
        dx, dnorm[l], dffn, gmix[l] = _layer_bwd(l, dx, sm["norm_w"][l], ffn_w[l], saved[l], pending)
        if l % 2 == 0:
            g_a, g_b = gmix[l]["w_in"], jnp.concatenate([gmix[l]["wo_a"], gmix[l]["wo_b"]], axis=0)
        else:
            g_a, g_b = gmix[l]["w1"], gmix[l]["w2"]
        g_a = halves(g_a).astype(BF16)
        g_b = g_b.reshape(NCHIP, 2, D // 8, D).astype(BF16)
        dx, grads = lax.optimization_barrier((dx, [dffn[0], dffn[1], g_a, g_b]))
        pending = hand_on(l, grads, _swap_halves(f"swap_grads_{l}", grads))
    dx = pending(dx)
    gm, gc = [gmix[0], gmix[2]], [gmix[1], gmix[3]]
    small_g = dict(
        norm_w=jnp.stack(dnorm), dn_conv_w=jnp.stack([gm[e]["dn_conv_w"] for e in range(2)]),
        conv_b_pw1=jnp.stack([jnp.concatenate([gc[e]["b1a"], gc[e]["b1b"]], axis=1)[0] for e in range(2)]),
        conv_w_dw=jnp.stack([gc[e]["w_dw"] for e in range(2)]),
        conv_b_dw=jnp.stack([gc[e]["b_dw"][0] for e in range(2)]),
        conv_ln_w=jnp.stack([gc[e]["ln_w"][0] for e in range(2)]),
        conv_ln_b=jnp.stack([gc[e]["ln_b"][0] for e in range(2)]),
        conv_b_pw2=jnp.stack([gc[e]["b2"][0] for e in range(2)]))
    small_by_chip = jnp.stack([_pack([jnp.split(small_g[n], NCHIP, axis=-1)[q] for n in small_names], SMALL_ROWS)
                               for q in range(NCHIP)])
    rep_g = _pack([jnp.stack([gm[e]["sinks"][0] for e in range(2)]), jnp.stack([gm[e]["a_log"][0] for e in range(2)]),
                   jnp.stack([gm[e]["dt_bias"][0] for e in range(2)]),
                   jnp.stack([gm[e]["dn_norm_w"][0] for e in range(2)]), dfw[0]], REP_ROWS)
    small_sum, rep_sum = _exchange_small(small_by_chip, rep_g)

    res = {}
    partial_sums = {k: r[...] for k, r in sums.items()}
    for n, key, row0 in (("ffn_w_gate", "gu", 0), ("ffn_w_up", "gu", D), ("ffn_w_down", "down", 0),
                         ("mix_w_in", "w_in", 0), ("mix_w_out", "w_out", 0), ("conv_w_pw1", "pw1", 0),
                         ("conv_w_pw2", "pw2", 0)):
        view = (lambda a: a) if w[n].ndim == 4 else halves
        outs = _adamw_big(f"adamw_{n}", view(w[n]), view(m[n]), view(v[n]), partial_sums[key], row0)
        res[n] = [o.reshape(w[n].shape) for o in outs]
    outs = _adamw_small("adamw_small", *[_pack([d[n] for n in small_names], SMALL_ROWS) for d in (w, m, v)],
                        small_sum)
    for i, n in enumerate(small_names):
        res[n] = [_unpack(o, small_shapes)[i] for o in outs]
    outs = _adamw_small("adamw_replicated", *[_pack([d[n] for n in rep_names], REP_ROWS) for d in (w, m, v)],
                        rep_sum)
    for i, n in enumerate(rep_names):
        res[n] = [_unpack(o, rep_shapes)[i] for o in outs]

    total = lax.psum(loss[0, 0], ("x", "y", "c"))
    return (total, dx[None], *[res[n][0] for n in order], *[res[n][1] for n in order],
            *[res[n][2] for n in order], *[res[n][3] for n in order])
```

```python
import jax
import jax.numpy as jnp
from jax import lax
from jax.experimental import pallas as pl
from jax.experimental.pallas import tpu as pltpu
from jax.experimental.pallas import tpu_sc as plsc

F32, BF16 = jnp.float32, jnp.bfloat16
MESH = pl.DeviceIdType.MESH
ANY = pl.BlockSpec(memory_space=pl.ANY)

T, D, F = 2048, 1024, 2816
DEPTH = 4
EPS = 1e-6
HEADS, HDIM, KV_HEADS, GROUP = 8, 64, 2, 4
WINDOW = BLOCK = 128
CHUNK = 64
NCHUNK = T // CHUNK
DN_CONV, CONV_WIDTH = 4, 31
Q_A, KV_A, QKV_B, V_B = 512, 128, 1536, 512
IN_COLS = 2832
IN_SPLITS = (0, 512, 640, 768, 2304, 2816, 2832)
NCHIP, NDEV = 4, 8
FS = F // NCHIP
LR, B1, B2, AEPS, WD, STEP = 0.001, 0.9, 0.999, 1e-08, 0.01, 10
V7X_VMEM_BYTES = 64 * 1024 * 1024
VMEM_LIMIT = V7X_VMEM_BYTES * 7 // 8
LANES = 128


def _cp(*sem):
    return pltpu.CompilerParams(dimension_semantics=sem, vmem_limit_bytes=VMEM_LIMIT)


def _sds(shape, dtype=F32):
    return jax.ShapeDtypeStruct(tuple(shape), dtype)


def _full(shape):
    nd = len(shape)
    return pl.BlockSpec(tuple(shape), lambda *_: (0,) * nd)


def _split_bf16(a):
    hi = a.astype(BF16)
    return hi, (a - hi.astype(F32)).astype(BF16)


def _dg(a, b, ca, cb, hi=False):
    if a.ndim == 3 and b.ndim == 3:
        dims = (((ca + 1,), (cb + 1,)), ((0,), (0,)))
    else:
        dims = (((ca,), (cb,)), ((), ()))
    dot = lambda p, q: lax.dot_general(p, q, dims, preferred_element_type=F32)
    if hi:
        a_hi, a_lo = _split_bf16(a.astype(F32))
        b_hi, b_lo = _split_bf16(b.astype(F32))
        return dot(a_hi, b_hi) + (dot(a_hi, b_lo) + dot(a_lo, b_hi))
    return dot(a.astype(BF16), b.astype(BF16))


def _make_mm(hi):
    @jax.custom_vjp
    def nn(a, b):
        return _dg(a, b, 1, 0, hi)

    @jax.custom_vjp
    def nt(a, b):
        return _dg(a, b, 1, 1, hi)

    @jax.custom_vjp
    def tn(a, b):
        return _dg(a, b, 0, 0, hi)

    nn.defvjp(lambda a, b: (_dg(a, b, 1, 0, hi), (a, b)),
              lambda r, g: (_dg(g, r[1], 1, 1, hi).astype(r[0].dtype), _dg(r[0], g, 0, 0, hi).astype(r[1].dtype)))
    nt.defvjp(lambda a, b: (_dg(a, b, 1, 1, hi), (a, b)),
              lambda r, g: (_dg(g, r[1], 1, 0, hi).astype(r[0].dtype), _dg(g, r[0], 0, 0, hi).astype(r[1].dtype)))
    tn.defvjp(lambda a, b: (_dg(a, b, 0, 0, hi), (a, b)),
              lambda r, g: (_dg(r[1], g, 1, 1, hi).astype(r[0].dtype), _dg(r[0], g, 1, 0, hi).astype(r[1].dtype)))
    return nn, nt, tn


_nn, _nt, _tn = _make_mm(False)
_nn_hi, _nt_hi, _tn_hi = _make_mm(True)


def _rms(x, w):
    return x * lax.rsqrt(jnp.mean(x * x, axis=-1, keepdims=True) + EPS) * w


def _layernorm(x, w, b):
    xc = x - jnp.mean(x, axis=-1, keepdims=True)
    return xc * lax.rsqrt(jnp.mean(xc * xc, axis=-1, keepdims=True) + EPS) * w + b


def _silu(x):
    return x * jax.nn.sigmoid(x)


def _iota2(shape, dim):
    return lax.broadcasted_iota(jnp.int32, shape, dim)


def _flat_weights(lhs_idx, weights):
    specs, ops, lhs_of, where = [], [], [], []
    for a, (k, w) in enumerate(zip(lhs_idx, weights)):
        for q in range(1 if w.ndim == 2 else w.shape[0]):
            specs.append(_full(w.shape) if w.ndim == 2
                         else pl.BlockSpec((None,) + w.shape[1:], lambda i, q=q: (q, 0, 0)))
            ops.append(w)
            lhs_of.append(k)
            where.append((a, None if w.ndim == 2 else q))
    return specs, ops, lhs_of, where


def _blk_fwd(name, pre, lhs_idx, post, toks, smalls, weights, outs, tm=512):
    wspecs, wops, lhs_of, _ = _flat_weights(lhs_idx, weights)
    nt_, ns, nw = len(toks), len(smalls), len(wops)

    def body(*refs):
        tv = [r[...] for r in refs[:nt_]]
        sv = [r[...] for r in refs[nt_:nt_ + ns]]
        wr = refs[nt_ + ns:nt_ + ns + nw]
        orf = refs[nt_ + ns + nw:]
        lhs = pre(tv, sv)
        ys = [_dg(lhs[i], w[...], 1, 0) for i, w in zip(lhs_of, wr)]
        for o_ref, o in zip(orf, post(ys, tv, sv)):
            o_ref[...] = o.astype(o_ref.dtype)

    in_specs = ([pl.BlockSpec((tm, a.shape[1]), lambda i: (i, 0)) for a in toks]
                + [_full(a.shape) for a in smalls] + wspecs)
    out_specs = [pl.BlockSpec((tm, w_), lambda i: (i, 0)) for w_, _ in outs]
    return pl.pallas_call(
        body, grid=(T // tm,), in_specs=in_specs, out_specs=out_specs,
        out_shape=[_sds((T, w_), dt) for w_, dt in outs], name=name, compiler_params=_cp("parallel"),
    )(*toks, *smalls, *wops)


def _blk_bwd(name, pre, lhs_idx, post, toks, smalls, weights, ct_groups, res=None, tm=256, wchunk=512):
    wspecs, wops, lhs_of, where = _flat_weights(lhs_idx, weights)
    nt_, ns, nw, na = len(toks), len(smalls), len(wops), len(weights)
    cts = [a for g in ct_groups for a in g]
    nc = len(cts)
    widths = [sum(a.shape[1] for a in g) for g in ct_groups]
    has_res = res is not None

    def body(*refs):
        p = 0
        tr = refs[p:p + nt_]; p += nt_
        sr = refs[p:p + ns]; p += ns
        wr = refs[p:p + nw]; p += nw
        cr = refs[p:p + nc]; p += nc
        rr = refs[p:p + has_res]; p += has_res
        dtr = refs[p:p + nt_]; p += nt_
        dsr = refs[p:p + ns]; p += ns
        dwr = refs[p:p + na]; p += na
        scr = refs[p:]
        i = pl.program_id(0)

        @pl.when(i == 0)
        def _():
            for r in list(dsr) + list(dwr):
                r[...] = jnp.zeros_like(r)

        tv = [r[...] for r in tr]
        sv = [r[...] for r in sr]
        ctv, q, si = [], 0, 0
        for g in ct_groups:
            if len(g) == 1:
                ctv.append(cr[q][...].astype(F32))
            else:
                off = 0
                for j, a in enumerate(g):
                    scr[si][:, off:off + a.shape[1]] = cr[q + j][...].astype(F32)
                    off += a.shape[1]
                ctv.append(scr[si][...])
                si += 1
            q += len(g)

        lhs, vjp_pre = jax.vjp(lambda *a: tuple(pre(list(a[:nt_]), list(a[nt_:]))), *tv, *sv)
        lhs_b = [l.astype(BF16) for l in lhs]
        ys = [_dg(lhs_b[k], w[...], 1, 0) for k, w in zip(lhs_of, wr)]
        _, vjp_post = jax.vjp(lambda *a: tuple(post(list(a[:nw]), list(a[nw:nw + nt_]), list(a[nw + nt_:]))),
                              *ys, *tv, *sv)
        gp = vjp_post(tuple(ctv))
        dys, dt_post, ds_post = gp[:nw], gp[nw:nw + nt_], gp[nw + nt_:]
        dlhs = [None] * len(lhs)
        for k, w, dy, (a, q) in zip(lhs_of, wr, dys, where):
            dyb = dy.astype(BF16)
            n = w.shape[1]
            for c0 in range(0, n, wchunk):
                c1 = min(n, c0 + wchunk)
                part = _dg(lhs_b[k], dyb[:, c0:c1], 0, 0)
                if q is None:
                    dwr[a][:, c0:c1] += part
                else:
                    dwr[a][q, :, c0:c1] += part
            d = _dg(dyb, w[...], 1, 1)
            dlhs[k] = d if dlhs[k] is None else dlhs[k] + d
        gq = vjp_pre(tuple(d.astype(l.dtype) for d, l in zip(dlhs, lhs)))
        dt_pre, ds_pre = gq[:nt_], gq[nt_:]
        for j in range(nt_):
            d = dt_post[j] + dt_pre[j]
            if j == 0 and has_res:
                d = d + rr[0][...]
            dtr[j][...] = d
        for j in range(ns):
            dsr[j][...] += ds_post[j] + ds_pre[j]

    tok_spec = lambda a: pl.BlockSpec((tm, a.shape[1]), lambda i: (i, 0))
    in_specs = ([tok_spec(a) for a in toks] + [_full(a.shape) for a in smalls] + wspecs
                + [tok_spec(a) for a in cts] + ([tok_spec(res)] if has_res else []))
    out_specs = [tok_spec(a) for a in toks] + [_full(a.shape) for a in smalls] + [_full(w.shape) for w in weights]
    out_shape = ([_sds(a.shape) for a in toks] + [_sds(a.shape) for a in smalls] + [_sds(w.shape) for w in weights])
    scratch = [pltpu.VMEM((tm, wd), F32) for g, wd in zip(ct_groups, widths) if len(g) > 1]
    outs = pl.pallas_call(
        body, grid=(T // tm,), in_specs=in_specs, out_specs=out_specs, out_shape=out_shape,
        scratch_shapes=scratch, name=name, compiler_params=_cp("arbitrary"),
    )(*toks, *smalls, *wops, *cts, *([res] if has_res else []))
    return outs[:nt_], outs[nt_:nt_ + ns], outs[nt_ + ns:]


def _ffn_fwd(name, x, nw, gu, wd, idx, tm=512):
    def body(x_ref, nw_ref, wg_ref, wu_ref, wd_ref, o_ref, h_scr):
        s = pl.program_id(1)

        @pl.when(s == 0)
        def _():
            xv = x_ref[...]
            h_scr[...] = _rms(xv, nw_ref[...]).astype(BF16)
            o_ref[...] = xv

        h = h_scr[...]
        a = _dg(h, wg_ref[...], 1, 0)
        b = _dg(h, wu_ref[...], 1, 0)
        o_ref[...] += 0.5 * _dg(_silu(a) * b, wd_ref[...], 1, 0)

    wspec = lambda r, c, k: pl.BlockSpec((None, None, r, c), lambda i, s: (s, idx, k, 0))
    return pl.pallas_call(
        body, grid=(T // tm, NCHIP),
        in_specs=[pl.BlockSpec((tm, D), lambda i, s: (i, 0)), _full((1, D)), wspec(D, FS, 0), wspec(D, FS, 1),
                  wspec(FS, D, 0)],
        out_specs=pl.BlockSpec((tm, D), lambda i, s: (i, 0)), out_shape=_sds((T, D)),
        scratch_shapes=[pltpu.VMEM((tm, D), BF16)], name=name, compiler_params=_cp("parallel", "arbitrary"),
    )(x, nw, gu, gu, wd)


def _ffn_bwd(name, x, nw, gu, wd, idx, dy, gbufs=None, tm=512):
    ni = T // tm

    def body(x_ref, dy_ref, nw_ref, wg_ref, wu_ref, wd_ref, dx_ref, dnw_ref, dgu_ref, dwd_ref,
             dh_acc, ag, au, ad):
        s, i = pl.program_id(0), pl.program_id(1)
        rows = pl.ds(pl.multiple_of(i * tm, tm), tm)

        @pl.when((s == 0) & (i == 0))
        def _():
            dnw_ref[...] = jnp.zeros_like(dnw_ref)

        @pl.when(i == 0)
        def _():
            ag[...] = jnp.zeros_like(ag)
            au[...] = jnp.zeros_like(au)
            ad[...] = jnp.zeros_like(ad)

        xv, nwv, dyv = x_ref[...], nw_ref[...], dy_ref[...]
        h, vjp_rms = jax.vjp(_rms, xv, nwv)
        hb = h.astype(BF16)
        a = _dg(hb, wg_ref[...], 1, 0)
        b = _dg(hb, wu_ref[...], 1, 0)
        sa = jax.nn.sigmoid(a)
        act = a * sa
        dyb = (0.5 * dyv).astype(BF16)
        ad[...] += _dg(act * b, dyb, 0, 0)
        dact = _dg(dyb, wd_ref[...], 1, 1)
        da = (dact * b * (sa * (1.0 + a * (1.0 - sa)))).astype(BF16)
        db = (dact * act).astype(BF16)
        ag[...] += _dg(hb, da, 0, 0)
        au[...] += _dg(hb, db, 0, 0)
        dh = _dg(da, wg_ref[...], 1, 1) + _dg(db, wu_ref[...], 1, 1)

        @pl.when(s == 0)
        def _():
            dh_acc[rows, :] = dh

        @pl.when(s > 0)
        def _():
            dh_acc[rows, :] += dh

        @pl.when(s == NCHIP - 1)
        def _():
            dx, dnw = vjp_rms(dh_acc[rows, :])
            dx_ref[...] = dyv + dx
            dnw_ref[...] += dnw

        @pl.when(i == ni - 1)
        def _():
            dgu_ref[0:D, :] = ag[...].astype(BF16)
            dgu_ref[D:, :] = au[...].astype(BF16)
            dwd_ref[...] = ad[...].astype(BF16)

    wspec = lambda r, c, k: pl.BlockSpec((None, None, r, c), lambda s, i: (s, idx, k, 0),
                                         pipeline_mode=pl.Buffered(1))
    last = lambda s, i: (jnp.where(s == NCHIP - 1, i, 0), 0)
    nb = 0 if gbufs is None else 2
    return pl.pallas_call(
        lambda *refs: body(*refs[:6], *refs[6 + nb:]), grid=(NCHIP, ni),
        in_specs=[pl.BlockSpec((tm, D), lambda s, i: (i, 0)), pl.BlockSpec((tm, D), lambda s, i: (i, 0)),
                  _full((1, D)), wspec(D, FS, 0), wspec(D, FS, 1), wspec(FS, D, 0)] + [ANY] * nb,
        out_specs=[pl.BlockSpec((tm, D), last), _full((1, D)), wspec(2 * D, FS, 0), wspec(FS, D, 0)],
        out_shape=[_sds((T, D)), _sds((1, D)), _sds(gu.shape, BF16), _sds(wd.shape, BF16)],
        input_output_aliases={6 + k: 2 + k for k in range(nb)},
        scratch_shapes=[pltpu.VMEM((T, D), F32), pltpu.VMEM((D, FS), F32), pltpu.VMEM((D, FS), F32),
                        pltpu.VMEM((FS, D), F32)],
        name=name, compiler_params=_cp("arbitrary", "arbitrary"),
    )(x, dy, nw, gu, gu, wd, *(gbufs or ()))


CONV_ROWS = 256


def _conv_pad(k):
    return 8 * ((k - 1 + 7) // 8)


def _conv_fwd(name, x, w, b, act):
    k_w, c = w.shape
    tc = 256 if c % 256 == 0 else LANES
    pad = _conv_pad(k_w)
    has_b = b is not None

    def body(*refs):
        x_ref, w_ref = refs[0], refs[1]
        b_ref = refs[2] if has_b else None
        y_ref, xp = refs[2 + has_b], refs[3 + has_b]
        xp[0:pad, :] = jnp.zeros((pad, tc), F32)
        xp[pad:, :] = x_ref[...]

        def step(t, carry):
            base = pl.multiple_of(t * CONV_ROWS, CONV_ROWS)
            win = xp[pl.ds(base, CONV_ROWS + pad), :]
            acc = jnp.zeros((CONV_ROWS, tc), F32)
            for k in range(k_w):
                o = pad - (k_w - 1) + k
                acc = acc + w_ref[k:k + 1, :] * win[o:o + CONV_ROWS, :]
            if has_b:
                acc = acc + b_ref[...]
            y_ref[pl.ds(base, CONV_ROWS), :] = _silu(acc) if act else acc
            return carry

        lax.fori_loop(0, T // CONV_ROWS, step, 0)

    col = lambda r: pl.BlockSpec((r, tc), lambda j: (0, j))
    ins = [x, w] + ([b] if has_b else [])
    return pl.pallas_call(
        body, grid=(c // tc,), in_specs=[col(T), col(k_w)] + ([col(1)] if has_b else []), out_specs=col(T),
        out_shape=_sds((T, c)), scratch_shapes=[pltpu.VMEM((T + pad, tc), F32)], name=name,
        compiler_params=_cp("parallel"),
    )(*ins)


def _conv_bwd(name, x, w, b, act, dy):
    k_w, c = w.shape
    tc = 256 if c % 256 == 0 else LANES
    pad = _conv_pad(k_w)
    has_b = b is not None

    def body(*refs):
        x_ref, w_ref, dy_ref = refs[0], refs[1], refs[2]
        b_ref = refs[3] if has_b else None
        dx_ref, dw_ref, db_ref, xp, dp = refs[3 + has_b:]
        xp[0:pad, :] = jnp.zeros((pad, tc), F32)
        xp[pad:, :] = x_ref[...]
        dp[T:, :] = jnp.zeros((pad, tc), F32)
        dw_ref[...] = jnp.zeros_like(dw_ref)
        db_ref[...] = jnp.zeros_like(db_ref)

        def step1(t, carry):
            base = pl.multiple_of(t * CONV_ROWS, CONV_ROWS)
            d = dy_ref[pl.ds(base, CONV_ROWS), :]
            win = xp[pl.ds(base, CONV_ROWS + pad), :]
            offs = [pad - (k_w - 1) + k for k in range(k_w)]
            if act:
                acc = jnp.zeros((CONV_ROWS, tc), F32)
                for k, o in enumerate(offs):
                    acc = acc + w_ref[k:k + 1, :] * win[o:o + CONV_ROWS, :]
                if has_b:
                    acc = acc + b_ref[...]
                sg = jax.nn.sigmoid(acc)
                d = d * (sg * (1.0 + acc * (1.0 - sg)))
            dp[pl.ds(base, CONV_ROWS), :] = d
            for k, o in enumerate(offs):
                dw_ref[k:k + 1, :] += jnp.sum(d * win[o:o + CONV_ROWS, :], axis=0, keepdims=True)
            db_ref[...] += jnp.sum(d, axis=0, keepdims=True)
            return carry

        lax.fori_loop(0, T // CONV_ROWS, step1, 0)

        def step2(t, carry):
            base = pl.multiple_of(t * CONV_ROWS, CONV_ROWS)
            win = dp[pl.ds(base, CONV_ROWS + pad), :]
            acc = jnp.zeros((CONV_ROWS, tc), F32)
            for k in range(k_w):
                o = (k_w - 1) - k
                acc = acc + w_ref[k:k + 1, :] * win[o:o + CONV_ROWS, :]
            dx_ref[pl.ds(base, CONV_ROWS), :] = acc
            return carry

        lax.fori_loop(0, T // CONV_ROWS, step2, 0)

    col = lambda r: pl.BlockSpec((r, tc), lambda j: (0, j))
    ins = [x, w, dy] + ([b] if has_b else [])
    return pl.pallas_call(
        body, grid=(c // tc,), in_specs=[col(T), col(k_w), col(T)] + ([col(1)] if has_b else []),
        out_specs=[col(T), col(k_w), col(1)], out_shape=[_sds((T, c)), _sds((k_w, c)), _sds((1, c))],
        scratch_shapes=[pltpu.VMEM((T + pad, tc), F32), pltpu.VMEM((T + pad, tc), F32)], name=name,
        compiler_params=_cp("parallel"),
    )(*ins)


def _attn_consts(n):
    i = _iota2((BLOCK, 2 * BLOCK), 0)
    j = _iota2((BLOCK, 2 * BLOCK), 1)
    dist = i + BLOCK - j
    valid = (dist >= 0) & (dist < WINDOW) & ((n > 0) | (j >= BLOCK))
    return dist.astype(F32), valid


def _attn_block(q4, kk, vv, sinks, dist, valid, kv):
    outs = []
    lane = _iota2((1, HEADS), 1)
    for g in range(GROUP):
        h = kv * GROUP + g
        slope = 2.0 ** (-8.0 * (h + 1) / HEADS)
        s = _nt(q4[:, g * HDIM:(g + 1) * HDIM], kk) * (HDIM ** -0.5)
        s = jnp.where(valid, s - slope * dist, -1e30)
        sink = jnp.sum(jnp.where(lane == h, sinks, 0.0), axis=1, keepdims=True)
        m = jnp.maximum(jnp.max(s, axis=-1, keepdims=True), sink)
        e = jnp.exp(s - m)
        p = e / (jnp.sum(e, axis=-1, keepdims=True) + jnp.exp(sink - m))
        outs.append(_nn(p, vv))
    return tuple(outs)


def _attn_fwd(name, qa, ka, va, sinks):
    def body(q_ref, k_ref, v_ref, s_ref, o_ref, kp, vp):
        kp[0:BLOCK, :] = jnp.zeros((BLOCK, KV_A), F32)
        vp[0:BLOCK, :] = jnp.zeros((BLOCK, KV_A), F32)
        kp[BLOCK:, :] = k_ref[...]
        vp[BLOCK:, :] = v_ref[...]
        sinks_v = s_ref[...]

        def step(n, carry):
            r = pl.multiple_of(n * BLOCK, BLOCK)
            dist, valid = _attn_consts(n)
            k2 = kp[pl.ds(r, 2 * BLOCK), :]
            v2 = vp[pl.ds(r, 2 * BLOCK), :]
            for kv in range(KV_HEADS):
                q4 = q_ref[pl.ds(r, BLOCK), kv * GROUP * HDIM:(kv + 1) * GROUP * HDIM]
                og = _attn_block(q4, k2[:, kv * HDIM:(kv + 1) * HDIM], v2[:, kv * HDIM:(kv + 1) * HDIM], sinks_v,
                                 dist, valid, kv)
                for g in range(GROUP):
                    h = kv * GROUP + g
                    o_ref[pl.ds(r, BLOCK), h * HDIM:(h + 1) * HDIM] = og[g]
            return carry

        lax.fori_loop(0, T // BLOCK, step, 0)

    return pl.pallas_call(
        body, out_shape=_sds((T, Q_A)),
        scratch_shapes=[pltpu.VMEM((T + BLOCK, KV_A), F32), pltpu.VMEM((T + BLOCK, KV_A), F32)], name=name,
        compiler_params=pltpu.CompilerParams(vmem_limit_bytes=VMEM_LIMIT),
    )(qa, ka, va, sinks)


def _attn_bwd(name, qa, ka, va, sinks, do):
    def body(q_ref, k_ref, v_ref, s_ref, do_ref, dq_ref, dk_ref, dv_ref, ds_ref, kp, vp, dkp, dvp):
        kp[0:BLOCK, :] = jnp.zeros((BLOCK, KV_A), F32)
        vp[0:BLOCK, :] = jnp.zeros((BLOCK, KV_A), F32)
        kp[BLOCK:, :] = k_ref[...]
        vp[BLOCK:, :] = v_ref[...]
        dkp[...] = jnp.zeros_like(dkp)
        dvp[...] = jnp.zeros_like(dvp)
        ds_ref[...] = jnp.zeros_like(ds_ref)
        sinks_v = s_ref[...]

        def step(n, carry):
            r = pl.multiple_of(n * BLOCK, BLOCK)
            dist, valid = _attn_consts(n)
            k2 = kp[pl.ds(r, 2 * BLOCK), :]
            v2 = vp[pl.ds(r, 2 * BLOCK), :]
            for kv in range(KV_HEADS):
                cols = slice(kv * HDIM, (kv + 1) * HDIM)
                q4 = q_ref[pl.ds(r, BLOCK), kv * GROUP * HDIM:(kv + 1) * GROUP * HDIM]
                _, vjp = jax.vjp(lambda q, k, v, s: _attn_block(q, k, v, s, dist, valid, kv),
                                 q4, k2[:, cols], v2[:, cols], sinks_v)
                cts = tuple(do_ref[pl.ds(r, BLOCK), (kv * GROUP + g) * HDIM:(kv * GROUP + g + 1) * HDIM]
                            for g in range(GROUP))
                dq4, dkk, dvv, dsk = vjp(cts)
                dq_ref[pl.ds(r, BLOCK), kv * GROUP * HDIM:(kv + 1) * GROUP * HDIM] = dq4
                dkp[pl.ds(r, 2 * BLOCK), cols] += dkk
                dvp[pl.ds(r, 2 * BLOCK), cols] += dvv
                ds_ref[...] += dsk
            return carry

        lax.fori_loop(0, T // BLOCK, step, 0)
        dk_ref[...] = dkp[BLOCK:, :]
        dv_ref[...] = dvp[BLOCK:, :]

    pad = lambda: pltpu.VMEM((T + BLOCK, KV_A), F32)
    return pl.pallas_call(
        body, out_shape=[_sds((T, Q_A)), _sds((T, KV_A)), _sds((T, KV_A)), _sds((1, HEADS))],
        scratch_shapes=[pad(), pad(), pad(), pad()], name=name,
        compiler_params=pltpu.CompilerParams(vmem_limit_bytes=VMEM_LIMIT),
    )(qa, ka, va, sinks, do)


def _dn_consts():
    i = _iota2((CHUNK, CHUNK), 0)
    j = _iota2((CHUNK, CHUNK), 1)
    return dict(causal=i >= j, strict=i > j, eye=(i == j).astype(F32), ltri=(i >= j).astype(F32),
                ones=jnp.ones((CHUNK, CHUNK), F32), last=(_iota2((CHUNK, 1), 0) == CHUNK - 1).astype(F32))


def _l2norm(x):
    return x * lax.rsqrt(jnp.sum(x * x, axis=-1, keepdims=True) + EPS)


def _head_cols(m):
    lane = _iota2((1, HEADS), 1)
    return jnp.concatenate([jnp.sum(jnp.where(lane == h, m, 0.0), axis=1, keepdims=True)[None]
                            for h in range(HEADS)], axis=0)


def _dn_local(q3, k3, v3, braw, araw, alog, dtb, cs):
    q = _l2norm(q3) * (HDIM ** -0.5)
    k = _l2norm(k3)
    g = -jnp.exp(alog) * jax.nn.softplus(araw + dtb)
    gc_all = _nn_hi(cs["ltri"], g)
    egc_all = jnp.exp(gc_all)
    beta, gc, egc = _head_cols(jax.nn.sigmoid(braw)), _head_cols(gc_all), _head_cols(egc_all)
    a = jnp.broadcast_to(gc, (HEADS, CHUNK, CHUNK))
    diff = a - jnp.swapaxes(a, 1, 2)
    decay = jnp.where(cs["causal"], jnp.exp(jnp.where(cs["causal"], diff, 0.0)), 0.0)
    kb = k * beta
    low = jnp.where(cs["strict"], _nt(kb, k) * decay, 0.0)
    inv = cs["eye"] - low
    pw = low
    for _ in range(5):
        pw = _nn_hi(pw, pw)
        inv = inv + _nn_hi(inv, pw)
    u = _nn_hi(inv, v3 * beta)
    w = _nn_hi(inv, kb * egc)
    attn = _nt(q, k) * decay
    gc_last = jnp.sum(gc * cs["last"], axis=1, keepdims=True)
    return u, w, attn, q * egc, k * jnp.exp(gc_last - gc), egc_all


def _heads3(ref, off=0):
    return jnp.concatenate([ref[:, off + h * HDIM:off + (h + 1) * HDIM][None] for h in range(HEADS)], axis=0)


def _dn_local_fwd(name, qkv, ba, alog, dtb):
    def body(qkv_ref, ba_ref, al_ref, dt_ref, u_ref, w_ref, at_ref, qd_ref, kd_ref, eg_ref):
        bav = ba_ref[...]
        outs = _dn_local(_heads3(qkv_ref), _heads3(qkv_ref, 512), _heads3(qkv_ref, 1024), bav[:, :HEADS],
                         bav[:, HEADS:], al_ref[...], dt_ref[...], _dn_consts())
        for r, o in zip((u_ref, w_ref, at_ref, qd_ref, kd_ref), outs[:5]):
            for h in range(HEADS):
                r[:, h * HDIM:(h + 1) * HDIM] = o[h]
        eg_ref[...] = outs[5]

    row = lambda w_: pl.BlockSpec((CHUNK, w_), lambda n: (n, 0))
    return pl.pallas_call(
        body, grid=(NCHUNK,), in_specs=[row(QKV_B), row(2 * HEADS), _full((1, HEADS)), _full((1, HEADS))],
        out_specs=[row(V_B)] * 5 + [row(HEADS)], out_shape=[_sds((T, V_B))] * 5 + [_sds((T, HEADS))], name=name,
        compiler_params=_cp("parallel"),
    )(qkv, ba, alog, dtb)


def _dn_local_bwd(name, qkv, ba, alog, dtb, cts):
    def body(qkv_ref, ba_ref, al_ref, dt_ref, du_ref, dw_ref, dat_ref, dqd_ref, dkd_ref, deg_ref,
             dqkv_ref, dba_ref, dal_ref, ddt_ref):
        @pl.when(pl.program_id(0) == 0)
        def _():
            dal_ref[...] = jnp.zeros_like(dal_ref)
            ddt_ref[...] = jnp.zeros_like(ddt_ref)

        cs = _dn_consts()
        bav = ba_ref[...]
        _, vjp = jax.vjp(lambda *a: _dn_local(*a, cs), _heads3(qkv_ref), _heads3(qkv_ref, 512),
                         _heads3(qkv_ref, 1024), bav[:, :HEADS], bav[:, HEADS:], al_ref[...], dt_ref[...])
        dq, dk, dv, dbr, dar, dal, ddt = vjp((_heads3(du_ref), _heads3(dw_ref), _heads3(dat_ref), _heads3(dqd_ref),
                                              _heads3(dkd_ref), deg_ref[...]))
        for h in range(HEADS):
            dqkv_ref[:, h * HDIM:(h + 1) * HDIM] = dq[h]
            dqkv_ref[:, 512 + h * HDIM:512 + (h + 1) * HDIM] = dk[h]
            dqkv_ref[:, 1024 + h * HDIM:1024 + (h + 1) * HDIM] = dv[h]
        dba_ref[:, :HEADS] = dbr
        dba_ref[:, HEADS:] = dar
        dal_ref[...] += dal
        ddt_ref[...] += ddt

    row = lambda w_: pl.BlockSpec((CHUNK, w_), lambda n: (n, 0))
    return pl.pallas_call(
        body, grid=(NCHUNK,),
        in_specs=[row(QKV_B), row(2 * HEADS), _full((1, HEADS)), _full((1, HEADS))] + [row(V_B)] * 5 + [row(HEADS)],
        out_specs=[row(QKV_B), row(2 * HEADS), _full((1, HEADS)), _full((1, HEADS))],
        out_shape=[_sds((T, QKV_B)), _sds((T, 2 * HEADS)), _sds((1, HEADS)), _sds((1, HEADS))], name=name,
        compiler_params=_cp("arbitrary"),
    )(qkv, ba, alog, dtb, *cts)


def _dn_step(s, u, w, attn, qd, kd, egc, z, nw):
    last = (_iota2((CHUNK, 1), 0) == CHUNK - 1).astype(F32)
    gl = jnp.sum(_head_cols(egc) * last, axis=1, keepdims=True)
    v_new = u - _nn(w, s)
    o = _nn(qd, s) + _nn(attn, v_new)
    s_new = s * gl + _tn(kd, v_new)
    return s_new, _rms(o, nw) * _silu(z)


def _unheads(ref, v3):
    for h in range(HEADS):
        ref[:, h * HDIM:(h + 1) * HDIM] = v3[h]


def _dn_rec_fwd(name, u, w, attn, qd, kd, egc, z, nw):
    def body(u_ref, w_ref, at_ref, qd_ref, kd_ref, eg_ref, z_ref, nw_ref, o_ref, ss_ref, s_scr):
        @pl.when(pl.program_id(0) == 0)
        def _():
            s_scr[...] = jnp.zeros_like(s_scr)

        s = s_scr[...]
        ss_ref[...] = s
        s_new, on = _dn_step(s, _heads3(u_ref), _heads3(w_ref), _heads3(at_ref), _heads3(qd_ref), _heads3(kd_ref),
                             eg_ref[...], _heads3(z_ref), nw_ref[...])
        s_scr[...] = s_new
        _unheads(o_ref, on)

    row = lambda w_: pl.BlockSpec((CHUNK, w_), lambda n: (n, 0))
    return pl.pallas_call(
        body, grid=(NCHUNK,), in_specs=[row(V_B)] * 5 + [row(HEADS), row(V_B), _full((1, HDIM))],
        out_specs=[row(V_B), pl.BlockSpec((None, HEADS, HDIM, HDIM), lambda n: (n, 0, 0, 0))],
        out_shape=[_sds((T, V_B)), _sds((NCHUNK, HEADS, HDIM, HDIM))],
        scratch_shapes=[pltpu.VMEM((HEADS, HDIM, HDIM), F32)], name=name, compiler_params=_cp("arbitrary"),
    )(u, w, attn, qd, kd, egc, z, nw)


def _dn_rec_bwd(name, u, w, attn, qd, kd, egc, z, nw, ss, do):
    def body(u_ref, w_ref, at_ref, qd_ref, kd_ref, eg_ref, z_ref, nw_ref, ss_ref, do_ref,
             du_ref, dw_ref, dat_ref, dqd_ref, dkd_ref, deg_ref, dz_ref, dnw_ref, ds_scr):
        @pl.when(pl.program_id(0) == 0)
        def _():
            ds_scr[...] = jnp.zeros_like(ds_scr)
            dnw_ref[...] = jnp.zeros_like(dnw_ref)

        _, vjp = jax.vjp(_dn_step, ss_ref[...], _heads3(u_ref), _heads3(w_ref), _heads3(at_ref), _heads3(qd_ref),
                         _heads3(kd_ref), eg_ref[...], _heads3(z_ref), nw_ref[...])
        ds, du, dw, dat, dqd, dkd, deg, dz, dnw = vjp((ds_scr[...], _heads3(do_ref)))
        ds_scr[...] = ds
        for r, v in zip((du_ref, dw_ref, dat_ref, dqd_ref, dkd_ref, dz_ref), (du, dw, dat, dqd, dkd, dz)):
            _unheads(r, v)
        deg_ref[...] = deg
        dnw_ref[...] += dnw

    row = lambda w_: pl.BlockSpec((CHUNK, w_), lambda n: (NCHUNK - 1 - n, 0))
    return pl.pallas_call(
        body, grid=(NCHUNK,),
        in_specs=[row(V_B)] * 5 + [row(HEADS), row(V_B), _full((1, HDIM)),
                                   pl.BlockSpec((None, HEADS, HDIM, HDIM), lambda n: (NCHUNK - 1 - n, 0, 0, 0)),
                                   row(V_B)],
        out_specs=[row(V_B)] * 5 + [row(HEADS), row(V_B), _full((1, HDIM))],
        out_shape=[_sds((T, V_B))] * 5 + [_sds((T, HEADS)), _sds((T, V_B)), _sds((1, HDIM))],
        scratch_shapes=[pltpu.VMEM((HEADS, HDIM, HDIM), F32)], name=name, compiler_params=_cp("arbitrary"),
    )(u, w, attn, qd, kd, egc, z, nw, ss, do)


def _final(name, x, fw, target, tm=512):
    def body(x_ref, fw_ref, t_ref, l_ref, dx_ref, dfw_ref):
        @pl.when(pl.program_id(0) == 0)
        def _():
            l_ref[...] = jnp.zeros_like(l_ref)
            dfw_ref[...] = jnp.zeros_like(dfw_ref)

        tv = t_ref[...]

        def f(xv, fwv):
            err = _rms(xv, fwv) - tv
            per_tok = jnp.mean(err * err, axis=-1, keepdims=True)
            return 0.5 * jnp.sum(per_tok, axis=0, keepdims=True)

        loss, vjp = jax.vjp(f, x_ref[...], fw_ref[...])
        dx, dfw = vjp(jnp.ones((1, 1), F32))
        l_ref[...] += loss
        dx_ref[...] = dx
        dfw_ref[...] += dfw

    tok = pl.BlockSpec((tm, D), lambda i: (i, 0))
    return pl.pallas_call(
        body, grid=(T // tm,), in_specs=[tok, _full((1, D)), tok], out_specs=[_full((1, 1)), tok, _full((1, D))],
        out_shape=[_sds((1, 1)), _sds((T, D)), _sds((1, D))], name=name, compiler_params=_cp("arbitrary"),
    )(x, fw, target)


def _m1_pre(tv, sv):
    return [_rms(tv[0], sv[0])]


def _m1_post(ys, tv, sv):
    return (jnp.concatenate(ys, axis=1),)


def _m1_post_split(ys, tv, sv):
    proj = jnp.concatenate(ys, axis=1)
    return tuple(proj[:, a:b] for a, b in zip(IN_SPLITS[:-1], IN_SPLITS[1:]))


def _m5_pre(tv, sv):
    return [tv[1], tv[2]]


def _m5_post(ys, tv, sv):
    return (tv[0] + ys[0] + ys[1],)


def _c1_pre(tv, sv):
    return [_rms(tv[0], sv[0])]


def _c1_post(ys, tv, sv):
    return ((jnp.concatenate(ys[:2], axis=1) + sv[1]) * jax.nn.sigmoid(jnp.concatenate(ys[2:], axis=1) + sv[2]),)


def _c3_pre(tv, sv):
    return [_silu(_layernorm(tv[0], sv[0], sv[1]))]


def _c3_post(ys, tv, sv):
    return (tv[1] + ys[0] + sv[2],)


def _row(v):
    return v.reshape(1, -1)


def _mixer_fwd(tag, x, p):
    parts = _blk_fwd(f"m1_fwd_{tag}", _m1_pre, [0], _m1_post_split, [x], [p["nw"]], [p["w_in"]],
                     [(b - a, F32) for a, b in zip(IN_SPLITS[:-1], IN_SPLITS[1:])])
    qa, ka, va, qkvb, z, ba = parts
    att = _attn_fwd(f"attn_fwd_{tag}", qa, ka, va, p["sinks"])
    qkvc = _conv_fwd(f"dnconv_fwd_{tag}", qkvb, p["dn_conv_w"], None, True)
    loc = _dn_local_fwd(f"dnloc_fwd_{tag}", qkvc, ba, p["a_log"], p["dt_bias"])
    og, ss = _dn_rec_fwd(f"dnrec_fwd_{tag}", *loc, z, p["dn_norm_w"])
    (out,) = _blk_fwd(f"m5_fwd_{tag}", _m5_pre, [0, 1], _m5_post, [x, att, og], [], [p["wo_a"], p["wo_b"]],
                      [(D, F32)])
    return out, dict(x=x, qa=qa, ka=ka, va=va, qkvb=qkvb, z=z, ba=ba, att=att, qkvc=qkvc, loc=loc, og=og, ss=ss)


def _mixer_bwd(tag, dy, p, s):
    (dxa, datt, dog), _, (dwo_a, dwo_b) = _blk_bwd(f"m5_bwd_{tag}", _m5_pre, [0, 1], _m5_post,
                                                   [s["x"], s["att"], s["og"]], [], [p["wo_a"], p["wo_b"]], [[dy]])
    rec = _dn_rec_bwd(f"dnrec_bwd_{tag}", *s["loc"], s["z"], p["dn_norm_w"], s["ss"], dog)
    dz, dnw_dn = rec[6], rec[7]
    dqkvc, dba, dalog, ddtb = _dn_local_bwd(f"dnloc_bwd_{tag}", s["qkvc"], s["ba"], p["a_log"], p["dt_bias"],
                                            rec[:6])
    dqkvb, dconvw, _ = _conv_bwd(f"dnconv_bwd_{tag}", s["qkvb"], p["dn_conv_w"], None, True, dqkvc)
    dqa, dka, dva, dsinks = _attn_bwd(f"attn_bwd_{tag}", s["qa"], s["ka"], s["va"], p["sinks"], datt)
    (dx,), (dnw,), (dw_in,) = _blk_bwd(f"m1_bwd_{tag}", _m1_pre, [0], _m1_post, [s["x"]], [p["nw"]], [p["w_in"]],
                                       [[dqa, dka, dva, dqkvb, dz, dba]], res=dxa)
    return dx, dict(nw=dnw, w_in=dw_in, wo_a=dwo_a, wo_b=dwo_b, dn_conv_w=dconvw, sinks=dsinks, a_log=dalog,
                    dt_bias=ddtb, dn_norm_w=dnw_dn)


def _conformer_fwd(tag, x, p):
    (glu,) = _blk_fwd(f"c1_fwd_{tag}", _c1_pre, [0], _c1_post, [x], [p["nw"], p["b1a"], p["b1b"]], [p["w1"]],
                      [(D, F32)])
    cc = _conv_fwd(f"dwconv_fwd_{tag}", glu, p["w_dw"], p["b_dw"], False)
    (out,) = _blk_fwd(f"c3_fwd_{tag}", _c3_pre, [0], _c3_post, [cc, x], [p["ln_w"], p["ln_b"], p["b2"]], [p["w2"]],
                      [(D, F32)])
    return out, dict(x=x, glu=glu, cc=cc)


def _conformer_bwd(tag, dy, p, s):
    (dcc, dxa), (dlnw, dlnb, db2), (dw2,) = _blk_bwd(f"c3_bwd_{tag}", _c3_pre, [0], _c3_post, [s["cc"], s["x"]],
                                                     [p["ln_w"], p["ln_b"], p["b2"]], [p["w2"]], [[dy]])
    dglu, dwdw, dbdw = _conv_bwd(f"dwconv_bwd_{tag}", s["glu"], p["w_dw"], p["b_dw"], False, dcc)
    (dx,), (dnw, db1a, db1b), (dw1,) = _blk_bwd(f"c1_bwd_{tag}", _c1_pre, [0], _c1_post, [s["x"]],
                                                [p["nw"], p["b1a"], p["b1b"]], [p["w1"]], [[dglu]], res=dxa)
    return dx, dict(nw=dnw, b1a=db1a, b1b=db1b, w1=dw1, w_dw=dwdw, b_dw=dbdw, ln_w=dlnw, ln_b=dlnb, b2=db2, w2=dw2)


def _layer_fwd(l, x, nw, ffn, p):
    x1 = _ffn_fwd(f"ffn_fwd_{l}a", x, _row(nw[0]), *ffn, 0)
    p = dict(p, nw=_row(nw[1]))
    x2, sv = (_mixer_fwd if l % 2 == 0 else _conformer_fwd)(str(l), x1, p)
    return _ffn_fwd(f"ffn_fwd_{l}b", x2, _row(nw[2]), *ffn, 1), (x, x2, p, sv)


def _layer_bwd(l, dx, nw, ffn, saved, after_first=lambda dx: dx):
    x0, x2, p, sv = saved
    dx, dn2, *dffn = _ffn_bwd(f"ffn_bwd_{l}b", x2, _row(nw[2]), *ffn, 1, dx)
    dx = after_first(dx)
    dx, dmix = (_mixer_bwd if l % 2 == 0 else _conformer_bwd)(str(l), dx, p, sv)
    dx, dn0, *dffn = _ffn_bwd(f"ffn_bwd_{l}a", x0, _row(nw[0]), *ffn, 0, dx, dffn)
    return dx, jnp.concatenate([dn0, dmix.pop("nw"), dn2], axis=0), dffn, dmix


def _place():
    x, y, c = lax.axis_index("x"), lax.axis_index("y"), lax.axis_index("c")
    chips = [(1 - x, y), (x, 1 - y), (1 - x, 1 - y)]
    return x, y, c, 2 * x + y, chips, [2 * px + py for px, py in chips]


def _handshake(peers):
    barrier = pltpu.get_barrier_semaphore()
    for p in peers:
        pl.semaphore_signal(barrier, inc=1, device_id=p, device_id_type=MESH)
    pl.semaphore_wait(barrier, len(peers))


def _chip_peers():
    x, y, c, _, chips, _ = _place()
    return [(*chip, c) for chip in chips] + [(x, y, 1 - c)]


def _gather_copies(ins, outs, nb, send, recv, fsend, frecv, lsem):
    n_in = len(ins)
    x, y, c, me, chips, cidx = _place()
    sib = (x, y, 1 - c)
    local = [pltpu.make_async_copy(ins[a], outs[a].at[me], lsem.at[a]) for a in range(n_in)]
    for cp in local:
        cp.start()

    def ici(a, j):
        k = a * 3 + j
        src, dst = (ins[a], outs[a].at[me]) if a >= nb else (ins[a].at[pl.ds(c, 1)], outs[a].at[me, pl.ds(c, 1)])
        return pltpu.make_async_remote_copy(src, dst, send.at[k], recv.at[k], device_id=(*chips[j], c),
                                            device_id_type=MESH)

    def landed(a, j):
        k = a * 3 + j
        dst = outs[a].at[cidx[j]] if a >= nb else outs[a].at[cidx[j], pl.ds(c, 1)]
        return pltpu.make_async_remote_copy(dst, dst, send.at[k], recv.at[k], device_id=(*chips[j], c),
                                            device_id_type=MESH)

    def passed(a, j, who):
        k = a * 3 + j
        part = outs[a].at[cidx[j], pl.ds(who, 1)]
        return pltpu.make_async_remote_copy(part, part, fsend.at[k], frecv.at[k], device_id=sib, device_id_type=MESH)

    sends = [ici(a, j) for a in range(n_in) for j in range(3)]
    for cp in sends:
        cp.start()
    for a in range(nb):
        for j in range(3):
            landed(a, j).wait_recv()
            cp = passed(a, j, c)
            cp.start()
            sends.append(cp)
    for a in range(nb, n_in):
        for j in range(3):
            landed(a, j).wait_recv()
    for a in range(nb):
        for j in range(3):
            passed(a, j, 1 - c).wait_recv()
    for cp in sends:
        cp.wait_send()
    for cp in local:
        cp.wait()


def _gather_sems(n_in, nb):
    dma = pltpu.SemaphoreType.DMA
    return [dma((3 * n_in,)), dma((3 * n_in,)), dma((3 * nb,)), dma((3 * nb,)), dma((n_in,))]


def _gather_async(name, halved, whole=()):
    nb, arrs = len(halved), list(halved) + list(whole)
    hbm = pltpu.MemorySpace.HBM
    ins = [jax.new_ref(a, memory_space=hbm) for a in arrs]
    outs = [jax.empty_ref(_sds((NCHIP,) + a.shape, a.dtype), memory_space=hbm) for a in arrs]

    @pl.kernel(mesh=plsc.ScalarSubcoreMesh(axis_name="seq", num_cores=1), name=name,
               scratch_types=tuple(_gather_sems(len(arrs), nb)),
               compiler_params=pltpu.CompilerParams(collective_id=2))
    def launch(send, recv, fsend, frecv, lsem):
        _handshake(_chip_peers())
        _gather_copies(ins, outs, nb, send, recv, fsend, frecv, lsem)

    launch()
    return outs


def _swap_halves(name, grads, after=None):
    n = len(grads)
    hbm = pltpu.MemorySpace.HBM
    ins = [jax.new_ref(g, memory_space=hbm) for g in grads]
    outs = [jax.empty_ref(_sds((NCHIP, g.shape[1] // 2) + g.shape[2:], g.dtype), memory_space=hbm) for g in grads]
    tile = (2 * 8, LANES)
    token = None if after is None else jax.empty_ref(_sds(tile, BF16), memory_space=hbm)

    @pl.kernel(mesh=plsc.ScalarSubcoreMesh(axis_name="seq", num_cores=1), name=name,
               scratch_types=(pltpu.SemaphoreType.DMA((n + 1,)), pltpu.SemaphoreType.DMA((n,))),
               compiler_params=pltpu.CompilerParams(collective_id=1))
    def launch(send, recv):
        x, y, c, _, _, _ = _place()
        sib = (x, y, 1 - c)
        _handshake([sib])
        if after is not None:
            tick = pltpu.make_async_copy(after.at[0, 0, 0, pl.ds(0, tile[0]), pl.ds(0, tile[1])], token, send.at[n])
            tick.start()
            tick.wait()
        cps = []
        for a in range(n):
            h = grads[a].shape[1] // 2
            cps.append(pltpu.make_async_remote_copy(ins[a].at[:, pl.ds((1 - c) * h, h)], outs[a], send.at[a],
                                                    recv.at[a], device_id=sib, device_id_type=MESH))
        for cp in cps:
            cp.start()
        for cp in cps:
            cp.wait()

    launch()
    return outs


def _row_tile(r, cap=256):
    return max(t for t in range(8, cap + 1, 8) if r % t == 0)


def _add_half(name, g, r, c_arr):
    _, l, rows, cols = g.shape
    h = l // 2
    tr = _row_tile(rows)

    def body(c_ref, g_ref, r_ref, o_ref):
        o_ref[...] = (g_ref[...].astype(F32) + r_ref[...].astype(F32)).astype(BF16)

    blk = (None, None, tr, cols)
    return pl.pallas_call(
        body,
        grid_spec=pltpu.PrefetchScalarGridSpec(
            num_scalar_prefetch=1, grid=(NCHIP, h, rows // tr),
            in_specs=[pl.BlockSpec(blk, lambda j, i, t, c_ref: (j, c_ref[0] * h + i, t, 0)),
                      pl.BlockSpec(blk, lambda j, i, t, c_ref: (j, i, t, 0))],
            out_specs=pl.BlockSpec(blk, lambda j, i, t, c_ref: (j, i, t, 0))),
        out_shape=_sds((NCHIP, h, rows, cols), BF16), name=name,
        compiler_params=_cp("parallel", "parallel", "parallel"),
    )(c_arr, g, r)


def _scatter_async(name, parts, sums, where):
    nb = len(parts)
    ins = [jax.new_ref(p, memory_space=pltpu.MemorySpace.HBM) for p in parts]
    dma = pltpu.SemaphoreType.DMA

    @pl.kernel(mesh=plsc.ScalarSubcoreMesh(axis_name="seq", num_cores=1), name=name,
               scratch_types=(dma((3 * nb,)), dma((3 * nb,)), dma((4 * nb,)), dma((4 * nb,)), dma((nb,))),
               compiler_params=pltpu.CompilerParams(collective_id=3))
    def launch(send, recv, fsend, frecv, lsem):
        _handshake(_chip_peers())
        x, y, c, me, chips, cidx = _place()
        sib = (x, y, 1 - c)

        def slot(a, half, chip):
            return sums[a].at[half, chip, pl.ds(where[a], 1)]

        local = [pltpu.make_async_copy(ins[a].at[me], slot(a, c, me), lsem.at[a]) for a in range(nb)]
        for cp in local:
            cp.start()

        def ici(a, j):
            return pltpu.make_async_remote_copy(ins[a].at[cidx[j]], slot(a, c, me), send.at[a * 3 + j],
                                                recv.at[a * 3 + j], device_id=(*chips[j], c), device_id_type=MESH)

        def landed(a, j):
            dst = slot(a, c, cidx[j])
            return pltpu.make_async_remote_copy(dst, dst, send.at[a * 3 + j], recv.at[a * 3 + j],
                                                device_id=(*chips[j], c), device_id_type=MESH)

        def passed(a, j, who):
            dst = slot(a, who, me if j == 3 else cidx[j])
            src = ins[a].at[me] if j == 3 else dst
            return pltpu.make_async_remote_copy(src, dst, fsend.at[a * 4 + j], frecv.at[a * 4 + j], device_id=sib,
                                                device_id_type=MESH)

        sends = [ici(a, j) for a in range(nb) for j in range(3)] + [passed(a, 3, c) for a in range(nb)]
        for cp in sends:
            cp.start()
        for a in range(nb):
            for j in range(3):
                landed(a, j).wait_recv()
                cp = passed(a, j, c)
                cp.start()
                sends.append(cp)
        for a in range(nb):
            for j in range(4):
                passed(a, j, 1 - c).wait_recv()
        for cp in sends:
            cp.wait_send()
        for cp in local:
            cp.wait()

    launch()


def _exchange_small(small, rep):
    def body(small_in, rep_in, small_out, rep_out, lsem, ssend, srecv):
        x, y, c, me, _, _ = _place()
        dev = 4 * x + 2 * y + c
        local = [pltpu.make_async_copy(small_in.at[me], small_out.at[dev], lsem.at[0]),
                 pltpu.make_async_copy(rep_in, rep_out.at[dev], lsem.at[1])]
        for cp in local:
            cp.start()

        def peer(r):
            return (1 - x if r & 4 else x), (1 - y if r & 2 else y), (1 - c if r & 1 else c)

        def tiny(r, which):
            px, py, pc = peer(r)
            k = (r - 1) * 2 + which
            if which == 0:
                return pltpu.make_async_remote_copy(small_in.at[2 * px + py], small_out.at[dev], ssend.at[k],
                                                    srecv.at[k], device_id=(px, py, pc), device_id_type=MESH)
            return pltpu.make_async_remote_copy(rep_in, rep_out.at[dev], ssend.at[k], srecv.at[k],
                                                device_id=(px, py, pc), device_id_type=MESH)

        def tiny_landed(r, which):
            px, py, pc = peer(r)
            k = (r - 1) * 2 + which
            dst = (small_out if which == 0 else rep_out).at[4 * px + 2 * py + pc]
            return pltpu.make_async_remote_copy(dst, dst, ssend.at[k], srecv.at[k], device_id=(px, py, pc),
                                                device_id_type=MESH)

        sends = [tiny(r, w) for r in range(1, NDEV) for w in range(2)]
        for cp in sends:
            cp.start()
        for r in range(1, NDEV):
            for w in range(2):
                tiny_landed(r, w).wait_recv()
        for cp in sends:
            cp.wait_send()
        for cp in local:
            cp.wait()

    dma = pltpu.SemaphoreType.DMA
    return pl.pallas_call(
        body, in_specs=[ANY] * 2, out_specs=[ANY] * 2,
        out_shape=[_sds((NDEV,) + small.shape[1:], F32), _sds((NDEV,) + rep.shape, F32)],
        scratch_shapes=[dma((2,)), dma((2 * (NDEV - 1),)), dma((2 * (NDEV - 1),))], name="exchange_small_grads",
    )(small, rep)


def _adamw_math(w, g, m, v):
    m = B1 * m + (1.0 - B1) * g
    v = B2 * v + (1.0 - B2) * (g * g)
    m_hat = m / (1.0 - B1 ** STEP)
    v_hat = v / (1.0 - B2 ** STEP)
    return -LR * (m_hat / (jnp.sqrt(v_hat) + AEPS) + WD * w), m, v


def _adamw_big(name, w, m, v, parts, row0=0):
    n, _, rows, cols = w.shape
    tr = _row_tile(rows)
    t0 = row0 // tr

    def body(w_ref, m_ref, v_ref, p_ref, g_ref, d_ref, nm_ref, nv_ref):
        g = p_ref[0].astype(F32)
        for q in range(1, NCHIP):
            g = g + p_ref[q].astype(F32)
        d, nm, nv = _adamw_math(w_ref[...], g, m_ref[...], v_ref[...])
        g_ref[...], d_ref[...], nm_ref[...], nv_ref[...] = g, d, nm, nv

    spec = pl.BlockSpec((None, None, tr, cols), lambda i, p, t: (i, p, t, 0))
    return pl.pallas_call(
        body, grid=(n, 2, rows // tr),
        in_specs=[spec, spec, spec,
                  pl.BlockSpec((None, NCHIP, None, tr, cols), lambda i, p, t: (p, 0, i, t0 + t, 0))],
        out_specs=[spec] * 4, out_shape=[_sds(w.shape)] * 4, name=name,
        compiler_params=_cp("parallel", "parallel", "parallel"),
    )(w, m, v, parts)


def _adamw_small(name, w, m, v, parts):
    def body(w_ref, m_ref, v_ref, p_ref, g_ref, d_ref, nm_ref, nv_ref):
        g = p_ref[0]
        for q in range(1, NDEV):
            g = g + p_ref[q]
        d, nm, nv = _adamw_math(w_ref[...], g, m_ref[...], v_ref[...])
        g_ref[...], d_ref[...], nm_ref[...], nv_ref[...] = g, d, nm, nv

    return pl.pallas_call(body, out_shape=[_sds(w.shape)] * 4, name=name)(w, m, v, parts)


def _pack(arrs, rows):
    flat = jnp.concatenate([a.reshape(-1) for a in arrs])
    return jnp.pad(flat, (0, rows * LANES - flat.shape[0])).reshape(rows, LANES)


def _unpack(packed, shapes):
    flat, out, o = packed.reshape(-1), [], 0
    for s in shapes:
        n = 1
        for d in s:
            n *= d
        out.append(flat[o:o + n].reshape(s))
        o += n
    return out


SMALL_ROWS, REP_ROWS = 200, 16


def kernel(x, norm_w, ffn_w_gate, ffn_w_up, ffn_w_down, mix_w_in, dn_conv_w, attn_sinks, dn_a_log, dn_dt_bias, dn_norm_w, mix_w_out, conv_w_pw1, conv_b_pw1, conv_w_dw, conv_b_dw, conv_ln_w, conv_ln_b, conv_w_pw2, conv_b_pw2, final_norm_w, loss_target, m_norm_w, m_ffn_w_gate, m_ffn_w_up, m_ffn_w_down, m_mix_w_in, m_dn_conv_w, m_attn_sinks, m_dn_a_log, m_dn_dt_bias, m_dn_norm_w, m_mix_w_out, m_conv_w_pw1, m_conv_b_pw1, m_conv_w_dw, m_conv_b_dw, m_conv_ln_w, m_conv_ln_b, m_conv_w_pw2, m_conv_b_pw2, m_final_norm_w, v_norm_w, v_ffn_w_gate, v_ffn_w_up, v_ffn_w_down, v_mix_w_in, v_dn_conv_w, v_attn_sinks, v_dn_a_log, v_dn_dt_bias, v_dn_norm_w, v_mix_w_out, v_conv_w_pw1, v_conv_b_pw1, v_conv_w_dw, v_conv_b_dw, v_conv_ln_w, v_conv_ln_b, v_conv_w_pw2, v_conv_b_pw2, v_final_norm_w):
    small_names = ["norm_w", "dn_conv_w", "conv_b_pw1", "conv_w_dw", "conv_b_dw", "conv_ln_w", "conv_ln_b",
                   "conv_b_pw2"]
    rep_names = ["attn_sinks", "dn_a_log", "dn_dt_bias", "dn_norm_w", "final_norm_w"]
    w = dict(norm_w=norm_w, ffn_w_gate=ffn_w_gate, ffn_w_up=ffn_w_up, ffn_w_down=ffn_w_down, mix_w_in=mix_w_in, dn_conv_w=dn_conv_w, attn_sinks=attn_sinks, dn_a_log=dn_a_log, dn_dt_bias=dn_dt_bias, dn_norm_w=dn_norm_w, mix_w_out=mix_w_out, conv_w_pw1=conv_w_pw1, conv_b_pw1=conv_b_pw1, conv_w_dw=conv_w_dw, conv_b_dw=conv_b_dw, conv_ln_w=conv_ln_w, conv_ln_b=conv_ln_b, conv_w_pw2=conv_w_pw2, conv_b_pw2=conv_b_pw2, final_norm_w=final_norm_w)
    m = dict(norm_w=m_norm_w, ffn_w_gate=m_ffn_w_gate, ffn_w_up=m_ffn_w_up, ffn_w_down=m_ffn_w_down, mix_w_in=m_mix_w_in, dn_conv_w=m_dn_conv_w, attn_sinks=m_attn_sinks, dn_a_log=m_dn_a_log, dn_dt_bias=m_dn_dt_bias, dn_norm_w=m_dn_norm_w, mix_w_out=m_mix_w_out, conv_w_pw1=m_conv_w_pw1, conv_b_pw1=m_conv_b_pw1, conv_w_dw=m_conv_w_dw, conv_b_dw=m_conv_b_dw, conv_ln_w=m_conv_ln_w, conv_ln_b=m_conv_ln_b, conv_w_pw2=m_conv_w_pw2, conv_b_pw2=m_conv_b_pw2, final_norm_w=m_final_norm_w)
    v = dict(norm_w=v_norm_w, ffn_w_gate=v_ffn_w_gate, ffn_w_up=v_ffn_w_up, ffn_w_down=v_ffn_w_down, mix_w_in=v_mix_w_in, dn_conv_w=v_dn_conv_w, attn_sinks=v_attn_sinks, dn_a_log=v_dn_a_log, dn_dt_bias=v_dn_dt_bias, dn_norm_w=v_dn_norm_w, mix_w_out=v_mix_w_out, conv_w_pw1=v_conv_w_pw1, conv_b_pw1=v_conv_b_pw1, conv_w_dw=v_conv_w_dw, conv_b_dw=v_conv_b_dw, conv_ln_w=v_conv_ln_w, conv_ln_b=v_conv_ln_b, conv_w_pw2=v_conv_w_pw2, conv_b_pw2=v_conv_b_pw2, final_norm_w=v_final_norm_w)
    order = ["norm_w", "ffn_w_gate", "ffn_w_up", "ffn_w_down", "mix_w_in", "dn_conv_w", "attn_sinks", "dn_a_log",
             "dn_dt_bias", "dn_norm_w", "mix_w_out", "conv_w_pw1", "conv_b_pw1", "conv_w_dw", "conv_b_dw",
             "conv_ln_w", "conv_ln_b", "conv_w_pw2", "conv_b_pw2", "final_norm_w"]

    small_shapes = [w[n].shape for n in small_names]
    rep_shapes = [w[n].shape for n in rep_names]

    def halves(a):
        return a.reshape(a.shape[:-2] + (2, a.shape[-2] // 2, a.shape[-1]))

    def layer_shards(l):
        mix_in, mix_out = (mix_w_in, mix_w_out) if l % 2 == 0 else (conv_w_pw1, conv_w_pw2)
        return [t.astype(BF16) for t in (jnp.concatenate([ffn_w_gate[l], ffn_w_up[l]], axis=1), ffn_w_down[l],
                                         halves(mix_in[l // 2]), halves(mix_out[l // 2]))]

    first = layer_shards(0) + [_pack([w[n] for n in small_names], SMALL_ROWS)]
    first, (ffn_w_gate, ffn_w_up, ffn_w_down, mix_w_in, mix_w_out, conv_w_pw1, conv_w_pw2) = lax.optimization_barrier(
        (first, (ffn_w_gate, ffn_w_up, ffn_w_down, mix_w_in, mix_w_out, conv_w_pw1, conv_w_pw2)))
    gathering = [_gather_async("gather_layer0", first[:4], first[4:])]
    gathering += [_gather_async(f"gather_layer{l}", layer_shards(l)) for l in range(1, DEPTH)]

    def mixer_params(l, w_a, w_b):
        e = l // 2
        w_a = w_a.reshape(NCHIP, D, -1)
        w_b = w_b.reshape(D, D)
        if l % 2 == 0:
            return dict(w_in=w_a, dn_conv_w=sm["dn_conv_w"][e], sinks=_row(attn_sinks[e]), a_log=_row(dn_a_log[e]),
                        dt_bias=_row(dn_dt_bias[e]), dn_norm_w=_row(dn_norm_w[e]), wo_a=w_b[:Q_A], wo_b=w_b[Q_A:])
        return dict(b1a=_row(sm["conv_b_pw1"][e, :D]), b1b=_row(sm["conv_b_pw1"][e, D:]), w1=w_a,
                    w_dw=sm["conv_w_dw"][e], b_dw=_row(sm["conv_b_dw"][e]), ln_w=_row(sm["conv_ln_w"][e]),
                    ln_b=_row(sm["conv_ln_b"][e]), b2=_row(sm["conv_b_pw2"][e]), w2=w_b)

    xs, saved, ffn_w = x[0], [], []
    for l in range(DEPTH):
        got = [r[...] for r in gathering[l]]
        if l == 0:
            per_chip = [_unpack(got[4][q], small_shapes) for q in range(NCHIP)]
            sm = {n: jnp.concatenate([per_chip[q][i] for q in range(NCHIP)], axis=-1)
                  for i, n in enumerate(small_names)}
        else:
            xs, got = lax.optimization_barrier((xs, got))
        ffn_w.append(got[:2])
        xs, sv = _layer_fwd(l, xs, sm["norm_w"][l], got[:2], mixer_params(l, got[2], got[3]))
        saved.append(sv)
    loss, dx, dfw = _final("final", xs, _row(final_norm_w), loss_target[0])

    hbm = pltpu.MemorySpace.HBM
    sum_shapes = dict(gu=(DEPTH, 2 * D, FS), down=(DEPTH, FS, D), w_in=(2, D // 2, IN_COLS // NCHIP),
                      w_out=(2, D // 8, D), pw1=(2, D // 2, D // 2), pw2=(2, D // 8, D))
    sums = {k: jax.empty_ref(_sds((2, NCHIP) + s, BF16), memory_space=hbm) for k, s in sum_shapes.items()}
    c_arr = lax.axis_index("c").astype(jnp.int32).reshape(1)
    dnorm, gmix = [None] * DEPTH, [None] * DEPTH

    def hand_on(l, grads, swapped):
        def run(dx):
            dx, other = lax.optimization_barrier((dx, [r[...] for r in swapped]))
            parts = [_add_half(f"add_half_{l}_{k}", gg, rr, c_arr) for k, (gg, rr) in enumerate(zip(grads, other))]
            dx, parts = lax.optimization_barrier((dx, parts))
            keys = ("gu", "down", "w_in", "w_out") if l % 2 == 0 else ("gu", "down", "pw1", "pw2")
            _scatter_async(f"scatter_grads_{l}", parts, [sums[k] for k in keys], [l, l, l // 2, l // 2])
            return dx
        return run

    pending = lambda dx: dx
    for l in reversed(range(DEPTH)):
        dx, dnorm[l], dffn, gmix[l] = _layer_bwd(l, dx, sm["norm_w"][l], ffn_w[l], saved[l], pending)
        if l % 2 == 0:
            g_a, g_b = gmix[l]["w_in"], jnp.concatenate([gmix[l]["wo_a"], gmix[l]["wo_b"]], axis=0)
        else:
            g_a, g_b = gmix[l]["w1"], gmix[l]["w2"]
        g_a = halves(g_a).astype(BF16)
        g_b = g_b.reshape(NCHIP, 2, D // 8, D).astype(BF16)
        dx, grads = lax.optimization_barrier((dx, [dffn[0], dffn[1], g_a, g_b]))
        pending = hand_on(l, grads, _swap_halves(f"swap_grads_{l}", grads, sums["gu"] if l < DEPTH - 1 else None))
    dx = pending(dx)
    gm, gc = [gmix[0], gmix[2]], [gmix[1], gmix[3]]
    small_g = dict(
        norm_w=jnp.stack(dnorm), dn_conv_w=jnp.stack([gm[e]["dn_conv_w"] for e in range(2)]),
        conv_b_pw1=jnp.stack([jnp.concatenate([gc[e]["b1a"], gc[e]["b1b"]], axis=1)[0] for e in range(2)]),
        conv_w_dw=jnp.stack([gc[e]["w_dw"] for e in range(2)]),
        conv_b_dw=jnp.stack([gc[e]["b_dw"][0] for e in range(2)]),
        conv_ln_w=jnp.stack([gc[e]["ln_w"][0] for e in range(2)]),
        conv_ln_b=jnp.stack([gc[e]["ln_b"][0] for e in range(2)]),
        conv_b_pw2=jnp.stack([gc[e]["b2"][0] for e in range(2)]))
    small_by_chip = jnp.stack([_pack([jnp.split(small_g[n], NCHIP, axis=-1)[q] for n in small_names], SMALL_ROWS)
                               for q in range(NCHIP)])
    rep_g = _pack([jnp.stack([gm[e]["sinks"][0] for e in range(2)]), jnp.stack([gm[e]["a_log"][0] for e in range(2)]),
                   jnp.stack([gm[e]["dt_bias"][0] for e in range(2)]),
                   jnp.stack([gm[e]["dn_norm_w"][0] for e in range(2)]), dfw[0]], REP_ROWS)
    small_sum, rep_sum = _exchange_small(small_by_chip, rep_g)

    res = {}
    partial_sums = {k: r[...] for k, r in sums.items()}
    for n, key, row0 in (("ffn_w_gate", "gu", 0), ("ffn_w_up", "gu", D), ("ffn_w_down", "down", 0),
                         ("mix_w_in", "w_in", 0), ("mix_w_out", "w_out", 0), ("conv_w_pw1", "pw1", 0),
                         ("conv_w_pw2", "pw2", 0)):
        view = (lambda a: a) if w[n].ndim == 4 else halves
        outs = _adamw_big(f"adamw_{n}", view(w[n]), view(m[n]), view(v[n]), partial_sums[key], row0)
        res[n] = [o.reshape(w[n].shape) for o in outs]
    outs = _adamw_small("adamw_small", *[_pack([d[n] for n in small_names], SMALL_ROWS) for d in (w, m, v)],
                        small_sum)
    for i, n in enumerate(small_names):
        res[n] = [_unpack(o, small_shapes)[i] for o in outs]
    outs = _adamw_small("adamw_replicated", *[_pack([d[n] for n in rep_names], REP_ROWS) for d in (w, m, v)],
                        rep_sum)
    for i, n in enumerate(rep_names):
        res[n] = [_unpack(o, rep_shapes)[i] for o in outs]

    total = lax.psum(loss[0, 0], ("x", "y", "c"))
    return (total, dx[None], *[res[n][0] for n in order], *[res[n][1] for n in order],
            *[res[n][2] for n in order], *[res[n][3] for n in order])
```

```python
import jax
import jax.numpy as jnp
from jax import lax
from jax.experimental import pallas as pl
from jax.experimental.pallas import tpu as pltpu
from jax.experimental.pallas import tpu_sc as plsc

F32, BF16 = jnp.float32, jnp.bfloat16
MESH = pl.DeviceIdType.MESH
ANY = pl.BlockSpec(memory_space=pl.ANY)

T, D, F = 2048, 1024, 2816
DEPTH = 4
EPS = 1e-6
HEADS, HDIM, KV_HEADS, GROUP = 8, 64, 2, 4
WINDOW = BLOCK = 128
CHUNK = 64
NCHUNK = T // CHUNK
DN_CONV, CONV_WIDTH = 4, 31
Q_A, KV_A, QKV_B, V_B = 512, 128, 1536, 512
IN_COLS = 2832
IN_SPLITS = (0, 512, 640, 768, 2304, 2816, 2832)
NCHIP, NDEV = 4, 8
FS = F // NCHIP
LR, B1, B2, AEPS, WD, STEP = 0.001, 0.9, 0.999, 1e-08, 0.01, 10
V7X_VMEM_BYTES = 64 * 1024 * 1024
VMEM_LIMIT = V7X_VMEM_BYTES * 7 // 8
LANES = 128


def _cp(*sem):
    return pltpu.CompilerParams(dimension_semantics=sem, vmem_limit_bytes=VMEM_LIMIT)


def _sds(shape, dtype=F32):
    return jax.ShapeDtypeStruct(tuple(shape), dtype)


def _full(shape):
    nd = len(shape)
    return pl.BlockSpec(tuple(shape), lambda *_: (0,) * nd)


def _split_bf16(a):
    hi = a.astype(BF16)
    return hi, (a - hi.astype(F32)).astype(BF16)


def _dg(a, b, ca, cb, hi=False):
    if a.ndim == 3 and b.ndim == 3:
        dims = (((ca + 1,), (cb + 1,)), ((0,), (0,)))
    else:
        dims = (((ca,), (cb,)), ((), ()))
    dot = lambda p, q: lax.dot_general(p, q, dims, preferred_element_type=F32)
    if hi:
        a_hi, a_lo = _split_bf16(a.astype(F32))
        b_hi, b_lo = _split_bf16(b.astype(F32))
        return dot(a_hi, b_hi) + (dot(a_hi, b_lo) + dot(a_lo, b_hi))
    return dot(a.astype(BF16), b.astype(BF16))


def _make_mm(hi):
    @jax.custom_vjp
    def nn(a, b):
        return _dg(a, b, 1, 0, hi)

    @jax.custom_vjp
    def nt(a, b):
        return _dg(a, b, 1, 1, hi)

    @jax.custom_vjp
    def tn(a, b):
        return _dg(a, b, 0, 0, hi)

    nn.defvjp(lambda a, b: (_dg(a, b, 1, 0, hi), (a, b)),
              lambda r, g: (_dg(g, r[1], 1, 1, hi).astype(r[0].dtype), _dg(r[0], g, 0, 0, hi).astype(r[1].dtype)))
    nt.defvjp(lambda a, b: (_dg(a, b, 1, 1, hi), (a, b)),
              lambda r, g: (_dg(g, r[1], 1, 0, hi).astype(r[0].dtype), _dg(g, r[0], 0, 0, hi).astype(r[1].dtype)))
    tn.defvjp(lambda a, b: (_dg(a, b, 0, 0, hi), (a, b)),
              lambda r, g: (_dg(r[1], g, 1, 1, hi).astype(r[0].dtype), _dg(r[0], g, 1, 0, hi).astype(r[1].dtype)))
    return nn, nt, tn


_nn, _nt, _tn = _make_mm(False)
_nn_hi, _nt_hi, _tn_hi = _make_mm(True)


def _rms(x, w):
    return x * lax.rsqrt(jnp.mean(x * x, axis=-1, keepdims=True) + EPS) * w


def _layernorm(x, w, b):
    xc = x - jnp.mean(x, axis=-1, keepdims=True)
    return xc * lax.rsqrt(jnp.mean(xc * xc, axis=-1, keepdims=True) + EPS) * w + b


def _silu(x):
    return x * jax.nn.sigmoid(x)


def _iota2(shape, dim):
    return lax.broadcasted_iota(jnp.int32, shape, dim)


def _flat_weights(lhs_idx, weights):
    specs, ops, lhs_of, where = [], [], [], []
    for a, (k, w) in enumerate(zip(lhs_idx, weights)):
        for q in range(1 if w.ndim == 2 else w.shape[0]):
            specs.append(_full(w.shape) if w.ndim == 2
                         else pl.BlockSpec((None,) + w.shape[1:], lambda i, q=q: (q, 0, 0)))
            ops.append(w)
            lhs_of.append(k)
            where.append((a, None if w.ndim == 2 else q))
    return specs, ops, lhs_of, where


def _blk_fwd(name, pre, lhs_idx, post, toks, smalls, weights, outs, tm=512):
    wspecs, wops, lhs_of, _ = _flat_weights(lhs_idx, weights)
    nt_, ns, nw = len(toks), len(smalls), len(wops)

    def body(*refs):
        tv = [r[...] for r in refs[:nt_]]
        sv = [r[...] for r in refs[nt_:nt_ + ns]]
        wr = refs[nt_ + ns:nt_ + ns + nw]
        orf = refs[nt_ + ns + nw:]
        lhs = pre(tv, sv)
        ys = [_dg(lhs[i], w[...], 1, 0) for i, w in zip(lhs_of, wr)]
        for o_ref, o in zip(orf, post(ys, tv, sv)):
            o_ref[...] = o.astype(o_ref.dtype)

    in_specs = ([pl.BlockSpec((tm, a.shape[1]), lambda i: (i, 0)) for a in toks]
                + [_full(a.shape) for a in smalls] + wspecs)
    out_specs = [pl.BlockSpec((tm, w_), lambda i: (i, 0)) for w_, _ in outs]
    return pl.pallas_call(
        body, grid=(T // tm,), in_specs=in_specs, out_specs=out_specs,
        out_shape=[_sds((T, w_), dt) for w_, dt in outs], name=name, compiler_params=_cp("parallel"),
    )(*toks, *smalls, *wops)


def _blk_bwd(name, pre, lhs_idx, post, toks, smalls, weights, ct_groups, res=None, tm=256, wchunk=512):
    wspecs, wops, lhs_of, where = _flat_weights(lhs_idx, weights)
    nt_, ns, nw, na = len(toks), len(smalls), len(wops), len(weights)
    cts = [a for g in ct_groups for a in g]
    nc = len(cts)
    widths = [sum(a.shape[1] for a in g) for g in ct_groups]
    has_res = res is not None

    def body(*refs):
        p = 0
        tr = refs[p:p + nt_]; p += nt_
        sr = refs[p:p + ns]; p += ns
        wr = refs[p:p + nw]; p += nw
        cr = refs[p:p + nc]; p += nc
        rr = refs[p:p + has_res]; p += has_res
        dtr = refs[p:p + nt_]; p += nt_
        dsr = refs[p:p + ns]; p += ns
        dwr = refs[p:p + na]; p += na
        scr = refs[p:]
        i = pl.program_id(0)

        @pl.when(i == 0)
        def _():
            for r in list(dsr) + list(dwr):
                r[...] = jnp.zeros_like(r)

        tv = [r[...] for r in tr]
        sv = [r[...] for r in sr]
        ctv, q, si = [], 0, 0
        for g in ct_groups:
            if len(g) == 1:
                ctv.append(cr[q][...].astype(F32))
            else:
                off = 0
                for j, a in enumerate(g):
                    scr[si][:, off:off + a.shape[1]] = cr[q + j][...].astype(F32)
                    off += a.shape[1]
                ctv.append(scr[si][...])
                si += 1
            q += len(g)

        lhs, vjp_pre = jax.vjp(lambda *a: tuple(pre(list(a[:nt_]), list(a[nt_:]))), *tv, *sv)
        lhs_b = [l.astype(BF16) for l in lhs]
        ys = [_dg(lhs_b[k], w[...], 1, 0) for k, w in zip(lhs_of, wr)]
        _, vjp_post = jax.vjp(lambda *a: tuple(post(list(a[:nw]), list(a[nw:nw + nt_]), list(a[nw + nt_:]))),
                              *ys, *tv, *sv)
        gp = vjp_post(tuple(ctv))
        dys, dt_post, ds_post = gp[:nw], gp[nw:nw + nt_], gp[nw + nt_:]
        dlhs = [None] * len(lhs)
        for k, w, dy, (a, q) in zip(lhs_of, wr, dys, where):
            dyb = dy.astype(BF16)
            n = w.shape[1]
            for c0 in range(0, n, wchunk):
                c1 = min(n, c0 + wchunk)
                part = _dg(lhs_b[k], dyb[:, c0:c1], 0, 0)
                if q is None:
                    dwr[a][:, c0:c1] += part
                else:
                    dwr[a][q, :, c0:c1] += part
            d = _dg(dyb, w[...], 1, 1)
            dlhs[k] = d if dlhs[k] is None else dlhs[k] + d
        gq = vjp_pre(tuple(d.astype(l.dtype) for d, l in zip(dlhs, lhs)))
        dt_pre, ds_pre = gq[:nt_], gq[nt_:]
        for j in range(nt_):
            d = dt_post[j] + dt_pre[j]
            if j == 0 and has_res:
                d = d + rr[0][...]
            dtr[j][...] = d
        for j in range(ns):
            dsr[j][...] += ds_post[j] + ds_pre[j]

    tok_spec = lambda a: pl.BlockSpec((tm, a.shape[1]), lambda i: (i, 0))
    in_specs = ([tok_spec(a) for a in toks] + [_full(a.shape) for a in smalls] + wspecs
                + [tok_spec(a) for a in cts] + ([tok_spec(res)] if has_res else []))
    out_specs = [tok_spec(a) for a in toks] + [_full(a.shape) for a in smalls] + [_full(w.shape) for w in weights]
    out_shape = ([_sds(a.shape) for a in toks] + [_sds(a.shape) for a in smalls] + [_sds(w.shape) for w in weights])
    scratch = [pltpu.VMEM((tm, wd), F32) for g, wd in zip(ct_groups, widths) if len(g) > 1]
    outs = pl.pallas_call(
        body, grid=(T // tm,), in_specs=in_specs, out_specs=out_specs, out_shape=out_shape,
        scratch_shapes=scratch, name=name, compiler_params=_cp("arbitrary"),
    )(*toks, *smalls, *wops, *cts, *([res] if has_res else []))
    return outs[:nt_], outs[nt_:nt_ + ns], outs[nt_ + ns:]


def _ffn_fwd(name, x, nw, ffn, idx, tm=512):
    def body(x_ref, nw_ref, wg_ref, wu_ref, wd_ref, o_ref, h_scr):
        s = pl.program_id(1)

        @pl.when(s == 0)
        def _():
            xv = x_ref[...]
            h_scr[...] = _rms(xv, nw_ref[...]).astype(BF16)
            o_ref[...] = xv

        h = h_scr[...]
        a = _dg(h, wg_ref[...], 1, 1)
        b = _dg(h, wu_ref[...], 1, 1)
        o_ref[...] += 0.5 * _dg(_silu(a) * b, wd_ref[...], 1, 0)

    wspec = lambda k: pl.BlockSpec((None, None, FS, D), lambda i, s: (s, idx, k, 0))
    return pl.pallas_call(
        body, grid=(T // tm, NCHIP),
        in_specs=[pl.BlockSpec((tm, D), lambda i, s: (i, 0)), _full((1, D)), wspec(0), wspec(1), wspec(2)],
        out_specs=pl.BlockSpec((tm, D), lambda i, s: (i, 0)), out_shape=_sds((T, D)),
        scratch_shapes=[pltpu.VMEM((tm, D), BF16)], name=name, compiler_params=_cp("parallel", "arbitrary"),
    )(x, nw, ffn, ffn, ffn)


def _ffn_bwd(name, x, nw, ffn, idx, dy, gbuf=None, tm=512):
    ni = T // tm

    def body(x_ref, dy_ref, nw_ref, wg_ref, wu_ref, wd_ref, dx_ref, dnw_ref, dffn_ref, dh_acc, ag, au, ad):
        s, i = pl.program_id(0), pl.program_id(1)
        rows = pl.ds(pl.multiple_of(i * tm, tm), tm)

        @pl.when((s == 0) & (i == 0))
        def _():
            dnw_ref[...] = jnp.zeros_like(dnw_ref)

        @pl.when(i == 0)
        def _():
            ag[...] = jnp.zeros_like(ag)
            au[...] = jnp.zeros_like(au)
            ad[...] = jnp.zeros_like(ad)

        xv, nwv, dyv = x_ref[...], nw_ref[...], dy_ref[...]
        h, vjp_rms = jax.vjp(_rms, xv, nwv)
        hb = h.astype(BF16)
        a = _dg(hb, wg_ref[...], 1, 1)
        b = _dg(hb, wu_ref[...], 1, 1)
        sa = jax.nn.sigmoid(a)
        act = a * sa
        dyb = (0.5 * dyv).astype(BF16)
        ad[...] += _dg(act * b, dyb, 0, 0)
        dact = _dg(dyb, wd_ref[...], 1, 1)
        da = (dact * b * (sa * (1.0 + a * (1.0 - sa)))).astype(BF16)
        db = (dact * act).astype(BF16)
        ag[...] += _dg(da, hb, 0, 0)
        au[...] += _dg(db, hb, 0, 0)
        dh = _dg(da, wg_ref[...], 1, 0) + _dg(db, wu_ref[...], 1, 0)

        @pl.when(s == 0)
        def _():
            dh_acc[rows, :] = dh

        @pl.when(s > 0)
        def _():
            dh_acc[rows, :] += dh

        @pl.when(s == NCHIP - 1)
        def _():
            dx, dnw = vjp_rms(dh_acc[rows, :])
            dx_ref[...] = dyv + dx
            dnw_ref[...] += dnw

        @pl.when(i == ni - 1)
        def _():
            dffn_ref[0:FS, :] = ag[...].astype(BF16)
            dffn_ref[FS:2 * FS, :] = au[...].astype(BF16)
            dffn_ref[2 * FS:, :] = ad[...].astype(BF16)

    wspec = lambda r, k: pl.BlockSpec((None, None, r, D), lambda s, i: (s, idx, k, 0), pipeline_mode=pl.Buffered(1))
    last = lambda s, i: (jnp.where(s == NCHIP - 1, i, 0), 0)
    nb = 0 if gbuf is None else 1
    return pl.pallas_call(
        lambda *refs: body(*refs[:6], *refs[6 + nb:]), grid=(NCHIP, ni),
        in_specs=[pl.BlockSpec((tm, D), lambda s, i: (i, 0)), pl.BlockSpec((tm, D), lambda s, i: (i, 0)),
                  _full((1, D)), wspec(FS, 0), wspec(FS, 1), wspec(FS, 2)] + [ANY] * nb,
        out_specs=[pl.BlockSpec((tm, D), last), _full((1, D)), wspec(3 * FS, 0)],
        out_shape=[_sds((T, D)), _sds((1, D)), _sds(ffn.shape, BF16)],
        input_output_aliases={6 + k: 2 + k for k in range(nb)},
        scratch_shapes=[pltpu.VMEM((T, D), F32)] + [pltpu.VMEM((FS, D), F32)] * 3,
        name=name, compiler_params=_cp("arbitrary", "arbitrary"),
    )(x, dy, nw, ffn, ffn, ffn, *(() if gbuf is None else (gbuf,)))


CONV_ROWS = 256


def _conv_pad(k):
    return 8 * ((k - 1 + 7) // 8)


def _conv_fwd(name, x, w, b, act):
    k_w, c = w.shape
    tc = 256 if c % 256 == 0 else LANES
    pad = _conv_pad(k_w)
    has_b = b is not None

    def body(*refs):
        x_ref, w_ref = refs[0], refs[1]
        b_ref = refs[2] if has_b else None
        y_ref, xp = refs[2 + has_b], refs[3 + has_b]
        xp[0:pad, :] = jnp.zeros((pad, tc), F32)
        xp[pad:, :] = x_ref[...]

        def step(t, carry):
            base = pl.multiple_of(t * CONV_ROWS, CONV_ROWS)
            win = xp[pl.ds(base, CONV_ROWS + pad), :]
            acc = jnp.zeros((CONV_ROWS, tc), F32)
            for k in range(k_w):
                o = pad - (k_w - 1) + k
                acc = acc + w_ref[k:k + 1, :] * win[o:o + CONV_ROWS, :]
            if has_b:
                acc = acc + b_ref[...]
            y_ref[pl.ds(base, CONV_ROWS), :] = _silu(acc) if act else acc
            return carry

        lax.fori_loop(0, T // CONV_ROWS, step, 0)

    col = lambda r: pl.BlockSpec((r, tc), lambda j: (0, j))
    ins = [x, w] + ([b] if has_b else [])
    return pl.pallas_call(
        body, grid=(c // tc,), in_specs=[col(T), col(k_w)] + ([col(1)] if has_b else []), out_specs=col(T),
        out_shape=_sds((T, c)), scratch_shapes=[pltpu.VMEM((T + pad, tc), F32)], name=name,
        compiler_params=_cp("parallel"),
    )(*ins)


def _conv_bwd(name, x, w, b, act, dy):
    k_w, c = w.shape
    tc = 256 if c % 256 == 0 else LANES
    pad = _conv_pad(k_w)
    has_b = b is not None

    def body(*refs):
        x_ref, w_ref, dy_ref = refs[0], refs[1], refs[2]
        b_ref = refs[3] if has_b else None
        dx_ref, dw_ref, db_ref, xp, dp = refs[3 + has_b:]
        xp[0:pad, :] = jnp.zeros((pad, tc), F32)
        xp[pad:, :] = x_ref[...]
        dp[T:, :] = jnp.zeros((pad, tc), F32)
        dw_ref[...] = jnp.zeros_like(dw_ref)
        db_ref[...] = jnp.zeros_like(db_ref)

        def step1(t, carry):
            base = pl.multiple_of(t * CONV_ROWS, CONV_ROWS)
            d = dy_ref[pl.ds(base, CONV_ROWS), :]
            win = xp[pl.ds(base, CONV_ROWS + pad), :]
            offs = [pad - (k_w - 1) + k for k in range(k_w)]
            if act:
                acc = jnp.zeros((CONV_ROWS, tc), F32)
                for k, o in enumerate(offs):
                    acc = acc + w_ref[k:k + 1, :] * win[o:o + CONV_ROWS, :]
                if has_b:
                    acc = acc + b_ref[...]
                sg = jax.nn.sigmoid(acc)
                d = d * (sg * (1.0 + acc * (1.0 - sg)))
            dp[pl.ds(base, CONV_ROWS), :] = d
            for k, o in enumerate(offs):
                dw_ref[k:k + 1, :] += jnp.sum(d * win[o:o + CONV_ROWS, :], axis=0, keepdims=True)
            db_ref[...] += jnp.sum(d, axis=0, keepdims=True)
            return carry

        lax.fori_loop(0, T // CONV_ROWS, step1, 0)

        def step2(t, carry):
            base = pl.multiple_of(t * CONV_ROWS, CONV_ROWS)
            win = dp[pl.ds(base, CONV_ROWS + pad), :]
            acc = jnp.zeros((CONV_ROWS, tc), F32)
            for k in range(k_w):
                o = (k_w - 1) - k
                acc = acc + w_ref[k:k + 1, :] * win[o:o + CONV_ROWS, :]
            dx_ref[pl.ds(base, CONV_ROWS), :] = acc
            return carry

        lax.fori_loop(0, T // CONV_ROWS, step2, 0)

    col = lambda r: pl.BlockSpec((r, tc), lambda j: (0, j))
    ins = [x, w, dy] + ([b] if has_b else [])
    return pl.pallas_call(
        body, grid=(c // tc,), in_specs=[col(T), col(k_w), col(T)] + ([col(1)] if has_b else []),
        out_specs=[col(T), col(k_w), col(1)], out_shape=[_sds((T, c)), _sds((k_w, c)), _sds((1, c))],
        scratch_shapes=[pltpu.VMEM((T + pad, tc), F32), pltpu.VMEM((T + pad, tc), F32)], name=name,
        compiler_params=_cp("parallel"),
    )(*ins)


def _attn_consts(n):
    i = _iota2((BLOCK, 2 * BLOCK), 0)
    j = _iota2((BLOCK, 2 * BLOCK), 1)
    dist = i + BLOCK - j
    valid = (dist >= 0) & (dist < WINDOW) & ((n > 0) | (j >= BLOCK))
    return dist.astype(F32), valid


def _attn_block(q4, kk, vv, sinks, dist, valid, kv):
    outs = []
    lane = _iota2((1, HEADS), 1)
    for g in range(GROUP):
        h = kv * GROUP + g
        slope = 2.0 ** (-8.0 * (h + 1) / HEADS)
        s = _nt(q4[:, g * HDIM:(g + 1) * HDIM], kk) * (HDIM ** -0.5)
        s = jnp.where(valid, s - slope * dist, -1e30)
        sink = jnp.sum(jnp.where(lane == h, sinks, 0.0), axis=1, keepdims=True)
        m = jnp.maximum(jnp.max(s, axis=-1, keepdims=True), sink)
        e = jnp.exp(s - m)
        p = e / (jnp.sum(e, axis=-1, keepdims=True) + jnp.exp(sink - m))
        outs.append(_nn(p, vv))
    return tuple(outs)


def _attn_fwd(name, qa, ka, va, sinks):
    def body(q_ref, k_ref, v_ref, s_ref, o_ref, kp, vp):
        kp[0:BLOCK, :] = jnp.zeros((BLOCK, KV_A), F32)
        vp[0:BLOCK, :] = jnp.zeros((BLOCK, KV_A), F32)
        kp[BLOCK:, :] = k_ref[...]
        vp[BLOCK:, :] = v_ref[...]
        sinks_v = s_ref[...]

        def step(n, carry):
            r = pl.multiple_of(n * BLOCK, BLOCK)
            dist, valid = _attn_consts(n)
            k2 = kp[pl.ds(r, 2 * BLOCK), :]
            v2 = vp[pl.ds(r, 2 * BLOCK), :]
            for kv in range(KV_HEADS):
                q4 = q_ref[pl.ds(r, BLOCK), kv * GROUP * HDIM:(kv + 1) * GROUP * HDIM]
                og = _attn_block(q4, k2[:, kv * HDIM:(kv + 1) * HDIM], v2[:, kv * HDIM:(kv + 1) * HDIM], sinks_v,
                                 dist, valid, kv)
                for g in range(GROUP):
                    h = kv * GROUP + g
                    o_ref[pl.ds(r, BLOCK), h * HDIM:(h + 1) * HDIM] = og[g]
            return carry

        lax.fori_loop(0, T // BLOCK, step, 0)

    return pl.pallas_call(
        body, out_shape=_sds((T, Q_A)),
        scratch_shapes=[pltpu.VMEM((T + BLOCK, KV_A), F32), pltpu.VMEM((T + BLOCK, KV_A), F32)], name=name,
        compiler_params=pltpu.CompilerParams(vmem_limit_bytes=VMEM_LIMIT),
    )(qa, ka, va, sinks)


def _attn_bwd(name, qa, ka, va, sinks, do):
    def body(q_ref, k_ref, v_ref, s_ref, do_ref, dq_ref, dk_ref, dv_ref, ds_ref, kp, vp, dkp, dvp):
        kp[0:BLOCK, :] = jnp.zeros((BLOCK, KV_A), F32)
        vp[0:BLOCK, :] = jnp.zeros((BLOCK, KV_A), F32)
        kp[BLOCK:, :] = k_ref[...]
        vp[BLOCK:, :] = v_ref[...]
        dkp[...] = jnp.zeros_like(dkp)
        dvp[...] = jnp.zeros_like(dvp)
        ds_ref[...] = jnp.zeros_like(ds_ref)
        sinks_v = s_ref[...]

        def step(n, carry):
            r = pl.multiple_of(n * BLOCK, BLOCK)
            dist, valid = _attn_consts(n)
            k2 = kp[pl.ds(r, 2 * BLOCK), :]
            v2 = vp[pl.ds(r, 2 * BLOCK), :]
            for kv in range(KV_HEADS):
                cols = slice(kv * HDIM, (kv + 1) * HDIM)
                q4 = q_ref[pl.ds(r, BLOCK), kv * GROUP * HDIM:(kv + 1) * GROUP * HDIM]
                _, vjp = jax.vjp(lambda q, k, v, s: _attn_block(q, k, v, s, dist, valid, kv),
                                 q4, k2[:, cols], v2[:, cols], sinks_v)
                cts = tuple(do_ref[pl.ds(r, BLOCK), (kv * GROUP + g) * HDIM:(kv * GROUP + g + 1) * HDIM]
                            for g in range(GROUP))
                dq4, dkk, dvv, dsk = vjp(cts)
                dq_ref[pl.ds(r, BLOCK), kv * GROUP * HDIM:(kv + 1) * GROUP * HDIM] = dq4
                dkp[pl.ds(r, 2 * BLOCK), cols] += dkk
                dvp[pl.ds(r, 2 * BLOCK), cols] += dvv
                ds_ref[...] += dsk
            return carry

        lax.fori_loop(0, T // BLOCK, step, 0)
        dk_ref[...] = dkp[BLOCK:, :]
        dv_ref[...] = dvp[BLOCK:, :]

    pad = lambda: pltpu.VMEM((T + BLOCK, KV_A), F32)
    return pl.pallas_call(
        body, out_shape=[_sds((T, Q_A)), _sds((T, KV_A)), _sds((T, KV_A)), _sds((1, HEADS))],
        scratch_shapes=[pad(), pad(), pad(), pad()], name=name,
        compiler_params=pltpu.CompilerParams(vmem_limit_bytes=VMEM_LIMIT),
    )(qa, ka, va, sinks, do)


def _dn_consts():
    i = _iota2((CHUNK, CHUNK), 0)
    j = _iota2((CHUNK, CHUNK), 1)
    return dict(causal=i >= j, strict=i > j, eye=(i == j).astype(F32), ltri=(i >= j).astype(F32),
                ones=jnp.ones((CHUNK, CHUNK), F32), last=(_iota2((CHUNK, 1), 0) == CHUNK - 1).astype(F32))


def _l2norm(x):
    return x * lax.rsqrt(jnp.sum(x * x, axis=-1, keepdims=True) + EPS)


def _head_cols(m):
    lane = _iota2((1, HEADS), 1)
    return jnp.concatenate([jnp.sum(jnp.where(lane == h, m, 0.0), axis=1, keepdims=True)[None]
                            for h in range(HEADS)], axis=0)


def _dn_local(q3, k3, v3, braw, araw, alog, dtb, cs):
    q = _l2norm(q3) * (HDIM ** -0.5)
    k = _l2norm(k3)
    g = -jnp.exp(alog) * jax.nn.softplus(araw + dtb)
    gc_all = _nn_hi(cs["ltri"], g)
    egc_all = jnp.exp(gc_all)
    beta, gc, egc = _head_cols(jax.nn.sigmoid(braw)), _head_cols(gc_all), _head_cols(egc_all)
    a = jnp.broadcast_to(gc, (HEADS, CHUNK, CHUNK))
    diff = a - jnp.swapaxes(a, 1, 2)
    decay = jnp.where(cs["causal"], jnp.exp(jnp.where(cs["causal"], diff, 0.0)), 0.0)
    kb = k * beta
    low = jnp.where(cs["strict"], _nt(kb, k) * decay, 0.0)
    inv = cs["eye"] - low
    pw = low
    for _ in range(5):
        pw = _nn_hi(pw, pw)
        inv = inv + _nn_hi(inv, pw)
    u = _nn_hi(inv, v3 * beta)
    w = _nn_hi(inv, kb * egc)
    attn = _nt(q, k) * decay
    gc_last = jnp.sum(gc * cs["last"], axis=1, keepdims=True)
    return u, w, attn, q * egc, k * jnp.exp(gc_last - gc), egc_all


def _heads3(ref, off=0):
    return jnp.concatenate([ref[:, off + h * HDIM:off + (h + 1) * HDIM][None] for h in range(HEADS)], axis=0)


def _dn_local_fwd(name, qkv, ba, alog, dtb):
    def body(qkv_ref, ba_ref, al_ref, dt_ref, u_ref, w_ref, at_ref, qd_ref, kd_ref, eg_ref):
        bav = ba_ref[...]
        outs = _dn_local(_heads3(qkv_ref), _heads3(qkv_ref, 512), _heads3(qkv_ref, 1024), bav[:, :HEADS],
                         bav[:, HEADS:], al_ref[...], dt_ref[...], _dn_consts())
        for r, o in zip((u_ref, w_ref, at_ref, qd_ref, kd_ref), outs[:5]):
            for h in range(HEADS):
                r[:, h * HDIM:(h + 1) * HDIM] = o[h]
        eg_ref[...] = outs[5]

    row = lambda w_: pl.BlockSpec((CHUNK, w_), lambda n: (n, 0))
    return pl.pallas_call(
        body, grid=(NCHUNK,), in_specs=[row(QKV_B), row(2 * HEADS), _full((1, HEADS)), _full((1, HEADS))],
        out_specs=[row(V_B)] * 5 + [row(HEADS)], out_shape=[_sds((T, V_B))] * 5 + [_sds((T, HEADS))], name=name,
        compiler_params=_cp("parallel"),
    )(qkv, ba, alog, dtb)


def _dn_local_bwd(name, qkv, ba, alog, dtb, cts):
    def body(qkv_ref, ba_ref, al_ref, dt_ref, du_ref, dw_ref, dat_ref, dqd_ref, dkd_ref, deg_ref,
             dqkv_ref, dba_ref, dal_ref, ddt_ref):
        @pl.when(pl.program_id(0) == 0)
        def _():
            dal_ref[...] = jnp.zeros_like(dal_ref)
            ddt_ref[...] = jnp.zeros_like(ddt_ref)

        cs = _dn_consts()
        bav = ba_ref[...]
        _, vjp = jax.vjp(lambda *a: _dn_local(*a, cs), _heads3(qkv_ref), _heads3(qkv_ref, 512),
                         _heads3(qkv_ref, 1024), bav[:, :HEADS], bav[:, HEADS:], al_ref[...], dt_ref[...])
        dq, dk, dv, dbr, dar, dal, ddt = vjp((_heads3(du_ref), _heads3(dw_ref), _heads3(dat_ref), _heads3(dqd_ref),
                                              _heads3(dkd_ref), deg_ref[...]))
        for h in range(HEADS):
            dqkv_ref[:, h * HDIM:(h + 1) * HDIM] = dq[h]
            dqkv_ref[:, 512 + h * HDIM:512 + (h + 1) * HDIM] = dk[h]
            dqkv_ref[:, 1024 + h * HDIM:1024 + (h + 1) * HDIM] = dv[h]
        dba_ref[:, :HEADS] = dbr
        dba_ref[:, HEADS:] = dar
        dal_ref[...] += dal
        ddt_ref[...] += ddt

    row = lambda w_: pl.BlockSpec((CHUNK, w_), lambda n: (n, 0))
    return pl.pallas_call(
        body, grid=(NCHUNK,),
        in_specs=[row(QKV_B), row(2 * HEADS), _full((1, HEADS)), _full((1, HEADS))] + [row(V_B)] * 5 + [row(HEADS)],
        out_specs=[row(QKV_B), row(2 * HEADS), _full((1, HEADS)), _full((1, HEADS))],
        out_shape=[_sds((T, QKV_B)), _sds((T, 2 * HEADS)), _sds((1, HEADS)), _sds((1, HEADS))], name=name,
        compiler_params=_cp("arbitrary"),
    )(qkv, ba, alog, dtb, *cts)


def _dn_step(s, u, w, attn, qd, kd, egc, z, nw):
    last = (_iota2((CHUNK, 1), 0) == CHUNK - 1).astype(F32)
    gl = jnp.sum(_head_cols(egc) * last, axis=1, keepdims=True)
    v_new = u - _nn(w, s)
    o = _nn(qd, s) + _nn(attn, v_new)
    s_new = s * gl + _tn(kd, v_new)
    return s_new, _rms(o, nw) * _silu(z)


def _unheads(ref, v3):
    for h in range(HEADS):
        ref[:, h * HDIM:(h + 1) * HDIM] = v3[h]


def _dn_rec_fwd(name, u, w, attn, qd, kd, egc, z, nw):
    def body(u_ref, w_ref, at_ref, qd_ref, kd_ref, eg_ref, z_ref, nw_ref, o_ref, ss_ref, s_scr):
        @pl.when(pl.program_id(0) == 0)
        def _():
            s_scr[...] = jnp.zeros_like(s_scr)

        s = s_scr[...]
        ss_ref[...] = s
        s_new, on = _dn_step(s, _heads3(u_ref), _heads3(w_ref), _heads3(at_ref), _heads3(qd_ref), _heads3(kd_ref),
                             eg_ref[...], _heads3(z_ref), nw_ref[...])
        s_scr[...] = s_new
        _unheads(o_ref, on)

    row = lambda w_: pl.BlockSpec((CHUNK, w_), lambda n: (n, 0))
    return pl.pallas_call(
        body, grid=(NCHUNK,), in_specs=[row(V_B)] * 5 + [row(HEADS), row(V_B), _full((1, HDIM))],
        out_specs=[row(V_B), pl.BlockSpec((None, HEADS, HDIM, HDIM), lambda n: (n, 0, 0, 0))],
        out_shape=[_sds((T, V_B)), _sds((NCHUNK, HEADS, HDIM, HDIM))],
        scratch_shapes=[pltpu.VMEM((HEADS, HDIM, HDIM), F32)], name=name, compiler_params=_cp("arbitrary"),
    )(u, w, attn, qd, kd, egc, z, nw)


def _dn_rec_bwd(name, u, w, attn, qd, kd, egc, z, nw, ss, do):
    def body(u_ref, w_ref, at_ref, qd_ref, kd_ref, eg_ref, z_ref, nw_ref, ss_ref, do_ref,
             du_ref, dw_ref, dat_ref, dqd_ref, dkd_ref, deg_ref, dz_ref, dnw_ref, ds_scr):
        @pl.when(pl.program_id(0) == 0)
        def _():
            ds_scr[...] = jnp.zeros_like(ds_scr)
            dnw_ref[...] = jnp.zeros_like(dnw_ref)

        _, vjp = jax.vjp(_dn_step, ss_ref[...], _heads3(u_ref), _heads3(w_ref), _heads3(at_ref), _heads3(qd_ref),
                         _heads3(kd_ref), eg_ref[...], _heads3(z_ref), nw_ref[...])
        ds, du, dw, dat, dqd, dkd, deg, dz, dnw = vjp((ds_scr[...], _heads3(do_ref)))
        ds_scr[...] = ds
        for r, v in zip((du_ref, dw_ref, dat_ref, dqd_ref, dkd_ref, dz_ref), (du, dw, dat, dqd, dkd, dz)):
            _unheads(r, v)
        deg_ref[...] = deg
        dnw_ref[...] += dnw

    row = lambda w_: pl.BlockSpec((CHUNK, w_), lambda n: (NCHUNK - 1 - n, 0))
    return pl.pallas_call(
        body, grid=(NCHUNK,),
        in_specs=[row(V_B)] * 5 + [row(HEADS), row(V_B), _full((1, HDIM)),
                                   pl.BlockSpec((None, HEADS, HDIM, HDIM), lambda n: (NCHUNK - 1 - n, 0, 0, 0)),
                                   row(V_B)],
        out_specs=[row(V_B)] * 5 + [row(HEADS), row(V_B), _full((1, HDIM))],
        out_shape=[_sds((T, V_B))] * 5 + [_sds((T, HEADS)), _sds((T, V_B)), _sds((1, HDIM))],
        scratch_shapes=[pltpu.VMEM((HEADS, HDIM, HDIM), F32)], name=name, compiler_params=_cp("arbitrary"),
    )(u, w, attn, qd, kd, egc, z, nw, ss, do)


def _final(name, x, fw, target, tm=512):
    def body(x_ref, fw_ref, t_ref, l_ref, dx_ref, dfw_ref):
        @pl.when(pl.program_id(0) == 0)
        def _():
            l_ref[...] = jnp.zeros_like(l_ref)
            dfw_ref[...] = jnp.zeros_like(dfw_ref)

        tv = t_ref[...]

        def f(xv, fwv):
            err = _rms(xv, fwv) - tv
            per_tok = jnp.mean(err * err, axis=-1, keepdims=True)
            return 0.5 * jnp.sum(per_tok, axis=0, keepdims=True)

        loss, vjp = jax.vjp(f, x_ref[...], fw_ref[...])
        dx, dfw = vjp(jnp.ones((1, 1), F32))
        l_ref[...] += loss
        dx_ref[...] = dx
        dfw_ref[...] += dfw

    tok = pl.BlockSpec((tm, D), lambda i: (i, 0))
    return pl.pallas_call(
        body, grid=(T // tm,), in_specs=[tok, _full((1, D)), tok], out_specs=[_full((1, 1)), tok, _full((1, D))],
        out_shape=[_sds((1, 1)), _sds((T, D)), _sds((1, D))], name=name, compiler_params=_cp("arbitrary"),
    )(x, fw, target)


def _m1_pre(tv, sv):
    return [_rms(tv[0], sv[0])]


def _m1_post(ys, tv, sv):
    return (jnp.concatenate(ys, axis=1),)


def _m1_post_split(ys, tv, sv):
    proj = jnp.concatenate(ys, axis=1)
    return tuple(proj[:, a:b] for a, b in zip(IN_SPLITS[:-1], IN_SPLITS[1:]))


def _m5_pre(tv, sv):
    return [tv[1], tv[2]]


def _m5_post(ys, tv, sv):
    return (tv[0] + ys[0] + ys[1],)


def _c1_pre(tv, sv):
    return [_rms(tv[0], sv[0])]


def _c1_post(ys, tv, sv):
    return ((jnp.concatenate(ys[:2], axis=1) + sv[1]) * jax.nn.sigmoid(jnp.concatenate(ys[2:], axis=1) + sv[2]),)


def _c3_pre(tv, sv):
    return [_silu(_layernorm(tv[0], sv[0], sv[1]))]


def _c3_post(ys, tv, sv):
    return (tv[1] + ys[0] + sv[2],)


def _row(v):
    return v.reshape(1, -1)


def _mixer_fwd(tag, x, p):
    parts = _blk_fwd(f"m1_fwd_{tag}", _m1_pre, [0], _m1_post_split, [x], [p["nw"]], [p["w_in"]],
                     [(b - a, F32) for a, b in zip(IN_SPLITS[:-1], IN_SPLITS[1:])])
    qa, ka, va, qkvb, z, ba = parts
    att = _attn_fwd(f"attn_fwd_{tag}", qa, ka, va, p["sinks"])
    qkvc = _conv_fwd(f"dnconv_fwd_{tag}", qkvb, p["dn_conv_w"], None, True)
    loc = _dn_local_fwd(f"dnloc_fwd_{tag}", qkvc, ba, p["a_log"], p["dt_bias"])
    og, ss = _dn_rec_fwd(f"dnrec_fwd_{tag}", *loc, z, p["dn_norm_w"])
    (out,) = _blk_fwd(f"m5_fwd_{tag}", _m5_pre, [0, 1], _m5_post, [x, att, og], [], [p["wo_a"], p["wo_b"]],
                      [(D, F32)])
    return out, dict(x=x, qa=qa, ka=ka, va=va, qkvb=qkvb, z=z, ba=ba, att=att, qkvc=qkvc, loc=loc, og=og, ss=ss)


def _mixer_bwd(tag, dy, p, s):
    (dxa, datt, dog), _, (dwo_a, dwo_b) = _blk_bwd(f"m5_bwd_{tag}", _m5_pre, [0, 1], _m5_post,
                                                   [s["x"], s["att"], s["og"]], [], [p["wo_a"], p["wo_b"]], [[dy]])
    rec = _dn_rec_bwd(f"dnrec_bwd_{tag}", *s["loc"], s["z"], p["dn_norm_w"], s["ss"], dog)
    dz, dnw_dn = rec[6], rec[7]
    dqkvc, dba, dalog, ddtb = _dn_local_bwd(f"dnloc_bwd_{tag}", s["qkvc"], s["ba"], p["a_log"], p["dt_bias"],
                                            rec[:6])
    dqkvb, dconvw, _ = _conv_bwd(f"dnconv_bwd_{tag}", s["qkvb"], p["dn_conv_w"], None, True, dqkvc)
    dqa, dka, dva, dsinks = _attn_bwd(f"attn_bwd_{tag}", s["qa"], s["ka"], s["va"], p["sinks"], datt)
    (dx,), (dnw,), (dw_in,) = _blk_bwd(f"m1_bwd_{tag}", _m1_pre, [0], _m1_post, [s["x"]], [p["nw"]], [p["w_in"]],
                                       [[dqa, dka, dva, dqkvb, dz, dba]], res=dxa)
    return dx, dict(nw=dnw, w_in=dw_in, wo_a=dwo_a, wo_b=dwo_b, dn_conv_w=dconvw, sinks=dsinks, a_log=dalog,
                    dt_bias=ddtb, dn_norm_w=dnw_dn)


def _conformer_fwd(tag, x, p):
    (glu,) = _blk_fwd(f"c1_fwd_{tag}", _c1_pre, [0], _c1_post, [x], [p["nw"], p["b1a"], p["b1b"]], [p["w1"]],
                      [(D, F32)])
    cc = _conv_fwd(f"dwconv_fwd_{tag}", glu, p["w_dw"], p["b_dw"], False)
    (out,) = _blk_fwd(f"c3_fwd_{tag}", _c3_pre, [0], _c3_post, [cc, x], [p["ln_w"], p["ln_b"], p["b2"]], [p["w2"]],
                      [(D, F32)])
    return out, dict(x=x, glu=glu, cc=cc)


def _conformer_bwd(tag, dy, p, s):
    (dcc, dxa), (dlnw, dlnb, db2), (dw2,) = _blk_bwd(f"c3_bwd_{tag}", _c3_pre, [0], _c3_post, [s["cc"], s["x"]],
                                                     [p["ln_w"], p["ln_b"], p["b2"]], [p["w2"]], [[dy]])
    dglu, dwdw, dbdw = _conv_bwd(f"dwconv_bwd_{tag}", s["glu"], p["w_dw"], p["b_dw"], False, dcc)
    (dx,), (dnw, db1a, db1b), (dw1,) = _blk_bwd(f"c1_bwd_{tag}", _c1_pre, [0], _c1_post, [s["x"]],
                                                [p["nw"], p["b1a"], p["b1b"]], [p["w1"]], [[dglu]], res=dxa)
    return dx, dict(nw=dnw, b1a=db1a, b1b=db1b, w1=dw1, w_dw=dwdw, b_dw=dbdw, ln_w=dlnw, ln_b=dlnb, b2=db2, w2=dw2)


def _layer_fwd(l, x, nw, ffn, p):
    x1 = _ffn_fwd(f"ffn_fwd_{l}a", x, _row(nw[0]), ffn, 0)
    p = dict(p, nw=_row(nw[1]))
    x2, sv = (_mixer_fwd if l % 2 == 0 else _conformer_fwd)(str(l), x1, p)
    return _ffn_fwd(f"ffn_fwd_{l}b", x2, _row(nw[2]), ffn, 1), (x, x2, p, sv)


def _layer_bwd(l, dx, nw, ffn, saved, after_first=lambda dx: dx):
    x0, x2, p, sv = saved
    dx, dn2, dffn = _ffn_bwd(f"ffn_bwd_{l}b", x2, _row(nw[2]), ffn, 1, dx)
    dx = after_first(dx)
    dx, dmix = (_mixer_bwd if l % 2 == 0 else _conformer_bwd)(str(l), dx, p, sv)
    dx, dn0, dffn = _ffn_bwd(f"ffn_bwd_{l}a", x0, _row(nw[0]), ffn, 0, dx, dffn)
    return dx, jnp.concatenate([dn0, dmix.pop("nw"), dn2], axis=0), dffn, dmix


def _place():
    x, y, c = lax.axis_index("x"), lax.axis_index("y"), lax.axis_index("c")
    chips = [(1 - x, y), (x, 1 - y), (1 - x, 1 - y)]
    return x, y, c, 2 * x + y, chips, [2 * px + py for px, py in chips]


def _handshake(peers):
    barrier = pltpu.get_barrier_semaphore()
    for p in peers:
        pl.semaphore_signal(barrier, inc=1, device_id=p, device_id_type=MESH)
    pl.semaphore_wait(barrier, len(peers))


def _chip_peers():
    x, y, c, _, chips, _ = _place()
    return [(*chip, c) for chip in chips] + [(x, y, 1 - c)]


def _gather_copies(ins, outs, nb, send, recv, fsend, frecv, lsem):
    n_in = len(ins)
    x, y, c, me, chips, cidx = _place()
    sib = (x, y, 1 - c)
    local = [pltpu.make_async_copy(ins[a], outs[a].at[me], lsem.at[a]) for a in range(n_in)]
    for cp in local:
        cp.start()

    def region(a, k, who):
        if k < 2:
            return outs[a].at[cidx[k], pl.ds(who, 1)]
        r = ins[a].shape[1] // 2
        return outs[a].at[cidx[2], pl.ds(who, 1), pl.ds((k - 2) * r, r)]

    def hop(a, k):
        if k < 2:
            src, dst = ins[a].at[pl.ds(c, 1)], outs[a].at[me, pl.ds(c, 1)]
        else:
            r = ins[a].shape[1] // 2
            src = dst = outs[a].at[cidx[3 - k], pl.ds(c, 1), pl.ds((k - 2) * r, r)]
        return pltpu.make_async_remote_copy(src, dst, send.at[4 * a + k], recv.at[4 * a + k],
                                            device_id=(*chips[k % 2], c), device_id_type=MESH)

    def landed(a, k):
        dst = region(a, k, c)
        return pltpu.make_async_remote_copy(dst, dst, send.at[4 * a + k], recv.at[4 * a + k],
                                            device_id=(*chips[k % 2], c), device_id_type=MESH)

    def passed(a, k, who):
        part = region(a, k, who)
        return pltpu.make_async_remote_copy(part, part, fsend.at[4 * a + k], frecv.at[4 * a + k], device_id=sib,
                                            device_id_type=MESH)

    def direct(a, j):
        k = 4 * nb + 3 * (a - nb) + j
        return pltpu.make_async_remote_copy(ins[a], outs[a].at[me], send.at[k], recv.at[k],
                                            device_id=(*chips[j], c), device_id_type=MESH)

    def direct_landed(a, j):
        k = 4 * nb + 3 * (a - nb) + j
        dst = outs[a].at[cidx[j]]
        return pltpu.make_async_remote_copy(dst, dst, send.at[k], recv.at[k], device_id=(*chips[j], c),
                                            device_id_type=MESH)

    sends = [hop(a, k) for a in range(nb) for k in range(2)] + [direct(a, j) for a in range(nb, n_in) for j in range(3)]
    for cp in sends:
        cp.start()
    for a in range(nb):
        for k in (1, 0):
            landed(a, k).wait_recv()
            for cp in (hop(a, 3 - k), passed(a, k, c)):
                cp.start()
                sends.append(cp)
    for a in range(nb):
        for k in (2, 3):
            landed(a, k).wait_recv()
            cp = passed(a, k, c)
            cp.start()
            sends.append(cp)
    for a in range(nb, n_in):
        for j in range(3):
            direct_landed(a, j).wait_recv()
    for a in range(nb):
        for k in range(4):
            passed(a, k, 1 - c).wait_recv()
    for cp in sends:
        cp.wait_send()
    for cp in local:
        cp.wait()


def _gather_sems(n_in, nb):
    dma = pltpu.SemaphoreType.DMA
    n_ici = 4 * nb + 3 * (n_in - nb)
    return [dma((n_ici,)), dma((n_ici,)), dma((4 * nb,)), dma((4 * nb,)), dma((n_in,))]


def _gather_async(name, halved, whole=()):
    nb, arrs = len(halved), list(halved) + list(whole)
    hbm = pltpu.MemorySpace.HBM
    ins = [jax.new_ref(a, memory_space=hbm) for a in arrs]
    outs = [jax.empty_ref(_sds((NCHIP,) + a.shape, a.dtype), memory_space=hbm) for a in arrs]

    @pl.kernel(mesh=plsc.ScalarSubcoreMesh(axis_name="seq", num_cores=1), name=name,
               scratch_types=tuple(_gather_sems(len(arrs), nb)),
               compiler_params=pltpu.CompilerParams(collective_id=2))
    def launch(send, recv, fsend, frecv, lsem):
        _handshake(_chip_peers())
        _gather_copies(ins, outs, nb, send, recv, fsend, frecv, lsem)

    launch()
    return outs


def _swap_halves(name, grads, after=None):
    n = len(grads)
    hbm = pltpu.MemorySpace.HBM
    ins = [jax.new_ref(g, memory_space=hbm) for g in grads]
    outs = [jax.empty_ref(_sds((NCHIP, g.shape[1] // 2) + g.shape[2:], g.dtype), memory_space=hbm) for g in grads]
    tile = (2 * 8, LANES)
    token = None if after is None else jax.empty_ref(_sds(tile, BF16), memory_space=hbm)

    @pl.kernel(mesh=plsc.ScalarSubcoreMesh(axis_name="seq", num_cores=1), name=name,
               scratch_types=(pltpu.SemaphoreType.DMA((n + 1,)), pltpu.SemaphoreType.DMA((n,))),
               compiler_params=pltpu.CompilerParams(collective_id=1))
    def launch(send, recv):
        x, y, c, _, _, _ = _place()
        sib = (x, y, 1 - c)
        _handshake([sib])
        if after is not None:
            tick = pltpu.make_async_copy(after.at[0, 0, 0, pl.ds(0, tile[0]), pl.ds(0, tile[1])], token, send.at[n])
            tick.start()
            tick.wait()
        cps = []
        for a in range(n):
            h = grads[a].shape[1] // 2
            cps.append(pltpu.make_async_remote_copy(ins[a].at[:, pl.ds((1 - c) * h, h)], outs[a], send.at[a],
                                                    recv.at[a], device_id=sib, device_id_type=MESH))
        for cp in cps:
            cp.start()
        for cp in cps:
            cp.wait()

    launch()
    return outs


def _row_tile(r, cap=256):
    return max(t for t in range(8, cap + 1, 8) if r % t == 0)


def _add_half(name, g, r, c_arr):
    _, l, rows, cols = g.shape
    h = l // 2
    tr = _row_tile(rows)

    def body(c_ref, g_ref, r_ref, o_ref):
        o_ref[...] = (g_ref[...].astype(F32) + r_ref[...].astype(F32)).astype(BF16)

    blk = (None, None, tr, cols)
    return pl.pallas_call(
        body,
        grid_spec=pltpu.PrefetchScalarGridSpec(
            num_scalar_prefetch=1, grid=(NCHIP, h, rows // tr),
            in_specs=[pl.BlockSpec(blk, lambda j, i, t, c_ref: (j, c_ref[0] * h + i, t, 0)),
                      pl.BlockSpec(blk, lambda j, i, t, c_ref: (j, i, t, 0))],
            out_specs=pl.BlockSpec(blk, lambda j, i, t, c_ref: (j, i, t, 0))),
        out_shape=_sds((NCHIP, h, rows, cols), BF16), name=name,
        compiler_params=_cp("parallel", "parallel", "parallel"),
    )(c_arr, g, r)


def _scatter_async(name, parts, sums, where):
    nb = len(parts)
    ins = [jax.new_ref(p, memory_space=pltpu.MemorySpace.HBM) for p in parts]
    dma = pltpu.SemaphoreType.DMA

    @pl.kernel(mesh=plsc.ScalarSubcoreMesh(axis_name="seq", num_cores=1), name=name,
               scratch_types=(dma((3 * nb,)), dma((3 * nb,)), dma((4 * nb,)), dma((4 * nb,)), dma((nb,))),
               compiler_params=pltpu.CompilerParams(collective_id=3))
    def launch(send, recv, fsend, frecv, lsem):
        _handshake(_chip_peers())
        x, y, c, me, chips, cidx = _place()
        sib = (x, y, 1 - c)

        def slot(a, half, chip):
            return sums[a].at[half, chip, pl.ds(where[a], 1)]

        local = [pltpu.make_async_copy(ins[a].at[me], slot(a, c, me), lsem.at[a]) for a in range(nb)]
        for cp in local:
            cp.start()

        def ici(a, j):
            return pltpu.make_async_remote_copy(ins[a].at[cidx[j]], slot(a, c, me), send.at[a * 3 + j],
                                                recv.at[a * 3 + j], device_id=(*chips[j], c), device_id_type=MESH)

        def landed(a, j):
            dst = slot(a, c, cidx[j])
            return pltpu.make_async_remote_copy(dst, dst, send.at[a * 3 + j], recv.at[a * 3 + j],
                                                device_id=(*chips[j], c), device_id_type=MESH)

        def passed(a, j, who):
            dst = slot(a, who, me if j == 3 else cidx[j])
            src = ins[a].at[me] if j == 3 else dst
            return pltpu.make_async_remote_copy(src, dst, fsend.at[a * 4 + j], frecv.at[a * 4 + j], device_id=sib,
                                                device_id_type=MESH)

        sends = [ici(a, j) for a in range(nb) for j in range(3)] + [passed(a, 3, c) for a in range(nb)]
        for cp in sends:
            cp.start()
        for a in range(nb):
            for j in range(3):
                landed(a, j).wait_recv()
                cp = passed(a, j, c)
                cp.start()
                sends.append(cp)
        for a in range(nb):
            for j in range(4):
                passed(a, j, 1 - c).wait_recv()
        for cp in sends:
            cp.wait_send()
        for cp in local:
            cp.wait()

    launch()


def _exchange_small(small, rep):
    def body(small_in, rep_in, small_out, rep_out, lsem, ssend, srecv):
        x, y, c, me, _, _ = _place()
        dev = 4 * x + 2 * y + c
        local = [pltpu.make_async_copy(small_in.at[me], small_out.at[dev], lsem.at[0]),
                 pltpu.make_async_copy(rep_in, rep_out.at[dev], lsem.at[1])]
        for cp in local:
            cp.start()

        def peer(r):
            return (1 - x if r & 4 else x), (1 - y if r & 2 else y), (1 - c if r & 1 else c)

        def tiny(r, which):
            px, py, pc = peer(r)
            k = (r - 1) * 2 + which
            if which == 0:
                return pltpu.make_async_remote_copy(small_in.at[2 * px + py], small_out.at[dev], ssend.at[k],
                                                    srecv.at[k], device_id=(px, py, pc), device_id_type=MESH)
            return pltpu.make_async_remote_copy(rep_in, rep_out.at[dev], ssend.at[k], srecv.at[k],
                                                device_id=(px, py, pc), device_id_type=MESH)

        def tiny_landed(r, which):
            px, py, pc = peer(r)
            k = (r - 1) * 2 + which
            dst = (small_out if which == 0 else rep_out).at[4 * px + 2 * py + pc]
            return pltpu.make_async_remote_copy(dst, dst, ssend.at[k], srecv.at[k], device_id=(px, py, pc),
                                                device_id_type=MESH)

        sends = [tiny(r, w) for r in range(1, NDEV) for w in range(2)]
        for cp in sends:
            cp.start()
        for r in range(1, NDEV):
            for w in range(2):
                tiny_landed(r, w).wait_recv()
        for cp in sends:
            cp.wait_send()
        for cp in local:
            cp.wait()

    dma = pltpu.SemaphoreType.DMA
    return pl.pallas_call(
        body, in_specs=[ANY] * 2, out_specs=[ANY] * 2,
        out_shape=[_sds((NDEV,) + small.shape[1:], F32), _sds((NDEV,) + rep.shape, F32)],
        scratch_shapes=[dma((2,)), dma((2 * (NDEV - 1),)), dma((2 * (NDEV - 1),))], name="exchange_small_grads",
    )(small, rep)


def _adamw_math(w, g, m, v):
    m = B1 * m + (1.0 - B1) * g
    v = B2 * v + (1.0 - B2) * (g * g)
    m_hat = m / (1.0 - B1 ** STEP)
    v_hat = v / (1.0 - B2 ** STEP)
    return -LR * (m_hat / (jnp.sqrt(v_hat) + AEPS) + WD * w), m, v


def _adamw_big(name, w, m, v, parts, row0=0):
    n, _, rows, cols = w.shape
    tr = _row_tile(rows)
    t0 = row0 // tr

    def body(w_ref, m_ref, v_ref, p_ref, g_ref, d_ref, nm_ref, nv_ref):
        g = p_ref[0].astype(F32)
        for q in range(1, NCHIP):
            g = g + p_ref[q].astype(F32)
        d, nm, nv = _adamw_math(w_ref[...], g, m_ref[...], v_ref[...])
        g_ref[...], d_ref[...], nm_ref[...], nv_ref[...] = g, d, nm, nv

    spec = pl.BlockSpec((None, None, tr, cols), lambda i, p, t: (i, p, t, 0))
    return pl.pallas_call(
        body, grid=(n, 2, rows // tr),
        in_specs=[spec, spec, spec,
                  pl.BlockSpec((None, NCHIP, None, tr, cols), lambda i, p, t: (p, 0, i, t0 + t, 0))],
        out_specs=[spec] * 4, out_shape=[_sds(w.shape)] * 4, name=name,
        compiler_params=_cp("parallel", "parallel", "parallel"),
    )(w, m, v, parts)


def _adamw_small(name, w, m, v, parts):
    def body(w_ref, m_ref, v_ref, p_ref, g_ref, d_ref, nm_ref, nv_ref):
        g = p_ref[0]
        for q in range(1, NDEV):
            g = g + p_ref[q]
        d, nm, nv = _adamw_math(w_ref[...], g, m_ref[...], v_ref[...])
        g_ref[...], d_ref[...], nm_ref[...], nv_ref[...] = g, d, nm, nv

    return pl.pallas_call(body, out_shape=[_sds(w.shape)] * 4, name=name)(w, m, v, parts)


def _pack(arrs, rows):
    flat = jnp.concatenate([a.reshape(-1) for a in arrs])
    return jnp.pad(flat, (0, rows * LANES - flat.shape[0])).reshape(rows, LANES)


def _unpack(packed, shapes):
    flat, out, o = packed.reshape(-1), [], 0
    for s in shapes:
        n = 1
        for d in s:
            n *= d
        out.append(flat[o:o + n].reshape(s))
        o += n
    return out


SMALL_ROWS, REP_ROWS = 200, 16


def kernel(x, norm_w, ffn_w_gate, ffn_w_up, ffn_w_down, mix_w_in, dn_conv_w, attn_sinks, dn_a_log, dn_dt_bias, dn_norm_w, mix_w_out, conv_w_pw1, conv_b_pw1, conv_w_dw, conv_b_dw, conv_ln_w, conv_ln_b, conv_w_pw2, conv_b_pw2, final_norm_w, loss_target, m_norm_w, m_ffn_w_gate, m_ffn_w_up, m_ffn_w_down, m_mix_w_in, m_dn_conv_w, m_attn_sinks, m_dn_a_log, m_dn_dt_bias, m_dn_norm_w, m_mix_w_out, m_conv_w_pw1, m_conv_b_pw1, m_conv_w_dw, m_conv_b_dw, m_conv_ln_w, m_conv_ln_b, m_conv_w_pw2, m_conv_b_pw2, m_final_norm_w, v_norm_w, v_ffn_w_gate, v_ffn_w_up, v_ffn_w_down, v_mix_w_in, v_dn_conv_w, v_attn_sinks, v_dn_a_log, v_dn_dt_bias, v_dn_norm_w, v_mix_w_out, v_conv_w_pw1, v_conv_b_pw1, v_conv_w_dw, v_conv_b_dw, v_conv_ln_w, v_conv_ln_b, v_conv_w_pw2, v_conv_b_pw2, v_final_norm_w):
    small_names = ["norm_w", "dn_conv_w", "conv_b_pw1", "conv_w_dw", "conv_b_dw", "conv_ln_w", "conv_ln_b",
                   "conv_b_pw2"]
    rep_names = ["attn_sinks", "dn_a_log", "dn_dt_bias", "dn_norm_w", "final_norm_w"]
    w = dict(norm_w=norm_w, ffn_w_gate=ffn_w_gate, ffn_w_up=ffn_w_up, ffn_w_down=ffn_w_down, mix_w_in=mix_w_in, dn_conv_w=dn_conv_w, attn_sinks=attn_sinks, dn_a_log=dn_a_log, dn_dt_bias=dn_dt_bias, dn_norm_w=dn_norm_w, mix_w_out=mix_w_out, conv_w_pw1=conv_w_pw1, conv_b_pw1=conv_b_pw1, conv_w_dw=conv_w_dw, conv_b_dw=conv_b_dw, conv_ln_w=conv_ln_w, conv_ln_b=conv_ln_b, conv_w_pw2=conv_w_pw2, conv_b_pw2=conv_b_pw2, final_norm_w=final_norm_w)
    m = dict(norm_w=m_norm_w, ffn_w_gate=m_ffn_w_gate, ffn_w_up=m_ffn_w_up, ffn_w_down=m_ffn_w_down, mix_w_in=m_mix_w_in, dn_conv_w=m_dn_conv_w, attn_sinks=m_attn_sinks, dn_a_log=m_dn_a_log, dn_dt_bias=m_dn_dt_bias, dn_norm_w=m_dn_norm_w, mix_w_out=m_mix_w_out, conv_w_pw1=m_conv_w_pw1, conv_b_pw1=m_conv_b_pw1, conv_w_dw=m_conv_w_dw, conv_b_dw=m_conv_b_dw, conv_ln_w=m_conv_ln_w, conv_ln_b=m_conv_ln_b, conv_w_pw2=m_conv_w_pw2, conv_b_pw2=m_conv_b_pw2, final_norm_w=m_final_norm_w)
    v = dict(norm_w=v_norm_w, ffn_w_gate=v_ffn_w_gate, ffn_w_up=v_ffn_w_up, ffn_w_down=v_ffn_w_down, mix_w_in=v_mix_w_in, dn_conv_w=v_dn_conv_w, attn_sinks=v_attn_sinks, dn_a_log=v_dn_a_log, dn_dt_bias=v_dn_dt_bias, dn_norm_w=v_dn_norm_w, mix_w_out=v_mix_w_out, conv_w_pw1=v_conv_w_pw1, conv_b_pw1=v_conv_b_pw1, conv_w_dw=v_conv_w_dw, conv_b_dw=v_conv_b_dw, conv_ln_w=v_conv_ln_w, conv_ln_b=v_conv_ln_b, conv_w_pw2=v_conv_w_pw2, conv_b_pw2=v_conv_b_pw2, final_norm_w=v_final_norm_w)
    order = ["norm_w", "ffn_w_gate", "ffn_w_up", "ffn_w_down", "mix_w_in", "dn_conv_w", "attn_sinks", "dn_a_log",
             "dn_dt_bias", "dn_norm_w", "mix_w_out", "conv_w_pw1", "conv_b_pw1", "conv_w_dw", "conv_b_dw",
             "conv_ln_w", "conv_ln_b", "conv_w_pw2", "conv_b_pw2", "final_norm_w"]

    small_shapes = [w[n].shape for n in small_names]
    rep_shapes = [w[n].shape for n in rep_names]

    def halves(a):
        return a.reshape(a.shape[:-2] + (2, a.shape[-2] // 2, a.shape[-1]))

    tr = lambda a: jnp.swapaxes(a, -1, -2)
    gate_t, up_t = tr(ffn_w_gate), tr(ffn_w_up)

    def layer_shards(l):
        mix_in, mix_out = (mix_w_in, mix_w_out) if l % 2 == 0 else (conv_w_pw1, conv_w_pw2)
        return [t.astype(BF16) for t in (jnp.concatenate([gate_t[l], up_t[l], ffn_w_down[l]], axis=1),
                                         halves(mix_in[l // 2]), halves(mix_out[l // 2]))]

    first = layer_shards(0) + [_pack([w[n] for n in small_names], SMALL_ROWS)]
    first, (gate_t, up_t, ffn_w_down, mix_w_in, mix_w_out, conv_w_pw1, conv_w_pw2) = lax.optimization_barrier(
        (first, (gate_t, up_t, ffn_w_down, mix_w_in, mix_w_out, conv_w_pw1, conv_w_pw2)))
    gathering = [_gather_async("gather_layer0", first[:3], first[3:])]
    gathering += [_gather_async(f"gather_layer{l}", layer_shards(l)) for l in range(1, DEPTH)]

    def mixer_params(l, w_a, w_b):
        e = l // 2
        w_a = w_a.reshape(NCHIP, D, -1)
        w_b = w_b.reshape(D, D)
        if l % 2 == 0:
            return dict(w_in=w_a, dn_conv_w=sm["dn_conv_w"][e], sinks=_row(attn_sinks[e]), a_log=_row(dn_a_log[e]),
                        dt_bias=_row(dn_dt_bias[e]), dn_norm_w=_row(dn_norm_w[e]), wo_a=w_b[:Q_A], wo_b=w_b[Q_A:])
        return dict(b1a=_row(sm["conv_b_pw1"][e, :D]), b1b=_row(sm["conv_b_pw1"][e, D:]), w1=w_a,
                    w_dw=sm["conv_w_dw"][e], b_dw=_row(sm["conv_b_dw"][e]), ln_w=_row(sm["conv_ln_w"][e]),
                    ln_b=_row(sm["conv_ln_b"][e]), b2=_row(sm["conv_b_pw2"][e]), w2=w_b)

    xs, saved, ffn_w = x[0], [], []
    for l in range(DEPTH):
        got = [r[...] for r in gathering[l]]
        if l == 0:
            per_chip = [_unpack(got[3][q], small_shapes) for q in range(NCHIP)]
            sm = {n: jnp.concatenate([per_chip[q][i] for q in range(NCHIP)], axis=-1)
                  for i, n in enumerate(small_names)}
        else:
            xs, got = lax.optimization_barrier((xs, got))
        ffn_w.append(got[0])
        xs, sv = _layer_fwd(l, xs, sm["norm_w"][l], got[0], mixer_params(l, got[1], got[2]))
        saved.append(sv)
    loss, dx, dfw = _final("final", xs, _row(final_norm_w), loss_target[0])

    hbm = pltpu.MemorySpace.HBM
    sum_shapes = dict(ffn=(DEPTH, 3 * FS, D), w_in=(2, D // 2, IN_COLS // NCHIP),
                      w_out=(2, D // 8, D), pw1=(2, D // 2, D // 2), pw2=(2, D // 8, D))
    sums = {k: jax.empty_ref(_sds((2, NCHIP) + s, BF16), memory_space=hbm) for k, s in sum_shapes.items()}
    c_arr = lax.axis_index("c").astype(jnp.int32).reshape(1)
    dnorm, gmix = [None] * DEPTH, [None] * DEPTH

    def hand_on(l, grads, swapped):
        def run(dx):
            dx, other = lax.optimization_barrier((dx, [r[...] for r in swapped]))
            parts = [_add_half(f"add_half_{l}_{k}", gg, rr, c_arr) for k, (gg, rr) in enumerate(zip(grads, other))]
            dx, parts = lax.optimization_barrier((dx, parts))
            keys = ("ffn", "w_in", "w_out") if l % 2 == 0 else ("ffn", "pw1", "pw2")
            _scatter_async(f"scatter_grads_{l}", parts, [sums[k] for k in keys], [l, l // 2, l // 2])
            return dx
        return run

    pending = lambda dx: dx
    for l in reversed(range(DEPTH)):
        dx, dnorm[l], dffn, gmix[l] = _layer_bwd(l, dx, sm["norm_w"][l], ffn_w[l], saved[l], pending)
        if l % 2 == 0:
            g_a, g_b = gmix[l]["w_in"], jnp.concatenate([gmix[l]["wo_a"], gmix[l]["wo_b"]], axis=0)
        else:
            g_a, g_b = gmix[l]["w1"], gmix[l]["w2"]
        g_a = halves(g_a).astype(BF16)
        g_b = g_b.reshape(NCHIP, 2, D // 8, D).astype(BF16)
        dx, grads = lax.optimization_barrier((dx, [dffn, g_a, g_b]))
        pending = hand_on(l, grads, _swap_halves(f"swap_grads_{l}", grads, sums["ffn"] if l < DEPTH - 1 else None))
    dx = pending(dx)
    gm, gc = [gmix[0], gmix[2]], [gmix[1], gmix[3]]
    small_g = dict(
        norm_w=jnp.stack(dnorm), dn_conv_w=jnp.stack([gm[e]["dn_conv_w"] for e in range(2)]),
        conv_b_pw1=jnp.stack([jnp.concatenate([gc[e]["b1a"], gc[e]["b1b"]], axis=1)[0] for e in range(2)]),
        conv_w_dw=jnp.stack([gc[e]["w_dw"] for e in range(2)]),
        conv_b_dw=jnp.stack([gc[e]["b_dw"][0] for e in range(2)]),
        conv_ln_w=jnp.stack([gc[e]["ln_w"][0] for e in range(2)]),
        conv_ln_b=jnp.stack([gc[e]["ln_b"][0] for e in range(2)]),
        conv_b_pw2=jnp.stack([gc[e]["b2"][0] for e in range(2)]))
    small_by_chip = jnp.stack([_pack([jnp.split(small_g[n], NCHIP, axis=-1)[q] for n in small_names], SMALL_ROWS)
                               for q in range(NCHIP)])
    rep_g = _pack([jnp.stack([gm[e]["sinks"][0] for e in range(2)]), jnp.stack([gm[e]["a_log"][0] for e in range(2)]),
                   jnp.stack([gm[e]["dt_bias"][0] for e in range(2)]),
                   jnp.stack([gm[e]["dn_norm_w"][0] for e in range(2)]), dfw[0]], REP_ROWS)
    small_sum, rep_sum = _exchange_small(small_by_chip, rep_g)

    res = {}
    partial_sums = {k: r[...] for k, r in sums.items()}
    for n, key, row0 in (("ffn_w_gate", "ffn", 0), ("ffn_w_up", "ffn", FS), ("ffn_w_down", "ffn", 2 * FS),
                         ("mix_w_in", "w_in", 0), ("mix_w_out", "w_out", 0), ("conv_w_pw1", "pw1", 0),
                         ("conv_w_pw2", "pw2", 0)):
        view, back = (tr, tr) if n in ("ffn_w_gate", "ffn_w_up") else (
            (lambda a: a) if w[n].ndim == 4 else halves, lambda o, n=n: o.reshape(w[n].shape))
        outs = _adamw_big(f"adamw_{n}", view(w[n]), view(m[n]), view(v[n]), partial_sums[key], row0)
        res[n] = [back(o) for o in outs]
    outs = _adamw_small("adamw_small", *[_pack([d[n] for n in small_names], SMALL_ROWS) for d in (w, m, v)],
                        small_sum)
    for i, n in enumerate(small_names):
        res[n] = [_unpack(o, small_shapes)[i] for o in outs]
    outs = _adamw_small("adamw_replicated", *[_pack([d[n] for n in rep_names], REP_ROWS) for d in (w, m, v)],
                        rep_sum)
    for i, n in enumerate(rep_names):
        res[n] = [_unpack(o, rep_shapes)[i] for o in outs]

    total = lax.psum(loss[0, 0], ("x", "y", "c"))
    return (total, dx[None], *[res[n][0] for n in order], *[res[n][1] for n in order],
            *[res[n][2] for n in order], *[res[n][3] for n in order])
```

```python
import jax
import jax.numpy as jnp
from jax import lax
from jax.experimental import pallas as pl
from jax.experimental.pallas import tpu as pltpu
from jax.experimental.pallas import tpu_sc as plsc

F32, BF16 = jnp.float32, jnp.bfloat16
MESH = pl.DeviceIdType.MESH
ANY = pl.BlockSpec(memory_space=pl.ANY)

T, D, F = 2048, 1024, 2816
DEPTH = 4
EPS = 1e-6
HEADS, HDIM, KV_HEADS, GROUP = 8, 64, 2, 4
WINDOW = BLOCK = 128
CHUNK = 64
NCHUNK = T // CHUNK
DN_CONV, CONV_WIDTH = 4, 31
Q_A, KV_A, QKV_B, V_B = 512, 128, 1536, 512
IN_COLS = 2832
IN_SPLITS = (0, 512, 640, 768, 2304, 2816, 2832)
NCHIP, NDEV = 4, 8
FS = F // NCHIP
LR, B1, B2, AEPS, WD, STEP = 0.001, 0.9, 0.999, 1e-08, 0.01, 10
V7X_VMEM_BYTES = 64 * 1024 * 1024
VMEM_LIMIT = V7X_VMEM_BYTES * 7 // 8
LANES = 128


def _cp(*sem):
    return pltpu.CompilerParams(dimension_semantics=sem, vmem_limit_bytes=VMEM_LIMIT)


def _sds(shape, dtype=F32):
    return jax.ShapeDtypeStruct(tuple(shape), dtype)


def _full(shape):
    nd = len(shape)
    return pl.BlockSpec(tuple(shape), lambda *_: (0,) * nd)


def _split_bf16(a):
    hi = a.astype(BF16)
    return hi, (a - hi.astype(F32)).astype(BF16)


def _dg(a, b, ca, cb, hi=False):
    if a.ndim == 3 and b.ndim == 3:
        dims = (((ca + 1,), (cb + 1,)), ((0,), (0,)))
    else:
        dims = (((ca,), (cb,)), ((), ()))
    dot = lambda p, q: lax.dot_general(p, q, dims, preferred_element_type=F32)
    if hi:
        a_hi, a_lo = _split_bf16(a.astype(F32))
        b_hi, b_lo = _split_bf16(b.astype(F32))
        return dot(a_hi, b_hi) + (dot(a_hi, b_lo) + dot(a_lo, b_hi))
    return dot(a.astype(BF16), b.astype(BF16))


def _make_mm(hi):
    @jax.custom_vjp
    def nn(a, b):
        return _dg(a, b, 1, 0, hi)

    @jax.custom_vjp
    def nt(a, b):
        return _dg(a, b, 1, 1, hi)

    @jax.custom_vjp
    def tn(a, b):
        return _dg(a, b, 0, 0, hi)

    nn.defvjp(lambda a, b: (_dg(a, b, 1, 0, hi), (a, b)),
              lambda r, g: (_dg(g, r[1], 1, 1, hi).astype(r[0].dtype), _dg(r[0], g, 0, 0, hi).astype(r[1].dtype)))
    nt.defvjp(lambda a, b: (_dg(a, b, 1, 1, hi), (a, b)),
              lambda r, g: (_dg(g, r[1], 1, 0, hi).astype(r[0].dtype), _dg(g, r[0], 0, 0, hi).astype(r[1].dtype)))
    tn.defvjp(lambda a, b: (_dg(a, b, 0, 0, hi), (a, b)),
              lambda r, g: (_dg(r[1], g, 1, 1, hi).astype(r[0].dtype), _dg(r[0], g, 1, 0, hi).astype(r[1].dtype)))
    return nn, nt, tn


_nn, _nt, _tn = _make_mm(False)
_nn_hi, _nt_hi, _tn_hi = _make_mm(True)


def _rms(x, w):
    return x * lax.rsqrt(jnp.mean(x * x, axis=-1, keepdims=True) + EPS) * w


def _layernorm(x, w, b):
    xc = x - jnp.mean(x, axis=-1, keepdims=True)
    return xc * lax.rsqrt(jnp.mean(xc * xc, axis=-1, keepdims=True) + EPS) * w + b


def _silu(x):
    return x * jax.nn.sigmoid(x)


def _iota2(shape, dim):
    return lax.broadcasted_iota(jnp.int32, shape, dim)


def _flat_weights(lhs_idx, weights):
    specs, ops, lhs_of, where = [], [], [], []
    for a, (k, w) in enumerate(zip(lhs_idx, weights)):
        for q in range(1 if w.ndim == 2 else w.shape[0]):
            specs.append(_full(w.shape) if w.ndim == 2
                         else pl.BlockSpec((None,) + w.shape[1:], lambda i, q=q: (q, 0, 0)))
            ops.append(w)
            lhs_of.append(k)
            where.append((a, None if w.ndim == 2 else q))
    return specs, ops, lhs_of, where


def _blk_fwd(name, pre, lhs_idx, post, toks, smalls, weights, outs, tm=512):
    wspecs, wops, lhs_of, _ = _flat_weights(lhs_idx, weights)
    nt_, ns, nw = len(toks), len(smalls), len(wops)

    def body(*refs):
        tv = [r[...] for r in refs[:nt_]]
        sv = [r[...] for r in refs[nt_:nt_ + ns]]
        wr = refs[nt_ + ns:nt_ + ns + nw]
        orf = refs[nt_ + ns + nw:]
        lhs = pre(tv, sv)
        ys = [_dg(lhs[i], w[...], 1, 0) for i, w in zip(lhs_of, wr)]
        for o_ref, o in zip(orf, post(ys, tv, sv)):
            o_ref[...] = o.astype(o_ref.dtype)

    in_specs = ([pl.BlockSpec((tm, a.shape[1]), lambda i: (i, 0)) for a in toks]
                + [_full(a.shape) for a in smalls] + wspecs)
    out_specs = [pl.BlockSpec((tm, w_), lambda i: (i, 0)) for w_, _ in outs]
    return pl.pallas_call(
        body, grid=(T // tm,), in_specs=in_specs, out_specs=out_specs,
        out_shape=[_sds((T, w_), dt) for w_, dt in outs], name=name, compiler_params=_cp("parallel"),
    )(*toks, *smalls, *wops)


def _blk_bwd(name, pre, lhs_idx, post, toks, smalls, weights, ct_groups, res=None, tm=256, wchunk=512):
    wspecs, wops, lhs_of, where = _flat_weights(lhs_idx, weights)
    nt_, ns, nw, na = len(toks), len(smalls), len(wops), len(weights)
    cts = [a for g in ct_groups for a in g]
    nc = len(cts)
    widths = [sum(a.shape[1] for a in g) for g in ct_groups]
    has_res = res is not None

    def body(*refs):
        p = 0
        tr = refs[p:p + nt_]; p += nt_
        sr = refs[p:p + ns]; p += ns
        wr = refs[p:p + nw]; p += nw
        cr = refs[p:p + nc]; p += nc
        rr = refs[p:p + has_res]; p += has_res
        dtr = refs[p:p + nt_]; p += nt_
        dsr = refs[p:p + ns]; p += ns
        dwr = refs[p:p + na]; p += na
        scr = refs[p:]
        i = pl.program_id(0)

        @pl.when(i == 0)
        def _():
            for r in list(dsr) + list(dwr):
                r[...] = jnp.zeros_like(r)

        tv = [r[...] for r in tr]
        sv = [r[...] for r in sr]
        ctv, q, si = [], 0, 0
        for g in ct_groups:
            if len(g) == 1:
                ctv.append(cr[q][...].astype(F32))
            else:
                off = 0
                for j, a in enumerate(g):
                    scr[si][:, off:off + a.shape[1]] = cr[q + j][...].astype(F32)
                    off += a.shape[1]
                ctv.append(scr[si][...])
                si += 1
            q += len(g)

        lhs, vjp_pre = jax.vjp(lambda *a: tuple(pre(list(a[:nt_]), list(a[nt_:]))), *tv, *sv)
        lhs_b = [l.astype(BF16) for l in lhs]
        ys = [_dg(lhs_b[k], w[...], 1, 0) for k, w in zip(lhs_of, wr)]
        _, vjp_post = jax.vjp(lambda *a: tuple(post(list(a[:nw]), list(a[nw:nw + nt_]), list(a[nw + nt_:]))),
                              *ys, *tv, *sv)
        gp = vjp_post(tuple(ctv))
        dys, dt_post, ds_post = gp[:nw], gp[nw:nw + nt_], gp[nw + nt_:]
        dlhs = [None] * len(lhs)
        for k, w, dy, (a, q) in zip(lhs_of, wr, dys, where):
            dyb = dy.astype(BF16)
            n = w.shape[1]
            for c0 in range(0, n, wchunk):
                c1 = min(n, c0 + wchunk)
                part = _dg(lhs_b[k], dyb[:, c0:c1], 0, 0)
                if q is None:
                    dwr[a][:, c0:c1] += part
                else:
                    dwr[a][q, :, c0:c1] += part
            d = _dg(dyb, w[...], 1, 1)
            dlhs[k] = d if dlhs[k] is None else dlhs[k] + d
        gq = vjp_pre(tuple(d.astype(l.dtype) for d, l in zip(dlhs, lhs)))
        dt_pre, ds_pre = gq[:nt_], gq[nt_:]
        for j in range(nt_):
            d = dt_post[j] + dt_pre[j]
            if j == 0 and has_res:
                d = d + rr[0][...]
            dtr[j][...] = d
        for j in range(ns):
            dsr[j][...] += ds_post[j] + ds_pre[j]

    tok_spec = lambda a: pl.BlockSpec((tm, a.shape[1]), lambda i: (i, 0))
    in_specs = ([tok_spec(a) for a in toks] + [_full(a.shape) for a in smalls] + wspecs
                + [tok_spec(a) for a in cts] + ([tok_spec(res)] if has_res else []))
    out_specs = [tok_spec(a) for a in toks] + [_full(a.shape) for a in smalls] + [_full(w.shape) for w in weights]
    out_shape = ([_sds(a.shape) for a in toks] + [_sds(a.shape) for a in smalls] + [_sds(w.shape) for w in weights])
    scratch = [pltpu.VMEM((tm, wd), F32) for g, wd in zip(ct_groups, widths) if len(g) > 1]
    outs = pl.pallas_call(
        body, grid=(T // tm,), in_specs=in_specs, out_specs=out_specs, out_shape=out_shape,
        scratch_shapes=scratch, name=name, compiler_params=_cp("arbitrary"),
    )(*toks, *smalls, *wops, *cts, *([res] if has_res else []))
    return outs[:nt_], outs[nt_:nt_ + ns], outs[nt_ + ns:]


def _ffn_fwd(name, x, nw, ffn, idx, tm=512):
    def body(x_ref, nw_ref, wg_ref, wu_ref, wd_ref, o_ref, a_ref, b_ref, h_scr):
        s = pl.program_id(1)

        @pl.when(s == 0)
        def _():
            xv = x_ref[...]
            h_scr[...] = _rms(xv, nw_ref[...]).astype(BF16)
            o_ref[...] = xv

        h = h_scr[...]
        a = _dg(h, wg_ref[...], 1, 1).astype(BF16)
        b = _dg(h, wu_ref[...], 1, 1).astype(BF16)
        a_ref[...] = a
        b_ref[...] = b
        o_ref[...] += 0.5 * _dg(_swiglu_act(a, b)[0], wd_ref[...], 1, 0)

    wspec = lambda k: pl.BlockSpec((None, None, FS, D), lambda i, s: (s, idx, k, 0))
    act = pl.BlockSpec((None, tm, FS), lambda i, s: (s, i, 0))
    return pl.pallas_call(
        body, grid=(T // tm, NCHIP),
        in_specs=[pl.BlockSpec((tm, D), lambda i, s: (i, 0)), _full((1, D)), wspec(0), wspec(1), wspec(2)],
        out_specs=[pl.BlockSpec((tm, D), lambda i, s: (i, 0)), act, act],
        out_shape=[_sds((T, D)), _sds((NCHIP, T, FS), BF16), _sds((NCHIP, T, FS), BF16)],
        scratch_shapes=[pltpu.VMEM((tm, D), BF16)], name=name, compiler_params=_cp("parallel", "arbitrary"),
    )(x, nw, ffn, ffn, ffn)


def _swiglu_act(a, b):
    a, b = a.astype(F32), b.astype(F32)
    sa = jax.nn.sigmoid(a)
    act = a * sa
    return act * b, a, b, sa, act


def _ffn_bwd(name, x, nw, ffn, idx, pre, dy, gbuf=None, tm=512):
    ni = T // tm

    def body(x_ref, dy_ref, nw_ref, wg_ref, wu_ref, wd_ref, a_ref, b_ref, dx_ref, dnw_ref, dffn_ref, dh_acc, ag, au,
             ad):
        s, i = pl.program_id(0), pl.program_id(1)
        rows = pl.ds(pl.multiple_of(i * tm, tm), tm)

        @pl.when((s == 0) & (i == 0))
        def _():
            dnw_ref[...] = jnp.zeros_like(dnw_ref)

        @pl.when(i == 0)
        def _():
            ag[...] = jnp.zeros_like(ag)
            au[...] = jnp.zeros_like(au)
            ad[...] = jnp.zeros_like(ad)

        xv, nwv, dyv = x_ref[...], nw_ref[...], dy_ref[...]
        h, vjp_rms = jax.vjp(_rms, xv, nwv)
        hb = h.astype(BF16)
        gated, a, b, sa, act = _swiglu_act(a_ref[...], b_ref[...])
        dyb = (0.5 * dyv).astype(BF16)
        ad[...] += _dg(gated, dyb, 0, 0)
        dact = _dg(dyb, wd_ref[...], 1, 1)
        da = (dact * b * (sa * (1.0 + a * (1.0 - sa)))).astype(BF16)
        db = (dact * act).astype(BF16)
        ag[...] += _dg(da, hb, 0, 0)
        au[...] += _dg(db, hb, 0, 0)
        dh = _dg(da, wg_ref[...], 1, 0) + _dg(db, wu_ref[...], 1, 0)

        @pl.when(s == 0)
        def _():
            dh_acc[rows, :] = dh

        @pl.when(s > 0)
        def _():
            dh_acc[rows, :] += dh

        @pl.when(s == NCHIP - 1)
        def _():
            dx, dnw = vjp_rms(dh_acc[rows, :])
            dx_ref[...] = dyv + dx
            dnw_ref[...] += dnw

        @pl.when(i == ni - 1)
        def _():
            dffn_ref[0:FS, :] = ag[...].astype(BF16)
            dffn_ref[FS:2 * FS, :] = au[...].astype(BF16)
            dffn_ref[2 * FS:, :] = ad[...].astype(BF16)

    wspec = lambda r, k: pl.BlockSpec((None, None, r, D), lambda s, i: (s, idx, k, 0), pipeline_mode=pl.Buffered(1))
    last = lambda s, i: (jnp.where(s == NCHIP - 1, i, 0), 0)
    nb = 0 if gbuf is None else 1
    act = pl.BlockSpec((None, tm, FS), lambda s, i: (s, i, 0))
    return pl.pallas_call(
        lambda *refs: body(*refs[:8], *refs[8 + nb:]), grid=(NCHIP, ni),
        in_specs=[pl.BlockSpec((tm, D), lambda s, i: (i, 0)), pl.BlockSpec((tm, D), lambda s, i: (i, 0)),
                  _full((1, D)), wspec(FS, 0), wspec(FS, 1), wspec(FS, 2), act, act] + [ANY] * nb,
        out_specs=[pl.BlockSpec((tm, D), last), _full((1, D)), wspec(3 * FS, 0)],
        out_shape=[_sds((T, D)), _sds((1, D)), _sds(ffn.shape, BF16)],
        input_output_aliases={8 + k: 2 + k for k in range(nb)},
        scratch_shapes=[pltpu.VMEM((T, D), F32)] + [pltpu.VMEM((FS, D), F32)] * 3,
        name=name, compiler_params=_cp("arbitrary", "arbitrary"),
    )(x, dy, nw, ffn, ffn, ffn, *pre, *(() if gbuf is None else (gbuf,)))


CONV_ROWS = 256


def _conv_pad(k):
    return 8 * ((k - 1 + 7) // 8)


def _conv_fwd(name, x, w, b, act):
    k_w, c = w.shape
    tc = 256 if c % 256 == 0 else LANES
    pad = _conv_pad(k_w)
    has_b = b is not None

    def body(*refs):
        x_ref, w_ref = refs[0], refs[1]
        b_ref = refs[2] if has_b else None
        y_ref, xp = refs[2 + has_b], refs[3 + has_b]
        xp[0:pad, :] = jnp.zeros((pad, tc), F32)
        xp[pad:, :] = x_ref[...]

        def step(t, carry):
            base = pl.multiple_of(t * CONV_ROWS, CONV_ROWS)
            win = xp[pl.ds(base, CONV_ROWS + pad), :]
            acc = jnp.zeros((CONV_ROWS, tc), F32)
            for k in range(k_w):
                o = pad - (k_w - 1) + k
                acc = acc + w_ref[k:k + 1, :] * win[o:o + CONV_ROWS, :]
            if has_b:
                acc = acc + b_ref[...]
            y_ref[pl.ds(base, CONV_ROWS), :] = _silu(acc) if act else acc
            return carry

        lax.fori_loop(0, T // CONV_ROWS, step, 0)

    col = lambda r: pl.BlockSpec((r, tc), lambda j: (0, j))
    ins = [x, w] + ([b] if has_b else [])
    return pl.pallas_call(
        body, grid=(c // tc,), in_specs=[col(T), col(k_w)] + ([col(1)] if has_b else []), out_specs=col(T),
        out_shape=_sds((T, c)), scratch_shapes=[pltpu.VMEM((T + pad, tc), F32)], name=name,
        compiler_params=_cp("parallel"),
    )(*ins)


def _conv_bwd(name, x, w, b, act, dy):
    k_w, c = w.shape
    tc = 256 if c % 256 == 0 else LANES
    pad = _conv_pad(k_w)
    has_b = b is not None

    def body(*refs):
        x_ref, w_ref, dy_ref = refs[0], refs[1], refs[2]
        b_ref = refs[3] if has_b else None
        dx_ref, dw_ref, db_ref, xp, dp = refs[3 + has_b:]
        xp[0:pad, :] = jnp.zeros((pad, tc), F32)
        xp[pad:, :] = x_ref[...]
        dp[T:, :] = jnp.zeros((pad, tc), F32)
        dw_ref[...] = jnp.zeros_like(dw_ref)
        db_ref[...] = jnp.zeros_like(db_ref)

        def step1(t, carry):
            base = pl.multiple_of(t * CONV_ROWS, CONV_ROWS)
            d = dy_ref[pl.ds(base, CONV_ROWS), :]
            win = xp[pl.ds(base, CONV_ROWS + pad), :]
            offs = [pad - (k_w - 1) + k for k in range(k_w)]
            if act:
                acc = jnp.zeros((CONV_ROWS, tc), F32)
                for k, o in enumerate(offs):
                    acc = acc + w_ref[k:k + 1, :] * win[o:o + CONV_ROWS, :]
                if has_b:
                    acc = acc + b_ref[...]
                sg = jax.nn.sigmoid(acc)
                d = d * (sg * (1.0 + acc * (1.0 - sg)))
            dp[pl.ds(base, CONV_ROWS), :] = d
            for k, o in enumerate(offs):
                dw_ref[k:k + 1, :] += jnp.sum(d * win[o:o + CONV_ROWS, :], axis=0, keepdims=True)
            db_ref[...] += jnp.sum(d, axis=0, keepdims=True)
            return carry

        lax.fori_loop(0, T // CONV_ROWS, step1, 0)

        def step2(t, carry):
            base = pl.multiple_of(t * CONV_ROWS, CONV_ROWS)
            win = dp[pl.ds(base, CONV_ROWS + pad), :]
            acc = jnp.zeros((CONV_ROWS, tc), F32)
            for k in range(k_w):
                o = (k_w - 1) - k
                acc = acc + w_ref[k:k + 1, :] * win[o:o + CONV_ROWS, :]
            dx_ref[pl.ds(base, CONV_ROWS), :] = acc
            return carry

        lax.fori_loop(0, T // CONV_ROWS, step2, 0)

    col = lambda r: pl.BlockSpec((r, tc), lambda j: (0, j))
    ins = [x, w, dy] + ([b] if has_b else [])
    return pl.pallas_call(
        body, grid=(c // tc,), in_specs=[col(T), col(k_w), col(T)] + ([col(1)] if has_b else []),
        out_specs=[col(T), col(k_w), col(1)], out_shape=[_sds((T, c)), _sds((k_w, c)), _sds((1, c))],
        scratch_shapes=[pltpu.VMEM((T + pad, tc), F32), pltpu.VMEM((T + pad, tc), F32)], name=name,
        compiler_params=_cp("parallel"),
    )(*ins)


def _attn_consts(n):
    i = _iota2((BLOCK, 2 * BLOCK), 0)
    j = _iota2((BLOCK, 2 * BLOCK), 1)
    dist = i + BLOCK - j
    valid = (dist >= 0) & (dist < WINDOW) & ((n > 0) | (j >= BLOCK))
    return dist.astype(F32), valid


def _attn_block(q4, kk, vv, sinks, dist, valid, kv):
    outs = []
    lane = _iota2((1, HEADS), 1)
    for g in range(GROUP):
        h = kv * GROUP + g
        slope = 2.0 ** (-8.0 * (h + 1) / HEADS)
        s = _nt(q4[:, g * HDIM:(g + 1) * HDIM], kk) * (HDIM ** -0.5)
        s = jnp.where(valid, s - slope * dist, -1e30)
        sink = jnp.sum(jnp.where(lane == h, sinks, 0.0), axis=1, keepdims=True)
        m = jnp.maximum(jnp.max(s, axis=-1, keepdims=True), sink)
        e = jnp.exp(s - m)
        p = e / (jnp.sum(e, axis=-1, keepdims=True) + jnp.exp(sink - m))
        outs.append(_nn(p, vv))
    return tuple(outs)


def _attn_fwd(name, qa, ka, va, sinks):
    def body(q_ref, k_ref, v_ref, s_ref, o_ref, kp, vp):
        kp[0:BLOCK, :] = jnp.zeros((BLOCK, KV_A), F32)
        vp[0:BLOCK, :] = jnp.zeros((BLOCK, KV_A), F32)
        kp[BLOCK:, :] = k_ref[...]
        vp[BLOCK:, :] = v_ref[...]
        sinks_v = s_ref[...]

        def step(n, carry):
            r = pl.multiple_of(n * BLOCK, BLOCK)
            dist, valid = _attn_consts(n)
            k2 = kp[pl.ds(r, 2 * BLOCK), :]
            v2 = vp[pl.ds(r, 2 * BLOCK), :]
            for kv in range(KV_HEADS):
                q4 = q_ref[pl.ds(r, BLOCK), kv * GROUP * HDIM:(kv + 1) * GROUP * HDIM]
                og = _attn_block(q4, k2[:, kv * HDIM:(kv + 1) * HDIM], v2[:, kv * HDIM:(kv + 1) * HDIM], sinks_v,
                                 dist, valid, kv)
                for g in range(GROUP):
                    h = kv * GROUP + g
                    o_ref[pl.ds(r, BLOCK), h * HDIM:(h + 1) * HDIM] = og[g]
            return carry

        lax.fori_loop(0, T // BLOCK, step, 0)

    return pl.pallas_call(
        body, out_shape=_sds((T, Q_A)),
        scratch_shapes=[pltpu.VMEM((T + BLOCK, KV_A), F32), pltpu.VMEM((T + BLOCK, KV_A), F32)], name=name,
        compiler_params=pltpu.CompilerParams(vmem_limit_bytes=VMEM_LIMIT),
    )(qa, ka, va, sinks)


def _attn_bwd(name, qa, ka, va, sinks, do):
    def body(q_ref, k_ref, v_ref, s_ref, do_ref, dq_ref, dk_ref, dv_ref, ds_ref, kp, vp, dkp, dvp):
        kp[0:BLOCK, :] = jnp.zeros((BLOCK, KV_A), F32)
        vp[0:BLOCK, :] = jnp.zeros((BLOCK, KV_A), F32)
        kp[BLOCK:, :] = k_ref[...]
        vp[BLOCK:, :] = v_ref[...]
        dkp[...] = jnp.zeros_like(dkp)
        dvp[...] = jnp.zeros_like(dvp)
        ds_ref[...] = jnp.zeros_like(ds_ref)
        sinks_v = s_ref[...]

        def step(n, carry):
            r = pl.multiple_of(n * BLOCK, BLOCK)
            dist, valid = _attn_consts(n)
            k2 = kp[pl.ds(r, 2 * BLOCK), :]
            v2 = vp[pl.ds(r, 2 * BLOCK), :]
            for kv in range(KV_HEADS):
                cols = slice(kv * HDIM, (kv + 1) * HDIM)
                q4 = q_ref[pl.ds(r, BLOCK), kv * GROUP * HDIM:(kv + 1) * GROUP * HDIM]
                _, vjp = jax.vjp(lambda q, k, v, s: _attn_block(q, k, v, s, dist, valid, kv),
                                 q4, k2[:, cols], v2[:, cols], sinks_v)
                cts = tuple(do_ref[pl.ds(r, BLOCK), (kv * GROUP + g) * HDIM:(kv * GROUP + g + 1) * HDIM]
                            for g in range(GROUP))
                dq4, dkk, dvv, dsk = vjp(cts)
                dq_ref[pl.ds(r, BLOCK), kv * GROUP * HDIM:(kv + 1) * GROUP * HDIM] = dq4
                dkp[pl.ds(r, 2 * BLOCK), cols] += dkk
                dvp[pl.ds(r, 2 * BLOCK), cols] += dvv
                ds_ref[...] += dsk
            return carry

        lax.fori_loop(0, T // BLOCK, step, 0)
        dk_ref[...] = dkp[BLOCK:, :]
        dv_ref[...] = dvp[BLOCK:, :]

    pad = lambda: pltpu.VMEM((T + BLOCK, KV_A), F32)
    return pl.pallas_call(
        body, out_shape=[_sds((T, Q_A)), _sds((T, KV_A)), _sds((T, KV_A)), _sds((1, HEADS))],
        scratch_shapes=[pad(), pad(), pad(), pad()], name=name,
        compiler_params=pltpu.CompilerParams(vmem_limit_bytes=VMEM_LIMIT),
    )(qa, ka, va, sinks, do)


def _dn_consts():
    i = _iota2((CHUNK, CHUNK), 0)
    j = _iota2((CHUNK, CHUNK), 1)
    return dict(causal=i >= j, strict=i > j, eye=(i == j).astype(F32), ltri=(i >= j).astype(F32),
                ones=jnp.ones((CHUNK, CHUNK), F32), last=(_iota2((CHUNK, 1), 0) == CHUNK - 1).astype(F32))


def _l2norm(x):
    return x * lax.rsqrt(jnp.sum(x * x, axis=-1, keepdims=True) + EPS)


def _head_cols(m):
    lane = _iota2((1, HEADS), 1)
    return jnp.concatenate([jnp.sum(jnp.where(lane == h, m, 0.0), axis=1, keepdims=True)[None]
                            for h in range(HEADS)], axis=0)


def _dn_local(q3, k3, v3, braw, araw, alog, dtb, cs):
    q = _l2norm(q3) * (HDIM ** -0.5)
    k = _l2norm(k3)
    g = -jnp.exp(alog) * jax.nn.softplus(araw + dtb)
    gc_all = _nn_hi(cs["ltri"], g)
    egc_all = jnp.exp(gc_all)
    beta, gc, egc = _head_cols(jax.nn.sigmoid(braw)), _head_cols(gc_all), _head_cols(egc_all)
    a = jnp.broadcast_to(gc, (HEADS, CHUNK, CHUNK))
    diff = a - jnp.swapaxes(a, 1, 2)
    decay = jnp.where(cs["causal"], jnp.exp(jnp.where(cs["causal"], diff, 0.0)), 0.0)
    kb = k * beta
    low = jnp.where(cs["strict"], _nt(kb, k) * decay, 0.0)
    inv = cs["eye"] - low
    pw = low
    for _ in range(5):
        pw = _nn_hi(pw, pw)
        inv = inv + _nn_hi(inv, pw)
    u = _nn_hi(inv, v3 * beta)
    w = _nn_hi(inv, kb * egc)
    attn = _nt(q, k) * decay
    gc_last = jnp.sum(gc * cs["last"], axis=1, keepdims=True)
    return u, w, attn, q * egc, k * jnp.exp(gc_last - gc), egc_all


def _heads3(ref, off=0):
    return jnp.concatenate([ref[:, off + h * HDIM:off + (h + 1) * HDIM][None] for h in range(HEADS)], axis=0)


def _dn_local_fwd(name, qkv, ba, alog, dtb):
    def body(qkv_ref, ba_ref, al_ref, dt_ref, u_ref, w_ref, at_ref, qd_ref, kd_ref, eg_ref):
        bav = ba_ref[...]
        outs = _dn_local(_heads3(qkv_ref), _heads3(qkv_ref, 512), _heads3(qkv_ref, 1024), bav[:, :HEADS],
                         bav[:, HEADS:], al_ref[...], dt_ref[...], _dn_consts())
        for r, o in zip((u_ref, w_ref, at_ref, qd_ref, kd_ref), outs[:5]):
            for h in range(HEADS):
                r[:, h * HDIM:(h + 1) * HDIM] = o[h]
        eg_ref[...] = outs[5]

    row = lambda w_: pl.BlockSpec((CHUNK, w_), lambda n: (n, 0))
    return pl.pallas_call(
        body, grid=(NCHUNK,), in_specs=[row(QKV_B), row(2 * HEADS), _full((1, HEADS)), _full((1, HEADS))],
        out_specs=[row(V_B)] * 5 + [row(HEADS)], out_shape=[_sds((T, V_B))] * 5 + [_sds((T, HEADS))], name=name,
        compiler_params=_cp("parallel"),
    )(qkv, ba, alog, dtb)


def _dn_local_bwd(name, qkv, ba, alog, dtb, cts):
    def body(qkv_ref, ba_ref, al_ref, dt_ref, du_ref, dw_ref, dat_ref, dqd_ref, dkd_ref, deg_ref,
             dqkv_ref, dba_ref, dal_ref, ddt_ref):
        @pl.when(pl.program_id(0) == 0)
        def _():
            dal_ref[...] = jnp.zeros_like(dal_ref)
            ddt_ref[...] = jnp.zeros_like(ddt_ref)

        cs = _dn_consts()
        bav = ba_ref[...]
        _, vjp = jax.vjp(lambda *a: _dn_local(*a, cs), _heads3(qkv_ref), _heads3(qkv_ref, 512),
                         _heads3(qkv_ref, 1024), bav[:, :HEADS], bav[:, HEADS:], al_ref[...], dt_ref[...])
        dq, dk, dv, dbr, dar, dal, ddt = vjp((_heads3(du_ref), _heads3(dw_ref), _heads3(dat_ref), _heads3(dqd_ref),
                                              _heads3(dkd_ref), deg_ref[...]))
        for h in range(HEADS):
            dqkv_ref[:, h * HDIM:(h + 1) * HDIM] = dq[h]
            dqkv_ref[:, 512 + h * HDIM:512 + (h + 1) * HDIM] = dk[h]
            dqkv_ref[:, 1024 + h * HDIM:1024 + (h + 1) * HDIM] = dv[h]
        dba_ref[:, :HEADS] = dbr
        dba_ref[:, HEADS:] = dar
        dal_ref[...] += dal
        ddt_ref[...] += ddt

    row = lambda w_: pl.BlockSpec((CHUNK, w_), lambda n: (n, 0))
    return pl.pallas_call(
        body, grid=(NCHUNK,),
        in_specs=[row(QKV_B), row(2 * HEADS), _full((1, HEADS)), _full((1, HEADS))] + [row(V_B)] * 5 + [row(HEADS)],
        out_specs=[row(QKV_B), row(2 * HEADS), _full((1, HEADS)), _full((1, HEADS))],
        out_shape=[_sds((T, QKV_B)), _sds((T, 2 * HEADS)), _sds((1, HEADS)), _sds((1, HEADS))], name=name,
        compiler_params=_cp("arbitrary"),
    )(qkv, ba, alog, dtb, *cts)


def _dn_step(s, u, w, attn, qd, kd, egc, z, nw):
    last = (_iota2((CHUNK, 1), 0) == CHUNK - 1).astype(F32)
    gl = jnp.sum(_head_cols(egc) * last, axis=1, keepdims=True)
    v_new = u - _nn(w, s)
    o = _nn(qd, s) + _nn(attn, v_new)
    s_new = s * gl + _tn(kd, v_new)
    return s_new, _rms(o, nw) * _silu(z)


def _unheads(ref, v3):
    for h in range(HEADS):
        ref[:, h * HDIM:(h + 1) * HDIM] = v3[h]


def _dn_rec_fwd(name, u, w, attn, qd, kd, egc, z, nw):
    def body(u_ref, w_ref, at_ref, qd_ref, kd_ref, eg_ref, z_ref, nw_ref, o_ref, ss_ref, s_scr):
        @pl.when(pl.program_id(0) == 0)
        def _():
            s_scr[...] = jnp.zeros_like(s_scr)

        s = s_scr[...]
        ss_ref[...] = s
        s_new, on = _dn_step(s, _heads3(u_ref), _heads3(w_ref), _heads3(at_ref), _heads3(qd_ref), _heads3(kd_ref),
                             eg_ref[...], _heads3(z_ref), nw_ref[...])
        s_scr[...] = s_new
        _unheads(o_ref, on)

    row = lambda w_: pl.BlockSpec((CHUNK, w_), lambda n: (n, 0))
    return pl.pallas_call(
        body, grid=(NCHUNK,), in_specs=[row(V_B)] * 5 + [row(HEADS), row(V_B), _full((1, HDIM))],
        out_specs=[row(V_B), pl.BlockSpec((None, HEADS, HDIM, HDIM), lambda n: (n, 0, 0, 0))],
        out_shape=[_sds((T, V_B)), _sds((NCHUNK, HEADS, HDIM, HDIM))],
        scratch_shapes=[pltpu.VMEM((HEADS, HDIM, HDIM), F32)], name=name, compiler_params=_cp("arbitrary"),
    )(u, w, attn, qd, kd, egc, z, nw)


def _dn_rec_bwd(name, u, w, attn, qd, kd, egc, z, nw, ss, do):
    def body(u_ref, w_ref, at_ref, qd_ref, kd_ref, eg_ref, z_ref, nw_ref, ss_ref, do_ref,
             du_ref, dw_ref, dat_ref, dqd_ref, dkd_ref, deg_ref, dz_ref, dnw_ref, ds_scr):
        @pl.when(pl.program_id(0) == 0)
        def _():
            ds_scr[...] = jnp.zeros_like(ds_scr)
            dnw_ref[...] = jnp.zeros_like(dnw_ref)

        _, vjp = jax.vjp(_dn_step, ss_ref[...], _heads3(u_ref), _heads3(w_ref), _heads3(at_ref), _heads3(qd_ref),
                         _heads3(kd_ref), eg_ref[...], _heads3(z_ref), nw_ref[...])
        ds, du, dw, dat, dqd, dkd, deg, dz, dnw = vjp((ds_scr[...], _heads3(do_ref)))
        ds_scr[...] = ds
        for r, v in zip((du_ref, dw_ref, dat_ref, dqd_ref, dkd_ref, dz_ref), (du, dw, dat, dqd, dkd, dz)):
            _unheads(r, v)
        deg_ref[...] = deg
        dnw_ref[...] += dnw

    row = lambda w_: pl.BlockSpec((CHUNK, w_), lambda n: (NCHUNK - 1 - n, 0))
    return pl.pallas_call(
        body, grid=(NCHUNK,),
        in_specs=[row(V_B)] * 5 + [row(HEADS), row(V_B), _full((1, HDIM)),
                                   pl.BlockSpec((None, HEADS, HDIM, HDIM), lambda n: (NCHUNK - 1 - n, 0, 0, 0)),
                                   row(V_B)],
        out_specs=[row(V_B)] * 5 + [row(HEADS), row(V_B), _full((1, HDIM))],
        out_shape=[_sds((T, V_B))] * 5 + [_sds((T, HEADS)), _sds((T, V_B)), _sds((1, HDIM))],
        scratch_shapes=[pltpu.VMEM((HEADS, HDIM, HDIM), F32)], name=name, compiler_params=_cp("arbitrary"),
    )(u, w, attn, qd, kd, egc, z, nw, ss, do)


def _final(name, x, fw, target, tm=512):
    def body(x_ref, fw_ref, t_ref, l_ref, dx_ref, dfw_ref):
        @pl.when(pl.program_id(0) == 0)
        def _():
            l_ref[...] = jnp.zeros_like(l_ref)
            dfw_ref[...] = jnp.zeros_like(dfw_ref)

        tv = t_ref[...]

        def f(xv, fwv):
            err = _rms(xv, fwv) - tv
            per_tok = jnp.mean(err * err, axis=-1, keepdims=True)
            return 0.5 * jnp.sum(per_tok, axis=0, keepdims=True)

        loss, vjp = jax.vjp(f, x_ref[...], fw_ref[...])
        dx, dfw = vjp(jnp.ones((1, 1), F32))
        l_ref[...] += loss
        dx_ref[...] = dx
        dfw_ref[...] += dfw

    tok = pl.BlockSpec((tm, D), lambda i: (i, 0))
    return pl.pallas_call(
        body, grid=(T // tm,), in_specs=[tok, _full((1, D)), tok], out_specs=[_full((1, 1)), tok, _full((1, D))],
        out_shape=[_sds((1, 1)), _sds((T, D)), _sds((1, D))], name=name, compiler_params=_cp("arbitrary"),
    )(x, fw, target)


def _m1_pre(tv, sv):
    return [_rms(tv[0], sv[0])]


def _m1_post(ys, tv, sv):
    return (jnp.concatenate(ys, axis=1),)


def _m1_post_split(ys, tv, sv):
    proj = jnp.concatenate(ys, axis=1)
    return tuple(proj[:, a:b] for a, b in zip(IN_SPLITS[:-1], IN_SPLITS[1:]))


def _m5_pre(tv, sv):
    return [tv[1], tv[2]]


def _m5_post(ys, tv, sv):
    return (tv[0] + ys[0] + ys[1],)


def _c1_pre(tv, sv):
    return [_rms(tv[0], sv[0])]


def _c1_post(ys, tv, sv):
    return ((jnp.concatenate(ys[:2], axis=1) + sv[1]) * jax.nn.sigmoid(jnp.concatenate(ys[2:], axis=1) + sv[2]),)


def _c3_pre(tv, sv):
    return [_silu(_layernorm(tv[0], sv[0], sv[1]))]


def _c3_post(ys, tv, sv):
    return (tv[1] + ys[0] + sv[2],)


def _row(v):
    return v.reshape(1, -1)


def _mixer_fwd(tag, x, p):
    parts = _blk_fwd(f"m1_fwd_{tag}", _m1_pre, [0], _m1_post_split, [x], [p["nw"]], [p["w_in"]],
                     [(b - a, F32) for a, b in zip(IN_SPLITS[:-1], IN_SPLITS[1:])])
    qa, ka, va, qkvb, z, ba = parts
    att = _attn_fwd(f"attn_fwd_{tag}", qa, ka, va, p["sinks"])
    qkvc = _conv_fwd(f"dnconv_fwd_{tag}", qkvb, p["dn_conv_w"], None, True)
    loc = _dn_local_fwd(f"dnloc_fwd_{tag}", qkvc, ba, p["a_log"], p["dt_bias"])
    og, ss = _dn_rec_fwd(f"dnrec_fwd_{tag}", *loc, z, p["dn_norm_w"])
    (out,) = _blk_fwd(f"m5_fwd_{tag}", _m5_pre, [0, 1], _m5_post, [x, att, og], [], [p["wo_a"], p["wo_b"]],
                      [(D, F32)])
    return out, dict(x=x, qa=qa, ka=ka, va=va, qkvb=qkvb, z=z, ba=ba, att=att, qkvc=qkvc, loc=loc, og=og, ss=ss)


def _mixer_bwd(tag, dy, p, s):
    (dxa, datt, dog), _, (dwo_a, dwo_b) = _blk_bwd(f"m5_bwd_{tag}", _m5_pre, [0, 1], _m5_post,
                                                   [s["x"], s["att"], s["og"]], [], [p["wo_a"], p["wo_b"]], [[dy]])
    rec = _dn_rec_bwd(f"dnrec_bwd_{tag}", *s["loc"], s["z"], p["dn_norm_w"], s["ss"], dog)
    dz, dnw_dn = rec[6], rec[7]
    dqkvc, dba, dalog, ddtb = _dn_local_bwd(f"dnloc_bwd_{tag}", s["qkvc"], s["ba"], p["a_log"], p["dt_bias"],
                                            rec[:6])
    dqkvb, dconvw, _ = _conv_bwd(f"dnconv_bwd_{tag}", s["qkvb"], p["dn_conv_w"], None, True, dqkvc)
    dqa, dka, dva, dsinks = _attn_bwd(f"attn_bwd_{tag}", s["qa"], s["ka"], s["va"], p["sinks"], datt)
    (dx,), (dnw,), (dw_in,) = _blk_bwd(f"m1_bwd_{tag}", _m1_pre, [0], _m1_post, [s["x"]], [p["nw"]], [p["w_in"]],
                                       [[dqa, dka, dva, dqkvb, dz, dba]], res=dxa)
    return dx, dict(nw=dnw, w_in=dw_in, wo_a=dwo_a, wo_b=dwo_b, dn_conv_w=dconvw, sinks=dsinks, a_log=dalog,
                    dt_bias=ddtb, dn_norm_w=dnw_dn)


def _conformer_fwd(tag, x, p):
    (glu,) = _blk_fwd(f"c1_fwd_{tag}", _c1_pre, [0], _c1_post, [x], [p["nw"], p["b1a"], p["b1b"]], [p["w1"]],
                      [(D, F32)])
    cc = _conv_fwd(f"dwconv_fwd_{tag}", glu, p["w_dw"], p["b_dw"], False)
    (out,) = _blk_fwd(f"c3_fwd_{tag}", _c3_pre, [0], _c3_post, [cc, x], [p["ln_w"], p["ln_b"], p["b2"]], [p["w2"]],
                      [(D, F32)])
    return out, dict(x=x, glu=glu, cc=cc)


def _conformer_bwd(tag, dy, p, s):
    (dcc, dxa), (dlnw, dlnb, db2), (dw2,) = _blk_bwd(f"c3_bwd_{tag}", _c3_pre, [0], _c3_post, [s["cc"], s["x"]],
                                                     [p["ln_w"], p["ln_b"], p["b2"]], [p["w2"]], [[dy]])
    dglu, dwdw, dbdw = _conv_bwd(f"dwconv_bwd_{tag}", s["glu"], p["w_dw"], p["b_dw"], False, dcc)
    (dx,), (dnw, db1a, db1b), (dw1,) = _blk_bwd(f"c1_bwd_{tag}", _c1_pre, [0], _c1_post, [s["x"]],
                                                [p["nw"], p["b1a"], p["b1b"]], [p["w1"]], [[dglu]], res=dxa)
    return dx, dict(nw=dnw, b1a=db1a, b1b=db1b, w1=dw1, w_dw=dwdw, b_dw=dbdw, ln_w=dlnw, ln_b=dlnb, b2=db2, w2=dw2)


def _layer_fwd(l, x, nw, ffn, p):
    x1, *pre_a = _ffn_fwd(f"ffn_fwd_{l}a", x, _row(nw[0]), ffn, 0)
    p = dict(p, nw=_row(nw[1]))
    x2, sv = (_mixer_fwd if l % 2 == 0 else _conformer_fwd)(str(l), x1, p)
    out, *pre_b = _ffn_fwd(f"ffn_fwd_{l}b", x2, _row(nw[2]), ffn, 1)
    return out, (x, x2, p, sv, pre_a, pre_b)


def _layer_bwd(l, dx, nw, ffn, saved, after_first=lambda dx: dx):
    x0, x2, p, sv, pre_a, pre_b = saved
    dx, dn2, dffn = _ffn_bwd(f"ffn_bwd_{l}b", x2, _row(nw[2]), ffn, 1, pre_b, dx)
    dx = after_first(dx)
    dx, dmix = (_mixer_bwd if l % 2 == 0 else _conformer_bwd)(str(l), dx, p, sv)
    dx, dn0, dffn = _ffn_bwd(f"ffn_bwd_{l}a", x0, _row(nw[0]), ffn, 0, pre_a, dx, dffn)
    return dx, jnp.concatenate([dn0, dmix.pop("nw"), dn2], axis=0), dffn, dmix


def _place():
    x, y, c = lax.axis_index("x"), lax.axis_index("y"), lax.axis_index("c")
    chips = [(1 - x, y), (x, 1 - y), (1 - x, 1 - y)]
    return x, y, c, 2 * x + y, chips, [2 * px + py for px, py in chips]


def _handshake(peers):
    barrier = pltpu.get_barrier_semaphore()
    for p in peers:
        pl.semaphore_signal(barrier, inc=1, device_id=p, device_id_type=MESH)
    pl.semaphore_wait(barrier, len(peers))


def _chip_peers():
    x, y, c, _, chips, _ = _place()
    return [(*chip, c) for chip in chips] + [(x, y, 1 - c)]


def _gather_copies(ins, outs, nb, send, recv, fsend, frecv, lsem):
    n_in = len(ins)
    x, y, c, me, chips, cidx = _place()
    sib = (x, y, 1 - c)
    local = [pltpu.make_async_copy(ins[a], outs[a].at[me], lsem.at[a]) for a in range(n_in)]
    for cp in local:
        cp.start()

    def region(a, k, who):
        if k < 2:
            return outs[a].at[cidx[k], pl.ds(who, 1)]
        r = ins[a].shape[1] // 2
        return outs[a].at[cidx[2], pl.ds(who, 1), pl.ds((k - 2) * r, r)]

    def hop(a, k):
        if k < 2:
            src, dst = ins[a].at[pl.ds(c, 1)], outs[a].at[me, pl.ds(c, 1)]
        else:
            r = ins[a].shape[1] // 2
            src = dst = outs[a].at[cidx[3 - k], pl.ds(c, 1), pl.ds((k - 2) * r, r)]
        return pltpu.make_async_remote_copy(src, dst, send.at[4 * a + k], recv.at[4 * a + k],
                                            device_id=(*chips[k % 2], c), device_id_type=MESH)

    def landed(a, k):
        dst = region(a, k, c)
        return pltpu.make_async_remote_copy(dst, dst, send.at[4 * a + k], recv.at[4 * a + k],
                                            device_id=(*chips[k % 2], c), device_id_type=MESH)

    def passed(a, k, who):
        part = region(a, k, who)
        return pltpu.make_async_remote_copy(part, part, fsend.at[4 * a + k], frecv.at[4 * a + k], device_id=sib,
                                            device_id_type=MESH)

    def direct(a, j):
        k = 4 * nb + 3 * (a - nb) + j
        return pltpu.make_async_remote_copy(ins[a], outs[a].at[me], send.at[k], recv.at[k],
                                            device_id=(*chips[j], c), device_id_type=MESH)

    def direct_landed(a, j):
        k = 4 * nb + 3 * (a - nb) + j
        dst = outs[a].at[cidx[j]]
        return pltpu.make_async_remote_copy(dst, dst, send.at[k], recv.at[k], device_id=(*chips[j], c),
                                            device_id_type=MESH)

    sends = [hop(a, k) for a in range(nb) for k in range(2)] + [direct(a, j) for a in range(nb, n_in) for j in range(3)]
    for cp in sends:
        cp.start()
    for a in range(nb):
        for k in (1, 0):
            landed(a, k).wait_recv()
            for cp in (hop(a, 3 - k), passed(a, k, c)):
                cp.start()
                sends.append(cp)
    for a in range(nb):
        for k in (2, 3):
            landed(a, k).wait_recv()
            cp = passed(a, k, c)
            cp.start()
            sends.append(cp)
    for a in range(nb, n_in):
        for j in range(3):
            direct_landed(a, j).wait_recv()
    for a in range(nb):
        for k in range(4):
            passed(a, k, 1 - c).wait_recv()
    for cp in sends:
        cp.wait_send()
    for cp in local:
        cp.wait()


def _gather_sems(n_in, nb):
    dma = pltpu.SemaphoreType.DMA
    n_ici = 4 * nb + 3 * (n_in - nb)
    return [dma((n_ici,)), dma((n_ici,)), dma((4 * nb,)), dma((4 * nb,)), dma((n_in,))]


def _gather_async(name, halved, whole=()):
    nb, arrs = len(halved), list(halved) + list(whole)
    hbm = pltpu.MemorySpace.HBM
    ins = [jax.new_ref(a, memory_space=hbm) for a in arrs]
    outs = [jax.empty_ref(_sds((NCHIP,) + a.shape, a.dtype), memory_space=hbm) for a in arrs]

    @pl.kernel(mesh=plsc.ScalarSubcoreMesh(axis_name="seq", num_cores=1), name=name,
               scratch_types=tuple(_gather_sems(len(arrs), nb)),
               compiler_params=pltpu.CompilerParams(collective_id=2))
    def launch(send, recv, fsend, frecv, lsem):
        _handshake(_chip_peers())
        _gather_copies(ins, outs, nb, send, recv, fsend, frecv, lsem)

    launch()
    return outs


def _swap_halves(name, grads, after=None):
    n = len(grads)
    hbm = pltpu.MemorySpace.HBM
    ins = [jax.new_ref(g, memory_space=hbm) for g in grads]
    outs = [jax.empty_ref(_sds((NCHIP, g.shape[1] // 2) + g.shape[2:], g.dtype), memory_space=hbm) for g in grads]
    tile = (2 * 8, LANES)
    token = None if after is None else jax.empty_ref(_sds(tile, BF16), memory_space=hbm)

    @pl.kernel(mesh=plsc.ScalarSubcoreMesh(axis_name="seq", num_cores=1), name=name,
               scratch_types=(pltpu.SemaphoreType.DMA((n + 1,)), pltpu.SemaphoreType.DMA((n,))),
               compiler_params=pltpu.CompilerParams(collective_id=1))
    def launch(send, recv):
        x, y, c, _, _, _ = _place()
        sib = (x, y, 1 - c)
        _handshake([sib])
        if after is not None:
            tick = pltpu.make_async_copy(after.at[0, 0, 0, pl.ds(0, tile[0]), pl.ds(0, tile[1])], token, send.at[n])
            tick.start()
            tick.wait()
        cps = []
        for a in range(n):
            h = grads[a].shape[1] // 2
            cps.append(pltpu.make_async_remote_copy(ins[a].at[:, pl.ds((1 - c) * h, h)], outs[a], send.at[a],
                                                    recv.at[a], device_id=sib, device_id_type=MESH))
        for cp in cps:
            cp.start()
        for cp in cps:
            cp.wait()

    launch()
    return outs


def _row_tile(r, cap=256):
    return max(t for t in range(8, cap + 1, 8) if r % t == 0)


def _add_half(name, g, r, c_arr):
    _, l, rows, cols = g.shape
    h = l // 2
    tr = _row_tile(rows)

    def body(c_ref, g_ref, r_ref, o_ref):
        o_ref[...] = (g_ref[...].astype(F32) + r_ref[...].astype(F32)).astype(BF16)

    blk = (None, None, tr, cols)
    return pl.pallas_call(
        body,
        grid_spec=pltpu.PrefetchScalarGridSpec(
            num_scalar_prefetch=1, grid=(NCHIP, h, rows // tr),
            in_specs=[pl.BlockSpec(blk, lambda j, i, t, c_ref: (j, c_ref[0] * h + i, t, 0)),
                      pl.BlockSpec(blk, lambda j, i, t, c_ref: (j, i, t, 0))],
            out_specs=pl.BlockSpec(blk, lambda j, i, t, c_ref: (j, i, t, 0))),
        out_shape=_sds((NCHIP, h, rows, cols), BF16), name=name,
        compiler_params=_cp("parallel", "parallel", "parallel"),
    )(c_arr, g, r)


def _scatter_async(name, parts, sums, where):
    nb = len(parts)
    ins = [jax.new_ref(p, memory_space=pltpu.MemorySpace.HBM) for p in parts]
    dma = pltpu.SemaphoreType.DMA

    @pl.kernel(mesh=plsc.ScalarSubcoreMesh(axis_name="seq", num_cores=1), name=name,
               scratch_types=(dma((3 * nb,)), dma((3 * nb,)), dma((4 * nb,)), dma((4 * nb,)), dma((nb,))),
               compiler_params=pltpu.CompilerParams(collective_id=3))
    def launch(send, recv, fsend, frecv, lsem):
        _handshake(_chip_peers())
        x, y, c, me, chips, cidx = _place()
        sib = (x, y, 1 - c)

        def slot(a, half, chip):
            return sums[a].at[half, chip, pl.ds(where[a], 1)]

        local = [pltpu.make_async_copy(ins[a].at[me], slot(a, c, me), lsem.at[a]) for a in range(nb)]
        for cp in local:
            cp.start()

        def ici(a, j):
            return pltpu.make_async_remote_copy(ins[a].at[cidx[j]], slot(a, c, me), send.at[a * 3 + j],
                                                recv.at[a * 3 + j], device_id=(*chips[j], c), device_id_type=MESH)

        def landed(a, j):
            dst = slot(a, c, cidx[j])
            return pltpu.make_async_remote_copy(dst, dst, send.at[a * 3 + j], recv.at[a * 3 + j],
                                                device_id=(*chips[j], c), device_id_type=MESH)

        def passed(a, j, who):
            dst = slot(a, who, me if j == 3 else cidx[j])
            src = ins[a].at[me] if j == 3 else dst
            return pltpu.make_async_remote_copy(src, dst, fsend.at[a * 4 + j], frecv.at[a * 4 + j], device_id=sib,
                                                device_id_type=MESH)

        sends = [ici(a, j) for a in range(nb) for j in range(3)] + [passed(a, 3, c) for a in range(nb)]
        for cp in sends:
            cp.start()
        for a in range(nb):
            for j in range(3):
                landed(a, j).wait_recv()
                cp = passed(a, j, c)
                cp.start()
                sends.append(cp)
        for a in range(nb):
            for j in range(4):
                passed(a, j, 1 - c).wait_recv()
        for cp in sends:
            cp.wait_send()
        for cp in local:
            cp.wait()

    launch()


def _exchange_small(small, rep):
    def body(small_in, rep_in, small_out, rep_out, lsem, ssend, srecv):
        x, y, c, me, _, _ = _place()
        dev = 4 * x + 2 * y + c
        local = [pltpu.make_async_copy(small_in.at[me], small_out.at[dev], lsem.at[0]),
                 pltpu.make_async_copy(rep_in, rep_out.at[dev], lsem.at[1])]
        for cp in local:
            cp.start()

        def peer(r):
            return (1 - x if r & 4 else x), (1 - y if r & 2 else y), (1 - c if r & 1 else c)

        def tiny(r, which):
            px, py, pc = peer(r)
            k = (r - 1) * 2 + which
            if which == 0:
                return pltpu.make_async_remote_copy(small_in.at[2 * px + py], small_out.at[dev], ssend.at[k],
                                                    srecv.at[k], device_id=(px, py, pc), device_id_type=MESH)
            return pltpu.make_async_remote_copy(rep_in, rep_out.at[dev], ssend.at[k], srecv.at[k],
                                                device_id=(px, py, pc), device_id_type=MESH)

        def tiny_landed(r, which):
            px, py, pc = peer(r)
            k = (r - 1) * 2 + which
            dst = (small_out if which == 0 else rep_out).at[4 * px + 2 * py + pc]
            return pltpu.make_async_remote_copy(dst, dst, ssend.at[k], srecv.at[k], device_id=(px, py, pc),
                                                device_id_type=MESH)

        sends = [tiny(r, w) for r in range(1, NDEV) for w in range(2)]
        for cp in sends:
            cp.start()
        for r in range(1, NDEV):
            for w in range(2):
                tiny_landed(r, w).wait_recv()
        for cp in sends:
            cp.wait_send()
        for cp in local:
            cp.wait()

    dma = pltpu.SemaphoreType.DMA
    return pl.pallas_call(
        body, in_specs=[ANY] * 2, out_specs=[ANY] * 2,
        out_shape=[_sds((NDEV,) + small.shape[1:], F32), _sds((NDEV,) + rep.shape, F32)],
        scratch_shapes=[dma((2,)), dma((2 * (NDEV - 1),)), dma((2 * (NDEV - 1),))], name="exchange_small_grads",
    )(small, rep)


def _adamw_math(w, g, m, v):
    m = B1 * m + (1.0 - B1) * g
    v = B2 * v + (1.0 - B2) * (g * g)
    m_hat = m / (1.0 - B1 ** STEP)
    v_hat = v / (1.0 - B2 ** STEP)
    return -LR * (m_hat / (jnp.sqrt(v_hat) + AEPS) + WD * w), m, v


def _adamw_big(name, w, m, v, parts, row0=0, first=0, outs=None):
    _, _, rows, cols = w.shape
    n = parts.shape[2]
    tr = _row_tile(rows)
    t0 = row0 // tr

    def body(w_ref, m_ref, v_ref, p_ref, *rest):
        g_ref, d_ref, nm_ref, nv_ref = rest[-4:]
        g = p_ref[0].astype(F32)
        for q in range(1, NCHIP):
            g = g + p_ref[q].astype(F32)
        d, nm, nv = _adamw_math(w_ref[...], g, m_ref[...], v_ref[...])
        g_ref[...], d_ref[...], nm_ref[...], nv_ref[...] = g, d, nm, nv

    spec = pl.BlockSpec((None, None, tr, cols), lambda i, p, t: (first + i, p, t, 0))
    na = 0 if outs is None else 4
    return pl.pallas_call(
        body, grid=(n, 2, rows // tr),
        in_specs=[spec, spec, spec,
                  pl.BlockSpec((None, NCHIP, None, tr, cols), lambda i, p, t: (p, 0, i, t0 + t, 0))] + [ANY] * na,
        out_specs=[spec] * 4, out_shape=[_sds(w.shape)] * 4, input_output_aliases={4 + k: k for k in range(na)},
        name=name, compiler_params=_cp("parallel", "parallel", "parallel"),
    )(w, m, v, parts, *(outs or ()))


def _adamw_small(name, w, m, v, parts):
    def body(w_ref, m_ref, v_ref, p_ref, g_ref, d_ref, nm_ref, nv_ref):
        g = p_ref[0]
        for q in range(1, NDEV):
            g = g + p_ref[q]
        d, nm, nv = _adamw_math(w_ref[...], g, m_ref[...], v_ref[...])
        g_ref[...], d_ref[...], nm_ref[...], nv_ref[...] = g, d, nm, nv

    return pl.pallas_call(body, out_shape=[_sds(w.shape)] * 4, name=name)(w, m, v, parts)


def _pack(arrs, rows):
    flat = jnp.concatenate([a.reshape(-1) for a in arrs])
    return jnp.pad(flat, (0, rows * LANES - flat.shape[0])).reshape(rows, LANES)


def _unpack(packed, shapes):
    flat, out, o = packed.reshape(-1), [], 0
    for s in shapes:
        n = 1
        for d in s:
            n *= d
        out.append(flat[o:o + n].reshape(s))
        o += n
    return out


SMALL_ROWS, REP_ROWS = 200, 16


def kernel(x, norm_w, ffn_w_gate, ffn_w_up, ffn_w_down, mix_w_in, dn_conv_w, attn_sinks, dn_a_log, dn_dt_bias, dn_norm_w, mix_w_out, conv_w_pw1, conv_b_pw1, conv_w_dw, conv_b_dw, conv_ln_w, conv_ln_b, conv_w_pw2, conv_b_pw2, final_norm_w, loss_target, m_norm_w, m_ffn_w_gate, m_ffn_w_up, m_ffn_w_down, m_mix_w_in, m_dn_conv_w, m_attn_sinks, m_dn_a_log, m_dn_dt_bias, m_dn_norm_w, m_mix_w_out, m_conv_w_pw1, m_conv_b_pw1, m_conv_w_dw, m_conv_b_dw, m_conv_ln_w, m_conv_ln_b, m_conv_w_pw2, m_conv_b_pw2, m_final_norm_w, v_norm_w, v_ffn_w_gate, v_ffn_w_up, v_ffn_w_down, v_mix_w_in, v_dn_conv_w, v_attn_sinks, v_dn_a_log, v_dn_dt_bias, v_dn_norm_w, v_mix_w_out, v_conv_w_pw1, v_conv_b_pw1, v_conv_w_dw, v_conv_b_dw, v_conv_ln_w, v_conv_ln_b, v_conv_w_pw2, v_conv_b_pw2, v_final_norm_w):
    small_names = ["norm_w", "dn_conv_w", "conv_b_pw1", "conv_w_dw", "conv_b_dw", "conv_ln_w", "conv_ln_b",
                   "conv_b_pw2"]
    rep_names = ["attn_sinks", "dn_a_log", "dn_dt_bias", "dn_norm_w", "final_norm_w"]
    w = dict(norm_w=norm_w, ffn_w_gate=ffn_w_gate, ffn_w_up=ffn_w_up, ffn_w_down=ffn_w_down, mix_w_in=mix_w_in, dn_conv_w=dn_conv_w, attn_sinks=attn_sinks, dn_a_log=dn_a_log, dn_dt_bias=dn_dt_bias, dn_norm_w=dn_norm_w, mix_w_out=mix_w_out, conv_w_pw1=conv_w_pw1, conv_b_pw1=conv_b_pw1, conv_w_dw=conv_w_dw, conv_b_dw=conv_b_dw, conv_ln_w=conv_ln_w, conv_ln_b=conv_ln_b, conv_w_pw2=conv_w_pw2, conv_b_pw2=conv_b_pw2, final_norm_w=final_norm_w)
    m = dict(norm_w=m_norm_w, ffn_w_gate=m_ffn_w_gate, ffn_w_up=m_ffn_w_up, ffn_w_down=m_ffn_w_down, mix_w_in=m_mix_w_in, dn_conv_w=m_dn_conv_w, attn_sinks=m_attn_sinks, dn_a_log=m_dn_a_log, dn_dt_bias=m_dn_dt_bias, dn_norm_w=m_dn_norm_w, mix_w_out=m_mix_w_out, conv_w_pw1=m_conv_w_pw1, conv_b_pw1=m_conv_b_pw1, conv_w_dw=m_conv_w_dw, conv_b_dw=m_conv_b_dw, conv_ln_w=m_conv_ln_w, conv_ln_b=m_conv_ln_b, conv_w_pw2=m_conv_w_pw2, conv_b_pw2=m_conv_b_pw2, final_norm_w=m_final_norm_w)
    v = dict(norm_w=v_norm_w, ffn_w_gate=v_ffn_w_gate, ffn_w_up=v_ffn_w_up, ffn_w_down=v_ffn_w_down, mix_w_in=v_mix_w_in, dn_conv_w=v_dn_conv_w, attn_sinks=v_attn_sinks, dn_a_log=v_dn_a_log, dn_dt_bias=v_dn_dt_bias, dn_norm_w=v_dn_norm_w, mix_w_out=v_mix_w_out, conv_w_pw1=v_conv_w_pw1, conv_b_pw1=v_conv_b_pw1, conv_w_dw=v_conv_w_dw, conv_b_dw=v_conv_b_dw, conv_ln_w=v_conv_ln_w, conv_ln_b=v_conv_ln_b, conv_w_pw2=v_conv_w_pw2, conv_b_pw2=v_conv_b_pw2, final_norm_w=v_final_norm_w)
    order = ["norm_w", "ffn_w_gate", "ffn_w_up", "ffn_w_down", "mix_w_in", "dn_conv_w", "attn_sinks", "dn_a_log",
             "dn_dt_bias", "dn_norm_w", "mix_w_out", "conv_w_pw1", "conv_b_pw1", "conv_w_dw", "conv_b_dw",
             "conv_ln_w", "conv_ln_b", "conv_w_pw2", "conv_b_pw2", "final_norm_w"]

    small_shapes = [w[n].shape for n in small_names]
    rep_shapes = [w[n].shape for n in rep_names]

    def halves(a):
        return a.reshape(a.shape[:-2] + (2, a.shape[-2] // 2, a.shape[-1]))

    tr = lambda a: jnp.swapaxes(a, -1, -2)
    gate_t, up_t = tr(ffn_w_gate), tr(ffn_w_up)

    def layer_shards(l):
        mix_in, mix_out = (mix_w_in, mix_w_out) if l % 2 == 0 else (conv_w_pw1, conv_w_pw2)
        return [t.astype(BF16) for t in (jnp.concatenate([gate_t[l], up_t[l], ffn_w_down[l]], axis=1),
                                         halves(mix_in[l // 2]), halves(mix_out[l // 2]))]

    first = layer_shards(0) + [_pack([w[n] for n in small_names], SMALL_ROWS)]
    first, (gate_t, up_t, ffn_w_down, mix_w_in, mix_w_out, conv_w_pw1, conv_w_pw2) = lax.optimization_barrier(
        (first, (gate_t, up_t, ffn_w_down, mix_w_in, mix_w_out, conv_w_pw1, conv_w_pw2)))
    gathering = [_gather_async("gather_layer0", first[:3], first[3:])]
    gathering += [_gather_async(f"gather_layer{l}", layer_shards(l)) for l in range(1, DEPTH)]

    def mixer_params(l, w_a, w_b):
        e = l // 2
        w_a = w_a.reshape(NCHIP, D, -1)
        w_b = w_b.reshape(D, D)
        if l % 2 == 0:
            return dict(w_in=w_a, dn_conv_w=sm["dn_conv_w"][e], sinks=_row(attn_sinks[e]), a_log=_row(dn_a_log[e]),
                        dt_bias=_row(dn_dt_bias[e]), dn_norm_w=_row(dn_norm_w[e]), wo_a=w_b[:Q_A], wo_b=w_b[Q_A:])
        return dict(b1a=_row(sm["conv_b_pw1"][e, :D]), b1b=_row(sm["conv_b_pw1"][e, D:]), w1=w_a,
                    w_dw=sm["conv_w_dw"][e], b_dw=_row(sm["conv_b_dw"][e]), ln_w=_row(sm["conv_ln_w"][e]),
                    ln_b=_row(sm["conv_ln_b"][e]), b2=_row(sm["conv_b_pw2"][e]), w2=w_b)

    xs, saved, ffn_w = x[0], [], []
    for l in range(DEPTH):
        got = [r[...] for r in gathering[l]]
        if l == 0:
            per_chip = [_unpack(got[3][q], small_shapes) for q in range(NCHIP)]
            sm = {n: jnp.concatenate([per_chip[q][i] for q in range(NCHIP)], axis=-1)
                  for i, n in enumerate(small_names)}
        else:
            xs, got = lax.optimization_barrier((xs, got))
        ffn_w.append(got[0])
        xs, sv = _layer_fwd(l, xs, sm["norm_w"][l], got[0], mixer_params(l, got[1], got[2]))
        saved.append(sv)
    loss, dx, dfw = _final("final", xs, _row(final_norm_w), loss_target[0])

    hbm = pltpu.MemorySpace.HBM
    row_shapes = dict(ffn=(3 * FS, D), w_in=(D // 2, IN_COLS // NCHIP), w_out=(D // 8, D), pw1=(D // 2, D // 2),
                      pw2=(D // 8, D))
    new_sums = lambda k, n: jax.empty_ref(_sds((2, NCHIP, n) + row_shapes[k], BF16), memory_space=hbm)
    sums_0 = {k: new_sums(k, 1) for k in ("ffn", "w_in", "w_out")}
    sums = dict(ffn=new_sums("ffn", DEPTH - 1), w_in=new_sums("w_in", 1), w_out=new_sums("w_out", 1),
                pw1=new_sums("pw1", 2), pw2=new_sums("pw2", 2))
    c_arr = lax.axis_index("c").astype(jnp.int32).reshape(1)
    dnorm, gmix = [None] * DEPTH, [None] * DEPTH

    def hand_on(l, grads, swapped):
        def run(dx):
            dx, other = lax.optimization_barrier((dx, [r[...] for r in swapped]))
            parts = [_add_half(f"add_half_{l}_{k}", gg, rr, c_arr) for k, (gg, rr) in enumerate(zip(grads, other))]
            dx, parts = lax.optimization_barrier((dx, parts))
            keys = ("ffn", "w_in", "w_out") if l % 2 == 0 else ("ffn", "pw1", "pw2")
            if l == 0:
                _scatter_async("scatter_grads_0", parts, [sums_0[k] for k in keys], [0, 0, 0])
            else:
                _scatter_async(f"scatter_grads_{l}", parts, [sums[k] for k in keys],
                               [l - 1, 0, 0] if l % 2 == 0 else [l - 1, l // 2, l // 2])
            return dx
        return run

    pending = lambda dx: dx
    for l in reversed(range(DEPTH)):
        dx, dnorm[l], dffn, gmix[l] = _layer_bwd(l, dx, sm["norm_w"][l], ffn_w[l], saved[l], pending)
        if l % 2 == 0:
            g_a, g_b = gmix[l]["w_in"], jnp.concatenate([gmix[l]["wo_a"], gmix[l]["wo_b"]], axis=0)
        else:
            g_a, g_b = gmix[l]["w1"], gmix[l]["w2"]
        g_a = halves(g_a).astype(BF16)
        g_b = g_b.reshape(NCHIP, 2, D // 8, D).astype(BF16)
        dx, grads = lax.optimization_barrier((dx, [dffn, g_a, g_b]))
        pending = hand_on(l, grads, _swap_halves(f"swap_grads_{l}", grads, sums["ffn"] if l < DEPTH - 1 else None))
    gm, gc = [gmix[0], gmix[2]], [gmix[1], gmix[3]]
    small_g = dict(
        norm_w=jnp.stack(dnorm), dn_conv_w=jnp.stack([gm[e]["dn_conv_w"] for e in range(2)]),
        conv_b_pw1=jnp.stack([jnp.concatenate([gc[e]["b1a"], gc[e]["b1b"]], axis=1)[0] for e in range(2)]),
        conv_w_dw=jnp.stack([gc[e]["w_dw"] for e in range(2)]),
        conv_b_dw=jnp.stack([gc[e]["b_dw"][0] for e in range(2)]),
        conv_ln_w=jnp.stack([gc[e]["ln_w"][0] for e in range(2)]),
        conv_ln_b=jnp.stack([gc[e]["ln_b"][0] for e in range(2)]),
        conv_b_pw2=jnp.stack([gc[e]["b2"][0] for e in range(2)]))
    small_by_chip = jnp.stack([_pack([jnp.split(small_g[n], NCHIP, axis=-1)[q] for n in small_names], SMALL_ROWS)
                               for q in range(NCHIP)])
    rep_g = _pack([jnp.stack([gm[e]["sinks"][0] for e in range(2)]), jnp.stack([gm[e]["a_log"][0] for e in range(2)]),
                   jnp.stack([gm[e]["dt_bias"][0] for e in range(2)]),
                   jnp.stack([gm[e]["dn_norm_w"][0] for e in range(2)]), dfw[0]], REP_ROWS)
    small_sum, rep_sum = _exchange_small(small_by_chip, rep_g)
    dx, small_sum, rep_sum = lax.optimization_barrier((dx, small_sum, rep_sum))
    dx = pending(dx)

    big = (("ffn_w_gate", "ffn", 0), ("ffn_w_up", "ffn", FS), ("ffn_w_down", "ffn", 2 * FS), ("mix_w_in", "w_in", 0),
           ("mix_w_out", "w_out", 0), ("conv_w_pw1", "pw1", 0), ("conv_w_pw2", "pw2", 0))
    views = {n: (tr, tr) if n in ("ffn_w_gate", "ffn_w_up") else (
        (lambda a: a) if w[n].ndim == 4 else halves, lambda o, n=n: o.reshape(w[n].shape)) for n, _, _ in big}
    partial_sums = {k: r[...] for k, r in sums.items()}
    upper = {}
    for n, key, row0 in big:
        view = views[n][0]
        upper[n] = _adamw_big(f"adamw_{n}", view(w[n]), view(m[n]), view(v[n]), partial_sums[key], row0,
                              first=0 if key in ("pw1", "pw2") else 1)
    upper, partial_sums_0 = lax.optimization_barrier((upper, {k: r[...] for k, r in sums_0.items()}))
    res = {}
    for n, key, row0 in big:
        view, back = views[n]
        outs = upper[n] if key not in partial_sums_0 else _adamw_big(
            f"adamw_{n}_0", view(w[n]), view(m[n]), view(v[n]), partial_sums_0[key], row0, first=0, outs=upper[n])
        res[n] = [back(o) for o in outs]
    outs = _adamw_small("adamw_small", *[_pack([d[n] for n in small_names], SMALL_ROWS) for d in (w, m, v)],
                        small_sum)
    for i, n in enumerate(small_names):
        res[n] = [_unpack(o, small_shapes)[i] for o in outs]
    outs = _adamw_small("adamw_replicated", *[_pack([d[n] for n in rep_names], REP_ROWS) for d in (w, m, v)],
                        rep_sum)
    for i, n in enumerate(rep_names):
        res[n] = [_unpack(o, rep_shapes)[i] for o in outs]

    total = lax.psum(loss[0, 0], ("x", "y", "c"))
    return (total, dx[None], *[res[n][0] for n in order], *[res[n][1] for n in order],
            *[res[n][2] for n in order], *[res[n][3] for n in order])
```

```python
import jax
import jax.numpy as jnp
from jax import lax
from jax.experimental import pallas as pl
from jax.experimental.pallas import tpu as pltpu
from jax.experimental.pallas import tpu_sc as plsc

F32, BF16 = jnp.float32, jnp.bfloat16
MESH = pl.DeviceIdType.MESH
ANY = pl.BlockSpec(memory_space=pl.ANY)

T, D, F = 2048, 1024, 2816
DEPTH = 4
EPS = 1e-6
HEADS, HDIM, KV_HEADS, GROUP = 8, 64, 2, 4
WINDOW = BLOCK = 128
CHUNK = 64
NCHUNK = T // CHUNK
DN_CONV, CONV_WIDTH = 4, 31
Q_A, KV_A, QKV_B, V_B = 512, 128, 1536, 512
IN_COLS = 2832
IN_SPLITS = (0, 512, 640, 768, 2304, 2816, 2832)
NCHIP, NDEV = 4, 8
FS = F // NCHIP
LR, B1, B2, AEPS, WD, STEP = 0.001, 0.9, 0.999, 1e-08, 0.01, 10
V7X_VMEM_BYTES = 64 * 1024 * 1024
VMEM_LIMIT = V7X_VMEM_BYTES * 7 // 8
LANES = 128


def _cp(*sem):
    return pltpu.CompilerParams(dimension_semantics=sem, vmem_limit_bytes=VMEM_LIMIT)


def _sds(shape, dtype=F32):
    return jax.ShapeDtypeStruct(tuple(shape), dtype)


def _full(shape):
    nd = len(shape)
    return pl.BlockSpec(tuple(shape), lambda *_: (0,) * nd)


def _split_bf16(a):
    hi = a.astype(BF16)
    return hi, (a - hi.astype(F32)).astype(BF16)


def _dg(a, b, ca, cb, hi=False):
    if a.ndim == 3 and b.ndim == 3:
        dims = (((ca + 1,), (cb + 1,)), ((0,), (0,)))
    else:
        dims = (((ca,), (cb,)), ((), ()))
    dot = lambda p, q: lax.dot_general(p, q, dims, preferred_element_type=F32)
    if hi:
        a_hi, a_lo = _split_bf16(a.astype(F32))
        b_hi, b_lo = _split_bf16(b.astype(F32))
        return dot(a_hi, b_hi) + (dot(a_hi, b_lo) + dot(a_lo, b_hi))
    return dot(a.astype(BF16), b.astype(BF16))


def _make_mm(hi):
    @jax.custom_vjp
    def nn(a, b):
        return _dg(a, b, 1, 0, hi)

    @jax.custom_vjp
    def nt(a, b):
        return _dg(a, b, 1, 1, hi)

    @jax.custom_vjp
    def tn(a, b):
        return _dg(a, b, 0, 0, hi)

    nn.defvjp(lambda a, b: (_dg(a, b, 1, 0, hi), (a, b)),
              lambda r, g: (_dg(g, r[1], 1, 1, hi).astype(r[0].dtype), _dg(r[0], g, 0, 0, hi).astype(r[1].dtype)))
    nt.defvjp(lambda a, b: (_dg(a, b, 1, 1, hi), (a, b)),
              lambda r, g: (_dg(g, r[1], 1, 0, hi).astype(r[0].dtype), _dg(g, r[0], 0, 0, hi).astype(r[1].dtype)))
    tn.defvjp(lambda a, b: (_dg(a, b, 0, 0, hi), (a, b)),
              lambda r, g: (_dg(r[1], g, 1, 1, hi).astype(r[0].dtype), _dg(r[0], g, 1, 0, hi).astype(r[1].dtype)))
    return nn, nt, tn


_nn, _nt, _tn = _make_mm(False)
_nn_hi, _nt_hi, _tn_hi = _make_mm(True)


def _rms(x, w):
    return x * lax.rsqrt(jnp.mean(x * x, axis=-1, keepdims=True) + EPS) * w


def _layernorm(x, w, b):
    xc = x - jnp.mean(x, axis=-1, keepdims=True)
    return xc * lax.rsqrt(jnp.mean(xc * xc, axis=-1, keepdims=True) + EPS) * w + b


def _silu(x):
    return x * jax.nn.sigmoid(x)


def _iota2(shape, dim):
    return lax.broadcasted_iota(jnp.int32, shape, dim)


def _flat_weights(lhs_idx, weights):
    specs, ops, lhs_of, where = [], [], [], []
    for a, (k, w) in enumerate(zip(lhs_idx, weights)):
        for q in range(1 if w.ndim == 2 else w.shape[0]):
            specs.append(_full(w.shape) if w.ndim == 2
                         else pl.BlockSpec((None,) + w.shape[1:], lambda i, q=q: (q, 0, 0)))
            ops.append(w)
            lhs_of.append(k)
            where.append((a, None if w.ndim == 2 else q))
    return specs, ops, lhs_of, where


def _blk_fwd(name, pre, lhs_idx, post, toks, smalls, weights, outs, tm=512):
    wspecs, wops, lhs_of, _ = _flat_weights(lhs_idx, weights)
    nt_, ns, nw = len(toks), len(smalls), len(wops)

    def body(*refs):
        tv = [r[...] for r in refs[:nt_]]
        sv = [r[...] for r in refs[nt_:nt_ + ns]]
        wr = refs[nt_ + ns:nt_ + ns + nw]
        orf = refs[nt_ + ns + nw:]
        lhs = pre(tv, sv)
        ys = [_dg(lhs[i], w[...], 1, 0) for i, w in zip(lhs_of, wr)]
        for o_ref, o in zip(orf, post(ys, tv, sv)):
            o_ref[...] = o.astype(o_ref.dtype)

    in_specs = ([pl.BlockSpec((tm, a.shape[1]), lambda i: (i, 0)) for a in toks]
                + [_full(a.shape) for a in smalls] + wspecs)
    out_specs = [pl.BlockSpec((tm, w_), lambda i: (i, 0)) for w_, _ in outs]
    return pl.pallas_call(
        body, grid=(T // tm,), in_specs=in_specs, out_specs=out_specs,
        out_shape=[_sds((T, w_), dt) for w_, dt in outs], name=name, compiler_params=_cp("parallel"),
    )(*toks, *smalls, *wops)


def _blk_bwd(name, pre, lhs_idx, post, toks, smalls, weights, ct_groups, res=None, tm=256, wchunk=512):
    wspecs, wops, lhs_of, where = _flat_weights(lhs_idx, weights)
    nt_, ns, nw, na = len(toks), len(smalls), len(wops), len(weights)
    cts = [a for g in ct_groups for a in g]
    nc = len(cts)
    widths = [sum(a.shape[1] for a in g) for g in ct_groups]
    has_res = res is not None

    def body(*refs):
        p = 0
        tr = refs[p:p + nt_]; p += nt_
        sr = refs[p:p + ns]; p += ns
        wr = refs[p:p + nw]; p += nw
        cr = refs[p:p + nc]; p += nc
        rr = refs[p:p + has_res]; p += has_res
        dtr = refs[p:p + nt_]; p += nt_
        dsr = refs[p:p + ns]; p += ns
        dwr = refs[p:p + na]; p += na
        scr = refs[p:]
        i = pl.program_id(0)

        @pl.when(i == 0)
        def _():
            for r in list(dsr) + list(dwr):
                r[...] = jnp.zeros_like(r)

        tv = [r[...] for r in tr]
        sv = [r[...] for r in sr]
        ctv, q, si = [], 0, 0
        for g in ct_groups:
            if len(g) == 1:
                ctv.append(cr[q][...].astype(F32))
            else:
                off = 0
                for j, a in enumerate(g):
                    scr[si][:, off:off + a.shape[1]] = cr[q + j][...].astype(F32)
                    off += a.shape[1]
                ctv.append(scr[si][...])
                si += 1
            q += len(g)

        lhs, vjp_pre = jax.vjp(lambda *a: tuple(pre(list(a[:nt_]), list(a[nt_:]))), *tv, *sv)
        lhs_b = [l.astype(BF16) for l in lhs]
        ys = [_dg(lhs_b[k], w[...], 1, 0) for k, w in zip(lhs_of, wr)]
        _, vjp_post = jax.vjp(lambda *a: tuple(post(list(a[:nw]), list(a[nw:nw + nt_]), list(a[nw + nt_:]))),
                              *ys, *tv, *sv)
        gp = vjp_post(tuple(ctv))
        dys, dt_post, ds_post = gp[:nw], gp[nw:nw + nt_], gp[nw + nt_:]
        dlhs = [None] * len(lhs)
        for k, w, dy, (a, q) in zip(lhs_of, wr, dys, where):
            dyb = dy.astype(BF16)
            n = w.shape[1]
            for c0 in range(0, n, wchunk):
                c1 = min(n, c0 + wchunk)
                part = _dg(lhs_b[k], dyb[:, c0:c1], 0, 0)
                if q is None:
                    dwr[a][:, c0:c1] += part
                else:
                    dwr[a][q, :, c0:c1] += part
            d = _dg(dyb, w[...], 1, 1)
            dlhs[k] = d if dlhs[k] is None else dlhs[k] + d
        gq = vjp_pre(tuple(d.astype(l.dtype) for d, l in zip(dlhs, lhs)))
        dt_pre, ds_pre = gq[:nt_], gq[nt_:]
        for j in range(nt_):
            d = dt_post[j] + dt_pre[j]
            if j == 0 and has_res:
                d = d + rr[0][...]
            dtr[j][...] = d
        for j in range(ns):
            dsr[j][...] += ds_post[j] + ds_pre[j]

    tok_spec = lambda a: pl.BlockSpec((tm, a.shape[1]), lambda i: (i, 0))
    in_specs = ([tok_spec(a) for a in toks] + [_full(a.shape) for a in smalls] + wspecs
                + [tok_spec(a) for a in cts] + ([tok_spec(res)] if has_res else []))
    out_specs = [tok_spec(a) for a in toks] + [_full(a.shape) for a in smalls] + [_full(w.shape) for w in weights]
    out_shape = ([_sds(a.shape) for a in toks] + [_sds(a.shape) for a in smalls] + [_sds(w.shape) for w in weights])
    scratch = [pltpu.VMEM((tm, wd), F32) for g, wd in zip(ct_groups, widths) if len(g) > 1]
    outs = pl.pallas_call(
        body, grid=(T // tm,), in_specs=in_specs, out_specs=out_specs, out_shape=out_shape,
        scratch_shapes=scratch, name=name, compiler_params=_cp("arbitrary"),
    )(*toks, *smalls, *wops, *cts, *([res] if has_res else []))
    return outs[:nt_], outs[nt_:nt_ + ns], outs[nt_ + ns:]


def _ffn_fwd(name, x, nw, ffn, idx, tm=512):
    def body(x_ref, nw_ref, wg_ref, wu_ref, wd_ref, o_ref, a_ref, b_ref, h_scr):
        s = pl.program_id(1)

        @pl.when(s == 0)
        def _():
            xv = x_ref[...]
            h_scr[...] = _rms(xv, nw_ref[...]).astype(BF16)
            o_ref[...] = xv

        h = h_scr[...]
        a = _dg(h, wg_ref[...], 1, 1).astype(BF16)
        b = _dg(h, wu_ref[...], 1, 1).astype(BF16)
        a_ref[...] = a
        b_ref[...] = b
        o_ref[...] += 0.5 * _dg(_swiglu_act(a, b)[0], wd_ref[...], 1, 0)

    wspec = lambda k: pl.BlockSpec((None, None, FS, D), lambda i, s: (s, idx, k, 0))
    act = pl.BlockSpec((None, tm, FS), lambda i, s: (s, i, 0))
    return pl.pallas_call(
        body, grid=(T // tm, NCHIP),
        in_specs=[pl.BlockSpec((tm, D), lambda i, s: (i, 0)), _full((1, D)), wspec(0), wspec(1), wspec(2)],
        out_specs=[pl.BlockSpec((tm, D), lambda i, s: (i, 0)), act, act],
        out_shape=[_sds((T, D)), _sds((NCHIP, T, FS), BF16), _sds((NCHIP, T, FS), BF16)],
        scratch_shapes=[pltpu.VMEM((tm, D), BF16)], name=name, compiler_params=_cp("parallel", "arbitrary"),
    )(x, nw, ffn, ffn, ffn)


def _swiglu_act(a, b):
    a, b = a.astype(F32), b.astype(F32)
    sa = jax.nn.sigmoid(a)
    act = a * sa
    return act * b, a, b, sa, act


def _ffn_bwd(name, x, nw, ffn, idx, pre, dy, gbuf=None, tm=512):
    ni = T // tm

    def body(x_ref, dy_ref, nw_ref, wg_ref, wu_ref, wd_ref, a_ref, b_ref, dx_ref, dnw_ref, dffn_ref, dh_acc, ag, au,
             ad):
        s, i = pl.program_id(0), pl.program_id(1)
        rows = pl.ds(pl.multiple_of(i * tm, tm), tm)

        @pl.when((s == 0) & (i == 0))
        def _():
            dnw_ref[...] = jnp.zeros_like(dnw_ref)

        @pl.when(i == 0)
        def _():
            ag[...] = jnp.zeros_like(ag)
            au[...] = jnp.zeros_like(au)
            ad[...] = jnp.zeros_like(ad)

        xv, nwv, dyv = x_ref[...], nw_ref[...], dy_ref[...]
        h, vjp_rms = jax.vjp(_rms, xv, nwv)
        hb = h.astype(BF16)
        gated, a, b, sa, act = _swiglu_act(a_ref[...], b_ref[...])
        dyb = (0.5 * dyv).astype(BF16)
        ad[...] += _dg(gated, dyb, 0, 0)
        dact = _dg(dyb, wd_ref[...], 1, 1)
        da = (dact * b * (sa * (1.0 + a * (1.0 - sa)))).astype(BF16)
        db = (dact * act).astype(BF16)
        ag[...] += _dg(da, hb, 0, 0)
        au[...] += _dg(db, hb, 0, 0)
        dh = _dg(da, wg_ref[...], 1, 0) + _dg(db, wu_ref[...], 1, 0)

        @pl.when(s == 0)
        def _():
            dh_acc[rows, :] = dh

        @pl.when(s > 0)
        def _():
            dh_acc[rows, :] += dh

        @pl.when(s == NCHIP - 1)
        def _():
            dx, dnw = vjp_rms(dh_acc[rows, :])
            dx_ref[...] = dyv + dx
            dnw_ref[...] += dnw

        @pl.when(i == ni - 1)
        def _():
            dffn_ref[0:FS, :] = ag[...].astype(BF16)
            dffn_ref[FS:2 * FS, :] = au[...].astype(BF16)
            dffn_ref[2 * FS:, :] = ad[...].astype(BF16)

    wspec = lambda r, k: pl.BlockSpec((None, None, r, D), lambda s, i: (s, idx, k, 0), pipeline_mode=pl.Buffered(1))
    last = lambda s, i: (jnp.where(s == NCHIP - 1, i, 0), 0)
    nb = 0 if gbuf is None else 1
    act = pl.BlockSpec((None, tm, FS), lambda s, i: (s, i, 0))
    return pl.pallas_call(
        lambda *refs: body(*refs[:8], *refs[8 + nb:]), grid=(NCHIP, ni),
        in_specs=[pl.BlockSpec((tm, D), lambda s, i: (i, 0)), pl.BlockSpec((tm, D), lambda s, i: (i, 0)),
                  _full((1, D)), wspec(FS, 0), wspec(FS, 1), wspec(FS, 2), act, act] + [ANY] * nb,
        out_specs=[pl.BlockSpec((tm, D), last), _full((1, D)), wspec(3 * FS, 0)],
        out_shape=[_sds((T, D)), _sds((1, D)), _sds(ffn.shape, BF16)],
        input_output_aliases={8 + k: 2 + k for k in range(nb)},
        scratch_shapes=[pltpu.VMEM((T, D), F32)] + [pltpu.VMEM((FS, D), F32)] * 3,
        name=name, compiler_params=_cp("arbitrary", "arbitrary"),
    )(x, dy, nw, ffn, ffn, ffn, *pre, *(() if gbuf is None else (gbuf,)))


CONV_ROWS = 256


def _conv_pad(k):
    return 8 * ((k - 1 + 7) // 8)


def _conv_fwd(name, x, w, b, act):
    k_w, c = w.shape
    tc = 256 if c % 256 == 0 else LANES
    pad = _conv_pad(k_w)
    has_b = b is not None

    def body(*refs):
        x_ref, w_ref = refs[0], refs[1]
        b_ref = refs[2] if has_b else None
        y_ref, xp = refs[2 + has_b], refs[3 + has_b]
        xp[0:pad, :] = jnp.zeros((pad, tc), F32)
        xp[pad:, :] = x_ref[...]

        def step(t, carry):
            base = pl.multiple_of(t * CONV_ROWS, CONV_ROWS)
            win = xp[pl.ds(base, CONV_ROWS + pad), :]
            acc = jnp.zeros((CONV_ROWS, tc), F32)
            for k in range(k_w):
                o = pad - (k_w - 1) + k
                acc = acc + w_ref[k:k + 1, :] * win[o:o + CONV_ROWS, :]
            if has_b:
                acc = acc + b_ref[...]
            y_ref[pl.ds(base, CONV_ROWS), :] = _silu(acc) if act else acc
            return carry

        lax.fori_loop(0, T // CONV_ROWS, step, 0)

    col = lambda r: pl.BlockSpec((r, tc), lambda j: (0, j))
    ins = [x, w] + ([b] if has_b else [])
    return pl.pallas_call(
        body, grid=(c // tc,), in_specs=[col(T), col(k_w)] + ([col(1)] if has_b else []), out_specs=col(T),
        out_shape=_sds((T, c)), scratch_shapes=[pltpu.VMEM((T + pad, tc), F32)], name=name,
        compiler_params=_cp("parallel"),
    )(*ins)


def _conv_bwd(name, x, w, b, act, dy):
    k_w, c = w.shape
    tc = 256 if c % 256 == 0 else LANES
    pad = _conv_pad(k_w)
    has_b = b is not None

    def body(*refs):
        x_ref, w_ref, dy_ref = refs[0], refs[1], refs[2]
        b_ref = refs[3] if has_b else None
        dx_ref, dw_ref, db_ref, xp, dp = refs[3 + has_b:]
        xp[0:pad, :] = jnp.zeros((pad, tc), F32)
        xp[pad:, :] = x_ref[...]
        dp[T:, :] = jnp.zeros((pad, tc), F32)
        dw_ref[...] = jnp.zeros_like(dw_ref)
        db_ref[...] = jnp.zeros_like(db_ref)

        def step1(t, carry):
            base = pl.multiple_of(t * CONV_ROWS, CONV_ROWS)
            d = dy_ref[pl.ds(base, CONV_ROWS), :]
            win = xp[pl.ds(base, CONV_ROWS + pad), :]
            offs = [pad - (k_w - 1) + k for k in range(k_w)]
            if act:
                acc = jnp.zeros((CONV_ROWS, tc), F32)
                for k, o in enumerate(offs):
                    acc = acc + w_ref[k:k + 1, :] * win[o:o + CONV_ROWS, :]
                if has_b:
                    acc = acc + b_ref[...]
                sg = jax.nn.sigmoid(acc)
                d = d * (sg * (1.0 + acc * (1.0 - sg)))
            dp[pl.ds(base, CONV_ROWS), :] = d
            for k, o in enumerate(offs):
                dw_ref[k:k + 1, :] += jnp.sum(d * win[o:o + CONV_ROWS, :], axis=0, keepdims=True)
            db_ref[...] += jnp.sum(d, axis=0, keepdims=True)
            return carry

        lax.fori_loop(0, T // CONV_ROWS, step1, 0)

        def step2(t, carry):
            base = pl.multiple_of(t * CONV_ROWS, CONV_ROWS)
            win = dp[pl.ds(base, CONV_ROWS + pad), :]
            acc = jnp.zeros((CONV_ROWS, tc), F32)
            for k in range(k_w):
                o = (k_w - 1) - k
                acc = acc + w_ref[k:k + 1, :] * win[o:o + CONV_ROWS, :]
            dx_ref[pl.ds(base, CONV_ROWS), :] = acc
            return carry

        lax.fori_loop(0, T // CONV_ROWS, step2, 0)

    col = lambda r: pl.BlockSpec((r, tc), lambda j: (0, j))
    ins = [x, w, dy] + ([b] if has_b else [])
    return pl.pallas_call(
        body, grid=(c // tc,), in_specs=[col(T), col(k_w), col(T)] + ([col(1)] if has_b else []),
        out_specs=[col(T), col(k_w), col(1)], out_shape=[_sds((T, c)), _sds((k_w, c)), _sds((1, c))],
        scratch_shapes=[pltpu.VMEM((T + pad, tc), F32), pltpu.VMEM((T + pad, tc), F32)], name=name,
        compiler_params=_cp("parallel"),
    )(*ins)


def _attn_consts(n):
    i = _iota2((BLOCK, 2 * BLOCK), 0)
    j = _iota2((BLOCK, 2 * BLOCK), 1)
    dist = i + BLOCK - j
    valid = (dist >= 0) & (dist < WINDOW) & ((n > 0) | (j >= BLOCK))
    return dist.astype(F32), valid


def _attn_block(q4, kk, vv, sinks, dist, valid, kv):
    outs = []
    lane = _iota2((1, HEADS), 1)
    for g in range(GROUP):
        h = kv * GROUP + g
        slope = 2.0 ** (-8.0 * (h + 1) / HEADS)
        s = _nt(q4[:, g * HDIM:(g + 1) * HDIM], kk) * (HDIM ** -0.5)
        s = jnp.where(valid, s - slope * dist, -1e30)
        sink = jnp.sum(jnp.where(lane == h, sinks, 0.0), axis=1, keepdims=True)
        m = jnp.maximum(jnp.max(s, axis=-1, keepdims=True), sink)
        e = jnp.exp(s - m)
        p = e / (jnp.sum(e, axis=-1, keepdims=True) + jnp.exp(sink - m))
        outs.append(_nn(p, vv))
    return tuple(outs)


def _attn_fwd(name, qa, ka, va, sinks):
    def body(q_ref, k_ref, v_ref, s_ref, o_ref, kp, vp):
        kp[0:BLOCK, :] = jnp.zeros((BLOCK, KV_A), F32)
        vp[0:BLOCK, :] = jnp.zeros((BLOCK, KV_A), F32)
        kp[BLOCK:, :] = k_ref[...]
        vp[BLOCK:, :] = v_ref[...]
        sinks_v = s_ref[...]

        def step(n, carry):
            r = pl.multiple_of(n * BLOCK, BLOCK)
            dist, valid = _attn_consts(n)
            k2 = kp[pl.ds(r, 2 * BLOCK), :]
            v2 = vp[pl.ds(r, 2 * BLOCK), :]
            for kv in range(KV_HEADS):
                q4 = q_ref[pl.ds(r, BLOCK), kv * GROUP * HDIM:(kv + 1) * GROUP * HDIM]
                og = _attn_block(q4, k2[:, kv * HDIM:(kv + 1) * HDIM], v2[:, kv * HDIM:(kv + 1) * HDIM], sinks_v,
                                 dist, valid, kv)
                for g in range(GROUP):
                    h = kv * GROUP + g
                    o_ref[pl.ds(r, BLOCK), h * HDIM:(h + 1) * HDIM] = og[g]
            return carry

        lax.fori_loop(0, T // BLOCK, step, 0)

    return pl.pallas_call(
        body, out_shape=_sds((T, Q_A)),
        scratch_shapes=[pltpu.VMEM((T + BLOCK, KV_A), F32), pltpu.VMEM((T + BLOCK, KV_A), F32)], name=name,
        compiler_params=pltpu.CompilerParams(vmem_limit_bytes=VMEM_LIMIT),
    )(qa, ka, va, sinks)


def _attn_bwd(name, qa, ka, va, sinks, do):
    def body(q_ref, k_ref, v_ref, s_ref, do_ref, dq_ref, dk_ref, dv_ref, ds_ref, kp, vp, dkp, dvp):
        kp[0:BLOCK, :] = jnp.zeros((BLOCK, KV_A), F32)
        vp[0:BLOCK, :] = jnp.zeros((BLOCK, KV_A), F32)
        kp[BLOCK:, :] = k_ref[...]
        vp[BLOCK:, :] = v_ref[...]
        dkp[...] = jnp.zeros_like(dkp)
        dvp[...] = jnp.zeros_like(dvp)
        ds_ref[...] = jnp.zeros_like(ds_ref)
        sinks_v = s_ref[...]

        def step(n, carry):
            r = pl.multiple_of(n * BLOCK, BLOCK)
            dist, valid = _attn_consts(n)
            k2 = kp[pl.ds(r, 2 * BLOCK), :]
            v2 = vp[pl.ds(r, 2 * BLOCK), :]
            for kv in range(KV_HEADS):
                cols = slice(kv * HDIM, (kv + 1) * HDIM)
                q4 = q_ref[pl.ds(r, BLOCK), kv * GROUP * HDIM:(kv + 1) * GROUP * HDIM]
                _, vjp = jax.vjp(lambda q, k, v, s: _attn_block(q, k, v, s, dist, valid, kv),
                                 q4, k2[:, cols], v2[:, cols], sinks_v)
                cts = tuple(do_ref[pl.ds(r, BLOCK), (kv * GROUP + g) * HDIM:(kv * GROUP + g + 1) * HDIM]
                            for g in range(GROUP))
                dq4, dkk, dvv, dsk = vjp(cts)
                dq_ref[pl.ds(r, BLOCK), kv * GROUP * HDIM:(kv + 1) * GROUP * HDIM] = dq4
                dkp[pl.ds(r, 2 * BLOCK), cols] += dkk
                dvp[pl.ds(r, 2 * BLOCK), cols] += dvv
                ds_ref[...] += dsk
            return carry

        lax.fori_loop(0, T // BLOCK, step, 0)
        dk_ref[...] = dkp[BLOCK:, :]
        dv_ref[...] = dvp[BLOCK:, :]

    pad = lambda: pltpu.VMEM((T + BLOCK, KV_A), F32)
    return pl.pallas_call(
        body, out_shape=[_sds((T, Q_A)), _sds((T, KV_A)), _sds((T, KV_A)), _sds((1, HEADS))],
        scratch_shapes=[pad(), pad(), pad(), pad()], name=name,
        compiler_params=pltpu.CompilerParams(vmem_limit_bytes=VMEM_LIMIT),
    )(qa, ka, va, sinks, do)


def _dn_consts():
    i = _iota2((CHUNK, CHUNK), 0)
    j = _iota2((CHUNK, CHUNK), 1)
    return dict(causal=i >= j, strict=i > j, eye=(i == j).astype(F32), ltri=(i >= j).astype(F32),
                ones=jnp.ones((CHUNK, CHUNK), F32), last=(_iota2((CHUNK, 1), 0) == CHUNK - 1).astype(F32))


def _l2norm(x):
    return x * lax.rsqrt(jnp.sum(x * x, axis=-1, keepdims=True) + EPS)


def _head_cols(m):
    lane = _iota2((1, HEADS), 1)
    return jnp.concatenate([jnp.sum(jnp.where(lane == h, m, 0.0), axis=1, keepdims=True)[None]
                            for h in range(HEADS)], axis=0)


@jax.custom_vjp
def _unit_lower_inverse(low, known):
    if known is not None:
        return known
    inv = (_iota2((CHUNK, CHUNK), 0) == _iota2((CHUNK, CHUNK), 1)).astype(F32) - low
    pw = low
    for _ in range(5):
        pw = _dg(pw, pw, 1, 0, True)
        inv = inv + _dg(inv, pw, 1, 0, True)
    return inv


def _unit_lower_inverse_fwd(low, known):
    inv = _unit_lower_inverse(low, known)
    return inv, (inv, known)


def _unit_lower_inverse_bwd(res, g):
    inv, known = res
    d_low = -_dg(inv, _dg(g, inv, 1, 1, True), 0, 0, True)
    return d_low, (None if known is None else jnp.zeros_like(known))


_unit_lower_inverse.defvjp(_unit_lower_inverse_fwd, _unit_lower_inverse_bwd)


def _dn_local(q3, k3, v3, braw, araw, alog, dtb, cs, known_inv=None):
    q = _l2norm(q3) * (HDIM ** -0.5)
    k = _l2norm(k3)
    g = -jnp.exp(alog) * jax.nn.softplus(araw + dtb)
    gc_all = _nn_hi(cs["ltri"], g)
    egc_all = jnp.exp(gc_all)
    beta, gc, egc = _head_cols(jax.nn.sigmoid(braw)), _head_cols(gc_all), _head_cols(egc_all)
    a = jnp.broadcast_to(gc, (HEADS, CHUNK, CHUNK))
    diff = a - jnp.swapaxes(a, 1, 2)
    decay = jnp.where(cs["causal"], jnp.exp(jnp.where(cs["causal"], diff, 0.0)), 0.0)
    kb = k * beta
    low = jnp.where(cs["strict"], _nt(kb, k) * decay, 0.0)
    inv = _unit_lower_inverse(low, known_inv)
    u = _nn_hi(inv, v3 * beta)
    w = _nn_hi(inv, kb * egc)
    attn = _nt(q, k) * decay
    gc_last = jnp.sum(gc * cs["last"], axis=1, keepdims=True)
    return u, w, attn, q * egc, k * jnp.exp(gc_last - gc), egc_all, inv


def _heads3(ref, off=0):
    return jnp.concatenate([ref[:, off + h * HDIM:off + (h + 1) * HDIM][None] for h in range(HEADS)], axis=0)


def _dn_local_fwd(name, qkv, ba, alog, dtb):
    def body(qkv_ref, ba_ref, al_ref, dt_ref, u_ref, w_ref, at_ref, qd_ref, kd_ref, eg_ref, inv_ref):
        bav = ba_ref[...]
        outs = _dn_local(_heads3(qkv_ref), _heads3(qkv_ref, 512), _heads3(qkv_ref, 1024), bav[:, :HEADS],
                         bav[:, HEADS:], al_ref[...], dt_ref[...], _dn_consts())
        for r, o in zip((u_ref, w_ref, at_ref, qd_ref, kd_ref, inv_ref), outs[:5] + outs[6:]):
            _unheads(r, o)
        eg_ref[...] = outs[5]

    row = lambda w_: pl.BlockSpec((CHUNK, w_), lambda n: (n, 0))
    return pl.pallas_call(
        body, grid=(NCHUNK,), in_specs=[row(QKV_B), row(2 * HEADS), _full((1, HEADS)), _full((1, HEADS))],
        out_specs=[row(V_B)] * 5 + [row(HEADS), row(V_B)],
        out_shape=[_sds((T, V_B))] * 5 + [_sds((T, HEADS)), _sds((T, V_B))], name=name,
        compiler_params=_cp("parallel"),
    )(qkv, ba, alog, dtb)


def _dn_local_bwd(name, qkv, ba, alog, dtb, inv, cts):
    def body(qkv_ref, ba_ref, al_ref, dt_ref, inv_ref, du_ref, dw_ref, dat_ref, dqd_ref, dkd_ref, deg_ref,
             dqkv_ref, dba_ref, dal_ref, ddt_ref):
        @pl.when(pl.program_id(0) == 0)
        def _():
            dal_ref[...] = jnp.zeros_like(dal_ref)
            ddt_ref[...] = jnp.zeros_like(ddt_ref)

        cs = _dn_consts()
        bav = ba_ref[...]
        known = _heads3(inv_ref)
        _, vjp = jax.vjp(lambda *a: _dn_local(*a, cs, known)[:6], _heads3(qkv_ref), _heads3(qkv_ref, 512),
                         _heads3(qkv_ref, 1024), bav[:, :HEADS], bav[:, HEADS:], al_ref[...], dt_ref[...])
        dq, dk, dv, dbr, dar, dal, ddt = vjp((_heads3(du_ref), _heads3(dw_ref), _heads3(dat_ref), _heads3(dqd_ref),
                                              _heads3(dkd_ref), deg_ref[...]))
        for h in range(HEADS):
            dqkv_ref[:, h * HDIM:(h + 1) * HDIM] = dq[h]
            dqkv_ref[:, 512 + h * HDIM:512 + (h + 1) * HDIM] = dk[h]
            dqkv_ref[:, 1024 + h * HDIM:1024 + (h + 1) * HDIM] = dv[h]
        dba_ref[:, :HEADS] = dbr
        dba_ref[:, HEADS:] = dar
        dal_ref[...] += dal
        ddt_ref[...] += ddt

    row = lambda w_: pl.BlockSpec((CHUNK, w_), lambda n: (n, 0))
    return pl.pallas_call(
        body, grid=(NCHUNK,),
        in_specs=[row(QKV_B), row(2 * HEADS), _full((1, HEADS)), _full((1, HEADS))] + [row(V_B)] * 6 + [row(HEADS)],
        out_specs=[row(QKV_B), row(2 * HEADS), _full((1, HEADS)), _full((1, HEADS))],
        out_shape=[_sds((T, QKV_B)), _sds((T, 2 * HEADS)), _sds((1, HEADS)), _sds((1, HEADS))], name=name,
        compiler_params=_cp("arbitrary"),
    )(qkv, ba, alog, dtb, inv, *cts)


def _dn_step(s, u, w, attn, qd, kd, egc, z, nw):
    last = (_iota2((CHUNK, 1), 0) == CHUNK - 1).astype(F32)
    gl = jnp.sum(_head_cols(egc) * last, axis=1, keepdims=True)
    v_new = u - _nn(w, s)
    o = _nn(qd, s) + _nn(attn, v_new)
    s_new = s * gl + _tn(kd, v_new)
    return s_new, _rms(o, nw) * _silu(z)


def _unheads(ref, v3):
    for h in range(HEADS):
        ref[:, h * HDIM:(h + 1) * HDIM] = v3[h]


def _dn_rec_fwd(name, u, w, attn, qd, kd, egc, z, nw):
    def body(u_ref, w_ref, at_ref, qd_ref, kd_ref, eg_ref, z_ref, nw_ref, o_ref, ss_ref, s_scr):
        @pl.when(pl.program_id(0) == 0)
        def _():
            s_scr[...] = jnp.zeros_like(s_scr)

        s = s_scr[...]
        ss_ref[...] = s
        s_new, on = _dn_step(s, _heads3(u_ref), _heads3(w_ref), _heads3(at_ref), _heads3(qd_ref), _heads3(kd_ref),
                             eg_ref[...], _heads3(z_ref), nw_ref[...])
        s_scr[...] = s_new
        _unheads(o_ref, on)

    row = lambda w_: pl.BlockSpec((CHUNK, w_), lambda n: (n, 0))
    return pl.pallas_call(
        body, grid=(NCHUNK,), in_specs=[row(V_B)] * 5 + [row(HEADS), row(V_B), _full((1, HDIM))],
        out_specs=[row(V_B), pl.BlockSpec((None, HEADS, HDIM, HDIM), lambda n: (n, 0, 0, 0))],
        out_shape=[_sds((T, V_B)), _sds((NCHUNK, HEADS, HDIM, HDIM))],
        scratch_shapes=[pltpu.VMEM((HEADS, HDIM, HDIM), F32)], name=name, compiler_params=_cp("arbitrary"),
    )(u, w, attn, qd, kd, egc, z, nw)


def _dn_rec_bwd(name, u, w, attn, qd, kd, egc, z, nw, ss, do):
    def body(u_ref, w_ref, at_ref, qd_ref, kd_ref, eg_ref, z_ref, nw_ref, ss_ref, do_ref,
             du_ref, dw_ref, dat_ref, dqd_ref, dkd_ref, deg_ref, dz_ref, dnw_ref, ds_scr):
        @pl.when(pl.program_id(0) == 0)
        def _():
            ds_scr[...] = jnp.zeros_like(ds_scr)
            dnw_ref[...] = jnp.zeros_like(dnw_ref)

        _, vjp = jax.vjp(_dn_step, ss_ref[...], _heads3(u_ref), _heads3(w_ref), _heads3(at_ref), _heads3(qd_ref),
                         _heads3(kd_ref), eg_ref[...], _heads3(z_ref), nw_ref[...])
        ds, du, dw, dat, dqd, dkd, deg, dz, dnw = vjp((ds_scr[...], _heads3(do_ref)))
        ds_scr[...] = ds
        for r, v in zip((du_ref, dw_ref, dat_ref, dqd_ref, dkd_ref, dz_ref), (du, dw, dat, dqd, dkd, dz)):
            _unheads(r, v)
        deg_ref[...] = deg
        dnw_ref[...] += dnw

    row = lambda w_: pl.BlockSpec((CHUNK, w_), lambda n: (NCHUNK - 1 - n, 0))
    return pl.pallas_call(
        body, grid=(NCHUNK,),
        in_specs=[row(V_B)] * 5 + [row(HEADS), row(V_B), _full((1, HDIM)),
                                   pl.BlockSpec((None, HEADS, HDIM, HDIM), lambda n: (NCHUNK - 1 - n, 0, 0, 0)),
                                   row(V_B)],
        out_specs=[row(V_B)] * 5 + [row(HEADS), row(V_B), _full((1, HDIM))],
        out_shape=[_sds((T, V_B))] * 5 + [_sds((T, HEADS)), _sds((T, V_B)), _sds((1, HDIM))],
        scratch_shapes=[pltpu.VMEM((HEADS, HDIM, HDIM), F32)], name=name, compiler_params=_cp("arbitrary"),
    )(u, w, attn, qd, kd, egc, z, nw, ss, do)


def _final(name, x, fw, target, tm=512):
    def body(x_ref, fw_ref, t_ref, l_ref, dx_ref, dfw_ref):
        @pl.when(pl.program_id(0) == 0)
        def _():
            l_ref[...] = jnp.zeros_like(l_ref)
            dfw_ref[...] = jnp.zeros_like(dfw_ref)

        tv = t_ref[...]

        def f(xv, fwv):
            err = _rms(xv, fwv) - tv
            per_tok = jnp.mean(err * err, axis=-1, keepdims=True)
            return 0.5 * jnp.sum(per_tok, axis=0, keepdims=True)

        loss, vjp = jax.vjp(f, x_ref[...], fw_ref[...])
        dx, dfw = vjp(jnp.ones((1, 1), F32))
        l_ref[...] += loss
        dx_ref[...] = dx
        dfw_ref[...] += dfw

    tok = pl.BlockSpec((tm, D), lambda i: (i, 0))
    return pl.pallas_call(
        body, grid=(T // tm,), in_specs=[tok, _full((1, D)), tok], out_specs=[_full((1, 1)), tok, _full((1, D))],
        out_shape=[_sds((1, 1)), _sds((T, D)), _sds((1, D))], name=name, compiler_params=_cp("arbitrary"),
    )(x, fw, target)


def _m1_pre(tv, sv):
    return [_rms(tv[0], sv[0])]


def _m1_post(ys, tv, sv):
    return (jnp.concatenate(ys, axis=1),)


def _m1_post_split(ys, tv, sv):
    proj = jnp.concatenate(ys, axis=1)
    return tuple(proj[:, a:b] for a, b in zip(IN_SPLITS[:-1], IN_SPLITS[1:]))


def _m5_pre(tv, sv):
    return [tv[1], tv[2]]


def _m5_post(ys, tv, sv):
    return (tv[0] + ys[0] + ys[1],)


def _c1_pre(tv, sv):
    return [_rms(tv[0], sv[0])]


def _c1_post(ys, tv, sv):
    return ((jnp.concatenate(ys[:2], axis=1) + sv[1]) * jax.nn.sigmoid(jnp.concatenate(ys[2:], axis=1) + sv[2]),)


def _c3_pre(tv, sv):
    return [_silu(_layernorm(tv[0], sv[0], sv[1]))]


def _c3_post(ys, tv, sv):
    return (tv[1] + ys[0] + sv[2],)


def _row(v):
    return v.reshape(1, -1)


def _mixer_fwd(tag, x, p):
    parts = _blk_fwd(f"m1_fwd_{tag}", _m1_pre, [0], _m1_post_split, [x], [p["nw"]], [p["w_in"]],
                     [(b - a, F32) for a, b in zip(IN_SPLITS[:-1], IN_SPLITS[1:])])
    qa, ka, va, qkvb, z, ba = parts
    att = _attn_fwd(f"attn_fwd_{tag}", qa, ka, va, p["sinks"])
    qkvc = _conv_fwd(f"dnconv_fwd_{tag}", qkvb, p["dn_conv_w"], None, True)
    *loc, inv = _dn_local_fwd(f"dnloc_fwd_{tag}", qkvc, ba, p["a_log"], p["dt_bias"])
    og, ss = _dn_rec_fwd(f"dnrec_fwd_{tag}", *loc, z, p["dn_norm_w"])
    (out,) = _blk_fwd(f"m5_fwd_{tag}", _m5_pre, [0, 1], _m5_post, [x, att, og], [], [p["wo_a"], p["wo_b"]],
                      [(D, F32)])
    return out, dict(x=x, qa=qa, ka=ka, va=va, qkvb=qkvb, z=z, ba=ba, att=att, qkvc=qkvc, loc=loc, inv=inv, og=og,
                     ss=ss)


def _mixer_bwd(tag, dy, p, s):
    (dxa, datt, dog), _, (dwo_a, dwo_b) = _blk_bwd(f"m5_bwd_{tag}", _m5_pre, [0, 1], _m5_post,
                                                   [s["x"], s["att"], s["og"]], [], [p["wo_a"], p["wo_b"]], [[dy]])
    rec = _dn_rec_bwd(f"dnrec_bwd_{tag}", *s["loc"], s["z"], p["dn_norm_w"], s["ss"], dog)
    dz, dnw_dn = rec[6], rec[7]
    dqkvc, dba, dalog, ddtb = _dn_local_bwd(f"dnloc_bwd_{tag}", s["qkvc"], s["ba"], p["a_log"], p["dt_bias"],
                                            s["inv"], rec[:6])
    dqkvb, dconvw, _ = _conv_bwd(f"dnconv_bwd_{tag}", s["qkvb"], p["dn_conv_w"], None, True, dqkvc)
    dqa, dka, dva, dsinks = _attn_bwd(f"attn_bwd_{tag}", s["qa"], s["ka"], s["va"], p["sinks"], datt)
    (dx,), (dnw,), (dw_in,) = _blk_bwd(f"m1_bwd_{tag}", _m1_pre, [0], _m1_post, [s["x"]], [p["nw"]], [p["w_in"]],
                                       [[dqa, dka, dva, dqkvb, dz, dba]], res=dxa)
    return dx, dict(nw=dnw, w_in=dw_in, wo_a=dwo_a, wo_b=dwo_b, dn_conv_w=dconvw, sinks=dsinks, a_log=dalog,
                    dt_bias=ddtb, dn_norm_w=dnw_dn)


def _conformer_fwd(tag, x, p):
    (glu,) = _blk_fwd(f"c1_fwd_{tag}", _c1_pre, [0], _c1_post, [x], [p["nw"], p["b1a"], p["b1b"]], [p["w1"]],
                      [(D, F32)])
    cc = _conv_fwd(f"dwconv_fwd_{tag}", glu, p["w_dw"], p["b_dw"], False)
    (out,) = _blk_fwd(f"c3_fwd_{tag}", _c3_pre, [0], _c3_post, [cc, x], [p["ln_w"], p["ln_b"], p["b2"]], [p["w2"]],
                      [(D, F32)])
    return out, dict(x=x, glu=glu, cc=cc)


def _conformer_bwd(tag, dy, p, s):
    (dcc, dxa), (dlnw, dlnb, db2), (dw2,) = _blk_bwd(f"c3_bwd_{tag}", _c3_pre, [0], _c3_post, [s["cc"], s["x"]],
                                                     [p["ln_w"], p["ln_b"], p["b2"]], [p["w2"]], [[dy]])
    dglu, dwdw, dbdw = _conv_bwd(f"dwconv_bwd_{tag}", s["glu"], p["w_dw"], p["b_dw"], False, dcc)
    (dx,), (dnw, db1a, db1b), (dw1,) = _blk_bwd(f"c1_bwd_{tag}", _c1_pre, [0], _c1_post, [s["x"]],
                                                [p["nw"], p["b1a"], p["b1b"]], [p["w1"]], [[dglu]], res=dxa)
    return dx, dict(nw=dnw, b1a=db1a, b1b=db1b, w1=dw1, w_dw=dwdw, b_dw=dbdw, ln_w=dlnw, ln_b=dlnb, b2=db2, w2=dw2)


def _layer_fwd(l, x, nw, ffn, p):
    x1, *pre_a = _ffn_fwd(f"ffn_fwd_{l}a", x, _row(nw[0]), ffn, 0)
    p = dict(p, nw=_row(nw[1]))
    x2, sv = (_mixer_fwd if l % 2 == 0 else _conformer_fwd)(str(l), x1, p)
    out, *pre_b = _ffn_fwd(f"ffn_fwd_{l}b", x2, _row(nw[2]), ffn, 1)
    return out, (x, x2, p, sv, pre_a, pre_b)


def _layer_bwd(l, dx, nw, ffn, saved, after_first=lambda dx: dx):
    x0, x2, p, sv, pre_a, pre_b = saved
    dx, dn2, dffn = _ffn_bwd(f"ffn_bwd_{l}b", x2, _row(nw[2]), ffn, 1, pre_b, dx)
    dx = after_first(dx)
    dx, dmix = (_mixer_bwd if l % 2 == 0 else _conformer_bwd)(str(l), dx, p, sv)
    dx, dn0, dffn = _ffn_bwd(f"ffn_bwd_{l}a", x0, _row(nw[0]), ffn, 0, pre_a, dx, dffn)
    return dx, jnp.concatenate([dn0, dmix.pop("nw"), dn2], axis=0), dffn, dmix


def _place():
    x, y, c = lax.axis_index("x"), lax.axis_index("y"), lax.axis_index("c")
    chips = [(1 - x, y), (x, 1 - y), (1 - x, 1 - y)]
    return x, y, c, 2 * x + y, chips, [2 * px + py for px, py in chips]


def _handshake(peers):
    barrier = pltpu.get_barrier_semaphore()
    for p in peers:
        pl.semaphore_signal(barrier, inc=1, device_id=p, device_id_type=MESH)
    pl.semaphore_wait(barrier, len(peers))


def _chip_peers():
    x, y, c, _, chips, _ = _place()
    return [(*chip, c) for chip in chips] + [(x, y, 1 - c)]


def _gather_copies(ins, outs, nb, send, recv, fsend, frecv, lsem):
    n_in = len(ins)
    x, y, c, me, chips, cidx = _place()
    sib = (x, y, 1 - c)
    local = [pltpu.make_async_copy(ins[a], outs[a].at[me], lsem.at[a]) for a in range(n_in)]
    for cp in local:
        cp.start()

    def region(a, k, who):
        if k < 2:
            return outs[a].at[cidx[k], pl.ds(who, 1)]
        r = ins[a].shape[1] // 2
        return outs[a].at[cidx[2], pl.ds(who, 1), pl.ds((k - 2) * r, r)]

    def hop(a, k):
        if k < 2:
            src, dst = ins[a].at[pl.ds(c, 1)], outs[a].at[me, pl.ds(c, 1)]
        else:
            r = ins[a].shape[1] // 2
            src = dst = outs[a].at[cidx[3 - k], pl.ds(c, 1), pl.ds((k - 2) * r, r)]
        return pltpu.make_async_remote_copy(src, dst, send.at[4 * a + k], recv.at[4 * a + k],
                                            device_id=(*chips[k % 2], c), device_id_type=MESH)

    def landed(a, k):
        dst = region(a, k, c)
        return pltpu.make_async_remote_copy(dst, dst, send.at[4 * a + k], recv.at[4 * a + k],
                                            device_id=(*chips[k % 2], c), device_id_type=MESH)

    def passed(a, k, who):
        part = region(a, k, who)
        return pltpu.make_async_remote_copy(part, part, fsend.at[4 * a + k], frecv.at[4 * a + k], device_id=sib,
                                            device_id_type=MESH)

    def direct(a, j):
        k = 4 * nb + 3 * (a - nb) + j
        return pltpu.make_async_remote_copy(ins[a], outs[a].at[me], send.at[k], recv.at[k],
                                            device_id=(*chips[j], c), device_id_type=MESH)

    def direct_landed(a, j):
        k = 4 * nb + 3 * (a - nb) + j
        dst = outs[a].at[cidx[j]]
        return pltpu.make_async_remote_copy(dst, dst, send.at[k], recv.at[k], device_id=(*chips[j], c),
                                            device_id_type=MESH)

    sends = [hop(a, k) for a in range(nb) for k in range(2)] + [direct(a, j) for a in range(nb, n_in) for j in range(3)]
    for cp in sends:
        cp.start()
    for a in range(nb):
        for k in (1, 0):
            landed(a, k).wait_recv()
            for cp in (hop(a, 3 - k), passed(a, k, c)):
                cp.start()
                sends.append(cp)
    for a in range(nb):
        for k in (2, 3):
            landed(a, k).wait_recv()
            cp = passed(a, k, c)
            cp.start()
            sends.append(cp)
    for a in range(nb, n_in):
        for j in range(3):
            direct_landed(a, j).wait_recv()
    for a in range(nb):
        for k in range(4):
            passed(a, k, 1 - c).wait_recv()
    for cp in sends:
        cp.wait_send()
    for cp in local:
        cp.wait()


def _gather_sems(n_in, nb):
    dma = pltpu.SemaphoreType.DMA
    n_ici = 4 * nb + 3 * (n_in - nb)
    return [dma((n_ici,)), dma((n_ici,)), dma((4 * nb,)), dma((4 * nb,)), dma((n_in,))]


def _gather_async(name, halved, whole=()):
    nb, arrs = len(halved), list(halved) + list(whole)
    hbm = pltpu.MemorySpace.HBM
    ins = [jax.new_ref(a, memory_space=hbm) for a in arrs]
    outs = [jax.empty_ref(_sds((NCHIP,) + a.shape, a.dtype), memory_space=hbm) for a in arrs]

    @pl.kernel(mesh=plsc.ScalarSubcoreMesh(axis_name="seq", num_cores=1), name=name,
               scratch_types=tuple(_gather_sems(len(arrs), nb)),
               compiler_params=pltpu.CompilerParams(collective_id=2))
    def launch(send, recv, fsend, frecv, lsem):
        _handshake(_chip_peers())
        _gather_copies(ins, outs, nb, send, recv, fsend, frecv, lsem)

    launch()
    return outs


def _swap_halves(name, grads, after=None):
    n = len(grads)
    hbm = pltpu.MemorySpace.HBM
    ins = [jax.new_ref(g, memory_space=hbm) for g in grads]
    outs = [jax.empty_ref(_sds((NCHIP, g.shape[1] // 2) + g.shape[2:], g.dtype), memory_space=hbm) for g in grads]
    tile = (2 * 8, LANES)
    token = None if after is None else jax.empty_ref(_sds(tile, BF16), memory_space=hbm)

    @pl.kernel(mesh=plsc.ScalarSubcoreMesh(axis_name="seq", num_cores=1), name=name,
               scratch_types=(pltpu.SemaphoreType.DMA((n + 1,)), pltpu.SemaphoreType.DMA((n,))),
               compiler_params=pltpu.CompilerParams(collective_id=1))
    def launch(send, recv):
        x, y, c, _, _, _ = _place()
        sib = (x, y, 1 - c)
        _handshake([sib])
        if after is not None:
            tick = pltpu.make_async_copy(after.at[0, 0, 0, pl.ds(0, tile[0]), pl.ds(0, tile[1])], token, send.at[n])
            tick.start()
            tick.wait()
        cps = []
        for a in range(n):
            h = grads[a].shape[1] // 2
            cps.append(pltpu.make_async_remote_copy(ins[a].at[:, pl.ds((1 - c) * h, h)], outs[a], send.at[a],
                                                    recv.at[a], device_id=sib, device_id_type=MESH))
        for cp in cps:
            cp.start()
        for cp in cps:
            cp.wait()

    launch()
    return outs


def _row_tile(r, cap=256):
    return max(t for t in range(8, cap + 1, 8) if r % t == 0)


def _add_half(name, g, r, c_arr):
    _, l, rows, cols = g.shape
    h = l // 2
    tr = _row_tile(rows)

    def body(c_ref, g_ref, r_ref, o_ref):
        o_ref[...] = (g_ref[...].astype(F32) + r_ref[...].astype(F32)).astype(BF16)

    blk = (None, None, tr, cols)
    return pl.pallas_call(
        body,
        grid_spec=pltpu.PrefetchScalarGridSpec(
            num_scalar_prefetch=1, grid=(NCHIP, h, rows // tr),
            in_specs=[pl.BlockSpec(blk, lambda j, i, t, c_ref: (j, c_ref[0] * h + i, t, 0)),
                      pl.BlockSpec(blk, lambda j, i, t, c_ref: (j, i, t, 0))],
            out_specs=pl.BlockSpec(blk, lambda j, i, t, c_ref: (j, i, t, 0))),
        out_shape=_sds((NCHIP, h, rows, cols), BF16), name=name,
        compiler_params=_cp("parallel", "parallel", "parallel"),
    )(c_arr, g, r)


def _scatter_async(name, parts, sums, where):
    nb = len(parts)
    ins = [jax.new_ref(p, memory_space=pltpu.MemorySpace.HBM) for p in parts]
    dma = pltpu.SemaphoreType.DMA

    @pl.kernel(mesh=plsc.ScalarSubcoreMesh(axis_name="seq", num_cores=1), name=name,
               scratch_types=(dma((3 * nb,)), dma((3 * nb,)), dma((4 * nb,)), dma((4 * nb,)), dma((nb,))),
               compiler_params=pltpu.CompilerParams(collective_id=3))
    def launch(send, recv, fsend, frecv, lsem):
        _handshake(_chip_peers())
        x, y, c, me, chips, cidx = _place()
        sib = (x, y, 1 - c)

        def slot(a, half, chip):
            return sums[a].at[half, chip, pl.ds(where[a], 1)]

        local = [pltpu.make_async_copy(ins[a].at[me], slot(a, c, me), lsem.at[a]) for a in range(nb)]
        for cp in local:
            cp.start()

        def ici(a, j):
            return pltpu.make_async_remote_copy(ins[a].at[cidx[j]], slot(a, c, me), send.at[a * 3 + j],
                                                recv.at[a * 3 + j], device_id=(*chips[j], c), device_id_type=MESH)

        def landed(a, j):
            dst = slot(a, c, cidx[j])
            return pltpu.make_async_remote_copy(dst, dst, send.at[a * 3 + j], recv.at[a * 3 + j],
                                                device_id=(*chips[j], c), device_id_type=MESH)

        def passed(a, j, who):
            dst = slot(a, who, me if j == 3 else cidx[j])
            src = ins[a].at[me] if j == 3 else dst
            return pltpu.make_async_remote_copy(src, dst, fsend.at[a * 4 + j], frecv.at[a * 4 + j], device_id=sib,
                                                device_id_type=MESH)

        sends = [ici(a, j) for a in range(nb) for j in range(3)] + [passed(a, 3, c) for a in range(nb)]
        for cp in sends:
            cp.start()
        for a in range(nb):
            for j in range(3):
                landed(a, j).wait_recv()
                cp = passed(a, j, c)
                cp.start()
                sends.append(cp)
        for a in range(nb):
            for j in range(4):
                passed(a, j, 1 - c).wait_recv()
        for cp in sends:
            cp.wait_send()
        for cp in local:
            cp.wait()

    launch()


def _exchange_small(small, rep):
    def body(small_in, rep_in, small_out, rep_out, lsem, ssend, srecv):
        x, y, c, me, _, _ = _place()
        dev = 4 * x + 2 * y + c
        local = [pltpu.make_async_copy(small_in.at[me], small_out.at[dev], lsem.at[0]),
                 pltpu.make_async_copy(rep_in, rep_out.at[dev], lsem.at[1])]
        for cp in local:
            cp.start()

        def peer(r):
            return (1 - x if r & 4 else x), (1 - y if r & 2 else y), (1 - c if r & 1 else c)

        def tiny(r, which):
            px, py, pc = peer(r)
            k = (r - 1) * 2 + which
            if which == 0:
                return pltpu.make_async_remote_copy(small_in.at[2 * px + py], small_out.at[dev], ssend.at[k],
                                                    srecv.at[k], device_id=(px, py, pc), device_id_type=MESH)
            return pltpu.make_async_remote_copy(rep_in, rep_out.at[dev], ssend.at[k], srecv.at[k],
                                                device_id=(px, py, pc), device_id_type=MESH)

        def tiny_landed(r, which):
            px, py, pc = peer(r)
            k = (r - 1) * 2 + which
            dst = (small_out if which == 0 else rep_out).at[4 * px + 2 * py + pc]
            return pltpu.make_async_remote_copy(dst, dst, ssend.at[k], srecv.at[k], device_id=(px, py, pc),
                                                device_id_type=MESH)

        sends = [tiny(r, w) for r in range(1, NDEV) for w in range(2)]
        for cp in sends:
            cp.start()
        for r in range(1, NDEV):
            for w in range(2):
                tiny_landed(r, w).wait_recv()
        for cp in sends:
            cp.wait_send()
        for cp in local:
            cp.wait()

    dma = pltpu.SemaphoreType.DMA
    return pl.pallas_call(
        body, in_specs=[ANY] * 2, out_specs=[ANY] * 2,
        out_shape=[_sds((NDEV,) + small.shape[1:], F32), _sds((NDEV,) + rep.shape, F32)],
        scratch_shapes=[dma((2,)), dma((2 * (NDEV - 1),)), dma((2 * (NDEV - 1),))], name="exchange_small_grads",
    )(small, rep)


def _adamw_math(w, g, m, v):
    m = B1 * m + (1.0 - B1) * g
    v = B2 * v + (1.0 - B2) * (g * g)
    m_hat = m / (1.0 - B1 ** STEP)
    v_hat = v / (1.0 - B2 ** STEP)
    return -LR * (m_hat / (jnp.sqrt(v_hat) + AEPS) + WD * w), m, v


def _adamw_big(name, w, m, v, parts, row0=0, first=0, outs=None):
    _, _, rows, cols = w.shape
    n = parts.shape[2]
    tr = _row_tile(rows)
    t0 = row0 // tr

    def body(w_ref, m_ref, v_ref, p_ref, *rest):
        g_ref, d_ref, nm_ref, nv_ref = rest[-4:]
        g = p_ref[0].astype(F32)
        for q in range(1, NCHIP):
            g = g + p_ref[q].astype(F32)
        d, nm, nv = _adamw_math(w_ref[...], g, m_ref[...], v_ref[...])
        g_ref[...], d_ref[...], nm_ref[...], nv_ref[...] = g, d, nm, nv

    spec = pl.BlockSpec((None, None, tr, cols), lambda i, p, t: (first + i, p, t, 0))
    na = 0 if outs is None else 4
    return pl.pallas_call(
        body, grid=(n, 2, rows // tr),
        in_specs=[spec, spec, spec,
                  pl.BlockSpec((None, NCHIP, None, tr, cols), lambda i, p, t: (p, 0, i, t0 + t, 0))] + [ANY] * na,
        out_specs=[spec] * 4, out_shape=[_sds(w.shape)] * 4, input_output_aliases={4 + k: k for k in range(na)},
        name=name, compiler_params=_cp("parallel", "parallel", "parallel"),
    )(w, m, v, parts, *(outs or ()))


def _adamw_small(name, w, m, v, parts):
    def body(w_ref, m_ref, v_ref, p_ref, g_ref, d_ref, nm_ref, nv_ref):
        g = p_ref[0]
        for q in range(1, NDEV):
            g = g + p_ref[q]
        d, nm, nv = _adamw_math(w_ref[...], g, m_ref[...], v_ref[...])
        g_ref[...], d_ref[...], nm_ref[...], nv_ref[...] = g, d, nm, nv

    return pl.pallas_call(body, out_shape=[_sds(w.shape)] * 4, name=name)(w, m, v, parts)


def _pack(arrs, rows):
    flat = jnp.concatenate([a.reshape(-1) for a in arrs])
    return jnp.pad(flat, (0, rows * LANES - flat.shape[0])).reshape(rows, LANES)


def _unpack(packed, shapes):
    flat, out, o = packed.reshape(-1), [], 0
    for s in shapes:
        n = 1
        for d in s:
            n *= d
        out.append(flat[o:o + n].reshape(s))
        o += n
    return out


SMALL_ROWS, REP_ROWS = 200, 16


def kernel(x, norm_w, ffn_w_gate, ffn_w_up, ffn_w_down, mix_w_in, dn_conv_w, attn_sinks, dn_a_log, dn_dt_bias, dn_norm_w, mix_w_out, conv_w_pw1, conv_b_pw1, conv_w_dw, conv_b_dw, conv_ln_w, conv_ln_b, conv_w_pw2, conv_b_pw2, final_norm_w, loss_target, m_norm_w, m_ffn_w_gate, m_ffn_w_up, m_ffn_w_down, m_mix_w_in, m_dn_conv_w, m_attn_sinks, m_dn_a_log, m_dn_dt_bias, m_dn_norm_w, m_mix_w_out, m_conv_w_pw1, m_conv_b_pw1, m_conv_w_dw, m_conv_b_dw, m_conv_ln_w, m_conv_ln_b, m_conv_w_pw2, m_conv_b_pw2, m_final_norm_w, v_norm_w, v_ffn_w_gate, v_ffn_w_up, v_ffn_w_down, v_mix_w_in, v_dn_conv_w, v_attn_sinks, v_dn_a_log, v_dn_dt_bias, v_dn_norm_w, v_mix_w_out, v_conv_w_pw1, v_conv_b_pw1, v_conv_w_dw, v_conv_b_dw, v_conv_ln_w, v_conv_ln_b, v_conv_w_pw2, v_conv_b_pw2, v_final_norm_w):
    small_names = ["norm_w", "dn_conv_w", "conv_b_pw1", "conv_w_dw", "conv_b_dw", "conv_ln_w", "conv_ln_b",
                   "conv_b_pw2"]
    rep_names = ["attn_sinks", "dn_a_log", "dn_dt_bias", "dn_norm_w", "final_norm_w"]
    w = dict(norm_w=norm_w, ffn_w_gate=ffn_w_gate, ffn_w_up=ffn_w_up, ffn_w_down=ffn_w_down, mix_w_in=mix_w_in, dn_conv_w=dn_conv_w, attn_sinks=attn_sinks, dn_a_log=dn_a_log, dn_dt_bias=dn_dt_bias, dn_norm_w=dn_norm_w, mix_w_out=mix_w_out, conv_w_pw1=conv_w_pw1, conv_b_pw1=conv_b_pw1, conv_w_dw=conv_w_dw, conv_b_dw=conv_b_dw, conv_ln_w=conv_ln_w, conv_ln_b=conv_ln_b, conv_w_pw2=conv_w_pw2, conv_b_pw2=conv_b_pw2, final_norm_w=final_norm_w)
    m = dict(norm_w=m_norm_w, ffn_w_gate=m_ffn_w_gate, ffn_w_up=m_ffn_w_up, ffn_w_down=m_ffn_w_down, mix_w_in=m_mix_w_in, dn_conv_w=m_dn_conv_w, attn_sinks=m_attn_sinks, dn_a_log=m_dn_a_log, dn_dt_bias=m_dn_dt_bias, dn_norm_w=m_dn_norm_w, mix_w_out=m_mix_w_out, conv_w_pw1=m_conv_w_pw1, conv_b_pw1=m_conv_b_pw1, conv_w_dw=m_conv_w_dw, conv_b_dw=m_conv_b_dw, conv_ln_w=m_conv_ln_w, conv_ln_b=m_conv_ln_b, conv_w_pw2=m_conv_w_pw2, conv_b_pw2=m_conv_b_pw2, final_norm_w=m_final_norm_w)
    v = dict(norm_w=v_norm_w, ffn_w_gate=v_ffn_w_gate, ffn_w_up=v_ffn_w_up, ffn_w_down=v_ffn_w_down, mix_w_in=v_mix_w_in, dn_conv_w=v_dn_conv_w, attn_sinks=v_attn_sinks, dn_a_log=v_dn_a_log, dn_dt_bias=v_dn_dt_bias, dn_norm_w=v_dn_norm_w, mix_w_out=v_mix_w_out, conv_w_pw1=v_conv_w_pw1, conv_b_pw1=v_conv_b_pw1, conv_w_dw=v_conv_w_dw, conv_b_dw=v_conv_b_dw, conv_ln_w=v_conv_ln_w, conv_ln_b=v_conv_ln_b, conv_w_pw2=v_conv_w_pw2, conv_b_pw2=v_conv_b_pw2, final_norm_w=v_final_norm_w)
    order = ["norm_w", "ffn_w_gate", "ffn_w_up", "ffn_w_down", "mix_w_in", "dn_conv_w", "attn_sinks", "dn_a_log",
             "dn_dt_bias", "dn_norm_w", "mix_w_out", "conv_w_pw1", "conv_b_pw1", "conv_w_dw", "conv_b_dw",
             "conv_ln_w", "conv_ln_b", "conv_w_pw2", "conv_b_pw2", "final_norm_w"]

    small_shapes = [w[n].shape for n in small_names]
    rep_shapes = [w[n].shape for n in rep_names]

    def halves(a):
        return a.reshape(a.shape[:-2] + (2, a.shape[-2] // 2, a.shape[-1]))

    tr = lambda a: jnp.swapaxes(a, -1, -2)
    gate_t, up_t = tr(ffn_w_gate), tr(ffn_w_up)

    def layer_shards(l):
        mix_in, mix_out = (mix_w_in, mix_w_out) if l % 2 == 0 else (conv_w_pw1, conv_w_pw2)
        return [t.astype(BF16) for t in (jnp.concatenate([gate_t[l], up_t[l], ffn_w_down[l]], axis=1),
                                         halves(mix_in[l // 2]), halves(mix_out[l // 2]))]

    first = layer_shards(0) + [_pack([w[n] for n in small_names], SMALL_ROWS)]
    first, (gate_t, up_t, ffn_w_down, mix_w_in, mix_w_out, conv_w_pw1, conv_w_pw2) = lax.optimization_barrier(
        (first, (gate_t, up_t, ffn_w_down, mix_w_in, mix_w_out, conv_w_pw1, conv_w_pw2)))
    gathering = [_gather_async("gather_layer0", first[:3], first[3:])]
    gathering += [_gather_async(f"gather_layer{l}", layer_shards(l)) for l in range(1, DEPTH)]

    def mixer_params(l, w_a, w_b):
        e = l // 2
        w_a = w_a.reshape(NCHIP, D, -1)
        w_b = w_b.reshape(D, D)
        if l % 2 == 0:
            return dict(w_in=w_a, dn_conv_w=sm["dn_conv_w"][e], sinks=_row(attn_sinks[e]), a_log=_row(dn_a_log[e]),
                        dt_bias=_row(dn_dt_bias[e]), dn_norm_w=_row(dn_norm_w[e]), wo_a=w_b[:Q_A], wo_b=w_b[Q_A:])
        return dict(b1a=_row(sm["conv_b_pw1"][e, :D]), b1b=_row(sm["conv_b_pw1"][e, D:]), w1=w_a,
                    w_dw=sm["conv_w_dw"][e], b_dw=_row(sm["conv_b_dw"][e]), ln_w=_row(sm["conv_ln_w"][e]),
                    ln_b=_row(sm["conv_ln_b"][e]), b2=_row(sm["conv_b_pw2"][e]), w2=w_b)

    xs, saved, ffn_w = x[0], [], []
    for l in range(DEPTH):
        got = [r[...] for r in gathering[l]]
        if l == 0:
            per_chip = [_unpack(got[3][q], small_shapes) for q in range(NCHIP)]
            sm = {n: jnp.concatenate([per_chip[q][i] for q in range(NCHIP)], axis=-1)
                  for i, n in enumerate(small_names)}
        else:
            xs, got = lax.optimization_barrier((xs, got))
        ffn_w.append(got[0])
        xs, sv = _layer_fwd(l, xs, sm["norm_w"][l], got[0], mixer_params(l, got[1], got[2]))
        saved.append(sv)
    loss, dx, dfw = _final("final", xs, _row(final_norm_w), loss_target[0])

    hbm = pltpu.MemorySpace.HBM
    row_shapes = dict(ffn=(3 * FS, D), w_in=(D // 2, IN_COLS // NCHIP), w_out=(D // 8, D), pw1=(D // 2, D // 2),
                      pw2=(D // 8, D))
    new_sums = lambda k, n: jax.empty_ref(_sds((2, NCHIP, n) + row_shapes[k], BF16), memory_space=hbm)
    sums_0 = {k: new_sums(k, 1) for k in ("ffn", "w_in", "w_out")}
    sums = dict(ffn=new_sums("ffn", DEPTH - 1), w_in=new_sums("w_in", 1), w_out=new_sums("w_out", 1),
                pw1=new_sums("pw1", 2), pw2=new_sums("pw2", 2))
    c_arr = lax.axis_index("c").astype(jnp.int32).reshape(1)
    dnorm, gmix = [None] * DEPTH, [None] * DEPTH

    def hand_on(l, grads, swapped):
        def run(dx):
            dx, other = lax.optimization_barrier((dx, [r[...] for r in swapped]))
            parts = [_add_half(f"add_half_{l}_{k}", gg, rr, c_arr) for k, (gg, rr) in enumerate(zip(grads, other))]
            dx, parts = lax.optimization_barrier((dx, parts))
            keys = ("ffn", "w_in", "w_out") if l % 2 == 0 else ("ffn", "pw1", "pw2")
            if l == 0:
                _scatter_async("scatter_grads_0", parts, [sums_0[k] for k in keys], [0, 0, 0])
            else:
                _scatter_async(f"scatter_grads_{l}", parts, [sums[k] for k in keys],
                               [l - 1, 0, 0] if l % 2 == 0 else [l - 1, l // 2, l // 2])
            return dx
        return run

    pending = lambda dx: dx
    for l in reversed(range(DEPTH)):
        dx, dnorm[l], dffn, gmix[l] = _layer_bwd(l, dx, sm["norm_w"][l], ffn_w[l], saved[l], pending)
        if l % 2 == 0:
            g_a, g_b = gmix[l]["w_in"], jnp.concatenate([gmix[l]["wo_a"], gmix[l]["wo_b"]], axis=0)
        else:
            g_a, g_b = gmix[l]["w1"], gmix[l]["w2"]
        g_a = halves(g_a).astype(BF16)
        g_b = g_b.reshape(NCHIP, 2, D // 8, D).astype(BF16)
        dx, grads = lax.optimization_barrier((dx, [dffn, g_a, g_b]))
        pending = hand_on(l, grads, _swap_halves(f"swap_grads_{l}", grads, sums["ffn"] if l < DEPTH - 1 else None))
    gm, gc = [gmix[0], gmix[2]], [gmix[1], gmix[3]]
    small_g = dict(
        norm_w=jnp.stack(dnorm), dn_conv_w=jnp.stack([gm[e]["dn_conv_w"] for e in range(2)]),
        conv_b_pw1=jnp.stack([jnp.concatenate([gc[e]["b1a"], gc[e]["b1b"]], axis=1)[0] for e in range(2)]),
        conv_w_dw=jnp.stack([gc[e]["w_dw"] for e in range(2)]),
        conv_b_dw=jnp.stack([gc[e]["b_dw"][0] for e in range(2)]),
        conv_ln_w=jnp.stack([gc[e]["ln_w"][0] for e in range(2)]),
        conv_ln_b=jnp.stack([gc[e]["ln_b"][0] for e in range(2)]),
        conv_b_pw2=jnp.stack([gc[e]["b2"][0] for e in range(2)]))
    small_by_chip = jnp.stack([_pack([jnp.split(small_g[n], NCHIP, axis=-1)[q] for n in small_names], SMALL_ROWS)
                               for q in range(NCHIP)])
    rep_g = _pack([jnp.stack([gm[e]["sinks"][0] for e in range(2)]), jnp.stack([gm[e]["a_log"][0] for e in range(2)]),
                   jnp.stack([gm[e]["dt_bias"][0] for e in range(2)]),
                   jnp.stack([gm[e]["dn_norm_w"][0] for e in range(2)]), dfw[0]], REP_ROWS)
    small_sum, rep_sum = _exchange_small(small_by_chip, rep_g)
    dx, small_sum, rep_sum = lax.optimization_barrier((dx, small_sum, rep_sum))
    dx = pending(dx)

    big = (("ffn_w_gate", "ffn", 0), ("ffn_w_up", "ffn", FS), ("ffn_w_down", "ffn", 2 * FS), ("mix_w_in", "w_in", 0),
           ("mix_w_out", "w_out", 0), ("conv_w_pw1", "pw1", 0), ("conv_w_pw2", "pw2", 0))
    views = {n: (tr, tr) if n in ("ffn_w_gate", "ffn_w_up") else (
        (lambda a: a) if w[n].ndim == 4 else halves, lambda o, n=n: o.reshape(w[n].shape)) for n, _, _ in big}
    partial_sums = {k: r[...] for k, r in sums.items()}
    upper = {}
    for n, key, row0 in big:
        view = views[n][0]
        upper[n] = _adamw_big(f"adamw_{n}", view(w[n]), view(m[n]), view(v[n]), partial_sums[key], row0,
                              first=0 if key in ("pw1", "pw2") else 1)
    upper, partial_sums_0 = lax.optimization_barrier((upper, {k: r[...] for k, r in sums_0.items()}))
    res = {}
    for n, key, row0 in big:
        view, back = views[n]
        outs = upper[n] if key not in partial_sums_0 else _adamw_big(
            f"adamw_{n}_0", view(w[n]), view(m[n]), view(v[n]), partial_sums_0[key], row0, first=0, outs=upper[n])
        res[n] = [back(o) for o in outs]
    outs = _adamw_small("adamw_small", *[_pack([d[n] for n in small_names], SMALL_ROWS) for d in (w, m, v)],
                        small_sum)
    for i, n in enumerate(small_names):
        res[n] = [_unpack(o, small_shapes)[i] for o in outs]
    outs = _adamw_small("adamw_replicated", *[_pack([d[n] for n in rep_names], REP_ROWS) for d in (w, m, v)],
                        rep_sum)
    for i, n in enumerate(rep_names):
        res[n] = [_unpack(o, rep_shapes)[i] for o in outs]

    total = lax.psum(loss[0, 0], ("x", "y", "c"))
    return (total, dx[None], *[res[n][0] for n in order], *[res[n][1] for n in order],
            *[res[n][2] for n in order], *[res[n][3] for n in order])
```

```python
import jax
import jax.numpy as jnp
from jax import lax
from jax.experimental import pallas as pl
from jax.experimental.pallas import tpu as pltpu
from jax.experimental.pallas import tpu_sc as plsc

F32, BF16 = jnp.float32, jnp.bfloat16
MESH = pl.DeviceIdType.MESH
ANY = pl.BlockSpec(memory_space=pl.ANY)

T, D, F = 2048, 1024, 2816
DEPTH = 4
EPS = 1e-6
HEADS, HDIM, KV_HEADS, GROUP = 8, 64, 2, 4
WINDOW = BLOCK = 128
CHUNK = 64
NCHUNK = T // CHUNK
DN_CONV, CONV_WIDTH = 4, 31
Q_A, KV_A, QKV_B, V_B = 512, 128, 1536, 512
IN_COLS = 2832
IN_SPLITS = (0, 512, 640, 768, 2304, 2816, 2832)
NCHIP, NDEV = 4, 8
FS = F // NCHIP
LR, B1, B2, AEPS, WD, STEP = 0.001, 0.9, 0.999, 1e-08, 0.01, 10
V7X_VMEM_BYTES = 64 * 1024 * 1024
VMEM_LIMIT = V7X_VMEM_BYTES * 7 // 8
LANES = 128


def _cp(*sem):
    return pltpu.CompilerParams(dimension_semantics=sem, vmem_limit_bytes=VMEM_LIMIT)


def _sds(shape, dtype=F32):
    return jax.ShapeDtypeStruct(tuple(shape), dtype)


def _full(shape):
    nd = len(shape)
    return pl.BlockSpec(tuple(shape), lambda *_: (0,) * nd)


def _split_bf16(a):
    hi = a.astype(BF16)
    return hi, (a - hi.astype(F32)).astype(BF16)


def _dg(a, b, ca, cb, hi=False):
    if a.ndim == 3 and b.ndim == 3:
        dims = (((ca + 1,), (cb + 1,)), ((0,), (0,)))
    else:
        dims = (((ca,), (cb,)), ((), ()))
    dot = lambda p, q: lax.dot_general(p, q, dims, preferred_element_type=F32)
    if hi:
        a_hi, a_lo = _split_bf16(a.astype(F32))
        b_hi, b_lo = _split_bf16(b.astype(F32))
        return dot(a_hi, b_hi) + (dot(a_hi, b_lo) + dot(a_lo, b_hi))
    return dot(a.astype(BF16), b.astype(BF16))


def _make_mm(hi):
    @jax.custom_vjp
    def nn(a, b):
        return _dg(a, b, 1, 0, hi)

    @jax.custom_vjp
    def nt(a, b):
        return _dg(a, b, 1, 1, hi)

    @jax.custom_vjp
    def tn(a, b):
        return _dg(a, b, 0, 0, hi)

    nn.defvjp(lambda a, b: (_dg(a, b, 1, 0, hi), (a, b)),
              lambda r, g: (_dg(g, r[1], 1, 1, hi).astype(r[0].dtype), _dg(r[0], g, 0, 0, hi).astype(r[1].dtype)))
    nt.defvjp(lambda a, b: (_dg(a, b, 1, 1, hi), (a, b)),
              lambda r, g: (_dg(g, r[1], 1, 0, hi).astype(r[0].dtype), _dg(g, r[0], 0, 0, hi).astype(r[1].dtype)))
    tn.defvjp(lambda a, b: (_dg(a, b, 0, 0, hi), (a, b)),
              lambda r, g: (_dg(r[1], g, 1, 1, hi).astype(r[0].dtype), _dg(r[0], g, 1, 0, hi).astype(r[1].dtype)))
    return nn, nt, tn


_nn, _nt, _tn = _make_mm(False)
_nn_hi, _nt_hi, _tn_hi = _make_mm(True)


def _rms(x, w):
    return x * lax.rsqrt(jnp.mean(x * x, axis=-1, keepdims=True) + EPS) * w


def _layernorm(x, w, b):
    xc = x - jnp.mean(x, axis=-1, keepdims=True)
    return xc * lax.rsqrt(jnp.mean(xc * xc, axis=-1, keepdims=True) + EPS) * w + b


def _silu(x):
    return x * jax.nn.sigmoid(x)


def _iota2(shape, dim):
    return lax.broadcasted_iota(jnp.int32, shape, dim)


def _flat_weights(lhs_idx, weights):
    specs, ops, lhs_of, where = [], [], [], []
    for a, (k, w) in enumerate(zip(lhs_idx, weights)):
        for q in range(1 if w.ndim == 2 else w.shape[0]):
            specs.append(_full(w.shape) if w.ndim == 2
                         else pl.BlockSpec((None,) + w.shape[1:], lambda i, q=q: (q, 0, 0)))
            ops.append(w)
            lhs_of.append(k)
            where.append((a, None if w.ndim == 2 else q))
    return specs, ops, lhs_of, where


def _blk_fwd(name, pre, lhs_idx, post, toks, smalls, weights, outs, tm=512):
    wspecs, wops, lhs_of, _ = _flat_weights(lhs_idx, weights)
    nt_, ns, nw = len(toks), len(smalls), len(wops)

    def body(*refs):
        tv = [r[...] for r in refs[:nt_]]
        sv = [r[...] for r in refs[nt_:nt_ + ns]]
        wr = refs[nt_ + ns:nt_ + ns + nw]
        orf = refs[nt_ + ns + nw:]
        lhs = pre(tv, sv)
        ys = [_dg(lhs[i], w[...], 1, 0) for i, w in zip(lhs_of, wr)]
        for o_ref, o in zip(orf, post(ys, tv, sv)):
            o_ref[...] = o.astype(o_ref.dtype)

    in_specs = ([pl.BlockSpec((tm, a.shape[1]), lambda i: (i, 0)) for a in toks]
                + [_full(a.shape) for a in smalls] + wspecs)
    out_specs = [pl.BlockSpec((tm, w_), lambda i: (i, 0)) for w_, _ in outs]
    return pl.pallas_call(
        body, grid=(T // tm,), in_specs=in_specs, out_specs=out_specs,
        out_shape=[_sds((T, w_), dt) for w_, dt in outs], name=name, compiler_params=_cp("parallel"),
    )(*toks, *smalls, *wops)


def _blk_bwd(name, pre, lhs_idx, post, toks, smalls, weights, ct_groups, res=None, linear_post=False, tm=256,
             wchunk=512):
    wspecs, wops, lhs_of, where = _flat_weights(lhs_idx, weights)
    nt_, ns, nw, na = len(toks), len(smalls), len(wops), len(weights)
    cts = [a for g in ct_groups for a in g]
    nc = len(cts)
    widths = [sum(a.shape[1] for a in g) for g in ct_groups]
    has_res = res is not None

    def body(*refs):
        p = 0
        tr = refs[p:p + nt_]; p += nt_
        sr = refs[p:p + ns]; p += ns
        wr = refs[p:p + nw]; p += nw
        cr = refs[p:p + nc]; p += nc
        rr = refs[p:p + has_res]; p += has_res
        dtr = refs[p:p + nt_]; p += nt_
        dsr = refs[p:p + ns]; p += ns
        dwr = refs[p:p + na]; p += na
        scr = refs[p:]
        i = pl.program_id(0)

        @pl.when(i == 0)
        def _():
            for r in list(dsr) + list(dwr):
                r[...] = jnp.zeros_like(r)

        tv = [r[...] for r in tr]
        sv = [r[...] for r in sr]
        ctv, q, si = [], 0, 0
        for g in ct_groups:
            if len(g) == 1:
                ctv.append(cr[q][...].astype(F32))
            else:
                off = 0
                for j, a in enumerate(g):
                    scr[si][:, off:off + a.shape[1]] = cr[q + j][...].astype(F32)
                    off += a.shape[1]
                ctv.append(scr[si][...])
                si += 1
            q += len(g)

        lhs, vjp_pre = jax.vjp(lambda *a: tuple(pre(list(a[:nt_]), list(a[nt_:]))), *tv, *sv)
        lhs_b = [l.astype(BF16) for l in lhs]
        ys = [jnp.zeros((tm, w.shape[1]), F32) if linear_post else _dg(lhs_b[k], w[...], 1, 0)
              for k, w in zip(lhs_of, wr)]
        _, vjp_post = jax.vjp(lambda *a: tuple(post(list(a[:nw]), list(a[nw:nw + nt_]), list(a[nw + nt_:]))),
                              *ys, *tv, *sv)
        gp = vjp_post(tuple(ctv))
        dys, dt_post, ds_post = gp[:nw], gp[nw:nw + nt_], gp[nw + nt_:]
        dlhs = [None] * len(lhs)
        for k, w, dy, (a, q) in zip(lhs_of, wr, dys, where):
            dyb = dy.astype(BF16)
            n = w.shape[1]
            for c0 in range(0, n, wchunk):
                c1 = min(n, c0 + wchunk)
                part = _dg(lhs_b[k], dyb[:, c0:c1], 0, 0)
                if q is None:
                    dwr[a][:, c0:c1] += part
                else:
                    dwr[a][q, :, c0:c1] += part
            d = _dg(dyb, w[...], 1, 1)
            dlhs[k] = d if dlhs[k] is None else dlhs[k] + d
        gq = vjp_pre(tuple(d.astype(l.dtype) for d, l in zip(dlhs, lhs)))
        dt_pre, ds_pre = gq[:nt_], gq[nt_:]
        for j in range(nt_):
            d = dt_post[j] + dt_pre[j]
            if j == 0 and has_res:
                d = d + rr[0][...]
            dtr[j][...] = d
        for j in range(ns):
            dsr[j][...] += ds_post[j] + ds_pre[j]

    tok_spec = lambda a: pl.BlockSpec((tm, a.shape[1]), lambda i: (i, 0))
    in_specs = ([tok_spec(a) for a in toks] + [_full(a.shape) for a in smalls] + wspecs
                + [tok_spec(a) for a in cts] + ([tok_spec(res)] if has_res else []))
    out_specs = [tok_spec(a) for a in toks] + [_full(a.shape) for a in smalls] + [_full(w.shape) for w in weights]
    out_shape = ([_sds(a.shape) for a in toks] + [_sds(a.shape) for a in smalls] + [_sds(w.shape) for w in weights])
    scratch = [pltpu.VMEM((tm, wd), F32) for g, wd in zip(ct_groups, widths) if len(g) > 1]
    outs = pl.pallas_call(
        body, grid=(T // tm,), in_specs=in_specs, out_specs=out_specs, out_shape=out_shape,
        scratch_shapes=scratch, name=name, compiler_params=_cp("arbitrary"),
    )(*toks, *smalls, *wops, *cts, *([res] if has_res else []))
    return outs[:nt_], outs[nt_:nt_ + ns], outs[nt_ + ns:]


def _ffn_fwd(name, x, nw, ffn, idx, tm=512):
    def body(x_ref, nw_ref, wg_ref, wu_ref, wd_ref, o_ref, a_ref, b_ref, h_ref):
        s = pl.program_id(1)

        @pl.when(s == 0)
        def _():
            xv = x_ref[...]
            h_ref[...] = _rms(xv, nw_ref[...]).astype(BF16)
            o_ref[...] = xv

        h = h_ref[...]
        a = _dg(h, wg_ref[...], 1, 1).astype(BF16)
        b = _dg(h, wu_ref[...], 1, 1).astype(BF16)
        a_ref[...] = a
        b_ref[...] = b
        o_ref[...] += 0.5 * _dg(_swiglu_act(a, b)[0], wd_ref[...], 1, 0)

    wspec = lambda k: pl.BlockSpec((None, None, FS, D), lambda i, s: (s, idx, k, 0))
    act = pl.BlockSpec((None, tm, FS), lambda i, s: (s, i, 0))
    return pl.pallas_call(
        body, grid=(T // tm, NCHIP),
        in_specs=[pl.BlockSpec((tm, D), lambda i, s: (i, 0)), _full((1, D)), wspec(0), wspec(1), wspec(2)],
        out_specs=[pl.BlockSpec((tm, D), lambda i, s: (i, 0)), act, act, pl.BlockSpec((tm, D), lambda i, s: (i, 0))],
        out_shape=[_sds((T, D)), _sds((NCHIP, T, FS), BF16), _sds((NCHIP, T, FS), BF16), _sds((T, D), BF16)],
        name=name, compiler_params=_cp("parallel", "arbitrary"),
    )(x, nw, ffn, ffn, ffn)


def _swiglu_act(a, b):
    a, b = a.astype(F32), b.astype(F32)
    sa = jax.nn.sigmoid(a)
    act = a * sa
    return act * b, a, b, sa, act


def _ffn_bwd(name, x, nw, ffn, idx, pre, dy, gbuf=None, tm=512):
    ni = T // tm

    def body(x_ref, dy_ref, nw_ref, wg_ref, wu_ref, wd_ref, a_ref, b_ref, h_ref, dx_ref, dnw_ref, dffn_ref, dh_acc,
             ag, au, ad):
        s, i = pl.program_id(0), pl.program_id(1)
        rows = pl.ds(pl.multiple_of(i * tm, tm), tm)

        @pl.when((s == 0) & (i == 0))
        def _():
            dnw_ref[...] = jnp.zeros_like(dnw_ref)

        def accumulate(acc, part):
            @pl.when(i == 0)
            def _():
                acc[...] = part

            @pl.when(i > 0)
            def _():
                acc[...] += part

        hb = h_ref[...]
        gated, a, b, sa, act = _swiglu_act(a_ref[...], b_ref[...])
        dyb = (0.5 * dy_ref[...]).astype(BF16)
        accumulate(ad, _dg(gated, dyb, 0, 0))
        dact = _dg(dyb, wd_ref[...], 1, 1)
        da = (dact * b * (sa * (1.0 + a * (1.0 - sa)))).astype(BF16)
        db = (dact * act).astype(BF16)
        accumulate(ag, _dg(da, hb, 0, 0))
        accumulate(au, _dg(db, hb, 0, 0))
        dh = _dg(da, wg_ref[...], 1, 0) + _dg(db, wu_ref[...], 1, 0)

        @pl.when(s == 0)
        def _():
            dh_acc[rows, :] = dh

        @pl.when((s > 0) & (s < NCHIP - 1))
        def _():
            dh_acc[rows, :] += dh

        @pl.when(s == NCHIP - 1)
        def _():
            _, vjp_rms = jax.vjp(_rms, x_ref[...], nw_ref[...])
            dx, dnw = vjp_rms(dh_acc[rows, :] + dh)
            dx_ref[...] = dy_ref[...] + dx
            dnw_ref[...] += dnw

        @pl.when(i == ni - 1)
        def _():
            dffn_ref[0:FS, :] = ag[...].astype(BF16)
            dffn_ref[FS:2 * FS, :] = au[...].astype(BF16)
            dffn_ref[2 * FS:, :] = ad[...].astype(BF16)

    wspec = lambda r, k: pl.BlockSpec((None, None, r, D), lambda s, i: (s, idx, k, 0), pipeline_mode=pl.Buffered(1))
    last = lambda s, i: (jnp.where(s == NCHIP - 1, i, 0), 0)
    nb = 0 if gbuf is None else 1
    act = pl.BlockSpec((None, tm, FS), lambda s, i: (s, i, 0))
    tok = pl.BlockSpec((tm, D), lambda s, i: (i, 0))
    return pl.pallas_call(
        lambda *refs: body(*refs[:9], *refs[9 + nb:]), grid=(NCHIP, ni),
        in_specs=[pl.BlockSpec((tm, D), last), tok, _full((1, D)), wspec(FS, 0), wspec(FS, 1), wspec(FS, 2), act, act,
                  tok] + [ANY] * nb,
        out_specs=[pl.BlockSpec((tm, D), last), _full((1, D)), wspec(3 * FS, 0)],
        out_shape=[_sds((T, D)), _sds((1, D)), _sds(ffn.shape, BF16)],
        input_output_aliases={9 + k: 2 + k for k in range(nb)},
        scratch_shapes=[pltpu.VMEM((T, D), F32)] + [pltpu.VMEM((FS, D), F32)] * 3,
        name=name, compiler_params=_cp("arbitrary", "arbitrary"),
    )(x, dy, nw, ffn, ffn, ffn, *pre, *(() if gbuf is None else (gbuf,)))


CONV_ROWS = 256


def _conv_pad(k):
    return 8 * ((k - 1 + 7) // 8)


def _conv_fwd(name, x, w, b, act):
    k_w, c = w.shape
    tc = 256 if c % 256 == 0 else LANES
    pad = _conv_pad(k_w)
    has_b = b is not None

    def body(*refs):
        x_ref, w_ref = refs[0], refs[1]
        b_ref = refs[2] if has_b else None
        y_ref, xp = refs[2 + has_b], refs[3 + has_b]
        xp[0:pad, :] = jnp.zeros((pad, tc), F32)
        xp[pad:, :] = x_ref[...]

        def step(t, carry):
            base = pl.multiple_of(t * CONV_ROWS, CONV_ROWS)
            win = xp[pl.ds(base, CONV_ROWS + pad), :]
            acc = jnp.zeros((CONV_ROWS, tc), F32)
            for k in range(k_w):
                o = pad - (k_w - 1) + k
                acc = acc + w_ref[k:k + 1, :] * win[o:o + CONV_ROWS, :]
            if has_b:
                acc = acc + b_ref[...]
            y_ref[pl.ds(base, CONV_ROWS), :] = _silu(acc) if act else acc
            return carry

        lax.fori_loop(0, T // CONV_ROWS, step, 0)

    col = lambda r: pl.BlockSpec((r, tc), lambda j: (0, j))
    ins = [x, w] + ([b] if has_b else [])
    return pl.pallas_call(
        body, grid=(c // tc,), in_specs=[col(T), col(k_w)] + ([col(1)] if has_b else []), out_specs=col(T),
        out_shape=_sds((T, c)), scratch_shapes=[pltpu.VMEM((T + pad, tc), F32)], name=name,
        compiler_params=_cp("parallel"),
    )(*ins)


def _conv_bwd(name, x, w, b, act, dy):
    k_w, c = w.shape
    tc = 256 if c % 256 == 0 else LANES
    pad = _conv_pad(k_w)
    has_b = b is not None

    def body(*refs):
        x_ref, w_ref, dy_ref = refs[0], refs[1], refs[2]
        b_ref = refs[3] if has_b else None
        dx_ref, dw_ref, db_ref, xp, dp = refs[3 + has_b:]
        xp[0:pad, :] = jnp.zeros((pad, tc), F32)
        xp[pad:, :] = x_ref[...]
        dp[T:, :] = jnp.zeros((pad, tc), F32)
        dw_ref[...] = jnp.zeros_like(dw_ref)
        db_ref[...] = jnp.zeros_like(db_ref)

        def step1(t, carry):
            base = pl.multiple_of(t * CONV_ROWS, CONV_ROWS)
            d = dy_ref[pl.ds(base, CONV_ROWS), :]
            win = xp[pl.ds(base, CONV_ROWS + pad), :]
            offs = [pad - (k_w - 1) + k for k in range(k_w)]
            if act:
                acc = jnp.zeros((CONV_ROWS, tc), F32)
                for k, o in enumerate(offs):
                    acc = acc + w_ref[k:k + 1, :] * win[o:o + CONV_ROWS, :]
                if has_b:
                    acc = acc + b_ref[...]
                sg = jax.nn.sigmoid(acc)
                d = d * (sg * (1.0 + acc * (1.0 - sg)))
            dp[pl.ds(base, CONV_ROWS), :] = d
            for k, o in enumerate(offs):
                dw_ref[k:k + 1, :] += jnp.sum(d * win[o:o + CONV_ROWS, :], axis=0, keepdims=True)
            db_ref[...] += jnp.sum(d, axis=0, keepdims=True)
            return carry

        lax.fori_loop(0, T // CONV_ROWS, step1, 0)

        def step2(t, carry):
            base = pl.multiple_of(t * CONV_ROWS, CONV_ROWS)
            win = dp[pl.ds(base, CONV_ROWS + pad), :]
            acc = jnp.zeros((CONV_ROWS, tc), F32)
            for k in range(k_w):
                o = (k_w - 1) - k
                acc = acc + w_ref[k:k + 1, :] * win[o:o + CONV_ROWS, :]
            dx_ref[pl.ds(base, CONV_ROWS), :] = acc
            return carry

        lax.fori_loop(0, T // CONV_ROWS, step2, 0)

    col = lambda r: pl.BlockSpec((r, tc), lambda j: (0, j))
    ins = [x, w, dy] + ([b] if has_b else [])
    return pl.pallas_call(
        body, grid=(c // tc,), in_specs=[col(T), col(k_w), col(T)] + ([col(1)] if has_b else []),
        out_specs=[col(T), col(k_w), col(1)], out_shape=[_sds((T, c)), _sds((k_w, c)), _sds((1, c))],
        scratch_shapes=[pltpu.VMEM((T + pad, tc), F32), pltpu.VMEM((T + pad, tc), F32)], name=name,
        compiler_params=_cp("parallel"),
    )(*ins)


def _attn_consts(n):
    i = _iota2((BLOCK, 2 * BLOCK), 0)
    j = _iota2((BLOCK, 2 * BLOCK), 1)
    dist = i + BLOCK - j
    valid = (dist >= 0) & (dist < WINDOW) & ((n > 0) | (j >= BLOCK))
    return dist.astype(F32), valid


def _attn_block(q4, kk, vv, sinks, dist, valid, kv):
    outs = []
    lane = _iota2((1, HEADS), 1)
    for g in range(GROUP):
        h = kv * GROUP + g
        slope = 2.0 ** (-8.0 * (h + 1) / HEADS)
        s = _nt(q4[:, g * HDIM:(g + 1) * HDIM], kk) * (HDIM ** -0.5)
        s = jnp.where(valid, s - slope * dist, -1e30)
        sink = jnp.sum(jnp.where(lane == h, sinks, 0.0), axis=1, keepdims=True)
        m = jnp.maximum(jnp.max(s, axis=-1, keepdims=True), sink)
        e = jnp.exp(s - m)
        p = e / (jnp.sum(e, axis=-1, keepdims=True) + jnp.exp(sink - m))
        outs.append(_nn(p, vv))
    return tuple(outs)


def _attn_fwd(name, qa, ka, va, sinks):
    def body(q_ref, k_ref, v_ref, s_ref, o_ref, kp, vp):
        kp[0:BLOCK, :] = jnp.zeros((BLOCK, KV_A), F32)
        vp[0:BLOCK, :] = jnp.zeros((BLOCK, KV_A), F32)
        kp[BLOCK:, :] = k_ref[...]
        vp[BLOCK:, :] = v_ref[...]
        sinks_v = s_ref[...]

        def step(n, carry):
            r = pl.multiple_of(n * BLOCK, BLOCK)
            dist, valid = _attn_consts(n)
            k2 = kp[pl.ds(r, 2 * BLOCK), :]
            v2 = vp[pl.ds(r, 2 * BLOCK), :]
            for kv in range(KV_HEADS):
                q4 = q_ref[pl.ds(r, BLOCK), kv * GROUP * HDIM:(kv + 1) * GROUP * HDIM]
                og = _attn_block(q4, k2[:, kv * HDIM:(kv + 1) * HDIM], v2[:, kv * HDIM:(kv + 1) * HDIM], sinks_v,
                                 dist, valid, kv)
                for g in range(GROUP):
                    h = kv * GROUP + g
                    o_ref[pl.ds(r, BLOCK), h * HDIM:(h + 1) * HDIM] = og[g]
            return carry

        lax.fori_loop(0, T // BLOCK, step, 0)

    return pl.pallas_call(
        body, out_shape=_sds((T, Q_A)),
        scratch_shapes=[pltpu.VMEM((T + BLOCK, KV_A), F32), pltpu.VMEM((T + BLOCK, KV_A), F32)], name=name,
        compiler_params=pltpu.CompilerParams(vmem_limit_bytes=VMEM_LIMIT),
    )(qa, ka, va, sinks)


def _attn_bwd(name, qa, ka, va, sinks, do):
    def body(q_ref, k_ref, v_ref, s_ref, do_ref, dq_ref, dk_ref, dv_ref, ds_ref, kp, vp, dkp, dvp):
        kp[0:BLOCK, :] = jnp.zeros((BLOCK, KV_A), F32)
        vp[0:BLOCK, :] = jnp.zeros((BLOCK, KV_A), F32)
        kp[BLOCK:, :] = k_ref[...]
        vp[BLOCK:, :] = v_ref[...]
        dkp[...] = jnp.zeros_like(dkp)
        dvp[...] = jnp.zeros_like(dvp)
        ds_ref[...] = jnp.zeros_like(ds_ref)
        sinks_v = s_ref[...]

        def step(n, carry):
            r = pl.multiple_of(n * BLOCK, BLOCK)
            dist, valid = _attn_consts(n)
            k2 = kp[pl.ds(r, 2 * BLOCK), :]
            v2 = vp[pl.ds(r, 2 * BLOCK), :]
            for kv in range(KV_HEADS):
                cols = slice(kv * HDIM, (kv + 1) * HDIM)
                q4 = q_ref[pl.ds(r, BLOCK), kv * GROUP * HDIM:(kv + 1) * GROUP * HDIM]
                _, vjp = jax.vjp(lambda q, k, v, s: _attn_block(q, k, v, s, dist, valid, kv),
                                 q4, k2[:, cols], v2[:, cols], sinks_v)
                cts = tuple(do_ref[pl.ds(r, BLOCK), (kv * GROUP + g) * HDIM:(kv * GROUP + g + 1) * HDIM]
                            for g in range(GROUP))
                dq4, dkk, dvv, dsk = vjp(cts)
                dq_ref[pl.ds(r, BLOCK), kv * GROUP * HDIM:(kv + 1) * GROUP * HDIM] = dq4
                dkp[pl.ds(r, 2 * BLOCK), cols] += dkk
                dvp[pl.ds(r, 2 * BLOCK), cols] += dvv
                ds_ref[...] += dsk
            return carry

        lax.fori_loop(0, T // BLOCK, step, 0)
        dk_ref[...] = dkp[BLOCK:, :]
        dv_ref[...] = dvp[BLOCK:, :]

    pad = lambda: pltpu.VMEM((T + BLOCK, KV_A), F32)
    return pl.pallas_call(
        body, out_shape=[_sds((T, Q_A)), _sds((T, KV_A)), _sds((T, KV_A)), _sds((1, HEADS))],
        scratch_shapes=[pad(), pad(), pad(), pad()], name=name,
        compiler_params=pltpu.CompilerParams(vmem_limit_bytes=VMEM_LIMIT),
    )(qa, ka, va, sinks, do)


def _dn_consts():
    i = _iota2((CHUNK, CHUNK), 0)
    j = _iota2((CHUNK, CHUNK), 1)
    return dict(causal=i >= j, strict=i > j, eye=(i == j).astype(F32), ltri=(i >= j).astype(F32),
                ones=jnp.ones((CHUNK, CHUNK), F32), last=(_iota2((CHUNK, 1), 0) == CHUNK - 1).astype(F32))


def _l2norm(x):
    return x * lax.rsqrt(jnp.sum(x * x, axis=-1, keepdims=True) + EPS)


def _head_cols(m):
    lane = _iota2((1, HEADS), 1)
    return jnp.concatenate([jnp.sum(jnp.where(lane == h, m, 0.0), axis=1, keepdims=True)[None]
                            for h in range(HEADS)], axis=0)


@jax.custom_vjp
def _unit_lower_inverse(low, known):
    if known is not None:
        return known
    inv = (_iota2((CHUNK, CHUNK), 0) == _iota2((CHUNK, CHUNK), 1)).astype(F32) - low
    pw = low
    for _ in range(5):
        pw = _dg(pw, pw, 1, 0, True)
        inv = inv + _dg(inv, pw, 1, 0, True)
    return inv


def _unit_lower_inverse_fwd(low, known):
    inv = _unit_lower_inverse(low, known)
    return inv, (inv, known)


def _unit_lower_inverse_bwd(res, g):
    inv, known = res
    d_low = -_dg(inv, _dg(g, inv, 1, 1, True), 0, 0, True)
    return d_low, (None if known is None else jnp.zeros_like(known))


_unit_lower_inverse.defvjp(_unit_lower_inverse_fwd, _unit_lower_inverse_bwd)


def _dn_local(q3, k3, v3, braw, araw, alog, dtb, cs, known_inv=None):
    q = _l2norm(q3) * (HDIM ** -0.5)
    k = _l2norm(k3)
    g = -jnp.exp(alog) * jax.nn.softplus(araw + dtb)
    gc_all = _nn_hi(cs["ltri"], g)
    egc_all = jnp.exp(gc_all)
    beta, gc, egc = _head_cols(jax.nn.sigmoid(braw)), _head_cols(gc_all), _head_cols(egc_all)
    a = jnp.broadcast_to(gc, (HEADS, CHUNK, CHUNK))
    diff = a - jnp.swapaxes(a, 1, 2)
    decay = jnp.where(cs["causal"], jnp.exp(jnp.where(cs["causal"], diff, 0.0)), 0.0)
    kb = k * beta
    low = jnp.where(cs["strict"], _nt(kb, k) * decay, 0.0)
    inv = _unit_lower_inverse(low, known_inv)
    u = _nn_hi(inv, v3 * beta)
    w = _nn_hi(inv, kb * egc)
    attn = _nt(q, k) * decay
    gc_last = jnp.sum(gc * cs["last"], axis=1, keepdims=True)
    return u, w, attn, q * egc, k * jnp.exp(gc_last - gc), egc_all, inv


def _heads3(ref, off=0):
    return jnp.concatenate([ref[:, off + h * HDIM:off + (h + 1) * HDIM][None] for h in range(HEADS)], axis=0)


def _dn_local_fwd(name, qkv, ba, alog, dtb):
    def body(qkv_ref, ba_ref, al_ref, dt_ref, u_ref, w_ref, at_ref, qd_ref, kd_ref, eg_ref, inv_ref):
        bav = ba_ref[...]
        outs = _dn_local(_heads3(qkv_ref), _heads3(qkv_ref, 512), _heads3(qkv_ref, 1024), bav[:, :HEADS],
                         bav[:, HEADS:], al_ref[...], dt_ref[...], _dn_consts())
        for r, o in zip((u_ref, w_ref, at_ref, qd_ref, kd_ref, inv_ref), outs[:5] + outs[6:]):
            _unheads(r, o)
        eg_ref[...] = outs[5]

    row = lambda w_: pl.BlockSpec((CHUNK, w_), lambda n: (n, 0))
    return pl.pallas_call(
        body, grid=(NCHUNK,), in_specs=[row(QKV_B), row(2 * HEADS), _full((1, HEADS)), _full((1, HEADS))],
        out_specs=[row(V_B)] * 5 + [row(HEADS), row(V_B)],
        out_shape=[_sds((T, V_B))] * 5 + [_sds((T, HEADS)), _sds((T, V_B))], name=name,
        compiler_params=_cp("parallel"),
    )(qkv, ba, alog, dtb)


def _dn_local_bwd(name, qkv, ba, alog, dtb, inv, cts):
    def body(qkv_ref, ba_ref, al_ref, dt_ref, inv_ref, du_ref, dw_ref, dat_ref, dqd_ref, dkd_ref, deg_ref,
             dqkv_ref, dba_ref, dal_ref, ddt_ref):
        @pl.when(pl.program_id(0) == 0)
        def _():
            dal_ref[...] = jnp.zeros_like(dal_ref)
            ddt_ref[...] = jnp.zeros_like(ddt_ref)

        cs = _dn_consts()
        bav = ba_ref[...]
        known = _heads3(inv_ref)
        _, vjp = jax.vjp(lambda *a: _dn_local(*a, cs, known)[:6], _heads3(qkv_ref), _heads3(qkv_ref, 512),
                         _heads3(qkv_ref, 1024), bav[:, :HEADS], bav[:, HEADS:], al_ref[...], dt_ref[...])
        dq, dk, dv, dbr, dar, dal, ddt = vjp((_heads3(du_ref), _heads3(dw_ref), _heads3(dat_ref), _heads3(dqd_ref),
                                              _heads3(dkd_ref), deg_ref[...]))
        for h in range(HEADS):
            dqkv_ref[:, h * HDIM:(h + 1) * HDIM] = dq[h]
            dqkv_ref[:, 512 + h * HDIM:512 + (h + 1) * HDIM] = dk[h]
            dqkv_ref[:, 1024 + h * HDIM:1024 + (h + 1) * HDIM] = dv[h]
        dba_ref[:, :HEADS] = dbr
        dba_ref[:, HEADS:] = dar
        dal_ref[...] += dal
        ddt_ref[...] += ddt

    row = lambda w_: pl.BlockSpec((CHUNK, w_), lambda n: (n, 0))
    return pl.pallas_call(
        body, grid=(NCHUNK,),
        in_specs=[row(QKV_B), row(2 * HEADS), _full((1, HEADS)), _full((1, HEADS))] + [row(V_B)] * 6 + [row(HEADS)],
        out_specs=[row(QKV_B), row(2 * HEADS), _full((1, HEADS)), _full((1, HEADS))],
        out_shape=[_sds((T, QKV_B)), _sds((T, 2 * HEADS)), _sds((1, HEADS)), _sds((1, HEADS))], name=name,
        compiler_params=_cp("arbitrary"),
    )(qkv, ba, alog, dtb, inv, *cts)


def _dn_step(s, u, w, attn, qd, kd, egc, z, nw):
    last = (_iota2((CHUNK, 1), 0) == CHUNK - 1).astype(F32)
    gl = jnp.sum(_head_cols(egc) * last, axis=1, keepdims=True)
    v_new = u - _nn(w, s)
    o = _nn(qd, s) + _nn(attn, v_new)
    s_new = s * gl + _tn(kd, v_new)
    return s_new, _rms(o, nw) * _silu(z)


def _unheads(ref, v3):
    for h in range(HEADS):
        ref[:, h * HDIM:(h + 1) * HDIM] = v3[h]


def _dn_rec_fwd(name, u, w, attn, qd, kd, egc, z, nw):
    def body(u_ref, w_ref, at_ref, qd_ref, kd_ref, eg_ref, z_ref, nw_ref, o_ref, ss_ref, s_scr):
        @pl.when(pl.program_id(0) == 0)
        def _():
            s_scr[...] = jnp.zeros_like(s_scr)

        s = s_scr[...]
        ss_ref[...] = s
        s_new, on = _dn_step(s, _heads3(u_ref), _heads3(w_ref), _heads3(at_ref), _heads3(qd_ref), _heads3(kd_ref),
                             eg_ref[...], _heads3(z_ref), nw_ref[...])
        s_scr[...] = s_new
        _unheads(o_ref, on)

    row = lambda w_: pl.BlockSpec((CHUNK, w_), lambda n: (n, 0))
    return pl.pallas_call(
        body, grid=(NCHUNK,), in_specs=[row(V_B)] * 5 + [row(HEADS), row(V_B), _full((1, HDIM))],
        out_specs=[row(V_B), pl.BlockSpec((None, HEADS, HDIM, HDIM), lambda n: (n, 0, 0, 0))],
        out_shape=[_sds((T, V_B)), _sds((NCHUNK, HEADS, HDIM, HDIM))],
        scratch_shapes=[pltpu.VMEM((HEADS, HDIM, HDIM), F32)], name=name, compiler_params=_cp("arbitrary"),
    )(u, w, attn, qd, kd, egc, z, nw)


def _dn_rec_bwd(name, u, w, attn, qd, kd, egc, z, nw, ss, do):
    def body(u_ref, w_ref, at_ref, qd_ref, kd_ref, eg_ref, z_ref, nw_ref, ss_ref, do_ref,
             du_ref, dw_ref, dat_ref, dqd_ref, dkd_ref, deg_ref, dz_ref, dnw_ref, ds_scr):
        @pl.when(pl.program_id(0) == 0)
        def _():
            ds_scr[...] = jnp.zeros_like(ds_scr)
            dnw_ref[...] = jnp.zeros_like(dnw_ref)

        _, vjp = jax.vjp(_dn_step, ss_ref[...], _heads3(u_ref), _heads3(w_ref), _heads3(at_ref), _heads3(qd_ref),
                         _heads3(kd_ref), eg_ref[...], _heads3(z_ref), nw_ref[...])
        ds, du, dw, dat, dqd, dkd, deg, dz, dnw = vjp((ds_scr[...], _heads3(do_ref)))
        ds_scr[...] = ds
        for r, v in zip((du_ref, dw_ref, dat_ref, dqd_ref, dkd_ref, dz_ref), (du, dw, dat, dqd, dkd, dz)):
            _unheads(r, v)
        deg_ref[...] = deg
        dnw_ref[...] += dnw

    row = lambda w_: pl.BlockSpec((CHUNK, w_), lambda n: (NCHUNK - 1 - n, 0))
    return pl.pallas_call(
        body, grid=(NCHUNK,),
        in_specs=[row(V_B)] * 5 + [row(HEADS), row(V_B), _full((1, HDIM)),
                                   pl.BlockSpec((None, HEADS, HDIM, HDIM), lambda n: (NCHUNK - 1 - n, 0, 0, 0)),
                                   row(V_B)],
        out_specs=[row(V_B)] * 5 + [row(HEADS), row(V_B), _full((1, HDIM))],
        out_shape=[_sds((T, V_B))] * 5 + [_sds((T, HEADS)), _sds((T, V_B)), _sds((1, HDIM))],
        scratch_shapes=[pltpu.VMEM((HEADS, HDIM, HDIM), F32)], name=name, compiler_params=_cp("arbitrary"),
    )(u, w, attn, qd, kd, egc, z, nw, ss, do)


def _final(name, x, fw, target, tm=512):
    def body(x_ref, fw_ref, t_ref, l_ref, dx_ref, dfw_ref):
        @pl.when(pl.program_id(0) == 0)
        def _():
            l_ref[...] = jnp.zeros_like(l_ref)
            dfw_ref[...] = jnp.zeros_like(dfw_ref)

        tv = t_ref[...]

        def f(xv, fwv):
            err = _rms(xv, fwv) - tv
            per_tok = jnp.mean(err * err, axis=-1, keepdims=True)
            return 0.5 * jnp.sum(per_tok, axis=0, keepdims=True)

        loss, vjp = jax.vjp(f, x_ref[...], fw_ref[...])
        dx, dfw = vjp(jnp.ones((1, 1), F32))
        l_ref[...] += loss
        dx_ref[...] = dx
        dfw_ref[...] += dfw

    tok = pl.BlockSpec((tm, D), lambda i: (i, 0))
    return pl.pallas_call(
        body, grid=(T // tm,), in_specs=[tok, _full((1, D)), tok], out_specs=[_full((1, 1)), tok, _full((1, D))],
        out_shape=[_sds((1, 1)), _sds((T, D)), _sds((1, D))], name=name, compiler_params=_cp("arbitrary"),
    )(x, fw, target)


def _m1_pre(tv, sv):
    return [_rms(tv[0], sv[0])]


def _m1_post(ys, tv, sv):
    return (jnp.concatenate(ys, axis=1),)


def _m1_post_split(ys, tv, sv):
    proj = jnp.concatenate(ys, axis=1)
    return tuple(proj[:, a:b] for a, b in zip(IN_SPLITS[:-1], IN_SPLITS[1:]))


def _m5_pre(tv, sv):
    return [tv[1], tv[2]]


def _m5_post(ys, tv, sv):
    return (tv[0] + ys[0] + ys[1],)


def _c1_pre(tv, sv):
    return [_rms(tv[0], sv[0])]


def _c1_post(ys, tv, sv):
    return ((jnp.concatenate(ys[:2], axis=1) + sv[1]) * jax.nn.sigmoid(jnp.concatenate(ys[2:], axis=1) + sv[2]),)


def _c3_pre(tv, sv):
    return [_silu(_layernorm(tv[0], sv[0], sv[1]))]


def _c3_post(ys, tv, sv):
    return (tv[1] + ys[0] + sv[2],)


def _row(v):
    return v.reshape(1, -1)


def _mixer_fwd(tag, x, p):
    parts = _blk_fwd(f"m1_fwd_{tag}", _m1_pre, [0], _m1_post_split, [x], [p["nw"]], [p["w_in"]],
                     [(b - a, F32) for a, b in zip(IN_SPLITS[:-1], IN_SPLITS[1:])])
    qa, ka, va, qkvb, z, ba = parts
    att = _attn_fwd(f"attn_fwd_{tag}", qa, ka, va, p["sinks"])
    qkvc = _conv_fwd(f"dnconv_fwd_{tag}", qkvb, p["dn_conv_w"], None, True)
    *loc, inv = _dn_local_fwd(f"dnloc_fwd_{tag}", qkvc, ba, p["a_log"], p["dt_bias"])
    og, ss = _dn_rec_fwd(f"dnrec_fwd_{tag}", *loc, z, p["dn_norm_w"])
    (out,) = _blk_fwd(f"m5_fwd_{tag}", _m5_pre, [0, 1], _m5_post, [x, att, og], [], [p["wo_a"], p["wo_b"]],
                      [(D, F32)])
    return out, dict(x=x, qa=qa, ka=ka, va=va, qkvb=qkvb, z=z, ba=ba, att=att, qkvc=qkvc, loc=loc, inv=inv, og=og,
                     ss=ss)


def _mixer_bwd(tag, dy, p, s):
    (dxa, datt, dog), _, (dwo_a, dwo_b) = _blk_bwd(f"m5_bwd_{tag}", _m5_pre, [0, 1], _m5_post,
                                                   [s["x"], s["att"], s["og"]], [], [p["wo_a"], p["wo_b"]], [[dy]],
                                                   linear_post=True)
    rec = _dn_rec_bwd(f"dnrec_bwd_{tag}", *s["loc"], s["z"], p["dn_norm_w"], s["ss"], dog)
    dz, dnw_dn = rec[6], rec[7]
    dqkvc, dba, dalog, ddtb = _dn_local_bwd(f"dnloc_bwd_{tag}", s["qkvc"], s["ba"], p["a_log"], p["dt_bias"],
                                            s["inv"], rec[:6])
    dqkvb, dconvw, _ = _conv_bwd(f"dnconv_bwd_{tag}", s["qkvb"], p["dn_conv_w"], None, True, dqkvc)
    dqa, dka, dva, dsinks = _attn_bwd(f"attn_bwd_{tag}", s["qa"], s["ka"], s["va"], p["sinks"], datt)
    (dx,), (dnw,), (dw_in,) = _blk_bwd(f"m1_bwd_{tag}", _m1_pre, [0], _m1_post, [s["x"]], [p["nw"]], [p["w_in"]],
                                       [[dqa, dka, dva, dqkvb, dz, dba]], res=dxa, linear_post=True)
    return dx, dict(nw=dnw, w_in=dw_in, wo_a=dwo_a, wo_b=dwo_b, dn_conv_w=dconvw, sinks=dsinks, a_log=dalog,
                    dt_bias=ddtb, dn_norm_w=dnw_dn)


def _conformer_fwd(tag, x, p):
    (glu,) = _blk_fwd(f"c1_fwd_{tag}", _c1_pre, [0], _c1_post, [x], [p["nw"], p["b1a"], p["b1b"]], [p["w1"]],
                      [(D, F32)])
    cc = _conv_fwd(f"dwconv_fwd_{tag}", glu, p["w_dw"], p["b_dw"], False)
    (out,) = _blk_fwd(f"c3_fwd_{tag}", _c3_pre, [0], _c3_post, [cc, x], [p["ln_w"], p["ln_b"], p["b2"]], [p["w2"]],
                      [(D, F32)])
    return out, dict(x=x, glu=glu, cc=cc)


def _conformer_bwd(tag, dy, p, s):
    (dcc, dxa), (dlnw, dlnb, db2), (dw2,) = _blk_bwd(f"c3_bwd_{tag}", _c3_pre, [0], _c3_post, [s["cc"], s["x"]],
                                                     [p["ln_w"], p["ln_b"], p["b2"]], [p["w2"]], [[dy]],
                                                     linear_post=True)
    dglu, dwdw, dbdw = _conv_bwd(f"dwconv_bwd_{tag}", s["glu"], p["w_dw"], p["b_dw"], False, dcc)
    (dx,), (dnw, db1a, db1b), (dw1,) = _blk_bwd(f"c1_bwd_{tag}", _c1_pre, [0], _c1_post, [s["x"]],
                                                [p["nw"], p["b1a"], p["b1b"]], [p["w1"]], [[dglu]], res=dxa)
    return dx, dict(nw=dnw, b1a=db1a, b1b=db1b, w1=dw1, w_dw=dwdw, b_dw=dbdw, ln_w=dlnw, ln_b=dlnb, b2=db2, w2=dw2)


def _layer_fwd(l, x, nw, ffn, p):
    x1, *pre_a = _ffn_fwd(f"ffn_fwd_{l}a", x, _row(nw[0]), ffn, 0)
    p = dict(p, nw=_row(nw[1]))
    x2, sv = (_mixer_fwd if l % 2 == 0 else _conformer_fwd)(str(l), x1, p)
    out, *pre_b = _ffn_fwd(f"ffn_fwd_{l}b", x2, _row(nw[2]), ffn, 1)
    return out, (x, x2, p, sv, pre_a, pre_b)


def _layer_bwd(l, dx, nw, ffn, saved, after_first=lambda dx: dx):
    x0, x2, p, sv, pre_a, pre_b = saved
    dx, dn2, dffn = _ffn_bwd(f"ffn_bwd_{l}b", x2, _row(nw[2]), ffn, 1, pre_b, dx)
    dx = after_first(dx)
    dx, dmix = (_mixer_bwd if l % 2 == 0 else _conformer_bwd)(str(l), dx, p, sv)
    dx, dn0, dffn = _ffn_bwd(f"ffn_bwd_{l}a", x0, _row(nw[0]), ffn, 0, pre_a, dx, dffn)
    return dx, jnp.concatenate([dn0, dmix.pop("nw"), dn2], axis=0), dffn, dmix


def _place():
    x, y, c = lax.axis_index("x"), lax.axis_index("y"), lax.axis_index("c")
    chips = [(1 - x, y), (x, 1 - y), (1 - x, 1 - y)]
    return x, y, c, 2 * x + y, chips, [2 * px + py for px, py in chips]


def _handshake(peers):
    barrier = pltpu.get_barrier_semaphore()
    for p in peers:
        pl.semaphore_signal(barrier, inc=1, device_id=p, device_id_type=MESH)
    pl.semaphore_wait(barrier, len(peers))


def _chip_peers():
    x, y, c, _, chips, _ = _place()
    return [(*chip, c) for chip in chips] + [(x, y, 1 - c)]


def _gather_copies(ins, outs, nb, send, recv, fsend, frecv, lsem):
    n_in = len(ins)
    x, y, c, me, chips, cidx = _place()
    sib = (x, y, 1 - c)
    local = [pltpu.make_async_copy(ins[a], outs[a].at[me], lsem.at[a]) for a in range(n_in)]
    for cp in local:
        cp.start()

    def region(a, k, who):
        if k < 2:
            return outs[a].at[cidx[k], pl.ds(who, 1)]
        r = ins[a].shape[1] // 2
        return outs[a].at[cidx[2], pl.ds(who, 1), pl.ds((k - 2) * r, r)]

    def hop(a, k):
        if k < 2:
            src, dst = ins[a].at[pl.ds(c, 1)], outs[a].at[me, pl.ds(c, 1)]
        else:
            r = ins[a].shape[1] // 2
            src = dst = outs[a].at[cidx[3 - k], pl.ds(c, 1), pl.ds((k - 2) * r, r)]
        return pltpu.make_async_remote_copy(src, dst, send.at[4 * a + k], recv.at[4 * a + k],
                                            device_id=(*chips[k % 2], c), device_id_type=MESH)

    def landed(a, k):
        dst = region(a, k, c)
        return pltpu.make_async_remote_copy(dst, dst, send.at[4 * a + k], recv.at[4 * a + k],
                                            device_id=(*chips[k % 2], c), device_id_type=MESH)

    def passed(a, k, who):
        part = region(a, k, who)
        return pltpu.make_async_remote_copy(part, part, fsend.at[4 * a + k], frecv.at[4 * a + k], device_id=sib,
                                            device_id_type=MESH)

    def direct(a, j):
        k = 4 * nb + 3 * (a - nb) + j
        return pltpu.make_async_remote_copy(ins[a], outs[a].at[me], send.at[k], recv.at[k],
                                            device_id=(*chips[j], c), device_id_type=MESH)

    def direct_landed(a, j):
        k = 4 * nb + 3 * (a - nb) + j
        dst = outs[a].at[cidx[j]]
        return pltpu.make_async_remote_copy(dst, dst, send.at[k], recv.at[k], device_id=(*chips[j], c),
                                            device_id_type=MESH)

    sends = [hop(a, k) for a in range(nb) for k in range(2)] + [direct(a, j) for a in range(nb, n_in) for j in range(3)]
    for cp in sends:
        cp.start()
    for a in range(nb):
        for k in (1, 0):
            landed(a, k).wait_recv()
            for cp in (hop(a, 3 - k), passed(a, k, c)):
                cp.start()
                sends.append(cp)
    for a in range(nb):
        for k in (2, 3):
            landed(a, k).wait_recv()
            cp = passed(a, k, c)
            cp.start()
            sends.append(cp)
    for a in range(nb, n_in):
        for j in range(3):
            direct_landed(a, j).wait_recv()
    for a in range(nb):
        for k in range(4):
            passed(a, k, 1 - c).wait_recv()
    for cp in sends:
        cp.wait_send()
    for cp in local:
        cp.wait()


def _gather_sems(n_in, nb):
    dma = pltpu.SemaphoreType.DMA
    n_ici = 4 * nb + 3 * (n_in - nb)
    return [dma((n_ici,)), dma((n_ici,)), dma((4 * nb,)), dma((4 * nb,)), dma((n_in,))]


def _gather_async(name, halved, whole=()):
    nb, arrs = len(halved), list(halved) + list(whole)
    hbm = pltpu.MemorySpace.HBM
    ins = [jax.new_ref(a, memory_space=hbm) for a in arrs]
    outs = [jax.empty_ref(_sds((NCHIP,) + a.shape, a.dtype), memory_space=hbm) for a in arrs]

    @pl.kernel(mesh=plsc.ScalarSubcoreMesh(axis_name="seq", num_cores=1), name=name,
               scratch_types=tuple(_gather_sems(len(arrs), nb)),
               compiler_params=pltpu.CompilerParams(collective_id=2))
    def launch(send, recv, fsend, frecv, lsem):
        _handshake(_chip_peers())
        _gather_copies(ins, outs, nb, send, recv, fsend, frecv, lsem)

    launch()
    return outs


def _swap_halves(name, grads, after=None):
    n = len(grads)
    hbm = pltpu.MemorySpace.HBM
    ins = [jax.new_ref(g, memory_space=hbm) for g in grads]
    outs = [jax.empty_ref(_sds((NCHIP, g.shape[1] // 2) + g.shape[2:], g.dtype), memory_space=hbm) for g in grads]
    tile = (2 * 8, LANES)
    token = None if after is None else jax.empty_ref(_sds(tile, BF16), memory_space=hbm)

    @pl.kernel(mesh=plsc.ScalarSubcoreMesh(axis_name="seq", num_cores=1), name=name,
               scratch_types=(pltpu.SemaphoreType.DMA((n + 1,)), pltpu.SemaphoreType.DMA((n,))),
               compiler_params=pltpu.CompilerParams(collective_id=1))
    def launch(send, recv):
        x, y, c, _, _, _ = _place()
        sib = (x, y, 1 - c)
        _handshake([sib])
        if after is not None:
            tick = pltpu.make_async_copy(after.at[0, 0, 0, pl.ds(0, tile[0]), pl.ds(0, tile[1])], token, send.at[n])
            tick.start()
            tick.wait()
        cps = []
        for a in range(n):
            h = grads[a].shape[1] // 2
            cps.append(pltpu.make_async_remote_copy(ins[a].at[:, pl.ds((1 - c) * h, h)], outs[a], send.at[a],
                                                    recv.at[a], device_id=sib, device_id_type=MESH))
        for cp in cps:
            cp.start()
        for cp in cps:
            cp.wait()

    launch()
    return outs


def _row_tile(r, cap=256):
    return max(t for t in range(8, cap + 1, 8) if r % t == 0)


def _add_half(name, g, r, c_arr):
    _, l, rows, cols = g.shape
    h = l // 2
    tr = _row_tile(rows, 1056)

    def body(c_ref, g_ref, r_ref, o_ref):
        o_ref[...] = (g_ref[...].astype(F32) + r_ref[...].astype(F32)).astype(BF16)

    blk = (None, None, tr, cols)
    return pl.pallas_call(
        body,
        grid_spec=pltpu.PrefetchScalarGridSpec(
            num_scalar_prefetch=1, grid=(NCHIP, h, rows // tr),
            in_specs=[pl.BlockSpec(blk, lambda j, i, t, c_ref: (j, c_ref[0] * h + i, t, 0)),
                      pl.BlockSpec(blk, lambda j, i, t, c_ref: (j, i, t, 0))],
            out_specs=pl.BlockSpec(blk, lambda j, i, t, c_ref: (j, i, t, 0))),
        out_shape=_sds((NCHIP, h, rows, cols), BF16), name=name,
        compiler_params=_cp("parallel", "parallel", "parallel"),
    )(c_arr, g, r)


def _scatter_async(name, parts, sums, where):
    nb = len(parts)
    ins = [jax.new_ref(p, memory_space=pltpu.MemorySpace.HBM) for p in parts]
    dma = pltpu.SemaphoreType.DMA

    @pl.kernel(mesh=plsc.ScalarSubcoreMesh(axis_name="seq", num_cores=1), name=name,
               scratch_types=(dma((3 * nb,)), dma((3 * nb,)), dma((4 * nb,)), dma((4 * nb,)), dma((nb,))),
               compiler_params=pltpu.CompilerParams(collective_id=3))
    def launch(send, recv, fsend, frecv, lsem):
        _handshake(_chip_peers())
        x, y, c, me, chips, cidx = _place()
        sib = (x, y, 1 - c)

        def slot(a, half, chip):
            return sums[a].at[half, chip, pl.ds(where[a], 1)]

        local = [pltpu.make_async_copy(ins[a].at[me], slot(a, c, me), lsem.at[a]) for a in range(nb)]
        for cp in local:
            cp.start()

        def ici(a, j):
            return pltpu.make_async_remote_copy(ins[a].at[cidx[j]], slot(a, c, me), send.at[a * 3 + j],
                                                recv.at[a * 3 + j], device_id=(*chips[j], c), device_id_type=MESH)

        def landed(a, j):
            dst = slot(a, c, cidx[j])
            return pltpu.make_async_remote_copy(dst, dst, send.at[a * 3 + j], recv.at[a * 3 + j],
                                                device_id=(*chips[j], c), device_id_type=MESH)

        def passed(a, j, who):
            dst = slot(a, who, me if j == 3 else cidx[j])
            src = ins[a].at[me] if j == 3 else dst
            return pltpu.make_async_remote_copy(src, dst, fsend.at[a * 4 + j], frecv.at[a * 4 + j], device_id=sib,
                                                device_id_type=MESH)

        sends = [ici(a, j) for a in range(nb) for j in range(3)] + [passed(a, 3, c) for a in range(nb)]
        for cp in sends:
            cp.start()
        for a in range(nb):
            for j in range(3):
                landed(a, j).wait_recv()
                cp = passed(a, j, c)
                cp.start()
                sends.append(cp)
        for a in range(nb):
            for j in range(4):
                passed(a, j, 1 - c).wait_recv()
        for cp in sends:
            cp.wait_send()
        for cp in local:
            cp.wait()

    launch()


def _exchange_small(small, rep):
    def body(small_in, rep_in, small_out, rep_out, lsem, ssend, srecv):
        x, y, c, me, _, _ = _place()
        dev = 4 * x + 2 * y + c
        local = [pltpu.make_async_copy(small_in.at[me], small_out.at[dev], lsem.at[0]),
                 pltpu.make_async_copy(rep_in, rep_out.at[dev], lsem.at[1])]
        for cp in local:
            cp.start()

        def peer(r):
            return (1 - x if r & 4 else x), (1 - y if r & 2 else y), (1 - c if r & 1 else c)

        def tiny(r, which):
            px, py, pc = peer(r)
            k = (r - 1) * 2 + which
            if which == 0:
                return pltpu.make_async_remote_copy(small_in.at[2 * px + py], small_out.at[dev], ssend.at[k],
                                                    srecv.at[k], device_id=(px, py, pc), device_id_type=MESH)
            return pltpu.make_async_remote_copy(rep_in, rep_out.at[dev], ssend.at[k], srecv.at[k],
                                                device_id=(px, py, pc), device_id_type=MESH)

        def tiny_landed(r, which):
            px, py, pc = peer(r)
            k = (r - 1) * 2 + which
            dst = (small_out if which == 0 else rep_out).at[4 * px + 2 * py + pc]
            return pltpu.make_async_remote_copy(dst, dst, ssend.at[k], srecv.at[k], device_id=(px, py, pc),
                                                device_id_type=MESH)

        sends = [tiny(r, w) for r in range(1, NDEV) for w in range(2)]
        for cp in sends:
            cp.start()
        for r in range(1, NDEV):
            for w in range(2):
                tiny_landed(r, w).wait_recv()
        for cp in sends:
            cp.wait_send()
        for cp in local:
            cp.wait()

    dma = pltpu.SemaphoreType.DMA
    return pl.pallas_call(
        body, in_specs=[ANY] * 2, out_specs=[ANY] * 2,
        out_shape=[_sds((NDEV,) + small.shape[1:], F32), _sds((NDEV,) + rep.shape, F32)],
        scratch_shapes=[dma((2,)), dma((2 * (NDEV - 1),)), dma((2 * (NDEV - 1),))], name="exchange_small_grads",
    )(small, rep)


def _adamw_math(w, g, m, v):
    m = B1 * m + (1.0 - B1) * g
    v = B2 * v + (1.0 - B2) * (g * g)
    m_hat = m / (1.0 - B1 ** STEP)
    v_hat = v / (1.0 - B2 ** STEP)
    return -LR * (m_hat / (jnp.sqrt(v_hat) + AEPS) + WD * w), m, v


def _adamw_big(name, w, m, v, parts, row0=0, first=0, outs=None):
    _, _, rows, cols = w.shape
    n = parts.shape[2]
    tr = _row_tile(rows)
    t0 = row0 // tr

    def body(w_ref, m_ref, v_ref, p_ref, *rest):
        g_ref, d_ref, nm_ref, nv_ref = rest[-4:]
        g = p_ref[0].astype(F32)
        for q in range(1, NCHIP):
            g = g + p_ref[q].astype(F32)
        d, nm, nv = _adamw_math(w_ref[...], g, m_ref[...], v_ref[...])
        g_ref[...], d_ref[...], nm_ref[...], nv_ref[...] = g, d, nm, nv

    spec = pl.BlockSpec((None, None, tr, cols), lambda i, p, t: (first + i, p, t, 0))
    na = 0 if outs is None else 4
    return pl.pallas_call(
        body, grid=(n, 2, rows // tr),
        in_specs=[spec, spec, spec,
                  pl.BlockSpec((None, NCHIP, None, tr, cols), lambda i, p, t: (p, 0, i, t0 + t, 0))] + [ANY] * na,
        out_specs=[spec] * 4, out_shape=[_sds(w.shape)] * 4, input_output_aliases={4 + k: k for k in range(na)},
        name=name, compiler_params=_cp("parallel", "parallel", "parallel"),
    )(w, m, v, parts, *(outs or ()))


def _adamw_small(name, w, m, v, parts):
    def body(w_ref, m_ref, v_ref, p_ref, g_ref, d_ref, nm_ref, nv_ref):
        g = p_ref[0]
        for q in range(1, NDEV):
            g = g + p_ref[q]
        d, nm, nv = _adamw_math(w_ref[...], g, m_ref[...], v_ref[...])
        g_ref[...], d_ref[...], nm_ref[...], nv_ref[...] = g, d, nm, nv

    return pl.pallas_call(body, out_shape=[_sds(w.shape)] * 4, name=name)(w, m, v, parts)


def _pack(arrs, rows):
    flat = jnp.concatenate([a.reshape(-1) for a in arrs])
    return jnp.pad(flat, (0, rows * LANES - flat.shape[0])).reshape(rows, LANES)


def _unpack(packed, shapes):
    flat, out, o = packed.reshape(-1), [], 0
    for s in shapes:
        n = 1
        for d in s:
            n *= d
        out.append(flat[o:o + n].reshape(s))
        o += n
    return out


SMALL_ROWS, REP_ROWS = 200, 16


def kernel(x, norm_w, ffn_w_gate, ffn_w_up, ffn_w_down, mix_w_in, dn_conv_w, attn_sinks, dn_a_log, dn_dt_bias, dn_norm_w, mix_w_out, conv_w_pw1, conv_b_pw1, conv_w_dw, conv_b_dw, conv_ln_w, conv_ln_b, conv_w_pw2, conv_b_pw2, final_norm_w, loss_target, m_norm_w, m_ffn_w_gate, m_ffn_w_up, m_ffn_w_down, m_mix_w_in, m_dn_conv_w, m_attn_sinks, m_dn_a_log, m_dn_dt_bias, m_dn_norm_w, m_mix_w_out, m_conv_w_pw1, m_conv_b_pw1, m_conv_w_dw, m_conv_b_dw, m_conv_ln_w, m_conv_ln_b, m_conv_w_pw2, m_conv_b_pw2, m_final_norm_w, v_norm_w, v_ffn_w_gate, v_ffn_w_up, v_ffn_w_down, v_mix_w_in, v_dn_conv_w, v_attn_sinks, v_dn_a_log, v_dn_dt_bias, v_dn_norm_w, v_mix_w_out, v_conv_w_pw1, v_conv_b_pw1, v_conv_w_dw, v_conv_b_dw, v_conv_ln_w, v_conv_ln_b, v_conv_w_pw2, v_conv_b_pw2, v_final_norm_w):
    small_names = ["norm_w", "dn_conv_w", "conv_b_pw1", "conv_w_dw", "conv_b_dw", "conv_ln_w", "conv_ln_b",
                   "conv_b_pw2"]
    rep_names = ["attn_sinks", "dn_a_log", "dn_dt_bias", "dn_norm_w", "final_norm_w"]
    w = dict(norm_w=norm_w, ffn_w_gate=ffn_w_gate, ffn_w_up=ffn_w_up, ffn_w_down=ffn_w_down, mix_w_in=mix_w_in, dn_conv_w=dn_conv_w, attn_sinks=attn_sinks, dn_a_log=dn_a_log, dn_dt_bias=dn_dt_bias, dn_norm_w=dn_norm_w, mix_w_out=mix_w_out, conv_w_pw1=conv_w_pw1, conv_b_pw1=conv_b_pw1, conv_w_dw=conv_w_dw, conv_b_dw=conv_b_dw, conv_ln_w=conv_ln_w, conv_ln_b=conv_ln_b, conv_w_pw2=conv_w_pw2, conv_b_pw2=conv_b_pw2, final_norm_w=final_norm_w)
    m = dict(norm_w=m_norm_w, ffn_w_gate=m_ffn_w_gate, ffn_w_up=m_ffn_w_up, ffn_w_down=m_ffn_w_down, mix_w_in=m_mix_w_in, dn_conv_w=m_dn_conv_w, attn_sinks=m_attn_sinks, dn_a_log=m_dn_a_log, dn_dt_bias=m_dn_dt_bias, dn_norm_w=m_dn_norm_w, mix_w_out=m_mix_w_out, conv_w_pw1=m_conv_w_pw1, conv_b_pw1=m_conv_b_pw1, conv_w_dw=m_conv_w_dw, conv_b_dw=m_conv_b_dw, conv_ln_w=m_conv_ln_w, conv_ln_b=m_conv_ln_b, conv_w_pw2=m_conv_w_pw2, conv_b_pw2=m_conv_b_pw2, final_norm_w=m_final_norm_w)
    v = dict(norm_w=v_norm_w, ffn_w_gate=v_ffn_w_gate, ffn_w_up=v_ffn_w_up, ffn_w_down=v_ffn_w_down, mix_w_in=v_mix_w_in, dn_conv_w=v_dn_conv_w, attn_sinks=v_attn_sinks, dn_a_log=v_dn_a_log, dn_dt_bias=v_dn_dt_bias, dn_norm_w=v_dn_norm_w, mix_w_out=v_mix_w_out, conv_w_pw1=v_conv_w_pw1, conv_b_pw1=v_conv_b_pw1, conv_w_dw=v_conv_w_dw, conv_b_dw=v_conv_b_dw, conv_ln_w=v_conv_ln_w, conv_ln_b=v_conv_ln_b, conv_w_pw2=v_conv_w_pw2, conv_b_pw2=v_conv_b_pw2, final_norm_w=v_final_norm_w)
    order = ["norm_w", "ffn_w_gate", "ffn_w_up", "ffn_w_down", "mix_w_in", "dn_conv_w", "attn_sinks", "dn_a_log",
             "dn_dt_bias", "dn_norm_w", "mix_w_out", "conv_w_pw1", "conv_b_pw1", "conv_w_dw", "conv_b_dw",
             "conv_ln_w", "conv_ln_b", "conv_w_pw2", "conv_b_pw2", "final_norm_w"]

    small_shapes = [w[n].shape for n in small_names]
    rep_shapes = [w[n].shape for n in rep_names]

    def halves(a):
        return a.reshape(a.shape[:-2] + (2, a.shape[-2] // 2, a.shape[-1]))

    tr = lambda a: jnp.swapaxes(a, -1, -2)
    gate_t, up_t = tr(ffn_w_gate), tr(ffn_w_up)

    def layer_shards(l):
        mix_in, mix_out = (mix_w_in, mix_w_out) if l % 2 == 0 else (conv_w_pw1, conv_w_pw2)
        return [t.astype(BF16) for t in (jnp.concatenate([gate_t[l], up_t[l], ffn_w_down[l]], axis=1),
                                         halves(mix_in[l // 2]), halves(mix_out[l // 2]))]

    first = layer_shards(0) + [_pack([w[n] for n in small_names], SMALL_ROWS)]
    first, (gate_t, up_t, ffn_w_down, mix_w_in, mix_w_out, conv_w_pw1, conv_w_pw2) = lax.optimization_barrier(
        (first, (gate_t, up_t, ffn_w_down, mix_w_in, mix_w_out, conv_w_pw1, conv_w_pw2)))
    gathering = [_gather_async("gather_layer0", first[:3], first[3:])]
    gathering += [_gather_async(f"gather_layer{l}", layer_shards(l)) for l in range(1, DEPTH)]

    def mixer_params(l, w_a, w_b):
        e = l // 2
        w_a = w_a.reshape(NCHIP, D, -1)
        w_b = w_b.reshape(D, D)
        if l % 2 == 0:
            return dict(w_in=w_a, dn_conv_w=sm["dn_conv_w"][e], sinks=_row(attn_sinks[e]), a_log=_row(dn_a_log[e]),
                        dt_bias=_row(dn_dt_bias[e]), dn_norm_w=_row(dn_norm_w[e]), wo_a=w_b[:Q_A], wo_b=w_b[Q_A:])
        return dict(b1a=_row(sm["conv_b_pw1"][e, :D]), b1b=_row(sm["conv_b_pw1"][e, D:]), w1=w_a,
                    w_dw=sm["conv_w_dw"][e], b_dw=_row(sm["conv_b_dw"][e]), ln_w=_row(sm["conv_ln_w"][e]),
                    ln_b=_row(sm["conv_ln_b"][e]), b2=_row(sm["conv_b_pw2"][e]), w2=w_b)

    xs, saved, ffn_w = x[0], [], []
    for l in range(DEPTH):
        got = [r[...] for r in gathering[l]]
        if l == 0:
            per_chip = [_unpack(got[3][q], small_shapes) for q in range(NCHIP)]
            sm = {n: jnp.concatenate([per_chip[q][i] for q in range(NCHIP)], axis=-1)
                  for i, n in enumerate(small_names)}
        else:
            xs, got = lax.optimization_barrier((xs, got))
        ffn_w.append(got[0])
        xs, sv = _layer_fwd(l, xs, sm["norm_w"][l], got[0], mixer_params(l, got[1], got[2]))
        saved.append(sv)
    loss, dx, dfw = _final("final", xs, _row(final_norm_w), loss_target[0])

    hbm = pltpu.MemorySpace.HBM
    row_shapes = dict(ffn=(3 * FS, D), w_in=(D // 2, IN_COLS // NCHIP), w_out=(D // 8, D), pw1=(D // 2, D // 2),
                      pw2=(D // 8, D))
    new_sums = lambda k, n: jax.empty_ref(_sds((2, NCHIP, n) + row_shapes[k], BF16), memory_space=hbm)
    sums_0 = {k: new_sums(k, 1) for k in ("ffn", "w_in", "w_out")}
    sums = dict(ffn=new_sums("ffn", DEPTH - 1), w_in=new_sums("w_in", 1), w_out=new_sums("w_out", 1),
                pw1=new_sums("pw1", 2), pw2=new_sums("pw2", 2))
    c_arr = lax.axis_index("c").astype(jnp.int32).reshape(1)
    dnorm, gmix = [None] * DEPTH, [None] * DEPTH

    def hand_on(l, grads, swapped):
        def run(dx):
            dx, other = lax.optimization_barrier((dx, [r[...] for r in swapped]))
            parts = [_add_half(f"add_half_{l}_{k}", gg, rr, c_arr) for k, (gg, rr) in enumerate(zip(grads, other))]
            dx, parts = lax.optimization_barrier((dx, parts))
            keys = ("ffn", "w_in", "w_out") if l % 2 == 0 else ("ffn", "pw1", "pw2")
            if l == 0:
                _scatter_async("scatter_grads_0", parts, [sums_0[k] for k in keys], [0, 0, 0])
            else:
                _scatter_async(f"scatter_grads_{l}", parts, [sums[k] for k in keys],
                               [l - 1, 0, 0] if l % 2 == 0 else [l - 1, l // 2, l // 2])
            return dx
        return run

    pending = lambda dx: dx
    for l in reversed(range(DEPTH)):
        dx, dnorm[l], dffn, gmix[l] = _layer_bwd(l, dx, sm["norm_w"][l], ffn_w[l], saved[l], pending)
        if l % 2 == 0:
            g_a, g_b = gmix[l]["w_in"], jnp.concatenate([gmix[l]["wo_a"], gmix[l]["wo_b"]], axis=0)
        else:
            g_a, g_b = gmix[l]["w1"], gmix[l]["w2"]
        g_a = halves(g_a).astype(BF16)
        g_b = g_b.reshape(NCHIP, 2, D // 8, D).astype(BF16)
        dx, grads = lax.optimization_barrier((dx, [dffn, g_a, g_b]))
        pending = hand_on(l, grads, _swap_halves(f"swap_grads_{l}", grads, sums["ffn"] if l < DEPTH - 1 else None))
    gm, gc = [gmix[0], gmix[2]], [gmix[1], gmix[3]]
    small_g = dict(
        norm_w=jnp.stack(dnorm), dn_conv_w=jnp.stack([gm[e]["dn_conv_w"] for e in range(2)]),
        conv_b_pw1=jnp.stack([jnp.concatenate([gc[e]["b1a"], gc[e]["b1b"]], axis=1)[0] for e in range(2)]),
        conv_w_dw=jnp.stack([gc[e]["w_dw"] for e in range(2)]),
        conv_b_dw=jnp.stack([gc[e]["b_dw"][0] for e in range(2)]),
        conv_ln_w=jnp.stack([gc[e]["ln_w"][0] for e in range(2)]),
        conv_ln_b=jnp.stack([gc[e]["ln_b"][0] for e in range(2)]),
        conv_b_pw2=jnp.stack([gc[e]["b2"][0] for e in range(2)]))
    small_by_chip = jnp.stack([_pack([jnp.split(small_g[n], NCHIP, axis=-1)[q] for n in small_names], SMALL_ROWS)
                               for q in range(NCHIP)])
    rep_g = _pack([jnp.stack([gm[e]["sinks"][0] for e in range(2)]), jnp.stack([gm[e]["a_log"][0] for e in range(2)]),
                   jnp.stack([gm[e]["dt_bias"][0] for e in range(2)]),
                   jnp.stack([gm[e]["dn_norm_w"][0] for e in range(2)]), dfw[0]], REP_ROWS)
    small_sum, rep_sum = _exchange_small(small_by_chip, rep_g)
    dx, small_sum, rep_sum = lax.optimization_barrier((dx, small_sum, rep_sum))
    dx = pending(dx)

    big = (("ffn_w_gate", "ffn", 0), ("ffn_w_up", "ffn", FS), ("ffn_w_down", "ffn", 2 * FS), ("mix_w_in", "w_in", 0),
           ("mix_w_out", "w_out", 0), ("conv_w_pw1", "pw1", 0), ("conv_w_pw2", "pw2", 0))
    views = {n: (tr, tr) if n in ("ffn_w_gate", "ffn_w_up") else (
        (lambda a: a) if w[n].ndim == 4 else halves, lambda o, n=n: o.reshape(w[n].shape)) for n, _, _ in big}
    partial_sums = {k: r[...] for k, r in sums.items()}
    upper = {}
    for n, key, row0 in big:
        view = views[n][0]
        upper[n] = _adamw_big(f"adamw_{n}", view(w[n]), view(m[n]), view(v[n]), partial_sums[key], row0,
                              first=0 if key in ("pw1", "pw2") else 1)
    upper, partial_sums_0 = lax.optimization_barrier((upper, {k: r[...] for k, r in sums_0.items()}))
    res = {}
    for n, key, row0 in big:
        view, back = views[n]
        outs = upper[n] if key not in partial_sums_0 else _adamw_big(
            f"adamw_{n}_0", view(w[n]), view(m[n]), view(v[n]), partial_sums_0[key], row0, first=0, outs=upper[n])
        res[n] = [back(o) for o in outs]
    outs = _adamw_small("adamw_small", *[_pack([d[n] for n in small_names], SMALL_ROWS) for d in (w, m, v)],
                        small_sum)
    for i, n in enumerate(small_names):
        res[n] = [_unpack(o, small_shapes)[i] for o in outs]
    outs = _adamw_small("adamw_replicated", *[_pack([d[n] for n in rep_names], REP_ROWS) for d in (w, m, v)],
                        rep_sum)
    for i, n in enumerate(rep_names):
        res[n] = [_unpack(o, rep_shapes)[i] for o in outs]

    total = lax.psum(loss[0, 0], ("x", "y", "c"))
    return (total, dx[None], *[res[n][0] for n in order], *[res[n][1] for n in order],
            *[res[n][2] for n in order], *[res[n][3] for n in order])
```

```python
import jax
import jax.numpy as jnp
from jax import lax
from jax.experimental import pallas as pl
from jax.experimental.pallas import tpu as pltpu
from jax.experimental.pallas import tpu_sc as plsc

F32, BF16 = jnp.float32, jnp.bfloat16
MESH = pl.DeviceIdType.MESH
ANY = pl.BlockSpec(memory_space=pl.ANY)

T, D, F = 2048, 1024, 2816
DEPTH = 4
EPS = 1e-6
HEADS, HDIM, KV_HEADS, GROUP = 8, 64, 2, 4
WINDOW = BLOCK = 128
CHUNK = 64
NCHUNK = T // CHUNK
DN_CONV, CONV_WIDTH = 4, 31
Q_A, KV_A, QKV_B, V_B = 512, 128, 1536, 512
IN_COLS = 2832
IN_SPLITS = (0, 512, 640, 768, 2304, 2816, 2832)
NCHIP, NDEV = 4, 8
FS = F // NCHIP
LR, B1, B2, AEPS, WD, STEP = 0.001, 0.9, 0.999, 1e-08, 0.01, 10
V7X_VMEM_BYTES = 64 * 1024 * 1024
VMEM_LIMIT = V7X_VMEM_BYTES * 7 // 8
LANES = 128


def _cp(*sem):
    return pltpu.CompilerParams(dimension_semantics=sem, vmem_limit_bytes=VMEM_LIMIT)


def _sds(shape, dtype=F32):
    return jax.ShapeDtypeStruct(tuple(shape), dtype)


def _full(shape):
    nd = len(shape)
    return pl.BlockSpec(tuple(shape), lambda *_: (0,) * nd)


def _split_bf16(a):
    hi = a.astype(BF16)
    return hi, (a - hi.astype(F32)).astype(BF16)


def _dg(a, b, ca, cb, hi=False):
    if a.ndim == 3 and b.ndim == 3:
        dims = (((ca + 1,), (cb + 1,)), ((0,), (0,)))
    else:
        dims = (((ca,), (cb,)), ((), ()))
    dot = lambda p, q: lax.dot_general(p, q, dims, preferred_element_type=F32)
    if hi:
        a_hi, a_lo = _split_bf16(a.astype(F32))
        b_hi, b_lo = _split_bf16(b.astype(F32))
        return dot(a_hi, b_hi) + (dot(a_hi, b_lo) + dot(a_lo, b_hi))
    return dot(a.astype(BF16), b.astype(BF16))


def _make_mm(hi):
    @jax.custom_vjp
    def nn(a, b):
        return _dg(a, b, 1, 0, hi)

    @jax.custom_vjp
    def nt(a, b):
        return _dg(a, b, 1, 1, hi)

    @jax.custom_vjp
    def tn(a, b):
        return _dg(a, b, 0, 0, hi)

    nn.defvjp(lambda a, b: (_dg(a, b, 1, 0, hi), (a, b)),
              lambda r, g: (_dg(g, r[1], 1, 1, hi).astype(r[0].dtype), _dg(r[0], g, 0, 0, hi).astype(r[1].dtype)))
    nt.defvjp(lambda a, b: (_dg(a, b, 1, 1, hi), (a, b)),
              lambda r, g: (_dg(g, r[1], 1, 0, hi).astype(r[0].dtype), _dg(g, r[0], 0, 0, hi).astype(r[1].dtype)))
    tn.defvjp(lambda a, b: (_dg(a, b, 0, 0, hi), (a, b)),
              lambda r, g: (_dg(r[1], g, 1, 1, hi).astype(r[0].dtype), _dg(r[0], g, 1, 0, hi).astype(r[1].dtype)))
    return nn, nt, tn


_nn, _nt, _tn = _make_mm(False)
_nn_hi, _nt_hi, _tn_hi = _make_mm(True)


def _rms(x, w):
    return x * lax.rsqrt(jnp.mean(x * x, axis=-1, keepdims=True) + EPS) * w


def _layernorm(x, w, b):
    xc = x - jnp.mean(x, axis=-1, keepdims=True)
    return xc * lax.rsqrt(jnp.mean(xc * xc, axis=-1, keepdims=True) + EPS) * w + b


def _silu(x):
    return x * jax.nn.sigmoid(x)


def _iota2(shape, dim):
    return lax.broadcasted_iota(jnp.int32, shape, dim)


def _flat_weights(lhs_idx, weights):
    specs, ops, lhs_of, where = [], [], [], []
    for a, (k, w) in enumerate(zip(lhs_idx, weights)):
        for q in range(1 if w.ndim == 2 else w.shape[0]):
            specs.append(_full(w.shape) if w.ndim == 2
                         else pl.BlockSpec((None,) + w.shape[1:], lambda i, q=q: (q, 0, 0)))
            ops.append(w)
            lhs_of.append(k)
            where.append((a, None if w.ndim == 2 else q))
    return specs, ops, lhs_of, where


def _blk_fwd(name, pre, lhs_idx, post, toks, smalls, weights, outs, tm=512):
    wspecs, wops, lhs_of, _ = _flat_weights(lhs_idx, weights)
    nt_, ns, nw = len(toks), len(smalls), len(wops)

    def body(*refs):
        tv = [r[...] for r in refs[:nt_]]
        sv = [r[...] for r in refs[nt_:nt_ + ns]]
        wr = refs[nt_ + ns:nt_ + ns + nw]
        orf = refs[nt_ + ns + nw:]
        lhs = pre(tv, sv)
        ys = [_dg(lhs[i], w[...], 1, 0) for i, w in zip(lhs_of, wr)]
        for o_ref, o in zip(orf, post(ys, tv, sv)):
            o_ref[...] = o.astype(o_ref.dtype)

    in_specs = ([pl.BlockSpec((tm, a.shape[1]), lambda i: (i, 0)) for a in toks]
                + [_full(a.shape) for a in smalls] + wspecs)
    out_specs = [pl.BlockSpec((tm, w_), lambda i: (i, 0)) for w_, _ in outs]
    return pl.pallas_call(
        body, grid=(T // tm,), in_specs=in_specs, out_specs=out_specs,
        out_shape=[_sds((T, w_), dt) for w_, dt in outs], name=name, compiler_params=_cp("parallel"),
    )(*toks, *smalls, *wops)


def _blk_bwd(name, pre, lhs_idx, post, toks, smalls, weights, ct_groups, res=None, linear_post=False, tm=256,
             wchunk=512):
    wspecs, wops, lhs_of, where = _flat_weights(lhs_idx, weights)
    nt_, ns, nw, na = len(toks), len(smalls), len(wops), len(weights)
    cts = [a for g in ct_groups for a in g]
    nc = len(cts)
    widths = [sum(a.shape[1] for a in g) for g in ct_groups]
    has_res = res is not None

    def body(*refs):
        p = 0
        tr = refs[p:p + nt_]; p += nt_
        sr = refs[p:p + ns]; p += ns
        wr = refs[p:p + nw]; p += nw
        cr = refs[p:p + nc]; p += nc
        rr = refs[p:p + has_res]; p += has_res
        dtr = refs[p:p + nt_]; p += nt_
        dsr = refs[p:p + ns]; p += ns
        dwr = refs[p:p + na]; p += na
        scr = refs[p:]
        i = pl.program_id(0)

        @pl.when(i == 0)
        def _():
            for r in list(dsr) + list(dwr):
                r[...] = jnp.zeros_like(r)

        tv = [r[...] for r in tr]
        sv = [r[...] for r in sr]
        ctv, q, si = [], 0, 0
        for g in ct_groups:
            if len(g) == 1:
                ctv.append(cr[q][...].astype(F32))
            else:
                off = 0
                for j, a in enumerate(g):
                    scr[si][:, off:off + a.shape[1]] = cr[q + j][...].astype(F32)
                    off += a.shape[1]
                ctv.append(scr[si][...])
                si += 1
            q += len(g)

        lhs, vjp_pre = jax.vjp(lambda *a: tuple(pre(list(a[:nt_]), list(a[nt_:]))), *tv, *sv)
        lhs_b = [l.astype(BF16) for l in lhs]
        ys = [jnp.zeros((tm, w.shape[1]), F32) if linear_post else _dg(lhs_b[k], w[...], 1, 0)
              for k, w in zip(lhs_of, wr)]
        _, vjp_post = jax.vjp(lambda *a: tuple(post(list(a[:nw]), list(a[nw:nw + nt_]), list(a[nw + nt_:]))),
                              *ys, *tv, *sv)
        gp = vjp_post(tuple(ctv))
        dys, dt_post, ds_post = gp[:nw], gp[nw:nw + nt_], gp[nw + nt_:]
        dlhs = [None] * len(lhs)
        for k, w, dy, (a, q) in zip(lhs_of, wr, dys, where):
            dyb = dy.astype(BF16)
            n = w.shape[1]
            for c0 in range(0, n, wchunk):
                c1 = min(n, c0 + wchunk)
                part = _dg(lhs_b[k], dyb[:, c0:c1], 0, 0)
                if q is None:
                    dwr[a][:, c0:c1] += part
                else:
                    dwr[a][q, :, c0:c1] += part
            d = _dg(dyb, w[...], 1, 1)
            dlhs[k] = d if dlhs[k] is None else dlhs[k] + d
        gq = vjp_pre(tuple(d.astype(l.dtype) for d, l in zip(dlhs, lhs)))
        dt_pre, ds_pre = gq[:nt_], gq[nt_:]
        for j in range(nt_):
            d = dt_post[j] + dt_pre[j]
            if j == 0 and has_res:
                d = d + rr[0][...]
            dtr[j][...] = d
        for j in range(ns):
            dsr[j][...] += ds_post[j] + ds_pre[j]

    tok_spec = lambda a: pl.BlockSpec((tm, a.shape[1]), lambda i: (i, 0))
    in_specs = ([tok_spec(a) for a in toks] + [_full(a.shape) for a in smalls] + wspecs
                + [tok_spec(a) for a in cts] + ([tok_spec(res)] if has_res else []))
    out_specs = [tok_spec(a) for a in toks] + [_full(a.shape) for a in smalls] + [_full(w.shape) for w in weights]
    out_shape = ([_sds(a.shape) for a in toks] + [_sds(a.shape) for a in smalls] + [_sds(w.shape) for w in weights])
    scratch = [pltpu.VMEM((tm, wd), F32) for g, wd in zip(ct_groups, widths) if len(g) > 1]
    outs = pl.pallas_call(
        body, grid=(T // tm,), in_specs=in_specs, out_specs=out_specs, out_shape=out_shape,
        scratch_shapes=scratch, name=name, compiler_params=_cp("arbitrary"),
    )(*toks, *smalls, *wops, *cts, *([res] if has_res else []))
    return outs[:nt_], outs[nt_:nt_ + ns], outs[nt_ + ns:]


def _ffn_fwd(name, x, nw, ffn, idx, tm=512):
    def body(x_ref, nw_ref, wg_ref, wu_ref, wd_ref, o_ref, a_ref, b_ref, h_ref):
        s = pl.program_id(1)

        @pl.when(s == 0)
        def _():
            xv = x_ref[...]
            h_ref[...] = _rms(xv, nw_ref[...]).astype(BF16)
            o_ref[...] = xv

        h = h_ref[...]
        a = _dg(h, wg_ref[...], 1, 1).astype(BF16)
        b = _dg(h, wu_ref[...], 1, 1).astype(BF16)
        a_ref[...] = a
        b_ref[...] = b
        o_ref[...] += 0.5 * _dg(_swiglu_act(a, b)[0], wd_ref[...], 1, 0)

    wspec = lambda k: pl.BlockSpec((None, None, FS, D), lambda i, s: (s, idx, k, 0))
    act = pl.BlockSpec((None, tm, FS), lambda i, s: (s, i, 0))
    return pl.pallas_call(
        body, grid=(T // tm, NCHIP),
        in_specs=[pl.BlockSpec((tm, D), lambda i, s: (i, 0)), _full((1, D)), wspec(0), wspec(1), wspec(2)],
        out_specs=[pl.BlockSpec((tm, D), lambda i, s: (i, 0)), act, act, pl.BlockSpec((tm, D), lambda i, s: (i, 0))],
        out_shape=[_sds((T, D)), _sds((NCHIP, T, FS), BF16), _sds((NCHIP, T, FS), BF16), _sds((T, D), BF16)],
        name=name, compiler_params=_cp("parallel", "arbitrary"),
    )(x, nw, ffn, ffn, ffn)


def _swiglu_act(a, b):
    a, b = a.astype(F32), b.astype(F32)
    sa = jax.nn.sigmoid(a)
    act = a * sa
    return act * b, a, b, sa, act


def _ffn_bwd(name, x, nw, ffn, idx, pre, dy, gbuf=None, tm=512):
    ni = T // tm

    def body(x_ref, dy_ref, nw_ref, wg_ref, wu_ref, wd_ref, a_ref, b_ref, h_ref, dx_ref, dnw_ref, dffn_ref, dh_acc,
             ag, au, ad):
        s, i = pl.program_id(0), pl.program_id(1)
        rows = pl.ds(pl.multiple_of(i * tm, tm), tm)

        @pl.when((s == 0) & (i == 0))
        def _():
            dnw_ref[...] = jnp.zeros_like(dnw_ref)

        @pl.when(i == 0)
        def _():
            ag[...] = jnp.zeros_like(ag)
            au[...] = jnp.zeros_like(au)
            ad[...] = jnp.zeros_like(ad)

        hb = h_ref[...]
        gated, a, b, sa, act = _swiglu_act(a_ref[...], b_ref[...])
        dyb = (0.5 * dy_ref[...]).astype(BF16)
        ad[...] += _dg(gated, dyb, 0, 0)
        dact = _dg(dyb, wd_ref[...], 1, 1)
        da = (dact * b * (sa * (1.0 + a * (1.0 - sa)))).astype(BF16)
        db = (dact * act).astype(BF16)
        ag[...] += _dg(da, hb, 0, 0)
        au[...] += _dg(db, hb, 0, 0)
        dh = _dg(da, wg_ref[...], 1, 0) + _dg(db, wu_ref[...], 1, 0)

        @pl.when(s == 0)
        def _():
            dh_acc[rows, :] = dh

        @pl.when((s > 0) & (s < NCHIP - 1))
        def _():
            dh_acc[rows, :] += dh

        @pl.when(s == NCHIP - 1)
        def _():
            _, vjp_rms = jax.vjp(_rms, x_ref[...], nw_ref[...])
            dx, dnw = vjp_rms(dh_acc[rows, :] + dh)
            dx_ref[...] = dy_ref[...] + dx
            dnw_ref[...] += dnw

        @pl.when(i == ni - 1)
        def _():
            dffn_ref[0:FS, :] = ag[...].astype(BF16)
            dffn_ref[FS:2 * FS, :] = au[...].astype(BF16)
            dffn_ref[2 * FS:, :] = ad[...].astype(BF16)

    wspec = lambda r, k: pl.BlockSpec((None, None, r, D), lambda s, i: (s, idx, k, 0), pipeline_mode=pl.Buffered(1))
    last = lambda s, i: (jnp.where(s == NCHIP - 1, i, 0), 0)
    nb = 0 if gbuf is None else 1
    act = pl.BlockSpec((None, tm, FS), lambda s, i: (s, i, 0))
    tok = pl.BlockSpec((tm, D), lambda s, i: (i, 0))
    return pl.pallas_call(
        lambda *refs: body(*refs[:9], *refs[9 + nb:]), grid=(NCHIP, ni),
        in_specs=[pl.BlockSpec((tm, D), last), tok, _full((1, D)), wspec(FS, 0), wspec(FS, 1), wspec(FS, 2), act, act,
                  tok] + [ANY] * nb,
        out_specs=[pl.BlockSpec((tm, D), last), _full((1, D)), wspec(3 * FS, 0)],
        out_shape=[_sds((T, D)), _sds((1, D)), _sds(ffn.shape, BF16)],
        input_output_aliases={9 + k: 2 + k for k in range(nb)},
        scratch_shapes=[pltpu.VMEM((T, D), F32)] + [pltpu.VMEM((FS, D), F32)] * 3,
        name=name, compiler_params=_cp("arbitrary", "arbitrary"),
    )(x, dy, nw, ffn, ffn, ffn, *pre, *(() if gbuf is None else (gbuf,)))


CONV_ROWS = 256


def _conv_pad(k):
    return 8 * ((k - 1 + 7) // 8)


def _conv_fwd(name, x, w, b, act):
    k_w, c = w.shape
    tc = 256 if c % 256 == 0 else LANES
    pad = _conv_pad(k_w)
    has_b = b is not None

    def body(*refs):
        x_ref, w_ref = refs[0], refs[1]
        b_ref = refs[2] if has_b else None
        y_ref, xp = refs[2 + has_b], refs[3 + has_b]
        xp[0:pad, :] = jnp.zeros((pad, tc), F32)
        xp[pad:, :] = x_ref[...]

        def step(t, carry):
            base = pl.multiple_of(t * CONV_ROWS, CONV_ROWS)
            win = xp[pl.ds(base, CONV_ROWS + pad), :]
            acc = jnp.zeros((CONV_ROWS, tc), F32)
            for k in range(k_w):
                o = pad - (k_w - 1) + k
                acc = acc + w_ref[k:k + 1, :] * win[o:o + CONV_ROWS, :]
            if has_b:
                acc = acc + b_ref[...]
            y_ref[pl.ds(base, CONV_ROWS), :] = _silu(acc) if act else acc
            return carry

        lax.fori_loop(0, T // CONV_ROWS, step, 0)

    col = lambda r: pl.BlockSpec((r, tc), lambda j: (0, j))
    ins = [x, w] + ([b] if has_b else [])
    return pl.pallas_call(
        body, grid=(c // tc,), in_specs=[col(T), col(k_w)] + ([col(1)] if has_b else []), out_specs=col(T),
        out_shape=_sds((T, c)), scratch_shapes=[pltpu.VMEM((T + pad, tc), F32)], name=name,
        compiler_params=_cp("parallel"),
    )(*ins)


def _conv_bwd(name, x, w, b, act, dy):
    k_w, c = w.shape
    tc = 256 if c % 256 == 0 else LANES
    pad = _conv_pad(k_w)
    has_b = b is not None

    def body(*refs):
        x_ref, w_ref, dy_ref = refs[0], refs[1], refs[2]
        b_ref = refs[3] if has_b else None
        dx_ref, dw_ref, db_ref, xp, dp = refs[3 + has_b:]
        xp[0:pad, :] = jnp.zeros((pad, tc), F32)
        xp[pad:, :] = x_ref[...]
        dp[T:, :] = jnp.zeros((pad, tc), F32)
        dw_ref[...] = jnp.zeros_like(dw_ref)
        db_ref[...] = jnp.zeros_like(db_ref)

        def step1(t, carry):
            base = pl.multiple_of(t * CONV_ROWS, CONV_ROWS)
            d = dy_ref[pl.ds(base, CONV_ROWS), :]
            win = xp[pl.ds(base, CONV_ROWS + pad), :]
            offs = [pad - (k_w - 1) + k for k in range(k_w)]
            if act:
                acc = jnp.zeros((CONV_ROWS, tc), F32)
                for k, o in enumerate(offs):
                    acc = acc + w_ref[k:k + 1, :] * win[o:o + CONV_ROWS, :]
                if has_b:
                    acc = acc + b_ref[...]
                sg = jax.nn.sigmoid(acc)
                d = d * (sg * (1.0 + acc * (1.0 - sg)))
            dp[pl.ds(base, CONV_ROWS), :] = d
            for k, o in enumerate(offs):
                dw_ref[k:k + 1, :] += jnp.sum(d * win[o:o + CONV_ROWS, :], axis=0, keepdims=True)
            db_ref[...] += jnp.sum(d, axis=0, keepdims=True)
            return carry

        lax.fori_loop(0, T // CONV_ROWS, step1, 0)

        def step2(t, carry):
            base = pl.multiple_of(t * CONV_ROWS, CONV_ROWS)
            win = dp[pl.ds(base, CONV_ROWS + pad), :]
            acc = jnp.zeros((CONV_ROWS, tc), F32)
            for k in range(k_w):
                o = (k_w - 1) - k
                acc = acc + w_ref[k:k + 1, :] * win[o:o + CONV_ROWS, :]
            dx_ref[pl.ds(base, CONV_ROWS), :] = acc
            return carry

        lax.fori_loop(0, T // CONV_ROWS, step2, 0)

    col = lambda r: pl.BlockSpec((r, tc), lambda j: (0, j))
    ins = [x, w, dy] + ([b] if has_b else [])
    return pl.pallas_call(
        body, grid=(c // tc,), in_specs=[col(T), col(k_w), col(T)] + ([col(1)] if has_b else []),
        out_specs=[col(T), col(k_w), col(1)], out_shape=[_sds((T, c)), _sds((k_w, c)), _sds((1, c))],
        scratch_shapes=[pltpu.VMEM((T + pad, tc), F32), pltpu.VMEM((T + pad, tc), F32)], name=name,
        compiler_params=_cp("parallel"),
    )(*ins)


def _attn_consts(n):
    i = _iota2((BLOCK, 2 * BLOCK), 0)
    j = _iota2((BLOCK, 2 * BLOCK), 1)
    dist = i + BLOCK - j
    valid = (dist >= 0) & (dist < WINDOW) & ((n > 0) | (j >= BLOCK))
    return dist.astype(F32), valid


def _attn_block(q4, kk, vv, sinks, dist, valid, kv):
    outs = []
    lane = _iota2((1, HEADS), 1)
    for g in range(GROUP):
        h = kv * GROUP + g
        slope = 2.0 ** (-8.0 * (h + 1) / HEADS)
        s = _nt(q4[:, g * HDIM:(g + 1) * HDIM], kk) * (HDIM ** -0.5)
        s = jnp.where(valid, s - slope * dist, -1e30)
        sink = jnp.sum(jnp.where(lane == h, sinks, 0.0), axis=1, keepdims=True)
        m = jnp.maximum(jnp.max(s, axis=-1, keepdims=True), sink)
        e = jnp.exp(s - m)
        p = e / (jnp.sum(e, axis=-1, keepdims=True) + jnp.exp(sink - m))
        outs.append(_nn(p, vv))
    return tuple(outs)


def _attn_fwd(name, qa, ka, va, sinks):
    def body(q_ref, k_ref, v_ref, s_ref, o_ref, kp, vp):
        kp[0:BLOCK, :] = jnp.zeros((BLOCK, KV_A), F32)
        vp[0:BLOCK, :] = jnp.zeros((BLOCK, KV_A), F32)
        kp[BLOCK:, :] = k_ref[...]
        vp[BLOCK:, :] = v_ref[...]
        sinks_v = s_ref[...]

        def step(n, carry):
            r = pl.multiple_of(n * BLOCK, BLOCK)
            dist, valid = _attn_consts(n)
            k2 = kp[pl.ds(r, 2 * BLOCK), :]
            v2 = vp[pl.ds(r, 2 * BLOCK), :]
            for kv in range(KV_HEADS):
                q4 = q_ref[pl.ds(r, BLOCK), kv * GROUP * HDIM:(kv + 1) * GROUP * HDIM]
                og = _attn_block(q4, k2[:, kv * HDIM:(kv + 1) * HDIM], v2[:, kv * HDIM:(kv + 1) * HDIM], sinks_v,
                                 dist, valid, kv)
                for g in range(GROUP):
                    h = kv * GROUP + g
                    o_ref[pl.ds(r, BLOCK), h * HDIM:(h + 1) * HDIM] = og[g]
            return carry

        lax.fori_loop(0, T // BLOCK, step, 0)

    return pl.pallas_call(
        body, out_shape=_sds((T, Q_A)),
        scratch_shapes=[pltpu.VMEM((T + BLOCK, KV_A), F32), pltpu.VMEM((T + BLOCK, KV_A), F32)], name=name,
        compiler_params=pltpu.CompilerParams(vmem_limit_bytes=VMEM_LIMIT),
    )(qa, ka, va, sinks)


def _attn_bwd(name, qa, ka, va, sinks, do):
    def body(q_ref, k_ref, v_ref, s_ref, do_ref, dq_ref, dk_ref, dv_ref, ds_ref, kp, vp, dkp, dvp):
        kp[0:BLOCK, :] = jnp.zeros((BLOCK, KV_A), F32)
        vp[0:BLOCK, :] = jnp.zeros((BLOCK, KV_A), F32)
        kp[BLOCK:, :] = k_ref[...]
        vp[BLOCK:, :] = v_ref[...]
        dkp[...] = jnp.zeros_like(dkp)
        dvp[...] = jnp.zeros_like(dvp)
        ds_ref[...] = jnp.zeros_like(ds_ref)
        sinks_v = s_ref[...]

        def step(n, carry):
            r = pl.multiple_of(n * BLOCK, BLOCK)
            dist, valid = _attn_consts(n)
            k2 = kp[pl.ds(r, 2 * BLOCK), :]
            v2 = vp[pl.ds(r, 2 * BLOCK), :]
            for kv in range(KV_HEADS):
                cols = slice(kv * HDIM, (kv + 1) * HDIM)
                q4 = q_ref[pl.ds(r, BLOCK), kv * GROUP * HDIM:(kv + 1) * GROUP * HDIM]
                _, vjp = jax.vjp(lambda q, k, v, s: _attn_block(q, k, v, s, dist, valid, kv),
                                 q4, k2[:, cols], v2[:, cols], sinks_v)
                cts = tuple(do_ref[pl.ds(r, BLOCK), (kv * GROUP + g) * HDIM:(kv * GROUP + g + 1) * HDIM]
                            for g in range(GROUP))
                dq4, dkk, dvv, dsk = vjp(cts)
                dq_ref[pl.ds(r, BLOCK), kv * GROUP * HDIM:(kv + 1) * GROUP * HDIM] = dq4
                dkp[pl.ds(r, 2 * BLOCK), cols] += dkk
                dvp[pl.ds(r, 2 * BLOCK), cols] += dvv
                ds_ref[...] += dsk
            return carry

        lax.fori_loop(0, T // BLOCK, step, 0)
        dk_ref[...] = dkp[BLOCK:, :]
        dv_ref[...] = dvp[BLOCK:, :]

    pad = lambda: pltpu.VMEM((T + BLOCK, KV_A), F32)
    return pl.pallas_call(
        body, out_shape=[_sds((T, Q_A)), _sds((T, KV_A)), _sds((T, KV_A)), _sds((1, HEADS))],
        scratch_shapes=[pad(), pad(), pad(), pad()], name=name,
        compiler_params=pltpu.CompilerParams(vmem_limit_bytes=VMEM_LIMIT),
    )(qa, ka, va, sinks, do)


def _dn_consts():
    i = _iota2((CHUNK, CHUNK), 0)
    j = _iota2((CHUNK, CHUNK), 1)
    return dict(causal=i >= j, strict=i > j, eye=(i == j).astype(F32), ltri=(i >= j).astype(F32),
                ones=jnp.ones((CHUNK, CHUNK), F32), last=(_iota2((CHUNK, 1), 0) == CHUNK - 1).astype(F32))


def _l2norm(x):
    return x * lax.rsqrt(jnp.sum(x * x, axis=-1, keepdims=True) + EPS)


def _head_cols(m):
    lane = _iota2((1, HEADS), 1)
    return jnp.concatenate([jnp.sum(jnp.where(lane == h, m, 0.0), axis=1, keepdims=True)[None]
                            for h in range(HEADS)], axis=0)


@jax.custom_vjp
def _unit_lower_inverse(low, known):
    if known is not None:
        return known
    inv = (_iota2((CHUNK, CHUNK), 0) == _iota2((CHUNK, CHUNK), 1)).astype(F32) - low
    pw = low
    for _ in range(5):
        pw = _dg(pw, pw, 1, 0, True)
        inv = inv + _dg(inv, pw, 1, 0, True)
    return inv


def _unit_lower_inverse_fwd(low, known):
    inv = _unit_lower_inverse(low, known)
    return inv, (inv, known)


def _unit_lower_inverse_bwd(res, g):
    inv, known = res
    d_low = -_dg(inv, _dg(g, inv, 1, 1, True), 0, 0, True)
    return d_low, (None if known is None else jnp.zeros_like(known))


_unit_lower_inverse.defvjp(_unit_lower_inverse_fwd, _unit_lower_inverse_bwd)


def _dn_local(q3, k3, v3, braw, araw, alog, dtb, cs, known_inv=None):
    q = _l2norm(q3) * (HDIM ** -0.5)
    k = _l2norm(k3)
    g = -jnp.exp(alog) * jax.nn.softplus(araw + dtb)
    gc_all = _nn_hi(cs["ltri"], g)
    egc_all = jnp.exp(gc_all)
    beta, gc, egc = _head_cols(jax.nn.sigmoid(braw)), _head_cols(gc_all), _head_cols(egc_all)
    a = jnp.broadcast_to(gc, (HEADS, CHUNK, CHUNK))
    diff = a - jnp.swapaxes(a, 1, 2)
    decay = jnp.where(cs["causal"], jnp.exp(jnp.where(cs["causal"], diff, 0.0)), 0.0)
    kb = k * beta
    low = jnp.where(cs["strict"], _nt(kb, k) * decay, 0.0)
    inv = _unit_lower_inverse(low, known_inv)
    u = _nn_hi(inv, v3 * beta)
    w = _nn_hi(inv, kb * egc)
    attn = _nt(q, k) * decay
    gc_last = jnp.sum(gc * cs["last"], axis=1, keepdims=True)
    return u, w, attn, q * egc, k * jnp.exp(gc_last - gc), egc_all, inv


def _heads3(ref, off=0):
    return jnp.concatenate([ref[:, off + h * HDIM:off + (h + 1) * HDIM][None] for h in range(HEADS)], axis=0)


def _dn_local_fwd(name, qkv, ba, alog, dtb):
    def body(qkv_ref, ba_ref, al_ref, dt_ref, u_ref, w_ref, at_ref, qd_ref, kd_ref, eg_ref, inv_ref):
        bav = ba_ref[...]
        outs = _dn_local(_heads3(qkv_ref), _heads3(qkv_ref, 512), _heads3(qkv_ref, 1024), bav[:, :HEADS],
                         bav[:, HEADS:], al_ref[...], dt_ref[...], _dn_consts())
        for r, o in zip((u_ref, w_ref, at_ref, qd_ref, kd_ref, inv_ref), outs[:5] + outs[6:]):
            _unheads(r, o)
        eg_ref[...] = outs[5]

    row = lambda w_: pl.BlockSpec((CHUNK, w_), lambda n: (n, 0))
    return pl.pallas_call(
        body, grid=(NCHUNK,), in_specs=[row(QKV_B), row(2 * HEADS), _full((1, HEADS)), _full((1, HEADS))],
        out_specs=[row(V_B)] * 5 + [row(HEADS), row(V_B)],
        out_shape=[_sds((T, V_B))] * 5 + [_sds((T, HEADS)), _sds((T, V_B))], name=name,
        compiler_params=_cp("parallel"),
    )(qkv, ba, alog, dtb)


def _dn_local_bwd(name, qkv, ba, alog, dtb, inv, cts):
    def body(qkv_ref, ba_ref, al_ref, dt_ref, inv_ref, du_ref, dw_ref, dat_ref, dqd_ref, dkd_ref, deg_ref,
             dqkv_ref, dba_ref, dal_ref, ddt_ref):
        @pl.when(pl.program_id(0) == 0)
        def _():
            dal_ref[...] = jnp.zeros_like(dal_ref)
            ddt_ref[...] = jnp.zeros_like(ddt_ref)

        cs = _dn_consts()
        bav = ba_ref[...]
        known = _heads3(inv_ref)
        _, vjp = jax.vjp(lambda *a: _dn_local(*a, cs, known)[:6], _heads3(qkv_ref), _heads3(qkv_ref, 512),
                         _heads3(qkv_ref, 1024), bav[:, :HEADS], bav[:, HEADS:], al_ref[...], dt_ref[...])
        dq, dk, dv, dbr, dar, dal, ddt = vjp((_heads3(du_ref), _heads3(dw_ref), _heads3(dat_ref), _heads3(dqd_ref),
                                              _heads3(dkd_ref), deg_ref[...]))
        for h in range(HEADS):
            dqkv_ref[:, h * HDIM:(h + 1) * HDIM] = dq[h]
            dqkv_ref[:, 512 + h * HDIM:512 + (h + 1) * HDIM] = dk[h]
            dqkv_ref[:, 1024 + h * HDIM:1024 + (h + 1) * HDIM] = dv[h]
        dba_ref[:, :HEADS] = dbr
        dba_ref[:, HEADS:] = dar
        dal_ref[...] += dal
        ddt_ref[...] += ddt

    row = lambda w_: pl.BlockSpec((CHUNK, w_), lambda n: (n, 0))
    return pl.pallas_call(
        body, grid=(NCHUNK,),
        in_specs=[row(QKV_B), row(2 * HEADS), _full((1, HEADS)), _full((1, HEADS))] + [row(V_B)] * 6 + [row(HEADS)],
        out_specs=[row(QKV_B), row(2 * HEADS), _full((1, HEADS)), _full((1, HEADS))],
        out_shape=[_sds((T, QKV_B)), _sds((T, 2 * HEADS)), _sds((1, HEADS)), _sds((1, HEADS))], name=name,
        compiler_params=_cp("arbitrary"),
    )(qkv, ba, alog, dtb, inv, *cts)


def _dn_step(s, u, w, attn, qd, kd, egc, z, nw):
    last = (_iota2((CHUNK, 1), 0) == CHUNK - 1).astype(F32)
    gl = jnp.sum(_head_cols(egc) * last, axis=1, keepdims=True)
    v_new = u - _nn(w, s)
    o = _nn(qd, s) + _nn(attn, v_new)
    s_new = s * gl + _tn(kd, v_new)
    return s_new, _rms(o, nw) * _silu(z)


def _unheads(ref, v3):
    for h in range(HEADS):
        ref[:, h * HDIM:(h + 1) * HDIM] = v3[h]


def _dn_rec_fwd(name, u, w, attn, qd, kd, egc, z, nw):
    def body(u_ref, w_ref, at_ref, qd_ref, kd_ref, eg_ref, z_ref, nw_ref, o_ref, ss_ref, s_scr):
        @pl.when(pl.program_id(0) == 0)
        def _():
            s_scr[...] = jnp.zeros_like(s_scr)

        s = s_scr[...]
        ss_ref[...] = s
        s_new, on = _dn_step(s, _heads3(u_ref), _heads3(w_ref), _heads3(at_ref), _heads3(qd_ref), _heads3(kd_ref),
                             eg_ref[...], _heads3(z_ref), nw_ref[...])
        s_scr[...] = s_new
        _unheads(o_ref, on)

    row = lambda w_: pl.BlockSpec((CHUNK, w_), lambda n: (n, 0))
    return pl.pallas_call(
        body, grid=(NCHUNK,), in_specs=[row(V_B)] * 5 + [row(HEADS), row(V_B), _full((1, HDIM))],
        out_specs=[row(V_B), pl.BlockSpec((None, HEADS, HDIM, HDIM), lambda n: (n, 0, 0, 0))],
        out_shape=[_sds((T, V_B)), _sds((NCHUNK, HEADS, HDIM, HDIM))],
        scratch_shapes=[pltpu.VMEM((HEADS, HDIM, HDIM), F32)], name=name, compiler_params=_cp("arbitrary"),
    )(u, w, attn, qd, kd, egc, z, nw)


def _dn_rec_bwd(name, u, w, attn, qd, kd, egc, z, nw, ss, do):
    def body(u_ref, w_ref, at_ref, qd_ref, kd_ref, eg_ref, z_ref, nw_ref, ss_ref, do_ref,
             du_ref, dw_ref, dat_ref, dqd_ref, dkd_ref, deg_ref, dz_ref, dnw_ref, ds_scr):
        @pl.when(pl.program_id(0) == 0)
        def _():
            ds_scr[...] = jnp.zeros_like(ds_scr)
            dnw_ref[...] = jnp.zeros_like(dnw_ref)

        _, vjp = jax.vjp(_dn_step, ss_ref[...], _heads3(u_ref), _heads3(w_ref), _heads3(at_ref), _heads3(qd_ref),
                         _heads3(kd_ref), eg_ref[...], _heads3(z_ref), nw_ref[...])
        ds, du, dw, dat, dqd, dkd, deg, dz, dnw = vjp((ds_scr[...], _heads3(do_ref)))
        ds_scr[...] = ds
        for r, v in zip((du_ref, dw_ref, dat_ref, dqd_ref, dkd_ref, dz_ref), (du, dw, dat, dqd, dkd, dz)):
            _unheads(r, v)
        deg_ref[...] = deg
        dnw_ref[...] += dnw

    row = lambda w_: pl.BlockSpec((CHUNK, w_), lambda n: (NCHUNK - 1 - n, 0))
    return pl.pallas_call(
        body, grid=(NCHUNK,),
        in_specs=[row(V_B)] * 5 + [row(HEADS), row(V_B), _full((1, HDIM)),
                                   pl.BlockSpec((None, HEADS, HDIM, HDIM), lambda n: (NCHUNK - 1 - n, 0, 0, 0)),
                                   row(V_B)],
        out_specs=[row(V_B)] * 5 + [row(HEADS), row(V_B), _full((1, HDIM))],
        out_shape=[_sds((T, V_B))] * 5 + [_sds((T, HEADS)), _sds((T, V_B)), _sds((1, HDIM))],
        scratch_shapes=[pltpu.VMEM((HEADS, HDIM, HDIM), F32)], name=name, compiler_params=_cp("arbitrary"),
    )(u, w, attn, qd, kd, egc, z, nw, ss, do)


def _final(name, x, fw, target, tm=512):
    def body(x_ref, fw_ref, t_ref, l_ref, dx_ref, dfw_ref):
        @pl.when(pl.program_id(0) == 0)
        def _():
            l_ref[...] = jnp.zeros_like(l_ref)
            dfw_ref[...] = jnp.zeros_like(dfw_ref)

        tv = t_ref[...]

        def f(xv, fwv):
            err = _rms(xv, fwv) - tv
            per_tok = jnp.mean(err * err, axis=-1, keepdims=True)
            return 0.5 * jnp.sum(per_tok, axis=0, keepdims=True)

        loss, vjp = jax.vjp(f, x_ref[...], fw_ref[...])
        dx, dfw = vjp(jnp.ones((1, 1), F32))
        l_ref[...] += loss
        dx_ref[...] = dx
        dfw_ref[...] += dfw

    tok = pl.BlockSpec((tm, D), lambda i: (i, 0))
    return pl.pallas_call(
        body, grid=(T // tm,), in_specs=[tok, _full((1, D)), tok], out_specs=[_full((1, 1)), tok, _full((1, D))],
        out_shape=[_sds((1, 1)), _sds((T, D)), _sds((1, D))], name=name, compiler_params=_cp("arbitrary"),
    )(x, fw, target)


def _m1_pre(tv, sv):
    return [_rms(tv[0], sv[0])]


def _m1_post(ys, tv, sv):
    return (jnp.concatenate(ys, axis=1),)


def _m1_post_split(ys, tv, sv):
    proj = jnp.concatenate(ys, axis=1)
    return tuple(proj[:, a:b] for a, b in zip(IN_SPLITS[:-1], IN_SPLITS[1:]))


def _m5_pre(tv, sv):
    return [tv[1], tv[2]]


def _m5_post(ys, tv, sv):
    return (tv[0] + ys[0] + ys[1],)


def _c1_pre(tv, sv):
    return [_rms(tv[0], sv[0])]


def _c1_post(ys, tv, sv):
    return ((jnp.concatenate(ys[:2], axis=1) + sv[1]) * jax.nn.sigmoid(jnp.concatenate(ys[2:], axis=1) + sv[2]),)


def _c3_pre(tv, sv):
    return [_silu(_layernorm(tv[0], sv[0], sv[1]))]


def _c3_post(ys, tv, sv):
    return (tv[1] + ys[0] + sv[2],)


def _row(v):
    return v.reshape(1, -1)


def _mixer_fwd(tag, x, p):
    parts = _blk_fwd(f"m1_fwd_{tag}", _m1_pre, [0], _m1_post_split, [x], [p["nw"]], [p["w_in"]],
                     [(b - a, F32) for a, b in zip(IN_SPLITS[:-1], IN_SPLITS[1:])])
    qa, ka, va, qkvb, z, ba = parts
    att = _attn_fwd(f"attn_fwd_{tag}", qa, ka, va, p["sinks"])
    qkvc = _conv_fwd(f"dnconv_fwd_{tag}", qkvb, p["dn_conv_w"], None, True)
    *loc, inv = _dn_local_fwd(f"dnloc_fwd_{tag}", qkvc, ba, p["a_log"], p["dt_bias"])
    og, ss = _dn_rec_fwd(f"dnrec_fwd_{tag}", *loc, z, p["dn_norm_w"])
    (out,) = _blk_fwd(f"m5_fwd_{tag}", _m5_pre, [0, 1], _m5_post, [x, att, og], [], [p["wo_a"], p["wo_b"]],
                      [(D, F32)])
    return out, dict(x=x, qa=qa, ka=ka, va=va, qkvb=qkvb, z=z, ba=ba, att=att, qkvc=qkvc, loc=loc, inv=inv, og=og,
                     ss=ss)


def _mixer_bwd(tag, dy, p, s):
    (dxa, datt, dog), _, (dwo_a, dwo_b) = _blk_bwd(f"m5_bwd_{tag}", _m5_pre, [0, 1], _m5_post,
                                                   [s["x"], s["att"], s["og"]], [], [p["wo_a"], p["wo_b"]], [[dy]],
                                                   linear_post=True)
    rec = _dn_rec_bwd(f"dnrec_bwd_{tag}", *s["loc"], s["z"], p["dn_norm_w"], s["ss"], dog)
    dz, dnw_dn = rec[6], rec[7]
    dqkvc, dba, dalog, ddtb = _dn_local_bwd(f"dnloc_bwd_{tag}", s["qkvc"], s["ba"], p["a_log"], p["dt_bias"],
                                            s["inv"], rec[:6])
    dqkvb, dconvw, _ = _conv_bwd(f"dnconv_bwd_{tag}", s["qkvb"], p["dn_conv_w"], None, True, dqkvc)
    dqa, dka, dva, dsinks = _attn_bwd(f"attn_bwd_{tag}", s["qa"], s["ka"], s["va"], p["sinks"], datt)
    (dx,), (dnw,), (dw_in,) = _blk_bwd(f"m1_bwd_{tag}", _m1_pre, [0], _m1_post, [s["x"]], [p["nw"]], [p["w_in"]],
                                       [[dqa, dka, dva, dqkvb, dz, dba]], res=dxa, linear_post=True)
    return dx, dict(nw=dnw, w_in=dw_in, wo_a=dwo_a, wo_b=dwo_b, dn_conv_w=dconvw, sinks=dsinks, a_log=dalog,
                    dt_bias=ddtb, dn_norm_w=dnw_dn)


def _conformer_fwd(tag, x, p):
    (glu,) = _blk_fwd(f"c1_fwd_{tag}", _c1_pre, [0], _c1_post, [x], [p["nw"], p["b1a"], p["b1b"]], [p["w1"]],
                      [(D, F32)])
    cc = _conv_fwd(f"dwconv_fwd_{tag}", glu, p["w_dw"], p["b_dw"], False)
    (out,) = _blk_fwd(f"c3_fwd_{tag}", _c3_pre, [0], _c3_post, [cc, x], [p["ln_w"], p["ln_b"], p["b2"]], [p["w2"]],
                      [(D, F32)])
    return out, dict(x=x, glu=glu, cc=cc)


def _conformer_bwd(tag, dy, p, s):
    (dcc, dxa), (dlnw, dlnb, db2), (dw2,) = _blk_bwd(f"c3_bwd_{tag}", _c3_pre, [0], _c3_post, [s["cc"], s["x"]],
                                                     [p["ln_w"], p["ln_b"], p["b2"]], [p["w2"]], [[dy]],
                                                     linear_post=True)
    dglu, dwdw, dbdw = _conv_bwd(f"dwconv_bwd_{tag}", s["glu"], p["w_dw"], p["b_dw"], False, dcc)
    (dx,), (dnw, db1a, db1b), (dw1,) = _blk_bwd(f"c1_bwd_{tag}", _c1_pre, [0], _c1_post, [s["x"]],
                                                [p["nw"], p["b1a"], p["b1b"]], [p["w1"]], [[dglu]], res=dxa)
    return dx, dict(nw=dnw, b1a=db1a, b1b=db1b, w1=dw1, w_dw=dwdw, b_dw=dbdw, ln_w=dlnw, ln_b=dlnb, b2=db2, w2=dw2)


def _layer_fwd(l, x, nw, ffn, p):
    x1, *pre_a = _ffn_fwd(f"ffn_fwd_{l}a", x, _row(nw[0]), ffn, 0)
    p = dict(p, nw=_row(nw[1]))
    x2, sv = (_mixer_fwd if l % 2 == 0 else _conformer_fwd)(str(l), x1, p)
    out, *pre_b = _ffn_fwd(f"ffn_fwd_{l}b", x2, _row(nw[2]), ffn, 1)
    return out, (x, x2, p, sv, pre_a, pre_b)


def _layer_bwd(l, dx, nw, ffn, saved, after_first=lambda dx: dx):
    x0, x2, p, sv, pre_a, pre_b = saved
    dx, dn2, dffn = _ffn_bwd(f"ffn_bwd_{l}b", x2, _row(nw[2]), ffn, 1, pre_b, dx)
    dx = after_first(dx)
    dx, dmix = (_mixer_bwd if l % 2 == 0 else _conformer_bwd)(str(l), dx, p, sv)
    dx, dn0, dffn = _ffn_bwd(f"ffn_bwd_{l}a", x0, _row(nw[0]), ffn, 0, pre_a, dx, dffn)
    return dx, jnp.concatenate([dn0, dmix.pop("nw"), dn2], axis=0), dffn, dmix


def _place():
    x, y, c = lax.axis_index("x"), lax.axis_index("y"), lax.axis_index("c")
    chips = [(1 - x, y), (x, 1 - y), (1 - x, 1 - y)]
    return x, y, c, 2 * x + y, chips, [2 * px + py for px, py in chips]


def _handshake(peers):
    barrier = pltpu.get_barrier_semaphore()
    for p in peers:
        pl.semaphore_signal(barrier, inc=1, device_id=p, device_id_type=MESH)
    pl.semaphore_wait(barrier, len(peers))


def _chip_peers():
    x, y, c, _, chips, _ = _place()
    return [(*chip, c) for chip in chips] + [(x, y, 1 - c)]


def _gather_copies(ins, outs, nb, send, recv, fsend, frecv, lsem):
    n_in = len(ins)
    x, y, c, me, chips, cidx = _place()
    sib = (x, y, 1 - c)
    local = [pltpu.make_async_copy(ins[a], outs[a].at[me], lsem.at[a]) for a in range(n_in)]
    for cp in local:
        cp.start()

    def region(a, k, who):
        if k < 2:
            return outs[a].at[cidx[k], pl.ds(who, 1)]
        r = ins[a].shape[1] // 2
        return outs[a].at[cidx[2], pl.ds(who, 1), pl.ds((k - 2) * r, r)]

    def hop(a, k):
        if k < 2:
            src, dst = ins[a].at[pl.ds(c, 1)], outs[a].at[me, pl.ds(c, 1)]
        else:
            r = ins[a].shape[1] // 2
            src = dst = outs[a].at[cidx[3 - k], pl.ds(c, 1), pl.ds((k - 2) * r, r)]
        return pltpu.make_async_remote_copy(src, dst, send.at[4 * a + k], recv.at[4 * a + k],
                                            device_id=(*chips[k % 2], c), device_id_type=MESH)

    def landed(a, k):
        dst = region(a, k, c)
        return pltpu.make_async_remote_copy(dst, dst, send.at[4 * a + k], recv.at[4 * a + k],
                                            device_id=(*chips[k % 2], c), device_id_type=MESH)

    def passed(a, k, who):
        part = region(a, k, who)
        return pltpu.make_async_remote_copy(part, part, fsend.at[4 * a + k], frecv.at[4 * a + k], device_id=sib,
                                            device_id_type=MESH)

    def direct(a, j):
        k = 4 * nb + 3 * (a - nb) + j
        return pltpu.make_async_remote_copy(ins[a], outs[a].at[me], send.at[k], recv.at[k],
                                            device_id=(*chips[j], c), device_id_type=MESH)

    def direct_landed(a, j):
        k = 4 * nb + 3 * (a - nb) + j
        dst = outs[a].at[cidx[j]]
        return pltpu.make_async_remote_copy(dst, dst, send.at[k], recv.at[k], device_id=(*chips[j], c),
                                            device_id_type=MESH)

    sends = [hop(a, k) for a in range(nb) for k in range(2)] + [direct(a, j) for a in range(nb, n_in) for j in range(3)]
    for cp in sends:
        cp.start()
    for a in range(nb):
        for k in (1, 0):
            landed(a, k).wait_recv()
            for cp in (hop(a, 3 - k), passed(a, k, c)):
                cp.start()
                sends.append(cp)
    for a in range(nb):
        for k in (2, 3):
            landed(a, k).wait_recv()
            cp = passed(a, k, c)
            cp.start()
            sends.append(cp)
    for a in range(nb, n_in):
        for j in range(3):
            direct_landed(a, j).wait_recv()
    for a in range(nb):
        for k in range(4):
            passed(a, k, 1 - c).wait_recv()
    for cp in sends:
        cp.wait_send()
    for cp in local:
        cp.wait()


def _gather_sems(n_in, nb):
    dma = pltpu.SemaphoreType.DMA
    n_ici = 4 * nb + 3 * (n_in - nb)
    return [dma((n_ici,)), dma((n_ici,)), dma((4 * nb,)), dma((4 * nb,)), dma((n_in,))]


def _gather_async(name, halved, whole=()):
    nb, arrs = len(halved), list(halved) + list(whole)
    hbm = pltpu.MemorySpace.HBM
    ins = [jax.new_ref(a, memory_space=hbm) for a in arrs]
    outs = [jax.empty_ref(_sds((NCHIP,) + a.shape, a.dtype), memory_space=hbm) for a in arrs]

    @pl.kernel(mesh=plsc.ScalarSubcoreMesh(axis_name="seq", num_cores=1), name=name,
               scratch_types=tuple(_gather_sems(len(arrs), nb)),
               compiler_params=pltpu.CompilerParams(collective_id=2))
    def launch(send, recv, fsend, frecv, lsem):
        _handshake(_chip_peers())
        _gather_copies(ins, outs, nb, send, recv, fsend, frecv, lsem)

    launch()
    return outs


def _swap_halves(name, grads, after=None):
    n = len(grads)
    hbm = pltpu.MemorySpace.HBM
    ins = [jax.new_ref(g, memory_space=hbm) for g in grads]
    outs = [jax.empty_ref(_sds((NCHIP, g.shape[1] // 2) + g.shape[2:], g.dtype), memory_space=hbm) for g in grads]
    tile = (2 * 8, LANES)
    token = None if after is None else jax.empty_ref(_sds(tile, BF16), memory_space=hbm)

    @pl.kernel(mesh=plsc.ScalarSubcoreMesh(axis_name="seq", num_cores=1), name=name,
               scratch_types=(pltpu.SemaphoreType.DMA((n + 1,)), pltpu.SemaphoreType.DMA((n,))),
               compiler_params=pltpu.CompilerParams(collective_id=1))
    def launch(send, recv):
        x, y, c, _, _, _ = _place()
        sib = (x, y, 1 - c)
        _handshake([sib])
        if after is not None:
            tick = pltpu.make_async_copy(after.at[0, 0, 0, pl.ds(0, tile[0]), pl.ds(0, tile[1])], token, send.at[n])
            tick.start()
            tick.wait()
        cps = []
        for a in range(n):
            h = grads[a].shape[1] // 2
            cps.append(pltpu.make_async_remote_copy(ins[a].at[:, pl.ds((1 - c) * h, h)], outs[a], send.at[a],
                                                    recv.at[a], device_id=sib, device_id_type=MESH))
        for cp in cps:
            cp.start()
        for cp in cps:
            cp.wait()

    launch()
    return outs


def _row_tile(r, cap=256):
    return max(t for t in range(8, cap + 1, 8) if r % t == 0)


def _add_half(name, g, r, c_arr):
    _, l, rows, cols = g.shape
    h = l // 2
    tr = _row_tile(rows, 1056)

    def body(c_ref, g_ref, r_ref, o_ref):
        o_ref[...] = (g_ref[...].astype(F32) + r_ref[...].astype(F32)).astype(BF16)

    blk = (None, None, tr, cols)
    return pl.pallas_call(
        body,
        grid_spec=pltpu.PrefetchScalarGridSpec(
            num_scalar_prefetch=1, grid=(NCHIP, h, rows // tr),
            in_specs=[pl.BlockSpec(blk, lambda j, i, t, c_ref: (j, c_ref[0] * h + i, t, 0)),
                      pl.BlockSpec(blk, lambda j, i, t, c_ref: (j, i, t, 0))],
            out_specs=pl.BlockSpec(blk, lambda j, i, t, c_ref: (j, i, t, 0))),
        out_shape=_sds((NCHIP, h, rows, cols), BF16), name=name,
        compiler_params=_cp("parallel", "parallel", "parallel"),
    )(c_arr, g, r)


def _scatter_async(name, parts, sums, where):
    nb = len(parts)
    ins = [jax.new_ref(p, memory_space=pltpu.MemorySpace.HBM) for p in parts]
    dma = pltpu.SemaphoreType.DMA

    @pl.kernel(mesh=plsc.ScalarSubcoreMesh(axis_name="seq", num_cores=1), name=name,
               scratch_types=(dma((3 * nb,)), dma((3 * nb,)), dma((4 * nb,)), dma((4 * nb,)), dma((nb,))),
               compiler_params=pltpu.CompilerParams(collective_id=3))
    def launch(send, recv, fsend, frecv, lsem):
        _handshake(_chip_peers())
        x, y, c, me, chips, cidx = _place()
        sib = (x, y, 1 - c)

        def slot(a, half, chip):
            return sums[a].at[half, chip, pl.ds(where[a], 1)]

        local = [pltpu.make_async_copy(ins[a].at[me], slot(a, c, me), lsem.at[a]) for a in range(nb)]
        for cp in local:
            cp.start()

        def ici(a, j):
            return pltpu.make_async_remote_copy(ins[a].at[cidx[j]], slot(a, c, me), send.at[a * 3 + j],
                                                recv.at[a * 3 + j], device_id=(*chips[j], c), device_id_type=MESH)

        def landed(a, j):
            dst = slot(a, c, cidx[j])
            return pltpu.make_async_remote_copy(dst, dst, send.at[a * 3 + j], recv.at[a * 3 + j],
                                                device_id=(*chips[j], c), device_id_type=MESH)

        def passed(a, j, who):
            dst = slot(a, who, me if j == 3 else cidx[j])
            src = ins[a].at[me] if j == 3 else dst
            return pltpu.make_async_remote_copy(src, dst, fsend.at[a * 4 + j], frecv.at[a * 4 + j], device_id=sib,
                                                device_id_type=MESH)

        sends = [ici(a, j) for a in range(nb) for j in range(3)] + [passed(a, 3, c) for a in range(nb)]
        for cp in sends:
            cp.start()
        for a in range(nb):
            for j in range(3):
                landed(a, j).wait_recv()
                cp = passed(a, j, c)
                cp.start()
                sends.append(cp)
        for a in range(nb):
            for j in range(4):
                passed(a, j, 1 - c).wait_recv()
        for cp in sends:
            cp.wait_send()
        for cp in local:
            cp.wait()

    launch()


def _exchange_small(small, rep):
    def body(small_in, rep_in, small_out, rep_out, lsem, ssend, srecv):
        x, y, c, me, _, _ = _place()
        dev = 4 * x + 2 * y + c
        local = [pltpu.make_async_copy(small_in.at[me], small_out.at[dev], lsem.at[0]),
                 pltpu.make_async_copy(rep_in, rep_out.at[dev], lsem.at[1])]
        for cp in local:
            cp.start()

        def peer(r):
            return (1 - x if r & 4 else x), (1 - y if r & 2 else y), (1 - c if r & 1 else c)

        def tiny(r, which):
            px, py, pc = peer(r)
            k = (r - 1) * 2 + which
            if which == 0:
                return pltpu.make_async_remote_copy(small_in.at[2 * px + py], small_out.at[dev], ssend.at[k],
                                                    srecv.at[k], device_id=(px, py, pc), device_id_type=MESH)
            return pltpu.make_async_remote_copy(rep_in, rep_out.at[dev], ssend.at[k], srecv.at[k],
                                                device_id=(px, py, pc), device_id_type=MESH)

        def tiny_landed(r, which):
            px, py, pc = peer(r)
            k = (r - 1) * 2 + which
            dst = (small_out if which == 0 else rep_out).at[4 * px + 2 * py + pc]
            return pltpu.make_async_remote_copy(dst, dst, ssend.at[k], srecv.at[k], device_id=(px, py, pc),
                                                device_id_type=MESH)

        sends = [tiny(r, w) for r in range(1, NDEV) for w in range(2)]
        for cp in sends:
            cp.start()
        for r in range(1, NDEV):
            for w in range(2):
                tiny_landed(r, w).wait_recv()
        for cp in sends:
            cp.wait_send()
        for cp in local:
            cp.wait()

    dma = pltpu.SemaphoreType.DMA
    return pl.pallas_call(
        body, in_specs=[ANY] * 2, out_specs=[ANY] * 2,
        out_shape=[_sds((NDEV,) + small.shape[1:], F32), _sds((NDEV,) + rep.shape, F32)],
        scratch_shapes=[dma((2,)), dma((2 * (NDEV - 1),)), dma((2 * (NDEV - 1),))], name="exchange_small_grads",
    )(small, rep)


def _adamw_math(w, g, m, v):
    m = B1 * m + (1.0 - B1) * g
    v = B2 * v + (1.0 - B2) * (g * g)
    m_hat = m / (1.0 - B1 ** STEP)
    v_hat = v / (1.0 - B2 ** STEP)
    return -LR * (m_hat / (jnp.sqrt(v_hat) + AEPS) + WD * w), m, v


def _adamw_big(name, w, m, v, parts, row0=0, first=0, outs=None):
    _, _, rows, cols = w.shape
    n = parts.shape[2]
    tr = _row_tile(rows)
    t0 = row0 // tr

    def body(w_ref, m_ref, v_ref, p_ref, *rest):
        g_ref, d_ref, nm_ref, nv_ref = rest[-4:]
        g = p_ref[0].astype(F32)
        for q in range(1, NCHIP):
            g = g + p_ref[q].astype(F32)
        d, nm, nv = _adamw_math(w_ref[...], g, m_ref[...], v_ref[...])
        g_ref[...], d_ref[...], nm_ref[...], nv_ref[...] = g, d, nm, nv

    spec = pl.BlockSpec((None, None, tr, cols), lambda i, p, t: (first + i, p, t, 0))
    na = 0 if outs is None else 4
    return pl.pallas_call(
        body, grid=(n, 2, rows // tr),
        in_specs=[spec, spec, spec,
                  pl.BlockSpec((None, NCHIP, None, tr, cols), lambda i, p, t: (p, 0, i, t0 + t, 0))] + [ANY] * na,
        out_specs=[spec] * 4, out_shape=[_sds(w.shape)] * 4, input_output_aliases={4 + k: k for k in range(na)},
        name=name, compiler_params=_cp("parallel", "parallel", "parallel"),
    )(w, m, v, parts, *(outs or ()))


def _adamw_small(name, w, m, v, parts):
    def body(w_ref, m_ref, v_ref, p_ref, g_ref, d_ref, nm_ref, nv_ref):
        g = p_ref[0]
        for q in range(1, NDEV):
            g = g + p_ref[q]
        d, nm, nv = _adamw_math(w_ref[...], g, m_ref[...], v_ref[...])
        g_ref[...], d_ref[...], nm_ref[...], nv_ref[...] = g, d, nm, nv

    return pl.pallas_call(body, out_shape=[_sds(w.shape)] * 4, name=name)(w, m, v, parts)


def _pack(arrs, rows):
    flat = jnp.concatenate([a.reshape(-1) for a in arrs])
    return jnp.pad(flat, (0, rows * LANES - flat.shape[0])).reshape(rows, LANES)


def _unpack(packed, shapes):
    flat, out, o = packed.reshape(-1), [], 0
    for s in shapes:
        n = 1
        for d in s:
            n *= d
        out.append(flat[o:o + n].reshape(s))
        o += n
    return out


SMALL_ROWS, REP_ROWS = 200, 16


def kernel(x, norm_w, ffn_w_gate, ffn_w_up, ffn_w_down, mix_w_in, dn_conv_w, attn_sinks, dn_a_log, dn_dt_bias, dn_norm_w, mix_w_out, conv_w_pw1, conv_b_pw1, conv_w_dw, conv_b_dw, conv_ln_w, conv_ln_b, conv_w_pw2, conv_b_pw2, final_norm_w, loss_target, m_norm_w, m_ffn_w_gate, m_ffn_w_up, m_ffn_w_down, m_mix_w_in, m_dn_conv_w, m_attn_sinks, m_dn_a_log, m_dn_dt_bias, m_dn_norm_w, m_mix_w_out, m_conv_w_pw1, m_conv_b_pw1, m_conv_w_dw, m_conv_b_dw, m_conv_ln_w, m_conv_ln_b, m_conv_w_pw2, m_conv_b_pw2, m_final_norm_w, v_norm_w, v_ffn_w_gate, v_ffn_w_up, v_ffn_w_down, v_mix_w_in, v_dn_conv_w, v_attn_sinks, v_dn_a_log, v_dn_dt_bias, v_dn_norm_w, v_mix_w_out, v_conv_w_pw1, v_conv_b_pw1, v_conv_w_dw, v_conv_b_dw, v_conv_ln_w, v_conv_ln_b, v_conv_w_pw2, v_conv_b_pw2, v_final_norm_w):
    small_names = ["norm_w", "dn_conv_w", "conv_b_pw1", "conv_w_dw", "conv_b_dw", "conv_ln_w", "conv_ln_b",
                   "conv_b_pw2"]
    rep_names = ["attn_sinks", "dn_a_log", "dn_dt_bias", "dn_norm_w", "final_norm_w"]
    w = dict(norm_w=norm_w, ffn_w_gate=ffn_w_gate, ffn_w_up=ffn_w_up, ffn_w_down=ffn_w_down, mix_w_in=mix_w_in, dn_conv_w=dn_conv_w, attn_sinks=attn_sinks, dn_a_log=dn_a_log, dn_dt_bias=dn_dt_bias, dn_norm_w=dn_norm_w, mix_w_out=mix_w_out, conv_w_pw1=conv_w_pw1, conv_b_pw1=conv_b_pw1, conv_w_dw=conv_w_dw, conv_b_dw=conv_b_dw, conv_ln_w=conv_ln_w, conv_ln_b=conv_ln_b, conv_w_pw2=conv_w_pw2, conv_b_pw2=conv_b_pw2, final_norm_w=final_norm_w)
    m = dict(norm_w=m_norm_w, ffn_w_gate=m_ffn_w_gate, ffn_w_up=m_ffn_w_up, ffn_w_down=m_ffn_w_down, mix_w_in=m_mix_w_in, dn_conv_w=m_dn_conv_w, attn_sinks=m_attn_sinks, dn_a_log=m_dn_a_log, dn_dt_bias=m_dn_dt_bias, dn_norm_w=m_dn_norm_w, mix_w_out=m_mix_w_out, conv_w_pw1=m_conv_w_pw1, conv_b_pw1=m_conv_b_pw1, conv_w_dw=m_conv_w_dw, conv_b_dw=m_conv_b_dw, conv_ln_w=m_conv_ln_w, conv_ln_b=m_conv_ln_b, conv_w_pw2=m_conv_w_pw2, conv_b_pw2=m_conv_b_pw2, final_norm_w=m_final_norm_w)
    v = dict(norm_w=v_norm_w, ffn_w_gate=v_ffn_w_gate, ffn_w_up=v_ffn_w_up, ffn_w_down=v_ffn_w_down, mix_w_in=v_mix_w_in, dn_conv_w=v_dn_conv_w, attn_sinks=v_attn_sinks, dn_a_log=v_dn_a_log, dn_dt_bias=v_dn_dt_bias, dn_norm_w=v_dn_norm_w, mix_w_out=v_mix_w_out, conv_w_pw1=v_conv_w_pw1, conv_b_pw1=v_conv_b_pw1, conv_w_dw=v_conv_w_dw, conv_b_dw=v_conv_b_dw, conv_ln_w=v_conv_ln_w, conv_ln_b=v_conv_ln_b, conv_w_pw2=v_conv_w_pw2, conv_b_pw2=v_conv_b_pw2, final_norm_w=v_final_norm_w)
    order = ["norm_w", "ffn_w_gate", "ffn_w_up", "ffn_w_down", "mix_w_in", "dn_conv_w", "attn_sinks", "dn_a_log",
             "dn_dt_bias", "dn_norm_w", "mix_w_out", "conv_w_pw1", "conv_b_pw1", "conv_w_dw", "conv_b_dw",
             "conv_ln_w", "conv_ln_b", "conv_w_pw2", "conv_b_pw2", "final_norm_w"]

    small_shapes = [w[n].shape for n in small_names]
    rep_shapes = [w[n].shape for n in rep_names]

    def halves(a):
        return a.reshape(a.shape[:-2] + (2, a.shape[-2] // 2, a.shape[-1]))

    tr = lambda a: jnp.swapaxes(a, -1, -2)
    gate_t, up_t = tr(ffn_w_gate), tr(ffn_w_up)

    def layer_shards(l):
        mix_in, mix_out = (mix_w_in, mix_w_out) if l % 2 == 0 else (conv_w_pw1, conv_w_pw2)
        return [t.astype(BF16) for t in (jnp.concatenate([gate_t[l], up_t[l], ffn_w_down[l]], axis=1),
                                         halves(mix_in[l // 2]), halves(mix_out[l // 2]))]

    first = layer_shards(0) + [_pack([w[n] for n in small_names], SMALL_ROWS)]
    first, (gate_t, up_t, ffn_w_down, mix_w_in, mix_w_out, conv_w_pw1, conv_w_pw2) = lax.optimization_barrier(
        (first, (gate_t, up_t, ffn_w_down, mix_w_in, mix_w_out, conv_w_pw1, conv_w_pw2)))
    gathering = [_gather_async("gather_layer0", first[:3], first[3:])]
    gathering += [_gather_async(f"gather_layer{l}", layer_shards(l)) for l in range(1, DEPTH)]

    def mixer_params(l, w_a, w_b):
        e = l // 2
        w_a = w_a.reshape(NCHIP, D, -1)
        w_b = w_b.reshape(D, D)
        if l % 2 == 0:
            return dict(w_in=w_a, dn_conv_w=sm["dn_conv_w"][e], sinks=_row(attn_sinks[e]), a_log=_row(dn_a_log[e]),
                        dt_bias=_row(dn_dt_bias[e]), dn_norm_w=_row(dn_norm_w[e]), wo_a=w_b[:Q_A], wo_b=w_b[Q_A:])
        return dict(b1a=_row(sm["conv_b_pw1"][e, :D]), b1b=_row(sm["conv_b_pw1"][e, D:]), w1=w_a,
                    w_dw=sm["conv_w_dw"][e], b_dw=_row(sm["conv_b_dw"][e]), ln_w=_row(sm["conv_ln_w"][e]),
                    ln_b=_row(sm["conv_ln_b"][e]), b2=_row(sm["conv_b_pw2"][e]), w2=w_b)

    xs, saved, ffn_w = x[0], [], []
    for l in range(DEPTH):
        got = [r[...] for r in gathering[l]]
        if l == 0:
            per_chip = [_unpack(got[3][q], small_shapes) for q in range(NCHIP)]
            sm = {n: jnp.concatenate([per_chip[q][i] for q in range(NCHIP)], axis=-1)
                  for i, n in enumerate(small_names)}
        else:
            xs, got = lax.optimization_barrier((xs, got))
        ffn_w.append(got[0])
        xs, sv = _layer_fwd(l, xs, sm["norm_w"][l], got[0], mixer_params(l, got[1], got[2]))
        saved.append(sv)
    loss, dx, dfw = _final("final", xs, _row(final_norm_w), loss_target[0])

    hbm = pltpu.MemorySpace.HBM
    row_shapes = dict(ffn=(3 * FS, D), w_in=(D // 2, IN_COLS // NCHIP), w_out=(D // 8, D), pw1=(D // 2, D // 2),
                      pw2=(D // 8, D))
    new_sums = lambda k, n: jax.empty_ref(_sds((2, NCHIP, n) + row_shapes[k], BF16), memory_space=hbm)
    sums_0 = {k: new_sums(k, 1) for k in ("ffn", "w_in", "w_out")}
    sums = dict(ffn=new_sums("ffn", DEPTH - 1), w_in=new_sums("w_in", 1), w_out=new_sums("w_out", 1),
                pw1=new_sums("pw1", 2), pw2=new_sums("pw2", 2))
    c_arr = lax.axis_index("c").astype(jnp.int32).reshape(1)
    dnorm, gmix = [None] * DEPTH, [None] * DEPTH

    def hand_on(l, grads, swapped):
        def run(dx):
            dx, other = lax.optimization_barrier((dx, [r[...] for r in swapped]))
            parts = [_add_half(f"add_half_{l}_{k}", gg, rr, c_arr) for k, (gg, rr) in enumerate(zip(grads, other))]
            dx, parts = lax.optimization_barrier((dx, parts))
            keys = ("ffn", "w_in", "w_out") if l % 2 == 0 else ("ffn", "pw1", "pw2")
            if l == 0:
                _scatter_async("scatter_grads_0", parts, [sums_0[k] for k in keys], [0, 0, 0])
            else:
                _scatter_async(f"scatter_grads_{l}", parts, [sums[k] for k in keys],
                               [l - 1, 0, 0] if l % 2 == 0 else [l - 1, l // 2, l // 2])
            return dx
        return run

    pending = lambda dx: dx
    for l in reversed(range(DEPTH)):
        dx, dnorm[l], dffn, gmix[l] = _layer_bwd(l, dx, sm["norm_w"][l], ffn_w[l], saved[l], pending)
        if l % 2 == 0:
            g_a, g_b = gmix[l]["w_in"], jnp.concatenate([gmix[l]["wo_a"], gmix[l]["wo_b"]], axis=0)
        else:
            g_a, g_b = gmix[l]["w1"], gmix[l]["w2"]
        g_a = halves(g_a).astype(BF16)
        g_b = g_b.reshape(NCHIP, 2, D // 8, D).astype(BF16)
        dx, grads = lax.optimization_barrier((dx, [dffn, g_a, g_b]))
        pending = hand_on(l, grads, _swap_halves(f"swap_grads_{l}", grads, sums["ffn"] if l < DEPTH - 1 else None))
    gm, gc = [gmix[0], gmix[2]], [gmix[1], gmix[3]]
    small_g = dict(
        norm_w=jnp.stack(dnorm), dn_conv_w=jnp.stack([gm[e]["dn_conv_w"] for e in range(2)]),
        conv_b_pw1=jnp.stack([jnp.concatenate([gc[e]["b1a"], gc[e]["b1b"]], axis=1)[0] for e in range(2)]),
        conv_w_dw=jnp.stack([gc[e]["w_dw"] for e in range(2)]),
        conv_b_dw=jnp.stack([gc[e]["b_dw"][0] for e in range(2)]),
        conv_ln_w=jnp.stack([gc[e]["ln_w"][0] for e in range(2)]),
        conv_ln_b=jnp.stack([gc[e]["ln_b"][0] for e in range(2)]),
        conv_b_pw2=jnp.stack([gc[e]["b2"][0] for e in range(2)]))
    small_by_chip = jnp.stack([_pack([jnp.split(small_g[n], NCHIP, axis=-1)[q] for n in small_names], SMALL_ROWS)
                               for q in range(NCHIP)])
    rep_g = _pack([jnp.stack([gm[e]["sinks"][0] for e in range(2)]), jnp.stack([gm[e]["a_log"][0] for e in range(2)]),
                   jnp.stack([gm[e]["dt_bias"][0] for e in range(2)]),
                   jnp.stack([gm[e]["dn_norm_w"][0] for e in range(2)]), dfw[0]], REP_ROWS)
    small_sum, rep_sum = _exchange_small(small_by_chip, rep_g)
    dx, small_sum, rep_sum = lax.optimization_barrier((dx, small_sum, rep_sum))
    dx = pending(dx)

    big = (("ffn_w_gate", "ffn", 0), ("ffn_w_up", "ffn", FS), ("ffn_w_down", "ffn", 2 * FS), ("mix_w_in", "w_in", 0),
           ("mix_w_out", "w_out", 0), ("conv_w_pw1", "pw1", 0), ("conv_w_pw2", "pw2", 0))
    views = {n: (tr, tr) if n in ("ffn_w_gate", "ffn_w_up") else (
        (lambda a: a) if w[n].ndim == 4 else halves, lambda o, n=n: o.reshape(w[n].shape)) for n, _, _ in big}
    partial_sums = {k: r[...] for k, r in sums.items()}
    upper = {}
    for n, key, row0 in big:
        view = views[n][0]
        upper[n] = _adamw_big(f"adamw_{n}", view(w[n]), view(m[n]), view(v[n]), partial_sums[key], row0,
                              first=0 if key in ("pw1", "pw2") else 1)
    upper, partial_sums_0 = lax.optimization_barrier((upper, {k: r[...] for k, r in sums_0.items()}))
    res = {}
    for n, key, row0 in big:
        view, back = views[n]
        outs = upper[n] if key not in partial_sums_0 else _adamw_big(
            f"adamw_{n}_0", view(w[n]), view(m[n]), view(v[n]), partial_sums_0[key], row0, first=0, outs=upper[n])
        res[n] = [back(o) for o in outs]
    outs = _adamw_small("adamw_small", *[_pack([d[n] for n in small_names], SMALL_ROWS) for d in (w, m, v)],
                        small_sum)
    for i, n in enumerate(small_names):
        res[n] = [_unpack(o, small_shapes)[i] for o in outs]
    outs = _adamw_small("adamw_replicated", *[_pack([d[n] for n in rep_names], REP_ROWS) for d in (w, m, v)],
                        rep_sum)
    for i, n in enumerate(rep_names):
        res[n] = [_unpack(o, rep_shapes)[i] for o in outs]

    total = lax.psum(loss[0, 0], ("x", "y", "c"))
    return (total, dx[None], *[res[n][0] for n in order], *[res[n][1] for n in order],
            *[res[n][2] for n in order], *[res[n][3] for n in order])
```

```python
import jax
import jax.numpy as jnp
from jax import lax
from jax.experimental import pallas as pl
from jax.experimental.pallas import tpu as pltpu
from jax.experimental.pallas import tpu_sc as plsc

F32, BF16 = jnp.float32, jnp.bfloat16
MESH = pl.DeviceIdType.MESH
ANY = pl.BlockSpec(memory_space=pl.ANY)

T, D, F = 2048, 1024, 2816
DEPTH = 4
EPS = 1e-6
HEADS, HDIM, KV_HEADS, GROUP = 8, 64, 2, 4
WINDOW = BLOCK = 128
CHUNK = 64
NCHUNK = T // CHUNK
DN_CONV, CONV_WIDTH = 4, 31
Q_A, KV_A, QKV_B, V_B = 512, 128, 1536, 512
IN_COLS = 2832
IN_SPLITS = (0, 512, 640, 768, 2304, 2816, 2832)
NCHIP, NDEV = 4, 8
FS = F // NCHIP
LR, B1, B2, AEPS, WD, STEP = 0.001, 0.9, 0.999, 1e-08, 0.01, 10
V7X_VMEM_BYTES = 64 * 1024 * 1024
VMEM_LIMIT = V7X_VMEM_BYTES * 7 // 8
LANES = 128


def _cp(*sem):
    return pltpu.CompilerParams(dimension_semantics=sem, vmem_limit_bytes=VMEM_LIMIT)


def _sds(shape, dtype=F32):
    return jax.ShapeDtypeStruct(tuple(shape), dtype)


def _full(shape):
    nd = len(shape)
    return pl.BlockSpec(tuple(shape), lambda *_: (0,) * nd)


def _split_bf16(a):
    hi = a.astype(BF16)
    return hi, (a - hi.astype(F32)).astype(BF16)


def _dg(a, b, ca, cb, hi=False):
    if a.ndim == 3 and b.ndim == 3:
        dims = (((ca + 1,), (cb + 1,)), ((0,), (0,)))
    else:
        dims = (((ca,), (cb,)), ((), ()))
    dot = lambda p, q: lax.dot_general(p, q, dims, preferred_element_type=F32)
    if hi:
        a_hi, a_lo = _split_bf16(a.astype(F32))
        b_hi, b_lo = _split_bf16(b.astype(F32))
        return dot(a_hi, b_hi) + (dot(a_hi, b_lo) + dot(a_lo, b_hi))
    return dot(a.astype(BF16), b.astype(BF16))


def _make_mm(hi):
    @jax.custom_vjp
    def nn(a, b):
        return _dg(a, b, 1, 0, hi)

    @jax.custom_vjp
    def nt(a, b):
        return _dg(a, b, 1, 1, hi)

    @jax.custom_vjp
    def tn(a, b):
        return _dg(a, b, 0, 0, hi)

    nn.defvjp(lambda a, b: (_dg(a, b, 1, 0, hi), (a, b)),
              lambda r, g: (_dg(g, r[1], 1, 1, hi).astype(r[0].dtype), _dg(r[0], g, 0, 0, hi).astype(r[1].dtype)))
    nt.defvjp(lambda a, b: (_dg(a, b, 1, 1, hi), (a, b)),
              lambda r, g: (_dg(g, r[1], 1, 0, hi).astype(r[0].dtype), _dg(g, r[0], 0, 0, hi).astype(r[1].dtype)))
    tn.defvjp(lambda a, b: (_dg(a, b, 0, 0, hi), (a, b)),
              lambda r, g: (_dg(r[1], g, 1, 1, hi).astype(r[0].dtype), _dg(r[0], g, 1, 0, hi).astype(r[1].dtype)))
    return nn, nt, tn


_nn, _nt, _tn = _make_mm(False)
_nn_hi, _nt_hi, _tn_hi = _make_mm(True)


def _rms(x, w):
    return x * lax.rsqrt(jnp.mean(x * x, axis=-1, keepdims=True) + EPS) * w


def _layernorm(x, w, b):
    xc = x - jnp.mean(x, axis=-1, keepdims=True)
    return xc * lax.rsqrt(jnp.mean(xc * xc, axis=-1, keepdims=True) + EPS) * w + b


def _silu(x):
    return x * jax.nn.sigmoid(x)


def _iota2(shape, dim):
    return lax.broadcasted_iota(jnp.int32, shape, dim)


def _flat_weights(lhs_idx, weights):
    specs, ops, lhs_of, where = [], [], [], []
    for a, (k, w) in enumerate(zip(lhs_idx, weights)):
        for q in range(1 if w.ndim == 2 else w.shape[0]):
            specs.append(_full(w.shape) if w.ndim == 2
                         else pl.BlockSpec((None,) + w.shape[1:], lambda i, q=q: (q, 0, 0)))
            ops.append(w)
            lhs_of.append(k)
            where.append((a, None if w.ndim == 2 else q))
    return specs, ops, lhs_of, where


def _blk_fwd(name, pre, lhs_idx, post, toks, smalls, weights, outs, tm=512):
    wspecs, wops, lhs_of, _ = _flat_weights(lhs_idx, weights)
    nt_, ns, nw = len(toks), len(smalls), len(wops)

    def body(*refs):
        tv = [r[...] for r in refs[:nt_]]
        sv = [r[...] for r in refs[nt_:nt_ + ns]]
        wr = refs[nt_ + ns:nt_ + ns + nw]
        orf = refs[nt_ + ns + nw:]
        lhs = pre(tv, sv)
        ys = [_dg(lhs[i], w[...], 1, 0) for i, w in zip(lhs_of, wr)]
        for o_ref, o in zip(orf, post(ys, tv, sv)):
            o_ref[...] = o.astype(o_ref.dtype)

    in_specs = ([pl.BlockSpec((tm, a.shape[1]), lambda i: (i, 0)) for a in toks]
                + [_full(a.shape) for a in smalls] + wspecs)
    out_specs = [pl.BlockSpec((tm, w_), lambda i: (i, 0)) for w_, _ in outs]
    return pl.pallas_call(
        body, grid=(T // tm,), in_specs=in_specs, out_specs=out_specs,
        out_shape=[_sds((T, w_), dt) for w_, dt in outs], name=name, compiler_params=_cp("parallel"),
    )(*toks, *smalls, *wops)


def _blk_bwd(name, pre, lhs_idx, post, toks, smalls, weights, ct_groups, res=None, linear_post=False, tm=256,
             wchunk=512):
    wspecs, wops, lhs_of, where = _flat_weights(lhs_idx, weights)
    nt_, ns, nw, na = len(toks), len(smalls), len(wops), len(weights)
    cts = [a for g in ct_groups for a in g]
    nc = len(cts)
    widths = [sum(a.shape[1] for a in g) for g in ct_groups]
    has_res = res is not None

    def body(*refs):
        p = 0
        tr = refs[p:p + nt_]; p += nt_
        sr = refs[p:p + ns]; p += ns
        wr = refs[p:p + nw]; p += nw
        cr = refs[p:p + nc]; p += nc
        rr = refs[p:p + has_res]; p += has_res
        dtr = refs[p:p + nt_]; p += nt_
        dsr = refs[p:p + ns]; p += ns
        dwr = refs[p:p + na]; p += na
        scr = refs[p:]
        i = pl.program_id(0)

        @pl.when(i == 0)
        def _():
            for r in list(dsr) + list(dwr):
                r[...] = jnp.zeros_like(r)

        tv = [r[...] for r in tr]
        sv = [r[...] for r in sr]
        ctv, q, si = [], 0, 0
        for g in ct_groups:
            if len(g) == 1:
                ctv.append(cr[q][...].astype(F32))
            else:
                off = 0
                for j, a in enumerate(g):
                    scr[si][:, off:off + a.shape[1]] = cr[q + j][...].astype(F32)
                    off += a.shape[1]
                ctv.append(scr[si][...])
                si += 1
            q += len(g)

        lhs, vjp_pre = jax.vjp(lambda *a: tuple(pre(list(a[:nt_]), list(a[nt_:]))), *tv, *sv)
        lhs_b = [l.astype(BF16) for l in lhs]
        ys = [jnp.zeros((tm, w.shape[1]), F32) if linear_post else _dg(lhs_b[k], w[...], 1, 0)
              for k, w in zip(lhs_of, wr)]
        _, vjp_post = jax.vjp(lambda *a: tuple(post(list(a[:nw]), list(a[nw:nw + nt_]), list(a[nw + nt_:]))),
                              *ys, *tv, *sv)
        gp = vjp_post(tuple(ctv))
        dys, dt_post, ds_post = gp[:nw], gp[nw:nw + nt_], gp[nw + nt_:]
        dlhs = [None] * len(lhs)
        for k, w, dy, (a, q) in zip(lhs_of, wr, dys, where):
            dyb = dy.astype(BF16)
            n = w.shape[1]
            for c0 in range(0, n, wchunk):
                c1 = min(n, c0 + wchunk)
                part = _dg(lhs_b[k], dyb[:, c0:c1], 0, 0)
                if q is None:
                    dwr[a][:, c0:c1] += part
                else:
                    dwr[a][q, :, c0:c1] += part
            d = _dg(dyb, w[...], 1, 1)
            dlhs[k] = d if dlhs[k] is None else dlhs[k] + d
        gq = vjp_pre(tuple(d.astype(l.dtype) for d, l in zip(dlhs, lhs)))
        dt_pre, ds_pre = gq[:nt_], gq[nt_:]
        for j in range(nt_):
            d = dt_post[j] + dt_pre[j]
            if j == 0 and has_res:
                d = d + rr[0][...]
            dtr[j][...] = d
        for j in range(ns):
            dsr[j][...] += ds_post[j] + ds_pre[j]

    tok_spec = lambda a: pl.BlockSpec((tm, a.shape[1]), lambda i: (i, 0))
    in_specs = ([tok_spec(a) for a in toks] + [_full(a.shape) for a in smalls] + wspecs
                + [tok_spec(a) for a in cts] + ([tok_spec(res)] if has_res else []))
    out_specs = [tok_spec(a) for a in toks] + [_full(a.shape) for a in smalls] + [_full(w.shape) for w in weights]
    out_shape = ([_sds(a.shape) for a in toks] + [_sds(a.shape) for a in smalls] + [_sds(w.shape) for w in weights])
    scratch = [pltpu.VMEM((tm, wd), F32) for g, wd in zip(ct_groups, widths) if len(g) > 1]
    outs = pl.pallas_call(
        body, grid=(T // tm,), in_specs=in_specs, out_specs=out_specs, out_shape=out_shape,
        scratch_shapes=scratch, name=name, compiler_params=_cp("arbitrary"),
    )(*toks, *smalls, *wops, *cts, *([res] if has_res else []))
    return outs[:nt_], outs[nt_:nt_ + ns], outs[nt_ + ns:]


def _ffn_fwd(name, x, nw, ffn, idx, tm=512):
    def body(x_ref, nw_ref, wg_ref, wu_ref, wd_ref, o_ref, a_ref, b_ref, h_ref):
        s = pl.program_id(1)

        @pl.when(s == 0)
        def _():
            xv = x_ref[...]
            h_ref[...] = _rms(xv, nw_ref[...]).astype(BF16)
            o_ref[...] = xv

        h = h_ref[...]
        a = _dg(h, wg_ref[...], 1, 1).astype(BF16)
        b = _dg(h, wu_ref[...], 1, 1).astype(BF16)
        a_ref[...] = a
        b_ref[...] = b
        o_ref[...] += 0.5 * _dg(_swiglu_act(a, b)[0], wd_ref[...], 1, 0)

    wspec = lambda k: pl.BlockSpec((None, None, FS, D), lambda i, s: (s, idx, k, 0))
    act = pl.BlockSpec((None, tm, FS), lambda i, s: (s, i, 0))
    return pl.pallas_call(
        body, grid=(T // tm, NCHIP),
        in_specs=[pl.BlockSpec((tm, D), lambda i, s: (i, 0)), _full((1, D)), wspec(0), wspec(1), wspec(2)],
        out_specs=[pl.BlockSpec((tm, D), lambda i, s: (i, 0)), act, act, pl.BlockSpec((tm, D), lambda i, s: (i, 0))],
        out_shape=[_sds((T, D)), _sds((NCHIP, T, FS), BF16), _sds((NCHIP, T, FS), BF16), _sds((T, D), BF16)],
        name=name, compiler_params=_cp("parallel", "arbitrary"),
    )(x, nw, ffn, ffn, ffn)


def _swiglu_act(a, b):
    a, b = a.astype(F32), b.astype(F32)
    sa = jax.nn.sigmoid(a)
    act = a * sa
    return act * b, a, b, sa, act


def _ffn_bwd(name, x, nw, ffn, idx, pre, dy, gbuf=None, tm=512):
    ni = T // tm

    def body(x_ref, dy_ref, nw_ref, wg_ref, wu_ref, wd_ref, a_ref, b_ref, h_ref, dx_ref, dnw_ref, dffn_ref, dh_acc,
             ag, au, ad):
        s, i = pl.program_id(0), pl.program_id(1)
        rows = pl.ds(pl.multiple_of(i * tm, tm), tm)

        @pl.when((s == 0) & (i == 0))
        def _():
            dnw_ref[...] = jnp.zeros_like(dnw_ref)

        @pl.when(i == 0)
        def _():
            ag[...] = jnp.zeros_like(ag)
            au[...] = jnp.zeros_like(au)
            ad[...] = jnp.zeros_like(ad)

        hb = h_ref[...]
        gated, a, b, sa, act = _swiglu_act(a_ref[...], b_ref[...])
        dyb = (0.5 * dy_ref[...]).astype(BF16)
        ad[...] += _dg(gated, dyb, 0, 0)
        dact = _dg(dyb, wd_ref[...], 1, 1)
        da = (dact * b * (sa * (1.0 + a * (1.0 - sa)))).astype(BF16)
        db = (dact * act).astype(BF16)
        ag[...] += _dg(da, hb, 0, 0)
        au[...] += _dg(db, hb, 0, 0)
        dh = _dg(da, wg_ref[...], 1, 0) + _dg(db, wu_ref[...], 1, 0)

        @pl.when(s == 0)
        def _():
            dh_acc[rows, :] = dh

        @pl.when((s > 0) & (s < NCHIP - 1))
        def _():
            dh_acc[rows, :] += dh

        @pl.when(s == NCHIP - 1)
        def _():
            _, vjp_rms = jax.vjp(_rms, x_ref[...], nw_ref[...])
            dx, dnw = vjp_rms(dh_acc[rows, :] + dh)
            dx_ref[...] = dy_ref[...] + dx
            dnw_ref[...] += dnw

        @pl.when(i == ni - 1)
        def _():
            dffn_ref[0:FS, :] = ag[...].astype(BF16)
            dffn_ref[FS:2 * FS, :] = au[...].astype(BF16)
            dffn_ref[2 * FS:, :] = ad[...].astype(BF16)

    wspec = lambda r, k: pl.BlockSpec((None, None, r, D), lambda s, i: (s, idx, k, 0), pipeline_mode=pl.Buffered(1))
    last = lambda s, i: (jnp.where(s == NCHIP - 1, i, 0), 0)
    nb = 0 if gbuf is None else 1
    act = pl.BlockSpec((None, tm, FS), lambda s, i: (s, i, 0))
    tok = pl.BlockSpec((tm, D), lambda s, i: (i, 0))
    return pl.pallas_call(
        lambda *refs: body(*refs[:9], *refs[9 + nb:]), grid=(NCHIP, ni),
        in_specs=[pl.BlockSpec((tm, D), last), tok, _full((1, D)), wspec(FS, 0), wspec(FS, 1), wspec(FS, 2), act, act,
                  tok] + [ANY] * nb,
        out_specs=[pl.BlockSpec((tm, D), last), _full((1, D)), wspec(3 * FS, 0)],
        out_shape=[_sds((T, D)), _sds((1, D)), _sds(ffn.shape, BF16)],
        input_output_aliases={9 + k: 2 + k for k in range(nb)},
        scratch_shapes=[pltpu.VMEM((T, D), F32)] + [pltpu.VMEM((FS, D), F32)] * 3,
        name=name, compiler_params=_cp("arbitrary", "arbitrary"),
    )(x, dy, nw, ffn, ffn, ffn, *pre, *(() if gbuf is None else (gbuf,)))


CONV_ROWS = 256


def _conv_pad(k):
    return 8 * ((k - 1 + 7) // 8)


def _conv_fwd(name, x, w, b, act):
    k_w, c = w.shape
    tc = 256 if c % 256 == 0 else LANES
    pad = _conv_pad(k_w)
    has_b = b is not None

    def body(*refs):
        x_ref, w_ref = refs[0], refs[1]
        b_ref = refs[2] if has_b else None
        y_ref, xp = refs[2 + has_b], refs[3 + has_b]
        xp[0:pad, :] = jnp.zeros((pad, tc), F32)
        xp[pad:, :] = x_ref[...]

        def step(t, carry):
            base = pl.multiple_of(t * CONV_ROWS, CONV_ROWS)
            win = xp[pl.ds(base, CONV_ROWS + pad), :]
            acc = jnp.zeros((CONV_ROWS, tc), F32)
            for k in range(k_w):
                o = pad - (k_w - 1) + k
                acc = acc + w_ref[k:k + 1, :] * win[o:o + CONV_ROWS, :]
            if has_b:
                acc = acc + b_ref[...]
            y_ref[pl.ds(base, CONV_ROWS), :] = _silu(acc) if act else acc
            return carry

        lax.fori_loop(0, T // CONV_ROWS, step, 0)

    col = lambda r: pl.BlockSpec((r, tc), lambda j: (0, j))
    ins = [x, w] + ([b] if has_b else [])
    return pl.pallas_call(
        body, grid=(c // tc,), in_specs=[col(T), col(k_w)] + ([col(1)] if has_b else []), out_specs=col(T),
        out_shape=_sds((T, c)), scratch_shapes=[pltpu.VMEM((T + pad, tc), F32)], name=name,
        compiler_params=_cp("parallel"),
    )(*ins)


def _conv_bwd(name, x, w, b, act, dy):
    k_w, c = w.shape
    tc = 256 if c % 256 == 0 else LANES
    pad = _conv_pad(k_w)
    has_b = b is not None

    def body(*refs):
        x_ref, w_ref, dy_ref = refs[0], refs[1], refs[2]
        b_ref = refs[3] if has_b else None
        dx_ref, dw_ref, db_ref, xp, dp = refs[3 + has_b:]
        xp[0:pad, :] = jnp.zeros((pad, tc), F32)
        xp[pad:, :] = x_ref[...]
        dp[T:, :] = jnp.zeros((pad, tc), F32)
        dw_ref[...] = jnp.zeros_like(dw_ref)
        db_ref[...] = jnp.zeros_like(db_ref)

        def step1(t, carry):
            base = pl.multiple_of(t * CONV_ROWS, CONV_ROWS)
            d = dy_ref[pl.ds(base, CONV_ROWS), :]
            win = xp[pl.ds(base, CONV_ROWS + pad), :]
            offs = [pad - (k_w - 1) + k for k in range(k_w)]
            if act:
                acc = jnp.zeros((CONV_ROWS, tc), F32)
                for k, o in enumerate(offs):
                    acc = acc + w_ref[k:k + 1, :] * win[o:o + CONV_ROWS, :]
                if has_b:
                    acc = acc + b_ref[...]
                sg = jax.nn.sigmoid(acc)
                d = d * (sg * (1.0 + acc * (1.0 - sg)))
            dp[pl.ds(base, CONV_ROWS), :] = d
            for k, o in enumerate(offs):
                dw_ref[k:k + 1, :] += jnp.sum(d * win[o:o + CONV_ROWS, :], axis=0, keepdims=True)
            db_ref[...] += jnp.sum(d, axis=0, keepdims=True)
            return carry

        lax.fori_loop(0, T // CONV_ROWS, step1, 0)

        def step2(t, carry):
            base = pl.multiple_of(t * CONV_ROWS, CONV_ROWS)
            win = dp[pl.ds(base, CONV_ROWS + pad), :]
            acc = jnp.zeros((CONV_ROWS, tc), F32)
            for k in range(k_w):
                o = (k_w - 1) - k
                acc = acc + w_ref[k:k + 1, :] * win[o:o + CONV_ROWS, :]
            dx_ref[pl.ds(base, CONV_ROWS), :] = acc
            return carry

        lax.fori_loop(0, T // CONV_ROWS, step2, 0)

    col = lambda r: pl.BlockSpec((r, tc), lambda j: (0, j))
    ins = [x, w, dy] + ([b] if has_b else [])
    return pl.pallas_call(
        body, grid=(c // tc,), in_specs=[col(T), col(k_w), col(T)] + ([col(1)] if has_b else []),
        out_specs=[col(T), col(k_w), col(1)], out_shape=[_sds((T, c)), _sds((k_w, c)), _sds((1, c))],
        scratch_shapes=[pltpu.VMEM((T + pad, tc), F32), pltpu.VMEM((T + pad, tc), F32)], name=name,
        compiler_params=_cp("parallel"),
    )(*ins)


def _attn_consts(n):
    i = _iota2((BLOCK, 2 * BLOCK), 0)
    j = _iota2((BLOCK, 2 * BLOCK), 1)
    dist = i + BLOCK - j
    valid = (dist >= 0) & (dist < WINDOW) & ((n > 0) | (j >= BLOCK))
    return dist.astype(F32), valid


def _attn_block(q4, kk, vv, sinks, dist, valid, kv):
    outs = []
    lane = _iota2((1, HEADS), 1)
    for g in range(GROUP):
        h = kv * GROUP + g
        slope = 2.0 ** (-8.0 * (h + 1) / HEADS)
        s = _nt(q4[:, g * HDIM:(g + 1) * HDIM], kk) * (HDIM ** -0.5)
        s = jnp.where(valid, s - slope * dist, -1e30)
        sink = jnp.sum(jnp.where(lane == h, sinks, 0.0), axis=1, keepdims=True)
        m = jnp.maximum(jnp.max(s, axis=-1, keepdims=True), sink)
        e = jnp.exp(s - m)
        p = e / (jnp.sum(e, axis=-1, keepdims=True) + jnp.exp(sink - m))
        outs.append(_nn(p, vv))
    return tuple(outs)


def _attn_fwd(name, qa, ka, va, sinks):
    def body(q_ref, k_ref, v_ref, s_ref, o_ref, kp, vp):
        kp[0:BLOCK, :] = jnp.zeros((BLOCK, KV_A), F32)
        vp[0:BLOCK, :] = jnp.zeros((BLOCK, KV_A), F32)
        kp[BLOCK:, :] = k_ref[...]
        vp[BLOCK:, :] = v_ref[...]
        sinks_v = s_ref[...]

        def step(n, carry):
            r = pl.multiple_of(n * BLOCK, BLOCK)
            dist, valid = _attn_consts(n)
            k2 = kp[pl.ds(r, 2 * BLOCK), :]
            v2 = vp[pl.ds(r, 2 * BLOCK), :]
            for kv in range(KV_HEADS):
                q4 = q_ref[pl.ds(r, BLOCK), kv * GROUP * HDIM:(kv + 1) * GROUP * HDIM]
                og = _attn_block(q4, k2[:, kv * HDIM:(kv + 1) * HDIM], v2[:, kv * HDIM:(kv + 1) * HDIM], sinks_v,
                                 dist, valid, kv)
                for g in range(GROUP):
                    h = kv * GROUP + g
                    o_ref[pl.ds(r, BLOCK), h * HDIM:(h + 1) * HDIM] = og[g]
            return carry

        lax.fori_loop(0, T // BLOCK, step, 0)

    return pl.pallas_call(
        body, out_shape=_sds((T, Q_A)),
        scratch_shapes=[pltpu.VMEM((T + BLOCK, KV_A), F32), pltpu.VMEM((T + BLOCK, KV_A), F32)], name=name,
        compiler_params=pltpu.CompilerParams(vmem_limit_bytes=VMEM_LIMIT),
    )(qa, ka, va, sinks)


def _attn_bwd(name, qa, ka, va, sinks, do):
    def body(q_ref, k_ref, v_ref, s_ref, do_ref, dq_ref, dk_ref, dv_ref, ds_ref, kp, vp, dkp, dvp):
        kp[0:BLOCK, :] = jnp.zeros((BLOCK, KV_A), F32)
        vp[0:BLOCK, :] = jnp.zeros((BLOCK, KV_A), F32)
        kp[BLOCK:, :] = k_ref[...]
        vp[BLOCK:, :] = v_ref[...]
        dkp[...] = jnp.zeros_like(dkp)
        dvp[...] = jnp.zeros_like(dvp)
        ds_ref[...] = jnp.zeros_like(ds_ref)
        sinks_v = s_ref[...]

        def step(n, carry):
            r = pl.multiple_of(n * BLOCK, BLOCK)
            dist, valid = _attn_consts(n)
            k2 = kp[pl.ds(r, 2 * BLOCK), :]
            v2 = vp[pl.ds(r, 2 * BLOCK), :]
            for kv in range(KV_HEADS):
                cols = slice(kv * HDIM, (kv + 1) * HDIM)
                q4 = q_ref[pl.ds(r, BLOCK), kv * GROUP * HDIM:(kv + 1) * GROUP * HDIM]
                _, vjp = jax.vjp(lambda q, k, v, s: _attn_block(q, k, v, s, dist, valid, kv),
                                 q4, k2[:, cols], v2[:, cols], sinks_v)
                cts = tuple(do_ref[pl.ds(r, BLOCK), (kv * GROUP + g) * HDIM:(kv * GROUP + g + 1) * HDIM]
                            for g in range(GROUP))
                dq4, dkk, dvv, dsk = vjp(cts)
                dq_ref[pl.ds(r, BLOCK), kv * GROUP * HDIM:(kv + 1) * GROUP * HDIM] = dq4
                dkp[pl.ds(r, 2 * BLOCK), cols] += dkk
                dvp[pl.ds(r, 2 * BLOCK), cols] += dvv
                ds_ref[...] += dsk
            return carry

        lax.fori_loop(0, T // BLOCK, step, 0)
        dk_ref[...] = dkp[BLOCK:, :]
        dv_ref[...] = dvp[BLOCK:, :]

    pad = lambda: pltpu.VMEM((T + BLOCK, KV_A), F32)
    return pl.pallas_call(
        body, out_shape=[_sds((T, Q_A)), _sds((T, KV_A)), _sds((T, KV_A)), _sds((1, HEADS))],
        scratch_shapes=[pad(), pad(), pad(), pad()], name=name,
        compiler_params=pltpu.CompilerParams(vmem_limit_bytes=VMEM_LIMIT),
    )(qa, ka, va, sinks, do)


def _dn_consts():
    i = _iota2((CHUNK, CHUNK), 0)
    j = _iota2((CHUNK, CHUNK), 1)
    return dict(causal=i >= j, strict=i > j, eye=(i == j).astype(F32), ltri=(i >= j).astype(F32),
                ones=jnp.ones((CHUNK, CHUNK), F32), last=(_iota2((CHUNK, 1), 0) == CHUNK - 1).astype(F32))


def _l2norm(x):
    return x * lax.rsqrt(jnp.sum(x * x, axis=-1, keepdims=True) + EPS)


def _head_cols(m):
    lane = _iota2((1, HEADS), 1)
    return jnp.concatenate([jnp.sum(jnp.where(lane == h, m, 0.0), axis=1, keepdims=True)[None]
                            for h in range(HEADS)], axis=0)


@jax.custom_vjp
def _unit_lower_inverse(low, known):
    if known is not None:
        return known
    inv = (_iota2((CHUNK, CHUNK), 0) == _iota2((CHUNK, CHUNK), 1)).astype(F32) - low
    pw = low
    for _ in range(5):
        pw = _dg(pw, pw, 1, 0, True)
        inv = inv + _dg(inv, pw, 1, 0, True)
    return inv


def _unit_lower_inverse_fwd(low, known):
    inv = _unit_lower_inverse(low, known)
    return inv, (inv, known)


def _unit_lower_inverse_bwd(res, g):
    inv, known = res
    d_low = -_dg(inv, _dg(g, inv, 1, 1, True), 0, 0, True)
    return d_low, (None if known is None else jnp.zeros_like(known))


_unit_lower_inverse.defvjp(_unit_lower_inverse_fwd, _unit_lower_inverse_bwd)


def _dn_local(q3, k3, v3, braw, araw, alog, dtb, cs, known_inv=None):
    q = _l2norm(q3) * (HDIM ** -0.5)
    k = _l2norm(k3)
    g = -jnp.exp(alog) * jax.nn.softplus(araw + dtb)
    gc_all = _nn_hi(cs["ltri"], g)
    egc_all = jnp.exp(gc_all)
    beta, gc, egc = _head_cols(jax.nn.sigmoid(braw)), _head_cols(gc_all), _head_cols(egc_all)
    a = jnp.broadcast_to(gc, (HEADS, CHUNK, CHUNK))
    diff = a - jnp.swapaxes(a, 1, 2)
    decay = jnp.where(cs["causal"], jnp.exp(jnp.where(cs["causal"], diff, 0.0)), 0.0)
    kb = k * beta
    low = jnp.where(cs["strict"], _nt(kb, k) * decay, 0.0)
    inv = _unit_lower_inverse(low, known_inv)
    u = _nn_hi(inv, v3 * beta)
    w = _nn_hi(inv, kb * egc)
    attn = _nt(q, k) * decay
    gc_last = jnp.sum(gc * cs["last"], axis=1, keepdims=True)
    return u, w, attn, q * egc, k * jnp.exp(gc_last - gc), egc_all, inv


def _heads3(ref, off=0):
    return jnp.concatenate([ref[:, off + h * HDIM:off + (h + 1) * HDIM][None] for h in range(HEADS)], axis=0)


def _dn_local_fwd(name, qkv, ba, alog, dtb):
    def body(qkv_ref, ba_ref, al_ref, dt_ref, u_ref, w_ref, at_ref, qd_ref, kd_ref, eg_ref, inv_ref):
        bav = ba_ref[...]
        outs = _dn_local(_heads3(qkv_ref), _heads3(qkv_ref, 512), _heads3(qkv_ref, 1024), bav[:, :HEADS],
                         bav[:, HEADS:], al_ref[...], dt_ref[...], _dn_consts())
        for r, o in zip((u_ref, w_ref, at_ref, qd_ref, kd_ref, inv_ref), outs[:5] + outs[6:]):
            _unheads(r, o)
        eg_ref[...] = outs[5]

    row = lambda w_: pl.BlockSpec((CHUNK, w_), lambda n: (n, 0))
    return pl.pallas_call(
        body, grid=(NCHUNK,), in_specs=[row(QKV_B), row(2 * HEADS), _full((1, HEADS)), _full((1, HEADS))],
        out_specs=[row(V_B)] * 5 + [row(HEADS), row(V_B)],
        out_shape=[_sds((T, V_B))] * 5 + [_sds((T, HEADS)), _sds((T, V_B))], name=name,
        compiler_params=_cp("parallel"),
    )(qkv, ba, alog, dtb)


def _dn_local_bwd(name, qkv, ba, alog, dtb, inv, cts):
    def body(qkv_ref, ba_ref, al_ref, dt_ref, inv_ref, du_ref, dw_ref, dat_ref, dqd_ref, dkd_ref, deg_ref,
             dqkv_ref, dba_ref, dal_ref, ddt_ref):
        @pl.when(pl.program_id(0) == 0)
        def _():
            dal_ref[...] = jnp.zeros_like(dal_ref)
            ddt_ref[...] = jnp.zeros_like(ddt_ref)

        cs = _dn_consts()
        bav = ba_ref[...]
        known = _heads3(inv_ref)
        _, vjp = jax.vjp(lambda *a: _dn_local(*a, cs, known)[:6], _heads3(qkv_ref), _heads3(qkv_ref, 512),
                         _heads3(qkv_ref, 1024), bav[:, :HEADS], bav[:, HEADS:], al_ref[...], dt_ref[...])
        dq, dk, dv, dbr, dar, dal, ddt = vjp((_heads3(du_ref), _heads3(dw_ref), _heads3(dat_ref), _heads3(dqd_ref),
                                              _heads3(dkd_ref), deg_ref[...]))
        for h in range(HEADS):
            dqkv_ref[:, h * HDIM:(h + 1) * HDIM] = dq[h]
            dqkv_ref[:, 512 + h * HDIM:512 + (h + 1) * HDIM] = dk[h]
            dqkv_ref[:, 1024 + h * HDIM:1024 + (h + 1) * HDIM] = dv[h]
        dba_ref[:, :HEADS] = dbr
        dba_ref[:, HEADS:] = dar
        dal_ref[...] += dal
        ddt_ref[...] += ddt

    row = lambda w_: pl.BlockSpec((CHUNK, w_), lambda n: (n, 0))
    return pl.pallas_call(
        body, grid=(NCHUNK,),
        in_specs=[row(QKV_B), row(2 * HEADS), _full((1, HEADS)), _full((1, HEADS))] + [row(V_B)] * 6 + [row(HEADS)],
        out_specs=[row(QKV_B), row(2 * HEADS), _full((1, HEADS)), _full((1, HEADS))],
        out_shape=[_sds((T, QKV_B)), _sds((T, 2 * HEADS)), _sds((1, HEADS)), _sds((1, HEADS))], name=name,
        compiler_params=_cp("arbitrary"),
    )(qkv, ba, alog, dtb, inv, *cts)


def _dn_step(s, u, w, attn, qd, kd, egc, z, nw):
    last = (_iota2((CHUNK, 1), 0) == CHUNK - 1).astype(F32)
    gl = jnp.sum(_head_cols(egc) * last, axis=1, keepdims=True)
    v_new = u - _nn(w, s)
    o = _nn(qd, s) + _nn(attn, v_new)
    s_new = s * gl + _tn(kd, v_new)
    return s_new, _rms(o, nw) * _silu(z)


def _unheads(ref, v3):
    for h in range(HEADS):
        ref[:, h * HDIM:(h + 1) * HDIM] = v3[h]


def _dn_rec_fwd(name, u, w, attn, qd, kd, egc, z, nw):
    def body(u_ref, w_ref, at_ref, qd_ref, kd_ref, eg_ref, z_ref, nw_ref, o_ref, ss_ref, s_scr):
        @pl.when(pl.program_id(0) == 0)
        def _():
            s_scr[...] = jnp.zeros_like(s_scr)

        s = s_scr[...]
        ss_ref[...] = s
        s_new, on = _dn_step(s, _heads3(u_ref), _heads3(w_ref), _heads3(at_ref), _heads3(qd_ref), _heads3(kd_ref),
                             eg_ref[...], _heads3(z_ref), nw_ref[...])
        s_scr[...] = s_new
        _unheads(o_ref, on)

    row = lambda w_: pl.BlockSpec((CHUNK, w_), lambda n: (n, 0))
    return pl.pallas_call(
        body, grid=(NCHUNK,), in_specs=[row(V_B)] * 5 + [row(HEADS), row(V_B), _full((1, HDIM))],
        out_specs=[row(V_B), pl.BlockSpec((None, HEADS, HDIM, HDIM), lambda n: (n, 0, 0, 0))],
        out_shape=[_sds((T, V_B)), _sds((NCHUNK, HEADS, HDIM, HDIM))],
        scratch_shapes=[pltpu.VMEM((HEADS, HDIM, HDIM), F32)], name=name, compiler_params=_cp("arbitrary"),
    )(u, w, attn, qd, kd, egc, z, nw)


def _dn_rec_bwd(name, u, w, attn, qd, kd, egc, z, nw, ss, do):
    def body(u_ref, w_ref, at_ref, qd_ref, kd_ref, eg_ref, z_ref, nw_ref, ss_ref, do_ref,
             du_ref, dw_ref, dat_ref, dqd_ref, dkd_ref, deg_ref, dz_ref, dnw_ref, ds_scr):
        @pl.when(pl.program_id(0) == 0)
        def _():
            ds_scr[...] = jnp.zeros_like(ds_scr)
            dnw_ref[...] = jnp.zeros_like(dnw_ref)

        _, vjp = jax.vjp(_dn_step, ss_ref[...], _heads3(u_ref), _heads3(w_ref), _heads3(at_ref), _heads3(qd_ref),
                         _heads3(kd_ref), eg_ref[...], _heads3(z_ref), nw_ref[...])
        ds, du, dw, dat, dqd, dkd, deg, dz, dnw = vjp((ds_scr[...], _heads3(do_ref)))
        ds_scr[...] = ds
        for r, v in zip((du_ref, dw_ref, dat_ref, dqd_ref, dkd_ref, dz_ref), (du, dw, dat, dqd, dkd, dz)):
            _unheads(r, v)
        deg_ref[...] = deg
        dnw_ref[...] += dnw

    row = lambda w_: pl.BlockSpec((CHUNK, w_), lambda n: (NCHUNK - 1 - n, 0))
    return pl.pallas_call(
        body, grid=(NCHUNK,),
        in_specs=[row(V_B)] * 5 + [row(HEADS), row(V_B), _full((1, HDIM)),
                                   pl.BlockSpec((None, HEADS, HDIM, HDIM), lambda n: (NCHUNK - 1 - n, 0, 0, 0)),
                                   row(V_B)],
        out_specs=[row(V_B)] * 5 + [row(HEADS), row(V_B), _full((1, HDIM))],
        out_shape=[_sds((T, V_B))] * 5 + [_sds((T, HEADS)), _sds((T, V_B)), _sds((1, HDIM))],
        scratch_shapes=[pltpu.VMEM((HEADS, HDIM, HDIM), F32)], name=name, compiler_params=_cp("arbitrary"),
    )(u, w, attn, qd, kd, egc, z, nw, ss, do)


def _final(name, x, fw, target, tm=512):
    def body(x_ref, fw_ref, t_ref, l_ref, dx_ref, dfw_ref):
        @pl.when(pl.program_id(0) == 0)
        def _():
            l_ref[...] = jnp.zeros_like(l_ref)
            dfw_ref[...] = jnp.zeros_like(dfw_ref)

        tv = t_ref[...]

        def f(xv, fwv):
            err = _rms(xv, fwv) - tv
            per_tok = jnp.mean(err * err, axis=-1, keepdims=True)
            return 0.5 * jnp.sum(per_tok, axis=0, keepdims=True)

        loss, vjp = jax.vjp(f, x_ref[...], fw_ref[...])
        dx, dfw = vjp(jnp.ones((1, 1), F32))
        l_ref[...] += loss
        dx_ref[...] = dx
        dfw_ref[...] += dfw

    tok = pl.BlockSpec((tm, D), lambda i: (i, 0))
    return pl.pallas_call(
        body, grid=(T // tm,), in_specs=[tok, _full((1, D)), tok], out_specs=[_full((1, 1)), tok, _full((1, D))],
        out_shape=[_sds((1, 1)), _sds((T, D)), _sds((1, D))], name=name, compiler_params=_cp("arbitrary"),
    )(x, fw, target)


def _m1_pre(tv, sv):
    return [_rms(tv[0], sv[0])]


def _m1_post(ys, tv, sv):
    return (jnp.concatenate(ys, axis=1),)


def _m1_post_split(ys, tv, sv):
    proj = jnp.concatenate(ys, axis=1)
    return tuple(proj[:, a:b] for a, b in zip(IN_SPLITS[:-1], IN_SPLITS[1:]))


def _m5_pre(tv, sv):
    return [tv[1], tv[2]]


def _m5_post(ys, tv, sv):
    return (tv[0] + ys[0] + ys[1],)


def _c1_pre(tv, sv):
    return [_rms(tv[0], sv[0])]


def _c1_post(ys, tv, sv):
    return ((jnp.concatenate(ys[:2], axis=1) + sv[1]) * jax.nn.sigmoid(jnp.concatenate(ys[2:], axis=1) + sv[2]),)


def _c3_pre(tv, sv):
    return [_silu(_layernorm(tv[0], sv[0], sv[1]))]


def _c3_post(ys, tv, sv):
    return (tv[1] + ys[0] + sv[2],)


def _row(v):
    return v.reshape(1, -1)


def _mixer_fwd(tag, x, p):
    parts = _blk_fwd(f"m1_fwd_{tag}", _m1_pre, [0], _m1_post_split, [x], [p["nw"]], [p["w_in"]],
                     [(b - a, F32) for a, b in zip(IN_SPLITS[:-1], IN_SPLITS[1:])])
    qa, ka, va, qkvb, z, ba = parts
    att = _attn_fwd(f"attn_fwd_{tag}", qa, ka, va, p["sinks"])
    qkvc = _conv_fwd(f"dnconv_fwd_{tag}", qkvb, p["dn_conv_w"], None, True)
    *loc, inv = _dn_local_fwd(f"dnloc_fwd_{tag}", qkvc, ba, p["a_log"], p["dt_bias"])
    og, ss = _dn_rec_fwd(f"dnrec_fwd_{tag}", *loc, z, p["dn_norm_w"])
    (out,) = _blk_fwd(f"m5_fwd_{tag}", _m5_pre, [0, 1], _m5_post, [x, att, og], [], [p["wo_a"], p["wo_b"]],
                      [(D, F32)])
    return out, dict(x=x, qa=qa, ka=ka, va=va, qkvb=qkvb, z=z, ba=ba, att=att, qkvc=qkvc, loc=loc, inv=inv, og=og,
                     ss=ss)


def _mixer_bwd(tag, dy, p, s):
    (dxa, datt, dog), _, (dwo_a, dwo_b) = _blk_bwd(f"m5_bwd_{tag}", _m5_pre, [0, 1], _m5_post,
                                                   [s["x"], s["att"], s["og"]], [], [p["wo_a"], p["wo_b"]], [[dy]],
                                                   linear_post=True)
    rec = _dn_rec_bwd(f"dnrec_bwd_{tag}", *s["loc"], s["z"], p["dn_norm_w"], s["ss"], dog)
    dz, dnw_dn = rec[6], rec[7]
    dqkvc, dba, dalog, ddtb = _dn_local_bwd(f"dnloc_bwd_{tag}", s["qkvc"], s["ba"], p["a_log"], p["dt_bias"],
                                            s["inv"], rec[:6])
    dqkvb, dconvw, _ = _conv_bwd(f"dnconv_bwd_{tag}", s["qkvb"], p["dn_conv_w"], None, True, dqkvc)
    dqa, dka, dva, dsinks = _attn_bwd(f"attn_bwd_{tag}", s["qa"], s["ka"], s["va"], p["sinks"], datt)
    (dx,), (dnw,), (dw_in,) = _blk_bwd(f"m1_bwd_{tag}", _m1_pre, [0], _m1_post, [s["x"]], [p["nw"]], [p["w_in"]],
                                       [[dqa, dka, dva, dqkvb, dz, dba]], res=dxa, linear_post=True)
    return dx, dict(nw=dnw, w_in=dw_in, wo_a=dwo_a, wo_b=dwo_b, dn_conv_w=dconvw, sinks=dsinks, a_log=dalog,
                    dt_bias=ddtb, dn_norm_w=dnw_dn)


def _conformer_fwd(tag, x, p):
    (glu,) = _blk_fwd(f"c1_fwd_{tag}", _c1_pre, [0], _c1_post, [x], [p["nw"], p["b1a"], p["b1b"]], [p["w1"]],
                      [(D, F32)])
    cc = _conv_fwd(f"dwconv_fwd_{tag}", glu, p["w_dw"], p["b_dw"], False)
    (out,) = _blk_fwd(f"c3_fwd_{tag}", _c3_pre, [0], _c3_post, [cc, x], [p["ln_w"], p["ln_b"], p["b2"]], [p["w2"]],
                      [(D, F32)])
    return out, dict(x=x, glu=glu, cc=cc)


def _conformer_bwd(tag, dy, p, s):
    (dcc, dxa), (dlnw, dlnb, db2), (dw2,) = _blk_bwd(f"c3_bwd_{tag}", _c3_pre, [0], _c3_post, [s["cc"], s["x"]],
                                                     [p["ln_w"], p["ln_b"], p["b2"]], [p["w2"]], [[dy]],
                                                     linear_post=True)
    dglu, dwdw, dbdw = _conv_bwd(f"dwconv_bwd_{tag}", s["glu"], p["w_dw"], p["b_dw"], False, dcc)
    (dx,), (dnw, db1a, db1b), (dw1,) = _blk_bwd(f"c1_bwd_{tag}", _c1_pre, [0], _c1_post, [s["x"]],
                                                [p["nw"], p["b1a"], p["b1b"]], [p["w1"]], [[dglu]], res=dxa)
    return dx, dict(nw=dnw, b1a=db1a, b1b=db1b, w1=dw1, w_dw=dwdw, b_dw=dbdw, ln_w=dlnw, ln_b=dlnb, b2=db2, w2=dw2)


def _layer_fwd(l, x, nw, ffn, p):
    x1, *pre_a = _ffn_fwd(f"ffn_fwd_{l}a", x, _row(nw[0]), ffn, 0)
    p = dict(p, nw=_row(nw[1]))
    x2, sv = (_mixer_fwd if l % 2 == 0 else _conformer_fwd)(str(l), x1, p)
    out, *pre_b = _ffn_fwd(f"ffn_fwd_{l}b", x2, _row(nw[2]), ffn, 1)
    return out, (x, x2, p, sv, pre_a, pre_b)


def _layer_bwd(l, dx, nw, ffn, saved, after_first=lambda dx: dx):
    x0, x2, p, sv, pre_a, pre_b = saved
    dx, dn2, dffn = _ffn_bwd(f"ffn_bwd_{l}b", x2, _row(nw[2]), ffn, 1, pre_b, dx)
    dx = after_first(dx)
    dx, dmix = (_mixer_bwd if l % 2 == 0 else _conformer_bwd)(str(l), dx, p, sv)
    dx, dn0, dffn = _ffn_bwd(f"ffn_bwd_{l}a", x0, _row(nw[0]), ffn, 0, pre_a, dx, dffn)
    return dx, jnp.concatenate([dn0, dmix.pop("nw"), dn2], axis=0), dffn, dmix


def _place():
    x, y, c = lax.axis_index("x"), lax.axis_index("y"), lax.axis_index("c")
    chips = [(1 - x, y), (x, 1 - y), (1 - x, 1 - y)]
    return x, y, c, 2 * x + y, chips, [2 * px + py for px, py in chips]


def _handshake(peers):
    barrier = pltpu.get_barrier_semaphore()
    for p in peers:
        pl.semaphore_signal(barrier, inc=1, device_id=p, device_id_type=MESH)
    pl.semaphore_wait(barrier, len(peers))


def _chip_peers():
    x, y, c, _, chips, _ = _place()
    return [(*chip, c) for chip in chips] + [(x, y, 1 - c)]


def _gather_copies(ins, outs, nb, send, recv, fsend, frecv, lsem):
    n_in = len(ins)
    x, y, c, me, chips, cidx = _place()
    sib = (x, y, 1 - c)
    local = [pltpu.make_async_copy(ins[a], outs[a].at[me], lsem.at[a]) for a in range(n_in)]

    def region(a, k, who):
        if k < 2:
            return outs[a].at[cidx[k], pl.ds(who, 1)]
        r = ins[a].shape[1] // 2
        return outs[a].at[cidx[2], pl.ds(who, 1), pl.ds((k - 2) * r, r)]

    def hop(a, k):
        if k < 2:
            src, dst = ins[a].at[pl.ds(c, 1)], outs[a].at[me, pl.ds(c, 1)]
        else:
            r = ins[a].shape[1] // 2
            src = dst = outs[a].at[cidx[3 - k], pl.ds(c, 1), pl.ds((k - 2) * r, r)]
        return pltpu.make_async_remote_copy(src, dst, send.at[4 * a + k], recv.at[4 * a + k],
                                            device_id=(*chips[k % 2], c), device_id_type=MESH)

    def landed(a, k):
        dst = region(a, k, c)
        return pltpu.make_async_remote_copy(dst, dst, send.at[4 * a + k], recv.at[4 * a + k],
                                            device_id=(*chips[k % 2], c), device_id_type=MESH)

    def passed(a, k, who):
        part = region(a, k, who)
        return pltpu.make_async_remote_copy(part, part, fsend.at[4 * a + k], frecv.at[4 * a + k], device_id=sib,
                                            device_id_type=MESH)

    def direct(a, j):
        k = 4 * nb + 3 * (a - nb) + j
        return pltpu.make_async_remote_copy(ins[a], outs[a].at[me], send.at[k], recv.at[k],
                                            device_id=(*chips[j], c), device_id_type=MESH)

    def direct_landed(a, j):
        k = 4 * nb + 3 * (a - nb) + j
        dst = outs[a].at[cidx[j]]
        return pltpu.make_async_remote_copy(dst, dst, send.at[k], recv.at[k], device_id=(*chips[j], c),
                                            device_id_type=MESH)

    sends = [hop(a, k) for a in range(nb) for k in range(2)] + [direct(a, j) for a in range(nb, n_in) for j in range(3)]
    for cp in sends:
        cp.start()
    for cp in local:
        cp.start()
    for a in range(nb):
        for k in (1, 0):
            landed(a, k).wait_recv()
            for cp in (hop(a, 3 - k), passed(a, k, c)):
                cp.start()
                sends.append(cp)
    for a in range(nb):
        for k in (2, 3):
            landed(a, k).wait_recv()
            cp = passed(a, k, c)
            cp.start()
            sends.append(cp)
    for a in range(nb, n_in):
        for j in range(3):
            direct_landed(a, j).wait_recv()
    for a in range(nb):
        for k in range(4):
            passed(a, k, 1 - c).wait_recv()
    for cp in sends:
        cp.wait_send()
    for cp in local:
        cp.wait()


def _gather_sems(n_in, nb):
    dma = pltpu.SemaphoreType.DMA
    n_ici = 4 * nb + 3 * (n_in - nb)
    return [dma((n_ici,)), dma((n_ici,)), dma((4 * nb,)), dma((4 * nb,)), dma((n_in,))]


def _gather_async(name, halved, whole=()):
    nb, arrs = len(halved), list(halved) + list(whole)
    hbm = pltpu.MemorySpace.HBM
    ins = [jax.new_ref(a, memory_space=hbm) for a in arrs]
    outs = [jax.empty_ref(_sds((NCHIP,) + a.shape, a.dtype), memory_space=hbm) for a in arrs]

    @pl.kernel(mesh=plsc.ScalarSubcoreMesh(axis_name="seq", num_cores=1), name=name,
               scratch_types=tuple(_gather_sems(len(arrs), nb)),
               compiler_params=pltpu.CompilerParams(collective_id=2))
    def launch(send, recv, fsend, frecv, lsem):
        _handshake(_chip_peers())
        _gather_copies(ins, outs, nb, send, recv, fsend, frecv, lsem)

    launch()
    return outs


def _swap_halves(name, grads, after=None):
    n = len(grads)
    hbm = pltpu.MemorySpace.HBM
    ins = [jax.new_ref(g, memory_space=hbm) for g in grads]
    outs = [jax.empty_ref(_sds((NCHIP, g.shape[1] // 2) + g.shape[2:], g.dtype), memory_space=hbm) for g in grads]
    tile = (2 * 8, LANES)
    token = None if after is None else jax.empty_ref(_sds(tile, BF16), memory_space=hbm)

    @pl.kernel(mesh=plsc.ScalarSubcoreMesh(axis_name="seq", num_cores=1), name=name,
               scratch_types=(pltpu.SemaphoreType.DMA((n + 1,)), pltpu.SemaphoreType.DMA((n,))),
               compiler_params=pltpu.CompilerParams(collective_id=1))
    def launch(send, recv):
        x, y, c, _, _, _ = _place()
        sib = (x, y, 1 - c)
        _handshake([sib])
        if after is not None:
            tick = pltpu.make_async_copy(after.at[0, 0, 0, pl.ds(0, tile[0]), pl.ds(0, tile[1])], token, send.at[n])
            tick.start()
            tick.wait()
        cps = []
        for a in range(n):
            h = grads[a].shape[1] // 2
            cps.append(pltpu.make_async_remote_copy(ins[a].at[:, pl.ds((1 - c) * h, h)], outs[a], send.at[a],
                                                    recv.at[a], device_id=sib, device_id_type=MESH))
        for cp in cps:
            cp.start()
        for cp in cps:
            cp.wait()

    launch()
    return outs


def _row_tile(r, cap=256):
    return max(t for t in range(8, cap + 1, 8) if r % t == 0)


def _add_half(name, g, r, c_arr):
    _, l, rows, cols = g.shape
    h = l // 2
    tr = _row_tile(rows, 1056)

    def body(c_ref, g_ref, r_ref, o_ref):
        o_ref[...] = (g_ref[...].astype(F32) + r_ref[...].astype(F32)).astype(BF16)

    blk = (None, None, tr, cols)
    return pl.pallas_call(
        body,
        grid_spec=pltpu.PrefetchScalarGridSpec(
            num_scalar_prefetch=1, grid=(NCHIP, h, rows // tr),
            in_specs=[pl.BlockSpec(blk, lambda j, i, t, c_ref: (j, c_ref[0] * h + i, t, 0)),
                      pl.BlockSpec(blk, lambda j, i, t, c_ref: (j, i, t, 0))],
            out_specs=pl.BlockSpec(blk, lambda j, i, t, c_ref: (j, i, t, 0))),
        out_shape=_sds((NCHIP, h, rows, cols), BF16), name=name,
        compiler_params=_cp("parallel", "parallel", "parallel"),
    )(c_arr, g, r)


def _scatter_async(name, parts, sums, where):
    nb = len(parts)
    ins = [jax.new_ref(p, memory_space=pltpu.MemorySpace.HBM) for p in parts]
    dma = pltpu.SemaphoreType.DMA

    @pl.kernel(mesh=plsc.ScalarSubcoreMesh(axis_name="seq", num_cores=1), name=name,
               scratch_types=(dma((3 * nb,)), dma((3 * nb,)), dma((4 * nb,)), dma((4 * nb,)), dma((nb,))),
               compiler_params=pltpu.CompilerParams(collective_id=3))
    def launch(send, recv, fsend, frecv, lsem):
        _handshake(_chip_peers())
        x, y, c, me, chips, cidx = _place()
        sib = (x, y, 1 - c)

        def slot(a, half, chip):
            return sums[a].at[half, chip, pl.ds(where[a], 1)]

        local = [pltpu.make_async_copy(ins[a].at[me], slot(a, c, me), lsem.at[a]) for a in range(nb)]
        for cp in local:
            cp.start()

        def ici(a, j):
            return pltpu.make_async_remote_copy(ins[a].at[cidx[j]], slot(a, c, me), send.at[a * 3 + j],
                                                recv.at[a * 3 + j], device_id=(*chips[j], c), device_id_type=MESH)

        def landed(a, j):
            dst = slot(a, c, cidx[j])
            return pltpu.make_async_remote_copy(dst, dst, send.at[a * 3 + j], recv.at[a * 3 + j],
                                                device_id=(*chips[j], c), device_id_type=MESH)

        def passed(a, j, who):
            dst = slot(a, who, me if j == 3 else cidx[j])
            src = ins[a].at[me] if j == 3 else dst
            return pltpu.make_async_remote_copy(src, dst, fsend.at[a * 4 + j], frecv.at[a * 4 + j], device_id=sib,
                                                device_id_type=MESH)

        sends = [ici(a, j) for a in range(nb) for j in range(3)] + [passed(a, 3, c) for a in range(nb)]
        for cp in sends:
            cp.start()
        for a in range(nb):
            for j in range(3):
                landed(a, j).wait_recv()
                cp = passed(a, j, c)
                cp.start()
                sends.append(cp)
        for a in range(nb):
            for j in range(4):
                passed(a, j, 1 - c).wait_recv()
        for cp in sends:
            cp.wait_send()
        for cp in local:
            cp.wait()

    launch()


def _exchange_small(small, rep):
    def body(small_in, rep_in, small_out, rep_out, lsem, ssend, srecv):
        x, y, c, me, _, _ = _place()
        dev = 4 * x + 2 * y + c
        local = [pltpu.make_async_copy(small_in.at[me], small_out.at[dev], lsem.at[0]),
                 pltpu.make_async_copy(rep_in, rep_out.at[dev], lsem.at[1])]
        for cp in local:
            cp.start()

        def peer(r):
            return (1 - x if r & 4 else x), (1 - y if r & 2 else y), (1 - c if r & 1 else c)

        def tiny(r, which):
            px, py, pc = peer(r)
            k = (r - 1) * 2 + which
            if which == 0:
                return pltpu.make_async_remote_copy(small_in.at[2 * px + py], small_out.at[dev], ssend.at[k],
                                                    srecv.at[k], device_id=(px, py, pc), device_id_type=MESH)
            return pltpu.make_async_remote_copy(rep_in, rep_out.at[dev], ssend.at[k], srecv.at[k],
                                                device_id=(px, py, pc), device_id_type=MESH)

        def tiny_landed(r, which):
            px, py, pc = peer(r)
            k = (r - 1) * 2 + which
            dst = (small_out if which == 0 else rep_out).at[4 * px + 2 * py + pc]
            return pltpu.make_async_remote_copy(dst, dst, ssend.at[k], srecv.at[k], device_id=(px, py, pc),
                                                device_id_type=MESH)

        sends = [tiny(r, w) for r in range(1, NDEV) for w in range(2)]
        for cp in sends:
            cp.start()
        for r in range(1, NDEV):
            for w in range(2):
                tiny_landed(r, w).wait_recv()
        for cp in sends:
            cp.wait_send()
        for cp in local:
            cp.wait()

    dma = pltpu.SemaphoreType.DMA
    return pl.pallas_call(
        body, in_specs=[ANY] * 2, out_specs=[ANY] * 2,
        out_shape=[_sds((NDEV,) + small.shape[1:], F32), _sds((NDEV,) + rep.shape, F32)],
        scratch_shapes=[dma((2,)), dma((2 * (NDEV - 1),)), dma((2 * (NDEV - 1),))], name="exchange_small_grads",
    )(small, rep)


def _adamw_math(w, g, m, v):
    m = B1 * m + (1.0 - B1) * g
    v = B2 * v + (1.0 - B2) * (g * g)
    m_hat = m / (1.0 - B1 ** STEP)
    v_hat = v / (1.0 - B2 ** STEP)
    return -LR * (m_hat / (jnp.sqrt(v_hat) + AEPS) + WD * w), m, v


def _adamw_big(name, w, m, v, parts, row0=0, first=0, outs=None):
    _, _, rows, cols = w.shape
    n = parts.shape[2]
    tr = _row_tile(rows)
    t0 = row0 // tr

    def body(w_ref, m_ref, v_ref, p_ref, *rest):
        g_ref, d_ref, nm_ref, nv_ref = rest[-4:]
        g = p_ref[0].astype(F32)
        for q in range(1, NCHIP):
            g = g + p_ref[q].astype(F32)
        d, nm, nv = _adamw_math(w_ref[...], g, m_ref[...], v_ref[...])
        g_ref[...], d_ref[...], nm_ref[...], nv_ref[...] = g, d, nm, nv

    spec = pl.BlockSpec((None, None, tr, cols), lambda i, p, t: (first + i, p, t, 0))
    na = 0 if outs is None else 4
    return pl.pallas_call(
        body, grid=(n, 2, rows // tr),
        in_specs=[spec, spec, spec,
                  pl.BlockSpec((None, NCHIP, None, tr, cols), lambda i, p, t: (p, 0, i, t0 + t, 0))] + [ANY] * na,
        out_specs=[spec] * 4, out_shape=[_sds(w.shape)] * 4, input_output_aliases={4 + k: k for k in range(na)},
        name=name, compiler_params=_cp("parallel", "parallel", "parallel"),
    )(w, m, v, parts, *(outs or ()))


def _adamw_small(name, w, m, v, parts):
    def body(w_ref, m_ref, v_ref, p_ref, g_ref, d_ref, nm_ref, nv_ref):
        g = p_ref[0]
        for q in range(1, NDEV):
            g = g + p_ref[q]
        d, nm, nv = _adamw_math(w_ref[...], g, m_ref[...], v_ref[...])
        g_ref[...], d_ref[...], nm_ref[...], nv_ref[...] = g, d, nm, nv

    return pl.pallas_call(body, out_shape=[_sds(w.shape)] * 4, name=name)(w, m, v, parts)


def _pack(arrs, rows):
    flat = jnp.concatenate([a.reshape(-1) for a in arrs])
    return jnp.pad(flat, (0, rows * LANES - flat.shape[0])).reshape(rows, LANES)


def _unpack(packed, shapes):
    flat, out, o = packed.reshape(-1), [], 0
    for s in shapes:
        n = 1
        for d in s:
            n *= d
        out.append(flat[o:o + n].reshape(s))
        o += n
    return out


SMALL_ROWS, REP_ROWS = 200, 16


def kernel(x, norm_w, ffn_w_gate, ffn_w_up, ffn_w_down, mix_w_in, dn_conv_w, attn_sinks, dn_a_log, dn_dt_bias, dn_norm_w, mix_w_out, conv_w_pw1, conv_b_pw1, conv_w_dw, conv_b_dw, conv_ln_w, conv_ln_b, conv_w_pw2, conv_b_pw2, final_norm_w, loss_target, m_norm_w, m_ffn_w_gate, m_ffn_w_up, m_ffn_w_down, m_mix_w_in, m_dn_conv_w, m_attn_sinks, m_dn_a_log, m_dn_dt_bias, m_dn_norm_w, m_mix_w_out, m_conv_w_pw1, m_conv_b_pw1, m_conv_w_dw, m_conv_b_dw, m_conv_ln_w, m_conv_ln_b, m_conv_w_pw2, m_conv_b_pw2, m_final_norm_w, v_norm_w, v_ffn_w_gate, v_ffn_w_up, v_ffn_w_down, v_mix_w_in, v_dn_conv_w, v_attn_sinks, v_dn_a_log, v_dn_dt_bias, v_dn_norm_w, v_mix_w_out, v_conv_w_pw1, v_conv_b_pw1, v_conv_w_dw, v_conv_b_dw, v_conv_ln_w, v_conv_ln_b, v_conv_w_pw2, v_conv_b_pw2, v_final_norm_w):
    small_names = ["norm_w", "dn_conv_w", "conv_b_pw1", "conv_w_dw", "conv_b_dw", "conv_ln_w", "conv_ln_b",
                   "conv_b_pw2"]
    rep_names = ["attn_sinks", "dn_a_log", "dn_dt_bias", "dn_norm_w", "final_norm_w"]
    w = dict(norm_w=norm_w, ffn_w_gate=ffn_w_gate, ffn_w_up=ffn_w_up, ffn_w_down=ffn_w_down, mix_w_in=mix_w_in, dn_conv_w=dn_conv_w, attn_sinks=attn_sinks, dn_a_log=dn_a_log, dn_dt_bias=dn_dt_bias, dn_norm_w=dn_norm_w, mix_w_out=mix_w_out, conv_w_pw1=conv_w_pw1, conv_b_pw1=conv_b_pw1, conv_w_dw=conv_w_dw, conv_b_dw=conv_b_dw, conv_ln_w=conv_ln_w, conv_ln_b=conv_ln_b, conv_w_pw2=conv_w_pw2, conv_b_pw2=conv_b_pw2, final_norm_w=final_norm_w)
    m = dict(norm_w=m_norm_w, ffn_w_gate=m_ffn_w_gate, ffn_w_up=m_ffn_w_up, ffn_w_down=m_ffn_w_down, mix_w_in=m_mix_w_in, dn_conv_w=m_dn_conv_w, attn_sinks=m_attn_sinks, dn_a_log=m_dn_a_log, dn_dt_bias=m_dn_dt_bias, dn_norm_w=m_dn_norm_w, mix_w_out=m_mix_w_out, conv_w_pw1=m_conv_w_pw1, conv_b_pw1=m_conv_b_pw1, conv_w_dw=m_conv_w_dw, conv_b_dw=m_conv_b_dw, conv_ln_w=m_conv_ln_w, conv_ln_b=m_conv_ln_b, conv_w_pw2=m_conv_w_pw2, conv_b_pw2=m_conv_b_pw2, final_norm_w=m_final_norm_w)
    v = dict(norm_w=v_norm_w, ffn_w_gate=v_ffn_w_gate, ffn_w_up=v_ffn_w_up, ffn_w_down=v_ffn_w_down, mix_w_in=v_mix_w_in, dn_conv_w=v_dn_conv_w, attn_sinks=v_attn_sinks, dn_a_log=v_dn_a_log, dn_dt_bias=v_dn_dt_bias, dn_norm_w=v_dn_norm_w, mix_w_out=v_mix_w_out, conv_w_pw1=v_conv_w_pw1, conv_b_pw1=v_conv_b_pw1, conv_w_dw=v_conv_w_dw, conv_b_dw=v_conv_b_dw, conv_ln_w=v_conv_ln_w, conv_ln_b=v_conv_ln_b, conv_w_pw2=v_conv_w_pw2, conv_b_pw2=v_conv_b_pw2, final_norm_w=v_final_norm_w)
    order = ["norm_w", "ffn_w_gate", "ffn_w_up", "ffn_w_down", "mix_w_in", "dn_conv_w", "attn_sinks", "dn_a_log",
             "dn_dt_bias", "dn_norm_w", "mix_w_out", "conv_w_pw1", "conv_b_pw1", "conv_w_dw", "conv_b_dw",
             "conv_ln_w", "conv_ln_b", "conv_w_pw2", "conv_b_pw2", "final_norm_w"]

    small_shapes = [w[n].shape for n in small_names]
    rep_shapes = [w[n].shape for n in rep_names]

    def halves(a):
        return a.reshape(a.shape[:-2] + (2, a.shape[-2] // 2, a.shape[-1]))

    tr = lambda a: jnp.swapaxes(a, -1, -2)
    gate_t, up_t = tr(ffn_w_gate), tr(ffn_w_up)

    def layer_shards(l):
        mix_in, mix_out = (mix_w_in, mix_w_out) if l % 2 == 0 else (conv_w_pw1, conv_w_pw2)
        return [t.astype(BF16) for t in (jnp.concatenate([gate_t[l], up_t[l], ffn_w_down[l]], axis=1),
                                         halves(mix_in[l // 2]), halves(mix_out[l // 2]))]

    first = layer_shards(0) + [_pack([w[n] for n in small_names], SMALL_ROWS)]
    first, (gate_t, up_t, ffn_w_down, mix_w_in, mix_w_out, conv_w_pw1, conv_w_pw2) = lax.optimization_barrier(
        (first, (gate_t, up_t, ffn_w_down, mix_w_in, mix_w_out, conv_w_pw1, conv_w_pw2)))
    gathering = [_gather_async("gather_layer0", first[:3], first[3:])]
    gathering += [_gather_async(f"gather_layer{l}", layer_shards(l)) for l in range(1, DEPTH)]

    def mixer_params(l, w_a, w_b):
        e = l // 2
        w_a = w_a.reshape(NCHIP, D, -1)
        w_b = w_b.reshape(D, D)
        if l % 2 == 0:
            return dict(w_in=w_a, dn_conv_w=sm["dn_conv_w"][e], sinks=_row(attn_sinks[e]), a_log=_row(dn_a_log[e]),
                        dt_bias=_row(dn_dt_bias[e]), dn_norm_w=_row(dn_norm_w[e]), wo_a=w_b[:Q_A], wo_b=w_b[Q_A:])
        return dict(b1a=_row(sm["conv_b_pw1"][e, :D]), b1b=_row(sm["conv_b_pw1"][e, D:]), w1=w_a,
                    w_dw=sm["conv_w_dw"][e], b_dw=_row(sm["conv_b_dw"][e]), ln_w=_row(sm["conv_ln_w"][e]),
                    ln_b=_row(sm["conv_ln_b"][e]), b2=_row(sm["conv_b_pw2"][e]), w2=w_b)

    xs, saved, ffn_w = x[0], [], []
    for l in range(DEPTH):
        got = [r[...] for r in gathering[l]]
        if l == 0:
            per_chip = [_unpack(got[3][q], small_shapes) for q in range(NCHIP)]
            sm = {n: jnp.concatenate([per_chip[q][i] for q in range(NCHIP)], axis=-1)
                  for i, n in enumerate(small_names)}
        else:
            xs, got = lax.optimization_barrier((xs, got))
        ffn_w.append(got[0])
        xs, sv = _layer_fwd(l, xs, sm["norm_w"][l], got[0], mixer_params(l, got[1], got[2]))
        saved.append(sv)
    loss, dx, dfw = _final("final", xs, _row(final_norm_w), loss_target[0])

    hbm = pltpu.MemorySpace.HBM
    row_shapes = dict(ffn=(3 * FS, D), w_in=(D // 2, IN_COLS // NCHIP), w_out=(D // 8, D), pw1=(D // 2, D // 2),
                      pw2=(D // 8, D))
    new_sums = lambda k, n: jax.empty_ref(_sds((2, NCHIP, n) + row_shapes[k], BF16), memory_space=hbm)
    sums_0 = {k: new_sums(k, 1) for k in ("ffn", "w_in", "w_out")}
    sums = dict(ffn=new_sums("ffn", DEPTH - 1), w_in=new_sums("w_in", 1), w_out=new_sums("w_out", 1),
                pw1=new_sums("pw1", 2), pw2=new_sums("pw2", 2))
    c_arr = lax.axis_index("c").astype(jnp.int32).reshape(1)
    dnorm, gmix = [None] * DEPTH, [None] * DEPTH

    def hand_on(l, grads, swapped):
        def run(dx):
            dx, other = lax.optimization_barrier((dx, [r[...] for r in swapped]))
            parts = [_add_half(f"add_half_{l}_{k}", gg, rr, c_arr) for k, (gg, rr) in enumerate(zip(grads, other))]
            dx, parts = lax.optimization_barrier((dx, parts))
            keys = ("ffn", "w_in", "w_out") if l % 2 == 0 else ("ffn", "pw1", "pw2")
            if l == 0:
                _scatter_async("scatter_grads_0", parts, [sums_0[k] for k in keys], [0, 0, 0])
            else:
                _scatter_async(f"scatter_grads_{l}", parts, [sums[k] for k in keys],
                               [l - 1, 0, 0] if l % 2 == 0 else [l - 1, l // 2, l // 2])
            return dx
        return run

    pending = lambda dx: dx
    for l in reversed(range(DEPTH)):
        dx, dnorm[l], dffn, gmix[l] = _layer_bwd(l, dx, sm["norm_w"][l], ffn_w[l], saved[l], pending)
        if l % 2 == 0:
            g_a, g_b = gmix[l]["w_in"], jnp.concatenate([gmix[l]["wo_a"], gmix[l]["wo_b"]], axis=0)
        else:
            g_a, g_b = gmix[l]["w1"], gmix[l]["w2"]
        g_a = halves(g_a).astype(BF16)
        g_b = g_b.reshape(NCHIP, 2, D // 8, D).astype(BF16)
        dx, grads = lax.optimization_barrier((dx, [dffn, g_a, g_b]))
        pending = hand_on(l, grads, _swap_halves(f"swap_grads_{l}", grads, sums["ffn"] if l < DEPTH - 1 else None))
    gm, gc = [gmix[0], gmix[2]], [gmix[1], gmix[3]]
    small_g = dict(
        norm_w=jnp.stack(dnorm), dn_conv_w=jnp.stack([gm[e]["dn_conv_w"] for e in range(2)]),
        conv_b_pw1=jnp.stack([jnp.concatenate([gc[e]["b1a"], gc[e]["b1b"]], axis=1)[0] for e in range(2)]),
        conv_w_dw=jnp.stack([gc[e]["w_dw"] for e in range(2)]),
        conv_b_dw=jnp.stack([gc[e]["b_dw"][0] for e in range(2)]),
        conv_ln_w=jnp.stack([gc[e]["ln_w"][0] for e in range(2)]),
        conv_ln_b=jnp.stack([gc[e]["ln_b"][0] for e in range(2)]),
        conv_b_pw2=jnp.stack([gc[e]["b2"][0] for e in range(2)]))
    small_by_chip = jnp.stack([_pack([jnp.split(small_g[n], NCHIP, axis=-1)[q] for n in small_names], SMALL_ROWS)
                               for q in range(NCHIP)])
    rep_g = _pack([jnp.stack([gm[e]["sinks"][0] for e in range(2)]), jnp.stack([gm[e]["a_log"][0] for e in range(2)]),
                   jnp.stack([gm[e]["dt_bias"][0] for e in range(2)]),
                   jnp.stack([gm[e]["dn_norm_w"][0] for e in range(2)]), dfw[0]], REP_ROWS)
    small_sum, rep_sum = _exchange_small(small_by_chip, rep_g)
    dx, small_sum, rep_sum = lax.optimization_barrier((dx, small_sum, rep_sum))
    dx = pending(dx)

    big = (("ffn_w_gate", "ffn", 0), ("ffn_w_up", "ffn", FS), ("ffn_w_down", "ffn", 2 * FS), ("mix_w_in", "w_in", 0),
           ("mix_w_out", "w_out", 0), ("conv_w_pw1", "pw1", 0), ("conv_w_pw2", "pw2", 0))
    views = {n: (tr, tr) if n in ("ffn_w_gate", "ffn_w_up") else (
        (lambda a: a) if w[n].ndim == 4 else halves, lambda o, n=n: o.reshape(w[n].shape)) for n, _, _ in big}
    partial_sums = {k: r[...] for k, r in sums.items()}
    upper = {}
    for n, key, row0 in big:
        view = views[n][0]
        upper[n] = _adamw_big(f"adamw_{n}", view(w[n]), view(m[n]), view(v[n]), partial_sums[key], row0,
                              first=0 if key in ("pw1", "pw2") else 1)
    upper, partial_sums_0 = lax.optimization_barrier((upper, {k: r[...] for k, r in sums_0.items()}))
    res = {}
    for n, key, row0 in big:
        view, back = views[n]
        outs = upper[n] if key not in partial_sums_0 else _adamw_big(
            f"adamw_{n}_0", view(w[n]), view(m[n]), view(v[n]), partial_sums_0[key], row0, first=0, outs=upper[n])
        res[n] = [back(o) for o in outs]
    outs = _adamw_small("adamw_small", *[_pack([d[n] for n in small_names], SMALL_ROWS) for d in (w, m, v)],
                        small_sum)
    for i, n in enumerate(small_names):
        res[n] = [_unpack(o, small_shapes)[i] for o in outs]
    outs = _adamw_small("adamw_replicated", *[_pack([d[n] for n in rep_names], REP_ROWS) for d in (w, m, v)],
                        rep_sum)
    for i, n in enumerate(rep_names):
        res[n] = [_unpack(o, rep_shapes)[i] for o in outs]

    total = lax.psum(loss[0, 0], ("x", "y", "c"))
    return (total, dx[None], *[res[n][0] for n in order], *[res[n][1] for n in order],
            *[res[n][2] for n in order], *[res[n][3] for n in order])
```

```python
import jax
import jax.numpy as jnp
from jax import lax
from jax.experimental import pallas as pl
from jax.experimental.pallas import tpu as pltpu
from jax.experimental.pallas import tpu_sc as plsc

F32, BF16 = jnp.float32, jnp.bfloat16
MESH = pl.DeviceIdType.MESH
ANY = pl.BlockSpec(memory_space=pl.ANY)

T, D, F = 2048, 1024, 2816
DEPTH = 4
EPS = 1e-6
HEADS, HDIM, KV_HEADS, GROUP = 8, 64, 2, 4
WINDOW = BLOCK = 128
CHUNK = 64
NCHUNK = T // CHUNK
DN_CONV, CONV_WIDTH = 4, 31
Q_A, KV_A, QKV_B, V_B = 512, 128, 1536, 512
IN_COLS = 2832
IN_SPLITS = (0, 512, 640, 768, 2304, 2816, 2832)
NCHIP, NDEV = 4, 8
FS = F // NCHIP
LR, B1, B2, AEPS, WD, STEP = 0.001, 0.9, 0.999, 1e-08, 0.01, 10
V7X_VMEM_BYTES = 64 * 1024 * 1024
VMEM_LIMIT = V7X_VMEM_BYTES * 7 // 8
LANES = 128


def _cp(*sem):
    return pltpu.CompilerParams(dimension_semantics=sem, vmem_limit_bytes=VMEM_LIMIT)


def _sds(shape, dtype=F32):
    return jax.ShapeDtypeStruct(tuple(shape), dtype)


def _full(shape):
    nd = len(shape)
    return pl.BlockSpec(tuple(shape), lambda *_: (0,) * nd)


def _split_bf16(a):
    hi = a.astype(BF16)
    return hi, (a - hi.astype(F32)).astype(BF16)


def _dg(a, b, ca, cb, hi=False):
    if a.ndim == 3 and b.ndim == 3:
        dims = (((ca + 1,), (cb + 1,)), ((0,), (0,)))
    else:
        dims = (((ca,), (cb,)), ((), ()))
    dot = lambda p, q: lax.dot_general(p, q, dims, preferred_element_type=F32)
    if hi:
        a_hi, a_lo = _split_bf16(a.astype(F32))
        b_hi, b_lo = _split_bf16(b.astype(F32))
        return dot(a_hi, b_hi) + (dot(a_hi, b_lo) + dot(a_lo, b_hi))
    return dot(a.astype(BF16), b.astype(BF16))


def _make_mm(hi):
    @jax.custom_vjp
    def nn(a, b):
        return _dg(a, b, 1, 0, hi)

    @jax.custom_vjp
    def nt(a, b):
        return _dg(a, b, 1, 1, hi)

    @jax.custom_vjp
    def tn(a, b):
        return _dg(a, b, 0, 0, hi)

    nn.defvjp(lambda a, b: (_dg(a, b, 1, 0, hi), (a, b)),
              lambda r, g: (_dg(g, r[1], 1, 1, hi).astype(r[0].dtype), _dg(r[0], g, 0, 0, hi).astype(r[1].dtype)))
    nt.defvjp(lambda a, b: (_dg(a, b, 1, 1, hi), (a, b)),
              lambda r, g: (_dg(g, r[1], 1, 0, hi).astype(r[0].dtype), _dg(g, r[0], 0, 0, hi).astype(r[1].dtype)))
    tn.defvjp(lambda a, b: (_dg(a, b, 0, 0, hi), (a, b)),
              lambda r, g: (_dg(r[1], g, 1, 1, hi).astype(r[0].dtype), _dg(r[0], g, 1, 0, hi).astype(r[1].dtype)))
    return nn, nt, tn


_nn, _nt, _tn = _make_mm(False)
_nn_hi, _nt_hi, _tn_hi = _make_mm(True)


def _rms(x, w):
    return x * lax.rsqrt(jnp.mean(x * x, axis=-1, keepdims=True) + EPS) * w


def _layernorm(x, w, b):
    xc = x - jnp.mean(x, axis=-1, keepdims=True)
    return xc * lax.rsqrt(jnp.mean(xc * xc, axis=-1, keepdims=True) + EPS) * w + b


def _silu(x):
    return x * jax.nn.sigmoid(x)


def _iota2(shape, dim):
    return lax.broadcasted_iota(jnp.int32, shape, dim)


def _flat_weights(lhs_idx, weights):
    specs, ops, lhs_of, where = [], [], [], []
    for a, (k, w) in enumerate(zip(lhs_idx, weights)):
        for q in range(1 if w.ndim == 2 else w.shape[0]):
            specs.append(_full(w.shape) if w.ndim == 2
                         else pl.BlockSpec((None,) + w.shape[1:], lambda i, q=q: (q, 0, 0)))
            ops.append(w)
            lhs_of.append(k)
            where.append((a, None if w.ndim == 2 else q))
    return specs, ops, lhs_of, where


def _blk_fwd(name, pre, lhs_idx, post, toks, smalls, weights, outs, tm=512):
    wspecs, wops, lhs_of, _ = _flat_weights(lhs_idx, weights)
    nt_, ns, nw = len(toks), len(smalls), len(wops)

    def body(*refs):
        tv = [r[...] for r in refs[:nt_]]
        sv = [r[...] for r in refs[nt_:nt_ + ns]]
        wr = refs[nt_ + ns:nt_ + ns + nw]
        orf = refs[nt_ + ns + nw:]
        lhs = pre(tv, sv)
        ys = [_dg(lhs[i], w[...], 1, 0) for i, w in zip(lhs_of, wr)]
        for o_ref, o in zip(orf, post(ys, tv, sv)):
            o_ref[...] = o.astype(o_ref.dtype)

    in_specs = ([pl.BlockSpec((tm, a.shape[1]), lambda i: (i, 0)) for a in toks]
                + [_full(a.shape) for a in smalls] + wspecs)
    out_specs = [pl.BlockSpec((tm, w_), lambda i: (i, 0)) for w_, _ in outs]
    return pl.pallas_call(
        body, grid=(T // tm,), in_specs=in_specs, out_specs=out_specs,
        out_shape=[_sds((T, w_), dt) for w_, dt in outs], name=name, compiler_params=_cp("parallel"),
    )(*toks, *smalls, *wops)


def _blk_bwd(name, pre, lhs_idx, post, toks, smalls, weights, ct_groups, res=None, linear_post=False, tm=256,
             wchunk=512):
    wspecs, wops, lhs_of, where = _flat_weights(lhs_idx, weights)
    nt_, ns, nw, na = len(toks), len(smalls), len(wops), len(weights)
    cts = [a for g in ct_groups for a in g]
    nc = len(cts)
    widths = [sum(a.shape[1] for a in g) for g in ct_groups]
    has_res = res is not None

    def body(*refs):
        p = 0
        tr = refs[p:p + nt_]; p += nt_
        sr = refs[p:p + ns]; p += ns
        wr = refs[p:p + nw]; p += nw
        cr = refs[p:p + nc]; p += nc
        rr = refs[p:p + has_res]; p += has_res
        dtr = refs[p:p + nt_]; p += nt_
        dsr = refs[p:p + ns]; p += ns
        dwr = refs[p:p + na]; p += na
        scr = refs[p:]
        i = pl.program_id(0)

        @pl.when(i == 0)
        def _():
            for r in list(dsr) + list(dwr):
                r[...] = jnp.zeros_like(r)

        tv = [r[...] for r in tr]
        sv = [r[...] for r in sr]
        ctv, q, si = [], 0, 0
        for g in ct_groups:
            if len(g) == 1:
                ctv.append(cr[q][...].astype(F32))
            else:
                off = 0
                for j, a in enumerate(g):
                    scr[si][:, off:off + a.shape[1]] = cr[q + j][...].astype(F32)
                    off += a.shape[1]
                ctv.append(scr[si][...])
                si += 1
            q += len(g)

        lhs, vjp_pre = jax.vjp(lambda *a: tuple(pre(list(a[:nt_]), list(a[nt_:]))), *tv, *sv)
        lhs_b = [l.astype(BF16) for l in lhs]
        ys = [jnp.zeros((tm, w.shape[1]), F32) if linear_post else _dg(lhs_b[k], w[...], 1, 0)
              for k, w in zip(lhs_of, wr)]
        _, vjp_post = jax.vjp(lambda *a: tuple(post(list(a[:nw]), list(a[nw:nw + nt_]), list(a[nw + nt_:]))),
                              *ys, *tv, *sv)
        gp = vjp_post(tuple(ctv))
        dys, dt_post, ds_post = gp[:nw], gp[nw:nw + nt_], gp[nw + nt_:]
        dlhs = [None] * len(lhs)
        for k, w, dy, (a, q) in zip(lhs_of, wr, dys, where):
            dyb = dy.astype(BF16)
            n = w.shape[1]
            for c0 in range(0, n, wchunk):
                c1 = min(n, c0 + wchunk)
                part = _dg(lhs_b[k], dyb[:, c0:c1], 0, 0)
                if q is None:
                    dwr[a][:, c0:c1] += part
                else:
                    dwr[a][q, :, c0:c1] += part
            d = _dg(dyb, w[...], 1, 1)
            dlhs[k] = d if dlhs[k] is None else dlhs[k] + d
        gq = vjp_pre(tuple(d.astype(l.dtype) for d, l in zip(dlhs, lhs)))
        dt_pre, ds_pre = gq[:nt_], gq[nt_:]
        for j in range(nt_):
            d = dt_post[j] + dt_pre[j]
            if j == 0 and has_res:
                d = d + rr[0][...]
            dtr[j][...] = d
        for j in range(ns):
            dsr[j][...] += ds_post[j] + ds_pre[j]

    tok_spec = lambda a: pl.BlockSpec((tm, a.shape[1]), lambda i: (i, 0))
    in_specs = ([tok_spec(a) for a in toks] + [_full(a.shape) for a in smalls] + wspecs
                + [tok_spec(a) for a in cts] + ([tok_spec(res)] if has_res else []))
    out_specs = [tok_spec(a) for a in toks] + [_full(a.shape) for a in smalls] + [_full(w.shape) for w in weights]
    out_shape = ([_sds(a.shape) for a in toks] + [_sds(a.shape) for a in smalls] + [_sds(w.shape) for w in weights])
    scratch = [pltpu.VMEM((tm, wd), F32) for g, wd in zip(ct_groups, widths) if len(g) > 1]
    outs = pl.pallas_call(
        body, grid=(T // tm,), in_specs=in_specs, out_specs=out_specs, out_shape=out_shape,
        scratch_shapes=scratch, name=name, compiler_params=_cp("arbitrary"),
    )(*toks, *smalls, *wops, *cts, *([res] if has_res else []))
    return outs[:nt_], outs[nt_:nt_ + ns], outs[nt_ + ns:]


def _ffn_fwd(name, x, nw, ffn, idx, tm=512):
    def body(x_ref, nw_ref, wg_ref, wu_ref, wd_ref, o_ref, a_ref, b_ref, h_ref):
        s = pl.program_id(1)

        @pl.when(s == 0)
        def _():
            xv = x_ref[...]
            h_ref[...] = _rms(xv, nw_ref[...]).astype(BF16)
            o_ref[...] = xv

        h = h_ref[...]
        a = _dg(h, wg_ref[...], 1, 1).astype(BF16)
        b = _dg(h, wu_ref[...], 1, 1).astype(BF16)
        a_ref[...] = a
        b_ref[...] = b
        o_ref[...] += 0.5 * _dg(_swiglu_act(a, b)[0], wd_ref[...], 1, 0)

    wspec = lambda k: pl.BlockSpec((None, None, FS, D), lambda i, s: (s, idx, k, 0))
    act = pl.BlockSpec((None, tm, FS), lambda i, s: (s, i, 0))
    return pl.pallas_call(
        body, grid=(T // tm, NCHIP),
        in_specs=[pl.BlockSpec((tm, D), lambda i, s: (i, 0)), _full((1, D)), wspec(0), wspec(1), wspec(2)],
        out_specs=[pl.BlockSpec((tm, D), lambda i, s: (i, 0)), act, act, pl.BlockSpec((tm, D), lambda i, s: (i, 0))],
        out_shape=[_sds((T, D)), _sds((NCHIP, T, FS), BF16), _sds((NCHIP, T, FS), BF16), _sds((T, D), BF16)],
        name=name, compiler_params=_cp("parallel", "arbitrary"),
    )(x, nw, ffn, ffn, ffn)


def _swiglu_act(a, b):
    a, b = a.astype(F32), b.astype(F32)
    sa = jax.nn.sigmoid(a)
    act = a * sa
    return act * b, a, b, sa, act


def _ffn_bwd(name, x, nw, ffn, idx, pre, dy, gbuf=None, tm=512):
    ni = T // tm

    def body(x_ref, dy_ref, nw_ref, wg_ref, wu_ref, wd_ref, a_ref, b_ref, h_ref, dx_ref, dnw_ref, dffn_ref, dh_acc,
             ag, au, ad):
        s, i = pl.program_id(0), pl.program_id(1)
        rows = pl.ds(pl.multiple_of(i * tm, tm), tm)

        @pl.when((s == 0) & (i == 0))
        def _():
            dnw_ref[...] = jnp.zeros_like(dnw_ref)

        @pl.when(i == 0)
        def _():
            ag[...] = jnp.zeros_like(ag)
            au[...] = jnp.zeros_like(au)
            ad[...] = jnp.zeros_like(ad)

        hb = h_ref[...]
        gated, a, b, sa, act = _swiglu_act(a_ref[...], b_ref[...])
        dyb = (0.5 * dy_ref[...]).astype(BF16)
        ad[...] += _dg(gated, dyb, 0, 0)
        dact = _dg(dyb, wd_ref[...], 1, 1)
        da = (dact * b * (sa * (1.0 + a * (1.0 - sa)))).astype(BF16)
        db = (dact * act).astype(BF16)
        ag[...] += _dg(da, hb, 0, 0)
        au[...] += _dg(db, hb, 0, 0)
        dh = _dg(da, wg_ref[...], 1, 0) + _dg(db, wu_ref[...], 1, 0)

        @pl.when(s == 0)
        def _():
            dh_acc[rows, :] = dh

        @pl.when((s > 0) & (s < NCHIP - 1))
        def _():
            dh_acc[rows, :] += dh

        @pl.when(s == NCHIP - 1)
        def _():
            _, vjp_rms = jax.vjp(_rms, x_ref[...], nw_ref[...])
            dx, dnw = vjp_rms(dh_acc[rows, :] + dh)
            dx_ref[...] = dy_ref[...] + dx
            dnw_ref[...] += dnw

        @pl.when(i == ni - 1)
        def _():
            dffn_ref[0:FS, :] = ag[...].astype(BF16)
            dffn_ref[FS:2 * FS, :] = au[...].astype(BF16)
            dffn_ref[2 * FS:, :] = ad[...].astype(BF16)

    wspec = lambda r, k, blk=0: pl.BlockSpec((None, None, r, D), lambda s, i: (s, blk, k, 0),
                                             pipeline_mode=pl.Buffered(1))
    last = lambda s, i: (jnp.where(s == NCHIP - 1, i, 0), 0)
    nb = 0 if gbuf is None else 1
    act = pl.BlockSpec((None, tm, FS), lambda s, i: (s, i, 0))
    tok = pl.BlockSpec((tm, D), lambda s, i: (i, 0))
    return pl.pallas_call(
        lambda *refs: body(*refs[:9], *refs[9 + nb:]), grid=(NCHIP, ni),
        in_specs=[pl.BlockSpec((tm, D), last), tok, _full((1, D)), wspec(FS, 0), wspec(FS, 1), wspec(FS, 2), act, act,
                  tok] + [ANY] * nb,
        out_specs=[pl.BlockSpec((tm, D), last), _full((1, D)), wspec(3 * FS, 0, idx)],
        out_shape=[_sds((T, D)), _sds((1, D)), _sds((NCHIP, 2, 3 * FS, D), BF16)],
        input_output_aliases={9 + k: 2 + k for k in range(nb)},
        scratch_shapes=[pltpu.VMEM((T, D), F32)] + [pltpu.VMEM((FS, D), F32)] * 3,
        name=name, compiler_params=_cp("arbitrary", "arbitrary"),
    )(x, dy, nw, ffn, ffn, ffn, *pre, *(() if gbuf is None else (gbuf,)))


CONV_ROWS = 256


def _conv_pad(k):
    return 8 * ((k - 1 + 7) // 8)


def _shifted(win, o):
    n = win.shape[0]
    return (win if o % n == 0 else pltpu.roll(win, (n - o) % n, 0))[0:CONV_ROWS, :]


def _conv_fwd(name, x, w, b, act):
    k_w, c = w.shape
    tc = 256 if c % 256 == 0 else LANES
    pad = _conv_pad(k_w)
    has_b = b is not None

    def body(*refs):
        x_ref, w_ref = refs[0], refs[1]
        b_ref = refs[2] if has_b else None
        y_ref, xp = refs[2 + has_b], refs[3 + has_b]
        xp[0:pad, :] = jnp.zeros((pad, tc), F32)
        xp[pad:, :] = x_ref[...]

        def step(t, carry):
            base = pl.multiple_of(t * CONV_ROWS, CONV_ROWS)
            win = xp[pl.ds(base, CONV_ROWS + pad), :]
            acc = jnp.zeros((CONV_ROWS, tc), F32)
            for k in range(k_w):
                o = pad - (k_w - 1) + k
                acc = acc + w_ref[k:k + 1, :] * _shifted(win, o)
            if has_b:
                acc = acc + b_ref[...]
            y_ref[pl.ds(base, CONV_ROWS), :] = _silu(acc) if act else acc
            return carry

        lax.fori_loop(0, T // CONV_ROWS, step, 0)

    col = lambda r: pl.BlockSpec((r, tc), lambda j: (0, j))
    ins = [x, w] + ([b] if has_b else [])
    return pl.pallas_call(
        body, grid=(c // tc,), in_specs=[col(T), col(k_w)] + ([col(1)] if has_b else []), out_specs=col(T),
        out_shape=_sds((T, c)), scratch_shapes=[pltpu.VMEM((T + pad, tc), F32)], name=name,
        compiler_params=_cp("parallel"),
    )(*ins)


def _conv_bwd(name, x, w, b, act, dy):
    k_w, c = w.shape
    tc = 256 if c % 256 == 0 else LANES
    pad = _conv_pad(k_w)
    has_b = b is not None

    def body(*refs):
        x_ref, w_ref, dy_ref = refs[0], refs[1], refs[2]
        b_ref = refs[3] if has_b else None
        dx_ref, dw_ref, db_ref, xp, dp = refs[3 + has_b:]
        xp[0:pad, :] = jnp.zeros((pad, tc), F32)
        xp[pad:, :] = x_ref[...]
        dp[T:, :] = jnp.zeros((pad, tc), F32)
        dw_ref[...] = jnp.zeros_like(dw_ref)
        db_ref[...] = jnp.zeros_like(db_ref)

        def step1(t, carry):
            base = pl.multiple_of(t * CONV_ROWS, CONV_ROWS)
            d = dy_ref[pl.ds(base, CONV_ROWS), :]
            win = xp[pl.ds(base, CONV_ROWS + pad), :]
            offs = [pad - (k_w - 1) + k for k in range(k_w)]
            if act:
                acc = jnp.zeros((CONV_ROWS, tc), F32)
                for k, o in enumerate(offs):
                    acc = acc + w_ref[k:k + 1, :] * _shifted(win, o)
                if has_b:
                    acc = acc + b_ref[...]
                sg = jax.nn.sigmoid(acc)
                d = d * (sg * (1.0 + acc * (1.0 - sg)))
            dp[pl.ds(base, CONV_ROWS), :] = d
            for k, o in enumerate(offs):
                dw_ref[k:k + 1, :] += jnp.sum(d * _shifted(win, o), axis=0, keepdims=True)
            db_ref[...] += jnp.sum(d, axis=0, keepdims=True)
            return carry

        lax.fori_loop(0, T // CONV_ROWS, step1, 0)

        def step2(t, carry):
            base = pl.multiple_of(t * CONV_ROWS, CONV_ROWS)
            win = dp[pl.ds(base, CONV_ROWS + pad), :]
            acc = jnp.zeros((CONV_ROWS, tc), F32)
            for k in range(k_w):
                o = (k_w - 1) - k
                acc = acc + w_ref[k:k + 1, :] * _shifted(win, o)
            dx_ref[pl.ds(base, CONV_ROWS), :] = acc
            return carry

        lax.fori_loop(0, T // CONV_ROWS, step2, 0)

    col = lambda r: pl.BlockSpec((r, tc), lambda j: (0, j))
    ins = [x, w, dy] + ([b] if has_b else [])
    return pl.pallas_call(
        body, grid=(c // tc,), in_specs=[col(T), col(k_w), col(T)] + ([col(1)] if has_b else []),
        out_specs=[col(T), col(k_w), col(1)], out_shape=[_sds((T, c)), _sds((k_w, c)), _sds((1, c))],
        scratch_shapes=[pltpu.VMEM((T + pad, tc), F32), pltpu.VMEM((T + pad, tc), F32)], name=name,
        compiler_params=_cp("parallel"),
    )(*ins)


def _attn_consts(n):
    i = _iota2((BLOCK, 2 * BLOCK), 0)
    j = _iota2((BLOCK, 2 * BLOCK), 1)
    dist = i + BLOCK - j
    valid = (dist >= 0) & (dist < WINDOW) & ((n > 0) | (j >= BLOCK))
    return dist.astype(F32), valid


def _attn_block(q4, kk, vv, sinks, dist, valid, kv):
    outs = []
    lane = _iota2((1, HEADS), 1)
    for g in range(GROUP):
        h = kv * GROUP + g
        slope = 2.0 ** (-8.0 * (h + 1) / HEADS)
        s = _nt(q4[:, g * HDIM:(g + 1) * HDIM], kk) * (HDIM ** -0.5)
        s = jnp.where(valid, s - slope * dist, -1e30)
        sink = jnp.sum(jnp.where(lane == h, sinks, 0.0), axis=1, keepdims=True)
        m = jnp.maximum(jnp.max(s, axis=-1, keepdims=True), sink)
        e = jnp.exp(s - m)
        p = e / (jnp.sum(e, axis=-1, keepdims=True) + jnp.exp(sink - m))
        outs.append(_nn(p, vv))
    return tuple(outs)


def _attn_fwd(name, qa, ka, va, sinks):
    def body(q_ref, k_ref, v_ref, s_ref, o_ref, kp, vp):
        kp[0:BLOCK, :] = jnp.zeros((BLOCK, KV_A), F32)
        vp[0:BLOCK, :] = jnp.zeros((BLOCK, KV_A), F32)
        kp[BLOCK:, :] = k_ref[...]
        vp[BLOCK:, :] = v_ref[...]
        sinks_v = s_ref[...]

        def step(n, carry):
            r = pl.multiple_of(n * BLOCK, BLOCK)
            dist, valid = _attn_consts(n)
            k2 = kp[pl.ds(r, 2 * BLOCK), :]
            v2 = vp[pl.ds(r, 2 * BLOCK), :]
            for kv in range(KV_HEADS):
                q4 = q_ref[pl.ds(r, BLOCK), kv * GROUP * HDIM:(kv + 1) * GROUP * HDIM]
                og = _attn_block(q4, k2[:, kv * HDIM:(kv + 1) * HDIM], v2[:, kv * HDIM:(kv + 1) * HDIM], sinks_v,
                                 dist, valid, kv)
                for g in range(GROUP):
                    h = kv * GROUP + g
                    o_ref[pl.ds(r, BLOCK), h * HDIM:(h + 1) * HDIM] = og[g]
            return carry

        lax.fori_loop(0, T // BLOCK, step, 0)

    return pl.pallas_call(
        body, out_shape=_sds((T, Q_A)),
        scratch_shapes=[pltpu.VMEM((T + BLOCK, KV_A), F32), pltpu.VMEM((T + BLOCK, KV_A), F32)], name=name,
        compiler_params=pltpu.CompilerParams(vmem_limit_bytes=VMEM_LIMIT),
    )(qa, ka, va, sinks)


def _attn_bwd(name, qa, ka, va, sinks, do):
    def body(q_ref, k_ref, v_ref, s_ref, do_ref, dq_ref, dk_ref, dv_ref, ds_ref, kp, vp, dkp, dvp):
        kp[0:BLOCK, :] = jnp.zeros((BLOCK, KV_A), F32)
        vp[0:BLOCK, :] = jnp.zeros((BLOCK, KV_A), F32)
        kp[BLOCK:, :] = k_ref[...]
        vp[BLOCK:, :] = v_ref[...]
        dkp[...] = jnp.zeros_like(dkp)
        dvp[...] = jnp.zeros_like(dvp)
        ds_ref[...] = jnp.zeros_like(ds_ref)
        sinks_v = s_ref[...]

        def step(n, carry):
            r = pl.multiple_of(n * BLOCK, BLOCK)
            dist, valid = _attn_consts(n)
            k2 = kp[pl.ds(r, 2 * BLOCK), :]
            v2 = vp[pl.ds(r, 2 * BLOCK), :]
            for kv in range(KV_HEADS):
                cols = slice(kv * HDIM, (kv + 1) * HDIM)
                q4 = q_ref[pl.ds(r, BLOCK), kv * GROUP * HDIM:(kv + 1) * GROUP * HDIM]
                _, vjp = jax.vjp(lambda q, k, v, s: _attn_block(q, k, v, s, dist, valid, kv),
                                 q4, k2[:, cols], v2[:, cols], sinks_v)
                cts = tuple(do_ref[pl.ds(r, BLOCK), (kv * GROUP + g) * HDIM:(kv * GROUP + g + 1) * HDIM]
                            for g in range(GROUP))
                dq4, dkk, dvv, dsk = vjp(cts)
                dq_ref[pl.ds(r, BLOCK), kv * GROUP * HDIM:(kv + 1) * GROUP * HDIM] = dq4
                dkp[pl.ds(r, 2 * BLOCK), cols] += dkk
                dvp[pl.ds(r, 2 * BLOCK), cols] += dvv
                ds_ref[...] += dsk
            return carry

        lax.fori_loop(0, T // BLOCK, step, 0)
        dk_ref[...] = dkp[BLOCK:, :]
        dv_ref[...] = dvp[BLOCK:, :]

    pad = lambda: pltpu.VMEM((T + BLOCK, KV_A), F32)
    return pl.pallas_call(
        body, out_shape=[_sds((T, Q_A)), _sds((T, KV_A)), _sds((T, KV_A)), _sds((1, HEADS))],
        scratch_shapes=[pad(), pad(), pad(), pad()], name=name,
        compiler_params=pltpu.CompilerParams(vmem_limit_bytes=VMEM_LIMIT),
    )(qa, ka, va, sinks, do)


def _dn_consts():
    i = _iota2((CHUNK, CHUNK), 0)
    j = _iota2((CHUNK, CHUNK), 1)
    return dict(causal=i >= j, strict=i > j, eye=(i == j).astype(F32), ltri=(i >= j).astype(F32),
                ones=jnp.ones((CHUNK, CHUNK), F32), last=(_iota2((CHUNK, 1), 0) == CHUNK - 1).astype(F32))


def _l2norm(x):
    return x * lax.rsqrt(jnp.sum(x * x, axis=-1, keepdims=True) + EPS)


def _head_cols(m):
    lane = _iota2((1, HEADS), 1)
    return jnp.concatenate([jnp.sum(jnp.where(lane == h, m, 0.0), axis=1, keepdims=True)[None]
                            for h in range(HEADS)], axis=0)


@jax.custom_vjp
def _unit_lower_inverse(low, known):
    if known is not None:
        return known
    inv = (_iota2((CHUNK, CHUNK), 0) == _iota2((CHUNK, CHUNK), 1)).astype(F32) - low
    pw = low
    for _ in range(5):
        pw = _dg(pw, pw, 1, 0, True)
        inv = inv + _dg(inv, pw, 1, 0, True)
    return inv


def _unit_lower_inverse_fwd(low, known):
    inv = _unit_lower_inverse(low, known)
    return inv, (inv, known)


def _unit_lower_inverse_bwd(res, g):
    inv, known = res
    d_low = -_dg(inv, _dg(g, inv, 1, 1, True), 0, 0, True)
    return d_low, (None if known is None else jnp.zeros_like(known))


_unit_lower_inverse.defvjp(_unit_lower_inverse_fwd, _unit_lower_inverse_bwd)


def _dn_local(q3, k3, v3, braw, araw, alog, dtb, cs, known_inv=None):
    q = _l2norm(q3) * (HDIM ** -0.5)
    k = _l2norm(k3)
    g = -jnp.exp(alog) * jax.nn.softplus(araw + dtb)
    gc_all = _nn_hi(cs["ltri"], g)
    egc_all = jnp.exp(gc_all)
    beta, gc, egc = _head_cols(jax.nn.sigmoid(braw)), _head_cols(gc_all), _head_cols(egc_all)
    a = jnp.broadcast_to(gc, (HEADS, CHUNK, CHUNK))
    diff = a - jnp.swapaxes(a, 1, 2)
    decay = jnp.where(cs["causal"], jnp.exp(jnp.where(cs["causal"], diff, 0.0)), 0.0)
    kb = k * beta
    low = jnp.where(cs["strict"], _nt(kb, k) * decay, 0.0)
    inv = _unit_lower_inverse(low, known_inv)
    u = _nn_hi(inv, v3 * beta)
    w = _nn_hi(inv, kb * egc)
    attn = _nt(q, k) * decay
    gc_last = jnp.sum(gc * cs["last"], axis=1, keepdims=True)
    return u, w, attn, q * egc, k * jnp.exp(gc_last - gc), egc_all, inv


def _heads3(ref, off=0):
    return jnp.concatenate([ref[:, off + h * HDIM:off + (h + 1) * HDIM][None] for h in range(HEADS)], axis=0)


def _dn_local_fwd(name, qkv, ba, alog, dtb):
    def body(qkv_ref, ba_ref, al_ref, dt_ref, u_ref, w_ref, at_ref, qd_ref, kd_ref, eg_ref, inv_ref):
        bav = ba_ref[...]
        outs = _dn_local(_heads3(qkv_ref), _heads3(qkv_ref, 512), _heads3(qkv_ref, 1024), bav[:, :HEADS],
                         bav[:, HEADS:], al_ref[...], dt_ref[...], _dn_consts())
        for r, o in zip((u_ref, w_ref, at_ref, qd_ref, kd_ref, inv_ref), outs[:5] + outs[6:]):
            _unheads(r, o)
        eg_ref[...] = outs[5]

    row = lambda w_: pl.BlockSpec((CHUNK, w_), lambda n: (n, 0))
    return pl.pallas_call(
        body, grid=(NCHUNK,), in_specs=[row(QKV_B), row(2 * HEADS), _full((1, HEADS)), _full((1, HEADS))],
        out_specs=[row(V_B)] * 5 + [row(HEADS), row(V_B)],
        out_shape=[_sds((T, V_B))] * 5 + [_sds((T, HEADS)), _sds((T, V_B))], name=name,
        compiler_params=_cp("parallel"),
    )(qkv, ba, alog, dtb)


def _dn_local_bwd(name, qkv, ba, alog, dtb, inv, cts):
    def body(qkv_ref, ba_ref, al_ref, dt_ref, inv_ref, du_ref, dw_ref, dat_ref, dqd_ref, dkd_ref, deg_ref,
             dqkv_ref, dba_ref, dal_ref, ddt_ref):
        @pl.when(pl.program_id(0) == 0)
        def _():
            dal_ref[...] = jnp.zeros_like(dal_ref)
            ddt_ref[...] = jnp.zeros_like(ddt_ref)

        cs = _dn_consts()
        bav = ba_ref[...]
        known = _heads3(inv_ref)
        _, vjp = jax.vjp(lambda *a: _dn_local(*a, cs, known)[:6], _heads3(qkv_ref), _heads3(qkv_ref, 512),
                         _heads3(qkv_ref, 1024), bav[:, :HEADS], bav[:, HEADS:], al_ref[...], dt_ref[...])
        dq, dk, dv, dbr, dar, dal, ddt = vjp((_heads3(du_ref), _heads3(dw_ref), _heads3(dat_ref), _heads3(dqd_ref),
                                              _heads3(dkd_ref), deg_ref[...]))
        for h in range(HEADS):
            dqkv_ref[:, h * HDIM:(h + 1) * HDIM] = dq[h]
            dqkv_ref[:, 512 + h * HDIM:512 + (h + 1) * HDIM] = dk[h]
            dqkv_ref[:, 1024 + h * HDIM:1024 + (h + 1) * HDIM] = dv[h]
        dba_ref[:, :HEADS] = dbr
        dba_ref[:, HEADS:] = dar
        dal_ref[...] += dal
        ddt_ref[...] += ddt

    row = lambda w_: pl.BlockSpec((CHUNK, w_), lambda n: (n, 0))
    return pl.pallas_call(
        body, grid=(NCHUNK,),
        in_specs=[row(QKV_B), row(2 * HEADS), _full((1, HEADS)), _full((1, HEADS))] + [row(V_B)] * 6 + [row(HEADS)],
        out_specs=[row(QKV_B), row(2 * HEADS), _full((1, HEADS)), _full((1, HEADS))],
        out_shape=[_sds((T, QKV_B)), _sds((T, 2 * HEADS)), _sds((1, HEADS)), _sds((1, HEADS))], name=name,
        compiler_params=_cp("arbitrary"),
    )(qkv, ba, alog, dtb, inv, *cts)


def _dn_step(s, u, w, attn, qd, kd, egc, z, nw):
    last = (_iota2((CHUNK, 1), 0) == CHUNK - 1).astype(F32)
    gl = jnp.sum(_head_cols(egc) * last, axis=1, keepdims=True)
    v_new = u - _nn(w, s)
    o = _nn(qd, s) + _nn(attn, v_new)
    s_new = s * gl + _tn(kd, v_new)
    return s_new, _rms(o, nw) * _silu(z)


def _unheads(ref, v3):
    for h in range(HEADS):
        ref[:, h * HDIM:(h + 1) * HDIM] = v3[h]


def _dn_rec_fwd(name, u, w, attn, qd, kd, egc, z, nw):
    def body(u_ref, w_ref, at_ref, qd_ref, kd_ref, eg_ref, z_ref, nw_ref, o_ref, ss_ref, s_scr):
        @pl.when(pl.program_id(0) == 0)
        def _():
            s_scr[...] = jnp.zeros_like(s_scr)

        s = s_scr[...]
        ss_ref[...] = s
        s_new, on = _dn_step(s, _heads3(u_ref), _heads3(w_ref), _heads3(at_ref), _heads3(qd_ref), _heads3(kd_ref),
                             eg_ref[...], _heads3(z_ref), nw_ref[...])
        s_scr[...] = s_new
        _unheads(o_ref, on)

    row = lambda w_: pl.BlockSpec((CHUNK, w_), lambda n: (n, 0))
    return pl.pallas_call(
        body, grid=(NCHUNK,), in_specs=[row(V_B)] * 5 + [row(HEADS), row(V_B), _full((1, HDIM))],
        out_specs=[row(V_B), pl.BlockSpec((None, HEADS, HDIM, HDIM), lambda n: (n, 0, 0, 0))],
        out_shape=[_sds((T, V_B)), _sds((NCHUNK, HEADS, HDIM, HDIM))],
        scratch_shapes=[pltpu.VMEM((HEADS, HDIM, HDIM), F32)], name=name, compiler_params=_cp("arbitrary"),
    )(u, w, attn, qd, kd, egc, z, nw)


def _dn_rec_bwd(name, u, w, attn, qd, kd, egc, z, nw, ss, do):
    def body(u_ref, w_ref, at_ref, qd_ref, kd_ref, eg_ref, z_ref, nw_ref, ss_ref, do_ref,
             du_ref, dw_ref, dat_ref, dqd_ref, dkd_ref, deg_ref, dz_ref, dnw_ref, ds_scr):
        @pl.when(pl.program_id(0) == 0)
        def _():
            ds_scr[...] = jnp.zeros_like(ds_scr)
            dnw_ref[...] = jnp.zeros_like(dnw_ref)

        _, vjp = jax.vjp(_dn_step, ss_ref[...], _heads3(u_ref), _heads3(w_ref), _heads3(at_ref), _heads3(qd_ref),
                         _heads3(kd_ref), eg_ref[...], _heads3(z_ref), nw_ref[...])
        ds, du, dw, dat, dqd, dkd, deg, dz, dnw = vjp((ds_scr[...], _heads3(do_ref)))
        ds_scr[...] = ds
        for r, v in zip((du_ref, dw_ref, dat_ref, dqd_ref, dkd_ref, dz_ref), (du, dw, dat, dqd, dkd, dz)):
            _unheads(r, v)
        deg_ref[...] = deg
        dnw_ref[...] += dnw

    row = lambda w_: pl.BlockSpec((CHUNK, w_), lambda n: (NCHUNK - 1 - n, 0))
    return pl.pallas_call(
        body, grid=(NCHUNK,),
        in_specs=[row(V_B)] * 5 + [row(HEADS), row(V_B), _full((1, HDIM)),
                                   pl.BlockSpec((None, HEADS, HDIM, HDIM), lambda n: (NCHUNK - 1 - n, 0, 0, 0)),
                                   row(V_B)],
        out_specs=[row(V_B)] * 5 + [row(HEADS), row(V_B), _full((1, HDIM))],
        out_shape=[_sds((T, V_B))] * 5 + [_sds((T, HEADS)), _sds((T, V_B)), _sds((1, HDIM))],
        scratch_shapes=[pltpu.VMEM((HEADS, HDIM, HDIM), F32)], name=name, compiler_params=_cp("arbitrary"),
    )(u, w, attn, qd, kd, egc, z, nw, ss, do)


def _final(name, x, fw, target, tm=512):
    def body(x_ref, fw_ref, t_ref, l_ref, dx_ref, dfw_ref):
        @pl.when(pl.program_id(0) == 0)
        def _():
            l_ref[...] = jnp.zeros_like(l_ref)
            dfw_ref[...] = jnp.zeros_like(dfw_ref)

        tv = t_ref[...]

        def f(xv, fwv):
            err = _rms(xv, fwv) - tv
            per_tok = jnp.mean(err * err, axis=-1, keepdims=True)
            return 0.5 * jnp.sum(per_tok, axis=0, keepdims=True)

        loss, vjp = jax.vjp(f, x_ref[...], fw_ref[...])
        dx, dfw = vjp(jnp.ones((1, 1), F32))
        l_ref[...] += loss
        dx_ref[...] = dx
        dfw_ref[...] += dfw

    tok = pl.BlockSpec((tm, D), lambda i: (i, 0))
    return pl.pallas_call(
        body, grid=(T // tm,), in_specs=[tok, _full((1, D)), tok], out_specs=[_full((1, 1)), tok, _full((1, D))],
        out_shape=[_sds((1, 1)), _sds((T, D)), _sds((1, D))], name=name, compiler_params=_cp("arbitrary"),
    )(x, fw, target)


def _m1_pre(tv, sv):
    return [_rms(tv[0], sv[0])]


def _m1_post(ys, tv, sv):
    return (jnp.concatenate(ys, axis=1),)


def _m1_post_split(ys, tv, sv):
    proj = jnp.concatenate(ys, axis=1)
    return tuple(proj[:, a:b] for a, b in zip(IN_SPLITS[:-1], IN_SPLITS[1:]))


def _m5_pre(tv, sv):
    return [tv[1], tv[2]]


def _m5_post(ys, tv, sv):
    return (tv[0] + ys[0] + ys[1],)


def _c1_pre(tv, sv):
    return [_rms(tv[0], sv[0])]


def _c1_post(ys, tv, sv):
    return ((jnp.concatenate(ys[:2], axis=1) + sv[1]) * jax.nn.sigmoid(jnp.concatenate(ys[2:], axis=1) + sv[2]),)


def _c3_pre(tv, sv):
    return [_silu(_layernorm(tv[0], sv[0], sv[1]))]


def _c3_post(ys, tv, sv):
    return (tv[1] + ys[0] + sv[2],)


def _row(v):
    return v.reshape(1, -1)


def _mixer_fwd(tag, x, p):
    parts = _blk_fwd(f"m1_fwd_{tag}", _m1_pre, [0], _m1_post_split, [x], [p["nw"]], [p["w_in"]],
                     [(b - a, F32) for a, b in zip(IN_SPLITS[:-1], IN_SPLITS[1:])])
    qa, ka, va, qkvb, z, ba = parts
    att = _attn_fwd(f"attn_fwd_{tag}", qa, ka, va, p["sinks"])
    qkvc = _conv_fwd(f"dnconv_fwd_{tag}", qkvb, p["dn_conv_w"], None, True)
    *loc, inv = _dn_local_fwd(f"dnloc_fwd_{tag}", qkvc, ba, p["a_log"], p["dt_bias"])
    og, ss = _dn_rec_fwd(f"dnrec_fwd_{tag}", *loc, z, p["dn_norm_w"])
    (out,) = _blk_fwd(f"m5_fwd_{tag}", _m5_pre, [0, 1], _m5_post, [x, att, og], [], [p["wo_a"], p["wo_b"]],
                      [(D, F32)])
    return out, dict(x=x, qa=qa, ka=ka, va=va, qkvb=qkvb, z=z, ba=ba, att=att, qkvc=qkvc, loc=loc, inv=inv, og=og,
                     ss=ss)


def _mixer_bwd(tag, dy, p, s):
    (dxa, datt, dog), _, (dwo_a, dwo_b) = _blk_bwd(f"m5_bwd_{tag}", _m5_pre, [0, 1], _m5_post,
                                                   [s["x"], s["att"], s["og"]], [], [p["wo_a"], p["wo_b"]], [[dy]],
                                                   linear_post=True)
    rec = _dn_rec_bwd(f"dnrec_bwd_{tag}", *s["loc"], s["z"], p["dn_norm_w"], s["ss"], dog)
    dz, dnw_dn = rec[6], rec[7]
    dqkvc, dba, dalog, ddtb = _dn_local_bwd(f"dnloc_bwd_{tag}", s["qkvc"], s["ba"], p["a_log"], p["dt_bias"],
                                            s["inv"], rec[:6])
    dqkvb, dconvw, _ = _conv_bwd(f"dnconv_bwd_{tag}", s["qkvb"], p["dn_conv_w"], None, True, dqkvc)
    dqa, dka, dva, dsinks = _attn_bwd(f"attn_bwd_{tag}", s["qa"], s["ka"], s["va"], p["sinks"], datt)
    (dx,), (dnw,), (dw_in,) = _blk_bwd(f"m1_bwd_{tag}", _m1_pre, [0], _m1_post, [s["x"]], [p["nw"]], [p["w_in"]],
                                       [[dqa, dka, dva, dqkvb, dz, dba]], res=dxa, linear_post=True)
    return dx, dict(nw=dnw, w_in=dw_in, wo_a=dwo_a, wo_b=dwo_b, dn_conv_w=dconvw, sinks=dsinks, a_log=dalog,
                    dt_bias=ddtb, dn_norm_w=dnw_dn)


def _conformer_fwd(tag, x, p):
    (glu,) = _blk_fwd(f"c1_fwd_{tag}", _c1_pre, [0], _c1_post, [x], [p["nw"], p["b1a"], p["b1b"]], [p["w1"]],
                      [(D, F32)])
    cc = _conv_fwd(f"dwconv_fwd_{tag}", glu, p["w_dw"], p["b_dw"], False)
    (out,) = _blk_fwd(f"c3_fwd_{tag}", _c3_pre, [0], _c3_post, [cc, x], [p["ln_w"], p["ln_b"], p["b2"]], [p["w2"]],
                      [(D, F32)])
    return out, dict(x=x, glu=glu, cc=cc)


def _conformer_bwd(tag, dy, p, s):
    (dcc, dxa), (dlnw, dlnb, db2), (dw2,) = _blk_bwd(f"c3_bwd_{tag}", _c3_pre, [0], _c3_post, [s["cc"], s["x"]],
                                                     [p["ln_w"], p["ln_b"], p["b2"]], [p["w2"]], [[dy]],
                                                     linear_post=True)
    dglu, dwdw, dbdw = _conv_bwd(f"dwconv_bwd_{tag}", s["glu"], p["w_dw"], p["b_dw"], False, dcc)
    (dx,), (dnw, db1a, db1b), (dw1,) = _blk_bwd(f"c1_bwd_{tag}", _c1_pre, [0], _c1_post, [s["x"]],
                                                [p["nw"], p["b1a"], p["b1b"]], [p["w1"]], [[dglu]], res=dxa)
    return dx, dict(nw=dnw, b1a=db1a, b1b=db1b, w1=dw1, w_dw=dwdw, b_dw=dbdw, ln_w=dlnw, ln_b=dlnb, b2=db2, w2=dw2)


def _layer_fwd(l, x, nw, ffn_a, get_ffn_b, p):
    x1, *pre_a = _ffn_fwd(f"ffn_fwd_{l}a", x, _row(nw[0]), ffn_a, 0)
    p = dict(p, nw=_row(nw[1]))
    x2, sv = (_mixer_fwd if l % 2 == 0 else _conformer_fwd)(str(l), x1, p)
    x2, ffn_b = get_ffn_b(x2)
    out, *pre_b = _ffn_fwd(f"ffn_fwd_{l}b", x2, _row(nw[2]), ffn_b, 0)
    return out, (x, x2, p, sv, pre_a, pre_b, ffn_a, ffn_b)


def _layer_bwd(l, dx, nw, saved, after_first=lambda dx: dx):
    x0, x2, p, sv, pre_a, pre_b, ffn_a, ffn_b = saved
    dx, dn2, dffn = _ffn_bwd(f"ffn_bwd_{l}b", x2, _row(nw[2]), ffn_b, 1, pre_b, dx)
    dx = after_first(dx)
    dx, dmix = (_mixer_bwd if l % 2 == 0 else _conformer_bwd)(str(l), dx, p, sv)
    dx, dn0, dffn = _ffn_bwd(f"ffn_bwd_{l}a", x0, _row(nw[0]), ffn_a, 0, pre_a, dx, dffn)
    return dx, jnp.concatenate([dn0, dmix.pop("nw"), dn2], axis=0), dffn, dmix


def _place():
    x, y, c = lax.axis_index("x"), lax.axis_index("y"), lax.axis_index("c")
    chips = [(1 - x, y), (x, 1 - y), (1 - x, 1 - y)]
    return x, y, c, 2 * x + y, chips, [2 * px + py for px, py in chips]


def _handshake(peers):
    barrier = pltpu.get_barrier_semaphore()
    for p in peers:
        pl.semaphore_signal(barrier, inc=1, device_id=p, device_id_type=MESH)
    pl.semaphore_wait(barrier, len(peers))


def _chip_peers():
    x, y, c, _, chips, _ = _place()
    return [(*chip, c) for chip in chips] + [(x, y, 1 - c)]


def _gather_copies(ins, outs, nb, send, recv, fsend, frecv, lsem):
    n_in = len(ins)
    x, y, c, me, chips, cidx = _place()
    sib = (x, y, 1 - c)
    local = [pltpu.make_async_copy(ins[a], outs[a].at[me], lsem.at[a]) for a in range(n_in)]

    def region(a, k, who):
        if k < 2:
            return outs[a].at[cidx[k], pl.ds(who, 1)]
        r = ins[a].shape[1] // 2
        return outs[a].at[cidx[2], pl.ds(who, 1), pl.ds((k - 2) * r, r)]

    def hop(a, k):
        if k < 2:
            src, dst = ins[a].at[pl.ds(c, 1)], outs[a].at[me, pl.ds(c, 1)]
        else:
            r = ins[a].shape[1] // 2
            src = dst = outs[a].at[cidx[3 - k], pl.ds(c, 1), pl.ds((k - 2) * r, r)]
        return pltpu.make_async_remote_copy(src, dst, send.at[4 * a + k], recv.at[4 * a + k],
                                            device_id=(*chips[k % 2], c), device_id_type=MESH)

    def landed(a, k):
        dst = region(a, k, c)
        return pltpu.make_async_remote_copy(dst, dst, send.at[4 * a + k], recv.at[4 * a + k],
                                            device_id=(*chips[k % 2], c), device_id_type=MESH)

    def passed(a, k, who):
        part = region(a, k, who)
        return pltpu.make_async_remote_copy(part, part, fsend.at[4 * a + k], frecv.at[4 * a + k], device_id=sib,
                                            device_id_type=MESH)

    def direct(a, j):
        k = 4 * nb + 3 * (a - nb) + j
        return pltpu.make_async_remote_copy(ins[a], outs[a].at[me], send.at[k], recv.at[k],
                                            device_id=(*chips[j], c), device_id_type=MESH)

    def direct_landed(a, j):
        k = 4 * nb + 3 * (a - nb) + j
        dst = outs[a].at[cidx[j]]
        return pltpu.make_async_remote_copy(dst, dst, send.at[k], recv.at[k], device_id=(*chips[j], c),
                                            device_id_type=MESH)

    sends = [hop(a, k) for a in range(nb) for k in range(2)] + [direct(a, j) for a in range(nb, n_in) for j in range(3)]
    for cp in sends:
        cp.start()
    for cp in local:
        cp.start()
    for a in range(nb):
        for k in (1, 0):
            landed(a, k).wait_recv()
            for cp in (hop(a, 3 - k), passed(a, k, c)):
                cp.start()
                sends.append(cp)
    for a in range(nb):
        for k in (2, 3):
            landed(a, k).wait_recv()
            cp = passed(a, k, c)
            cp.start()
            sends.append(cp)
    for a in range(nb, n_in):
        for j in range(3):
            direct_landed(a, j).wait_recv()
    for a in range(nb):
        for k in range(4):
            passed(a, k, 1 - c).wait_recv()
    for cp in sends:
        cp.wait_send()
    for cp in local:
        cp.wait()


def _gather_sems(n_in, nb):
    dma = pltpu.SemaphoreType.DMA
    n_ici = 4 * nb + 3 * (n_in - nb)
    return [dma((n_ici,)), dma((n_ici,)), dma((4 * nb,)), dma((4 * nb,)), dma((n_in,))]


def _gather_async(name, halved, whole=()):
    nb, arrs = len(halved), list(halved) + list(whole)
    hbm = pltpu.MemorySpace.HBM
    ins = [jax.new_ref(a, memory_space=hbm) for a in arrs]
    outs = [jax.empty_ref(_sds((NCHIP,) + a.shape, a.dtype), memory_space=hbm) for a in arrs]

    @pl.kernel(mesh=plsc.ScalarSubcoreMesh(axis_name="seq", num_cores=1), name=name,
               scratch_types=tuple(_gather_sems(len(arrs), nb)),
               compiler_params=pltpu.CompilerParams(collective_id=2))
    def launch(send, recv, fsend, frecv, lsem):
        _handshake(_chip_peers())
        _gather_copies(ins, outs, nb, send, recv, fsend, frecv, lsem)

    launch()
    return outs


def _swap_halves(name, grads, after=None):
    n = len(grads)
    hbm = pltpu.MemorySpace.HBM
    ins = [jax.new_ref(g, memory_space=hbm) for g in grads]
    outs = [jax.empty_ref(_sds((NCHIP, g.shape[1] // 2) + g.shape[2:], g.dtype), memory_space=hbm) for g in grads]
    tile = (2 * 8, LANES)
    token = None if after is None else jax.empty_ref(_sds(tile, BF16), memory_space=hbm)

    @pl.kernel(mesh=plsc.ScalarSubcoreMesh(axis_name="seq", num_cores=1), name=name,
               scratch_types=(pltpu.SemaphoreType.DMA((n + 1,)), pltpu.SemaphoreType.DMA((n,))),
               compiler_params=pltpu.CompilerParams(collective_id=1))
    def launch(send, recv):
        x, y, c, _, _, _ = _place()
        sib = (x, y, 1 - c)
        _handshake([sib])
        if after is not None:
            tick = pltpu.make_async_copy(after.at[0, 0, 0, pl.ds(0, tile[0]), pl.ds(0, tile[1])], token, send.at[n])
            tick.start()
            tick.wait()
        cps = []
        for a in range(n):
            h = grads[a].shape[1] // 2
            cps.append(pltpu.make_async_remote_copy(ins[a].at[:, pl.ds((1 - c) * h, h)], outs[a], send.at[a],
                                                    recv.at[a], device_id=sib, device_id_type=MESH))
        for cp in cps:
            cp.start()
        for cp in cps:
            cp.wait()

    launch()
    return outs


def _row_tile(r, cap=256):
    return max(t for t in range(8, cap + 1, 8) if r % t == 0)


def _add_half(name, g, r, c_arr):
    _, l, rows, cols = g.shape
    h = l // 2
    tr = _row_tile(rows, 1056)

    def body(c_ref, g_ref, r_ref, o_ref):
        o_ref[...] = (g_ref[...].astype(F32) + r_ref[...].astype(F32)).astype(BF16)

    blk = (None, None, tr, cols)
    return pl.pallas_call(
        body,
        grid_spec=pltpu.PrefetchScalarGridSpec(
            num_scalar_prefetch=1, grid=(NCHIP, h, rows // tr),
            in_specs=[pl.BlockSpec(blk, lambda j, i, t, c_ref: (j, c_ref[0] * h + i, t, 0)),
                      pl.BlockSpec(blk, lambda j, i, t, c_ref: (j, i, t, 0))],
            out_specs=pl.BlockSpec(blk, lambda j, i, t, c_ref: (j, i, t, 0))),
        out_shape=_sds((NCHIP, h, rows, cols), BF16), name=name,
        compiler_params=_cp("parallel", "parallel", "parallel"),
    )(c_arr, g, r)


def _scatter_async(name, parts, sums, where):
    nb = len(parts)
    ins = [jax.new_ref(p, memory_space=pltpu.MemorySpace.HBM) for p in parts]
    dma = pltpu.SemaphoreType.DMA

    @pl.kernel(mesh=plsc.ScalarSubcoreMesh(axis_name="seq", num_cores=1), name=name,
               scratch_types=(dma((3 * nb,)), dma((3 * nb,)), dma((4 * nb,)), dma((4 * nb,)), dma((nb,))),
               compiler_params=pltpu.CompilerParams(collective_id=3))
    def launch(send, recv, fsend, frecv, lsem):
        _handshake(_chip_peers())
        x, y, c, me, chips, cidx = _place()
        sib = (x, y, 1 - c)

        def slot(a, half, chip):
            return sums[a].at[half, chip, pl.ds(where[a], 1)]

        local = [pltpu.make_async_copy(ins[a].at[me], slot(a, c, me), lsem.at[a]) for a in range(nb)]
        for cp in local:
            cp.start()

        def ici(a, j):
            return pltpu.make_async_remote_copy(ins[a].at[cidx[j]], slot(a, c, me), send.at[a * 3 + j],
                                                recv.at[a * 3 + j], device_id=(*chips[j], c), device_id_type=MESH)

        def landed(a, j):
            dst = slot(a, c, cidx[j])
            return pltpu.make_async_remote_copy(dst, dst, send.at[a * 3 + j], recv.at[a * 3 + j],
                                                device_id=(*chips[j], c), device_id_type=MESH)

        def passed(a, j, who):
            dst = slot(a, who, me if j == 3 else cidx[j])
            src = ins[a].at[me] if j == 3 else dst
            return pltpu.make_async_remote_copy(src, dst, fsend.at[a * 4 + j], frecv.at[a * 4 + j], device_id=sib,
                                                device_id_type=MESH)

        sends = [ici(a, j) for a in range(nb) for j in range(3)] + [passed(a, 3, c) for a in range(nb)]
        for cp in sends:
            cp.start()
        for a in range(nb):
            for j in range(3):
                landed(a, j).wait_recv()
                cp = passed(a, j, c)
                cp.start()
                sends.append(cp)
        for a in range(nb):
            for j in range(4):
                passed(a, j, 1 - c).wait_recv()
        for cp in sends:
            cp.wait_send()
        for cp in local:
            cp.wait()

    launch()


def _exchange_small(small, rep):
    def body(small_in, rep_in, small_out, rep_out, lsem, ssend, srecv):
        x, y, c, me, _, _ = _place()
        dev = 4 * x + 2 * y + c
        local = [pltpu.make_async_copy(small_in.at[me], small_out.at[dev], lsem.at[0]),
                 pltpu.make_async_copy(rep_in, rep_out.at[dev], lsem.at[1])]
        for cp in local:
            cp.start()

        def peer(r):
            return (1 - x if r & 4 else x), (1 - y if r & 2 else y), (1 - c if r & 1 else c)

        def tiny(r, which):
            px, py, pc = peer(r)
            k = (r - 1) * 2 + which
            if which == 0:
                return pltpu.make_async_remote_copy(small_in.at[2 * px + py], small_out.at[dev], ssend.at[k],
                                                    srecv.at[k], device_id=(px, py, pc), device_id_type=MESH)
            return pltpu.make_async_remote_copy(rep_in, rep_out.at[dev], ssend.at[k], srecv.at[k],
                                                device_id=(px, py, pc), device_id_type=MESH)

        def tiny_landed(r, which):
            px, py, pc = peer(r)
            k = (r - 1) * 2 + which
            dst = (small_out if which == 0 else rep_out).at[4 * px + 2 * py + pc]
            return pltpu.make_async_remote_copy(dst, dst, ssend.at[k], srecv.at[k], device_id=(px, py, pc),
                                                device_id_type=MESH)

        sends = [tiny(r, w) for r in range(1, NDEV) for w in range(2)]
        for cp in sends:
            cp.start()
        for r in range(1, NDEV):
            for w in range(2):
                tiny_landed(r, w).wait_recv()
        for cp in sends:
            cp.wait_send()
        for cp in local:
            cp.wait()

    dma = pltpu.SemaphoreType.DMA
    return pl.pallas_call(
        body, in_specs=[ANY] * 2, out_specs=[ANY] * 2,
        out_shape=[_sds((NDEV,) + small.shape[1:], F32), _sds((NDEV,) + rep.shape, F32)],
        scratch_shapes=[dma((2,)), dma((2 * (NDEV - 1),)), dma((2 * (NDEV - 1),))], name="exchange_small_grads",
    )(small, rep)


def _adamw_math(w, g, m, v):
    m = B1 * m + (1.0 - B1) * g
    v = B2 * v + (1.0 - B2) * (g * g)
    m_hat = m / (1.0 - B1 ** STEP)
    v_hat = v / (1.0 - B2 ** STEP)
    return -LR * (m_hat / (jnp.sqrt(v_hat) + AEPS) + WD * w), m, v


def _adamw_big(name, w, m, v, parts, row0=0, first=0, outs=None):
    _, _, rows, cols = w.shape
    n = parts.shape[2]
    tr = _row_tile(rows)
    t0 = row0 // tr

    def body(w_ref, m_ref, v_ref, p_ref, *rest):
        g_ref, d_ref, nm_ref, nv_ref = rest[-4:]
        g = p_ref[0].astype(F32)
        for q in range(1, NCHIP):
            g = g + p_ref[q].astype(F32)
        d, nm, nv = _adamw_math(w_ref[...], g, m_ref[...], v_ref[...])
        g_ref[...], d_ref[...], nm_ref[...], nv_ref[...] = g, d, nm, nv

    spec = pl.BlockSpec((None, None, tr, cols), lambda i, p, t: (first + i, p, t, 0))
    na = 0 if outs is None else 4
    return pl.pallas_call(
        body, grid=(n, 2, rows // tr),
        in_specs=[spec, spec, spec,
                  pl.BlockSpec((None, NCHIP, None, tr, cols), lambda i, p, t: (p, 0, i, t0 + t, 0))] + [ANY] * na,
        out_specs=[spec] * 4, out_shape=[_sds(w.shape)] * 4, input_output_aliases={4 + k: k for k in range(na)},
        name=name, compiler_params=_cp("parallel", "parallel", "parallel"),
    )(w, m, v, parts, *(outs or ()))


def _adamw_small(name, w, m, v, parts):
    def body(w_ref, m_ref, v_ref, p_ref, g_ref, d_ref, nm_ref, nv_ref):
        g = p_ref[0]
        for q in range(1, NDEV):
            g = g + p_ref[q]
        d, nm, nv = _adamw_math(w_ref[...], g, m_ref[...], v_ref[...])
        g_ref[...], d_ref[...], nm_ref[...], nv_ref[...] = g, d, nm, nv

    return pl.pallas_call(body, out_shape=[_sds(w.shape)] * 4, name=name)(w, m, v, parts)


def _pack(arrs, rows):
    flat = jnp.concatenate([a.reshape(-1) for a in arrs])
    return jnp.pad(flat, (0, rows * LANES - flat.shape[0])).reshape(rows, LANES)


def _unpack(packed, shapes):
    flat, out, o = packed.reshape(-1), [], 0
    for s in shapes:
        n = 1
        for d in s:
            n *= d
        out.append(flat[o:o + n].reshape(s))
        o += n
    return out


SMALL_ROWS, REP_ROWS = 200, 16


def kernel(x, norm_w, ffn_w_gate, ffn_w_up, ffn_w_down, mix_w_in, dn_conv_w, attn_sinks, dn_a_log, dn_dt_bias, dn_norm_w, mix_w_out, conv_w_pw1, conv_b_pw1, conv_w_dw, conv_b_dw, conv_ln_w, conv_ln_b, conv_w_pw2, conv_b_pw2, final_norm_w, loss_target, m_norm_w, m_ffn_w_gate, m_ffn_w_up, m_ffn_w_down, m_mix_w_in, m_dn_conv_w, m_attn_sinks, m_dn_a_log, m_dn_dt_bias, m_dn_norm_w, m_mix_w_out, m_conv_w_pw1, m_conv_b_pw1, m_conv_w_dw, m_conv_b_dw, m_conv_ln_w, m_conv_ln_b, m_conv_w_pw2, m_conv_b_pw2, m_final_norm_w, v_norm_w, v_ffn_w_gate, v_ffn_w_up, v_ffn_w_down, v_mix_w_in, v_dn_conv_w, v_attn_sinks, v_dn_a_log, v_dn_dt_bias, v_dn_norm_w, v_mix_w_out, v_conv_w_pw1, v_conv_b_pw1, v_conv_w_dw, v_conv_b_dw, v_conv_ln_w, v_conv_ln_b, v_conv_w_pw2, v_conv_b_pw2, v_final_norm_w):
    small_names = ["norm_w", "dn_conv_w", "conv_b_pw1", "conv_w_dw", "conv_b_dw", "conv_ln_w", "conv_ln_b",
                   "conv_b_pw2"]
    rep_names = ["attn_sinks", "dn_a_log", "dn_dt_bias", "dn_norm_w", "final_norm_w"]
    w = dict(norm_w=norm_w, ffn_w_gate=ffn_w_gate, ffn_w_up=ffn_w_up, ffn_w_down=ffn_w_down, mix_w_in=mix_w_in, dn_conv_w=dn_conv_w, attn_sinks=attn_sinks, dn_a_log=dn_a_log, dn_dt_bias=dn_dt_bias, dn_norm_w=dn_norm_w, mix_w_out=mix_w_out, conv_w_pw1=conv_w_pw1, conv_b_pw1=conv_b_pw1, conv_w_dw=conv_w_dw, conv_b_dw=conv_b_dw, conv_ln_w=conv_ln_w, conv_ln_b=conv_ln_b, conv_w_pw2=conv_w_pw2, conv_b_pw2=conv_b_pw2, final_norm_w=final_norm_w)
    m = dict(norm_w=m_norm_w, ffn_w_gate=m_ffn_w_gate, ffn_w_up=m_ffn_w_up, ffn_w_down=m_ffn_w_down, mix_w_in=m_mix_w_in, dn_conv_w=m_dn_conv_w, attn_sinks=m_attn_sinks, dn_a_log=m_dn_a_log, dn_dt_bias=m_dn_dt_bias, dn_norm_w=m_dn_norm_w, mix_w_out=m_mix_w_out, conv_w_pw1=m_conv_w_pw1, conv_b_pw1=m_conv_b_pw1, conv_w_dw=m_conv_w_dw, conv_b_dw=m_conv_b_dw, conv_ln_w=m_conv_ln_w, conv_ln_b=m_conv_ln_b, conv_w_pw2=m_conv_w_pw2, conv_b_pw2=m_conv_b_pw2, final_norm_w=m_final_norm_w)
    v = dict(norm_w=v_norm_w, ffn_w_gate=v_ffn_w_gate, ffn_w_up=v_ffn_w_up, ffn_w_down=v_ffn_w_down, mix_w_in=v_mix_w_in, dn_conv_w=v_dn_conv_w, attn_sinks=v_attn_sinks, dn_a_log=v_dn_a_log, dn_dt_bias=v_dn_dt_bias, dn_norm_w=v_dn_norm_w, mix_w_out=v_mix_w_out, conv_w_pw1=v_conv_w_pw1, conv_b_pw1=v_conv_b_pw1, conv_w_dw=v_conv_w_dw, conv_b_dw=v_conv_b_dw, conv_ln_w=v_conv_ln_w, conv_ln_b=v_conv_ln_b, conv_w_pw2=v_conv_w_pw2, conv_b_pw2=v_conv_b_pw2, final_norm_w=v_final_norm_w)
    order = ["norm_w", "ffn_w_gate", "ffn_w_up", "ffn_w_down", "mix_w_in", "dn_conv_w", "attn_sinks", "dn_a_log",
             "dn_dt_bias", "dn_norm_w", "mix_w_out", "conv_w_pw1", "conv_b_pw1", "conv_w_dw", "conv_b_dw",
             "conv_ln_w", "conv_ln_b", "conv_w_pw2", "conv_b_pw2", "final_norm_w"]

    small_shapes = [w[n].shape for n in small_names]
    rep_shapes = [w[n].shape for n in rep_names]

    def halves(a):
        return a.reshape(a.shape[:-2] + (2, a.shape[-2] // 2, a.shape[-1]))

    tr = lambda a: jnp.swapaxes(a, -1, -2)
    gate_t, up_t = tr(ffn_w_gate), tr(ffn_w_up)

    def layer_shards(l):
        mix_in, mix_out = (mix_w_in, mix_w_out) if l % 2 == 0 else (conv_w_pw1, conv_w_pw2)
        ffn = jnp.concatenate([gate_t[l], up_t[l], ffn_w_down[l]], axis=1)
        return ([t.astype(BF16) for t in (halves(ffn[0]), halves(mix_in[l // 2]), halves(mix_out[l // 2]))],
                [halves(ffn[1]).astype(BF16)])

    first = layer_shards(0)
    first = (first[0] + [_pack([w[n] for n in small_names], SMALL_ROWS)], first[1])
    first, (gate_t, up_t, ffn_w_down, mix_w_in, mix_w_out, conv_w_pw1, conv_w_pw2) = lax.optimization_barrier(
        (first, (gate_t, up_t, ffn_w_down, mix_w_in, mix_w_out, conv_w_pw1, conv_w_pw2)))
    gathering = [(_gather_async("gather_layer0a", first[0][:3], first[0][3:]),
                  _gather_async("gather_layer0b", first[1]))]
    for l in range(1, DEPTH):
        before, after = layer_shards(l)
        gathering.append((_gather_async(f"gather_layer{l}a", before), _gather_async(f"gather_layer{l}b", after)))
    ffn_block = lambda g: g.reshape(NCHIP, 1, 3 * FS, D)

    def mixer_params(l, w_a, w_b):
        e = l // 2
        w_a = w_a.reshape(NCHIP, D, -1)
        w_b = w_b.reshape(D, D)
        if l % 2 == 0:
            return dict(w_in=w_a, dn_conv_w=sm["dn_conv_w"][e], sinks=_row(attn_sinks[e]), a_log=_row(dn_a_log[e]),
                        dt_bias=_row(dn_dt_bias[e]), dn_norm_w=_row(dn_norm_w[e]), wo_a=w_b[:Q_A], wo_b=w_b[Q_A:])
        return dict(b1a=_row(sm["conv_b_pw1"][e, :D]), b1b=_row(sm["conv_b_pw1"][e, D:]), w1=w_a,
                    w_dw=sm["conv_w_dw"][e], b_dw=_row(sm["conv_b_dw"][e]), ln_w=_row(sm["conv_ln_w"][e]),
                    ln_b=_row(sm["conv_ln_b"][e]), b2=_row(sm["conv_b_pw2"][e]), w2=w_b)

    xs, saved = x[0], []
    for l in range(DEPTH):
        got = [r[...] for r in gathering[l][0]]
        if l == 0:
            per_chip = [_unpack(got[3][q], small_shapes) for q in range(NCHIP)]
            sm = {n: jnp.concatenate([per_chip[q][i] for q in range(NCHIP)], axis=-1)
                  for i, n in enumerate(small_names)}
        else:
            xs, got = lax.optimization_barrier((xs, got))

        def second_ffn(x2, l=l):
            x2, got_b = lax.optimization_barrier((x2, gathering[l][1][0][...]))
            return x2, ffn_block(got_b)

        xs, sv = _layer_fwd(l, xs, sm["norm_w"][l], ffn_block(got[0]), second_ffn, mixer_params(l, got[1], got[2]))
        saved.append(sv)
    loss, dx, dfw = _final("final", xs, _row(final_norm_w), loss_target[0])

    hbm = pltpu.MemorySpace.HBM
    row_shapes = dict(ffn=(3 * FS, D), w_in=(D // 2, IN_COLS // NCHIP), w_out=(D // 8, D), pw1=(D // 2, D // 2),
                      pw2=(D // 8, D))
    new_sums = lambda k, n: jax.empty_ref(_sds((2, NCHIP, n) + row_shapes[k], BF16), memory_space=hbm)
    sums_0 = {k: new_sums(k, 1) for k in ("ffn", "w_in", "w_out")}
    sums = dict(ffn=new_sums("ffn", DEPTH - 1), w_in=new_sums("w_in", 1), w_out=new_sums("w_out", 1),
                pw1=new_sums("pw1", 2), pw2=new_sums("pw2", 2))
    c_arr = lax.axis_index("c").astype(jnp.int32).reshape(1)
    dnorm, gmix = [None] * DEPTH, [None] * DEPTH

    def hand_on(l, grads, swapped):
        def run(dx):
            dx, other = lax.optimization_barrier((dx, [r[...] for r in swapped]))
            parts = [_add_half(f"add_half_{l}_{k}", gg, rr, c_arr) for k, (gg, rr) in enumerate(zip(grads, other))]
            dx, parts = lax.optimization_barrier((dx, parts))
            keys = ("ffn", "w_in", "w_out") if l % 2 == 0 else ("ffn", "pw1", "pw2")
            if l == 0:
                _scatter_async("scatter_grads_0", parts, [sums_0[k] for k in keys], [0, 0, 0])
            else:
                _scatter_async(f"scatter_grads_{l}", parts, [sums[k] for k in keys],
                               [l - 1, 0, 0] if l % 2 == 0 else [l - 1, l // 2, l // 2])
            return dx
        return run

    pending = lambda dx: dx
    for l in reversed(range(DEPTH)):
        dx, dnorm[l], dffn, gmix[l] = _layer_bwd(l, dx, sm["norm_w"][l], saved[l], pending)
        if l % 2 == 0:
            g_a, g_b = gmix[l]["w_in"], jnp.concatenate([gmix[l]["wo_a"], gmix[l]["wo_b"]], axis=0)
        else:
            g_a, g_b = gmix[l]["w1"], gmix[l]["w2"]
        g_a = halves(g_a).astype(BF16)
        g_b = g_b.reshape(NCHIP, 2, D // 8, D).astype(BF16)
        dx, grads = lax.optimization_barrier((dx, [dffn, g_a, g_b]))
        pending = hand_on(l, grads, _swap_halves(f"swap_grads_{l}", grads, sums["ffn"] if l < DEPTH - 1 else None))
    gm, gc = [gmix[0], gmix[2]], [gmix[1], gmix[3]]
    small_g = dict(
        norm_w=jnp.stack(dnorm), dn_conv_w=jnp.stack([gm[e]["dn_conv_w"] for e in range(2)]),
        conv_b_pw1=jnp.stack([jnp.concatenate([gc[e]["b1a"], gc[e]["b1b"]], axis=1)[0] for e in range(2)]),
        conv_w_dw=jnp.stack([gc[e]["w_dw"] for e in range(2)]),
        conv_b_dw=jnp.stack([gc[e]["b_dw"][0] for e in range(2)]),
        conv_ln_w=jnp.stack([gc[e]["ln_w"][0] for e in range(2)]),
        conv_ln_b=jnp.stack([gc[e]["ln_b"][0] for e in range(2)]),
        conv_b_pw2=jnp.stack([gc[e]["b2"][0] for e in range(2)]))
    small_by_chip = jnp.stack([_pack([jnp.split(small_g[n], NCHIP, axis=-1)[q] for n in small_names], SMALL_ROWS)
                               for q in range(NCHIP)])
    rep_g = _pack([jnp.stack([gm[e]["sinks"][0] for e in range(2)]), jnp.stack([gm[e]["a_log"][0] for e in range(2)]),
                   jnp.stack([gm[e]["dt_bias"][0] for e in range(2)]),
                   jnp.stack([gm[e]["dn_norm_w"][0] for e in range(2)]), dfw[0]], REP_ROWS)
    small_sum, rep_sum = _exchange_small(small_by_chip, rep_g)
    dx, small_sum, rep_sum = lax.optimization_barrier((dx, small_sum, rep_sum))
    dx = pending(dx)

    big = (("ffn_w_gate", "ffn", 0), ("ffn_w_up", "ffn", FS), ("ffn_w_down", "ffn", 2 * FS), ("mix_w_in", "w_in", 0),
           ("mix_w_out", "w_out", 0), ("conv_w_pw1", "pw1", 0), ("conv_w_pw2", "pw2", 0))
    views = {n: (tr, tr) if n in ("ffn_w_gate", "ffn_w_up") else (
        (lambda a: a) if w[n].ndim == 4 else halves, lambda o, n=n: o.reshape(w[n].shape)) for n, _, _ in big}
    partial_sums = {k: r[...] for k, r in sums.items()}
    upper = {}
    for n, key, row0 in big:
        view = views[n][0]
        upper[n] = _adamw_big(f"adamw_{n}", view(w[n]), view(m[n]), view(v[n]), partial_sums[key], row0,
                              first=0 if key in ("pw1", "pw2") else 1)
    upper, partial_sums_0 = lax.optimization_barrier((upper, {k: r[...] for k, r in sums_0.items()}))
    res = {}
    for n, key, row0 in big:
        view, back = views[n]
        outs = upper[n] if key not in partial_sums_0 else _adamw_big(
            f"adamw_{n}_0", view(w[n]), view(m[n]), view(v[n]), partial_sums_0[key], row0, first=0, outs=upper[n])
        res[n] = [back(o) for o in outs]
    outs = _adamw_small("adamw_small", *[_pack([d[n] for n in small_names], SMALL_ROWS) for d in (w, m, v)],
                        small_sum)
    for i, n in enumerate(small_names):
        res[n] = [_unpack(o, small_shapes)[i] for o in outs]
    outs = _adamw_small("adamw_replicated", *[_pack([d[n] for n in rep_names], REP_ROWS) for d in (w, m, v)],
                        rep_sum)
    for i, n in enumerate(rep_names):
        res[n] = [_unpack(o, rep_shapes)[i] for o in outs]

    total = lax.psum(loss[0, 0], ("x", "y", "c"))
    return (total, dx[None], *[res[n][0] for n in order], *[res[n][1] for n in order],
            *[res[n][2] for n in order], *[res[n][3] for n in order])
```

```python
import jax
import jax.numpy as jnp
from jax import lax
from jax.experimental import pallas as pl
from jax.experimental.pallas import tpu as pltpu
from jax.experimental.pallas import tpu_sc as plsc

F32, BF16 = jnp.float32, jnp.bfloat16
MESH = pl.DeviceIdType.MESH
ANY = pl.BlockSpec(memory_space=pl.ANY)

T, D, F = 2048, 1024, 2816
DEPTH = 4
EPS = 1e-6
HEADS, HDIM, KV_HEADS, GROUP = 8, 64, 2, 4
WINDOW = BLOCK = 128
CHUNK = 64
NCHUNK = T // CHUNK
DN_CONV, CONV_WIDTH = 4, 31
Q_A, KV_A, QKV_B, V_B = 512, 128, 1536, 512
IN_COLS = 2832
IN_SPLITS = (0, 512, 640, 768, 2304, 2816, 2832)
NCHIP, NDEV = 4, 8
FS = F // NCHIP
LR, B1, B2, AEPS, WD, STEP = 0.001, 0.9, 0.999, 1e-08, 0.01, 10
V7X_VMEM_BYTES = 64 * 1024 * 1024
VMEM_LIMIT = V7X_VMEM_BYTES * 7 // 8
LANES = 128


def _cp(*sem):
    return pltpu.CompilerParams(dimension_semantics=sem, vmem_limit_bytes=VMEM_LIMIT)


def _sds(shape, dtype=F32):
    return jax.ShapeDtypeStruct(tuple(shape), dtype)


def _full(shape):
    nd = len(shape)
    return pl.BlockSpec(tuple(shape), lambda *_: (0,) * nd)


def _split_bf16(a):
    hi = a.astype(BF16)
    return hi, (a - hi.astype(F32)).astype(BF16)


def _dg(a, b, ca, cb, hi=False):
    if a.ndim == 3 and b.ndim == 3:
        dims = (((ca + 1,), (cb + 1,)), ((0,), (0,)))
    else:
        dims = (((ca,), (cb,)), ((), ()))
    dot = lambda p, q: lax.dot_general(p, q, dims, preferred_element_type=F32)
    if hi:
        a_hi, a_lo = _split_bf16(a.astype(F32))
        b_hi, b_lo = _split_bf16(b.astype(F32))
        return dot(a_hi, b_hi) + (dot(a_hi, b_lo) + dot(a_lo, b_hi))
    return dot(a.astype(BF16), b.astype(BF16))


def _make_mm(hi):
    @jax.custom_vjp
    def nn(a, b):
        return _dg(a, b, 1, 0, hi)

    @jax.custom_vjp
    def nt(a, b):
        return _dg(a, b, 1, 1, hi)

    @jax.custom_vjp
    def tn(a, b):
        return _dg(a, b, 0, 0, hi)

    nn.defvjp(lambda a, b: (_dg(a, b, 1, 0, hi), (a, b)),
              lambda r, g: (_dg(g, r[1], 1, 1, hi).astype(r[0].dtype), _dg(r[0], g, 0, 0, hi).astype(r[1].dtype)))
    nt.defvjp(lambda a, b: (_dg(a, b, 1, 1, hi), (a, b)),
              lambda r, g: (_dg(g, r[1], 1, 0, hi).astype(r[0].dtype), _dg(g, r[0], 0, 0, hi).astype(r[1].dtype)))
    tn.defvjp(lambda a, b: (_dg(a, b, 0, 0, hi), (a, b)),
              lambda r, g: (_dg(r[1], g, 1, 1, hi).astype(r[0].dtype), _dg(r[0], g, 1, 0, hi).astype(r[1].dtype)))
    return nn, nt, tn


_nn, _nt, _tn = _make_mm(False)
_nn_hi, _nt_hi, _tn_hi = _make_mm(True)


def _rms(x, w):
    return x * lax.rsqrt(jnp.mean(x * x, axis=-1, keepdims=True) + EPS) * w


def _layernorm(x, w, b):
    xc = x - jnp.mean(x, axis=-1, keepdims=True)
    return xc * lax.rsqrt(jnp.mean(xc * xc, axis=-1, keepdims=True) + EPS) * w + b


def _silu(x):
    return x * jax.nn.sigmoid(x)


def _iota2(shape, dim):
    return lax.broadcasted_iota(jnp.int32, shape, dim)


def _flat_weights(lhs_idx, weights):
    specs, ops, lhs_of, where = [], [], [], []
    for a, (k, w) in enumerate(zip(lhs_idx, weights)):
        for q in range(1 if w.ndim == 2 else w.shape[0]):
            specs.append(_full(w.shape) if w.ndim == 2
                         else pl.BlockSpec((None,) + w.shape[1:], lambda i, q=q: (q, 0, 0)))
            ops.append(w)
            lhs_of.append(k)
            where.append((a, None if w.ndim == 2 else q))
    return specs, ops, lhs_of, where


def _blk_fwd(name, pre, lhs_idx, post, toks, smalls, weights, outs, tm=512):
    wspecs, wops, lhs_of, _ = _flat_weights(lhs_idx, weights)
    nt_, ns, nw = len(toks), len(smalls), len(wops)

    def body(*refs):
        tv = [r[...] for r in refs[:nt_]]
        sv = [r[...] for r in refs[nt_:nt_ + ns]]
        wr = refs[nt_ + ns:nt_ + ns + nw]
        orf = refs[nt_ + ns + nw:]
        lhs = pre(tv, sv)
        ys = [_dg(lhs[i], w[...], 1, 0) for i, w in zip(lhs_of, wr)]
        for o_ref, o in zip(orf, post(ys, tv, sv)):
            o_ref[...] = o.astype(o_ref.dtype)

    in_specs = ([pl.BlockSpec((tm, a.shape[1]), lambda i: (i, 0)) for a in toks]
                + [_full(a.shape) for a in smalls] + wspecs)
    out_specs = [pl.BlockSpec((tm, w_), lambda i: (i, 0)) for w_, _ in outs]
    return pl.pallas_call(
        body, grid=(T // tm,), in_specs=in_specs, out_specs=out_specs,
        out_shape=[_sds((T, w_), dt) for w_, dt in outs], name=name, compiler_params=_cp("parallel"),
    )(*toks, *smalls, *wops)


def _blk_bwd(name, pre, lhs_idx, post, toks, smalls, weights, ct_groups, res=None, linear_post=False, tm=256,
             wchunk=512):
    wspecs, wops, lhs_of, where = _flat_weights(lhs_idx, weights)
    nt_, ns, nw, na = len(toks), len(smalls), len(wops), len(weights)
    cts = [a for g in ct_groups for a in g]
    nc = len(cts)
    widths = [sum(a.shape[1] for a in g) for g in ct_groups]
    has_res = res is not None

    def body(*refs):
        p = 0
        tr = refs[p:p + nt_]; p += nt_
        sr = refs[p:p + ns]; p += ns
        wr = refs[p:p + nw]; p += nw
        cr = refs[p:p + nc]; p += nc
        rr = refs[p:p + has_res]; p += has_res
        dtr = refs[p:p + nt_]; p += nt_
        dsr = refs[p:p + ns]; p += ns
        dwr = refs[p:p + na]; p += na
        scr = refs[p:]
        i = pl.program_id(0)

        @pl.when(i == 0)
        def _():
            for r in list(dsr) + list(dwr):
                r[...] = jnp.zeros_like(r)

        tv = [r[...] for r in tr]
        sv = [r[...] for r in sr]
        ctv, q, si = [], 0, 0
        for g in ct_groups:
            if len(g) == 1:
                ctv.append(cr[q][...].astype(F32))
            else:
                off = 0
                for j, a in enumerate(g):
                    scr[si][:, off:off + a.shape[1]] = cr[q + j][...].astype(F32)
                    off += a.shape[1]
                ctv.append(scr[si][...])
                si += 1
            q += len(g)

        lhs, vjp_pre = jax.vjp(lambda *a: tuple(pre(list(a[:nt_]), list(a[nt_:]))), *tv, *sv)
        lhs_b = [l.astype(BF16) for l in lhs]
        ys = [jnp.zeros((tm, w.shape[1]), F32) if linear_post else _dg(lhs_b[k], w[...], 1, 0)
              for k, w in zip(lhs_of, wr)]
        _, vjp_post = jax.vjp(lambda *a: tuple(post(list(a[:nw]), list(a[nw:nw + nt_]), list(a[nw + nt_:]))),
                              *ys, *tv, *sv)
        gp = vjp_post(tuple(ctv))
        dys, dt_post, ds_post = gp[:nw], gp[nw:nw + nt_], gp[nw + nt_:]
        dlhs = [None] * len(lhs)
        for k, w, dy, (a, q) in zip(lhs_of, wr, dys, where):
            dyb = dy.astype(BF16)
            n = w.shape[1]
            for c0 in range(0, n, wchunk):
                c1 = min(n, c0 + wchunk)
                part = _dg(lhs_b[k], dyb[:, c0:c1], 0, 0)
                if q is None:
                    dwr[a][:, c0:c1] += part
                else:
                    dwr[a][q, :, c0:c1] += part
            d = _dg(dyb, w[...], 1, 1)
            dlhs[k] = d if dlhs[k] is None else dlhs[k] + d
        gq = vjp_pre(tuple(d.astype(l.dtype) for d, l in zip(dlhs, lhs)))
        dt_pre, ds_pre = gq[:nt_], gq[nt_:]
        for j in range(nt_):
            d = dt_post[j] + dt_pre[j]
            if j == 0 and has_res:
                d = d + rr[0][...]
            dtr[j][...] = d
        for j in range(ns):
            dsr[j][...] += ds_post[j] + ds_pre[j]

    tok_spec = lambda a: pl.BlockSpec((tm, a.shape[1]), lambda i: (i, 0))
    in_specs = ([tok_spec(a) for a in toks] + [_full(a.shape) for a in smalls] + wspecs
                + [tok_spec(a) for a in cts] + ([tok_spec(res)] if has_res else []))
    out_specs = [tok_spec(a) for a in toks] + [_full(a.shape) for a in smalls] + [_full(w.shape) for w in weights]
    out_shape = ([_sds(a.shape) for a in toks] + [_sds(a.shape) for a in smalls] + [_sds(w.shape) for w in weights])
    scratch = [pltpu.VMEM((tm, wd), F32) for g, wd in zip(ct_groups, widths) if len(g) > 1]
    outs = pl.pallas_call(
        body, grid=(T // tm,), in_specs=in_specs, out_specs=out_specs, out_shape=out_shape,
        scratch_shapes=scratch, name=name, compiler_params=_cp("arbitrary"),
    )(*toks, *smalls, *wops, *cts, *([res] if has_res else []))
    return outs[:nt_], outs[nt_:nt_ + ns], outs[nt_ + ns:]


def _ffn_fwd(name, x, nw, ffn, idx, tm=1024):
    def body(x_ref, nw_ref, wg_ref, wu_ref, wd_ref, o_ref, a_ref, b_ref, h_ref):
        s = pl.program_id(1)

        @pl.when(s == 0)
        def _():
            xv = x_ref[...]
            h_ref[...] = _rms(xv, nw_ref[...]).astype(BF16)
            o_ref[...] = xv

        h = h_ref[...]
        a = _dg(h, wg_ref[...], 1, 1).astype(BF16)
        b = _dg(h, wu_ref[...], 1, 1).astype(BF16)
        a_ref[...] = a
        b_ref[...] = b
        o_ref[...] += 0.5 * _dg(_swiglu_act(a, b)[0], wd_ref[...], 1, 0)

    wspec = lambda k: pl.BlockSpec((None, None, FS, D), lambda i, s: (s, idx, k, 0))
    act = pl.BlockSpec((None, tm, FS), lambda i, s: (s, i, 0))
    return pl.pallas_call(
        body, grid=(T // tm, NCHIP),
        in_specs=[pl.BlockSpec((tm, D), lambda i, s: (i, 0)), _full((1, D)), wspec(0), wspec(1), wspec(2)],
        out_specs=[pl.BlockSpec((tm, D), lambda i, s: (i, 0)), act, act, pl.BlockSpec((tm, D), lambda i, s: (i, 0))],
        out_shape=[_sds((T, D)), _sds((NCHIP, T, FS), BF16), _sds((NCHIP, T, FS), BF16), _sds((T, D), BF16)],
        name=name, compiler_params=_cp("parallel", "arbitrary"),
    )(x, nw, ffn, ffn, ffn)


def _swiglu_act(a, b):
    a, b = a.astype(F32), b.astype(F32)
    sa = jax.nn.sigmoid(a)
    act = a * sa
    return act * b, a, b, sa, act


def _ffn_bwd(name, x, nw, ffn, idx, pre, dy, gbuf=None, tm=512):
    ni = T // tm

    def body(x_ref, dy_ref, nw_ref, wg_ref, wu_ref, wd_ref, a_ref, b_ref, h_ref, dx_ref, dnw_ref, dffn_ref, dh_acc,
             ag, au, ad):
        s, i = pl.program_id(0), pl.program_id(1)
        rows = pl.ds(pl.multiple_of(i * tm, tm), tm)

        @pl.when((s == 0) & (i == 0))
        def _():
            dnw_ref[...] = jnp.zeros_like(dnw_ref)

        @pl.when(i == 0)
        def _():
            ag[...] = jnp.zeros_like(ag)
            au[...] = jnp.zeros_like(au)
            ad[...] = jnp.zeros_like(ad)

        hb = h_ref[...]
        gated, a, b, sa, act = _swiglu_act(a_ref[...], b_ref[...])
        dyb = (0.5 * dy_ref[...]).astype(BF16)
        ad[...] += _dg(gated, dyb, 0, 0)
        dact = _dg(dyb, wd_ref[...], 1, 1)
        da = (dact * b * (sa * (1.0 + a * (1.0 - sa)))).astype(BF16)
        db = (dact * act).astype(BF16)
        ag[...] += _dg(da, hb, 0, 0)
        au[...] += _dg(db, hb, 0, 0)
        dh = _dg(da, wg_ref[...], 1, 0) + _dg(db, wu_ref[...], 1, 0)

        @pl.when(s == 0)
        def _():
            dh_acc[rows, :] = dh

        @pl.when((s > 0) & (s < NCHIP - 1))
        def _():
            dh_acc[rows, :] += dh

        @pl.when(s == NCHIP - 1)
        def _():
            _, vjp_rms = jax.vjp(_rms, x_ref[...], nw_ref[...])
            dx, dnw = vjp_rms(dh_acc[rows, :] + dh)
            dx_ref[...] = dy_ref[...] + dx
            dnw_ref[...] += dnw

        @pl.when(i == ni - 1)
        def _():
            dffn_ref[0:FS, :] = ag[...].astype(BF16)
            dffn_ref[FS:2 * FS, :] = au[...].astype(BF16)
            dffn_ref[2 * FS:, :] = ad[...].astype(BF16)

    wspec = lambda r, k, blk=0: pl.BlockSpec((None, None, r, D), lambda s, i: (s, blk, k, 0),
                                             pipeline_mode=pl.Buffered(1))
    last = lambda s, i: (jnp.where(s == NCHIP - 1, i, 0), 0)
    nb = 0 if gbuf is None else 1
    act = pl.BlockSpec((None, tm, FS), lambda s, i: (s, i, 0))
    tok = pl.BlockSpec((tm, D), lambda s, i: (i, 0))
    return pl.pallas_call(
        lambda *refs: body(*refs[:9], *refs[9 + nb:]), grid=(NCHIP, ni),
        in_specs=[pl.BlockSpec((tm, D), last), tok, _full((1, D)), wspec(FS, 0), wspec(FS, 1), wspec(FS, 2), act, act,
                  tok] + [ANY] * nb,
        out_specs=[pl.BlockSpec((tm, D), last), _full((1, D)), wspec(3 * FS, 0, idx)],
        out_shape=[_sds((T, D)), _sds((1, D)), _sds((NCHIP, 2, 3 * FS, D), BF16)],
        input_output_aliases={9 + k: 2 + k for k in range(nb)},
        scratch_shapes=[pltpu.VMEM((T, D), F32)] + [pltpu.VMEM((FS, D), F32)] * 3,
        name=name, compiler_params=_cp("arbitrary", "arbitrary"),
    )(x, dy, nw, ffn, ffn, ffn, *pre, *(() if gbuf is None else (gbuf,)))


CONV_ROWS = 256


def _conv_pad(k):
    return 8 * ((k - 1 + 7) // 8)


def _shifted(win, o):
    n = win.shape[0]
    return (win if o % n == 0 else pltpu.roll(win, (n - o) % n, 0))[0:CONV_ROWS, :]


def _conv_fwd(name, x, w, b, act):
    k_w, c = w.shape
    tc = 256 if c % 256 == 0 else LANES
    pad = _conv_pad(k_w)
    has_b = b is not None

    def body(*refs):
        x_ref, w_ref = refs[0], refs[1]
        b_ref = refs[2] if has_b else None
        y_ref, xp = refs[2 + has_b], refs[3 + has_b]
        xp[0:pad, :] = jnp.zeros((pad, tc), F32)
        xp[pad:, :] = x_ref[...]

        def step(t, carry):
            base = pl.multiple_of(t * CONV_ROWS, CONV_ROWS)
            win = xp[pl.ds(base, CONV_ROWS + pad), :]
            acc = jnp.zeros((CONV_ROWS, tc), F32)
            for k in range(k_w):
                o = pad - (k_w - 1) + k
                acc = acc + w_ref[k:k + 1, :] * _shifted(win, o)
            if has_b:
                acc = acc + b_ref[...]
            y_ref[pl.ds(base, CONV_ROWS), :] = _silu(acc) if act else acc
            return carry

        lax.fori_loop(0, T // CONV_ROWS, step, 0)

    col = lambda r: pl.BlockSpec((r, tc), lambda j: (0, j))
    ins = [x, w] + ([b] if has_b else [])
    return pl.pallas_call(
        body, grid=(c // tc,), in_specs=[col(T), col(k_w)] + ([col(1)] if has_b else []), out_specs=col(T),
        out_shape=_sds((T, c)), scratch_shapes=[pltpu.VMEM((T + pad, tc), F32)], name=name,
        compiler_params=_cp("parallel"),
    )(*ins)


def _conv_bwd(name, x, w, b, act, dy):
    k_w, c = w.shape
    tc = 256 if c % 256 == 0 else LANES
    pad = _conv_pad(k_w)
    has_b = b is not None

    def body(*refs):
        x_ref, w_ref, dy_ref = refs[0], refs[1], refs[2]
        b_ref = refs[3] if has_b else None
        dx_ref, dw_ref, db_ref, xp, dp = refs[3 + has_b:]
        xp[0:pad, :] = jnp.zeros((pad, tc), F32)
        xp[pad:, :] = x_ref[...]
        dp[T:, :] = jnp.zeros((pad, tc), F32)
        dw_ref[...] = jnp.zeros_like(dw_ref)
        db_ref[...] = jnp.zeros_like(db_ref)

        def step1(t, carry):
            base = pl.multiple_of(t * CONV_ROWS, CONV_ROWS)
            d = dy_ref[pl.ds(base, CONV_ROWS), :]
            win = xp[pl.ds(base, CONV_ROWS + pad), :]
            offs = [pad - (k_w - 1) + k for k in range(k_w)]
            if act:
                acc = jnp.zeros((CONV_ROWS, tc), F32)
                for k, o in enumerate(offs):
                    acc = acc + w_ref[k:k + 1, :] * _shifted(win, o)
                if has_b:
                    acc = acc + b_ref[...]
                sg = jax.nn.sigmoid(acc)
                d = d * (sg * (1.0 + acc * (1.0 - sg)))
            dp[pl.ds(base, CONV_ROWS), :] = d
            for k, o in enumerate(offs):
                dw_ref[k:k + 1, :] += jnp.sum(d * _shifted(win, o), axis=0, keepdims=True)
            db_ref[...] += jnp.sum(d, axis=0, keepdims=True)
            return carry

        lax.fori_loop(0, T // CONV_ROWS, step1, 0)

        def step2(t, carry):
            base = pl.multiple_of(t * CONV_ROWS, CONV_ROWS)
            win = dp[pl.ds(base, CONV_ROWS + pad), :]
            acc = jnp.zeros((CONV_ROWS, tc), F32)
            for k in range(k_w):
                o = (k_w - 1) - k
                acc = acc + w_ref[k:k + 1, :] * _shifted(win, o)
            dx_ref[pl.ds(base, CONV_ROWS), :] = acc
            return carry

        lax.fori_loop(0, T // CONV_ROWS, step2, 0)

    col = lambda r: pl.BlockSpec((r, tc), lambda j: (0, j))
    ins = [x, w, dy] + ([b] if has_b else [])
    return pl.pallas_call(
        body, grid=(c // tc,), in_specs=[col(T), col(k_w), col(T)] + ([col(1)] if has_b else []),
        out_specs=[col(T), col(k_w), col(1)], out_shape=[_sds((T, c)), _sds((k_w, c)), _sds((1, c))],
        scratch_shapes=[pltpu.VMEM((T + pad, tc), F32), pltpu.VMEM((T + pad, tc), F32)], name=name,
        compiler_params=_cp("parallel"),
    )(*ins)


def _attn_consts(n):
    i = _iota2((BLOCK, 2 * BLOCK), 0)
    j = _iota2((BLOCK, 2 * BLOCK), 1)
    dist = i + BLOCK - j
    valid = (dist >= 0) & (dist < WINDOW) & ((n > 0) | (j >= BLOCK))
    return dist.astype(F32), valid


def _attn_block(q4, kk, vv, sinks, dist, valid, kv):
    outs = []
    lane = _iota2((1, HEADS), 1)
    for g in range(GROUP):
        h = kv * GROUP + g
        slope = 2.0 ** (-8.0 * (h + 1) / HEADS)
        s = _nt(q4[:, g * HDIM:(g + 1) * HDIM], kk) * (HDIM ** -0.5)
        s = jnp.where(valid, s - slope * dist, -1e30)
        sink = jnp.sum(jnp.where(lane == h, sinks, 0.0), axis=1, keepdims=True)
        m = jnp.maximum(jnp.max(s, axis=-1, keepdims=True), sink)
        e = jnp.exp(s - m)
        p = e / (jnp.sum(e, axis=-1, keepdims=True) + jnp.exp(sink - m))
        outs.append(_nn(p, vv))
    return tuple(outs)


def _attn_fwd(name, qa, ka, va, sinks):
    def body(q_ref, k_ref, v_ref, s_ref, o_ref, kp, vp):
        kp[0:BLOCK, :] = jnp.zeros((BLOCK, KV_A), F32)
        vp[0:BLOCK, :] = jnp.zeros((BLOCK, KV_A), F32)
        kp[BLOCK:, :] = k_ref[...]
        vp[BLOCK:, :] = v_ref[...]
        sinks_v = s_ref[...]

        def step(n, carry):
            r = pl.multiple_of(n * BLOCK, BLOCK)
            dist, valid = _attn_consts(n)
            k2 = kp[pl.ds(r, 2 * BLOCK), :]
            v2 = vp[pl.ds(r, 2 * BLOCK), :]
            for kv in range(KV_HEADS):
                q4 = q_ref[pl.ds(r, BLOCK), kv * GROUP * HDIM:(kv + 1) * GROUP * HDIM]
                og = _attn_block(q4, k2[:, kv * HDIM:(kv + 1) * HDIM], v2[:, kv * HDIM:(kv + 1) * HDIM], sinks_v,
                                 dist, valid, kv)
                for g in range(GROUP):
                    h = kv * GROUP + g
                    o_ref[pl.ds(r, BLOCK), h * HDIM:(h + 1) * HDIM] = og[g]
            return carry

        lax.fori_loop(0, T // BLOCK, step, 0)

    return pl.pallas_call(
        body, out_shape=_sds((T, Q_A)),
        scratch_shapes=[pltpu.VMEM((T + BLOCK, KV_A), F32), pltpu.VMEM((T + BLOCK, KV_A), F32)], name=name,
        compiler_params=pltpu.CompilerParams(vmem_limit_bytes=VMEM_LIMIT),
    )(qa, ka, va, sinks)


def _attn_bwd(name, qa, ka, va, sinks, do):
    def body(q_ref, k_ref, v_ref, s_ref, do_ref, dq_ref, dk_ref, dv_ref, ds_ref, kp, vp, dkp, dvp):
        kp[0:BLOCK, :] = jnp.zeros((BLOCK, KV_A), F32)
        vp[0:BLOCK, :] = jnp.zeros((BLOCK, KV_A), F32)
        kp[BLOCK:, :] = k_ref[...]
        vp[BLOCK:, :] = v_ref[...]
        dkp[...] = jnp.zeros_like(dkp)
        dvp[...] = jnp.zeros_like(dvp)
        ds_ref[...] = jnp.zeros_like(ds_ref)
        sinks_v = s_ref[...]

        def step(n, carry):
            r = pl.multiple_of(n * BLOCK, BLOCK)
            dist, valid = _attn_consts(n)
            k2 = kp[pl.ds(r, 2 * BLOCK), :]
            v2 = vp[pl.ds(r, 2 * BLOCK), :]
            for kv in range(KV_HEADS):
                cols = slice(kv * HDIM, (kv + 1) * HDIM)
                q4 = q_ref[pl.ds(r, BLOCK), kv * GROUP * HDIM:(kv + 1) * GROUP * HDIM]
                _, vjp = jax.vjp(lambda q, k, v, s: _attn_block(q, k, v, s, dist, valid, kv),
                                 q4, k2[:, cols], v2[:, cols], sinks_v)
                cts = tuple(do_ref[pl.ds(r, BLOCK), (kv * GROUP + g) * HDIM:(kv * GROUP + g + 1) * HDIM]
                            for g in range(GROUP))
                dq4, dkk, dvv, dsk = vjp(cts)
                dq_ref[pl.ds(r, BLOCK), kv * GROUP * HDIM:(kv + 1) * GROUP * HDIM] = dq4
                dkp[pl.ds(r, 2 * BLOCK), cols] += dkk
                dvp[pl.ds(r, 2 * BLOCK), cols] += dvv
                ds_ref[...] += dsk
            return carry

        lax.fori_loop(0, T // BLOCK, step, 0)
        dk_ref[...] = dkp[BLOCK:, :]
        dv_ref[...] = dvp[BLOCK:, :]

    pad = lambda: pltpu.VMEM((T + BLOCK, KV_A), F32)
    return pl.pallas_call(
        body, out_shape=[_sds((T, Q_A)), _sds((T, KV_A)), _sds((T, KV_A)), _sds((1, HEADS))],
        scratch_shapes=[pad(), pad(), pad(), pad()], name=name,
        compiler_params=pltpu.CompilerParams(vmem_limit_bytes=VMEM_LIMIT),
    )(qa, ka, va, sinks, do)


def _dn_consts():
    i = _iota2((CHUNK, CHUNK), 0)
    j = _iota2((CHUNK, CHUNK), 1)
    return dict(causal=i >= j, strict=i > j, eye=(i == j).astype(F32), ltri=(i >= j).astype(F32),
                ones=jnp.ones((CHUNK, CHUNK), F32), last=(_iota2((CHUNK, 1), 0) == CHUNK - 1).astype(F32))


def _l2norm(x):
    return x * lax.rsqrt(jnp.sum(x * x, axis=-1, keepdims=True) + EPS)


def _head_cols(m):
    lane = _iota2((1, HEADS), 1)
    return jnp.concatenate([jnp.sum(jnp.where(lane == h, m, 0.0), axis=1, keepdims=True)[None]
                            for h in range(HEADS)], axis=0)


@jax.custom_vjp
def _unit_lower_inverse(low, known):
    if known is not None:
        return known
    inv = (_iota2((CHUNK, CHUNK), 0) == _iota2((CHUNK, CHUNK), 1)).astype(F32) - low
    pw = low
    for _ in range(5):
        pw = _dg(pw, pw, 1, 0, True)
        inv = inv + _dg(inv, pw, 1, 0, True)
    return inv


def _unit_lower_inverse_fwd(low, known):
    inv = _unit_lower_inverse(low, known)
    return inv, (inv, known)


def _unit_lower_inverse_bwd(res, g):
    inv, known = res
    d_low = -_dg(inv, _dg(g, inv, 1, 1, True), 0, 0, True)
    return d_low, (None if known is None else jnp.zeros_like(known))


_unit_lower_inverse.defvjp(_unit_lower_inverse_fwd, _unit_lower_inverse_bwd)


def _dn_local(q3, k3, v3, braw, araw, alog, dtb, cs, known_inv=None):
    q = _l2norm(q3) * (HDIM ** -0.5)
    k = _l2norm(k3)
    g = -jnp.exp(alog) * jax.nn.softplus(araw + dtb)
    gc_all = _nn_hi(cs["ltri"], g)
    egc_all = jnp.exp(gc_all)
    beta, gc, egc = _head_cols(jax.nn.sigmoid(braw)), _head_cols(gc_all), _head_cols(egc_all)
    a = jnp.broadcast_to(gc, (HEADS, CHUNK, CHUNK))
    diff = a - jnp.swapaxes(a, 1, 2)
    decay = jnp.where(cs["causal"], jnp.exp(jnp.where(cs["causal"], diff, 0.0)), 0.0)
    kb = k * beta
    low = jnp.where(cs["strict"], _nt(kb, k) * decay, 0.0)
    inv = _unit_lower_inverse(low, known_inv)
    u = _nn_hi(inv, v3 * beta)
    w = _nn_hi(inv, kb * egc)
    attn = _nt(q, k) * decay
    gc_last = jnp.sum(gc * cs["last"], axis=1, keepdims=True)
    return u, w, attn, q * egc, k * jnp.exp(gc_last - gc), egc_all, inv


def _heads3(ref, off=0):
    return jnp.concatenate([ref[:, off + h * HDIM:off + (h + 1) * HDIM][None] for h in range(HEADS)], axis=0)


def _dn_local_fwd(name, qkv, ba, alog, dtb):
    def body(qkv_ref, ba_ref, al_ref, dt_ref, u_ref, w_ref, at_ref, qd_ref, kd_ref, eg_ref, inv_ref):
        bav = ba_ref[...]
        outs = _dn_local(_heads3(qkv_ref), _heads3(qkv_ref, 512), _heads3(qkv_ref, 1024), bav[:, :HEADS],
                         bav[:, HEADS:], al_ref[...], dt_ref[...], _dn_consts())
        for r, o in zip((u_ref, w_ref, at_ref, qd_ref, kd_ref, inv_ref), outs[:5] + outs[6:]):
            _unheads(r, o)
        eg_ref[...] = outs[5]

    row = lambda w_: pl.BlockSpec((CHUNK, w_), lambda n: (n, 0))
    return pl.pallas_call(
        body, grid=(NCHUNK,), in_specs=[row(QKV_B), row(2 * HEADS), _full((1, HEADS)), _full((1, HEADS))],
        out_specs=[row(V_B)] * 5 + [row(HEADS), row(V_B)],
        out_shape=[_sds((T, V_B))] * 5 + [_sds((T, HEADS)), _sds((T, V_B))], name=name,
        compiler_params=_cp("parallel"),
    )(qkv, ba, alog, dtb)


def _dn_local_bwd(name, qkv, ba, alog, dtb, inv, cts):
    def body(qkv_ref, ba_ref, al_ref, dt_ref, inv_ref, du_ref, dw_ref, dat_ref, dqd_ref, dkd_ref, deg_ref,
             dqkv_ref, dba_ref, dal_ref, ddt_ref):
        @pl.when(pl.program_id(0) == 0)
        def _():
            dal_ref[...] = jnp.zeros_like(dal_ref)
            ddt_ref[...] = jnp.zeros_like(ddt_ref)

        cs = _dn_consts()
        bav = ba_ref[...]
        known = _heads3(inv_ref)
        _, vjp = jax.vjp(lambda *a: _dn_local(*a, cs, known)[:6], _heads3(qkv_ref), _heads3(qkv_ref, 512),
                         _heads3(qkv_ref, 1024), bav[:, :HEADS], bav[:, HEADS:], al_ref[...], dt_ref[...])
        dq, dk, dv, dbr, dar, dal, ddt = vjp((_heads3(du_ref), _heads3(dw_ref), _heads3(dat_ref), _heads3(dqd_ref),
                                              _heads3(dkd_ref), deg_ref[...]))
        for h in range(HEADS):
            dqkv_ref[:, h * HDIM:(h + 1) * HDIM] = dq[h]
            dqkv_ref[:, 512 + h * HDIM:512 + (h + 1) * HDIM] = dk[h]
            dqkv_ref[:, 1024 + h * HDIM:1024 + (h + 1) * HDIM] = dv[h]
        dba_ref[:, :HEADS] = dbr
        dba_ref[:, HEADS:] = dar
        dal_ref[...] += dal
        ddt_ref[...] += ddt

    row = lambda w_: pl.BlockSpec((CHUNK, w_), lambda n: (n, 0))
    return pl.pallas_call(
        body, grid=(NCHUNK,),
        in_specs=[row(QKV_B), row(2 * HEADS), _full((1, HEADS)), _full((1, HEADS))] + [row(V_B)] * 6 + [row(HEADS)],
        out_specs=[row(QKV_B), row(2 * HEADS), _full((1, HEADS)), _full((1, HEADS))],
        out_shape=[_sds((T, QKV_B)), _sds((T, 2 * HEADS)), _sds((1, HEADS)), _sds((1, HEADS))], name=name,
        compiler_params=_cp("arbitrary"),
    )(qkv, ba, alog, dtb, inv, *cts)


def _dn_step(s, u, w, attn, qd, kd, egc, z, nw):
    last = (_iota2((CHUNK, 1), 0) == CHUNK - 1).astype(F32)
    gl = jnp.sum(_head_cols(egc) * last, axis=1, keepdims=True)
    v_new = u - _nn(w, s)
    o = _nn(qd, s) + _nn(attn, v_new)
    s_new = s * gl + _tn(kd, v_new)
    return s_new, _rms(o, nw) * _silu(z)


def _unheads(ref, v3):
    for h in range(HEADS):
        ref[:, h * HDIM:(h + 1) * HDIM] = v3[h]


def _dn_rec_fwd(name, u, w, attn, qd, kd, egc, z, nw):
    def body(u_ref, w_ref, at_ref, qd_ref, kd_ref, eg_ref, z_ref, nw_ref, o_ref, ss_ref, s_scr):
        @pl.when(pl.program_id(0) == 0)
        def _():
            s_scr[...] = jnp.zeros_like(s_scr)

        s = s_scr[...]
        ss_ref[...] = s
        s_new, on = _dn_step(s, _heads3(u_ref), _heads3(w_ref), _heads3(at_ref), _heads3(qd_ref), _heads3(kd_ref),
                             eg_ref[...], _heads3(z_ref), nw_ref[...])
        s_scr[...] = s_new
        _unheads(o_ref, on)

    row = lambda w_: pl.BlockSpec((CHUNK, w_), lambda n: (n, 0))
    return pl.pallas_call(
        body, grid=(NCHUNK,), in_specs=[row(V_B)] * 5 + [row(HEADS), row(V_B), _full((1, HDIM))],
        out_specs=[row(V_B), pl.BlockSpec((None, HEADS, HDIM, HDIM), lambda n: (n, 0, 0, 0))],
        out_shape=[_sds((T, V_B)), _sds((NCHUNK, HEADS, HDIM, HDIM))],
        scratch_shapes=[pltpu.VMEM((HEADS, HDIM, HDIM), F32)], name=name, compiler_params=_cp("arbitrary"),
    )(u, w, attn, qd, kd, egc, z, nw)


def _dn_rec_bwd(name, u, w, attn, qd, kd, egc, z, nw, ss, do):
    def body(u_ref, w_ref, at_ref, qd_ref, kd_ref, eg_ref, z_ref, nw_ref, ss_ref, do_ref,
             du_ref, dw_ref, dat_ref, dqd_ref, dkd_ref, deg_ref, dz_ref, dnw_ref, ds_scr):
        @pl.when(pl.program_id(0) == 0)
        def _():
            ds_scr[...] = jnp.zeros_like(ds_scr)
            dnw_ref[...] = jnp.zeros_like(dnw_ref)

        _, vjp = jax.vjp(_dn_step, ss_ref[...], _heads3(u_ref), _heads3(w_ref), _heads3(at_ref), _heads3(qd_ref),
                         _heads3(kd_ref), eg_ref[...], _heads3(z_ref), nw_ref[...])
        ds, du, dw, dat, dqd, dkd, deg, dz, dnw = vjp((ds_scr[...], _heads3(do_ref)))
        ds_scr[...] = ds
        for r, v in zip((du_ref, dw_ref, dat_ref, dqd_ref, dkd_ref, dz_ref), (du, dw, dat, dqd, dkd, dz)):
            _unheads(r, v)
        deg_ref[...] = deg
        dnw_ref[...] += dnw

    row = lambda w_: pl.BlockSpec((CHUNK, w_), lambda n: (NCHUNK - 1 - n, 0))
    return pl.pallas_call(
        body, grid=(NCHUNK,),
        in_specs=[row(V_B)] * 5 + [row(HEADS), row(V_B), _full((1, HDIM)),
                                   pl.BlockSpec((None, HEADS, HDIM, HDIM), lambda n: (NCHUNK - 1 - n, 0, 0, 0)),
                                   row(V_B)],
        out_specs=[row(V_B)] * 5 + [row(HEADS), row(V_B), _full((1, HDIM))],
        out_shape=[_sds((T, V_B))] * 5 + [_sds((T, HEADS)), _sds((T, V_B)), _sds((1, HDIM))],
        scratch_shapes=[pltpu.VMEM((HEADS, HDIM, HDIM), F32)], name=name, compiler_params=_cp("arbitrary"),
    )(u, w, attn, qd, kd, egc, z, nw, ss, do)


def _final(name, x, fw, target, tm=512):
    def body(x_ref, fw_ref, t_ref, l_ref, dx_ref, dfw_ref):
        @pl.when(pl.program_id(0) == 0)
        def _():
            l_ref[...] = jnp.zeros_like(l_ref)
            dfw_ref[...] = jnp.zeros_like(dfw_ref)

        tv = t_ref[...]

        def f(xv, fwv):
            err = _rms(xv, fwv) - tv
            per_tok = jnp.mean(err * err, axis=-1, keepdims=True)
            return 0.5 * jnp.sum(per_tok, axis=0, keepdims=True)

        loss, vjp = jax.vjp(f, x_ref[...], fw_ref[...])
        dx, dfw = vjp(jnp.ones((1, 1), F32))
        l_ref[...] += loss
        dx_ref[...] = dx
        dfw_ref[...] += dfw

    tok = pl.BlockSpec((tm, D), lambda i: (i, 0))
    return pl.pallas_call(
        body, grid=(T // tm,), in_specs=[tok, _full((1, D)), tok], out_specs=[_full((1, 1)), tok, _full((1, D))],
        out_shape=[_sds((1, 1)), _sds((T, D)), _sds((1, D))], name=name, compiler_params=_cp("arbitrary"),
    )(x, fw, target)


def _m1_pre(tv, sv):
    return [_rms(tv[0], sv[0])]


def _m1_post(ys, tv, sv):
    return (jnp.concatenate(ys, axis=1),)


def _m1_post_split(ys, tv, sv):
    proj = jnp.concatenate(ys, axis=1)
    return tuple(proj[:, a:b] for a, b in zip(IN_SPLITS[:-1], IN_SPLITS[1:]))


def _m5_pre(tv, sv):
    return [tv[1], tv[2]]


def _m5_post(ys, tv, sv):
    return (tv[0] + ys[0] + ys[1],)


def _c1_pre(tv, sv):
    return [_rms(tv[0], sv[0])]


def _c1_post(ys, tv, sv):
    return ((jnp.concatenate(ys[:2], axis=1) + sv[1]) * jax.nn.sigmoid(jnp.concatenate(ys[2:], axis=1) + sv[2]),)


def _c3_pre(tv, sv):
    return [_silu(_layernorm(tv[0], sv[0], sv[1]))]


def _c3_post(ys, tv, sv):
    return (tv[1] + ys[0] + sv[2],)


def _row(v):
    return v.reshape(1, -1)


def _mixer_fwd(tag, x, p):
    parts = _blk_fwd(f"m1_fwd_{tag}", _m1_pre, [0], _m1_post_split, [x], [p["nw"]], [p["w_in"]],
                     [(b - a, F32) for a, b in zip(IN_SPLITS[:-1], IN_SPLITS[1:])])
    qa, ka, va, qkvb, z, ba = parts
    att = _attn_fwd(f"attn_fwd_{tag}", qa, ka, va, p["sinks"])
    qkvc = _conv_fwd(f"dnconv_fwd_{tag}", qkvb, p["dn_conv_w"], None, True)
    *loc, inv = _dn_local_fwd(f"dnloc_fwd_{tag}", qkvc, ba, p["a_log"], p["dt_bias"])
    og, ss = _dn_rec_fwd(f"dnrec_fwd_{tag}", *loc, z, p["dn_norm_w"])
    (out,) = _blk_fwd(f"m5_fwd_{tag}", _m5_pre, [0, 1], _m5_post, [x, att, og], [], [p["wo_a"], p["wo_b"]],
                      [(D, F32)])
    return out, dict(x=x, qa=qa, ka=ka, va=va, qkvb=qkvb, z=z, ba=ba, att=att, qkvc=qkvc, loc=loc, inv=inv, og=og,
                     ss=ss)


def _mixer_bwd(tag, dy, p, s):
    (dxa, datt, dog), _, (dwo_a, dwo_b) = _blk_bwd(f"m5_bwd_{tag}", _m5_pre, [0, 1], _m5_post,
                                                   [s["x"], s["att"], s["og"]], [], [p["wo_a"], p["wo_b"]], [[dy]],
                                                   linear_post=True)
    rec = _dn_rec_bwd(f"dnrec_bwd_{tag}", *s["loc"], s["z"], p["dn_norm_w"], s["ss"], dog)
    dz, dnw_dn = rec[6], rec[7]
    dqkvc, dba, dalog, ddtb = _dn_local_bwd(f"dnloc_bwd_{tag}", s["qkvc"], s["ba"], p["a_log"], p["dt_bias"],
                                            s["inv"], rec[:6])
    dqkvb, dconvw, _ = _conv_bwd(f"dnconv_bwd_{tag}", s["qkvb"], p["dn_conv_w"], None, True, dqkvc)
    dqa, dka, dva, dsinks = _attn_bwd(f"attn_bwd_{tag}", s["qa"], s["ka"], s["va"], p["sinks"], datt)
    (dx,), (dnw,), (dw_in,) = _blk_bwd(f"m1_bwd_{tag}", _m1_pre, [0], _m1_post, [s["x"]], [p["nw"]], [p["w_in"]],
                                       [[dqa, dka, dva, dqkvb, dz, dba]], res=dxa, linear_post=True)
    return dx, dict(nw=dnw, w_in=dw_in, wo_a=dwo_a, wo_b=dwo_b, dn_conv_w=dconvw, sinks=dsinks, a_log=dalog,
                    dt_bias=ddtb, dn_norm_w=dnw_dn)


def _conformer_fwd(tag, x, p):
    (glu,) = _blk_fwd(f"c1_fwd_{tag}", _c1_pre, [0], _c1_post, [x], [p["nw"], p["b1a"], p["b1b"]], [p["w1"]],
                      [(D, F32)])
    cc = _conv_fwd(f"dwconv_fwd_{tag}", glu, p["w_dw"], p["b_dw"], False)
    (out,) = _blk_fwd(f"c3_fwd_{tag}", _c3_pre, [0], _c3_post, [cc, x], [p["ln_w"], p["ln_b"], p["b2"]], [p["w2"]],
                      [(D, F32)])
    return out, dict(x=x, glu=glu, cc=cc)


def _conformer_bwd(tag, dy, p, s):
    (dcc, dxa), (dlnw, dlnb, db2), (dw2,) = _blk_bwd(f"c3_bwd_{tag}", _c3_pre, [0], _c3_post, [s["cc"], s["x"]],
                                                     [p["ln_w"], p["ln_b"], p["b2"]], [p["w2"]], [[dy]],
                                                     linear_post=True)
    dglu, dwdw, dbdw = _conv_bwd(f"dwconv_bwd_{tag}", s["glu"], p["w_dw"], p["b_dw"], False, dcc)
    (dx,), (dnw, db1a, db1b), (dw1,) = _blk_bwd(f"c1_bwd_{tag}", _c1_pre, [0], _c1_post, [s["x"]],
                                                [p["nw"], p["b1a"], p["b1b"]], [p["w1"]], [[dglu]], res=dxa)
    return dx, dict(nw=dnw, b1a=db1a, b1b=db1b, w1=dw1, w_dw=dwdw, b_dw=dbdw, ln_w=dlnw, ln_b=dlnb, b2=db2, w2=dw2)


def _layer_fwd(l, x, nw, ffn_a, get_ffn_b, p):
    x1, *pre_a = _ffn_fwd(f"ffn_fwd_{l}a", x, _row(nw[0]), ffn_a, 0)
    p = dict(p, nw=_row(nw[1]))
    x2, sv = (_mixer_fwd if l % 2 == 0 else _conformer_fwd)(str(l), x1, p)
    x2, ffn_b = get_ffn_b(x2)
    out, *pre_b = _ffn_fwd(f"ffn_fwd_{l}b", x2, _row(nw[2]), ffn_b, 0)
    return out, (x, x2, p, sv, pre_a, pre_b, ffn_a, ffn_b)


def _layer_bwd(l, dx, nw, saved, after_first=lambda dx: dx):
    x0, x2, p, sv, pre_a, pre_b, ffn_a, ffn_b = saved
    dx, dn2, dffn = _ffn_bwd(f"ffn_bwd_{l}b", x2, _row(nw[2]), ffn_b, 1, pre_b, dx)
    dx = after_first(dx)
    dx, dmix = (_mixer_bwd if l % 2 == 0 else _conformer_bwd)(str(l), dx, p, sv)
    dx, dn0, dffn = _ffn_bwd(f"ffn_bwd_{l}a", x0, _row(nw[0]), ffn_a, 0, pre_a, dx, dffn)
    return dx, jnp.concatenate([dn0, dmix.pop("nw"), dn2], axis=0), dffn, dmix


def _place():
    x, y, c = lax.axis_index("x"), lax.axis_index("y"), lax.axis_index("c")
    chips = [(1 - x, y), (x, 1 - y), (1 - x, 1 - y)]
    return x, y, c, 2 * x + y, chips, [2 * px + py for px, py in chips]


def _handshake(peers):
    barrier = pltpu.get_barrier_semaphore()
    for p in peers:
        pl.semaphore_signal(barrier, inc=1, device_id=p, device_id_type=MESH)
    pl.semaphore_wait(barrier, len(peers))


def _chip_peers():
    x, y, c, _, chips, _ = _place()
    return [(*chip, c) for chip in chips] + [(x, y, 1 - c)]


def _gather_copies(ins, outs, nb, send, recv, fsend, frecv, lsem):
    n_in = len(ins)
    x, y, c, me, chips, cidx = _place()
    sib = (x, y, 1 - c)
    local = [pltpu.make_async_copy(ins[a], outs[a].at[me], lsem.at[a]) for a in range(n_in)]

    def region(a, k, who):
        if k < 2:
            return outs[a].at[cidx[k], pl.ds(who, 1)]
        r = ins[a].shape[1] // 2
        return outs[a].at[cidx[2], pl.ds(who, 1), pl.ds((k - 2) * r, r)]

    def hop(a, k):
        if k < 2:
            src, dst = ins[a].at[pl.ds(c, 1)], outs[a].at[me, pl.ds(c, 1)]
        else:
            r = ins[a].shape[1] // 2
            src = dst = outs[a].at[cidx[3 - k], pl.ds(c, 1), pl.ds((k - 2) * r, r)]
        return pltpu.make_async_remote_copy(src, dst, send.at[4 * a + k], recv.at[4 * a + k],
                                            device_id=(*chips[k % 2], c), device_id_type=MESH)

    def landed(a, k):
        dst = region(a, k, c)
        return pltpu.make_async_remote_copy(dst, dst, send.at[4 * a + k], recv.at[4 * a + k],
                                            device_id=(*chips[k % 2], c), device_id_type=MESH)

    def passed(a, k, who):
        part = region(a, k, who)
        return pltpu.make_async_remote_copy(part, part, fsend.at[4 * a + k], frecv.at[4 * a + k], device_id=sib,
                                            device_id_type=MESH)

    def direct(a, j):
        k = 4 * nb + 3 * (a - nb) + j
        return pltpu.make_async_remote_copy(ins[a], outs[a].at[me], send.at[k], recv.at[k],
                                            device_id=(*chips[j], c), device_id_type=MESH)

    def direct_landed(a, j):
        k = 4 * nb + 3 * (a - nb) + j
        dst = outs[a].at[cidx[j]]
        return pltpu.make_async_remote_copy(dst, dst, send.at[k], recv.at[k], device_id=(*chips[j], c),
                                            device_id_type=MESH)

    sends = [hop(a, k) for a in range(nb) for k in range(2)] + [direct(a, j) for a in range(nb, n_in) for j in range(3)]
    for cp in sends:
        cp.start()
    for cp in local:
        cp.start()
    for a in range(nb):
        for k in (1, 0):
            landed(a, k).wait_recv()
            for cp in (hop(a, 3 - k), passed(a, k, c)):
                cp.start()
                sends.append(cp)
    for a in range(nb):
        for k in (2, 3):
            landed(a, k).wait_recv()
            cp = passed(a, k, c)
            cp.start()
            sends.append(cp)
    for a in range(nb, n_in):
        for j in range(3):
            direct_landed(a, j).wait_recv()
    for a in range(nb):
        for k in range(4):
            passed(a, k, 1 - c).wait_recv()
    for cp in sends:
        cp.wait_send()
    for cp in local:
        cp.wait()


def _gather_sems(n_in, nb):
    dma = pltpu.SemaphoreType.DMA
    n_ici = 4 * nb + 3 * (n_in - nb)
    return [dma((n_ici,)), dma((n_ici,)), dma((4 * nb,)), dma((4 * nb,)), dma((n_in,))]


def _gather_async(name, halved, whole=()):
    nb, arrs = len(halved), list(halved) + list(whole)
    hbm = pltpu.MemorySpace.HBM
    ins = [jax.new_ref(a, memory_space=hbm) for a in arrs]
    outs = [jax.empty_ref(_sds((NCHIP,) + a.shape, a.dtype), memory_space=hbm) for a in arrs]

    @pl.kernel(mesh=plsc.ScalarSubcoreMesh(axis_name="seq", num_cores=1), name=name,
               scratch_types=tuple(_gather_sems(len(arrs), nb)),
               compiler_params=pltpu.CompilerParams(collective_id=2))
    def launch(send, recv, fsend, frecv, lsem):
        _handshake(_chip_peers())
        _gather_copies(ins, outs, nb, send, recv, fsend, frecv, lsem)

    launch()
    return outs


def _swap_halves(name, grads, after=None):
    n = len(grads)
    hbm = pltpu.MemorySpace.HBM
    ins = [jax.new_ref(g, memory_space=hbm) for g in grads]
    outs = [jax.empty_ref(_sds((NCHIP, g.shape[1] // 2) + g.shape[2:], g.dtype), memory_space=hbm) for g in grads]
    tile = (2 * 8, LANES)
    token = None if after is None else jax.empty_ref(_sds(tile, BF16), memory_space=hbm)

    @pl.kernel(mesh=plsc.ScalarSubcoreMesh(axis_name="seq", num_cores=1), name=name,
               scratch_types=(pltpu.SemaphoreType.DMA((n + 1,)), pltpu.SemaphoreType.DMA((n,))),
               compiler_params=pltpu.CompilerParams(collective_id=1))
    def launch(send, recv):
        x, y, c, _, _, _ = _place()
        sib = (x, y, 1 - c)
        _handshake([sib])
        if after is not None:
            tick = pltpu.make_async_copy(after.at[0, 0, 0, pl.ds(0, tile[0]), pl.ds(0, tile[1])], token, send.at[n])
            tick.start()
            tick.wait()
        cps = []
        for a in range(n):
            h = grads[a].shape[1] // 2
            cps.append(pltpu.make_async_remote_copy(ins[a].at[:, pl.ds((1 - c) * h, h)], outs[a], send.at[a],
                                                    recv.at[a], device_id=sib, device_id_type=MESH))
        for cp in cps:
            cp.start()
        for cp in cps:
            cp.wait()

    launch()
    return outs


def _row_tile(r, cap=256):
    return max(t for t in range(8, cap + 1, 8) if r % t == 0)


def _add_half(name, g, r, c_arr):
    _, l, rows, cols = g.shape
    h = l // 2
    tr = _row_tile(rows, 1056)

    def body(c_ref, g_ref, r_ref, o_ref):
        o_ref[...] = (g_ref[...].astype(F32) + r_ref[...].astype(F32)).astype(BF16)

    blk = (None, None, tr, cols)
    return pl.pallas_call(
        body,
        grid_spec=pltpu.PrefetchScalarGridSpec(
            num_scalar_prefetch=1, grid=(NCHIP, h, rows // tr),
            in_specs=[pl.BlockSpec(blk, lambda j, i, t, c_ref: (j, c_ref[0] * h + i, t, 0)),
                      pl.BlockSpec(blk, lambda j, i, t, c_ref: (j, i, t, 0))],
            out_specs=pl.BlockSpec(blk, lambda j, i, t, c_ref: (j, i, t, 0))),
        out_shape=_sds((NCHIP, h, rows, cols), BF16), name=name,
        compiler_params=_cp("parallel", "parallel", "parallel"),
    )(c_arr, g, r)


def _scatter_async(name, parts, sums, where):
    nb = len(parts)
    ins = [jax.new_ref(p, memory_space=pltpu.MemorySpace.HBM) for p in parts]
    dma = pltpu.SemaphoreType.DMA

    @pl.kernel(mesh=plsc.ScalarSubcoreMesh(axis_name="seq", num_cores=1), name=name,
               scratch_types=(dma((3 * nb,)), dma((3 * nb,)), dma((4 * nb,)), dma((4 * nb,)), dma((nb,))),
               compiler_params=pltpu.CompilerParams(collective_id=3))
    def launch(send, recv, fsend, frecv, lsem):
        _handshake(_chip_peers())
        x, y, c, me, chips, cidx = _place()
        sib = (x, y, 1 - c)

        def slot(a, half, chip):
            return sums[a].at[half, chip, pl.ds(where[a], 1)]

        local = [pltpu.make_async_copy(ins[a].at[me], slot(a, c, me), lsem.at[a]) for a in range(nb)]
        for cp in local:
            cp.start()

        def ici(a, j):
            return pltpu.make_async_remote_copy(ins[a].at[cidx[j]], slot(a, c, me), send.at[a * 3 + j],
                                                recv.at[a * 3 + j], device_id=(*chips[j], c), device_id_type=MESH)

        def landed(a, j):
            dst = slot(a, c, cidx[j])
            return pltpu.make_async_remote_copy(dst, dst, send.at[a * 3 + j], recv.at[a * 3 + j],
                                                device_id=(*chips[j], c), device_id_type=MESH)

        def passed(a, j, who):
            dst = slot(a, who, me if j == 3 else cidx[j])
            src = ins[a].at[me] if j == 3 else dst
            return pltpu.make_async_remote_copy(src, dst, fsend.at[a * 4 + j], frecv.at[a * 4 + j], device_id=sib,
                                                device_id_type=MESH)

        sends = [ici(a, j) for a in range(nb) for j in range(3)] + [passed(a, 3, c) for a in range(nb)]
        for cp in sends:
            cp.start()
        for a in range(nb):
            for j in range(3):
                landed(a, j).wait_recv()
                cp = passed(a, j, c)
                cp.start()
                sends.append(cp)
        for a in range(nb):
            for j in range(4):
                passed(a, j, 1 - c).wait_recv()
        for cp in sends:
            cp.wait_send()
        for cp in local:
            cp.wait()

    launch()


def _exchange_small(small, rep):
    def body(small_in, rep_in, small_out, rep_out, lsem, ssend, srecv):
        x, y, c, me, _, _ = _place()
        dev = 4 * x + 2 * y + c
        local = [pltpu.make_async_copy(small_in.at[me], small_out.at[dev], lsem.at[0]),
                 pltpu.make_async_copy(rep_in, rep_out.at[dev], lsem.at[1])]
        for cp in local:
            cp.start()

        def peer(r):
            return (1 - x if r & 4 else x), (1 - y if r & 2 else y), (1 - c if r & 1 else c)

        def tiny(r, which):
            px, py, pc = peer(r)
            k = (r - 1) * 2 + which
            if which == 0:
                return pltpu.make_async_remote_copy(small_in.at[2 * px + py], small_out.at[dev], ssend.at[k],
                                                    srecv.at[k], device_id=(px, py, pc), device_id_type=MESH)
            return pltpu.make_async_remote_copy(rep_in, rep_out.at[dev], ssend.at[k], srecv.at[k],
                                                device_id=(px, py, pc), device_id_type=MESH)

        def tiny_landed(r, which):
            px, py, pc = peer(r)
            k = (r - 1) * 2 + which
            dst = (small_out if which == 0 else rep_out).at[4 * px + 2 * py + pc]
            return pltpu.make_async_remote_copy(dst, dst, ssend.at[k], srecv.at[k], device_id=(px, py, pc),
                                                device_id_type=MESH)

        sends = [tiny(r, w) for r in range(1, NDEV) for w in range(2)]
        for cp in sends:
            cp.start()
        for r in range(1, NDEV):
            for w in range(2):
                tiny_landed(r, w).wait_recv()
        for cp in sends:
            cp.wait_send()
        for cp in local:
            cp.wait()

    dma = pltpu.SemaphoreType.DMA
    return pl.pallas_call(
        body, in_specs=[ANY] * 2, out_specs=[ANY] * 2,
        out_shape=[_sds((NDEV,) + small.shape[1:], F32), _sds((NDEV,) + rep.shape, F32)],
        scratch_shapes=[dma((2,)), dma((2 * (NDEV - 1),)), dma((2 * (NDEV - 1),))], name="exchange_small_grads",
    )(small, rep)


def _adamw_math(w, g, m, v):
    m = B1 * m + (1.0 - B1) * g
    v = B2 * v + (1.0 - B2) * (g * g)
    m_hat = m / (1.0 - B1 ** STEP)
    v_hat = v / (1.0 - B2 ** STEP)
    return -LR * (m_hat / (jnp.sqrt(v_hat) + AEPS) + WD * w), m, v


def _adamw_big(name, w, m, v, parts, row0=0, first=0, outs=None):
    _, _, rows, cols = w.shape
    n = parts.shape[2]
    tr = _row_tile(rows)
    t0 = row0 // tr

    def body(w_ref, m_ref, v_ref, p_ref, *rest):
        g_ref, d_ref, nm_ref, nv_ref = rest[-4:]
        g = p_ref[0].astype(F32)
        for q in range(1, NCHIP):
            g = g + p_ref[q].astype(F32)
        d, nm, nv = _adamw_math(w_ref[...], g, m_ref[...], v_ref[...])
        g_ref[...], d_ref[...], nm_ref[...], nv_ref[...] = g, d, nm, nv

    spec = pl.BlockSpec((None, None, tr, cols), lambda i, p, t: (first + i, p, t, 0))
    na = 0 if outs is None else 4
    return pl.pallas_call(
        body, grid=(n, 2, rows // tr),
        in_specs=[spec, spec, spec,
                  pl.BlockSpec((None, NCHIP, None, tr, cols), lambda i, p, t: (p, 0, i, t0 + t, 0))] + [ANY] * na,
        out_specs=[spec] * 4, out_shape=[_sds(w.shape)] * 4, input_output_aliases={4 + k: k for k in range(na)},
        name=name, compiler_params=_cp("parallel", "parallel", "parallel"),
    )(w, m, v, parts, *(outs or ()))


def _adamw_small(name, w, m, v, parts):
    def body(w_ref, m_ref, v_ref, p_ref, g_ref, d_ref, nm_ref, nv_ref):
        g = p_ref[0]
        for q in range(1, NDEV):
            g = g + p_ref[q]
        d, nm, nv = _adamw_math(w_ref[...], g, m_ref[...], v_ref[...])
        g_ref[...], d_ref[...], nm_ref[...], nv_ref[...] = g, d, nm, nv

    return pl.pallas_call(body, out_shape=[_sds(w.shape)] * 4, name=name)(w, m, v, parts)


def _pack(arrs, rows):
    flat = jnp.concatenate([a.reshape(-1) for a in arrs])
    return jnp.pad(flat, (0, rows * LANES - flat.shape[0])).reshape(rows, LANES)


def _unpack(packed, shapes):
    flat, out, o = packed.reshape(-1), [], 0
    for s in shapes:
        n = 1
        for d in s:
            n *= d
        out.append(flat[o:o + n].reshape(s))
        o += n
    return out


SMALL_ROWS, REP_ROWS = 200, 16


def kernel(x, norm_w, ffn_w_gate, ffn_w_up, ffn_w_down, mix_w_in, dn_conv_w, attn_sinks, dn_a_log, dn_dt_bias, dn_norm_w, mix_w_out, conv_w_pw1, conv_b_pw1, conv_w_dw, conv_b_dw, conv_ln_w, conv_ln_b, conv_w_pw2, conv_b_pw2, final_norm_w, loss_target, m_norm_w, m_ffn_w_gate, m_ffn_w_up, m_ffn_w_down, m_mix_w_in, m_dn_conv_w, m_attn_sinks, m_dn_a_log, m_dn_dt_bias, m_dn_norm_w, m_mix_w_out, m_conv_w_pw1, m_conv_b_pw1, m_conv_w_dw, m_conv_b_dw, m_conv_ln_w, m_conv_ln_b, m_conv_w_pw2, m_conv_b_pw2, m_final_norm_w, v_norm_w, v_ffn_w_gate, v_ffn_w_up, v_ffn_w_down, v_mix_w_in, v_dn_conv_w, v_attn_sinks, v_dn_a_log, v_dn_dt_bias, v_dn_norm_w, v_mix_w_out, v_conv_w_pw1, v_conv_b_pw1, v_conv_w_dw, v_conv_b_dw, v_conv_ln_w, v_conv_ln_b, v_conv_w_pw2, v_conv_b_pw2, v_final_norm_w):
    small_names = ["norm_w", "dn_conv_w", "conv_b_pw1", "conv_w_dw", "conv_b_dw", "conv_ln_w", "conv_ln_b",
                   "conv_b_pw2"]
    rep_names = ["attn_sinks", "dn_a_log", "dn_dt_bias", "dn_norm_w", "final_norm_w"]
    w = dict(norm_w=norm_w, ffn_w_gate=ffn_w_gate, ffn_w_up=ffn_w_up, ffn_w_down=ffn_w_down, mix_w_in=mix_w_in, dn_conv_w=dn_conv_w, attn_sinks=attn_sinks, dn_a_log=dn_a_log, dn_dt_bias=dn_dt_bias, dn_norm_w=dn_norm_w, mix_w_out=mix_w_out, conv_w_pw1=conv_w_pw1, conv_b_pw1=conv_b_pw1, conv_w_dw=conv_w_dw, conv_b_dw=conv_b_dw, conv_ln_w=conv_ln_w, conv_ln_b=conv_ln_b, conv_w_pw2=conv_w_pw2, conv_b_pw2=conv_b_pw2, final_norm_w=final_norm_w)
    m = dict(norm_w=m_norm_w, ffn_w_gate=m_ffn_w_gate, ffn_w_up=m_ffn_w_up, ffn_w_down=m_ffn_w_down, mix_w_in=m_mix_w_in, dn_conv_w=m_dn_conv_w, attn_sinks=m_attn_sinks, dn_a_log=m_dn_a_log, dn_dt_bias=m_dn_dt_bias, dn_norm_w=m_dn_norm_w, mix_w_out=m_mix_w_out, conv_w_pw1=m_conv_w_pw1, conv_b_pw1=m_conv_b_pw1, conv_w_dw=m_conv_w_dw, conv_b_dw=m_conv_b_dw, conv_ln_w=m_conv_ln_w, conv_ln_b=m_conv_ln_b, conv_w_pw2=m_conv_w_pw2, conv_b_pw2=m_conv_b_pw2, final_norm_w=m_final_norm_w)
    v = dict(norm_w=v_norm_w, ffn_w_gate=v_ffn_w_gate, ffn_w_up=v_ffn_w_up, ffn_w_down=v_ffn_w_down, mix_w_in=v_mix_w_in, dn_conv_w=v_dn_conv_w, attn_sinks=v_attn_sinks, dn_a_log=v_dn_a_log, dn_dt_bias=v_dn_dt_bias, dn_norm_w=v_dn_norm_w, mix_w_out=v_mix_w_out, conv_w_pw1=v_conv_w_pw1, conv_b_pw1=v_conv_b_pw1, conv_w_dw=v_conv_w_dw, conv_b_dw=v_conv_b_dw, conv_ln_w=v_conv_ln_w, conv_ln_b=v_conv_ln_b, conv_w_pw2=v_conv_w_pw2, conv_b_pw2=v_conv_b_pw2, final_norm_w=v_final_norm_w)
    order = ["norm_w", "ffn_w_gate", "ffn_w_up", "ffn_w_down", "mix_w_in", "dn_conv_w", "attn_sinks", "dn_a_log",
             "dn_dt_bias", "dn_norm_w", "mix_w_out", "conv_w_pw1", "conv_b_pw1", "conv_w_dw", "conv_b_dw",
             "conv_ln_w", "conv_ln_b", "conv_w_pw2", "conv_b_pw2", "final_norm_w"]

    small_shapes = [w[n].shape for n in small_names]
    rep_shapes = [w[n].shape for n in rep_names]

    def halves(a):
        return a.reshape(a.shape[:-2] + (2, a.shape[-2] // 2, a.shape[-1]))

    tr = lambda a: jnp.swapaxes(a, -1, -2)
    gate_t, up_t = tr(ffn_w_gate), tr(ffn_w_up)

    def layer_shards(l):
        mix_in, mix_out = (mix_w_in, mix_w_out) if l % 2 == 0 else (conv_w_pw1, conv_w_pw2)
        ffn = jnp.concatenate([gate_t[l], up_t[l], ffn_w_down[l]], axis=1)
        return ([t.astype(BF16) for t in (halves(ffn[0]), halves(mix_in[l // 2]), halves(mix_out[l // 2]))],
                [halves(ffn[1]).astype(BF16)])

    first = layer_shards(0)
    first = (first[0] + [_pack([w[n] for n in small_names], SMALL_ROWS)], first[1])
    first, (gate_t, up_t, ffn_w_down, mix_w_in, mix_w_out, conv_w_pw1, conv_w_pw2) = lax.optimization_barrier(
        (first, (gate_t, up_t, ffn_w_down, mix_w_in, mix_w_out, conv_w_pw1, conv_w_pw2)))
    gathering = [(_gather_async("gather_layer0a", first[0][:3], first[0][3:]),
                  _gather_async("gather_layer0b", first[1]))]
    for l in range(1, DEPTH):
        before, after = layer_shards(l)
        gathering.append((_gather_async(f"gather_layer{l}a", before), _gather_async(f"gather_layer{l}b", after)))
    ffn_block = lambda g: g.reshape(NCHIP, 1, 3 * FS, D)

    def mixer_params(l, w_a, w_b):
        e = l // 2
        w_a = w_a.reshape(NCHIP, D, -1)
        w_b = w_b.reshape(D, D)
        if l % 2 == 0:
            return dict(w_in=w_a, dn_conv_w=sm["dn_conv_w"][e], sinks=_row(attn_sinks[e]), a_log=_row(dn_a_log[e]),
                        dt_bias=_row(dn_dt_bias[e]), dn_norm_w=_row(dn_norm_w[e]), wo_a=w_b[:Q_A], wo_b=w_b[Q_A:])
        return dict(b1a=_row(sm["conv_b_pw1"][e, :D]), b1b=_row(sm["conv_b_pw1"][e, D:]), w1=w_a,
                    w_dw=sm["conv_w_dw"][e], b_dw=_row(sm["conv_b_dw"][e]), ln_w=_row(sm["conv_ln_w"][e]),
                    ln_b=_row(sm["conv_ln_b"][e]), b2=_row(sm["conv_b_pw2"][e]), w2=w_b)

    xs, saved = x[0], []
    for l in range(DEPTH):
        got = [r[...] for r in gathering[l][0]]
        if l == 0:
            per_chip = [_unpack(got[3][q], small_shapes) for q in range(NCHIP)]
            sm = {n: jnp.concatenate([per_chip[q][i] for q in range(NCHIP)], axis=-1)
                  for i, n in enumerate(small_names)}
        else:
            xs, got = lax.optimization_barrier((xs, got))

        def second_ffn(x2, l=l):
            x2, got_b = lax.optimization_barrier((x2, gathering[l][1][0][...]))
            return x2, ffn_block(got_b)

        xs, sv = _layer_fwd(l, xs, sm["norm_w"][l], ffn_block(got[0]), second_ffn, mixer_params(l, got[1], got[2]))
        saved.append(sv)
    loss, dx, dfw = _final("final", xs, _row(final_norm_w), loss_target[0])

    hbm = pltpu.MemorySpace.HBM
    row_shapes = dict(ffn=(3 * FS, D), w_in=(D // 2, IN_COLS // NCHIP), w_out=(D // 8, D), pw1=(D // 2, D // 2),
                      pw2=(D // 8, D))
    new_sums = lambda k, n: jax.empty_ref(_sds((2, NCHIP, n) + row_shapes[k], BF16), memory_space=hbm)
    sums_0 = {k: new_sums(k, 1) for k in ("ffn", "w_in", "w_out")}
    sums = dict(ffn=new_sums("ffn", DEPTH - 1), w_in=new_sums("w_in", 1), w_out=new_sums("w_out", 1),
                pw1=new_sums("pw1", 2), pw2=new_sums("pw2", 2))
    c_arr = lax.axis_index("c").astype(jnp.int32).reshape(1)
    dnorm, gmix = [None] * DEPTH, [None] * DEPTH

    def hand_on(l, grads, swapped):
        def run(dx):
            dx, other = lax.optimization_barrier((dx, [r[...] for r in swapped]))
            parts = [_add_half(f"add_half_{l}_{k}", gg, rr, c_arr) for k, (gg, rr) in enumerate(zip(grads, other))]
            dx, parts = lax.optimization_barrier((dx, parts))
            keys = ("ffn", "w_in", "w_out") if l % 2 == 0 else ("ffn", "pw1", "pw2")
            if l == 0:
                _scatter_async("scatter_grads_0", parts, [sums_0[k] for k in keys], [0, 0, 0])
            else:
                _scatter_async(f"scatter_grads_{l}", parts, [sums[k] for k in keys],
                               [l - 1, 0, 0] if l % 2 == 0 else [l - 1, l // 2, l // 2])
            return dx
        return run

    pending = lambda dx: dx
    for l in reversed(range(DEPTH)):
        dx, dnorm[l], dffn, gmix[l] = _layer_bwd(l, dx, sm["norm_w"][l], saved[l], pending)
        if l % 2 == 0:
            g_a, g_b = gmix[l]["w_in"], jnp.concatenate([gmix[l]["wo_a"], gmix[l]["wo_b"]], axis=0)
        else:
            g_a, g_b = gmix[l]["w1"], gmix[l]["w2"]
        g_a = halves(g_a).astype(BF16)
        g_b = g_b.reshape(NCHIP, 2, D // 8, D).astype(BF16)
        dx, grads = lax.optimization_barrier((dx, [dffn, g_a, g_b]))
        pending = hand_on(l, grads, _swap_halves(f"swap_grads_{l}", grads, sums["ffn"] if l < DEPTH - 1 else None))
    gm, gc = [gmix[0], gmix[2]], [gmix[1], gmix[3]]
    small_g = dict(
        norm_w=jnp.stack(dnorm), dn_conv_w=jnp.stack([gm[e]["dn_conv_w"] for e in range(2)]),
        conv_b_pw1=jnp.stack([jnp.concatenate([gc[e]["b1a"], gc[e]["b1b"]], axis=1)[0] for e in range(2)]),
        conv_w_dw=jnp.stack([gc[e]["w_dw"] for e in range(2)]),
        conv_b_dw=jnp.stack([gc[e]["b_dw"][0] for e in range(2)]),
        conv_ln_w=jnp.stack([gc[e]["ln_w"][0] for e in range(2)]),
        conv_ln_b=jnp.stack([gc[e]["ln_b"][0] for e in range(2)]),
        conv_b_pw2=jnp.stack([gc[e]["b2"][0] for e in range(2)]))
    small_by_chip = jnp.stack([_pack([jnp.split(small_g[n], NCHIP, axis=-1)[q] for n in small_names], SMALL_ROWS)
                               for q in range(NCHIP)])
    rep_g = _pack([jnp.stack([gm[e]["sinks"][0] for e in range(2)]), jnp.stack([gm[e]["a_log"][0] for e in range(2)]),
                   jnp.stack([gm[e]["dt_bias"][0] for e in range(2)]),
                   jnp.stack([gm[e]["dn_norm_w"][0] for e in range(2)]), dfw[0]], REP_ROWS)
    small_sum, rep_sum = _exchange_small(small_by_chip, rep_g)
    dx, small_sum, rep_sum = lax.optimization_barrier((dx, small_sum, rep_sum))
    dx = pending(dx)

    big = (("ffn_w_gate", "ffn", 0), ("ffn_w_up", "ffn", FS), ("ffn_w_down", "ffn", 2 * FS), ("mix_w_in", "w_in", 0),
           ("mix_w_out", "w_out", 0), ("conv_w_pw1", "pw1", 0), ("conv_w_pw2", "pw2", 0))
    views = {n: (tr, tr) if n in ("ffn_w_gate", "ffn_w_up") else (
        (lambda a: a) if w[n].ndim == 4 else halves, lambda o, n=n: o.reshape(w[n].shape)) for n, _, _ in big}
    partial_sums = {k: r[...] for k, r in sums.items()}
    upper = {}
    for n, key, row0 in big:
        view = views[n][0]
        upper[n] = _adamw_big(f"adamw_{n}", view(w[n]), view(m[n]), view(v[n]), partial_sums[key], row0,
                              first=0 if key in ("pw1", "pw2") else 1)
    upper, partial_sums_0 = lax.optimization_barrier((upper, {k: r[...] for k, r in sums_0.items()}))
    res = {}
    for n, key, row0 in big:
        view, back = views[n]
        outs = upper[n] if key not in partial_sums_0 else _adamw_big(
            f"adamw_{n}_0", view(w[n]), view(m[n]), view(v[n]), partial_sums_0[key], row0, first=0, outs=upper[n])
        res[n] = [back(o) for o in outs]
    outs = _adamw_small("adamw_small", *[_pack([d[n] for n in small_names], SMALL_ROWS) for d in (w, m, v)],
                        small_sum)
    for i, n in enumerate(small_names):
        res[n] = [_unpack(o, small_shapes)[i] for o in outs]
    outs = _adamw_small("adamw_replicated", *[_pack([d[n] for n in rep_names], REP_ROWS) for d in (w, m, v)],
                        rep_sum)
    for i, n in enumerate(rep_names):
        res[n] = [_unpack(o, rep_shapes)[i] for o in outs]

    total = lax.psum(loss[0, 0], ("x", "y", "c"))
    return (total, dx[None], *[res[n][0] for n in order], *[res[n][1] for n in order],
            *[res[n][2] for n in order], *[res[n][3] for n in order])
```

```python
import jax
import jax.numpy as jnp
from jax import lax
from jax.experimental import pallas as pl
from jax.experimental.pallas import tpu as pltpu
from jax.experimental.pallas import tpu_sc as plsc

F32, BF16 = jnp.float32, jnp.bfloat16
MESH = pl.DeviceIdType.MESH
ANY = pl.BlockSpec(memory_space=pl.ANY)

T, D, F = 2048, 1024, 2816
DEPTH = 4
EPS = 1e-6
HEADS, HDIM, KV_HEADS, GROUP = 8, 64, 2, 4
WINDOW = BLOCK = 128
CHUNK = 64
NCHUNK = T // CHUNK
DN_CONV, CONV_WIDTH = 4, 31
Q_A, KV_A, QKV_B, V_B = 512, 128, 1536, 512
IN_COLS = 2832
IN_SPLITS = (0, 512, 640, 768, 2304, 2816, 2832)
NCHIP, NDEV = 4, 8
FS = F // NCHIP
LR, B1, B2, AEPS, WD, STEP = 0.001, 0.9, 0.999, 1e-08, 0.01, 10
V7X_VMEM_BYTES = 64 * 1024 * 1024
VMEM_LIMIT = V7X_VMEM_BYTES * 7 // 8
LANES = 128


def _cp(*sem):
    return pltpu.CompilerParams(dimension_semantics=sem, vmem_limit_bytes=VMEM_LIMIT)


def _sds(shape, dtype=F32):
    return jax.ShapeDtypeStruct(tuple(shape), dtype)


def _full(shape):
    nd = len(shape)
    return pl.BlockSpec(tuple(shape), lambda *_: (0,) * nd)


def _split_bf16(a):
    hi = a.astype(BF16)
    return hi, (a - hi.astype(F32)).astype(BF16)


def _dg(a, b, ca, cb, hi=False):
    if a.ndim == 3 and b.ndim == 3:
        dims = (((ca + 1,), (cb + 1,)), ((0,), (0,)))
    else:
        dims = (((ca,), (cb,)), ((), ()))
    dot = lambda p, q: lax.dot_general(p, q, dims, preferred_element_type=F32)
    if hi:
        a_hi, a_lo = _split_bf16(a.astype(F32))
        b_hi, b_lo = _split_bf16(b.astype(F32))
        return dot(a_hi, b_hi) + (dot(a_hi, b_lo) + dot(a_lo, b_hi))
    return dot(a.astype(BF16), b.astype(BF16))


def _make_mm(hi):
    @jax.custom_vjp
    def nn(a, b):
        return _dg(a, b, 1, 0, hi)

    @jax.custom_vjp
    def nt(a, b):
        return _dg(a, b, 1, 1, hi)

    @jax.custom_vjp
    def tn(a, b):
        return _dg(a, b, 0, 0, hi)

    nn.defvjp(lambda a, b: (_dg(a, b, 1, 0, hi), (a, b)),
              lambda r, g: (_dg(g, r[1], 1, 1, hi).astype(r[0].dtype), _dg(r[0], g, 0, 0, hi).astype(r[1].dtype)))
    nt.defvjp(lambda a, b: (_dg(a, b, 1, 1, hi), (a, b)),
              lambda r, g: (_dg(g, r[1], 1, 0, hi).astype(r[0].dtype), _dg(g, r[0], 0, 0, hi).astype(r[1].dtype)))
    tn.defvjp(lambda a, b: (_dg(a, b, 0, 0, hi), (a, b)),
              lambda r, g: (_dg(r[1], g, 1, 1, hi).astype(r[0].dtype), _dg(r[0], g, 1, 0, hi).astype(r[1].dtype)))
    return nn, nt, tn


_nn, _nt, _tn = _make_mm(False)
_nn_hi, _nt_hi, _tn_hi = _make_mm(True)


def _rms(x, w):
    return x * lax.rsqrt(jnp.mean(x * x, axis=-1, keepdims=True) + EPS) * w


def _layernorm(x, w, b):
    xc = x - jnp.mean(x, axis=-1, keepdims=True)
    return xc * lax.rsqrt(jnp.mean(xc * xc, axis=-1, keepdims=True) + EPS) * w + b


def _silu(x):
    return x * jax.nn.sigmoid(x)


def _iota2(shape, dim):
    return lax.broadcasted_iota(jnp.int32, shape, dim)


def _flat_weights(lhs_idx, weights):
    specs, ops, lhs_of, where = [], [], [], []
    for a, (k, w) in enumerate(zip(lhs_idx, weights)):
        for q in range(1 if w.ndim == 2 else w.shape[0]):
            specs.append(_full(w.shape) if w.ndim == 2
                         else pl.BlockSpec((None,) + w.shape[1:], lambda i, q=q: (q, 0, 0)))
            ops.append(w)
            lhs_of.append(k)
            where.append((a, None if w.ndim == 2 else q))
    return specs, ops, lhs_of, where


def _blk_fwd(name, pre, lhs_idx, post, toks, smalls, weights, outs, tm=512):
    wspecs, wops, lhs_of, _ = _flat_weights(lhs_idx, weights)
    nt_, ns, nw = len(toks), len(smalls), len(wops)

    def body(*refs):
        tv = [r[...] for r in refs[:nt_]]
        sv = [r[...] for r in refs[nt_:nt_ + ns]]
        wr = refs[nt_ + ns:nt_ + ns + nw]
        orf = refs[nt_ + ns + nw:]
        lhs = pre(tv, sv)
        ys = [_dg(lhs[i], w[...], 1, 0) for i, w in zip(lhs_of, wr)]
        for o_ref, o in zip(orf, post(ys, tv, sv)):
            o_ref[...] = o.astype(o_ref.dtype)

    in_specs = ([pl.BlockSpec((tm, a.shape[1]), lambda i: (i, 0)) for a in toks]
                + [_full(a.shape) for a in smalls] + wspecs)
    out_specs = [pl.BlockSpec((tm, w_), lambda i: (i, 0)) for w_, _ in outs]
    return pl.pallas_call(
        body, grid=(T // tm,), in_specs=in_specs, out_specs=out_specs,
        out_shape=[_sds((T, w_), dt) for w_, dt in outs], name=name, compiler_params=_cp("parallel"),
    )(*toks, *smalls, *wops)


def _blk_bwd(name, pre, lhs_idx, post, toks, smalls, weights, ct_groups, res=None, linear_post=False, tm=256,
             wchunk=512):
    wspecs, wops, lhs_of, where = _flat_weights(lhs_idx, weights)
    nt_, ns, nw, na = len(toks), len(smalls), len(wops), len(weights)
    cts = [a for g in ct_groups for a in g]
    nc = len(cts)
    widths = [sum(a.shape[1] for a in g) for g in ct_groups]
    has_res = res is not None

    def body(*refs):
        p = 0
        tr = refs[p:p + nt_]; p += nt_
        sr = refs[p:p + ns]; p += ns
        wr = refs[p:p + nw]; p += nw
        cr = refs[p:p + nc]; p += nc
        rr = refs[p:p + has_res]; p += has_res
        dtr = refs[p:p + nt_]; p += nt_
        dsr = refs[p:p + ns]; p += ns
        dwr = refs[p:p + na]; p += na
        scr = refs[p:]
        i = pl.program_id(0)

        @pl.when(i == 0)
        def _():
            for r in list(dsr) + list(dwr):
                r[...] = jnp.zeros_like(r)

        tv = [r[...] for r in tr]
        sv = [r[...] for r in sr]
        ctv, q, si = [], 0, 0
        for g in ct_groups:
            if len(g) == 1:
                ctv.append(cr[q][...].astype(F32))
            else:
                off = 0
                for j, a in enumerate(g):
                    scr[si][:, off:off + a.shape[1]] = cr[q + j][...].astype(F32)
                    off += a.shape[1]
                ctv.append(scr[si][...])
                si += 1
            q += len(g)

        lhs, vjp_pre = jax.vjp(lambda *a: tuple(pre(list(a[:nt_]), list(a[nt_:]))), *tv, *sv)
        lhs_b = [l.astype(BF16) for l in lhs]
        ys = [jnp.zeros((tm, w.shape[1]), F32) if linear_post else _dg(lhs_b[k], w[...], 1, 0)
              for k, w in zip(lhs_of, wr)]
        _, vjp_post = jax.vjp(lambda *a: tuple(post(list(a[:nw]), list(a[nw:nw + nt_]), list(a[nw + nt_:]))),
                              *ys, *tv, *sv)
        gp = vjp_post(tuple(ctv))
        dys, dt_post, ds_post = gp[:nw], gp[nw:nw + nt_], gp[nw + nt_:]
        dlhs = [None] * len(lhs)
        for k, w, dy, (a, q) in zip(lhs_of, wr, dys, where):
            dyb = dy.astype(BF16)
            n = w.shape[1]
            for c0 in range(0, n, wchunk):
                c1 = min(n, c0 + wchunk)
                part = _dg(lhs_b[k], dyb[:, c0:c1], 0, 0)
                if q is None:
                    dwr[a][:, c0:c1] += part
                else:
                    dwr[a][q, :, c0:c1] += part
            d = _dg(dyb, w[...], 1, 1)
            dlhs[k] = d if dlhs[k] is None else dlhs[k] + d
        gq = vjp_pre(tuple(d.astype(l.dtype) for d, l in zip(dlhs, lhs)))
        dt_pre, ds_pre = gq[:nt_], gq[nt_:]
        for j in range(nt_):
            d = dt_post[j] + dt_pre[j]
            if j == 0 and has_res:
                d = d + rr[0][...]
            dtr[j][...] = d
        for j in range(ns):
            dsr[j][...] += ds_post[j] + ds_pre[j]

    tok_spec = lambda a: pl.BlockSpec((tm, a.shape[1]), lambda i: (i, 0))
    in_specs = ([tok_spec(a) for a in toks] + [_full(a.shape) for a in smalls] + wspecs
                + [tok_spec(a) for a in cts] + ([tok_spec(res)] if has_res else []))
    out_specs = [tok_spec(a) for a in toks] + [_full(a.shape) for a in smalls] + [_full(w.shape) for w in weights]
    out_shape = ([_sds(a.shape) for a in toks] + [_sds(a.shape) for a in smalls] + [_sds(w.shape) for w in weights])
    scratch = [pltpu.VMEM((tm, wd), F32) for g, wd in zip(ct_groups, widths) if len(g) > 1]
    outs = pl.pallas_call(
        body, grid=(T // tm,), in_specs=in_specs, out_specs=out_specs, out_shape=out_shape,
        scratch_shapes=scratch, name=name, compiler_params=_cp("arbitrary"),
    )(*toks, *smalls, *wops, *cts, *([res] if has_res else []))
    return outs[:nt_], outs[nt_:nt_ + ns], outs[nt_ + ns:]


def _ffn_fwd(name, x, nw, ffn, idx, tm=1024):
    def body(x_ref, nw_ref, wg_ref, wu_ref, wd_ref, o_ref, a_ref, b_ref, h_ref):
        s = pl.program_id(1)

        @pl.when(s == 0)
        def _():
            xv = x_ref[...]
            h_ref[...] = _rms(xv, nw_ref[...]).astype(BF16)
            o_ref[...] = xv

        h = h_ref[...]
        a = _dg(h, wg_ref[...], 1, 1).astype(BF16)
        b = _dg(h, wu_ref[...], 1, 1).astype(BF16)
        a_ref[...] = a
        b_ref[...] = b
        o_ref[...] += 0.5 * _dg(_swiglu_act(a, b)[0], wd_ref[...], 1, 0)

    wspec = lambda k: pl.BlockSpec((None, None, FS, D), lambda i, s: (s, idx, k, 0))
    act = pl.BlockSpec((None, tm, FS), lambda i, s: (s, i, 0))
    return pl.pallas_call(
        body, grid=(T // tm, NCHIP),
        in_specs=[pl.BlockSpec((tm, D), lambda i, s: (i, 0)), _full((1, D)), wspec(0), wspec(1), wspec(2)],
        out_specs=[pl.BlockSpec((tm, D), lambda i, s: (i, 0)), act, act, pl.BlockSpec((tm, D), lambda i, s: (i, 0))],
        out_shape=[_sds((T, D)), _sds((NCHIP, T, FS), BF16), _sds((NCHIP, T, FS), BF16), _sds((T, D), BF16)],
        name=name, compiler_params=_cp("parallel", "arbitrary"),
    )(x, nw, ffn, ffn, ffn)


def _swiglu_act(a, b):
    a, b = a.astype(F32), b.astype(F32)
    sa = jax.nn.sigmoid(a)
    act = a * sa
    return act * b, a, b, sa, act


def _ffn_bwd(name, x, nw, ffn, idx, pre, dy, gbuf=None, tm=512):
    ni = T // tm

    def body(x_ref, dy_ref, nw_ref, wg_ref, wu_ref, wd_ref, a_ref, b_ref, h_ref, dx_ref, dnw_ref, dffn_ref, dh_acc,
             ag, au, ad):
        s, i = pl.program_id(0), pl.program_id(1)
        rows = pl.ds(pl.multiple_of(i * tm, tm), tm)

        @pl.when((s == 0) & (i == 0))
        def _():
            dnw_ref[...] = jnp.zeros_like(dnw_ref)

        @pl.when(i == 0)
        def _():
            ag[...] = jnp.zeros_like(ag)
            au[...] = jnp.zeros_like(au)
            ad[...] = jnp.zeros_like(ad)

        hb = h_ref[...]
        gated, a, b, sa, act = _swiglu_act(a_ref[...], b_ref[...])
        dyb = (0.5 * dy_ref[...]).astype(BF16)
        ad[...] += _dg(gated, dyb, 0, 0)
        dact = _dg(dyb, wd_ref[...], 1, 1)
        da = (dact * b * (sa * (1.0 + a * (1.0 - sa)))).astype(BF16)
        db = (dact * act).astype(BF16)
        ag[...] += _dg(da, hb, 0, 0)
        au[...] += _dg(db, hb, 0, 0)
        dh = _dg(da, wg_ref[...], 1, 0) + _dg(db, wu_ref[...], 1, 0)

        @pl.when(s == 0)
        def _():
            dh_acc[rows, :] = dh

        @pl.when((s > 0) & (s < NCHIP - 1))
        def _():
            dh_acc[rows, :] += dh

        @pl.when(s == NCHIP - 1)
        def _():
            _, vjp_rms = jax.vjp(_rms, x_ref[...], nw_ref[...])
            dx, dnw = vjp_rms(dh_acc[rows, :] + dh)
            dx_ref[...] = dy_ref[...] + dx
            dnw_ref[...] += dnw

        @pl.when(i == ni - 1)
        def _():
            dffn_ref[0:FS, :] = ag[...].astype(BF16)
            dffn_ref[FS:2 * FS, :] = au[...].astype(BF16)
            dffn_ref[2 * FS:, :] = ad[...].astype(BF16)

    wspec = lambda r, k, blk=0: pl.BlockSpec((None, None, r, D), lambda s, i: (s, blk, k, 0),
                                             pipeline_mode=pl.Buffered(1))
    last = lambda s, i: (jnp.where(s == NCHIP - 1, i, 0), 0)
    nb = 0 if gbuf is None else 1
    act = pl.BlockSpec((None, tm, FS), lambda s, i: (s, i, 0))
    tok = pl.BlockSpec((tm, D), lambda s, i: (i, 0))
    return pl.pallas_call(
        lambda *refs: body(*refs[:9], *refs[9 + nb:]), grid=(NCHIP, ni),
        in_specs=[pl.BlockSpec((tm, D), last), tok, _full((1, D)), wspec(FS, 0), wspec(FS, 1), wspec(FS, 2), act, act,
                  tok] + [ANY] * nb,
        out_specs=[pl.BlockSpec((tm, D), last), _full((1, D)), wspec(3 * FS, 0, idx)],
        out_shape=[_sds((T, D)), _sds((1, D)), _sds((NCHIP, 2, 3 * FS, D), BF16)],
        input_output_aliases={9 + k: 2 + k for k in range(nb)},
        scratch_shapes=[pltpu.VMEM((T, D), F32)] + [pltpu.VMEM((FS, D), F32)] * 3,
        name=name, compiler_params=_cp("arbitrary", "arbitrary"),
    )(x, dy, nw, ffn, ffn, ffn, *pre, *(() if gbuf is None else (gbuf,)))


CONV_ROWS = 256


def _conv_pad(k):
    return 8 * ((k - 1 + 7) // 8)


def _shifted(win, o):
    n = win.shape[0]
    return (win if o % n == 0 else pltpu.roll(win, (n - o) % n, 0))[0:CONV_ROWS, :]


def _conv_fwd(name, x, w, b, act):
    k_w, c = w.shape
    tc = 256 if c % 256 == 0 else LANES
    pad = _conv_pad(k_w)
    has_b = b is not None

    def body(*refs):
        x_ref, w_ref = refs[0], refs[1]
        b_ref = refs[2] if has_b else None
        y_ref, xp = refs[2 + has_b], refs[3 + has_b]
        xp[0:pad, :] = jnp.zeros((pad, tc), F32)
        xp[pad:, :] = x_ref[...]

        def step(t, carry):
            base = pl.multiple_of(t * CONV_ROWS, CONV_ROWS)
            win = xp[pl.ds(base, CONV_ROWS + pad), :]
            acc = jnp.zeros((CONV_ROWS, tc), F32)
            for k in range(k_w):
                o = pad - (k_w - 1) + k
                acc = acc + w_ref[k:k + 1, :] * _shifted(win, o)
            if has_b:
                acc = acc + b_ref[...]
            y_ref[pl.ds(base, CONV_ROWS), :] = _silu(acc) if act else acc
            return carry

        lax.fori_loop(0, T // CONV_ROWS, step, 0)

    col = lambda r: pl.BlockSpec((r, tc), lambda j: (0, j))
    ins = [x, w] + ([b] if has_b else [])
    return pl.pallas_call(
        body, grid=(c // tc,), in_specs=[col(T), col(k_w)] + ([col(1)] if has_b else []), out_specs=col(T),
        out_shape=_sds((T, c)), scratch_shapes=[pltpu.VMEM((T + pad, tc), F32)], name=name,
        compiler_params=_cp("parallel"),
    )(*ins)


def _conv_bwd(name, x, w, b, act, dy):
    k_w, c = w.shape
    tc = 256 if c % 256 == 0 else LANES
    pad = _conv_pad(k_w)
    has_b = b is not None

    def body(*refs):
        x_ref, w_ref, dy_ref = refs[0], refs[1], refs[2]
        b_ref = refs[3] if has_b else None
        dx_ref, dw_ref, db_ref, xp, dp = refs[3 + has_b:]
        xp[0:pad, :] = jnp.zeros((pad, tc), F32)
        xp[pad:, :] = x_ref[...]
        dp[T:, :] = jnp.zeros((pad, tc), F32)
        dw_ref[...] = jnp.zeros_like(dw_ref)
        db_ref[...] = jnp.zeros_like(db_ref)

        def step1(t, carry):
            base = pl.multiple_of(t * CONV_ROWS, CONV_ROWS)
            d = dy_ref[pl.ds(base, CONV_ROWS), :]
            win = xp[pl.ds(base, CONV_ROWS + pad), :]
            offs = [pad - (k_w - 1) + k for k in range(k_w)]
            if act:
                acc = jnp.zeros((CONV_ROWS, tc), F32)
                for k, o in enumerate(offs):
                    acc = acc + w_ref[k:k + 1, :] * _shifted(win, o)
                if has_b:
                    acc = acc + b_ref[...]
                sg = jax.nn.sigmoid(acc)
                d = d * (sg * (1.0 + acc * (1.0 - sg)))
            dp[pl.ds(base, CONV_ROWS), :] = d
            for k, o in enumerate(offs):
                dw_ref[k:k + 1, :] += jnp.sum(d * _shifted(win, o), axis=0, keepdims=True)
            db_ref[...] += jnp.sum(d, axis=0, keepdims=True)
            return carry

        lax.fori_loop(0, T // CONV_ROWS, step1, 0)

        def step2(t, carry):
            base = pl.multiple_of(t * CONV_ROWS, CONV_ROWS)
            win = dp[pl.ds(base, CONV_ROWS + pad), :]
            acc = jnp.zeros((CONV_ROWS, tc), F32)
            for k in range(k_w):
                o = (k_w - 1) - k
                acc = acc + w_ref[k:k + 1, :] * _shifted(win, o)
            dx_ref[pl.ds(base, CONV_ROWS), :] = acc
            return carry

        lax.fori_loop(0, T // CONV_ROWS, step2, 0)

    col = lambda r: pl.BlockSpec((r, tc), lambda j: (0, j))
    ins = [x, w, dy] + ([b] if has_b else [])
    return pl.pallas_call(
        body, grid=(c // tc,), in_specs=[col(T), col(k_w), col(T)] + ([col(1)] if has_b else []),
        out_specs=[col(T), col(k_w), col(1)], out_shape=[_sds((T, c)), _sds((k_w, c)), _sds((1, c))],
        scratch_shapes=[pltpu.VMEM((T + pad, tc), F32), pltpu.VMEM((T + pad, tc), F32)], name=name,
        compiler_params=_cp("parallel"),
    )(*ins)


def _attn_consts(n):
    i = _iota2((BLOCK, 2 * BLOCK), 0)
    j = _iota2((BLOCK, 2 * BLOCK), 1)
    dist = i + BLOCK - j
    valid = (dist >= 0) & (dist < WINDOW) & ((n > 0) | (j >= BLOCK))
    return dist.astype(F32), valid


def _attn_block(q4, kk, vv, sinks, dist, valid, kv):
    outs = []
    lane = _iota2((1, HEADS), 1)
    for g in range(GROUP):
        h = kv * GROUP + g
        slope = 2.0 ** (-8.0 * (h + 1) / HEADS)
        s = _nt(q4[:, g * HDIM:(g + 1) * HDIM], kk) * (HDIM ** -0.5)
        s = jnp.where(valid, s - slope * dist, -1e30)
        sink = jnp.sum(jnp.where(lane == h, sinks, 0.0), axis=1, keepdims=True)
        m = jnp.maximum(jnp.max(s, axis=-1, keepdims=True), sink)
        e = jnp.exp(s - m)
        p = e / (jnp.sum(e, axis=-1, keepdims=True) + jnp.exp(sink - m))
        outs.append(_nn(p, vv))
    return tuple(outs)


def _attn_fwd(name, qa, ka, va, sinks):
    def body(q_ref, k_ref, v_ref, s_ref, o_ref, kp, vp):
        kp[0:BLOCK, :] = jnp.zeros((BLOCK, KV_A), F32)
        vp[0:BLOCK, :] = jnp.zeros((BLOCK, KV_A), F32)
        kp[BLOCK:, :] = k_ref[...]
        vp[BLOCK:, :] = v_ref[...]
        sinks_v = s_ref[...]

        def step(n, carry):
            r = pl.multiple_of(n * BLOCK, BLOCK)
            dist, valid = _attn_consts(n)
            k2 = kp[pl.ds(r, 2 * BLOCK), :]
            v2 = vp[pl.ds(r, 2 * BLOCK), :]
            for kv in range(KV_HEADS):
                q4 = q_ref[pl.ds(r, BLOCK), kv * GROUP * HDIM:(kv + 1) * GROUP * HDIM]
                og = _attn_block(q4, k2[:, kv * HDIM:(kv + 1) * HDIM], v2[:, kv * HDIM:(kv + 1) * HDIM], sinks_v,
                                 dist, valid, kv)
                for g in range(GROUP):
                    h = kv * GROUP + g
                    o_ref[pl.ds(r, BLOCK), h * HDIM:(h + 1) * HDIM] = og[g]
            return carry

        lax.fori_loop(0, T // BLOCK, step, 0)

    return pl.pallas_call(
        body, out_shape=_sds((T, Q_A)),
        scratch_shapes=[pltpu.VMEM((T + BLOCK, KV_A), F32), pltpu.VMEM((T + BLOCK, KV_A), F32)], name=name,
        compiler_params=pltpu.CompilerParams(vmem_limit_bytes=VMEM_LIMIT),
    )(qa, ka, va, sinks)


def _attn_bwd(name, qa, ka, va, sinks, do):
    def body(q_ref, k_ref, v_ref, s_ref, do_ref, dq_ref, dk_ref, dv_ref, ds_ref, kp, vp, dkp, dvp):
        kp[0:BLOCK, :] = jnp.zeros((BLOCK, KV_A), F32)
        vp[0:BLOCK, :] = jnp.zeros((BLOCK, KV_A), F32)
        kp[BLOCK:, :] = k_ref[...]
        vp[BLOCK:, :] = v_ref[...]
        dkp[...] = jnp.zeros_like(dkp)
        dvp[...] = jnp.zeros_like(dvp)
        ds_ref[...] = jnp.zeros_like(ds_ref)
        sinks_v = s_ref[...]

        def step(n, carry):
            r = pl.multiple_of(n * BLOCK, BLOCK)
            dist, valid = _attn_consts(n)
            k2 = kp[pl.ds(r, 2 * BLOCK), :]
            v2 = vp[pl.ds(r, 2 * BLOCK), :]
            for kv in range(KV_HEADS):
                cols = slice(kv * HDIM, (kv + 1) * HDIM)
                q4 = q_ref[pl.ds(r, BLOCK), kv * GROUP * HDIM:(kv + 1) * GROUP * HDIM]
                _, vjp = jax.vjp(lambda q, k, v, s: _attn_block(q, k, v, s, dist, valid, kv),
                                 q4, k2[:, cols], v2[:, cols], sinks_v)
                cts = tuple(do_ref[pl.ds(r, BLOCK), (kv * GROUP + g) * HDIM:(kv * GROUP + g + 1) * HDIM]
                            for g in range(GROUP))
                dq4, dkk, dvv, dsk = vjp(cts)
                dq_ref[pl.ds(r, BLOCK), kv * GROUP * HDIM:(kv + 1) * GROUP * HDIM] = dq4
                dkp[pl.ds(r, 2 * BLOCK), cols] += dkk
                dvp[pl.ds(r, 2 * BLOCK), cols] += dvv
                ds_ref[...] += dsk
            return carry

        lax.fori_loop(0, T // BLOCK, step, 0)
        dk_ref[...] = dkp[BLOCK:, :]
        dv_ref[...] = dvp[BLOCK:, :]

    pad = lambda: pltpu.VMEM((T + BLOCK, KV_A), F32)
    return pl.pallas_call(
        body, out_shape=[_sds((T, Q_A)), _sds((T, KV_A)), _sds((T, KV_A)), _sds((1, HEADS))],
        scratch_shapes=[pad(), pad(), pad(), pad()], name=name,
        compiler_params=pltpu.CompilerParams(vmem_limit_bytes=VMEM_LIMIT),
    )(qa, ka, va, sinks, do)


def _dn_consts():
    i = _iota2((CHUNK, CHUNK), 0)
    j = _iota2((CHUNK, CHUNK), 1)
    return dict(causal=i >= j, strict=i > j, eye=(i == j).astype(F32), ltri=(i >= j).astype(F32),
                ones=jnp.ones((CHUNK, CHUNK), F32), last=(_iota2((CHUNK, 1), 0) == CHUNK - 1).astype(F32))


def _l2norm(x):
    return x * lax.rsqrt(jnp.sum(x * x, axis=-1, keepdims=True) + EPS)


def _head_cols(m):
    lane = _iota2((1, HEADS), 1)
    return jnp.concatenate([jnp.sum(jnp.where(lane == h, m, 0.0), axis=1, keepdims=True)[None]
                            for h in range(HEADS)], axis=0)


@jax.custom_vjp
def _unit_lower_inverse(low, known):
    if known is not None:
        return known
    inv = (_iota2((CHUNK, CHUNK), 0) == _iota2((CHUNK, CHUNK), 1)).astype(F32) - low
    pw = low
    for _ in range(5):
        pw = _dg(pw, pw, 1, 0, True)
        inv = inv + _dg(inv, pw, 1, 0, True)
    return inv


def _unit_lower_inverse_fwd(low, known):
    inv = _unit_lower_inverse(low, known)
    return inv, (inv, known)


def _unit_lower_inverse_bwd(res, g):
    inv, known = res
    d_low = -_dg(inv, _dg(g, inv, 1, 1, True), 0, 0, True)
    return d_low, (None if known is None else jnp.zeros_like(known))


_unit_lower_inverse.defvjp(_unit_lower_inverse_fwd, _unit_lower_inverse_bwd)


def _dn_local(q3, k3, v3, braw, araw, alog, dtb, cs, known_inv=None):
    q = _l2norm(q3) * (HDIM ** -0.5)
    k = _l2norm(k3)
    g = -jnp.exp(alog) * jax.nn.softplus(araw + dtb)
    gc_all = _nn_hi(cs["ltri"], g)
    egc_all = jnp.exp(gc_all)
    beta, gc, egc = _head_cols(jax.nn.sigmoid(braw)), _head_cols(gc_all), _head_cols(egc_all)
    a = jnp.broadcast_to(gc, (HEADS, CHUNK, CHUNK))
    diff = a - jnp.swapaxes(a, 1, 2)
    decay = jnp.where(cs["causal"], jnp.exp(jnp.where(cs["causal"], diff, 0.0)), 0.0)
    kb = k * beta
    low = jnp.where(cs["strict"], _nt(kb, k) * decay, 0.0)
    inv = _unit_lower_inverse(low, known_inv)
    u = _nn_hi(inv, v3 * beta)
    w = _nn_hi(inv, kb * egc)
    attn = _nt(q, k) * decay
    gc_last = jnp.sum(gc * cs["last"], axis=1, keepdims=True)
    return u, w, attn, q * egc, k * jnp.exp(gc_last - gc), egc_all, inv


def _heads3(ref, off=0):
    return jnp.concatenate([ref[:, off + h * HDIM:off + (h + 1) * HDIM][None] for h in range(HEADS)], axis=0)


def _dn_local_fwd(name, qkv, ba, alog, dtb):
    def body(qkv_ref, ba_ref, al_ref, dt_ref, u_ref, w_ref, at_ref, qd_ref, kd_ref, eg_ref, inv_ref):
        bav = ba_ref[...]
        outs = _dn_local(_heads3(qkv_ref), _heads3(qkv_ref, 512), _heads3(qkv_ref, 1024), bav[:, :HEADS],
                         bav[:, HEADS:], al_ref[...], dt_ref[...], _dn_consts())
        for r, o in zip((u_ref, w_ref, at_ref, qd_ref, kd_ref, inv_ref), outs[:5] + outs[6:]):
            _unheads(r, o)
        eg_ref[...] = outs[5]

    row = lambda w_: pl.BlockSpec((CHUNK, w_), lambda n: (n, 0))
    return pl.pallas_call(
        body, grid=(NCHUNK,), in_specs=[row(QKV_B), row(2 * HEADS), _full((1, HEADS)), _full((1, HEADS))],
        out_specs=[row(V_B)] * 5 + [row(HEADS), row(V_B)],
        out_shape=[_sds((T, V_B))] * 5 + [_sds((T, HEADS)), _sds((T, V_B))], name=name,
        compiler_params=_cp("parallel"),
    )(qkv, ba, alog, dtb)


def _dn_local_bwd(name, qkv, ba, alog, dtb, inv, cts):
    def body(qkv_ref, ba_ref, al_ref, dt_ref, inv_ref, du_ref, dw_ref, dat_ref, dqd_ref, dkd_ref, deg_ref,
             dqkv_ref, dba_ref, dal_ref, ddt_ref):
        @pl.when(pl.program_id(0) == 0)
        def _():
            dal_ref[...] = jnp.zeros_like(dal_ref)
            ddt_ref[...] = jnp.zeros_like(ddt_ref)

        cs = _dn_consts()
        bav = ba_ref[...]
        known = _heads3(inv_ref)
        _, vjp = jax.vjp(lambda *a: _dn_local(*a, cs, known)[:6], _heads3(qkv_ref), _heads3(qkv_ref, 512),
                         _heads3(qkv_ref, 1024), bav[:, :HEADS], bav[:, HEADS:], al_ref[...], dt_ref[...])
        dq, dk, dv, dbr, dar, dal, ddt = vjp((_heads3(du_ref), _heads3(dw_ref), _heads3(dat_ref), _heads3(dqd_ref),
                                              _heads3(dkd_ref), deg_ref[...]))
        for h in range(HEADS):
            dqkv_ref[:, h * HDIM:(h + 1) * HDIM] = dq[h]
            dqkv_ref[:, 512 + h * HDIM:512 + (h + 1) * HDIM] = dk[h]
            dqkv_ref[:, 1024 + h * HDIM:1024 + (h + 1) * HDIM] = dv[h]
        dba_ref[:, :HEADS] = dbr
        dba_ref[:, HEADS:] = dar
        dal_ref[...] += dal
        ddt_ref[...] += ddt

    row = lambda w_: pl.BlockSpec((CHUNK, w_), lambda n: (n, 0))
    return pl.pallas_call(
        body, grid=(NCHUNK,),
        in_specs=[row(QKV_B), row(2 * HEADS), _full((1, HEADS)), _full((1, HEADS))] + [row(V_B)] * 6 + [row(HEADS)],
        out_specs=[row(QKV_B), row(2 * HEADS), _full((1, HEADS)), _full((1, HEADS))],
        out_shape=[_sds((T, QKV_B)), _sds((T, 2 * HEADS)), _sds((1, HEADS)), _sds((1, HEADS))], name=name,
        compiler_params=_cp("arbitrary"),
    )(qkv, ba, alog, dtb, inv, *cts)


def _dn_step(s, u, w, attn, qd, kd, egc, z, nw):
    last = (_iota2((CHUNK, 1), 0) == CHUNK - 1).astype(F32)
    gl = jnp.sum(_head_cols(egc) * last, axis=1, keepdims=True)
    v_new = u - _nn(w, s)
    o = _nn(qd, s) + _nn(attn, v_new)
    s_new = s * gl + _tn(kd, v_new)
    return s_new, _rms(o, nw) * _silu(z)


def _unheads(ref, v3):
    for h in range(HEADS):
        ref[:, h * HDIM:(h + 1) * HDIM] = v3[h]


def _dn_rec_fwd(name, u, w, attn, qd, kd, egc, z, nw):
    def body(u_ref, w_ref, at_ref, qd_ref, kd_ref, eg_ref, z_ref, nw_ref, o_ref, ss_ref, s_scr):
        @pl.when(pl.program_id(0) == 0)
        def _():
            s_scr[...] = jnp.zeros_like(s_scr)

        s = s_scr[...]
        ss_ref[...] = s
        s_new, on = _dn_step(s, _heads3(u_ref), _heads3(w_ref), _heads3(at_ref), _heads3(qd_ref), _heads3(kd_ref),
                             eg_ref[...], _heads3(z_ref), nw_ref[...])
        s_scr[...] = s_new
        _unheads(o_ref, on)

    row = lambda w_: pl.BlockSpec((CHUNK, w_), lambda n: (n, 0))
    return pl.pallas_call(
        body, grid=(NCHUNK,), in_specs=[row(V_B)] * 5 + [row(HEADS), row(V_B), _full((1, HDIM))],
        out_specs=[row(V_B), pl.BlockSpec((None, HEADS, HDIM, HDIM), lambda n: (n, 0, 0, 0))],
        out_shape=[_sds((T, V_B)), _sds((NCHUNK, HEADS, HDIM, HDIM))],
        scratch_shapes=[pltpu.VMEM((HEADS, HDIM, HDIM), F32)], name=name, compiler_params=_cp("arbitrary"),
    )(u, w, attn, qd, kd, egc, z, nw)


def _dn_rec_bwd(name, u, w, attn, qd, kd, egc, z, nw, ss, do):
    def body(u_ref, w_ref, at_ref, qd_ref, kd_ref, eg_ref, z_ref, nw_ref, ss_ref, do_ref,
             du_ref, dw_ref, dat_ref, dqd_ref, dkd_ref, deg_ref, dz_ref, dnw_ref, ds_scr):
        @pl.when(pl.program_id(0) == 0)
        def _():
            ds_scr[...] = jnp.zeros_like(ds_scr)
            dnw_ref[...] = jnp.zeros_like(dnw_ref)

        _, vjp = jax.vjp(_dn_step, ss_ref[...], _heads3(u_ref), _heads3(w_ref), _heads3(at_ref), _heads3(qd_ref),
                         _heads3(kd_ref), eg_ref[...], _heads3(z_ref), nw_ref[...])
        ds, du, dw, dat, dqd, dkd, deg, dz, dnw = vjp((ds_scr[...], _heads3(do_ref)))
        ds_scr[...] = ds
        for r, v in zip((du_ref, dw_ref, dat_ref, dqd_ref, dkd_ref, dz_ref), (du, dw, dat, dqd, dkd, dz)):
            _unheads(r, v)
        deg_ref[...] = deg
        dnw_ref[...] += dnw

    row = lambda w_: pl.BlockSpec((CHUNK, w_), lambda n: (NCHUNK - 1 - n, 0))
    return pl.pallas_call(
        body, grid=(NCHUNK,),
        in_specs=[row(V_B)] * 5 + [row(HEADS), row(V_B), _full((1, HDIM)),
                                   pl.BlockSpec((None, HEADS, HDIM, HDIM), lambda n: (NCHUNK - 1 - n, 0, 0, 0)),
                                   row(V_B)],
        out_specs=[row(V_B)] * 5 + [row(HEADS), row(V_B), _full((1, HDIM))],
        out_shape=[_sds((T, V_B))] * 5 + [_sds((T, HEADS)), _sds((T, V_B)), _sds((1, HDIM))],
        scratch_shapes=[pltpu.VMEM((HEADS, HDIM, HDIM), F32)], name=name, compiler_params=_cp("arbitrary"),
    )(u, w, attn, qd, kd, egc, z, nw, ss, do)


def _final(name, x, fw, target, tm=512):
    def body(x_ref, fw_ref, t_ref, l_ref, dx_ref, dfw_ref):
        @pl.when(pl.program_id(0) == 0)
        def _():
            l_ref[...] = jnp.zeros_like(l_ref)
            dfw_ref[...] = jnp.zeros_like(dfw_ref)

        tv = t_ref[...]

        def f(xv, fwv):
            err = _rms(xv, fwv) - tv
            per_tok = jnp.mean(err * err, axis=-1, keepdims=True)
            return 0.5 * jnp.sum(per_tok, axis=0, keepdims=True)

        loss, vjp = jax.vjp(f, x_ref[...], fw_ref[...])
        dx, dfw = vjp(jnp.ones((1, 1), F32))
        l_ref[...] += loss
        dx_ref[...] = dx
        dfw_ref[...] += dfw

    tok = pl.BlockSpec((tm, D), lambda i: (i, 0))
    return pl.pallas_call(
        body, grid=(T // tm,), in_specs=[tok, _full((1, D)), tok], out_specs=[_full((1, 1)), tok, _full((1, D))],
        out_shape=[_sds((1, 1)), _sds((T, D)), _sds((1, D))], name=name, compiler_params=_cp("arbitrary"),
    )(x, fw, target)


def _m1_pre(tv, sv):
    return [_rms(tv[0], sv[0])]


def _m1_post(ys, tv, sv):
    return (jnp.concatenate(ys, axis=1),)


def _m1_post_split(ys, tv, sv):
    proj = jnp.concatenate(ys, axis=1)
    return tuple(proj[:, a:b] for a, b in zip(IN_SPLITS[:-1], IN_SPLITS[1:]))


def _m5_pre(tv, sv):
    return [tv[1], tv[2]]


def _m5_post(ys, tv, sv):
    return (tv[0] + ys[0] + ys[1],)


def _c1_pre(tv, sv):
    return [_rms(tv[0], sv[0])]


def _c1_post(ys, tv, sv):
    return ((jnp.concatenate(ys[:2], axis=1) + sv[1]) * jax.nn.sigmoid(jnp.concatenate(ys[2:], axis=1) + sv[2]),)


def _c3_pre(tv, sv):
    return [_silu(_layernorm(tv[0], sv[0], sv[1]))]


def _c3_post(ys, tv, sv):
    return (tv[1] + ys[0] + sv[2],)


def _row(v):
    return v.reshape(1, -1)


def _mixer_fwd(tag, x, p):
    parts = _blk_fwd(f"m1_fwd_{tag}", _m1_pre, [0], _m1_post_split, [x], [p["nw"]], [p["w_in"]],
                     [(b - a, F32) for a, b in zip(IN_SPLITS[:-1], IN_SPLITS[1:])])
    qa, ka, va, qkvb, z, ba = parts
    att = _attn_fwd(f"attn_fwd_{tag}", qa, ka, va, p["sinks"])
    qkvc = _conv_fwd(f"dnconv_fwd_{tag}", qkvb, p["dn_conv_w"], None, True)
    *loc, inv = _dn_local_fwd(f"dnloc_fwd_{tag}", qkvc, ba, p["a_log"], p["dt_bias"])
    og, ss = _dn_rec_fwd(f"dnrec_fwd_{tag}", *loc, z, p["dn_norm_w"])
    (out,) = _blk_fwd(f"m5_fwd_{tag}", _m5_pre, [0, 1], _m5_post, [x, att, og], [], [p["wo_a"], p["wo_b"]],
                      [(D, F32)])
    return out, dict(x=x, qa=qa, ka=ka, va=va, qkvb=qkvb, z=z, ba=ba, att=att, qkvc=qkvc, loc=loc, inv=inv, og=og,
                     ss=ss)


def _mixer_bwd(tag, dy, p, s):
    (dxa, datt, dog), _, (dwo_a, dwo_b) = _blk_bwd(f"m5_bwd_{tag}", _m5_pre, [0, 1], _m5_post,
                                                   [s["x"], s["att"], s["og"]], [], [p["wo_a"], p["wo_b"]], [[dy]],
                                                   linear_post=True)
    rec = _dn_rec_bwd(f"dnrec_bwd_{tag}", *s["loc"], s["z"], p["dn_norm_w"], s["ss"], dog)
    dz, dnw_dn = rec[6], rec[7]
    dqkvc, dba, dalog, ddtb = _dn_local_bwd(f"dnloc_bwd_{tag}", s["qkvc"], s["ba"], p["a_log"], p["dt_bias"],
                                            s["inv"], rec[:6])
    dqkvb, dconvw, _ = _conv_bwd(f"dnconv_bwd_{tag}", s["qkvb"], p["dn_conv_w"], None, True, dqkvc)
    dqa, dka, dva, dsinks = _attn_bwd(f"attn_bwd_{tag}", s["qa"], s["ka"], s["va"], p["sinks"], datt)
    (dx,), (dnw,), (dw_in,) = _blk_bwd(f"m1_bwd_{tag}", _m1_pre, [0], _m1_post, [s["x"]], [p["nw"]], [p["w_in"]],
                                       [[dqa, dka, dva, dqkvb, dz, dba]], res=dxa, linear_post=True)
    return dx, dict(nw=dnw, w_in=dw_in, wo_a=dwo_a, wo_b=dwo_b, dn_conv_w=dconvw, sinks=dsinks, a_log=dalog,
                    dt_bias=ddtb, dn_norm_w=dnw_dn)


def _conformer_fwd(tag, x, p):
    (glu,) = _blk_fwd(f"c1_fwd_{tag}", _c1_pre, [0], _c1_post, [x], [p["nw"], p["b1a"], p["b1b"]], [p["w1"]],
                      [(D, F32)])
    cc = _conv_fwd(f"dwconv_fwd_{tag}", glu, p["w_dw"], p["b_dw"], False)
    (out,) = _blk_fwd(f"c3_fwd_{tag}", _c3_pre, [0], _c3_post, [cc, x], [p["ln_w"], p["ln_b"], p["b2"]], [p["w2"]],
                      [(D, F32)])
    return out, dict(x=x, glu=glu, cc=cc)


def _conformer_bwd(tag, dy, p, s):
    (dcc, dxa), (dlnw, dlnb, db2), (dw2,) = _blk_bwd(f"c3_bwd_{tag}", _c3_pre, [0], _c3_post, [s["cc"], s["x"]],
                                                     [p["ln_w"], p["ln_b"], p["b2"]], [p["w2"]], [[dy]],
                                                     linear_post=True)
    dglu, dwdw, dbdw = _conv_bwd(f"dwconv_bwd_{tag}", s["glu"], p["w_dw"], p["b_dw"], False, dcc)
    (dx,), (dnw, db1a, db1b), (dw1,) = _blk_bwd(f"c1_bwd_{tag}", _c1_pre, [0], _c1_post, [s["x"]],
                                                [p["nw"], p["b1a"], p["b1b"]], [p["w1"]], [[dglu]], res=dxa)
    return dx, dict(nw=dnw, b1a=db1a, b1b=db1b, w1=dw1, w_dw=dwdw, b_dw=dbdw, ln_w=dlnw, ln_b=dlnb, b2=db2, w2=dw2)


def _layer_fwd(l, x, nw, ffn_a, get_ffn_b, p):
    x1, *pre_a = _ffn_fwd(f"ffn_fwd_{l}a", x, _row(nw[0]), ffn_a, 0)
    p = dict(p, nw=_row(nw[1]))
    x2, sv = (_mixer_fwd if l % 2 == 0 else _conformer_fwd)(str(l), x1, p)
    x2, ffn_b = get_ffn_b(x2)
    out, *pre_b = _ffn_fwd(f"ffn_fwd_{l}b", x2, _row(nw[2]), ffn_b, 0)
    return out, (x, x2, p, sv, pre_a, pre_b, ffn_a, ffn_b)


def _layer_bwd(l, dx, nw, saved, after_first=lambda dx: dx):
    x0, x2, p, sv, pre_a, pre_b, ffn_a, ffn_b = saved
    dx, dn2, dffn = _ffn_bwd(f"ffn_bwd_{l}b", x2, _row(nw[2]), ffn_b, 1, pre_b, dx)
    dx = after_first(dx)
    dx, dmix = (_mixer_bwd if l % 2 == 0 else _conformer_bwd)(str(l), dx, p, sv)
    dx, dn0, dffn = _ffn_bwd(f"ffn_bwd_{l}a", x0, _row(nw[0]), ffn_a, 0, pre_a, dx, dffn)
    return dx, jnp.concatenate([dn0, dmix.pop("nw"), dn2], axis=0), dffn, dmix


def _place(staggered=False):
    x, y, c = lax.axis_index("x"), lax.axis_index("y"), lax.axis_index("c")
    s = c if staggered else 0
    flip_x, flip_y = (x + (1 - s) * (1 - 2 * x), y + s * (1 - 2 * y)), (x + s * (1 - 2 * x), y + (1 - s) * (1 - 2 * y))
    chips = [flip_x, flip_y, (1 - x, 1 - y)]
    return x, y, c, 2 * x + y, chips, [2 * px + py for px, py in chips]


def _handshake(peers):
    barrier = pltpu.get_barrier_semaphore()
    for p in peers:
        pl.semaphore_signal(barrier, inc=1, device_id=p, device_id_type=MESH)
    pl.semaphore_wait(barrier, len(peers))


def _chip_peers():
    x, y, c, _, chips, _ = _place()
    return [(*chip, c) for chip in chips] + [(x, y, 1 - c)]


def _gather_copies(ins, outs, nb, send, recv, fsend, frecv, lsem):
    n_in = len(ins)
    x, y, c, me, chips, cidx = _place(staggered=True)
    sib = (x, y, 1 - c)
    local = [pltpu.make_async_copy(ins[a], outs[a].at[me], lsem.at[a]) for a in range(n_in)]

    def region(a, k, who):
        if k < 2:
            return outs[a].at[cidx[k], pl.ds(who, 1)]
        r = ins[a].shape[1] // 2
        return outs[a].at[cidx[2], pl.ds(who, 1), pl.ds((k - 2) * r, r)]

    def hop(a, k):
        if k < 2:
            src, dst = ins[a].at[pl.ds(c, 1)], outs[a].at[me, pl.ds(c, 1)]
        else:
            r = ins[a].shape[1] // 2
            src = dst = outs[a].at[cidx[3 - k], pl.ds(c, 1), pl.ds((k - 2) * r, r)]
        return pltpu.make_async_remote_copy(src, dst, send.at[4 * a + k], recv.at[4 * a + k],
                                            device_id=(*chips[k % 2], c), device_id_type=MESH)

    def landed(a, k):
        dst = region(a, k, c)
        return pltpu.make_async_remote_copy(dst, dst, send.at[4 * a + k], recv.at[4 * a + k],
                                            device_id=(*chips[k % 2], c), device_id_type=MESH)

    def passed(a, k, who):
        part = region(a, k, who)
        return pltpu.make_async_remote_copy(part, part, fsend.at[4 * a + k], frecv.at[4 * a + k], device_id=sib,
                                            device_id_type=MESH)

    def direct(a, j):
        k = 4 * nb + 3 * (a - nb) + j
        return pltpu.make_async_remote_copy(ins[a], outs[a].at[me], send.at[k], recv.at[k],
                                            device_id=(*chips[j], c), device_id_type=MESH)

    def direct_landed(a, j):
        k = 4 * nb + 3 * (a - nb) + j
        dst = outs[a].at[cidx[j]]
        return pltpu.make_async_remote_copy(dst, dst, send.at[k], recv.at[k], device_id=(*chips[j], c),
                                            device_id_type=MESH)

    sends = [hop(a, k) for a in range(nb) for k in range(2)] + [direct(a, j) for a in range(nb, n_in) for j in range(3)]
    for cp in sends:
        cp.start()
    for cp in local:
        cp.start()
    for a in range(nb):
        for k in (1, 0):
            landed(a, k).wait_recv()
            for cp in (hop(a, 3 - k), passed(a, k, c)):
                cp.start()
                sends.append(cp)
    for a in range(nb):
        for k in (2, 3):
            landed(a, k).wait_recv()
            cp = passed(a, k, c)
            cp.start()
            sends.append(cp)
    for a in range(nb, n_in):
        for j in range(3):
            direct_landed(a, j).wait_recv()
    for a in range(nb):
        for k in range(4):
            passed(a, k, 1 - c).wait_recv()
    for cp in sends:
        cp.wait_send()
    for cp in local:
        cp.wait()


def _gather_sems(n_in, nb):
    dma = pltpu.SemaphoreType.DMA
    n_ici = 4 * nb + 3 * (n_in - nb)
    return [dma((n_ici,)), dma((n_ici,)), dma((4 * nb,)), dma((4 * nb,)), dma((n_in,))]


def _gather_async(name, halved, whole=()):
    nb, arrs = len(halved), list(halved) + list(whole)
    hbm = pltpu.MemorySpace.HBM
    ins = [jax.new_ref(a, memory_space=hbm) for a in arrs]
    outs = [jax.empty_ref(_sds((NCHIP,) + a.shape, a.dtype), memory_space=hbm) for a in arrs]

    @pl.kernel(mesh=plsc.ScalarSubcoreMesh(axis_name="seq", num_cores=1), name=name,
               scratch_types=tuple(_gather_sems(len(arrs), nb)),
               compiler_params=pltpu.CompilerParams(collective_id=2))
    def launch(send, recv, fsend, frecv, lsem):
        _handshake(_chip_peers())
        _gather_copies(ins, outs, nb, send, recv, fsend, frecv, lsem)

    launch()
    return outs


def _swap_halves(name, grads, after=None):
    n = len(grads)
    hbm = pltpu.MemorySpace.HBM
    ins = [jax.new_ref(g, memory_space=hbm) for g in grads]
    outs = [jax.empty_ref(_sds((NCHIP, g.shape[1] // 2) + g.shape[2:], g.dtype), memory_space=hbm) for g in grads]
    tile = (2 * 8, LANES)
    token = None if after is None else jax.empty_ref(_sds(tile, BF16), memory_space=hbm)

    @pl.kernel(mesh=plsc.ScalarSubcoreMesh(axis_name="seq", num_cores=1), name=name,
               scratch_types=(pltpu.SemaphoreType.DMA((n + 1,)), pltpu.SemaphoreType.DMA((n,))),
               compiler_params=pltpu.CompilerParams(collective_id=1))
    def launch(send, recv):
        x, y, c, _, _, _ = _place()
        sib = (x, y, 1 - c)
        _handshake([sib])
        if after is not None:
            tick = pltpu.make_async_copy(after.at[0, 0, 0, pl.ds(0, tile[0]), pl.ds(0, tile[1])], token, send.at[n])
            tick.start()
            tick.wait()
        cps = []
        for a in range(n):
            h = grads[a].shape[1] // 2
            cps.append(pltpu.make_async_remote_copy(ins[a].at[:, pl.ds((1 - c) * h, h)], outs[a], send.at[a],
                                                    recv.at[a], device_id=sib, device_id_type=MESH))
        for cp in cps:
            cp.start()
        for cp in cps:
            cp.wait()

    launch()
    return outs


def _row_tile(r, cap=256):
    return max(t for t in range(8, cap + 1, 8) if r % t == 0)


def _add_half(name, g, r, c_arr):
    _, l, rows, cols = g.shape
    h = l // 2
    tr = _row_tile(rows, 1056)

    def body(c_ref, g_ref, r_ref, o_ref):
        o_ref[...] = (g_ref[...].astype(F32) + r_ref[...].astype(F32)).astype(BF16)

    blk = (None, None, tr, cols)
    return pl.pallas_call(
        body,
        grid_spec=pltpu.PrefetchScalarGridSpec(
            num_scalar_prefetch=1, grid=(NCHIP, h, rows // tr),
            in_specs=[pl.BlockSpec(blk, lambda j, i, t, c_ref: (j, c_ref[0] * h + i, t, 0)),
                      pl.BlockSpec(blk, lambda j, i, t, c_ref: (j, i, t, 0))],
            out_specs=pl.BlockSpec(blk, lambda j, i, t, c_ref: (j, i, t, 0))),
        out_shape=_sds((NCHIP, h, rows, cols), BF16), name=name,
        compiler_params=_cp("parallel", "parallel", "parallel"),
    )(c_arr, g, r)


def _scatter_async(name, parts, sums, where):
    nb = len(parts)
    ins = [jax.new_ref(p, memory_space=pltpu.MemorySpace.HBM) for p in parts]
    dma = pltpu.SemaphoreType.DMA

    @pl.kernel(mesh=plsc.ScalarSubcoreMesh(axis_name="seq", num_cores=1), name=name,
               scratch_types=(dma((3 * nb,)), dma((3 * nb,)), dma((4 * nb,)), dma((4 * nb,)), dma((nb,))),
               compiler_params=pltpu.CompilerParams(collective_id=3))
    def launch(send, recv, fsend, frecv, lsem):
        _handshake(_chip_peers())
        x, y, c, me, chips, cidx = _place(staggered=True)
        sib = (x, y, 1 - c)

        def slot(a, half, chip):
            return sums[a].at[half, chip, pl.ds(where[a], 1)]

        local = [pltpu.make_async_copy(ins[a].at[me], slot(a, c, me), lsem.at[a]) for a in range(nb)]
        for cp in local:
            cp.start()

        def ici(a, j):
            return pltpu.make_async_remote_copy(ins[a].at[cidx[j]], slot(a, c, me), send.at[a * 3 + j],
                                                recv.at[a * 3 + j], device_id=(*chips[j], c), device_id_type=MESH)

        def landed(a, j):
            dst = slot(a, c, cidx[j])
            return pltpu.make_async_remote_copy(dst, dst, send.at[a * 3 + j], recv.at[a * 3 + j],
                                                device_id=(*chips[j], c), device_id_type=MESH)

        def passed(a, j, who):
            dst = slot(a, who, me if j == 3 else cidx[j])
            src = ins[a].at[me] if j == 3 else dst
            return pltpu.make_async_remote_copy(src, dst, fsend.at[a * 4 + j], frecv.at[a * 4 + j], device_id=sib,
                                                device_id_type=MESH)

        sends = [ici(a, j) for a in range(nb) for j in range(3)] + [passed(a, 3, c) for a in range(nb)]
        for cp in sends:
            cp.start()
        for a in range(nb):
            for j in range(3):
                landed(a, j).wait_recv()
                cp = passed(a, j, c)
                cp.start()
                sends.append(cp)
        for a in range(nb):
            for j in range(4):
                passed(a, j, 1 - c).wait_recv()
        for cp in sends:
            cp.wait_send()
        for cp in local:
            cp.wait()

    launch()


def _exchange_small(small, rep):
    def body(small_in, rep_in, small_out, rep_out, lsem, ssend, srecv):
        x, y, c, me, _, _ = _place()
        dev = 4 * x + 2 * y + c
        local = [pltpu.make_async_copy(small_in.at[me], small_out.at[dev], lsem.at[0]),
                 pltpu.make_async_copy(rep_in, rep_out.at[dev], lsem.at[1])]
        for cp in local:
            cp.start()

        def peer(r):
            return (1 - x if r & 4 else x), (1 - y if r & 2 else y), (1 - c if r & 1 else c)

        def tiny(r, which):
            px, py, pc = peer(r)
            k = (r - 1) * 2 + which
            if which == 0:
                return pltpu.make_async_remote_copy(small_in.at[2 * px + py], small_out.at[dev], ssend.at[k],
                                                    srecv.at[k], device_id=(px, py, pc), device_id_type=MESH)
            return pltpu.make_async_remote_copy(rep_in, rep_out.at[dev], ssend.at[k], srecv.at[k],
                                                device_id=(px, py, pc), device_id_type=MESH)

        def tiny_landed(r, which):
            px, py, pc = peer(r)
            k = (r - 1) * 2 + which
            dst = (small_out if which == 0 else rep_out).at[4 * px + 2 * py + pc]
            return pltpu.make_async_remote_copy(dst, dst, ssend.at[k], srecv.at[k], device_id=(px, py, pc),
                                                device_id_type=MESH)

        sends = [tiny(r, w) for r in range(1, NDEV) for w in range(2)]
        for cp in sends:
            cp.start()
        for r in range(1, NDEV):
            for w in range(2):
                tiny_landed(r, w).wait_recv()
        for cp in sends:
            cp.wait_send()
        for cp in local:
            cp.wait()

    dma = pltpu.SemaphoreType.DMA
    return pl.pallas_call(
        body, in_specs=[ANY] * 2, out_specs=[ANY] * 2,
        out_shape=[_sds((NDEV,) + small.shape[1:], F32), _sds((NDEV,) + rep.shape, F32)],
        scratch_shapes=[dma((2,)), dma((2 * (NDEV - 1),)), dma((2 * (NDEV - 1),))], name="exchange_small_grads",
    )(small, rep)


def _adamw_math(w, g, m, v):
    m = B1 * m + (1.0 - B1) * g
    v = B2 * v + (1.0 - B2) * (g * g)
    m_hat = m / (1.0 - B1 ** STEP)
    v_hat = v / (1.0 - B2 ** STEP)
    return -LR * (m_hat / (jnp.sqrt(v_hat) + AEPS) + WD * w), m, v


def _adamw_big(name, w, m, v, parts, row0=0, first=0, outs=None):
    _, _, rows, cols = w.shape
    n = parts.shape[2]
    tr = _row_tile(rows)
    t0 = row0 // tr

    def body(w_ref, m_ref, v_ref, p_ref, *rest):
        g_ref, d_ref, nm_ref, nv_ref = rest[-4:]
        g = p_ref[0].astype(F32)
        for q in range(1, NCHIP):
            g = g + p_ref[q].astype(F32)
        d, nm, nv = _adamw_math(w_ref[...], g, m_ref[...], v_ref[...])
        g_ref[...], d_ref[...], nm_ref[...], nv_ref[...] = g, d, nm, nv

    spec = pl.BlockSpec((None, None, tr, cols), lambda i, p, t: (first + i, p, t, 0))
    na = 0 if outs is None else 4
    return pl.pallas_call(
        body, grid=(n, 2, rows // tr),
        in_specs=[spec, spec, spec,
                  pl.BlockSpec((None, NCHIP, None, tr, cols), lambda i, p, t: (p, 0, i, t0 + t, 0))] + [ANY] * na,
        out_specs=[spec] * 4, out_shape=[_sds(w.shape)] * 4, input_output_aliases={4 + k: k for k in range(na)},
        name=name, compiler_params=_cp("parallel", "parallel", "parallel"),
    )(w, m, v, parts, *(outs or ()))


def _adamw_small(name, w, m, v, parts):
    def body(w_ref, m_ref, v_ref, p_ref, g_ref, d_ref, nm_ref, nv_ref):
        g = p_ref[0]
        for q in range(1, NDEV):
            g = g + p_ref[q]
        d, nm, nv = _adamw_math(w_ref[...], g, m_ref[...], v_ref[...])
        g_ref[...], d_ref[...], nm_ref[...], nv_ref[...] = g, d, nm, nv

    return pl.pallas_call(body, out_shape=[_sds(w.shape)] * 4, name=name)(w, m, v, parts)


def _pack(arrs, rows):
    flat = jnp.concatenate([a.reshape(-1) for a in arrs])
    return jnp.pad(flat, (0, rows * LANES - flat.shape[0])).reshape(rows, LANES)


def _unpack(packed, shapes):
    flat, out, o = packed.reshape(-1), [], 0
    for s in shapes:
        n = 1
        for d in s:
            n *= d
        out.append(flat[o:o + n].reshape(s))
        o += n
    return out


SMALL_ROWS, REP_ROWS = 200, 16


def kernel(x, norm_w, ffn_w_gate, ffn_w_up, ffn_w_down, mix_w_in, dn_conv_w, attn_sinks, dn_a_log, dn_dt_bias, dn_norm_w, mix_w_out, conv_w_pw1, conv_b_pw1, conv_w_dw, conv_b_dw, conv_ln_w, conv_ln_b, conv_w_pw2, conv_b_pw2, final_norm_w, loss_target, m_norm_w, m_ffn_w_gate, m_ffn_w_up, m_ffn_w_down, m_mix_w_in, m_dn_conv_w, m_attn_sinks, m_dn_a_log, m_dn_dt_bias, m_dn_norm_w, m_mix_w_out, m_conv_w_pw1, m_conv_b_pw1, m_conv_w_dw, m_conv_b_dw, m_conv_ln_w, m_conv_ln_b, m_conv_w_pw2, m_conv_b_pw2, m_final_norm_w, v_norm_w, v_ffn_w_gate, v_ffn_w_up, v_ffn_w_down, v_mix_w_in, v_dn_conv_w, v_attn_sinks, v_dn_a_log, v_dn_dt_bias, v_dn_norm_w, v_mix_w_out, v_conv_w_pw1, v_conv_b_pw1, v_conv_w_dw, v_conv_b_dw, v_conv_ln_w, v_conv_ln_b, v_conv_w_pw2, v_conv_b_pw2, v_final_norm_w):
    small_names = ["norm_w", "dn_conv_w", "conv_b_pw1", "conv_w_dw", "conv_b_dw", "conv_ln_w", "conv_ln_b",
                   "conv_b_pw2"]
    rep_names = ["attn_sinks", "dn_a_log", "dn_dt_bias", "dn_norm_w", "final_norm_w"]
    w = dict(norm_w=norm_w, ffn_w_gate=ffn_w_gate, ffn_w_up=ffn_w_up, ffn_w_down=ffn_w_down, mix_w_in=mix_w_in, dn_conv_w=dn_conv_w, attn_sinks=attn_sinks, dn_a_log=dn_a_log, dn_dt_bias=dn_dt_bias, dn_norm_w=dn_norm_w, mix_w_out=mix_w_out, conv_w_pw1=conv_w_pw1, conv_b_pw1=conv_b_pw1, conv_w_dw=conv_w_dw, conv_b_dw=conv_b_dw, conv_ln_w=conv_ln_w, conv_ln_b=conv_ln_b, conv_w_pw2=conv_w_pw2, conv_b_pw2=conv_b_pw2, final_norm_w=final_norm_w)
    m = dict(norm_w=m_norm_w, ffn_w_gate=m_ffn_w_gate, ffn_w_up=m_ffn_w_up, ffn_w_down=m_ffn_w_down, mix_w_in=m_mix_w_in, dn_conv_w=m_dn_conv_w, attn_sinks=m_attn_sinks, dn_a_log=m_dn_a_log, dn_dt_bias=m_dn_dt_bias, dn_norm_w=m_dn_norm_w, mix_w_out=m_mix_w_out, conv_w_pw1=m_conv_w_pw1, conv_b_pw1=m_conv_b_pw1, conv_w_dw=m_conv_w_dw, conv_b_dw=m_conv_b_dw, conv_ln_w=m_conv_ln_w, conv_ln_b=m_conv_ln_b, conv_w_pw2=m_conv_w_pw2, conv_b_pw2=m_conv_b_pw2, final_norm_w=m_final_norm_w)
    v = dict(norm_w=v_norm_w, ffn_w_gate=v_ffn_w_gate, ffn_w_up=v_ffn_w_up, ffn_w_down=v_ffn_w_down, mix_w_in=v_mix_w_in, dn_conv_w=v_dn_conv_w, attn_sinks=v_attn_sinks, dn_a_log=v_dn_a_log, dn_dt_bias=v_dn_dt_bias, dn_norm_w=v_dn_norm_w, mix_w_out=v_mix_w_out, conv_w_pw1=v_conv_w_pw1, conv_b_pw1=v_conv_b_pw1, conv_w_dw=v_conv_w_dw, conv_b_dw=v_conv_b_dw, conv_ln_w=v_conv_ln_w, conv_ln_b=v_conv_ln_b, conv_w_pw2=v_conv_w_pw2, conv_b_pw2=v_conv_b_pw2, final_norm_w=v_final_norm_w)
    order = ["norm_w", "ffn_w_gate", "ffn_w_up", "ffn_w_down", "mix_w_in", "dn_conv_w", "attn_sinks", "dn_a_log",
             "dn_dt_bias", "dn_norm_w", "mix_w_out", "conv_w_pw1", "conv_b_pw1", "conv_w_dw", "conv_b_dw",
             "conv_ln_w", "conv_ln_b", "conv_w_pw2", "conv_b_pw2", "final_norm_w"]

    small_shapes = [w[n].shape for n in small_names]
    rep_shapes = [w[n].shape for n in rep_names]

    def halves(a):
        return a.reshape(a.shape[:-2] + (2, a.shape[-2] // 2, a.shape[-1]))

    tr = lambda a: jnp.swapaxes(a, -1, -2)
    gate_t, up_t = tr(ffn_w_gate), tr(ffn_w_up)

    def layer_shards(l):
        mix_in, mix_out = (mix_w_in, mix_w_out) if l % 2 == 0 else (conv_w_pw1, conv_w_pw2)
        ffn = jnp.concatenate([gate_t[l], up_t[l], ffn_w_down[l]], axis=1)
        return ([t.astype(BF16) for t in (halves(ffn[0]), halves(mix_in[l // 2]), halves(mix_out[l // 2]))],
                [halves(ffn[1]).astype(BF16)])

    first = layer_shards(0)
    first = (first[0] + [_pack([w[n] for n in small_names], SMALL_ROWS)], first[1])
    first, (gate_t, up_t, ffn_w_down, mix_w_in, mix_w_out, conv_w_pw1, conv_w_pw2) = lax.optimization_barrier(
        (first, (gate_t, up_t, ffn_w_down, mix_w_in, mix_w_out, conv_w_pw1, conv_w_pw2)))
    gathering = [(_gather_async("gather_layer0a", first[0][:3], first[0][3:]),
                  _gather_async("gather_layer0b", first[1]))]
    for l in range(1, DEPTH):
        before, after = layer_shards(l)
        gathering.append((_gather_async(f"gather_layer{l}a", before), _gather_async(f"gather_layer{l}b", after)))
    ffn_block = lambda g: g.reshape(NCHIP, 1, 3 * FS, D)

    def mixer_params(l, w_a, w_b):
        e = l // 2
        w_a = w_a.reshape(NCHIP, D, -1)
        w_b = w_b.reshape(D, D)
        if l % 2 == 0:
            return dict(w_in=w_a, dn_conv_w=sm["dn_conv_w"][e], sinks=_row(attn_sinks[e]), a_log=_row(dn_a_log[e]),
                        dt_bias=_row(dn_dt_bias[e]), dn_norm_w=_row(dn_norm_w[e]), wo_a=w_b[:Q_A], wo_b=w_b[Q_A:])
        return dict(b1a=_row(sm["conv_b_pw1"][e, :D]), b1b=_row(sm["conv_b_pw1"][e, D:]), w1=w_a,
                    w_dw=sm["conv_w_dw"][e], b_dw=_row(sm["conv_b_dw"][e]), ln_w=_row(sm["conv_ln_w"][e]),
                    ln_b=_row(sm["conv_ln_b"][e]), b2=_row(sm["conv_b_pw2"][e]), w2=w_b)

    xs, saved = x[0], []
    for l in range(DEPTH):
        got = [r[...] for r in gathering[l][0]]
        if l == 0:
            per_chip = [_unpack(got[3][q], small_shapes) for q in range(NCHIP)]
            sm = {n: jnp.concatenate([per_chip[q][i] for q in range(NCHIP)], axis=-1)
                  for i, n in enumerate(small_names)}
        else:
            xs, got = lax.optimization_barrier((xs, got))

        def second_ffn(x2, l=l):
            x2, got_b = lax.optimization_barrier((x2, gathering[l][1][0][...]))
            return x2, ffn_block(got_b)

        xs, sv = _layer_fwd(l, xs, sm["norm_w"][l], ffn_block(got[0]), second_ffn, mixer_params(l, got[1], got[2]))
        saved.append(sv)
    loss, dx, dfw = _final("final", xs, _row(final_norm_w), loss_target[0])

    hbm = pltpu.MemorySpace.HBM
    row_shapes = dict(ffn=(3 * FS, D), w_in=(D // 2, IN_COLS // NCHIP), w_out=(D // 8, D), pw1=(D // 2, D // 2),
                      pw2=(D // 8, D))
    new_sums = lambda k, n: jax.empty_ref(_sds((2, NCHIP, n) + row_shapes[k], BF16), memory_space=hbm)
    sums_0 = {k: new_sums(k, 1) for k in ("ffn", "w_in", "w_out")}
    sums = dict(ffn=new_sums("ffn", DEPTH - 1), w_in=new_sums("w_in", 1), w_out=new_sums("w_out", 1),
                pw1=new_sums("pw1", 2), pw2=new_sums("pw2", 2))
    c_arr = lax.axis_index("c").astype(jnp.int32).reshape(1)
    dnorm, gmix = [None] * DEPTH, [None] * DEPTH

    def hand_on(l, grads, swapped):
        def run(dx):
            dx, other = lax.optimization_barrier((dx, [r[...] for r in swapped]))
            parts = [_add_half(f"add_half_{l}_{k}", gg, rr, c_arr) for k, (gg, rr) in enumerate(zip(grads, other))]
            dx, parts = lax.optimization_barrier((dx, parts))
            keys = ("ffn", "w_in", "w_out") if l % 2 == 0 else ("ffn", "pw1", "pw2")
            if l == 0:
                _scatter_async("scatter_grads_0", parts, [sums_0[k] for k in keys], [0, 0, 0])
            else:
                _scatter_async(f"scatter_grads_{l}", parts, [sums[k] for k in keys],
                               [l - 1, 0, 0] if l % 2 == 0 else [l - 1, l // 2, l // 2])
            return dx
        return run

    pending = lambda dx: dx
    for l in reversed(range(DEPTH)):
        dx, dnorm[l], dffn, gmix[l] = _layer_bwd(l, dx, sm["norm_w"][l], saved[l], pending)
        if l % 2 == 0:
            g_a, g_b = gmix[l]["w_in"], jnp.concatenate([gmix[l]["wo_a"], gmix[l]["wo_b"]], axis=0)
        else:
            g_a, g_b = gmix[l]["w1"], gmix[l]["w2"]
        g_a = halves(g_a).astype(BF16)
        g_b = g_b.reshape(NCHIP, 2, D // 8, D).astype(BF16)
        dx, grads = lax.optimization_barrier((dx, [dffn, g_a, g_b]))
        pending = hand_on(l, grads, _swap_halves(f"swap_grads_{l}", grads, sums["ffn"] if l < DEPTH - 1 else None))
    gm, gc = [gmix[0], gmix[2]], [gmix[1], gmix[3]]
    small_g = dict(
        norm_w=jnp.stack(dnorm), dn_conv_w=jnp.stack([gm[e]["dn_conv_w"] for e in range(2)]),
        conv_b_pw1=jnp.stack([jnp.concatenate([gc[e]["b1a"], gc[e]["b1b"]], axis=1)[0] for e in range(2)]),
        conv_w_dw=jnp.stack([gc[e]["w_dw"] for e in range(2)]),
        conv_b_dw=jnp.stack([gc[e]["b_dw"][0] for e in range(2)]),
        conv_ln_w=jnp.stack([gc[e]["ln_w"][0] for e in range(2)]),
        conv_ln_b=jnp.stack([gc[e]["ln_b"][0] for e in range(2)]),
        conv_b_pw2=jnp.stack([gc[e]["b2"][0] for e in range(2)]))
    small_by_chip = jnp.stack([_pack([jnp.split(small_g[n], NCHIP, axis=-1)[q] for n in small_names], SMALL_ROWS)
                               for q in range(NCHIP)])
    rep_g = _pack([jnp.stack([gm[e]["sinks"][0] for e in range(2)]), jnp.stack([gm[e]["a_log"][0] for e in range(2)]),
                   jnp.stack([gm[e]["dt_bias"][0] for e in range(2)]),
                   jnp.stack([gm[e]["dn_norm_w"][0] for e in range(2)]), dfw[0]], REP_ROWS)
    small_sum, rep_sum = _exchange_small(small_by_chip, rep_g)
    dx, small_sum, rep_sum = lax.optimization_barrier((dx, small_sum, rep_sum))
    dx = pending(dx)

    big = (("ffn_w_gate", "ffn", 0), ("ffn_w_up", "ffn", FS), ("ffn_w_down", "ffn", 2 * FS), ("mix_w_in", "w_in", 0),
           ("mix_w_out", "w_out", 0), ("conv_w_pw1", "pw1", 0), ("conv_w_pw2", "pw2", 0))
    views = {n: (tr, tr) if n in ("ffn_w_gate", "ffn_w_up") else (
        (lambda a: a) if w[n].ndim == 4 else halves, lambda o, n=n: o.reshape(w[n].shape)) for n, _, _ in big}
    partial_sums = {k: r[...] for k, r in sums.items()}
    upper = {}
    for n, key, row0 in big:
        view = views[n][0]
        upper[n] = _adamw_big(f"adamw_{n}", view(w[n]), view(m[n]), view(v[n]), partial_sums[key], row0,
                              first=0 if key in ("pw1", "pw2") else 1)
    upper, partial_sums_0 = lax.optimization_barrier((upper, {k: r[...] for k, r in sums_0.items()}))
    res = {}
    for n, key, row0 in big:
        view, back = views[n]
        outs = upper[n] if key not in partial_sums_0 else _adamw_big(
            f"adamw_{n}_0", view(w[n]), view(m[n]), view(v[n]), partial_sums_0[key], row0, first=0, outs=upper[n])
        res[n] = [back(o) for o in outs]
    outs = _adamw_small("adamw_small", *[_pack([d[n] for n in small_names], SMALL_ROWS) for d in (w, m, v)],
                        small_sum)
    for i, n in enumerate(small_names):
        res[n] = [_unpack(o, small_shapes)[i] for o in outs]
    outs = _adamw_small("adamw_replicated", *[_pack([d[n] for n in rep_names], REP_ROWS) for d in (w, m, v)],
                        rep_sum)
    for i, n in enumerate(rep_names):
        res[n] = [_unpack(o, rep_shapes)[i] for o in outs]

    total = lax.psum(loss[0, 0], ("x", "y", "c"))
    return (total, dx[None], *[res[n][0] for n in order], *[res[n][1] for n in order],
            *[res[n][2] for n in order], *[res[n][3] for n in order])
```

```python
import jax
import jax.numpy as jnp
from jax import lax
from jax.experimental import pallas as pl
from jax.experimental.pallas import tpu as pltpu
from jax.experimental.pallas import tpu_sc as plsc

F32, BF16 = jnp.float32, jnp.bfloat16
MESH = pl.DeviceIdType.MESH
ANY = pl.BlockSpec(memory_space=pl.ANY)

T, D, F = 2048, 1024, 2816
DEPTH = 4
EPS = 1e-6
HEADS, HDIM, KV_HEADS, GROUP = 8, 64, 2, 4
WINDOW = BLOCK = 128
CHUNK = 64
NCHUNK = T // CHUNK
DN_CONV, CONV_WIDTH = 4, 31
Q_A, KV_A, QKV_B, V_B = 512, 128, 1536, 512
IN_COLS = 2832
IN_SPLITS = (0, 512, 640, 768, 2304, 2816, 2832)
NCHIP, NDEV = 4, 8
FS = F // NCHIP
LR, B1, B2, AEPS, WD, STEP = 0.001, 0.9, 0.999, 1e-08, 0.01, 10
V7X_VMEM_BYTES = 64 * 1024 * 1024
VMEM_LIMIT = V7X_VMEM_BYTES * 7 // 8
LANES = 128


def _cp(*sem):
    return pltpu.CompilerParams(dimension_semantics=sem, vmem_limit_bytes=VMEM_LIMIT)


def _sds(shape, dtype=F32):
    return jax.ShapeDtypeStruct(tuple(shape), dtype)


def _full(shape):
    nd = len(shape)
    return pl.BlockSpec(tuple(shape), lambda *_: (0,) * nd)


def _split_bf16(a):
    hi = a.astype(BF16)
    return hi, (a - hi.astype(F32)).astype(BF16)


def _dg(a, b, ca, cb, hi=False):
    if a.ndim == 3 and b.ndim == 3:
        dims = (((ca + 1,), (cb + 1,)), ((0,), (0,)))
    else:
        dims = (((ca,), (cb,)), ((), ()))
    dot = lambda p, q: lax.dot_general(p, q, dims, preferred_element_type=F32)
    if hi:
        a_hi, a_lo = _split_bf16(a.astype(F32))
        b_hi, b_lo = _split_bf16(b.astype(F32))
        return dot(a_hi, b_hi) + (dot(a_hi, b_lo) + dot(a_lo, b_hi))
    return dot(a.astype(BF16), b.astype(BF16))


def _make_mm(hi):
    @jax.custom_vjp
    def nn(a, b):
        return _dg(a, b, 1, 0, hi)

    @jax.custom_vjp
    def nt(a, b):
        return _dg(a, b, 1, 1, hi)

    @jax.custom_vjp
    def tn(a, b):
        return _dg(a, b, 0, 0, hi)

    nn.defvjp(lambda a, b: (_dg(a, b, 1, 0, hi), (a, b)),
              lambda r, g: (_dg(g, r[1], 1, 1, hi).astype(r[0].dtype), _dg(r[0], g, 0, 0, hi).astype(r[1].dtype)))
    nt.defvjp(lambda a, b: (_dg(a, b, 1, 1, hi), (a, b)),
              lambda r, g: (_dg(g, r[1], 1, 0, hi).astype(r[0].dtype), _dg(g, r[0], 0, 0, hi).astype(r[1].dtype)))
    tn.defvjp(lambda a, b: (_dg(a, b, 0, 0, hi), (a, b)),
              lambda r, g: (_dg(r[1], g, 1, 1, hi).astype(r[0].dtype), _dg(r[0], g, 1, 0, hi).astype(r[1].dtype)))
    return nn, nt, tn


_nn, _nt, _tn = _make_mm(False)
_nn_hi, _nt_hi, _tn_hi = _make_mm(True)


def _rms(x, w):
    return x * lax.rsqrt(jnp.mean(x * x, axis=-1, keepdims=True) + EPS) * w


def _layernorm(x, w, b):
    xc = x - jnp.mean(x, axis=-1, keepdims=True)
    return xc * lax.rsqrt(jnp.mean(xc * xc, axis=-1, keepdims=True) + EPS) * w + b


def _silu(x):
    return x * jax.nn.sigmoid(x)


def _iota2(shape, dim):
    return lax.broadcasted_iota(jnp.int32, shape, dim)


def _flat_weights(lhs_idx, weights):
    specs, ops, lhs_of, where = [], [], [], []
    for a, (k, w) in enumerate(zip(lhs_idx, weights)):
        for q in range(1 if w.ndim == 2 else w.shape[0]):
            specs.append(_full(w.shape) if w.ndim == 2
                         else pl.BlockSpec((None,) + w.shape[1:], lambda i, q=q: (q, 0, 0)))
            ops.append(w)
            lhs_of.append(k)
            where.append((a, None if w.ndim == 2 else q))
    return specs, ops, lhs_of, where


def _blk_fwd(name, pre, lhs_idx, post, toks, smalls, weights, outs, tm=512):
    wspecs, wops, lhs_of, _ = _flat_weights(lhs_idx, weights)
    nt_, ns, nw = len(toks), len(smalls), len(wops)

    def body(*refs):
        tv = [r[...] for r in refs[:nt_]]
        sv = [r[...] for r in refs[nt_:nt_ + ns]]
        wr = refs[nt_ + ns:nt_ + ns + nw]
        orf = refs[nt_ + ns + nw:]
        lhs = pre(tv, sv)
        ys = [_dg(lhs[i], w[...], 1, 0) for i, w in zip(lhs_of, wr)]
        for o_ref, o in zip(orf, post(ys, tv, sv)):
            o_ref[...] = o.astype(o_ref.dtype)

    in_specs = ([pl.BlockSpec((tm, a.shape[1]), lambda i: (i, 0)) for a in toks]
                + [_full(a.shape) for a in smalls] + wspecs)
    out_specs = [pl.BlockSpec((tm, w_), lambda i: (i, 0)) for w_, _ in outs]
    return pl.pallas_call(
        body, grid=(T // tm,), in_specs=in_specs, out_specs=out_specs,
        out_shape=[_sds((T, w_), dt) for w_, dt in outs], name=name, compiler_params=_cp("parallel"),
    )(*toks, *smalls, *wops)


def _blk_bwd(name, pre, lhs_idx, post, toks, smalls, weights, ct_groups, res=None, linear_post=False, tm=256,
             wchunk=512):
    wspecs, wops, lhs_of, where = _flat_weights(lhs_idx, weights)
    nt_, ns, nw, na = len(toks), len(smalls), len(wops), len(weights)
    cts = [a for g in ct_groups for a in g]
    nc = len(cts)
    widths = [sum(a.shape[1] for a in g) for g in ct_groups]
    has_res = res is not None

    def body(*refs):
        p = 0
        tr = refs[p:p + nt_]; p += nt_
        sr = refs[p:p + ns]; p += ns
        wr = refs[p:p + nw]; p += nw
        cr = refs[p:p + nc]; p += nc
        rr = refs[p:p + has_res]; p += has_res
        dtr = refs[p:p + nt_]; p += nt_
        dsr = refs[p:p + ns]; p += ns
        dwr = refs[p:p + na]; p += na
        scr = refs[p:]
        i = pl.program_id(0)

        @pl.when(i == 0)
        def _():
            for r in list(dsr) + list(dwr):
                r[...] = jnp.zeros_like(r)

        tv = [r[...] for r in tr]
        sv = [r[...] for r in sr]
        ctv, q, si = [], 0, 0
        for g in ct_groups:
            if len(g) == 1:
                ctv.append(cr[q][...].astype(F32))
            else:
                off = 0
                for j, a in enumerate(g):
                    scr[si][:, off:off + a.shape[1]] = cr[q + j][...].astype(F32)
                    off += a.shape[1]
                ctv.append(scr[si][...])
                si += 1
            q += len(g)

        lhs, vjp_pre = jax.vjp(lambda *a: tuple(pre(list(a[:nt_]), list(a[nt_:]))), *tv, *sv)
        lhs_b = [l.astype(BF16) for l in lhs]
        ys = [jnp.zeros((tm, w.shape[1]), F32) if linear_post else _dg(lhs_b[k], w[...], 1, 0)
              for k, w in zip(lhs_of, wr)]
        _, vjp_post = jax.vjp(lambda *a: tuple(post(list(a[:nw]), list(a[nw:nw + nt_]), list(a[nw + nt_:]))),
                              *ys, *tv, *sv)
        gp = vjp_post(tuple(ctv))
        dys, dt_post, ds_post = gp[:nw], gp[nw:nw + nt_], gp[nw + nt_:]
        dlhs = [None] * len(lhs)
        for k, w, dy, (a, q) in zip(lhs_of, wr, dys, where):
            dyb = dy.astype(BF16)
            n = w.shape[1]
            for c0 in range(0, n, wchunk):
                c1 = min(n, c0 + wchunk)
                part = _dg(lhs_b[k], dyb[:, c0:c1], 0, 0)
                if q is None:
                    dwr[a][:, c0:c1] += part
                else:
                    dwr[a][q, :, c0:c1] += part
            d = _dg(dyb, w[...], 1, 1)
            dlhs[k] = d if dlhs[k] is None else dlhs[k] + d
        gq = vjp_pre(tuple(d.astype(l.dtype) for d, l in zip(dlhs, lhs)))
        dt_pre, ds_pre = gq[:nt_], gq[nt_:]
        for j in range(nt_):
            d = dt_post[j] + dt_pre[j]
            if j == 0 and has_res:
                d = d + rr[0][...]
            dtr[j][...] = d
        for j in range(ns):
            dsr[j][...] += ds_post[j] + ds_pre[j]

    tok_spec = lambda a: pl.BlockSpec((tm, a.shape[1]), lambda i: (i, 0))
    in_specs = ([tok_spec(a) for a in toks] + [_full(a.shape) for a in smalls] + wspecs
                + [tok_spec(a) for a in cts] + ([tok_spec(res)] if has_res else []))
    out_specs = [tok_spec(a) for a in toks] + [_full(a.shape) for a in smalls] + [_full(w.shape) for w in weights]
    out_shape = ([_sds(a.shape) for a in toks] + [_sds(a.shape) for a in smalls] + [_sds(w.shape) for w in weights])
    scratch = [pltpu.VMEM((tm, wd), F32) for g, wd in zip(ct_groups, widths) if len(g) > 1]
    outs = pl.pallas_call(
        body, grid=(T // tm,), in_specs=in_specs, out_specs=out_specs, out_shape=out_shape,
        scratch_shapes=scratch, name=name, compiler_params=_cp("arbitrary"),
    )(*toks, *smalls, *wops, *cts, *([res] if has_res else []))
    return outs[:nt_], outs[nt_:nt_ + ns], outs[nt_ + ns:]


def _ffn_fwd(name, x, nw, ffn, idx, tm=1024):
    def body(x_ref, nw_ref, wg_ref, wu_ref, wd_ref, o_ref, g_ref, da_ref, db_ref, h_ref):
        s = pl.program_id(1)

        @pl.when(s == 0)
        def _():
            xv = x_ref[...]
            h_ref[...] = _rms(xv, nw_ref[...]).astype(BF16)
            o_ref[...] = xv

        h = h_ref[...]
        a = _dg(h, wg_ref[...], 1, 1)
        b = _dg(h, wu_ref[...], 1, 1)
        sa = jax.nn.sigmoid(a)
        act = a * sa
        gated = (act * b).astype(BF16)
        g_ref[...] = gated
        da_ref[...] = (b * (sa * (1.0 + a * (1.0 - sa)))).astype(BF16)
        db_ref[...] = act.astype(BF16)
        o_ref[...] += 0.5 * _dg(gated, wd_ref[...], 1, 0)

    wspec = lambda k: pl.BlockSpec((None, None, FS, D), lambda i, s: (s, idx, k, 0))
    act = pl.BlockSpec((None, tm, FS), lambda i, s: (s, i, 0))
    return pl.pallas_call(
        body, grid=(T // tm, NCHIP),
        in_specs=[pl.BlockSpec((tm, D), lambda i, s: (i, 0)), _full((1, D)), wspec(0), wspec(1), wspec(2)],
        out_specs=[pl.BlockSpec((tm, D), lambda i, s: (i, 0)), act, act, act,
                   pl.BlockSpec((tm, D), lambda i, s: (i, 0))],
        out_shape=[_sds((T, D))] + [_sds((NCHIP, T, FS), BF16)] * 3 + [_sds((T, D), BF16)],
        name=name, compiler_params=_cp("parallel", "arbitrary"),
    )(x, nw, ffn, ffn, ffn)


def _ffn_bwd(name, x, nw, ffn, idx, pre, dy, gbuf=None, tm=512):
    ni = T // tm

    def body(x_ref, dy_ref, nw_ref, wg_ref, wu_ref, wd_ref, g_ref, fa_ref, fb_ref, h_ref, dx_ref, dnw_ref, dffn_ref,
             dh_acc, ag, au, ad):
        s, i = pl.program_id(0), pl.program_id(1)
        rows = pl.ds(pl.multiple_of(i * tm, tm), tm)

        @pl.when((s == 0) & (i == 0))
        def _():
            dnw_ref[...] = jnp.zeros_like(dnw_ref)

        @pl.when(i == 0)
        def _():
            ag[...] = jnp.zeros_like(ag)
            au[...] = jnp.zeros_like(au)
            ad[...] = jnp.zeros_like(ad)

        hb = h_ref[...]
        dyb = (0.5 * dy_ref[...]).astype(BF16)
        ad[...] += _dg(g_ref[...], dyb, 0, 0)
        dact = _dg(dyb, wd_ref[...], 1, 1)
        da = (dact * fa_ref[...].astype(F32)).astype(BF16)
        db = (dact * fb_ref[...].astype(F32)).astype(BF16)
        ag[...] += _dg(da, hb, 0, 0)
        au[...] += _dg(db, hb, 0, 0)
        dh = _dg(da, wg_ref[...], 1, 0) + _dg(db, wu_ref[...], 1, 0)

        @pl.when(s == 0)
        def _():
            dh_acc[rows, :] = dh

        @pl.when((s > 0) & (s < NCHIP - 1))
        def _():
            dh_acc[rows, :] += dh

        @pl.when(s == NCHIP - 1)
        def _():
            _, vjp_rms = jax.vjp(_rms, x_ref[...], nw_ref[...])
            dx, dnw = vjp_rms(dh_acc[rows, :] + dh)
            dx_ref[...] = dy_ref[...] + dx
            dnw_ref[...] += dnw

        @pl.when(i == ni - 1)
        def _():
            dffn_ref[0:FS, :] = ag[...].astype(BF16)
            dffn_ref[FS:2 * FS, :] = au[...].astype(BF16)
            dffn_ref[2 * FS:, :] = ad[...].astype(BF16)

    wspec = lambda r, k, blk=0: pl.BlockSpec((None, None, r, D), lambda s, i: (s, blk, k, 0),
                                             pipeline_mode=pl.Buffered(1))
    last = lambda s, i: (jnp.where(s == NCHIP - 1, i, 0), 0)
    nb = 0 if gbuf is None else 1
    act = pl.BlockSpec((None, tm, FS), lambda s, i: (s, i, 0))
    tok = pl.BlockSpec((tm, D), lambda s, i: (i, 0))
    return pl.pallas_call(
        lambda *refs: body(*refs[:10], *refs[10 + nb:]), grid=(NCHIP, ni),
        in_specs=[pl.BlockSpec((tm, D), last), tok, _full((1, D)), wspec(FS, 0), wspec(FS, 1), wspec(FS, 2), act, act,
                  act, tok] + [ANY] * nb,
        out_specs=[pl.BlockSpec((tm, D), last), _full((1, D)), wspec(3 * FS, 0, idx)],
        out_shape=[_sds((T, D)), _sds((1, D)), _sds((NCHIP, 2, 3 * FS, D), BF16)],
        input_output_aliases={10 + k: 2 + k for k in range(nb)},
        scratch_shapes=[pltpu.VMEM((T, D), F32)] + [pltpu.VMEM((FS, D), F32)] * 3,
        name=name, compiler_params=_cp("arbitrary", "arbitrary"),
    )(x, dy, nw, ffn, ffn, ffn, *pre, *(() if gbuf is None else (gbuf,)))


CONV_ROWS = 256


def _conv_pad(k):
    return 8 * ((k - 1 + 7) // 8)


def _shifted(win, o):
    n = win.shape[0]
    return (win if o % n == 0 else pltpu.roll(win, (n - o) % n, 0))[0:CONV_ROWS, :]


def _conv_fwd(name, x, w, b, act):
    k_w, c = w.shape
    tc = 256 if c % 256 == 0 else LANES
    pad = _conv_pad(k_w)
    has_b = b is not None

    def body(*refs):
        x_ref, w_ref = refs[0], refs[1]
        b_ref = refs[2] if has_b else None
        y_ref, xp = refs[2 + has_b], refs[3 + has_b]
        xp[0:pad, :] = jnp.zeros((pad, tc), F32)
        xp[pad:, :] = x_ref[...]

        def step(t, carry):
            base = pl.multiple_of(t * CONV_ROWS, CONV_ROWS)
            win = xp[pl.ds(base, CONV_ROWS + pad), :]
            acc = jnp.zeros((CONV_ROWS, tc), F32)
            for k in range(k_w):
                o = pad - (k_w - 1) + k
                acc = acc + w_ref[k:k + 1, :] * _shifted(win, o)
            if has_b:
                acc = acc + b_ref[...]
            y_ref[pl.ds(base, CONV_ROWS), :] = _silu(acc) if act else acc
            return carry

        lax.fori_loop(0, T // CONV_ROWS, step, 0)

    col = lambda r: pl.BlockSpec((r, tc), lambda j: (0, j))
    ins = [x, w] + ([b] if has_b else [])
    return pl.pallas_call(
        body, grid=(c // tc,), in_specs=[col(T), col(k_w)] + ([col(1)] if has_b else []), out_specs=col(T),
        out_shape=_sds((T, c)), scratch_shapes=[pltpu.VMEM((T + pad, tc), F32)], name=name,
        compiler_params=_cp("parallel"),
    )(*ins)


def _conv_bwd(name, x, w, b, act, dy):
    k_w, c = w.shape
    tc = 256 if c % 256 == 0 else LANES
    pad = _conv_pad(k_w)
    has_b = b is not None

    def body(*refs):
        x_ref, w_ref, dy_ref = refs[0], refs[1], refs[2]
        b_ref = refs[3] if has_b else None
        dx_ref, dw_ref, db_ref, xp, dp = refs[3 + has_b:]
        xp[0:pad, :] = jnp.zeros((pad, tc), F32)
        xp[pad:, :] = x_ref[...]
        dp[T:, :] = jnp.zeros((pad, tc), F32)
        dw_ref[...] = jnp.zeros_like(dw_ref)
        db_ref[...] = jnp.zeros_like(db_ref)

        def step1(t, carry):
            base = pl.multiple_of(t * CONV_ROWS, CONV_ROWS)
            d = dy_ref[pl.ds(base, CONV_ROWS), :]
            win = xp[pl.ds(base, CONV_ROWS + pad), :]
            offs = [pad - (k_w - 1) + k for k in range(k_w)]
            if act:
                acc = jnp.zeros((CONV_ROWS, tc), F32)
                for k, o in enumerate(offs):
                    acc = acc + w_ref[k:k + 1, :] * _shifted(win, o)
                if has_b:
                    acc = acc + b_ref[...]
                sg = jax.nn.sigmoid(acc)
                d = d * (sg * (1.0 + acc * (1.0 - sg)))
            dp[pl.ds(base, CONV_ROWS), :] = d
            for k, o in enumerate(offs):
                dw_ref[k:k + 1, :] += jnp.sum(d * _shifted(win, o), axis=0, keepdims=True)
            db_ref[...] += jnp.sum(d, axis=0, keepdims=True)
            return carry

        lax.fori_loop(0, T // CONV_ROWS, step1, 0)

        def step2(t, carry):
            base = pl.multiple_of(t * CONV_ROWS, CONV_ROWS)
            win = dp[pl.ds(base, CONV_ROWS + pad), :]
            acc = jnp.zeros((CONV_ROWS, tc), F32)
            for k in range(k_w):
                o = (k_w - 1) - k
                acc = acc + w_ref[k:k + 1, :] * _shifted(win, o)
            dx_ref[pl.ds(base, CONV_ROWS), :] = acc
            return carry

        lax.fori_loop(0, T // CONV_ROWS, step2, 0)

    col = lambda r: pl.BlockSpec((r, tc), lambda j: (0, j))
    ins = [x, w, dy] + ([b] if has_b else [])
    return pl.pallas_call(
        body, grid=(c // tc,), in_specs=[col(T), col(k_w), col(T)] + ([col(1)] if has_b else []),
        out_specs=[col(T), col(k_w), col(1)], out_shape=[_sds((T, c)), _sds((k_w, c)), _sds((1, c))],
        scratch_shapes=[pltpu.VMEM((T + pad, tc), F32), pltpu.VMEM((T + pad, tc), F32)], name=name,
        compiler_params=_cp("parallel"),
    )(*ins)


def _attn_consts(n):
    i = _iota2((BLOCK, 2 * BLOCK), 0)
    j = _iota2((BLOCK, 2 * BLOCK), 1)
    dist = i + BLOCK - j
    valid = (dist >= 0) & (dist < WINDOW) & ((n > 0) | (j >= BLOCK))
    return dist.astype(F32), valid


def _attn_block(q4, kk, vv, sinks, dist, valid, kv):
    outs = []
    lane = _iota2((1, HEADS), 1)
    for g in range(GROUP):
        h = kv * GROUP + g
        slope = 2.0 ** (-8.0 * (h + 1) / HEADS)
        s = _nt(q4[:, g * HDIM:(g + 1) * HDIM], kk) * (HDIM ** -0.5)
        s = jnp.where(valid, s - slope * dist, -1e30)
        sink = jnp.sum(jnp.where(lane == h, sinks, 0.0), axis=1, keepdims=True)
        m = jnp.maximum(jnp.max(s, axis=-1, keepdims=True), sink)
        e = jnp.exp(s - m)
        p = e / (jnp.sum(e, axis=-1, keepdims=True) + jnp.exp(sink - m))
        outs.append(_nn(p, vv))
    return tuple(outs)


def _attn_fwd(name, qa, ka, va, sinks):
    def body(q_ref, k_ref, v_ref, s_ref, o_ref, kp, vp):
        kp[0:BLOCK, :] = jnp.zeros((BLOCK, KV_A), F32)
        vp[0:BLOCK, :] = jnp.zeros((BLOCK, KV_A), F32)
        kp[BLOCK:, :] = k_ref[...]
        vp[BLOCK:, :] = v_ref[...]
        sinks_v = s_ref[...]

        def step(n, carry):
            r = pl.multiple_of(n * BLOCK, BLOCK)
            dist, valid = _attn_consts(n)
            k2 = kp[pl.ds(r, 2 * BLOCK), :]
            v2 = vp[pl.ds(r, 2 * BLOCK), :]
            for kv in range(KV_HEADS):
                q4 = q_ref[pl.ds(r, BLOCK), kv * GROUP * HDIM:(kv + 1) * GROUP * HDIM]
                og = _attn_block(q4, k2[:, kv * HDIM:(kv + 1) * HDIM], v2[:, kv * HDIM:(kv + 1) * HDIM], sinks_v,
                                 dist, valid, kv)
                for g in range(GROUP):
                    h = kv * GROUP + g
                    o_ref[pl.ds(r, BLOCK), h * HDIM:(h + 1) * HDIM] = og[g]
            return carry

        lax.fori_loop(0, T // BLOCK, step, 0)

    return pl.pallas_call(
        body, out_shape=_sds((T, Q_A)),
        scratch_shapes=[pltpu.VMEM((T + BLOCK, KV_A), F32), pltpu.VMEM((T + BLOCK, KV_A), F32)], name=name,
        compiler_params=pltpu.CompilerParams(vmem_limit_bytes=VMEM_LIMIT),
    )(qa, ka, va, sinks)


def _attn_bwd(name, qa, ka, va, sinks, do):
    def body(q_ref, k_ref, v_ref, s_ref, do_ref, dq_ref, dk_ref, dv_ref, ds_ref, kp, vp, dkp, dvp):
        kp[0:BLOCK, :] = jnp.zeros((BLOCK, KV_A), F32)
        vp[0:BLOCK, :] = jnp.zeros((BLOCK, KV_A), F32)
        kp[BLOCK:, :] = k_ref[...]
        vp[BLOCK:, :] = v_ref[...]
        dkp[...] = jnp.zeros_like(dkp)
        dvp[...] = jnp.zeros_like(dvp)
        ds_ref[...] = jnp.zeros_like(ds_ref)
        sinks_v = s_ref[...]

        def step(n, carry):
            r = pl.multiple_of(n * BLOCK, BLOCK)
            dist, valid = _attn_consts(n)
            k2 = kp[pl.ds(r, 2 * BLOCK), :]
            v2 = vp[pl.ds(r, 2 * BLOCK), :]
            for kv in range(KV_HEADS):
                cols = slice(kv * HDIM, (kv + 1) * HDIM)
                q4 = q_ref[pl.ds(r, BLOCK), kv * GROUP * HDIM:(kv + 1) * GROUP * HDIM]
                _, vjp = jax.vjp(lambda q, k, v, s: _attn_block(q, k, v, s, dist, valid, kv),
                                 q4, k2[:, cols], v2[:, cols], sinks_v)
                cts = tuple(do_ref[pl.ds(r, BLOCK), (kv * GROUP + g) * HDIM:(kv * GROUP + g + 1) * HDIM]
                            for g in range(GROUP))
                dq4, dkk, dvv, dsk = vjp(cts)
                dq_ref[pl.ds(r, BLOCK), kv * GROUP * HDIM:(kv + 1) * GROUP * HDIM] = dq4
                dkp[pl.ds(r, 2 * BLOCK), cols] += dkk
                dvp[pl.ds(r, 2 * BLOCK), cols] += dvv
                ds_ref[...] += dsk
            return carry

        lax.fori_loop(0, T // BLOCK, step, 0)
        dk_ref[...] = dkp[BLOCK:, :]
        dv_ref[...] = dvp[BLOCK:, :]

    pad = lambda: pltpu.VMEM((T + BLOCK, KV_A), F32)
    return pl.pallas_call(
        body, out_shape=[_sds((T, Q_A)), _sds((T, KV_A)), _sds((T, KV_A)), _sds((1, HEADS))],
        scratch_shapes=[pad(), pad(), pad(), pad()], name=name,
        compiler_params=pltpu.CompilerParams(vmem_limit_bytes=VMEM_LIMIT),
    )(qa, ka, va, sinks, do)


def _dn_consts():
    i = _iota2((CHUNK, CHUNK), 0)
    j = _iota2((CHUNK, CHUNK), 1)
    return dict(causal=i >= j, strict=i > j, eye=(i == j).astype(F32), ltri=(i >= j).astype(F32),
                ones=jnp.ones((CHUNK, CHUNK), F32), last=(_iota2((CHUNK, 1), 0) == CHUNK - 1).astype(F32))


def _l2norm(x):
    return x * lax.rsqrt(jnp.sum(x * x, axis=-1, keepdims=True) + EPS)


def _head_cols(m):
    lane = _iota2((1, HEADS), 1)
    return jnp.concatenate([jnp.sum(jnp.where(lane == h, m, 0.0), axis=1, keepdims=True)[None]
                            for h in range(HEADS)], axis=0)


@jax.custom_vjp
def _unit_lower_inverse(low, known):
    if known is not None:
        return known
    inv = (_iota2((CHUNK, CHUNK), 0) == _iota2((CHUNK, CHUNK), 1)).astype(F32) - low
    pw = low
    for _ in range(5):
        pw = _dg(pw, pw, 1, 0, True)
        inv = inv + _dg(inv, pw, 1, 0, True)
    return inv


def _unit_lower_inverse_fwd(low, known):
    inv = _unit_lower_inverse(low, known)
    return inv, (inv, known)


def _unit_lower_inverse_bwd(res, g):
    inv, known = res
    d_low = -_dg(inv, _dg(g, inv, 1, 1, True), 0, 0, True)
    return d_low, (None if known is None else jnp.zeros_like(known))


_unit_lower_inverse.defvjp(_unit_lower_inverse_fwd, _unit_lower_inverse_bwd)


def _dn_local(q3, k3, v3, braw, araw, alog, dtb, cs, known_inv=None):
    q = _l2norm(q3) * (HDIM ** -0.5)
    k = _l2norm(k3)
    g = -jnp.exp(alog) * jax.nn.softplus(araw + dtb)
    gc_all = _nn_hi(cs["ltri"], g)
    egc_all = jnp.exp(gc_all)
    beta, gc, egc = _head_cols(jax.nn.sigmoid(braw)), _head_cols(gc_all), _head_cols(egc_all)
    a = jnp.broadcast_to(gc, (HEADS, CHUNK, CHUNK))
    diff = a - jnp.swapaxes(a, 1, 2)
    decay = jnp.where(cs["causal"], jnp.exp(jnp.where(cs["causal"], diff, 0.0)), 0.0)
    kb = k * beta
    low = jnp.where(cs["strict"], _nt(kb, k) * decay, 0.0)
    inv = _unit_lower_inverse(low, known_inv)
    u = _nn_hi(inv, v3 * beta)
    w = _nn_hi(inv, kb * egc)
    attn = _nt(q, k) * decay
    gc_last = jnp.sum(gc * cs["last"], axis=1, keepdims=True)
    return u, w, attn, q * egc, k * jnp.exp(gc_last - gc), egc_all, inv


def _heads3(ref, off=0):
    return jnp.concatenate([ref[:, off + h * HDIM:off + (h + 1) * HDIM][None] for h in range(HEADS)], axis=0)


def _dn_local_fwd(name, qkv, ba, alog, dtb):
    def body(qkv_ref, ba_ref, al_ref, dt_ref, u_ref, w_ref, at_ref, qd_ref, kd_ref, eg_ref, inv_ref):
        bav = ba_ref[...]
        outs = _dn_local(_heads3(qkv_ref), _heads3(qkv_ref, 512), _heads3(qkv_ref, 1024), bav[:, :HEADS],
                         bav[:, HEADS:], al_ref[...], dt_ref[...], _dn_consts())
        for r, o in zip((u_ref, w_ref, at_ref, qd_ref, kd_ref, inv_ref), outs[:5] + outs[6:]):
            _unheads(r, o)
        eg_ref[...] = outs[5]

    row = lambda w_: pl.BlockSpec((CHUNK, w_), lambda n: (n, 0))
    return pl.pallas_call(
        body, grid=(NCHUNK,), in_specs=[row(QKV_B), row(2 * HEADS), _full((1, HEADS)), _full((1, HEADS))],
        out_specs=[row(V_B)] * 5 + [row(HEADS), row(V_B)],
        out_shape=[_sds((T, V_B))] * 5 + [_sds((T, HEADS)), _sds((T, V_B))], name=name,
        compiler_params=_cp("parallel"),
    )(qkv, ba, alog, dtb)


def _dn_local_bwd(name, qkv, ba, alog, dtb, inv, cts):
    def body(qkv_ref, ba_ref, al_ref, dt_ref, inv_ref, du_ref, dw_ref, dat_ref, dqd_ref, dkd_ref, deg_ref,
             dqkv_ref, dba_ref, dal_ref, ddt_ref):
        @pl.when(pl.program_id(0) == 0)
        def _():
            dal_ref[...] = jnp.zeros_like(dal_ref)
            ddt_ref[...] = jnp.zeros_like(ddt_ref)

        cs = _dn_consts()
        bav = ba_ref[...]
        known = _heads3(inv_ref)
        _, vjp = jax.vjp(lambda *a: _dn_local(*a, cs, known)[:6], _heads3(qkv_ref), _heads3(qkv_ref, 512),
                         _heads3(qkv_ref, 1024), bav[:, :HEADS], bav[:, HEADS:], al_ref[...], dt_ref[...])
        dq, dk, dv, dbr, dar, dal, ddt = vjp((_heads3(du_ref), _heads3(dw_ref), _heads3(dat_ref), _heads3(dqd_ref),
                                              _heads3(dkd_ref), deg_ref[...]))
        for h in range(HEADS):
            dqkv_ref[:, h * HDIM:(h + 1) * HDIM] = dq[h]
            dqkv_ref[:, 512 + h * HDIM:512 + (h + 1) * HDIM] = dk[h]
            dqkv_ref[:, 1024 + h * HDIM:1024 + (h + 1) * HDIM] = dv[h]
        dba_ref[:, :HEADS] = dbr
        dba_ref[:, HEADS:] = dar
        dal_ref[...] += dal
        ddt_ref[...] += ddt

    row = lambda w_: pl.BlockSpec((CHUNK, w_), lambda n: (n, 0))
    return pl.pallas_call(
        body, grid=(NCHUNK,),
        in_specs=[row(QKV_B), row(2 * HEADS), _full((1, HEADS)), _full((1, HEADS))] + [row(V_B)] * 6 + [row(HEADS)],
        out_specs=[row(QKV_B), row(2 * HEADS), _full((1, HEADS)), _full((1, HEADS))],
        out_shape=[_sds((T, QKV_B)), _sds((T, 2 * HEADS)), _sds((1, HEADS)), _sds((1, HEADS))], name=name,
        compiler_params=_cp("arbitrary"),
    )(qkv, ba, alog, dtb, inv, *cts)


def _dn_step(s, u, w, attn, qd, kd, egc, z, nw):
    last = (_iota2((CHUNK, 1), 0) == CHUNK - 1).astype(F32)
    gl = jnp.sum(_head_cols(egc) * last, axis=1, keepdims=True)
    v_new = u - _nn(w, s)
    o = _nn(qd, s) + _nn(attn, v_new)
    s_new = s * gl + _tn(kd, v_new)
    return s_new, _rms(o, nw) * _silu(z)


def _unheads(ref, v3):
    for h in range(HEADS):
        ref[:, h * HDIM:(h + 1) * HDIM] = v3[h]


def _dn_rec_fwd(name, u, w, attn, qd, kd, egc, z, nw):
    def body(u_ref, w_ref, at_ref, qd_ref, kd_ref, eg_ref, z_ref, nw_ref, o_ref, ss_ref, s_scr):
        @pl.when(pl.program_id(0) == 0)
        def _():
            s_scr[...] = jnp.zeros_like(s_scr)

        s = s_scr[...]
        ss_ref[...] = s
        s_new, on = _dn_step(s, _heads3(u_ref), _heads3(w_ref), _heads3(at_ref), _heads3(qd_ref), _heads3(kd_ref),
                             eg_ref[...], _heads3(z_ref), nw_ref[...])
        s_scr[...] = s_new
        _unheads(o_ref, on)

    row = lambda w_: pl.BlockSpec((CHUNK, w_), lambda n: (n, 0))
    return pl.pallas_call(
        body, grid=(NCHUNK,), in_specs=[row(V_B)] * 5 + [row(HEADS), row(V_B), _full((1, HDIM))],
        out_specs=[row(V_B), pl.BlockSpec((None, HEADS, HDIM, HDIM), lambda n: (n, 0, 0, 0))],
        out_shape=[_sds((T, V_B)), _sds((NCHUNK, HEADS, HDIM, HDIM))],
        scratch_shapes=[pltpu.VMEM((HEADS, HDIM, HDIM), F32)], name=name, compiler_params=_cp("arbitrary"),
    )(u, w, attn, qd, kd, egc, z, nw)


def _dn_rec_bwd(name, u, w, attn, qd, kd, egc, z, nw, ss, do):
    def body(u_ref, w_ref, at_ref, qd_ref, kd_ref, eg_ref, z_ref, nw_ref, ss_ref, do_ref,
             du_ref, dw_ref, dat_ref, dqd_ref, dkd_ref, deg_ref, dz_ref, dnw_ref, ds_scr):
        @pl.when(pl.program_id(0) == 0)
        def _():
            ds_scr[...] = jnp.zeros_like(ds_scr)
            dnw_ref[...] = jnp.zeros_like(dnw_ref)

        _, vjp = jax.vjp(_dn_step, ss_ref[...], _heads3(u_ref), _heads3(w_ref), _heads3(at_ref), _heads3(qd_ref),
                         _heads3(kd_ref), eg_ref[...], _heads3(z_ref), nw_ref[...])
        ds, du, dw, dat, dqd, dkd, deg, dz, dnw = vjp((ds_scr[...], _heads3(do_ref)))
        ds_scr[...] = ds
        for r, v in zip((du_ref, dw_ref, dat_ref, dqd_ref, dkd_ref, dz_ref), (du, dw, dat, dqd, dkd, dz)):
            _unheads(r, v)
        deg_ref[...] = deg
        dnw_ref[...] += dnw

    row = lambda w_: pl.BlockSpec((CHUNK, w_), lambda n: (NCHUNK - 1 - n, 0))
    return pl.pallas_call(
        body, grid=(NCHUNK,),
        in_specs=[row(V_B)] * 5 + [row(HEADS), row(V_B), _full((1, HDIM)),
                                   pl.BlockSpec((None, HEADS, HDIM, HDIM), lambda n: (NCHUNK - 1 - n, 0, 0, 0)),
                                   row(V_B)],
        out_specs=[row(V_B)] * 5 + [row(HEADS), row(V_B), _full((1, HDIM))],
        out_shape=[_sds((T, V_B))] * 5 + [_sds((T, HEADS)), _sds((T, V_B)), _sds((1, HDIM))],
        scratch_shapes=[pltpu.VMEM((HEADS, HDIM, HDIM), F32)], name=name, compiler_params=_cp("arbitrary"),
    )(u, w, attn, qd, kd, egc, z, nw, ss, do)


def _final(name, x, fw, target, tm=512):
    def body(x_ref, fw_ref, t_ref, l_ref, dx_ref, dfw_ref):
        @pl.when(pl.program_id(0) == 0)
        def _():
            l_ref[...] = jnp.zeros_like(l_ref)
            dfw_ref[...] = jnp.zeros_like(dfw_ref)

        tv = t_ref[...]

        def f(xv, fwv):
            err = _rms(xv, fwv) - tv
            per_tok = jnp.mean(err * err, axis=-1, keepdims=True)
            return 0.5 * jnp.sum(per_tok, axis=0, keepdims=True)

        loss, vjp = jax.vjp(f, x_ref[...], fw_ref[...])
        dx, dfw = vjp(jnp.ones((1, 1), F32))
        l_ref[...] += loss
        dx_ref[...] = dx
        dfw_ref[...] += dfw

    tok = pl.BlockSpec((tm, D), lambda i: (i, 0))
    return pl.pallas_call(
        body, grid=(T // tm,), in_specs=[tok, _full((1, D)), tok], out_specs=[_full((1, 1)), tok, _full((1, D))],
        out_shape=[_sds((1, 1)), _sds((T, D)), _sds((1, D))], name=name, compiler_params=_cp("arbitrary"),
    )(x, fw, target)


def _m1_pre(tv, sv):
    return [_rms(tv[0], sv[0])]


def _m1_post(ys, tv, sv):
    return (jnp.concatenate(ys, axis=1),)


def _m1_post_split(ys, tv, sv):
    proj = jnp.concatenate(ys, axis=1)
    return tuple(proj[:, a:b] for a, b in zip(IN_SPLITS[:-1], IN_SPLITS[1:]))


def _m5_pre(tv, sv):
    return [tv[1], tv[2]]


def _m5_post(ys, tv, sv):
    return (tv[0] + ys[0] + ys[1],)


def _c1_pre(tv, sv):
    return [_rms(tv[0], sv[0])]


def _c1_post(ys, tv, sv):
    return ((jnp.concatenate(ys[:2], axis=1) + sv[1]) * jax.nn.sigmoid(jnp.concatenate(ys[2:], axis=1) + sv[2]),)


def _c3_pre(tv, sv):
    return [_silu(_layernorm(tv[0], sv[0], sv[1]))]


def _c3_post(ys, tv, sv):
    return (tv[1] + ys[0] + sv[2],)


def _row(v):
    return v.reshape(1, -1)


def _mixer_fwd(tag, x, p):
    parts = _blk_fwd(f"m1_fwd_{tag}", _m1_pre, [0], _m1_post_split, [x], [p["nw"]], [p["w_in"]],
                     [(b - a, F32) for a, b in zip(IN_SPLITS[:-1], IN_SPLITS[1:])])
    qa, ka, va, qkvb, z, ba = parts
    att = _attn_fwd(f"attn_fwd_{tag}", qa, ka, va, p["sinks"])
    qkvc = _conv_fwd(f"dnconv_fwd_{tag}", qkvb, p["dn_conv_w"], None, True)
    *loc, inv = _dn_local_fwd(f"dnloc_fwd_{tag}", qkvc, ba, p["a_log"], p["dt_bias"])
    og, ss = _dn_rec_fwd(f"dnrec_fwd_{tag}", *loc, z, p["dn_norm_w"])
    (out,) = _blk_fwd(f"m5_fwd_{tag}", _m5_pre, [0, 1], _m5_post, [x, att, og], [], [p["wo_a"], p["wo_b"]],
                      [(D, F32)])
    return out, dict(x=x, qa=qa, ka=ka, va=va, qkvb=qkvb, z=z, ba=ba, att=att, qkvc=qkvc, loc=loc, inv=inv, og=og,
                     ss=ss)


def _mixer_bwd(tag, dy, p, s):
    (dxa, datt, dog), _, (dwo_a, dwo_b) = _blk_bwd(f"m5_bwd_{tag}", _m5_pre, [0, 1], _m5_post,
                                                   [s["x"], s["att"], s["og"]], [], [p["wo_a"], p["wo_b"]], [[dy]],
                                                   linear_post=True)
    rec = _dn_rec_bwd(f"dnrec_bwd_{tag}", *s["loc"], s["z"], p["dn_norm_w"], s["ss"], dog)
    dz, dnw_dn = rec[6], rec[7]
    dqkvc, dba, dalog, ddtb = _dn_local_bwd(f"dnloc_bwd_{tag}", s["qkvc"], s["ba"], p["a_log"], p["dt_bias"],
                                            s["inv"], rec[:6])
    dqkvb, dconvw, _ = _conv_bwd(f"dnconv_bwd_{tag}", s["qkvb"], p["dn_conv_w"], None, True, dqkvc)
    dqa, dka, dva, dsinks = _attn_bwd(f"attn_bwd_{tag}", s["qa"], s["ka"], s["va"], p["sinks"], datt)
    (dx,), (dnw,), (dw_in,) = _blk_bwd(f"m1_bwd_{tag}", _m1_pre, [0], _m1_post, [s["x"]], [p["nw"]], [p["w_in"]],
                                       [[dqa, dka, dva, dqkvb, dz, dba]], res=dxa, linear_post=True)
    return dx, dict(nw=dnw, w_in=dw_in, wo_a=dwo_a, wo_b=dwo_b, dn_conv_w=dconvw, sinks=dsinks, a_log=dalog,
                    dt_bias=ddtb, dn_norm_w=dnw_dn)


def _conformer_fwd(tag, x, p):
    (glu,) = _blk_fwd(f"c1_fwd_{tag}", _c1_pre, [0], _c1_post, [x], [p["nw"], p["b1a"], p["b1b"]], [p["w1"]],
                      [(D, F32)])
    cc = _conv_fwd(f"dwconv_fwd_{tag}", glu, p["w_dw"], p["b_dw"], False)
    (out,) = _blk_fwd(f"c3_fwd_{tag}", _c3_pre, [0], _c3_post, [cc, x], [p["ln_w"], p["ln_b"], p["b2"]], [p["w2"]],
                      [(D, F32)])
    return out, dict(x=x, glu=glu, cc=cc)


def _conformer_bwd(tag, dy, p, s):
    (dcc, dxa), (dlnw, dlnb, db2), (dw2,) = _blk_bwd(f"c3_bwd_{tag}", _c3_pre, [0], _c3_post, [s["cc"], s["x"]],
                                                     [p["ln_w"], p["ln_b"], p["b2"]], [p["w2"]], [[dy]],
                                                     linear_post=True)
    dglu, dwdw, dbdw = _conv_bwd(f"dwconv_bwd_{tag}", s["glu"], p["w_dw"], p["b_dw"], False, dcc)
    (dx,), (dnw, db1a, db1b), (dw1,) = _blk_bwd(f"c1_bwd_{tag}", _c1_pre, [0], _c1_post, [s["x"]],
                                                [p["nw"], p["b1a"], p["b1b"]], [p["w1"]], [[dglu]], res=dxa)
    return dx, dict(nw=dnw, b1a=db1a, b1b=db1b, w1=dw1, w_dw=dwdw, b_dw=dbdw, ln_w=dlnw, ln_b=dlnb, b2=db2, w2=dw2)


def _layer_fwd(l, x, nw, ffn_a, get_ffn_b, p):
    x1, *pre_a = _ffn_fwd(f"ffn_fwd_{l}a", x, _row(nw[0]), ffn_a, 0)
    p = dict(p, nw=_row(nw[1]))
    x2, sv = (_mixer_fwd if l % 2 == 0 else _conformer_fwd)(str(l), x1, p)
    x2, ffn_b = get_ffn_b(x2)
    out, *pre_b = _ffn_fwd(f"ffn_fwd_{l}b", x2, _row(nw[2]), ffn_b, 0)
    return out, (x, x2, p, sv, pre_a, pre_b, ffn_a, ffn_b)


def _layer_bwd(l, dx, nw, saved, after_first=lambda dx: dx):
    x0, x2, p, sv, pre_a, pre_b, ffn_a, ffn_b = saved
    dx, dn2, dffn = _ffn_bwd(f"ffn_bwd_{l}b", x2, _row(nw[2]), ffn_b, 1, pre_b, dx)
    dx = after_first(dx)
    dx, dmix = (_mixer_bwd if l % 2 == 0 else _conformer_bwd)(str(l), dx, p, sv)
    dx, dn0, dffn = _ffn_bwd(f"ffn_bwd_{l}a", x0, _row(nw[0]), ffn_a, 0, pre_a, dx, dffn)
    return dx, jnp.concatenate([dn0, dmix.pop("nw"), dn2], axis=0), dffn, dmix


def _place(staggered=False):
    x, y, c = lax.axis_index("x"), lax.axis_index("y"), lax.axis_index("c")
    s = c if staggered else 0
    flip_x, flip_y = (x + (1 - s) * (1 - 2 * x), y + s * (1 - 2 * y)), (x + s * (1 - 2 * x), y + (1 - s) * (1 - 2 * y))
    chips = [flip_x, flip_y, (1 - x, 1 - y)]
    return x, y, c, 2 * x + y, chips, [2 * px + py for px, py in chips]


def _handshake(peers):
    barrier = pltpu.get_barrier_semaphore()
    for p in peers:
        pl.semaphore_signal(barrier, inc=1, device_id=p, device_id_type=MESH)
    pl.semaphore_wait(barrier, len(peers))


def _chip_peers():
    x, y, c, _, chips, _ = _place()
    return [(*chip, c) for chip in chips] + [(x, y, 1 - c)]


def _gather_copies(ins, outs, nb, send, recv, fsend, frecv, lsem):
    n_in = len(ins)
    x, y, c, me, chips, cidx = _place(staggered=True)
    sib = (x, y, 1 - c)
    local = [pltpu.make_async_copy(ins[a], outs[a].at[me], lsem.at[a]) for a in range(n_in)]

    def region(a, k, who):
        if k < 2:
            return outs[a].at[cidx[k], pl.ds(who, 1)]
        r = ins[a].shape[1] // 2
        return outs[a].at[cidx[2], pl.ds(who, 1), pl.ds((k - 2) * r, r)]

    def hop(a, k):
        if k < 2:
            src, dst = ins[a].at[pl.ds(c, 1)], outs[a].at[me, pl.ds(c, 1)]
        else:
            r = ins[a].shape[1] // 2
            src = dst = outs[a].at[cidx[3 - k], pl.ds(c, 1), pl.ds((k - 2) * r, r)]
        return pltpu.make_async_remote_copy(src, dst, send.at[4 * a + k], recv.at[4 * a + k],
                                            device_id=(*chips[k % 2], c), device_id_type=MESH)

    def landed(a, k):
        dst = region(a, k, c)
        return pltpu.make_async_remote_copy(dst, dst, send.at[4 * a + k], recv.at[4 * a + k],
                                            device_id=(*chips[k % 2], c), device_id_type=MESH)

    def passed(a, k, who):
        part = region(a, k, who)
        return pltpu.make_async_remote_copy(part, part, fsend.at[4 * a + k], frecv.at[4 * a + k], device_id=sib,
                                            device_id_type=MESH)

    def direct(a, j):
        k = 4 * nb + 3 * (a - nb) + j
        return pltpu.make_async_remote_copy(ins[a], outs[a].at[me], send.at[k], recv.at[k],
                                            device_id=(*chips[j], c), device_id_type=MESH)

    def direct_landed(a, j):
        k = 4 * nb + 3 * (a - nb) + j
        dst = outs[a].at[cidx[j]]
        return pltpu.make_async_remote_copy(dst, dst, send.at[k], recv.at[k], device_id=(*chips[j], c),
                                            device_id_type=MESH)

    sends = [hop(a, k) for a in range(nb) for k in range(2)] + [direct(a, j) for a in range(nb, n_in) for j in range(3)]
    for cp in sends:
        cp.start()
    for cp in local:
        cp.start()
    for a in range(nb):
        for k in (1, 0):
            landed(a, k).wait_recv()
            for cp in (hop(a, 3 - k), passed(a, k, c)):
                cp.start()
                sends.append(cp)
    for a in range(nb):
        for k in (2, 3):
            landed(a, k).wait_recv()
            cp = passed(a, k, c)
            cp.start()
            sends.append(cp)
    for a in range(nb, n_in):
        for j in range(3):
            direct_landed(a, j).wait_recv()
    for a in range(nb):
        for k in range(4):
            passed(a, k, 1 - c).wait_recv()
    for cp in sends:
        cp.wait_send()
    for cp in local:
        cp.wait()


def _gather_sems(n_in, nb):
    dma = pltpu.SemaphoreType.DMA
    n_ici = 4 * nb + 3 * (n_in - nb)
    return [dma((n_ici,)), dma((n_ici,)), dma((4 * nb,)), dma((4 * nb,)), dma((n_in,))]


def _gather_async(name, halved, whole=()):
    nb, arrs = len(halved), list(halved) + list(whole)
    hbm = pltpu.MemorySpace.HBM
    ins = [jax.new_ref(a, memory_space=hbm) for a in arrs]
    outs = [jax.empty_ref(_sds((NCHIP,) + a.shape, a.dtype), memory_space=hbm) for a in arrs]

    @pl.kernel(mesh=plsc.ScalarSubcoreMesh(axis_name="seq", num_cores=1), name=name,
               scratch_types=tuple(_gather_sems(len(arrs), nb)),
               compiler_params=pltpu.CompilerParams(collective_id=2))
    def launch(send, recv, fsend, frecv, lsem):
        _handshake(_chip_peers())
        _gather_copies(ins, outs, nb, send, recv, fsend, frecv, lsem)

    launch()
    return outs


def _swap_halves(name, grads, after=None):
    n = len(grads)
    hbm = pltpu.MemorySpace.HBM
    ins = [jax.new_ref(g, memory_space=hbm) for g in grads]
    outs = [jax.empty_ref(_sds((NCHIP, g.shape[1] // 2) + g.shape[2:], g.dtype), memory_space=hbm) for g in grads]
    tile = (2 * 8, LANES)
    token = None if after is None else jax.empty_ref(_sds(tile, BF16), memory_space=hbm)

    @pl.kernel(mesh=plsc.ScalarSubcoreMesh(axis_name="seq", num_cores=1), name=name,
               scratch_types=(pltpu.SemaphoreType.DMA((n + 1,)), pltpu.SemaphoreType.DMA((n,))),
               compiler_params=pltpu.CompilerParams(collective_id=1))
    def launch(send, recv):
        x, y, c, _, _, _ = _place()
        sib = (x, y, 1 - c)
        _handshake([sib])
        if after is not None:
            tick = pltpu.make_async_copy(after.at[0, 0, 0, pl.ds(0, tile[0]), pl.ds(0, tile[1])], token, send.at[n])
            tick.start()
            tick.wait()
        cps = []
        for a in range(n):
            h = grads[a].shape[1] // 2
            cps.append(pltpu.make_async_remote_copy(ins[a].at[:, pl.ds((1 - c) * h, h)], outs[a], send.at[a],
                                                    recv.at[a], device_id=sib, device_id_type=MESH))
        for cp in cps:
            cp.start()
        for cp in cps:
            cp.wait()

    launch()
    return outs


def _row_tile(r, cap=256):
    return max(t for t in range(8, cap + 1, 8) if r % t == 0)


def _add_half(name, g, r, c_arr):
    _, l, rows, cols = g.shape
    h = l // 2
    tr = _row_tile(rows, 1056)

    def body(c_ref, g_ref, r_ref, o_ref):
        o_ref[...] = (g_ref[...].astype(F32) + r_ref[...].astype(F32)).astype(BF16)

    blk = (None, None, tr, cols)
    return pl.pallas_call(
        body,
        grid_spec=pltpu.PrefetchScalarGridSpec(
            num_scalar_prefetch=1, grid=(NCHIP, h, rows // tr),
            in_specs=[pl.BlockSpec(blk, lambda j, i, t, c_ref: (j, c_ref[0] * h + i, t, 0)),
                      pl.BlockSpec(blk, lambda j, i, t, c_ref: (j, i, t, 0))],
            out_specs=pl.BlockSpec(blk, lambda j, i, t, c_ref: (j, i, t, 0))),
        out_shape=_sds((NCHIP, h, rows, cols), BF16), name=name,
        compiler_params=_cp("parallel", "parallel", "parallel"),
    )(c_arr, g, r)


def _scatter_async(name, parts, sums, where):
    nb = len(parts)
    ins = [jax.new_ref(p, memory_space=pltpu.MemorySpace.HBM) for p in parts]
    dma = pltpu.SemaphoreType.DMA

    @pl.kernel(mesh=plsc.ScalarSubcoreMesh(axis_name="seq", num_cores=1), name=name,
               scratch_types=(dma((3 * nb,)), dma((3 * nb,)), dma((4 * nb,)), dma((4 * nb,)), dma((nb,))),
               compiler_params=pltpu.CompilerParams(collective_id=3))
    def launch(send, recv, fsend, frecv, lsem):
        _handshake(_chip_peers())
        x, y, c, me, chips, cidx = _place(staggered=True)
        sib = (x, y, 1 - c)

        def slot(a, half, chip):
            return sums[a].at[half, chip, pl.ds(where[a], 1)]

        local = [pltpu.make_async_copy(ins[a].at[me], slot(a, c, me), lsem.at[a]) for a in range(nb)]
        for cp in local:
            cp.start()

        def ici(a, j):
            return pltpu.make_async_remote_copy(ins[a].at[cidx[j]], slot(a, c, me), send.at[a * 3 + j],
                                                recv.at[a * 3 + j], device_id=(*chips[j], c), device_id_type=MESH)

        def landed(a, j):
            dst = slot(a, c, cidx[j])
            return pltpu.make_async_remote_copy(dst, dst, send.at[a * 3 + j], recv.at[a * 3 + j],
                                                device_id=(*chips[j], c), device_id_type=MESH)

        def passed(a, j, who):
            dst = slot(a, who, me if j == 3 else cidx[j])
            src = ins[a].at[me] if j == 3 else dst
            return pltpu.make_async_remote_copy(src, dst, fsend.at[a * 4 + j], frecv.at[a * 4 + j], device_id=sib,
                                                device_id_type=MESH)

        sends = [ici(a, j) for a in range(nb) for j in range(3)] + [passed(a, 3, c) for a in range(nb)]
        for cp in sends:
            cp.start()
        for a in range(nb):
            for j in range(3):
                landed(a, j).wait_recv()
                cp = passed(a, j, c)
                cp.start()
                sends.append(cp)
        for a in range(nb):
            for j in range(4):
                passed(a, j, 1 - c).wait_recv()
        for cp in sends:
            cp.wait_send()
        for cp in local:
            cp.wait()

    launch()


def _exchange_small(small, rep):
    def body(small_in, rep_in, small_out, rep_out, lsem, ssend, srecv):
        x, y, c, me, _, _ = _place()
        dev = 4 * x + 2 * y + c
        local = [pltpu.make_async_copy(small_in.at[me], small_out.at[dev], lsem.at[0]),
                 pltpu.make_async_copy(rep_in, rep_out.at[dev], lsem.at[1])]
        for cp in local:
            cp.start()

        def peer(r):
            return (1 - x if r & 4 else x), (1 - y if r & 2 else y), (1 - c if r & 1 else c)

        def tiny(r, which):
            px, py, pc = peer(r)
            k = (r - 1) * 2 + which
            if which == 0:
                return pltpu.make_async_remote_copy(small_in.at[2 * px + py], small_out.at[dev], ssend.at[k],
                                                    srecv.at[k], device_id=(px, py, pc), device_id_type=MESH)
            return pltpu.make_async_remote_copy(rep_in, rep_out.at[dev], ssend.at[k], srecv.at[k],
                                                device_id=(px, py, pc), device_id_type=MESH)

        def tiny_landed(r, which):
            px, py, pc = peer(r)
            k = (r - 1) * 2 + which
            dst = (small_out if which == 0 else rep_out).at[4 * px + 2 * py + pc]
            return pltpu.make_async_remote_copy(dst, dst, ssend.at[k], srecv.at[k], device_id=(px, py, pc),
                                                device_id_type=MESH)

        sends = [tiny(r, w) for r in range(1, NDEV) for w in range(2)]
        for cp in sends:
            cp.start()
        for r in range(1, NDEV):
            for w in range(2):
                tiny_landed(r, w).wait_recv()
        for cp in sends:
            cp.wait_send()
        for cp in local:
            cp.wait()

    dma = pltpu.SemaphoreType.DMA
    return pl.pallas_call(
        body, in_specs=[ANY] * 2, out_specs=[ANY] * 2,
        out_shape=[_sds((NDEV,) + small.shape[1:], F32), _sds((NDEV,) + rep.shape, F32)],
        scratch_shapes=[dma((2,)), dma((2 * (NDEV - 1),)), dma((2 * (NDEV - 1),))], name="exchange_small_grads",
    )(small, rep)


def _adamw_math(w, g, m, v):
    m = B1 * m + (1.0 - B1) * g
    v = B2 * v + (1.0 - B2) * (g * g)
    m_hat = m / (1.0 - B1 ** STEP)
    v_hat = v / (1.0 - B2 ** STEP)
    return -LR * (m_hat / (jnp.sqrt(v_hat) + AEPS) + WD * w), m, v


def _adamw_big(name, w, m, v, parts, row0=0, first=0, outs=None):
    _, _, rows, cols = w.shape
    n = parts.shape[2]
    tr = _row_tile(rows)
    t0 = row0 // tr

    def body(w_ref, m_ref, v_ref, p_ref, *rest):
        g_ref, d_ref, nm_ref, nv_ref = rest[-4:]
        g = p_ref[0].astype(F32)
        for q in range(1, NCHIP):
            g = g + p_ref[q].astype(F32)
        d, nm, nv = _adamw_math(w_ref[...], g, m_ref[...], v_ref[...])
        g_ref[...], d_ref[...], nm_ref[...], nv_ref[...] = g, d, nm, nv

    spec = pl.BlockSpec((None, None, tr, cols), lambda i, p, t: (first + i, p, t, 0))
    na = 0 if outs is None else 4
    return pl.pallas_call(
        body, grid=(n, 2, rows // tr),
        in_specs=[spec, spec, spec,
                  pl.BlockSpec((None, NCHIP, None, tr, cols), lambda i, p, t: (p, 0, i, t0 + t, 0))] + [ANY] * na,
        out_specs=[spec] * 4, out_shape=[_sds(w.shape)] * 4, input_output_aliases={4 + k: k for k in range(na)},
        name=name, compiler_params=_cp("parallel", "parallel", "parallel"),
    )(w, m, v, parts, *(outs or ()))


def _adamw_small(name, w, m, v, parts):
    def body(w_ref, m_ref, v_ref, p_ref, g_ref, d_ref, nm_ref, nv_ref):
        g = p_ref[0]
        for q in range(1, NDEV):
            g = g + p_ref[q]
        d, nm, nv = _adamw_math(w_ref[...], g, m_ref[...], v_ref[...])
        g_ref[...], d_ref[...], nm_ref[...], nv_ref[...] = g, d, nm, nv

    return pl.pallas_call(body, out_shape=[_sds(w.shape)] * 4, name=name)(w, m, v, parts)


def _pack(arrs, rows):
    flat = jnp.concatenate([a.reshape(-1) for a in arrs])
    return jnp.pad(flat, (0, rows * LANES - flat.shape[0])).reshape(rows, LANES)


def _unpack(packed, shapes):
    flat, out, o = packed.reshape(-1), [], 0
    for s in shapes:
        n = 1
        for d in s:
            n *= d
        out.append(flat[o:o + n].reshape(s))
        o += n
    return out


SMALL_ROWS, REP_ROWS = 200, 16


def kernel(x, norm_w, ffn_w_gate, ffn_w_up, ffn_w_down, mix_w_in, dn_conv_w, attn_sinks, dn_a_log, dn_dt_bias, dn_norm_w, mix_w_out, conv_w_pw1, conv_b_pw1, conv_w_dw, conv_b_dw, conv_ln_w, conv_ln_b, conv_w_pw2, conv_b_pw2, final_norm_w, loss_target, m_norm_w, m_ffn_w_gate, m_ffn_w_up, m_ffn_w_down, m_mix_w_in, m_dn_conv_w, m_attn_sinks, m_dn_a_log, m_dn_dt_bias, m_dn_norm_w, m_mix_w_out, m_conv_w_pw1, m_conv_b_pw1, m_conv_w_dw, m_conv_b_dw, m_conv_ln_w, m_conv_ln_b, m_conv_w_pw2, m_conv_b_pw2, m_final_norm_w, v_norm_w, v_ffn_w_gate, v_ffn_w_up, v_ffn_w_down, v_mix_w_in, v_dn_conv_w, v_attn_sinks, v_dn_a_log, v_dn_dt_bias, v_dn_norm_w, v_mix_w_out, v_conv_w_pw1, v_conv_b_pw1, v_conv_w_dw, v_conv_b_dw, v_conv_ln_w, v_conv_ln_b, v_conv_w_pw2, v_conv_b_pw2, v_final_norm_w):
    small_names = ["norm_w", "dn_conv_w", "conv_b_pw1", "conv_w_dw", "conv_b_dw", "conv_ln_w", "conv_ln_b",
                   "conv_b_pw2"]
    rep_names = ["attn_sinks", "dn_a_log", "dn_dt_bias", "dn_norm_w", "final_norm_w"]
    w = dict(norm_w=norm_w, ffn_w_gate=ffn_w_gate, ffn_w_up=ffn_w_up, ffn_w_down=ffn_w_down, mix_w_in=mix_w_in, dn_conv_w=dn_conv_w, attn_sinks=attn_sinks, dn_a_log=dn_a_log, dn_dt_bias=dn_dt_bias, dn_norm_w=dn_norm_w, mix_w_out=mix_w_out, conv_w_pw1=conv_w_pw1, conv_b_pw1=conv_b_pw1, conv_w_dw=conv_w_dw, conv_b_dw=conv_b_dw, conv_ln_w=conv_ln_w, conv_ln_b=conv_ln_b, conv_w_pw2=conv_w_pw2, conv_b_pw2=conv_b_pw2, final_norm_w=final_norm_w)
    m = dict(norm_w=m_norm_w, ffn_w_gate=m_ffn_w_gate, ffn_w_up=m_ffn_w_up, ffn_w_down=m_ffn_w_down, mix_w_in=m_mix_w_in, dn_conv_w=m_dn_conv_w, attn_sinks=m_attn_sinks, dn_a_log=m_dn_a_log, dn_dt_bias=m_dn_dt_bias, dn_norm_w=m_dn_norm_w, mix_w_out=m_mix_w_out, conv_w_pw1=m_conv_w_pw1, conv_b_pw1=m_conv_b_pw1, conv_w_dw=m_conv_w_dw, conv_b_dw=m_conv_b_dw, conv_ln_w=m_conv_ln_w, conv_ln_b=m_conv_ln_b, conv_w_pw2=m_conv_w_pw2, conv_b_pw2=m_conv_b_pw2, final_norm_w=m_final_norm_w)
    v = dict(norm_w=v_norm_w, ffn_w_gate=v_ffn_w_gate, ffn_w_up=v_ffn_w_up, ffn_w_down=v_ffn_w_down, mix_w_in=v_mix_w_in, dn_conv_w=v_dn_conv_w, attn_sinks=v_attn_sinks, dn_a_log=v_dn_a_log, dn_dt_bias=v_dn_dt_bias, dn_norm_w=v_dn_norm_w, mix_w_out=v_mix_w_out, conv_w_pw1=v_conv_w_pw1, conv_b_pw1=v_conv_b_pw1, conv_w_dw=v_conv_w_dw, conv_b_dw=v_conv_b_dw, conv_ln_w=v_conv_ln_w, conv_ln_b=v_conv_ln_b, conv_w_pw2=v_conv_w_pw2, conv_b_pw2=v_conv_b_pw2, final_norm_w=v_final_norm_w)
    order = ["norm_w", "ffn_w_gate", "ffn_w_up", "ffn_w_down", "mix_w_in", "dn_conv_w", "attn_sinks", "dn_a_log",
             "dn_dt_bias", "dn_norm_w", "mix_w_out", "conv_w_pw1", "conv_b_pw1", "conv_w_dw", "conv_b_dw",
             "conv_ln_w", "conv_ln_b", "conv_w_pw2", "conv_b_pw2", "final_norm_w"]

    small_shapes = [w[n].shape for n in small_names]
    rep_shapes = [w[n].shape for n in rep_names]

    def halves(a):
        return a.reshape(a.shape[:-2] + (2, a.shape[-2] // 2, a.shape[-1]))

    tr = lambda a: jnp.swapaxes(a, -1, -2)
    gate_t, up_t = tr(ffn_w_gate), tr(ffn_w_up)

    def layer_shards(l):
        mix_in, mix_out = (mix_w_in, mix_w_out) if l % 2 == 0 else (conv_w_pw1, conv_w_pw2)
        ffn = jnp.concatenate([gate_t[l], up_t[l], ffn_w_down[l]], axis=1)
        return ([t.astype(BF16) for t in (halves(ffn[0]), halves(mix_in[l // 2]), halves(mix_out[l // 2]))],
                [halves(ffn[1]).astype(BF16)])

    first = layer_shards(0)
    first = (first[0] + [_pack([w[n] for n in small_names], SMALL_ROWS)], first[1])
    first, (gate_t, up_t, ffn_w_down, mix_w_in, mix_w_out, conv_w_pw1, conv_w_pw2) = lax.optimization_barrier(
        (first, (gate_t, up_t, ffn_w_down, mix_w_in, mix_w_out, conv_w_pw1, conv_w_pw2)))
    gathering = [(_gather_async("gather_layer0a", first[0][:3], first[0][3:]),
                  _gather_async("gather_layer0b", first[1]))]
    for l in range(1, DEPTH):
        before, after = layer_shards(l)
        gathering.append((_gather_async(f"gather_layer{l}a", before), _gather_async(f"gather_layer{l}b", after)))
    ffn_block = lambda g: g.reshape(NCHIP, 1, 3 * FS, D)

    def mixer_params(l, w_a, w_b):
        e = l // 2
        w_a = w_a.reshape(NCHIP, D, -1)
        w_b = w_b.reshape(D, D)
        if l % 2 == 0:
            return dict(w_in=w_a, dn_conv_w=sm["dn_conv_w"][e], sinks=_row(attn_sinks[e]), a_log=_row(dn_a_log[e]),
                        dt_bias=_row(dn_dt_bias[e]), dn_norm_w=_row(dn_norm_w[e]), wo_a=w_b[:Q_A], wo_b=w_b[Q_A:])
        return dict(b1a=_row(sm["conv_b_pw1"][e, :D]), b1b=_row(sm["conv_b_pw1"][e, D:]), w1=w_a,
                    w_dw=sm["conv_w_dw"][e], b_dw=_row(sm["conv_b_dw"][e]), ln_w=_row(sm["conv_ln_w"][e]),
                    ln_b=_row(sm["conv_ln_b"][e]), b2=_row(sm["conv_b_pw2"][e]), w2=w_b)

    xs, saved = x[0], []
    for l in range(DEPTH):
        got = [r[...] for r in gathering[l][0]]
        if l == 0:
            per_chip = [_unpack(got[3][q], small_shapes) for q in range(NCHIP)]
            sm = {n: jnp.concatenate([per_chip[q][i] for q in range(NCHIP)], axis=-1)
                  for i, n in enumerate(small_names)}
        else:
            xs, got = lax.optimization_barrier((xs, got))

        def second_ffn(x2, l=l):
            x2, got_b = lax.optimization_barrier((x2, gathering[l][1][0][...]))
            return x2, ffn_block(got_b)

        xs, sv = _layer_fwd(l, xs, sm["norm_w"][l], ffn_block(got[0]), second_ffn, mixer_params(l, got[1], got[2]))
        saved.append(sv)
    loss, dx, dfw = _final("final", xs, _row(final_norm_w), loss_target[0])

    hbm = pltpu.MemorySpace.HBM
    row_shapes = dict(ffn=(3 * FS, D), w_in=(D // 2, IN_COLS // NCHIP), w_out=(D // 8, D), pw1=(D // 2, D // 2),
                      pw2=(D // 8, D))
    new_sums = lambda k, n: jax.empty_ref(_sds((2, NCHIP, n) + row_shapes[k], BF16), memory_space=hbm)
    sums_0 = {k: new_sums(k, 1) for k in ("ffn", "w_in", "w_out")}
    sums = dict(ffn=new_sums("ffn", DEPTH - 1), w_in=new_sums("w_in", 1), w_out=new_sums("w_out", 1),
                pw1=new_sums("pw1", 2), pw2=new_sums("pw2", 2))
    c_arr = lax.axis_index("c").astype(jnp.int32).reshape(1)
    dnorm, gmix = [None] * DEPTH, [None] * DEPTH

    def hand_on(l, grads, swapped):
        def run(dx):
            dx, other = lax.optimization_barrier((dx, [r[...] for r in swapped]))
            parts = [_add_half(f"add_half_{l}_{k}", gg, rr, c_arr) for k, (gg, rr) in enumerate(zip(grads, other))]
            dx, parts = lax.optimization_barrier((dx, parts))
            keys = ("ffn", "w_in", "w_out") if l % 2 == 0 else ("ffn", "pw1", "pw2")
            if l == 0:
                _scatter_async("scatter_grads_0", parts, [sums_0[k] for k in keys], [0, 0, 0])
            else:
                _scatter_async(f"scatter_grads_{l}", parts, [sums[k] for k in keys],
                               [l - 1, 0, 0] if l % 2 == 0 else [l - 1, l // 2, l // 2])
            return dx
        return run

    pending = lambda dx: dx
    for l in reversed(range(DEPTH)):
        dx, dnorm[l], dffn, gmix[l] = _layer_bwd(l, dx, sm["norm_w"][l], saved[l], pending)
        if l % 2 == 0:
            g_a, g_b = gmix[l]["w_in"], jnp.concatenate([gmix[l]["wo_a"], gmix[l]["wo_b"]], axis=0)
        else:
            g_a, g_b = gmix[l]["w1"], gmix[l]["w2"]
        g_a = halves(g_a).astype(BF16)
        g_b = g_b.reshape(NCHIP, 2, D // 8, D).astype(BF16)
        dx, grads = lax.optimization_barrier((dx, [dffn, g_a, g_b]))
        pending = hand_on(l, grads, _swap_halves(f"swap_grads_{l}", grads, sums["ffn"] if l < DEPTH - 1 else None))
    gm, gc = [gmix[0], gmix[2]], [gmix[1], gmix[3]]
    small_g = dict(
        norm_w=jnp.stack(dnorm), dn_conv_w=jnp.stack([gm[e]["dn_conv_w"] for e in range(2)]),
        conv_b_pw1=jnp.stack([jnp.concatenate([gc[e]["b1a"], gc[e]["b1b"]], axis=1)[0] for e in range(2)]),
        conv_w_dw=jnp.stack([gc[e]["w_dw"] for e in range(2)]),
        conv_b_dw=jnp.stack([gc[e]["b_dw"][0] for e in range(2)]),
        conv_ln_w=jnp.stack([gc[e]["ln_w"][0] for e in range(2)]),
        conv_ln_b=jnp.stack([gc[e]["ln_b"][0] for e in range(2)]),
        conv_b_pw2=jnp.stack([gc[e]["b2"][0] for e in range(2)]))
    small_by_chip = jnp.stack([_pack([jnp.split(small_g[n], NCHIP, axis=-1)[q] for n in small_names], SMALL_ROWS)
                               for q in range(NCHIP)])
    rep_g = _pack([jnp.stack([gm[e]["sinks"][0] for e in range(2)]), jnp.stack([gm[e]["a_log"][0] for e in range(2)]),
                   jnp.stack([gm[e]["dt_bias"][0] for e in range(2)]),
                   jnp.stack([gm[e]["dn_norm_w"][0] for e in range(2)]), dfw[0]], REP_ROWS)
    small_sum, rep_sum = _exchange_small(small_by_chip, rep_g)
    dx, small_sum, rep_sum = lax.optimization_barrier((dx, small_sum, rep_sum))
    dx = pending(dx)

    big = (("ffn_w_gate", "ffn", 0), ("ffn_w_up", "ffn", FS), ("ffn_w_down", "ffn", 2 * FS), ("mix_w_in", "w_in", 0),
           ("mix_w_out", "w_out", 0), ("conv_w_pw1", "pw1", 0), ("conv_w_pw2", "pw2", 0))
    views = {n: (tr, tr) if n in ("ffn_w_gate", "ffn_w_up") else (
        (lambda a: a) if w[n].ndim == 4 else halves, lambda o, n=n: o.reshape(w[n].shape)) for n, _, _ in big}
    partial_sums = {k: r[...] for k, r in sums.items()}
    upper = {}
    for n, key, row0 in big:
        view = views[n][0]
        upper[n] = _adamw_big(f"adamw_{n}", view(w[n]), view(m[n]), view(v[n]), partial_sums[key], row0,
                              first=0 if key in ("pw1", "pw2") else 1)
    upper, partial_sums_0 = lax.optimization_barrier((upper, {k: r[...] for k, r in sums_0.items()}))
    res = {}
    for n, key, row0 in big:
        view, back = views[n]
        outs = upper[n] if key not in partial_sums_0 else _adamw_big(
            f"adamw_{n}_0", view(w[n]), view(m[n]), view(v[n]), partial_sums_0[key], row0, first=0, outs=upper[n])
        res[n] = [back(o) for o in outs]
    outs = _adamw_small("adamw_small", *[_pack([d[n] for n in small_names], SMALL_ROWS) for d in (w, m, v)],
                        small_sum)
    for i, n in enumerate(small_names):
        res[n] = [_unpack(o, small_shapes)[i] for o in outs]
    outs = _adamw_small("adamw_replicated", *[_pack([d[n] for n in rep_names], REP_ROWS) for d in (w, m, v)],
                        rep_sum)
    for i, n in enumerate(rep_names):
        res[n] = [_unpack(o, rep_shapes)[i] for o in outs]

    total = lax.psum(loss[0, 0], ("x", "y", "c"))
    return (total, dx[None], *[res[n][0] for n in order], *[res[n][1] for n in order],
            *[res[n][2] for n in order], *[res[n][3] for n in order])
```

```python
import jax
import jax.numpy as jnp
from jax import lax
from jax.experimental import pallas as pl
from jax.experimental.pallas import tpu as pltpu
from jax.experimental.pallas import tpu_sc as plsc

F32, BF16 = jnp.float32, jnp.bfloat16
MESH = pl.DeviceIdType.MESH
ANY = pl.BlockSpec(memory_space=pl.ANY)

T, D, F = 2048, 1024, 2816
DEPTH = 4
EPS = 1e-6
HEADS, HDIM, KV_HEADS, GROUP = 8, 64, 2, 4
WINDOW = BLOCK = 128
CHUNK = 64
NCHUNK = T // CHUNK
Q_A, KV_A, QKV_B, V_B = 512, 128, 1536, 512
IN_COLS = 2832
IN_SPLITS = (0, 512, 640, 768, 2304, 2816, 2832)
NCHIP, NDEV = 4, 8
FS = F // NCHIP
LR, B1, B2, AEPS, WD, STEP = 0.001, 0.9, 0.999, 1e-08, 0.01, 10
V7X_VMEM_BYTES = 64 * 1024 * 1024
VMEM_LIMIT = V7X_VMEM_BYTES * 7 // 8
LANES = 128


def _cp(*sem):
    return pltpu.CompilerParams(dimension_semantics=sem, vmem_limit_bytes=VMEM_LIMIT)


def _sds(shape, dtype=F32):
    return jax.ShapeDtypeStruct(tuple(shape), dtype)


def _full(shape):
    nd = len(shape)
    return pl.BlockSpec(tuple(shape), lambda *_: (0,) * nd)


def _split_bf16(a):
    hi = a.astype(BF16)
    return hi, (a - hi.astype(F32)).astype(BF16)


def _dg(a, b, ca, cb, hi=False):
    if a.ndim == 3 and b.ndim == 3:
        dims = (((ca + 1,), (cb + 1,)), ((0,), (0,)))
    else:
        dims = (((ca,), (cb,)), ((), ()))
    dot = lambda p, q: lax.dot_general(p, q, dims, preferred_element_type=F32)
    if hi:
        a_hi, a_lo = _split_bf16(a.astype(F32))
        b_hi, b_lo = _split_bf16(b.astype(F32))
        return dot(a_hi, b_hi) + (dot(a_hi, b_lo) + dot(a_lo, b_hi))
    return dot(a.astype(BF16), b.astype(BF16))


def _make_mm(hi):
    @jax.custom_vjp
    def nn(a, b):
        return _dg(a, b, 1, 0, hi)

    @jax.custom_vjp
    def nt(a, b):
        return _dg(a, b, 1, 1, hi)

    @jax.custom_vjp
    def tn(a, b):
        return _dg(a, b, 0, 0, hi)

    nn.defvjp(lambda a, b: (_dg(a, b, 1, 0, hi), (a, b)),
              lambda r, g: (_dg(g, r[1], 1, 1, hi).astype(r[0].dtype), _dg(r[0], g, 0, 0, hi).astype(r[1].dtype)))
    nt.defvjp(lambda a, b: (_dg(a, b, 1, 1, hi), (a, b)),
              lambda r, g: (_dg(g, r[1], 1, 0, hi).astype(r[0].dtype), _dg(g, r[0], 0, 0, hi).astype(r[1].dtype)))
    tn.defvjp(lambda a, b: (_dg(a, b, 0, 0, hi), (a, b)),
              lambda r, g: (_dg(r[1], g, 1, 1, hi).astype(r[0].dtype), _dg(r[0], g, 1, 0, hi).astype(r[1].dtype)))
    return nn, nt, tn


_nn, _nt, _tn = _make_mm(False)
_nn_hi = _make_mm(True)[0]


def _rms(x, w):
    return x * lax.rsqrt(jnp.mean(x * x, axis=-1, keepdims=True) + EPS) * w


def _layernorm(x, w, b):
    xc = x - jnp.mean(x, axis=-1, keepdims=True)
    return xc * lax.rsqrt(jnp.mean(xc * xc, axis=-1, keepdims=True) + EPS) * w + b


def _silu(x):
    return x * jax.nn.sigmoid(x)


def _iota2(shape, dim):
    return lax.broadcasted_iota(jnp.int32, shape, dim)


def _flat_weights(lhs_idx, weights):
    specs, ops, lhs_of, where = [], [], [], []
    for a, (k, w) in enumerate(zip(lhs_idx, weights)):
        for q in range(1 if w.ndim == 2 else w.shape[0]):
            specs.append(_full(w.shape) if w.ndim == 2
                         else pl.BlockSpec((None,) + w.shape[1:], lambda i, q=q: (q, 0, 0)))
            ops.append(w)
            lhs_of.append(k)
            where.append((a, None if w.ndim == 2 else q))
    return specs, ops, lhs_of, where


def _blk_fwd(name, pre, lhs_idx, post, toks, smalls, weights, outs, tm=512):
    wspecs, wops, lhs_of, _ = _flat_weights(lhs_idx, weights)
    nt_, ns, nw = len(toks), len(smalls), len(wops)

    def body(*refs):
        tv = [r[...] for r in refs[:nt_]]
        sv = [r[...] for r in refs[nt_:nt_ + ns]]
        wr = refs[nt_ + ns:nt_ + ns + nw]
        orf = refs[nt_ + ns + nw:]
        lhs = pre(tv, sv)
        ys = [_dg(lhs[i], w[...], 1, 0) for i, w in zip(lhs_of, wr)]
        for o_ref, o in zip(orf, post(ys, tv, sv)):
            o_ref[...] = o.astype(o_ref.dtype)

    in_specs = ([pl.BlockSpec((tm, a.shape[1]), lambda i: (i, 0)) for a in toks]
                + [_full(a.shape) for a in smalls] + wspecs)
    out_specs = [pl.BlockSpec((tm, w_), lambda i: (i, 0)) for w_, _ in outs]
    return pl.pallas_call(
        body, grid=(T // tm,), in_specs=in_specs, out_specs=out_specs,
        out_shape=[_sds((T, w_), dt) for w_, dt in outs], name=name, compiler_params=_cp("parallel"),
    )(*toks, *smalls, *wops)


def _blk_bwd(name, pre, lhs_idx, post, toks, smalls, weights, ct_groups, res=None, linear_post=False, tm=256,
             wchunk=512):
    wspecs, wops, lhs_of, where = _flat_weights(lhs_idx, weights)
    nt_, ns, nw, na = len(toks), len(smalls), len(wops), len(weights)
    cts = [a for g in ct_groups for a in g]
    nc = len(cts)
    widths = [sum(a.shape[1] for a in g) for g in ct_groups]
    has_res = res is not None

    def body(*refs):
        p = 0
        tr = refs[p:p + nt_]; p += nt_
        sr = refs[p:p + ns]; p += ns
        wr = refs[p:p + nw]; p += nw
        cr = refs[p:p + nc]; p += nc
        rr = refs[p:p + has_res]; p += has_res
        dtr = refs[p:p + nt_]; p += nt_
        dsr = refs[p:p + ns]; p += ns
        dwr = refs[p:p + na]; p += na
        scr = refs[p:]
        i = pl.program_id(0)

        @pl.when(i == 0)
        def _():
            for r in list(dsr) + list(dwr):
                r[...] = jnp.zeros_like(r)

        tv = [r[...] for r in tr]
        sv = [r[...] for r in sr]
        ctv, q, si = [], 0, 0
        for g in ct_groups:
            if len(g) == 1:
                ctv.append(cr[q][...].astype(F32))
            else:
                off = 0
                for j, a in enumerate(g):
                    scr[si][:, off:off + a.shape[1]] = cr[q + j][...].astype(F32)
                    off += a.shape[1]
                ctv.append(scr[si][...])
                si += 1
            q += len(g)

        lhs, vjp_pre = jax.vjp(lambda *a: tuple(pre(list(a[:nt_]), list(a[nt_:]))), *tv, *sv)
        lhs_b = [l.astype(BF16) for l in lhs]
        ys = [jnp.zeros((tm, w.shape[1]), F32) if linear_post else _dg(lhs_b[k], w[...], 1, 0)
              for k, w in zip(lhs_of, wr)]
        _, vjp_post = jax.vjp(lambda *a: tuple(post(list(a[:nw]), list(a[nw:nw + nt_]), list(a[nw + nt_:]))),
                              *ys, *tv, *sv)
        gp = vjp_post(tuple(ctv))
        dys, dt_post, ds_post = gp[:nw], gp[nw:nw + nt_], gp[nw + nt_:]
        dlhs = [None] * len(lhs)
        for k, w, dy, (a, q) in zip(lhs_of, wr, dys, where):
            dyb = dy.astype(BF16)
            n = w.shape[1]
            for c0 in range(0, n, wchunk):
                c1 = min(n, c0 + wchunk)
                part = _dg(lhs_b[k], dyb[:, c0:c1], 0, 0)
                if q is None:
                    dwr[a][:, c0:c1] += part
                else:
                    dwr[a][q, :, c0:c1] += part
            d = _dg(dyb, w[...], 1, 1)
            dlhs[k] = d if dlhs[k] is None else dlhs[k] + d
        gq = vjp_pre(tuple(d.astype(l.dtype) for d, l in zip(dlhs, lhs)))
        dt_pre, ds_pre = gq[:nt_], gq[nt_:]
        for j in range(nt_):
            d = dt_post[j] + dt_pre[j]
            if j == 0 and has_res:
                d = d + rr[0][...]
            dtr[j][...] = d
        for j in range(ns):
            dsr[j][...] += ds_post[j] + ds_pre[j]

    tok_spec = lambda a: pl.BlockSpec((tm, a.shape[1]), lambda i: (i, 0))
    in_specs = ([tok_spec(a) for a in toks] + [_full(a.shape) for a in smalls] + wspecs
                + [tok_spec(a) for a in cts] + ([tok_spec(res)] if has_res else []))
    out_specs = [tok_spec(a) for a in toks] + [_full(a.shape) for a in smalls] + [_full(w.shape) for w in weights]
    out_shape = ([_sds(a.shape) for a in toks] + [_sds(a.shape) for a in smalls] + [_sds(w.shape) for w in weights])
    scratch = [pltpu.VMEM((tm, wd), F32) for g, wd in zip(ct_groups, widths) if len(g) > 1]
    outs = pl.pallas_call(
        body, grid=(T // tm,), in_specs=in_specs, out_specs=out_specs, out_shape=out_shape,
        scratch_shapes=scratch, name=name, compiler_params=_cp("arbitrary"),
    )(*toks, *smalls, *wops, *cts, *([res] if has_res else []))
    return outs[:nt_], outs[nt_:nt_ + ns], outs[nt_ + ns:]


def _ffn_fwd(name, x, nw, ffn, idx, tm=1024):
    def body(x_ref, nw_ref, wg_ref, wu_ref, wd_ref, o_ref, g_ref, da_ref, db_ref, h_ref):
        s = pl.program_id(1)

        @pl.when(s == 0)
        def _():
            xv = x_ref[...]
            h_ref[...] = _rms(xv, nw_ref[...]).astype(BF16)
            o_ref[...] = xv

        h = h_ref[...]
        a = _dg(h, wg_ref[...], 1, 1)
        b = _dg(h, wu_ref[...], 1, 1)
        sa = jax.nn.sigmoid(a)
        act = a * sa
        gated = (act * b).astype(BF16)
        g_ref[...] = gated
        da_ref[...] = (b * (sa * (1.0 + a * (1.0 - sa)))).astype(BF16)
        db_ref[...] = act.astype(BF16)
        o_ref[...] += 0.5 * _dg(gated, wd_ref[...], 1, 0)

    wspec = lambda k: pl.BlockSpec((None, None, FS, D), lambda i, s: (s, idx, k, 0))
    act = pl.BlockSpec((None, tm, FS), lambda i, s: (s, i, 0))
    return pl.pallas_call(
        body, grid=(T // tm, NCHIP),
        in_specs=[pl.BlockSpec((tm, D), lambda i, s: (i, 0)), _full((1, D)), wspec(0), wspec(1), wspec(2)],
        out_specs=[pl.BlockSpec((tm, D), lambda i, s: (i, 0)), act, act, act,
                   pl.BlockSpec((tm, D), lambda i, s: (i, 0))],
        out_shape=[_sds((T, D))] + [_sds((NCHIP, T, FS), BF16)] * 3 + [_sds((T, D), BF16)],
        name=name, compiler_params=_cp("parallel", "arbitrary"),
    )(x, nw, ffn, ffn, ffn)


def _ffn_bwd(name, x, nw, ffn, idx, pre, dy, gbuf=None, tm=512):
    ni = T // tm

    def body(x_ref, dy_ref, nw_ref, wg_ref, wu_ref, wd_ref, g_ref, fa_ref, fb_ref, h_ref, dx_ref, dnw_ref, dffn_ref,
             dh_acc, ag, au, ad):
        s, i = pl.program_id(0), pl.program_id(1)
        rows = pl.ds(pl.multiple_of(i * tm, tm), tm)

        @pl.when((s == 0) & (i == 0))
        def _():
            dnw_ref[...] = jnp.zeros_like(dnw_ref)

        @pl.when(i == 0)
        def _():
            ag[...] = jnp.zeros_like(ag)
            au[...] = jnp.zeros_like(au)
            ad[...] = jnp.zeros_like(ad)

        hb = h_ref[...]
        dyb = (0.5 * dy_ref[...]).astype(BF16)
        ad[...] += _dg(g_ref[...], dyb, 0, 0)
        dact = _dg(dyb, wd_ref[...], 1, 1)
        da = (dact * fa_ref[...].astype(F32)).astype(BF16)
        db = (dact * fb_ref[...].astype(F32)).astype(BF16)
        ag[...] += _dg(da, hb, 0, 0)
        au[...] += _dg(db, hb, 0, 0)
        dh = _dg(da, wg_ref[...], 1, 0) + _dg(db, wu_ref[...], 1, 0)

        @pl.when(s == 0)
        def _():
            dh_acc[rows, :] = dh

        @pl.when((s > 0) & (s < NCHIP - 1))
        def _():
            dh_acc[rows, :] += dh

        @pl.when(s == NCHIP - 1)
        def _():
            _, vjp_rms = jax.vjp(_rms, x_ref[...], nw_ref[...])
            dx, dnw = vjp_rms(dh_acc[rows, :] + dh)
            dx_ref[...] = dy_ref[...] + dx
            dnw_ref[...] += dnw

        @pl.when(i == ni - 1)
        def _():
            dffn_ref[0:FS, :] = ag[...].astype(BF16)
            dffn_ref[FS:2 * FS, :] = au[...].astype(BF16)
            dffn_ref[2 * FS:, :] = ad[...].astype(BF16)

    wspec = lambda r, k, blk=0: pl.BlockSpec((None, None, r, D), lambda s, i: (s, blk, k, 0),
                                             pipeline_mode=pl.Buffered(1))
    last = lambda s, i: (jnp.where(s == NCHIP - 1, i, 0), 0)
    nb = 0 if gbuf is None else 1
    act = pl.BlockSpec((None, tm, FS), lambda s, i: (s, i, 0))
    tok = pl.BlockSpec((tm, D), lambda s, i: (i, 0))
    return pl.pallas_call(
        lambda *refs: body(*refs[:10], *refs[10 + nb:]), grid=(NCHIP, ni),
        in_specs=[pl.BlockSpec((tm, D), last), tok, _full((1, D)), wspec(FS, 0), wspec(FS, 1), wspec(FS, 2), act, act,
                  act, tok] + [ANY] * nb,
        out_specs=[pl.BlockSpec((tm, D), last), _full((1, D)), wspec(3 * FS, 0, idx)],
        out_shape=[_sds((T, D)), _sds((1, D)), _sds((NCHIP, 2, 3 * FS, D), BF16)],
        input_output_aliases={10 + k: 2 + k for k in range(nb)},
        scratch_shapes=[pltpu.VMEM((T, D), F32)] + [pltpu.VMEM((FS, D), F32)] * 3,
        name=name, compiler_params=_cp("arbitrary", "arbitrary"),
    )(x, dy, nw, ffn, ffn, ffn, *pre, *(() if gbuf is None else (gbuf,)))


CONV_ROWS = 256


def _conv_pad(k):
    return 8 * ((k - 1 + 7) // 8)


def _shifted(win, o):
    n = win.shape[0]
    return (win if o % n == 0 else pltpu.roll(win, (n - o) % n, 0))[0:CONV_ROWS, :]


def _conv_fwd(name, x, w, b, act):
    k_w, c = w.shape
    tc = 256 if c % 256 == 0 else LANES
    pad = _conv_pad(k_w)
    has_b = b is not None

    def body(*refs):
        x_ref, w_ref = refs[0], refs[1]
        b_ref = refs[2] if has_b else None
        y_ref, xp = refs[2 + has_b], refs[3 + has_b]
        xp[0:pad, :] = jnp.zeros((pad, tc), F32)
        xp[pad:, :] = x_ref[...]

        def step(t, carry):
            base = pl.multiple_of(t * CONV_ROWS, CONV_ROWS)
            win = xp[pl.ds(base, CONV_ROWS + pad), :]
            acc = jnp.zeros((CONV_ROWS, tc), F32)
            for k in range(k_w):
                o = pad - (k_w - 1) + k
                acc = acc + w_ref[k:k + 1, :] * _shifted(win, o)
            if has_b:
                acc = acc + b_ref[...]
            y_ref[pl.ds(base, CONV_ROWS), :] = _silu(acc) if act else acc
            return carry

        lax.fori_loop(0, T // CONV_ROWS, step, 0)

    col = lambda r: pl.BlockSpec((r, tc), lambda j: (0, j))
    ins = [x, w] + ([b] if has_b else [])
    return pl.pallas_call(
        body, grid=(c // tc,), in_specs=[col(T), col(k_w)] + ([col(1)] if has_b else []), out_specs=col(T),
        out_shape=_sds((T, c)), scratch_shapes=[pltpu.VMEM((T + pad, tc), F32)], name=name,
        compiler_params=_cp("parallel"),
    )(*ins)


def _conv_bwd(name, x, w, b, act, dy):
    k_w, c = w.shape
    tc = 256 if c % 256 == 0 else LANES
    pad = _conv_pad(k_w)
    has_b = b is not None

    def body(*refs):
        x_ref, w_ref, dy_ref = refs[0], refs[1], refs[2]
        b_ref = refs[3] if has_b else None
        dx_ref, dw_ref, db_ref, xp, dp = refs[3 + has_b:]
        xp[0:pad, :] = jnp.zeros((pad, tc), F32)
        xp[pad:, :] = x_ref[...]
        dp[T:, :] = jnp.zeros((pad, tc), F32)
        dw_ref[...] = jnp.zeros_like(dw_ref)
        db_ref[...] = jnp.zeros_like(db_ref)

        def step1(t, carry):
            base = pl.multiple_of(t * CONV_ROWS, CONV_ROWS)
            d = dy_ref[pl.ds(base, CONV_ROWS), :]
            win = xp[pl.ds(base, CONV_ROWS + pad), :]
            offs = [pad - (k_w - 1) + k for k in range(k_w)]
            if act:
                acc = jnp.zeros((CONV_ROWS, tc), F32)
                for k, o in enumerate(offs):
                    acc = acc + w_ref[k:k + 1, :] * _shifted(win, o)
                if has_b:
                    acc = acc + b_ref[...]
                sg = jax.nn.sigmoid(acc)
                d = d * (sg * (1.0 + acc * (1.0 - sg)))
            dp[pl.ds(base, CONV_ROWS), :] = d
            for k, o in enumerate(offs):
                dw_ref[k:k + 1, :] += jnp.sum(d * _shifted(win, o), axis=0, keepdims=True)
            db_ref[...] += jnp.sum(d, axis=0, keepdims=True)
            return carry

        lax.fori_loop(0, T // CONV_ROWS, step1, 0)

        def step2(t, carry):
            base = pl.multiple_of(t * CONV_ROWS, CONV_ROWS)
            win = dp[pl.ds(base, CONV_ROWS + pad), :]
            acc = jnp.zeros((CONV_ROWS, tc), F32)
            for k in range(k_w):
                o = (k_w - 1) - k
                acc = acc + w_ref[k:k + 1, :] * _shifted(win, o)
            dx_ref[pl.ds(base, CONV_ROWS), :] = acc
            return carry

        lax.fori_loop(0, T // CONV_ROWS, step2, 0)

    col = lambda r: pl.BlockSpec((r, tc), lambda j: (0, j))
    ins = [x, w, dy] + ([b] if has_b else [])
    return pl.pallas_call(
        body, grid=(c // tc,), in_specs=[col(T), col(k_w), col(T)] + ([col(1)] if has_b else []),
        out_specs=[col(T), col(k_w), col(1)], out_shape=[_sds((T, c)), _sds((k_w, c)), _sds((1, c))],
        scratch_shapes=[pltpu.VMEM((T + pad, tc), F32), pltpu.VMEM((T + pad, tc), F32)], name=name,
        compiler_params=_cp("parallel"),
    )(*ins)


def _attn_consts(n):
    i = _iota2((BLOCK, 2 * BLOCK), 0)
    j = _iota2((BLOCK, 2 * BLOCK), 1)
    dist = i + BLOCK - j
    valid = (dist >= 0) & (dist < WINDOW) & ((n > 0) | (j >= BLOCK))
    return dist.astype(F32), valid


def _attn_block(q4, kk, vv, sinks, dist, valid, kv):
    outs = []
    lane = _iota2((1, HEADS), 1)
    for g in range(GROUP):
        h = kv * GROUP + g
        slope = 2.0 ** (-8.0 * (h + 1) / HEADS)
        s = _nt(q4[:, g * HDIM:(g + 1) * HDIM], kk) * (HDIM ** -0.5)
        s = jnp.where(valid, s - slope * dist, -1e30)
        sink = jnp.sum(jnp.where(lane == h, sinks, 0.0), axis=1, keepdims=True)
        m = jnp.maximum(jnp.max(s, axis=-1, keepdims=True), sink)
        e = jnp.exp(s - m)
        p = e / (jnp.sum(e, axis=-1, keepdims=True) + jnp.exp(sink - m))
        outs.append(_nn(p, vv))
    return tuple(outs)


def _attn_fwd(name, qa, ka, va, sinks):
    def body(q_ref, k_ref, v_ref, s_ref, o_ref, kp, vp):
        kp[0:BLOCK, :] = jnp.zeros((BLOCK, KV_A), F32)
        vp[0:BLOCK, :] = jnp.zeros((BLOCK, KV_A), F32)
        kp[BLOCK:, :] = k_ref[...]
        vp[BLOCK:, :] = v_ref[...]
        sinks_v = s_ref[...]

        def step(n, carry):
            r = pl.multiple_of(n * BLOCK, BLOCK)
            dist, valid = _attn_consts(n)
            k2 = kp[pl.ds(r, 2 * BLOCK), :]
            v2 = vp[pl.ds(r, 2 * BLOCK), :]
            for kv in range(KV_HEADS):
                q4 = q_ref[pl.ds(r, BLOCK), kv * GROUP * HDIM:(kv + 1) * GROUP * HDIM]
                og = _attn_block(q4, k2[:, kv * HDIM:(kv + 1) * HDIM], v2[:, kv * HDIM:(kv + 1) * HDIM], sinks_v,
                                 dist, valid, kv)
                for g in range(GROUP):
                    h = kv * GROUP + g
                    o_ref[pl.ds(r, BLOCK), h * HDIM:(h + 1) * HDIM] = og[g]
            return carry

        lax.fori_loop(0, T // BLOCK, step, 0)

    return pl.pallas_call(
        body, out_shape=_sds((T, Q_A)),
        scratch_shapes=[pltpu.VMEM((T + BLOCK, KV_A), F32), pltpu.VMEM((T + BLOCK, KV_A), F32)], name=name,
        compiler_params=pltpu.CompilerParams(vmem_limit_bytes=VMEM_LIMIT),
    )(qa, ka, va, sinks)


def _attn_bwd(name, qa, ka, va, sinks, do):
    def body(q_ref, k_ref, v_ref, s_ref, do_ref, dq_ref, dk_ref, dv_ref, ds_ref, kp, vp, dkp, dvp):
        kp[0:BLOCK, :] = jnp.zeros((BLOCK, KV_A), F32)
        vp[0:BLOCK, :] = jnp.zeros((BLOCK, KV_A), F32)
        kp[BLOCK:, :] = k_ref[...]
        vp[BLOCK:, :] = v_ref[...]
        dkp[...] = jnp.zeros_like(dkp)
        dvp[...] = jnp.zeros_like(dvp)
        ds_ref[...] = jnp.zeros_like(ds_ref)
        sinks_v = s_ref[...]

        def step(n, carry):
            r = pl.multiple_of(n * BLOCK, BLOCK)
            dist, valid = _attn_consts(n)
            k2 = kp[pl.ds(r, 2 * BLOCK), :]
            v2 = vp[pl.ds(r, 2 * BLOCK), :]
            for kv in range(KV_HEADS):
                cols = slice(kv * HDIM, (kv + 1) * HDIM)
                q4 = q_ref[pl.ds(r, BLOCK), kv * GROUP * HDIM:(kv + 1) * GROUP * HDIM]
                _, vjp = jax.vjp(lambda q, k, v, s: _attn_block(q, k, v, s, dist, valid, kv),
                                 q4, k2[:, cols], v2[:, cols], sinks_v)
                cts = tuple(do_ref[pl.ds(r, BLOCK), (kv * GROUP + g) * HDIM:(kv * GROUP + g + 1) * HDIM]
                            for g in range(GROUP))
                dq4, dkk, dvv, dsk = vjp(cts)
                dq_ref[pl.ds(r, BLOCK), kv * GROUP * HDIM:(kv + 1) * GROUP * HDIM] = dq4
                dkp[pl.ds(r, 2 * BLOCK), cols] += dkk
                dvp[pl.ds(r, 2 * BLOCK), cols] += dvv
                ds_ref[...] += dsk
            return carry

        lax.fori_loop(0, T // BLOCK, step, 0)
        dk_ref[...] = dkp[BLOCK:, :]
        dv_ref[...] = dvp[BLOCK:, :]

    pad = lambda: pltpu.VMEM((T + BLOCK, KV_A), F32)
    return pl.pallas_call(
        body, out_shape=[_sds((T, Q_A)), _sds((T, KV_A)), _sds((T, KV_A)), _sds((1, HEADS))],
        scratch_shapes=[pad(), pad(), pad(), pad()], name=name,
        compiler_params=pltpu.CompilerParams(vmem_limit_bytes=VMEM_LIMIT),
    )(qa, ka, va, sinks, do)


def _dn_consts():
    i = _iota2((CHUNK, CHUNK), 0)
    j = _iota2((CHUNK, CHUNK), 1)
    return dict(causal=i >= j, strict=i > j, ltri=(i >= j).astype(F32),
                last=(_iota2((CHUNK, 1), 0) == CHUNK - 1).astype(F32))


def _l2norm(x):
    return x * lax.rsqrt(jnp.sum(x * x, axis=-1, keepdims=True) + EPS)


def _head_cols(m):
    lane = _iota2((1, HEADS), 1)
    return jnp.concatenate([jnp.sum(jnp.where(lane == h, m, 0.0), axis=1, keepdims=True)[None]
                            for h in range(HEADS)], axis=0)


@jax.custom_vjp
def _unit_lower_inverse(low, known):
    if known is not None:
        return known
    inv = (_iota2((CHUNK, CHUNK), 0) == _iota2((CHUNK, CHUNK), 1)).astype(F32) - low
    pw = low
    for _ in range(5):
        pw = _dg(pw, pw, 1, 0, True)
        inv = inv + _dg(inv, pw, 1, 0, True)
    return inv


def _unit_lower_inverse_fwd(low, known):
    inv = _unit_lower_inverse(low, known)
    return inv, (inv, known)


def _unit_lower_inverse_bwd(res, g):
    inv, known = res
    d_low = -_dg(inv, _dg(g, inv, 1, 1, True), 0, 0, True)
    return d_low, (None if known is None else jnp.zeros_like(known))


_unit_lower_inverse.defvjp(_unit_lower_inverse_fwd, _unit_lower_inverse_bwd)


def _dn_local(q3, k3, v3, braw, araw, alog, dtb, cs, known_inv=None):
    q = _l2norm(q3) * (HDIM ** -0.5)
    k = _l2norm(k3)
    g = -jnp.exp(alog) * jax.nn.softplus(araw + dtb)
    gc_all = _nn_hi(cs["ltri"], g)
    egc_all = jnp.exp(gc_all)
    beta, gc, egc = _head_cols(jax.nn.sigmoid(braw)), _head_cols(gc_all), _head_cols(egc_all)
    a = jnp.broadcast_to(gc, (HEADS, CHUNK, CHUNK))
    diff = a - jnp.swapaxes(a, 1, 2)
    decay = jnp.where(cs["causal"], jnp.exp(jnp.where(cs["causal"], diff, 0.0)), 0.0)
    kb = k * beta
    low = jnp.where(cs["strict"], _nt(kb, k) * decay, 0.0)
    inv = _unit_lower_inverse(low, known_inv)
    u = _nn_hi(inv, v3 * beta)
    w = _nn_hi(inv, kb * egc)
    attn = _nt(q, k) * decay
    gc_last = jnp.sum(gc * cs["last"], axis=1, keepdims=True)
    return u, w, attn, q * egc, k * jnp.exp(gc_last - gc), egc_all, inv


def _heads3(ref, off=0):
    return jnp.concatenate([ref[:, off + h * HDIM:off + (h + 1) * HDIM][None] for h in range(HEADS)], axis=0)


def _dn_local_fwd(name, qkv, ba, alog, dtb):
    def body(qkv_ref, ba_ref, al_ref, dt_ref, u_ref, w_ref, at_ref, qd_ref, kd_ref, eg_ref, inv_ref):
        bav = ba_ref[...]
        outs = _dn_local(_heads3(qkv_ref), _heads3(qkv_ref, 512), _heads3(qkv_ref, 1024), bav[:, :HEADS],
                         bav[:, HEADS:], al_ref[...], dt_ref[...], _dn_consts())
        for r, o in zip((u_ref, w_ref, at_ref, qd_ref, kd_ref, inv_ref), outs[:5] + outs[6:]):
            _unheads(r, o)
        eg_ref[...] = outs[5]

    row = lambda w_: pl.BlockSpec((CHUNK, w_), lambda n: (n, 0))
    return pl.pallas_call(
        body, grid=(NCHUNK,), in_specs=[row(QKV_B), row(2 * HEADS), _full((1, HEADS)), _full((1, HEADS))],
        out_specs=[row(V_B)] * 5 + [row(HEADS), row(V_B)],
        out_shape=[_sds((T, V_B))] * 5 + [_sds((T, HEADS)), _sds((T, V_B))], name=name,
        compiler_params=_cp("parallel"),
    )(qkv, ba, alog, dtb)


def _dn_local_bwd(name, qkv, ba, alog, dtb, inv, cts):
    def body(qkv_ref, ba_ref, al_ref, dt_ref, inv_ref, du_ref, dw_ref, dat_ref, dqd_ref, dkd_ref, deg_ref,
             dqkv_ref, dba_ref, dal_ref, ddt_ref):
        @pl.when(pl.program_id(0) == 0)
        def _():
            dal_ref[...] = jnp.zeros_like(dal_ref)
            ddt_ref[...] = jnp.zeros_like(ddt_ref)

        cs = _dn_consts()
        bav = ba_ref[...]
        known = _heads3(inv_ref)
        _, vjp = jax.vjp(lambda *a: _dn_local(*a, cs, known)[:6], _heads3(qkv_ref), _heads3(qkv_ref, 512),
                         _heads3(qkv_ref, 1024), bav[:, :HEADS], bav[:, HEADS:], al_ref[...], dt_ref[...])
        dq, dk, dv, dbr, dar, dal, ddt = vjp((_heads3(du_ref), _heads3(dw_ref), _heads3(dat_ref), _heads3(dqd_ref),
                                              _heads3(dkd_ref), deg_ref[...]))
        for h in range(HEADS):
            dqkv_ref[:, h * HDIM:(h + 1) * HDIM] = dq[h]
            dqkv_ref[:, 512 + h * HDIM:512 + (h + 1) * HDIM] = dk[h]
            dqkv_ref[:, 1024 + h * HDIM:1024 + (h + 1) * HDIM] = dv[h]
        dba_ref[:, :HEADS] = dbr
        dba_ref[:, HEADS:] = dar
        dal_ref[...] += dal
        ddt_ref[...] += ddt

    row = lambda w_: pl.BlockSpec((CHUNK, w_), lambda n: (n, 0))
    return pl.pallas_call(
        body, grid=(NCHUNK,),
        in_specs=[row(QKV_B), row(2 * HEADS), _full((1, HEADS)), _full((1, HEADS))] + [row(V_B)] * 6 + [row(HEADS)],
        out_specs=[row(QKV_B), row(2 * HEADS), _full((1, HEADS)), _full((1, HEADS))],
        out_shape=[_sds((T, QKV_B)), _sds((T, 2 * HEADS)), _sds((1, HEADS)), _sds((1, HEADS))], name=name,
        compiler_params=_cp("arbitrary"),
    )(qkv, ba, alog, dtb, inv, *cts)


def _dn_step(s, u, w, attn, qd, kd, egc, z, nw):
    last = (_iota2((CHUNK, 1), 0) == CHUNK - 1).astype(F32)
    gl = jnp.sum(_head_cols(egc) * last, axis=1, keepdims=True)
    v_new = u - _nn(w, s)
    o = _nn(qd, s) + _nn(attn, v_new)
    s_new = s * gl + _tn(kd, v_new)
    return s_new, _rms(o, nw) * _silu(z)


def _unheads(ref, v3):
    for h in range(HEADS):
        ref[:, h * HDIM:(h + 1) * HDIM] = v3[h]


def _dn_rec_fwd(name, u, w, attn, qd, kd, egc, z, nw):
    def body(u_ref, w_ref, at_ref, qd_ref, kd_ref, eg_ref, z_ref, nw_ref, o_ref, ss_ref, s_scr):
        @pl.when(pl.program_id(0) == 0)
        def _():
            s_scr[...] = jnp.zeros_like(s_scr)

        s = s_scr[...]
        ss_ref[...] = s
        s_new, on = _dn_step(s, _heads3(u_ref), _heads3(w_ref), _heads3(at_ref), _heads3(qd_ref), _heads3(kd_ref),
                             eg_ref[...], _heads3(z_ref), nw_ref[...])
        s_scr[...] = s_new
        _unheads(o_ref, on)

    row = lambda w_: pl.BlockSpec((CHUNK, w_), lambda n: (n, 0))
    return pl.pallas_call(
        body, grid=(NCHUNK,), in_specs=[row(V_B)] * 5 + [row(HEADS), row(V_B), _full((1, HDIM))],
        out_specs=[row(V_B), pl.BlockSpec((None, HEADS, HDIM, HDIM), lambda n: (n, 0, 0, 0))],
        out_shape=[_sds((T, V_B)), _sds((NCHUNK, HEADS, HDIM, HDIM))],
        scratch_shapes=[pltpu.VMEM((HEADS, HDIM, HDIM), F32)], name=name, compiler_params=_cp("arbitrary"),
    )(u, w, attn, qd, kd, egc, z, nw)


def _dn_rec_bwd(name, u, w, attn, qd, kd, egc, z, nw, ss, do):
    def body(u_ref, w_ref, at_ref, qd_ref, kd_ref, eg_ref, z_ref, nw_ref, ss_ref, do_ref,
             du_ref, dw_ref, dat_ref, dqd_ref, dkd_ref, deg_ref, dz_ref, dnw_ref, ds_scr):
        @pl.when(pl.program_id(0) == 0)
        def _():
            ds_scr[...] = jnp.zeros_like(ds_scr)
            dnw_ref[...] = jnp.zeros_like(dnw_ref)

        _, vjp = jax.vjp(_dn_step, ss_ref[...], _heads3(u_ref), _heads3(w_ref), _heads3(at_ref), _heads3(qd_ref),
                         _heads3(kd_ref), eg_ref[...], _heads3(z_ref), nw_ref[...])
        ds, du, dw, dat, dqd, dkd, deg, dz, dnw = vjp((ds_scr[...], _heads3(do_ref)))
        ds_scr[...] = ds
        for r, v in zip((du_ref, dw_ref, dat_ref, dqd_ref, dkd_ref, dz_ref), (du, dw, dat, dqd, dkd, dz)):
            _unheads(r, v)
        deg_ref[...] = deg
        dnw_ref[...] += dnw

    row = lambda w_: pl.BlockSpec((CHUNK, w_), lambda n: (NCHUNK - 1 - n, 0))
    return pl.pallas_call(
        body, grid=(NCHUNK,),
        in_specs=[row(V_B)] * 5 + [row(HEADS), row(V_B), _full((1, HDIM)),
                                   pl.BlockSpec((None, HEADS, HDIM, HDIM), lambda n: (NCHUNK - 1 - n, 0, 0, 0)),
                                   row(V_B)],
        out_specs=[row(V_B)] * 5 + [row(HEADS), row(V_B), _full((1, HDIM))],
        out_shape=[_sds((T, V_B))] * 5 + [_sds((T, HEADS)), _sds((T, V_B)), _sds((1, HDIM))],
        scratch_shapes=[pltpu.VMEM((HEADS, HDIM, HDIM), F32)], name=name, compiler_params=_cp("arbitrary"),
    )(u, w, attn, qd, kd, egc, z, nw, ss, do)


def _final(name, x, fw, target, tm=512):
    def body(x_ref, fw_ref, t_ref, l_ref, dx_ref, dfw_ref):
        @pl.when(pl.program_id(0) == 0)
        def _():
            l_ref[...] = jnp.zeros_like(l_ref)
            dfw_ref[...] = jnp.zeros_like(dfw_ref)

        tv = t_ref[...]

        def f(xv, fwv):
            err = _rms(xv, fwv) - tv
            per_tok = jnp.mean(err * err, axis=-1, keepdims=True)
            return 0.5 * jnp.sum(per_tok, axis=0, keepdims=True)

        loss, vjp = jax.vjp(f, x_ref[...], fw_ref[...])
        dx, dfw = vjp(jnp.ones((1, 1), F32))
        l_ref[...] += loss
        dx_ref[...] = dx
        dfw_ref[...] += dfw

    tok = pl.BlockSpec((tm, D), lambda i: (i, 0))
    return pl.pallas_call(
        body, grid=(T // tm,), in_specs=[tok, _full((1, D)), tok], out_specs=[_full((1, 1)), tok, _full((1, D))],
        out_shape=[_sds((1, 1)), _sds((T, D)), _sds((1, D))], name=name, compiler_params=_cp("arbitrary"),
    )(x, fw, target)


def _m1_pre(tv, sv):
    return [_rms(tv[0], sv[0])]


def _m1_post(ys, tv, sv):
    return (jnp.concatenate(ys, axis=1),)


def _m1_post_split(ys, tv, sv):
    proj = jnp.concatenate(ys, axis=1)
    return tuple(proj[:, a:b] for a, b in zip(IN_SPLITS[:-1], IN_SPLITS[1:]))


def _m5_pre(tv, sv):
    return [tv[1], tv[2]]


def _m5_post(ys, tv, sv):
    return (tv[0] + ys[0] + ys[1],)


def _c1_pre(tv, sv):
    return [_rms(tv[0], sv[0])]


def _c1_post(ys, tv, sv):
    return ((jnp.concatenate(ys[:2], axis=1) + sv[1]) * jax.nn.sigmoid(jnp.concatenate(ys[2:], axis=1) + sv[2]),)


def _c3_pre(tv, sv):
    return [_silu(_layernorm(tv[0], sv[0], sv[1]))]


def _c3_post(ys, tv, sv):
    return (tv[1] + ys[0] + sv[2],)


def _row(v):
    return v.reshape(1, -1)


def _mixer_fwd(tag, x, p):
    parts = _blk_fwd(f"m1_fwd_{tag}", _m1_pre, [0], _m1_post_split, [x], [p["nw"]], [p["w_in"]],
                     [(b - a, F32) for a, b in zip(IN_SPLITS[:-1], IN_SPLITS[1:])])
    qa, ka, va, qkvb, z, ba = parts
    att = _attn_fwd(f"attn_fwd_{tag}", qa, ka, va, p["sinks"])
    qkvc = _conv_fwd(f"dnconv_fwd_{tag}", qkvb, p["dn_conv_w"], None, True)
    *loc, inv = _dn_local_fwd(f"dnloc_fwd_{tag}", qkvc, ba, p["a_log"], p["dt_bias"])
    og, ss = _dn_rec_fwd(f"dnrec_fwd_{tag}", *loc, z, p["dn_norm_w"])
    (out,) = _blk_fwd(f"m5_fwd_{tag}", _m5_pre, [0, 1], _m5_post, [x, att, og], [], [p["wo_a"], p["wo_b"]],
                      [(D, F32)])
    return out, dict(x=x, qa=qa, ka=ka, va=va, qkvb=qkvb, z=z, ba=ba, att=att, qkvc=qkvc, loc=loc, inv=inv, og=og,
                     ss=ss)


def _mixer_bwd(tag, dy, p, s):
    (dxa, datt, dog), _, (dwo_a, dwo_b) = _blk_bwd(f"m5_bwd_{tag}", _m5_pre, [0, 1], _m5_post,
                                                   [s["x"], s["att"], s["og"]], [], [p["wo_a"], p["wo_b"]], [[dy]],
                                                   linear_post=True)
    rec = _dn_rec_bwd(f"dnrec_bwd_{tag}", *s["loc"], s["z"], p["dn_norm_w"], s["ss"], dog)
    dz, dnw_dn = rec[6], rec[7]
    dqkvc, dba, dalog, ddtb = _dn_local_bwd(f"dnloc_bwd_{tag}", s["qkvc"], s["ba"], p["a_log"], p["dt_bias"],
                                            s["inv"], rec[:6])
    dqkvb, dconvw, _ = _conv_bwd(f"dnconv_bwd_{tag}", s["qkvb"], p["dn_conv_w"], None, True, dqkvc)
    dqa, dka, dva, dsinks = _attn_bwd(f"attn_bwd_{tag}", s["qa"], s["ka"], s["va"], p["sinks"], datt)
    (dx,), (dnw,), (dw_in,) = _blk_bwd(f"m1_bwd_{tag}", _m1_pre, [0], _m1_post, [s["x"]], [p["nw"]], [p["w_in"]],
                                       [[dqa, dka, dva, dqkvb, dz, dba]], res=dxa, linear_post=True)
    return dx, dict(nw=dnw, w_in=dw_in, wo_a=dwo_a, wo_b=dwo_b, dn_conv_w=dconvw, sinks=dsinks, a_log=dalog,
                    dt_bias=ddtb, dn_norm_w=dnw_dn)


def _conformer_fwd(tag, x, p):
    (glu,) = _blk_fwd(f"c1_fwd_{tag}", _c1_pre, [0], _c1_post, [x], [p["nw"], p["b1a"], p["b1b"]], [p["w1"]],
                      [(D, F32)])
    cc = _conv_fwd(f"dwconv_fwd_{tag}", glu, p["w_dw"], p["b_dw"], False)
    (out,) = _blk_fwd(f"c3_fwd_{tag}", _c3_pre, [0], _c3_post, [cc, x], [p["ln_w"], p["ln_b"], p["b2"]], [p["w2"]],
                      [(D, F32)])
    return out, dict(x=x, glu=glu, cc=cc)


def _conformer_bwd(tag, dy, p, s):
    (dcc, dxa), (dlnw, dlnb, db2), (dw2,) = _blk_bwd(f"c3_bwd_{tag}", _c3_pre, [0], _c3_post, [s["cc"], s["x"]],
                                                     [p["ln_w"], p["ln_b"], p["b2"]], [p["w2"]], [[dy]],
                                                     linear_post=True)
    dglu, dwdw, dbdw = _conv_bwd(f"dwconv_bwd_{tag}", s["glu"], p["w_dw"], p["b_dw"], False, dcc)
    (dx,), (dnw, db1a, db1b), (dw1,) = _blk_bwd(f"c1_bwd_{tag}", _c1_pre, [0], _c1_post, [s["x"]],
                                                [p["nw"], p["b1a"], p["b1b"]], [p["w1"]], [[dglu]], res=dxa)
    return dx, dict(nw=dnw, b1a=db1a, b1b=db1b, w1=dw1, w_dw=dwdw, b_dw=dbdw, ln_w=dlnw, ln_b=dlnb, b2=db2, w2=dw2)


def _layer_fwd(l, x, nw, ffn_a, get_ffn_b, p):
    x1, *pre_a = _ffn_fwd(f"ffn_fwd_{l}a", x, _row(nw[0]), ffn_a, 0)
    p = dict(p, nw=_row(nw[1]))
    x2, sv = (_mixer_fwd if l % 2 == 0 else _conformer_fwd)(str(l), x1, p)
    x2, ffn_b = get_ffn_b(x2)
    out, *pre_b = _ffn_fwd(f"ffn_fwd_{l}b", x2, _row(nw[2]), ffn_b, 0)
    return out, (x, x2, p, sv, pre_a, pre_b, ffn_a, ffn_b)


def _layer_bwd(l, dx, nw, saved, after_first=lambda dx: dx):
    x0, x2, p, sv, pre_a, pre_b, ffn_a, ffn_b = saved
    dx, dn2, dffn = _ffn_bwd(f"ffn_bwd_{l}b", x2, _row(nw[2]), ffn_b, 1, pre_b, dx)
    dx = after_first(dx)
    dx, dmix = (_mixer_bwd if l % 2 == 0 else _conformer_bwd)(str(l), dx, p, sv)
    dx, dn0, dffn = _ffn_bwd(f"ffn_bwd_{l}a", x0, _row(nw[0]), ffn_a, 0, pre_a, dx, dffn)
    return dx, jnp.concatenate([dn0, dmix.pop("nw"), dn2], axis=0), dffn, dmix


def _place(staggered=False):
    x, y, c = lax.axis_index("x"), lax.axis_index("y"), lax.axis_index("c")
    s = c if staggered else 0
    first, second = (x + (1 - s) * (1 - 2 * x), y + s * (1 - 2 * y)), (x + s * (1 - 2 * x), y + (1 - s) * (1 - 2 * y))
    chips = [first, second, (1 - x, 1 - y)]
    return x, y, c, 2 * x + y, chips, [2 * px + py for px, py in chips]


def _handshake(peers):
    barrier = pltpu.get_barrier_semaphore()
    for p in peers:
        pl.semaphore_signal(barrier, inc=1, device_id=p, device_id_type=MESH)
    pl.semaphore_wait(barrier, len(peers))


def _chip_peers():
    x, y, c, _, chips, _ = _place()
    return [(*chip, c) for chip in chips] + [(x, y, 1 - c)]


def _gather_copies(ins, outs, nb, send, recv, fsend, frecv, lsem, twins=None):
    n_in = len(ins)
    x, y, c, me, chips, cidx = _place(staggered=True)
    sib = (x, y, 1 - c)
    local = [pltpu.make_async_copy(ins[a], outs[a].at[me], lsem.at[a]) for a in range(n_in)]

    def region(a, k, who):
        if k < 2:
            return outs[a].at[cidx[k], pl.ds(who, 1)]
        r = ins[a].shape[1] // 2
        return outs[a].at[cidx[2], pl.ds(who, 1), pl.ds((k - 2) * r, r)]

    def hop(a, k):
        if k < 2:
            src, dst = (ins if k == 0 or twins is None else twins)[a].at[pl.ds(c, 1)], outs[a].at[me, pl.ds(c, 1)]
        else:
            r = ins[a].shape[1] // 2
            src = dst = outs[a].at[cidx[3 - k], pl.ds(c, 1), pl.ds((k - 2) * r, r)]
        return pltpu.make_async_remote_copy(src, dst, send.at[4 * a + k], recv.at[4 * a + k],
                                            device_id=(*chips[k % 2], c), device_id_type=MESH)

    def landed(a, k):
        dst = region(a, k, c)
        return pltpu.make_async_remote_copy(dst, dst, send.at[4 * a + k], recv.at[4 * a + k],
                                            device_id=(*chips[k % 2], c), device_id_type=MESH)

    def passed(a, k, who):
        part = region(a, k, who)
        return pltpu.make_async_remote_copy(part, part, fsend.at[4 * a + k], frecv.at[4 * a + k], device_id=sib,
                                            device_id_type=MESH)

    def direct(a, j):
        k = 4 * nb + 3 * (a - nb) + j
        return pltpu.make_async_remote_copy(ins[a], outs[a].at[me], send.at[k], recv.at[k],
                                            device_id=(*chips[j], c), device_id_type=MESH)

    def direct_landed(a, j):
        k = 4 * nb + 3 * (a - nb) + j
        dst = outs[a].at[cidx[j]]
        return pltpu.make_async_remote_copy(dst, dst, send.at[k], recv.at[k], device_id=(*chips[j], c),
                                            device_id_type=MESH)

    sends = [hop(a, k) for a in range(nb) for k in range(2)] + [direct(a, j) for a in range(nb, n_in) for j in range(3)]
    for cp in sends:
        cp.start()
    for cp in local:
        cp.start()
    for a in range(nb):
        for k in (1, 0):
            landed(a, k).wait_recv()
            for cp in (hop(a, 3 - k), passed(a, k, c)):
                cp.start()
                sends.append(cp)
    for a in range(nb):
        for k in (2, 3):
            landed(a, k).wait_recv()
            cp = passed(a, k, c)
            cp.start()
            sends.append(cp)
    for a in range(nb, n_in):
        for j in range(3):
            direct_landed(a, j).wait_recv()
    for a in range(nb):
        for k in range(4):
            passed(a, k, 1 - c).wait_recv()
    for cp in sends:
        cp.wait_send()
    for cp in local:
        cp.wait()


def _gather_sems(n_in, nb):
    dma = pltpu.SemaphoreType.DMA
    n_ici = 4 * nb + 3 * (n_in - nb)
    return [dma((n_ici,)), dma((n_ici,)), dma((4 * nb,)), dma((4 * nb,)), dma((n_in,))]


def _gather_async(name, halved, whole=()):
    nb, arrs = len(halved), list(halved) + list(whole)
    hbm = pltpu.MemorySpace.HBM
    ins = [jax.new_ref(a, memory_space=hbm) for a in arrs]
    twins = [jax.new_ref(a, memory_space=hbm) for a in halved]
    outs = [jax.empty_ref(_sds((NCHIP,) + a.shape, a.dtype), memory_space=hbm) for a in arrs]

    @pl.kernel(mesh=plsc.ScalarSubcoreMesh(axis_name="seq", num_cores=1), name=name,
               scratch_types=tuple(_gather_sems(len(arrs), nb)),
               compiler_params=pltpu.CompilerParams(collective_id=2))
    def launch(send, recv, fsend, frecv, lsem):
        _handshake(_chip_peers())
        _gather_copies(ins, outs, nb, send, recv, fsend, frecv, lsem, twins)

    launch()
    return outs


def _swap_halves(name, grads, after=None):
    n = len(grads)
    hbm = pltpu.MemorySpace.HBM
    ins = [jax.new_ref(g, memory_space=hbm) for g in grads]
    outs = [jax.empty_ref(_sds((NCHIP, g.shape[1] // 2) + g.shape[2:], g.dtype), memory_space=hbm) for g in grads]
    tile = (2 * 8, LANES)
    token = None if after is None else jax.empty_ref(_sds(tile, BF16), memory_space=hbm)

    @pl.kernel(mesh=plsc.ScalarSubcoreMesh(axis_name="seq", num_cores=1), name=name,
               scratch_types=(pltpu.SemaphoreType.DMA((n + 1,)), pltpu.SemaphoreType.DMA((n,))),
               compiler_params=pltpu.CompilerParams(collective_id=1))
    def launch(send, recv):
        x, y, c, _, _, _ = _place()
        sib = (x, y, 1 - c)
        _handshake([sib])
        if after is not None:
            tick = pltpu.make_async_copy(after.at[0, 0, 0, pl.ds(0, tile[0]), pl.ds(0, tile[1])], token, send.at[n])
            tick.start()
            tick.wait()
        cps = []
        for a in range(n):
            h = grads[a].shape[1] // 2
            cps.append(pltpu.make_async_remote_copy(ins[a].at[:, pl.ds((1 - c) * h, h)], outs[a], send.at[a],
                                                    recv.at[a], device_id=sib, device_id_type=MESH))
        for cp in cps:
            cp.start()
        for cp in cps:
            cp.wait()

    launch()
    return outs


def _row_tile(r, cap=256):
    return max(t for t in range(8, cap + 1, 8) if r % t == 0)


def _add_half(name, g, r, c_arr):
    _, l, rows, cols = g.shape
    h = l // 2
    tr = _row_tile(rows, 1056)

    def body(c_ref, g_ref, r_ref, o_ref):
        o_ref[...] = (g_ref[...].astype(F32) + r_ref[...].astype(F32)).astype(BF16)

    blk = (None, None, tr, cols)
    return pl.pallas_call(
        body,
        grid_spec=pltpu.PrefetchScalarGridSpec(
            num_scalar_prefetch=1, grid=(NCHIP, h, rows // tr),
            in_specs=[pl.BlockSpec(blk, lambda j, i, t, c_ref: (j, c_ref[0] * h + i, t, 0)),
                      pl.BlockSpec(blk, lambda j, i, t, c_ref: (j, i, t, 0))],
            out_specs=pl.BlockSpec(blk, lambda j, i, t, c_ref: (j, i, t, 0))),
        out_shape=_sds((NCHIP, h, rows, cols), BF16), name=name,
        compiler_params=_cp("parallel", "parallel", "parallel"),
    )(c_arr, g, r)


def _scatter_async(name, parts, sums, where):
    nb = len(parts)
    ins = [jax.new_ref(p, memory_space=pltpu.MemorySpace.HBM) for p in parts]
    dma = pltpu.SemaphoreType.DMA

    @pl.kernel(mesh=plsc.ScalarSubcoreMesh(axis_name="seq", num_cores=1), name=name,
               scratch_types=(dma((3 * nb,)), dma((3 * nb,)), dma((4 * nb,)), dma((4 * nb,)), dma((nb,))),
               compiler_params=pltpu.CompilerParams(collective_id=3))
    def launch(send, recv, fsend, frecv, lsem):
        _handshake(_chip_peers())
        x, y, c, me, chips, cidx = _place(staggered=True)
        sib = (x, y, 1 - c)

        def slot(a, half, chip):
            return sums[a].at[half, chip, pl.ds(where[a], 1)]

        local = [pltpu.make_async_copy(ins[a].at[me], slot(a, c, me), lsem.at[a]) for a in range(nb)]
        for cp in local:
            cp.start()

        def ici(a, j):
            return pltpu.make_async_remote_copy(ins[a].at[cidx[j]], slot(a, c, me), send.at[a * 3 + j],
                                                recv.at[a * 3 + j], device_id=(*chips[j], c), device_id_type=MESH)

        def landed(a, j):
            dst = slot(a, c, cidx[j])
            return pltpu.make_async_remote_copy(dst, dst, send.at[a * 3 + j], recv.at[a * 3 + j],
                                                device_id=(*chips[j], c), device_id_type=MESH)

        def passed(a, j, who):
            dst = slot(a, who, me if j == 3 else cidx[j])
            src = ins[a].at[me] if j == 3 else dst
            return pltpu.make_async_remote_copy(src, dst, fsend.at[a * 4 + j], frecv.at[a * 4 + j], device_id=sib,
                                                device_id_type=MESH)

        sends = [ici(a, j) for a in range(nb) for j in range(3)] + [passed(a, 3, c) for a in range(nb)]
        for cp in sends:
            cp.start()
        for a in range(nb):
            for j in range(3):
                landed(a, j).wait_recv()
                cp = passed(a, j, c)
                cp.start()
                sends.append(cp)
        for a in range(nb):
            for j in range(4):
                passed(a, j, 1 - c).wait_recv()
        for cp in sends:
            cp.wait_send()
        for cp in local:
            cp.wait()

    launch()


def _exchange_small(small, rep):
    def body(small_in, rep_in, small_out, rep_out, lsem, ssend, srecv):
        x, y, c, me, _, _ = _place()
        dev = 4 * x + 2 * y + c
        local = [pltpu.make_async_copy(small_in.at[me], small_out.at[dev], lsem.at[0]),
                 pltpu.make_async_copy(rep_in, rep_out.at[dev], lsem.at[1])]
        for cp in local:
            cp.start()

        def peer(r):
            return (1 - x if r & 4 else x), (1 - y if r & 2 else y), (1 - c if r & 1 else c)

        def tiny(r, which):
            px, py, pc = peer(r)
            k = (r - 1) * 2 + which
            if which == 0:
                return pltpu.make_async_remote_copy(small_in.at[2 * px + py], small_out.at[dev], ssend.at[k],
                                                    srecv.at[k], device_id=(px, py, pc), device_id_type=MESH)
            return pltpu.make_async_remote_copy(rep_in, rep_out.at[dev], ssend.at[k], srecv.at[k],
                                                device_id=(px, py, pc), device_id_type=MESH)

        def tiny_landed(r, which):
            px, py, pc = peer(r)
            k = (r - 1) * 2 + which
            dst = (small_out if which == 0 else rep_out).at[4 * px + 2 * py + pc]
            return pltpu.make_async_remote_copy(dst, dst, ssend.at[k], srecv.at[k], device_id=(px, py, pc),
                                                device_id_type=MESH)

        sends = [tiny(r, w) for r in range(1, NDEV) for w in range(2)]
        for cp in sends:
            cp.start()
        for r in range(1, NDEV):
            for w in range(2):
                tiny_landed(r, w).wait_recv()
        for cp in sends:
            cp.wait_send()
        for cp in local:
            cp.wait()

    dma = pltpu.SemaphoreType.DMA
    return pl.pallas_call(
        body, in_specs=[ANY] * 2, out_specs=[ANY] * 2,
        out_shape=[_sds((NDEV,) + small.shape[1:], F32), _sds((NDEV,) + rep.shape, F32)],
        scratch_shapes=[dma((2,)), dma((2 * (NDEV - 1),)), dma((2 * (NDEV - 1),))], name="exchange_small_grads",
    )(small, rep)


def _adamw_math(w, g, m, v):
    m = B1 * m + (1.0 - B1) * g
    v = B2 * v + (1.0 - B2) * (g * g)
    m_hat = m / (1.0 - B1 ** STEP)
    v_hat = v / (1.0 - B2 ** STEP)
    return -LR * (m_hat / (jnp.sqrt(v_hat) + AEPS) + WD * w), m, v


def _adamw_big(name, w, m, v, parts, row0=0, first=0, outs=None):
    _, _, rows, cols = w.shape
    n = parts.shape[2]
    tr = _row_tile(rows)
    t0 = row0 // tr

    def body(w_ref, m_ref, v_ref, p_ref, *rest):
        g_ref, d_ref, nm_ref, nv_ref = rest[-4:]
        g = p_ref[0].astype(F32)
        for q in range(1, NCHIP):
            g = g + p_ref[q].astype(F32)
        d, nm, nv = _adamw_math(w_ref[...], g, m_ref[...], v_ref[...])
        g_ref[...], d_ref[...], nm_ref[...], nv_ref[...] = g, d, nm, nv

    spec = pl.BlockSpec((None, None, tr, cols), lambda i, p, t: (first + i, p, t, 0))
    na = 0 if outs is None else 4
    return pl.pallas_call(
        body, grid=(n, 2, rows // tr),
        in_specs=[spec, spec, spec,
                  pl.BlockSpec((None, NCHIP, None, tr, cols), lambda i, p, t: (p, 0, i, t0 + t, 0))] + [ANY] * na,
        out_specs=[spec] * 4, out_shape=[_sds(w.shape)] * 4, input_output_aliases={4 + k: k for k in range(na)},
        name=name, compiler_params=_cp("parallel", "parallel", "parallel"),
    )(w, m, v, parts, *(outs or ()))


def _adamw_small(name, w, m, v, parts):
    def body(w_ref, m_ref, v_ref, p_ref, g_ref, d_ref, nm_ref, nv_ref):
        g = p_ref[0]
        for q in range(1, NDEV):
            g = g + p_ref[q]
        d, nm, nv = _adamw_math(w_ref[...], g, m_ref[...], v_ref[...])
        g_ref[...], d_ref[...], nm_ref[...], nv_ref[...] = g, d, nm, nv

    return pl.pallas_call(body, out_shape=[_sds(w.shape)] * 4, name=name)(w, m, v, parts)


def _pack(arrs, rows):
    flat = jnp.concatenate([a.reshape(-1) for a in arrs])
    return jnp.pad(flat, (0, rows * LANES - flat.shape[0])).reshape(rows, LANES)


def _unpack(packed, shapes):
    flat, out, o = packed.reshape(-1), [], 0
    for s in shapes:
        n = 1
        for d in s:
            n *= d
        out.append(flat[o:o + n].reshape(s))
        o += n
    return out


SMALL_ROWS, REP_ROWS = 200, 16


def kernel(x, norm_w, ffn_w_gate, ffn_w_up, ffn_w_down, mix_w_in, dn_conv_w, attn_sinks, dn_a_log, dn_dt_bias, dn_norm_w, mix_w_out, conv_w_pw1, conv_b_pw1, conv_w_dw, conv_b_dw, conv_ln_w, conv_ln_b, conv_w_pw2, conv_b_pw2, final_norm_w, loss_target, m_norm_w, m_ffn_w_gate, m_ffn_w_up, m_ffn_w_down, m_mix_w_in, m_dn_conv_w, m_attn_sinks, m_dn_a_log, m_dn_dt_bias, m_dn_norm_w, m_mix_w_out, m_conv_w_pw1, m_conv_b_pw1, m_conv_w_dw, m_conv_b_dw, m_conv_ln_w, m_conv_ln_b, m_conv_w_pw2, m_conv_b_pw2, m_final_norm_w, v_norm_w, v_ffn_w_gate, v_ffn_w_up, v_ffn_w_down, v_mix_w_in, v_dn_conv_w, v_attn_sinks, v_dn_a_log, v_dn_dt_bias, v_dn_norm_w, v_mix_w_out, v_conv_w_pw1, v_conv_b_pw1, v_conv_w_dw, v_conv_b_dw, v_conv_ln_w, v_conv_ln_b, v_conv_w_pw2, v_conv_b_pw2, v_final_norm_w):
    small_names = ["norm_w", "dn_conv_w", "conv_b_pw1", "conv_w_dw", "conv_b_dw", "conv_ln_w", "conv_ln_b",
                   "conv_b_pw2"]
    rep_names = ["attn_sinks", "dn_a_log", "dn_dt_bias", "dn_norm_w", "final_norm_w"]
    w = dict(norm_w=norm_w, ffn_w_gate=ffn_w_gate, ffn_w_up=ffn_w_up, ffn_w_down=ffn_w_down, mix_w_in=mix_w_in, dn_conv_w=dn_conv_w, attn_sinks=attn_sinks, dn_a_log=dn_a_log, dn_dt_bias=dn_dt_bias, dn_norm_w=dn_norm_w, mix_w_out=mix_w_out, conv_w_pw1=conv_w_pw1, conv_b_pw1=conv_b_pw1, conv_w_dw=conv_w_dw, conv_b_dw=conv_b_dw, conv_ln_w=conv_ln_w, conv_ln_b=conv_ln_b, conv_w_pw2=conv_w_pw2, conv_b_pw2=conv_b_pw2, final_norm_w=final_norm_w)
    m = dict(norm_w=m_norm_w, ffn_w_gate=m_ffn_w_gate, ffn_w_up=m_ffn_w_up, ffn_w_down=m_ffn_w_down, mix_w_in=m_mix_w_in, dn_conv_w=m_dn_conv_w, attn_sinks=m_attn_sinks, dn_a_log=m_dn_a_log, dn_dt_bias=m_dn_dt_bias, dn_norm_w=m_dn_norm_w, mix_w_out=m_mix_w_out, conv_w_pw1=m_conv_w_pw1, conv_b_pw1=m_conv_b_pw1, conv_w_dw=m_conv_w_dw, conv_b_dw=m_conv_b_dw, conv_ln_w=m_conv_ln_w, conv_ln_b=m_conv_ln_b, conv_w_pw2=m_conv_w_pw2, conv_b_pw2=m_conv_b_pw2, final_norm_w=m_final_norm_w)
    v = dict(norm_w=v_norm_w, ffn_w_gate=v_ffn_w_gate, ffn_w_up=v_ffn_w_up, ffn_w_down=v_ffn_w_down, mix_w_in=v_mix_w_in, dn_conv_w=v_dn_conv_w, attn_sinks=v_attn_sinks, dn_a_log=v_dn_a_log, dn_dt_bias=v_dn_dt_bias, dn_norm_w=v_dn_norm_w, mix_w_out=v_mix_w_out, conv_w_pw1=v_conv_w_pw1, conv_b_pw1=v_conv_b_pw1, conv_w_dw=v_conv_w_dw, conv_b_dw=v_conv_b_dw, conv_ln_w=v_conv_ln_w, conv_ln_b=v_conv_ln_b, conv_w_pw2=v_conv_w_pw2, conv_b_pw2=v_conv_b_pw2, final_norm_w=v_final_norm_w)
    order = ["norm_w", "ffn_w_gate", "ffn_w_up", "ffn_w_down", "mix_w_in", "dn_conv_w", "attn_sinks", "dn_a_log",
             "dn_dt_bias", "dn_norm_w", "mix_w_out", "conv_w_pw1", "conv_b_pw1", "conv_w_dw", "conv_b_dw",
             "conv_ln_w", "conv_ln_b", "conv_w_pw2", "conv_b_pw2", "final_norm_w"]

    small_shapes = [w[n].shape for n in small_names]
    rep_shapes = [w[n].shape for n in rep_names]

    def halves(a):
        return a.reshape(a.shape[:-2] + (2, a.shape[-2] // 2, a.shape[-1]))

    tr = lambda a: jnp.swapaxes(a, -1, -2)
    gate_t, up_t = tr(ffn_w_gate), tr(ffn_w_up)

    def layer_shards(l):
        mix_in, mix_out = (mix_w_in, mix_w_out) if l % 2 == 0 else (conv_w_pw1, conv_w_pw2)
        ffn = jnp.concatenate([gate_t[l], up_t[l], ffn_w_down[l]], axis=1)
        return ([t.astype(BF16) for t in (halves(ffn[0]), halves(mix_in[l // 2]), halves(mix_out[l // 2]))],
                [halves(ffn[1]).astype(BF16)])

    first = layer_shards(0)
    first = (first[0] + [_pack([w[n] for n in small_names], SMALL_ROWS)], first[1])
    first, (gate_t, up_t, ffn_w_down, mix_w_in, mix_w_out, conv_w_pw1, conv_w_pw2) = lax.optimization_barrier(
        (first, (gate_t, up_t, ffn_w_down, mix_w_in, mix_w_out, conv_w_pw1, conv_w_pw2)))
    gathering = [(_gather_async("gather_layer0a", first[0][:3], first[0][3:]),
                  _gather_async("gather_layer0b", first[1]))]
    for l in range(1, DEPTH):
        before, after = layer_shards(l)
        gathering.append((_gather_async(f"gather_layer{l}a", before), _gather_async(f"gather_layer{l}b", after)))
    ffn_block = lambda g: g.reshape(NCHIP, 1, 3 * FS, D)

    def mixer_params(l, w_a, w_b):
        e = l // 2
        w_a = w_a.reshape(NCHIP, D, -1)
        w_b = w_b.reshape(D, D)
        if l % 2 == 0:
            return dict(w_in=w_a, dn_conv_w=sm["dn_conv_w"][e], sinks=_row(attn_sinks[e]), a_log=_row(dn_a_log[e]),
                        dt_bias=_row(dn_dt_bias[e]), dn_norm_w=_row(dn_norm_w[e]), wo_a=w_b[:Q_A], wo_b=w_b[Q_A:])
        return dict(b1a=_row(sm["conv_b_pw1"][e, :D]), b1b=_row(sm["conv_b_pw1"][e, D:]), w1=w_a,
                    w_dw=sm["conv_w_dw"][e], b_dw=_row(sm["conv_b_dw"][e]), ln_w=_row(sm["conv_ln_w"][e]),
                    ln_b=_row(sm["conv_ln_b"][e]), b2=_row(sm["conv_b_pw2"][e]), w2=w_b)

    xs, saved = x[0], []
    for l in range(DEPTH):
        got = [r[...] for r in gathering[l][0]]
        if l == 0:
            per_chip = [_unpack(got[3][q], small_shapes) for q in range(NCHIP)]
            sm = {n: jnp.concatenate([per_chip[q][i] for q in range(NCHIP)], axis=-1)
                  for i, n in enumerate(small_names)}
        else:
            xs, got = lax.optimization_barrier((xs, got))

        def second_ffn(x2, l=l):
            x2, got_b = lax.optimization_barrier((x2, gathering[l][1][0][...]))
            return x2, ffn_block(got_b)

        xs, sv = _layer_fwd(l, xs, sm["norm_w"][l], ffn_block(got[0]), second_ffn, mixer_params(l, got[1], got[2]))
        saved.append(sv)
    loss, dx, dfw = _final("final", xs, _row(final_norm_w), loss_target[0])

    hbm = pltpu.MemorySpace.HBM
    row_shapes = dict(ffn=(3 * FS, D), w_in=(D // 2, IN_COLS // NCHIP), w_out=(D // 8, D), pw1=(D // 2, D // 2),
                      pw2=(D // 8, D))
    new_sums = lambda k, n: jax.empty_ref(_sds((2, NCHIP, n) + row_shapes[k], BF16), memory_space=hbm)
    sums_0 = {k: new_sums(k, 1) for k in ("ffn", "w_in", "w_out")}
    sums = dict(ffn=new_sums("ffn", DEPTH - 1), w_in=new_sums("w_in", 1), w_out=new_sums("w_out", 1),
                pw1=new_sums("pw1", 2), pw2=new_sums("pw2", 2))
    c_arr = lax.axis_index("c").astype(jnp.int32).reshape(1)
    dnorm, gmix = [None] * DEPTH, [None] * DEPTH

    def hand_on(l, grads, swapped):
        def run(dx):
            dx, other = lax.optimization_barrier((dx, [r[...] for r in swapped]))
            parts = [_add_half(f"add_half_{l}_{k}", gg, rr, c_arr) for k, (gg, rr) in enumerate(zip(grads, other))]
            dx, parts = lax.optimization_barrier((dx, parts))
            keys = ("ffn", "w_in", "w_out") if l % 2 == 0 else ("ffn", "pw1", "pw2")
            if l == 0:
                _scatter_async("scatter_grads_0", parts, [sums_0[k] for k in keys], [0, 0, 0])
            else:
                _scatter_async(f"scatter_grads_{l}", parts, [sums[k] for k in keys],
                               [l - 1, 0, 0] if l % 2 == 0 else [l - 1, l // 2, l // 2])
            return dx
        return run

    pending = lambda dx: dx
    for l in reversed(range(DEPTH)):
        dx, dnorm[l], dffn, gmix[l] = _layer_bwd(l, dx, sm["norm_w"][l], saved[l], pending)
        if l % 2 == 0:
            g_a, g_b = gmix[l]["w_in"], jnp.concatenate([gmix[l]["wo_a"], gmix[l]["wo_b"]], axis=0)
        else:
            g_a, g_b = gmix[l]["w1"], gmix[l]["w2"]
        g_a = halves(g_a).astype(BF16)
        g_b = g_b.reshape(NCHIP, 2, D // 8, D).astype(BF16)
        dx, grads = lax.optimization_barrier((dx, [dffn, g_a, g_b]))
        pending = hand_on(l, grads, _swap_halves(f"swap_grads_{l}", grads, sums["ffn"] if l < DEPTH - 1 else None))
    gm, gc = [gmix[0], gmix[2]], [gmix[1], gmix[3]]
    small_g = dict(
        norm_w=jnp.stack(dnorm), dn_conv_w=jnp.stack([gm[e]["dn_conv_w"] for e in range(2)]),
        conv_b_pw1=jnp.stack([jnp.concatenate([gc[e]["b1a"], gc[e]["b1b"]], axis=1)[0] for e in range(2)]),
        conv_w_dw=jnp.stack([gc[e]["w_dw"] for e in range(2)]),
        conv_b_dw=jnp.stack([gc[e]["b_dw"][0] for e in range(2)]),
        conv_ln_w=jnp.stack([gc[e]["ln_w"][0] for e in range(2)]),
        conv_ln_b=jnp.stack([gc[e]["ln_b"][0] for e in range(2)]),
        conv_b_pw2=jnp.stack([gc[e]["b2"][0] for e in range(2)]))
    small_by_chip = jnp.stack([_pack([jnp.split(small_g[n], NCHIP, axis=-1)[q] for n in small_names], SMALL_ROWS)
                               for q in range(NCHIP)])
    rep_g = _pack([jnp.stack([gm[e]["sinks"][0] for e in range(2)]), jnp.stack([gm[e]["a_log"][0] for e in range(2)]),
                   jnp.stack([gm[e]["dt_bias"][0] for e in range(2)]),
                   jnp.stack([gm[e]["dn_norm_w"][0] for e in range(2)]), dfw[0]], REP_ROWS)
    small_sum, rep_sum = _exchange_small(small_by_chip, rep_g)
    dx, small_sum, rep_sum = lax.optimization_barrier((dx, small_sum, rep_sum))
    dx = pending(dx)

    big = (("ffn_w_gate", "ffn", 0), ("ffn_w_up", "ffn", FS), ("ffn_w_down", "ffn", 2 * FS), ("mix_w_in", "w_in", 0),
           ("mix_w_out", "w_out", 0), ("conv_w_pw1", "pw1", 0), ("conv_w_pw2", "pw2", 0))
    views = {n: (tr, tr) if n in ("ffn_w_gate", "ffn_w_up") else (
        (lambda a: a) if w[n].ndim == 4 else halves, lambda o, n=n: o.reshape(w[n].shape)) for n, _, _ in big}
    partial_sums = {k: r[...] for k, r in sums.items()}
    upper = {}
    for n, key, row0 in big:
        view = views[n][0]
        upper[n] = _adamw_big(f"adamw_{n}", view(w[n]), view(m[n]), view(v[n]), partial_sums[key], row0,
                              first=0 if key in ("pw1", "pw2") else 1)
    upper, partial_sums_0 = lax.optimization_barrier((upper, {k: r[...] for k, r in sums_0.items()}))
    res = {}
    for n, key, row0 in big:
        view, back = views[n]
        outs = upper[n] if key not in partial_sums_0 else _adamw_big(
            f"adamw_{n}_0", view(w[n]), view(m[n]), view(v[n]), partial_sums_0[key], row0, first=0, outs=upper[n])
        res[n] = [back(o) for o in outs]
    outs = _adamw_small("adamw_small", *[_pack([d[n] for n in small_names], SMALL_ROWS) for d in (w, m, v)],
                        small_sum)
    for i, n in enumerate(small_names):
        res[n] = [_unpack(o, small_shapes)[i] for o in outs]
    outs = _adamw_small("adamw_replicated", *[_pack([d[n] for n in rep_names], REP_ROWS) for d in (w, m, v)],
                        rep_sum)
    for i, n in enumerate(rep_names):
        res[n] = [_unpack(o, rep_shapes)[i] for o in outs]

    total = lax.psum(loss[0, 0], ("x", "y", "c"))
    return (total, dx[None], *[res[n][0] for n in order], *[res[n][1] for n in order],
            *[res[n][2] for n in order], *[res[n][3] for n in order])
```

```python
import jax
import jax.numpy as jnp
from jax import lax
from jax.experimental import pallas as pl
from jax.experimental.pallas import tpu as pltpu
from jax.experimental.pallas import tpu_sc as plsc

F32, BF16 = jnp.float32, jnp.bfloat16
MESH = pl.DeviceIdType.MESH
ANY = pl.BlockSpec(memory_space=pl.ANY)

T, D, F = 2048, 1024, 2816
DEPTH = 4
EPS = 1e-6
HEADS, HDIM, KV_HEADS, GROUP = 8, 64, 2, 4
WINDOW = BLOCK = 128
CHUNK = 64
NCHUNK = T // CHUNK
Q_A, KV_A, QKV_B, V_B = 512, 128, 1536, 512
IN_COLS = 2832
IN_SPLITS = (0, 512, 640, 768, 2304, 2816, 2832)
NCHIP, NDEV = 4, 8
FS = F // NCHIP
LR, B1, B2, AEPS, WD, STEP = 0.001, 0.9, 0.999, 1e-08, 0.01, 10
V7X_VMEM_BYTES = 64 * 1024 * 1024
VMEM_LIMIT = V7X_VMEM_BYTES * 7 // 8
LANES = 128


def _cp(*sem):
    return pltpu.CompilerParams(dimension_semantics=sem, vmem_limit_bytes=VMEM_LIMIT)


def _sds(shape, dtype=F32):
    return jax.ShapeDtypeStruct(tuple(shape), dtype)


def _full(shape):
    nd = len(shape)
    return pl.BlockSpec(tuple(shape), lambda *_: (0,) * nd)


def _split_bf16(a):
    hi = a.astype(BF16)
    return hi, (a - hi.astype(F32)).astype(BF16)


def _dg(a, b, ca, cb, hi=False):
    if a.ndim == 3 and b.ndim == 3:
        dims = (((ca + 1,), (cb + 1,)), ((0,), (0,)))
    else:
        dims = (((ca,), (cb,)), ((), ()))
    dot = lambda p, q: lax.dot_general(p, q, dims, preferred_element_type=F32)
    if hi:
        a_hi, a_lo = _split_bf16(a.astype(F32))
        b_hi, b_lo = _split_bf16(b.astype(F32))
        return dot(a_hi, b_hi) + (dot(a_hi, b_lo) + dot(a_lo, b_hi))
    return dot(a.astype(BF16), b.astype(BF16))


def _make_mm(hi):
    @jax.custom_vjp
    def nn(a, b):
        return _dg(a, b, 1, 0, hi)

    @jax.custom_vjp
    def nt(a, b):
        return _dg(a, b, 1, 1, hi)

    @jax.custom_vjp
    def tn(a, b):
        return _dg(a, b, 0, 0, hi)

    nn.defvjp(lambda a, b: (_dg(a, b, 1, 0, hi), (a, b)),
              lambda r, g: (_dg(g, r[1], 1, 1, hi).astype(r[0].dtype), _dg(r[0], g, 0, 0, hi).astype(r[1].dtype)))
    nt.defvjp(lambda a, b: (_dg(a, b, 1, 1, hi), (a, b)),
              lambda r, g: (_dg(g, r[1], 1, 0, hi).astype(r[0].dtype), _dg(g, r[0], 0, 0, hi).astype(r[1].dtype)))
    tn.defvjp(lambda a, b: (_dg(a, b, 0, 0, hi), (a, b)),
              lambda r, g: (_dg(r[1], g, 1, 1, hi).astype(r[0].dtype), _dg(r[0], g, 1, 0, hi).astype(r[1].dtype)))
    return nn, nt, tn


_nn, _nt, _tn = _make_mm(False)
_nn_hi = _make_mm(True)[0]


def _rms(x, w):
    return x * lax.rsqrt(jnp.mean(x * x, axis=-1, keepdims=True) + EPS) * w


def _layernorm(x, w, b):
    xc = x - jnp.mean(x, axis=-1, keepdims=True)
    return xc * lax.rsqrt(jnp.mean(xc * xc, axis=-1, keepdims=True) + EPS) * w + b


def _silu(x):
    return x * jax.nn.sigmoid(x)


def _iota2(shape, dim):
    return lax.broadcasted_iota(jnp.int32, shape, dim)


def _flat_weights(lhs_idx, weights):
    specs, ops, lhs_of, where = [], [], [], []
    for a, (k, w) in enumerate(zip(lhs_idx, weights)):
        for q in range(1 if w.ndim == 2 else w.shape[0]):
            specs.append(_full(w.shape) if w.ndim == 2
                         else pl.BlockSpec((None,) + w.shape[1:], lambda i, q=q: (q, 0, 0)))
            ops.append(w)
            lhs_of.append(k)
            where.append((a, None if w.ndim == 2 else q))
    return specs, ops, lhs_of, where


def _blk_fwd(name, pre, lhs_idx, post, toks, smalls, weights, outs, tm=512):
    wspecs, wops, lhs_of, _ = _flat_weights(lhs_idx, weights)
    nt_, ns, nw = len(toks), len(smalls), len(wops)

    def body(*refs):
        tv = [r[...] for r in refs[:nt_]]
        sv = [r[...] for r in refs[nt_:nt_ + ns]]
        wr = refs[nt_ + ns:nt_ + ns + nw]
        orf = refs[nt_ + ns + nw:]
        lhs = pre(tv, sv)
        ys = [_dg(lhs[i], w[...], 1, 0) for i, w in zip(lhs_of, wr)]
        for o_ref, o in zip(orf, post(ys, tv, sv)):
            o_ref[...] = o.astype(o_ref.dtype)

    in_specs = ([pl.BlockSpec((tm, a.shape[1]), lambda i: (i, 0)) for a in toks]
                + [_full(a.shape) for a in smalls] + wspecs)
    out_specs = [pl.BlockSpec((tm, w_), lambda i: (i, 0)) for w_, _ in outs]
    return pl.pallas_call(
        body, grid=(T // tm,), in_specs=in_specs, out_specs=out_specs,
        out_shape=[_sds((T, w_), dt) for w_, dt in outs], name=name, compiler_params=_cp("parallel"),
    )(*toks, *smalls, *wops)


def _blk_bwd(name, pre, lhs_idx, post, toks, smalls, weights, ct_groups, res=None, linear_post=False, tm=256,
             wchunk=512):
    wspecs, wops, lhs_of, where = _flat_weights(lhs_idx, weights)
    nt_, ns, nw, na = len(toks), len(smalls), len(wops), len(weights)
    cts = [a for g in ct_groups for a in g]
    nc = len(cts)
    widths = [sum(a.shape[1] for a in g) for g in ct_groups]
    has_res = res is not None

    def body(*refs):
        p = 0
        tr = refs[p:p + nt_]; p += nt_
        sr = refs[p:p + ns]; p += ns
        wr = refs[p:p + nw]; p += nw
        cr = refs[p:p + nc]; p += nc
        rr = refs[p:p + has_res]; p += has_res
        dtr = refs[p:p + nt_]; p += nt_
        dsr = refs[p:p + ns]; p += ns
        dwr = refs[p:p + na]; p += na
        scr = refs[p:]
        i = pl.program_id(0)

        @pl.when(i == 0)
        def _():
            for r in list(dsr) + list(dwr):
                r[...] = jnp.zeros_like(r)

        tv = [r[...] for r in tr]
        sv = [r[...] for r in sr]
        ctv, q, si = [], 0, 0
        for g in ct_groups:
            if len(g) == 1:
                ctv.append(cr[q][...].astype(F32))
            else:
                off = 0
                for j, a in enumerate(g):
                    scr[si][:, off:off + a.shape[1]] = cr[q + j][...].astype(F32)
                    off += a.shape[1]
                ctv.append(scr[si][...])
                si += 1
            q += len(g)

        lhs, vjp_pre = jax.vjp(lambda *a: tuple(pre(list(a[:nt_]), list(a[nt_:]))), *tv, *sv)
        lhs_b = [l.astype(BF16) for l in lhs]
        ys = [jnp.zeros((tm, w.shape[1]), F32) if linear_post else _dg(lhs_b[k], w[...], 1, 0)
              for k, w in zip(lhs_of, wr)]
        _, vjp_post = jax.vjp(lambda *a: tuple(post(list(a[:nw]), list(a[nw:nw + nt_]), list(a[nw + nt_:]))),
                              *ys, *tv, *sv)
        gp = vjp_post(tuple(ctv))
        dys, dt_post, ds_post = gp[:nw], gp[nw:nw + nt_], gp[nw + nt_:]
        dlhs = [None] * len(lhs)
        for k, w, dy, (a, q) in zip(lhs_of, wr, dys, where):
            dyb = dy.astype(BF16)
            n = w.shape[1]
            for c0 in range(0, n, wchunk):
                c1 = min(n, c0 + wchunk)
                part = _dg(lhs_b[k], dyb[:, c0:c1], 0, 0)
                if q is None:
                    dwr[a][:, c0:c1] += part
                else:
                    dwr[a][q, :, c0:c1] += part
            d = _dg(dyb, w[...], 1, 1)
            dlhs[k] = d if dlhs[k] is None else dlhs[k] + d
        gq = vjp_pre(tuple(d.astype(l.dtype) for d, l in zip(dlhs, lhs)))
        dt_pre, ds_pre = gq[:nt_], gq[nt_:]
        for j in range(nt_):
            d = dt_post[j] + dt_pre[j]
            if j == 0 and has_res:
                d = d + rr[0][...]
            dtr[j][...] = d
        for j in range(ns):
            dsr[j][...] += ds_post[j] + ds_pre[j]

    tok_spec = lambda a: pl.BlockSpec((tm, a.shape[1]), lambda i: (i, 0))
    in_specs = ([tok_spec(a) for a in toks] + [_full(a.shape) for a in smalls] + wspecs
                + [tok_spec(a) for a in cts] + ([tok_spec(res)] if has_res else []))
    out_specs = [tok_spec(a) for a in toks] + [_full(a.shape) for a in smalls] + [_full(w.shape) for w in weights]
    out_shape = ([_sds(a.shape) for a in toks] + [_sds(a.shape) for a in smalls] + [_sds(w.shape) for w in weights])
    scratch = [pltpu.VMEM((tm, wd), F32) for g, wd in zip(ct_groups, widths) if len(g) > 1]
    outs = pl.pallas_call(
        body, grid=(T // tm,), in_specs=in_specs, out_specs=out_specs, out_shape=out_shape,
        scratch_shapes=scratch, name=name, compiler_params=_cp("arbitrary"),
    )(*toks, *smalls, *wops, *cts, *([res] if has_res else []))
    return outs[:nt_], outs[nt_:nt_ + ns], outs[nt_ + ns:]


def _ffn_fwd(name, x, nw, ffn, idx, tm=1024):
    def body(x_ref, nw_ref, wg_ref, wu_ref, wd_ref, o_ref, g_ref, da_ref, db_ref, h_ref):
        s = pl.program_id(1)

        @pl.when(s == 0)
        def _():
            xv = x_ref[...]
            h_ref[...] = _rms(xv, nw_ref[...]).astype(BF16)
            o_ref[...] = xv

        h = h_ref[...]
        a = _dg(h, wg_ref[...], 1, 1)
        b = _dg(h, wu_ref[...], 1, 1)
        sa = jax.nn.sigmoid(a)
        act = a * sa
        gated = (act * b).astype(BF16)
        g_ref[...] = gated
        da_ref[...] = (b * (sa * (1.0 + a * (1.0 - sa)))).astype(BF16)
        db_ref[...] = act.astype(BF16)
        o_ref[...] += 0.5 * _dg(gated, wd_ref[...], 1, 0)

    wspec = lambda k: pl.BlockSpec((None, None, FS, D), lambda i, s: (s, idx, k, 0))
    act = pl.BlockSpec((None, tm, FS), lambda i, s: (s, i, 0))
    return pl.pallas_call(
        body, grid=(T // tm, NCHIP),
        in_specs=[pl.BlockSpec((tm, D), lambda i, s: (i, 0)), _full((1, D)), wspec(0), wspec(1), wspec(2)],
        out_specs=[pl.BlockSpec((tm, D), lambda i, s: (i, 0)), act, act, act,
                   pl.BlockSpec((tm, D), lambda i, s: (i, 0))],
        out_shape=[_sds((T, D))] + [_sds((NCHIP, T, FS), BF16)] * 3 + [_sds((T, D), BF16)],
        name=name, compiler_params=_cp("parallel", "arbitrary"),
    )(x, nw, ffn, ffn, ffn)


def _ffn_bwd(name, x, nw, ffn, idx, pre, dy, gbuf=None, tm=512):
    ni = T // tm

    def body(x_ref, dy_ref, nw_ref, wg_ref, wu_ref, wd_ref, g_ref, fa_ref, fb_ref, h_ref, dx_ref, dnw_ref, dffn_ref,
             dh_acc, ag, au, ad):
        s, i = pl.program_id(0), pl.program_id(1)
        rows = pl.ds(pl.multiple_of(i * tm, tm), tm)

        @pl.when((s == 0) & (i == 0))
        def _():
            dnw_ref[...] = jnp.zeros_like(dnw_ref)

        @pl.when(i == 0)
        def _():
            ag[...] = jnp.zeros_like(ag)
            au[...] = jnp.zeros_like(au)
            ad[...] = jnp.zeros_like(ad)

        hb = h_ref[...]
        dyb = (0.5 * dy_ref[...]).astype(BF16)
        ad[...] += _dg(g_ref[...], dyb, 0, 0)
        dact = _dg(dyb, wd_ref[...], 1, 1)
        da = (dact * fa_ref[...].astype(F32)).astype(BF16)
        db = (dact * fb_ref[...].astype(F32)).astype(BF16)
        ag[...] += _dg(da, hb, 0, 0)
        au[...] += _dg(db, hb, 0, 0)
        dh = _dg(da, wg_ref[...], 1, 0) + _dg(db, wu_ref[...], 1, 0)

        @pl.when(s == 0)
        def _():
            dh_acc[rows, :] = dh

        @pl.when((s > 0) & (s < NCHIP - 1))
        def _():
            dh_acc[rows, :] += dh

        @pl.when(s == NCHIP - 1)
        def _():
            _, vjp_rms = jax.vjp(_rms, x_ref[...], nw_ref[...])
            dx, dnw = vjp_rms(dh_acc[rows, :] + dh)
            dx_ref[...] = dy_ref[...] + dx
            dnw_ref[...] += dnw

        @pl.when(i == ni - 1)
        def _():
            dffn_ref[0:FS, :] = ag[...].astype(BF16)
            dffn_ref[FS:2 * FS, :] = au[...].astype(BF16)
            dffn_ref[2 * FS:, :] = ad[...].astype(BF16)

    wspec = lambda r, k, blk=0: pl.BlockSpec((None, None, r, D), lambda s, i: (s, blk, k, 0),
                                             pipeline_mode=pl.Buffered(1))
    last = lambda s, i: (jnp.where(s == NCHIP - 1, i, 0), 0)
    nb = 0 if gbuf is None else 1
    act = pl.BlockSpec((None, tm, FS), lambda s, i: (s, i, 0))
    tok = pl.BlockSpec((tm, D), lambda s, i: (i, 0))
    return pl.pallas_call(
        lambda *refs: body(*refs[:10], *refs[10 + nb:]), grid=(NCHIP, ni),
        in_specs=[pl.BlockSpec((tm, D), last), tok, _full((1, D)), wspec(FS, 0), wspec(FS, 1), wspec(FS, 2), act, act,
                  act, tok] + [ANY] * nb,
        out_specs=[pl.BlockSpec((tm, D), last), _full((1, D)), wspec(3 * FS, 0, idx)],
        out_shape=[_sds((T, D)), _sds((1, D)), _sds((NCHIP, 2, 3 * FS, D), BF16)],
        input_output_aliases={10 + k: 2 + k for k in range(nb)},
        scratch_shapes=[pltpu.VMEM((T, D), F32)] + [pltpu.VMEM((FS, D), F32)] * 3,
        name=name, compiler_params=_cp("arbitrary", "arbitrary"),
    )(x, dy, nw, ffn, ffn, ffn, *pre, *(() if gbuf is None else (gbuf,)))


CONV_ROWS = 256


def _conv_pad(k):
    return 8 * ((k - 1 + 7) // 8)


def _shifted(win, o):
    n = win.shape[0]
    return (win if o % n == 0 else pltpu.roll(win, (n - o) % n, 0))[0:CONV_ROWS, :]


def _conv_fwd(name, x, w, b, act):
    k_w, c = w.shape
    tc = 256 if c % 256 == 0 else LANES
    pad = _conv_pad(k_w)
    has_b = b is not None

    def body(*refs):
        x_ref, w_ref = refs[0], refs[1]
        b_ref = refs[2] if has_b else None
        y_ref, xp = refs[2 + has_b], refs[3 + has_b]
        xp[0:pad, :] = jnp.zeros((pad, tc), F32)
        xp[pad:, :] = x_ref[...]

        def step(t, carry):
            base = pl.multiple_of(t * CONV_ROWS, CONV_ROWS)
            win = xp[pl.ds(base, CONV_ROWS + pad), :]
            acc = jnp.zeros((CONV_ROWS, tc), F32)
            for k in range(k_w):
                o = pad - (k_w - 1) + k
                acc = acc + w_ref[k:k + 1, :] * _shifted(win, o)
            if has_b:
                acc = acc + b_ref[...]
            y_ref[pl.ds(base, CONV_ROWS), :] = _silu(acc) if act else acc
            return carry

        lax.fori_loop(0, T // CONV_ROWS, step, 0)

    col = lambda r: pl.BlockSpec((r, tc), lambda j: (0, j))
    ins = [x, w] + ([b] if has_b else [])
    return pl.pallas_call(
        body, grid=(c // tc,), in_specs=[col(T), col(k_w)] + ([col(1)] if has_b else []), out_specs=col(T),
        out_shape=_sds((T, c)), scratch_shapes=[pltpu.VMEM((T + pad, tc), F32)], name=name,
        compiler_params=_cp("parallel"),
    )(*ins)


def _conv_bwd(name, x, w, b, act, dy):
    k_w, c = w.shape
    tc = 256 if c % 256 == 0 else LANES
    pad = _conv_pad(k_w)
    has_b = b is not None

    def body(*refs):
        x_ref, w_ref, dy_ref = refs[0], refs[1], refs[2]
        b_ref = refs[3] if has_b else None
        dx_ref, dw_ref, db_ref, xp, dp = refs[3 + has_b:]
        xp[0:pad, :] = jnp.zeros((pad, tc), F32)
        xp[pad:, :] = x_ref[...]
        dp[T:, :] = jnp.zeros((pad, tc), F32)
        dw_ref[...] = jnp.zeros_like(dw_ref)
        db_ref[...] = jnp.zeros_like(db_ref)

        def step1(t, carry):
            base = pl.multiple_of(t * CONV_ROWS, CONV_ROWS)
            d = dy_ref[pl.ds(base, CONV_ROWS), :]
            win = xp[pl.ds(base, CONV_ROWS + pad), :]
            offs = [pad - (k_w - 1) + k for k in range(k_w)]
            if act:
                acc = jnp.zeros((CONV_ROWS, tc), F32)
                for k, o in enumerate(offs):
                    acc = acc + w_ref[k:k + 1, :] * _shifted(win, o)
                if has_b:
                    acc = acc + b_ref[...]
                sg = jax.nn.sigmoid(acc)
                d = d * (sg * (1.0 + acc * (1.0 - sg)))
            dp[pl.ds(base, CONV_ROWS), :] = d
            for k, o in enumerate(offs):
                dw_ref[k:k + 1, :] += jnp.sum(d * _shifted(win, o), axis=0, keepdims=True)
            db_ref[...] += jnp.sum(d, axis=0, keepdims=True)
            return carry

        lax.fori_loop(0, T // CONV_ROWS, step1, 0)

        def step2(t, carry):
            base = pl.multiple_of(t * CONV_ROWS, CONV_ROWS)
            win = dp[pl.ds(base, CONV_ROWS + pad), :]
            acc = jnp.zeros((CONV_ROWS, tc), F32)
            for k in range(k_w):
                o = (k_w - 1) - k
                acc = acc + w_ref[k:k + 1, :] * _shifted(win, o)
            dx_ref[pl.ds(base, CONV_ROWS), :] = acc
            return carry

        lax.fori_loop(0, T // CONV_ROWS, step2, 0)

    col = lambda r: pl.BlockSpec((r, tc), lambda j: (0, j))
    ins = [x, w, dy] + ([b] if has_b else [])
    return pl.pallas_call(
        body, grid=(c // tc,), in_specs=[col(T), col(k_w), col(T)] + ([col(1)] if has_b else []),
        out_specs=[col(T), col(k_w), col(1)], out_shape=[_sds((T, c)), _sds((k_w, c)), _sds((1, c))],
        scratch_shapes=[pltpu.VMEM((T + pad, tc), F32), pltpu.VMEM((T + pad, tc), F32)], name=name,
        compiler_params=_cp("parallel"),
    )(*ins)


def _attn_consts(n):
    i = _iota2((BLOCK, 2 * BLOCK), 0)
    j = _iota2((BLOCK, 2 * BLOCK), 1)
    dist = i + BLOCK - j
    valid = (dist >= 0) & (dist < WINDOW) & ((n > 0) | (j >= BLOCK))
    return dist.astype(F32), valid


def _attn_block(q4, kk, vv, sinks, dist, valid, kv):
    outs = []
    lane = _iota2((1, HEADS), 1)
    for g in range(GROUP):
        h = kv * GROUP + g
        slope = 2.0 ** (-8.0 * (h + 1) / HEADS)
        s = _nt(q4[:, g * HDIM:(g + 1) * HDIM], kk) * (HDIM ** -0.5)
        s = jnp.where(valid, s - slope * dist, -1e30)
        sink = jnp.sum(jnp.where(lane == h, sinks, 0.0), axis=1, keepdims=True)
        m = jnp.maximum(jnp.max(s, axis=-1, keepdims=True), sink)
        e = jnp.exp(s - m)
        p = e / (jnp.sum(e, axis=-1, keepdims=True) + jnp.exp(sink - m))
        outs.append(_nn(p, vv))
    return tuple(outs)


def _attn_fwd(name, qa, ka, va, sinks):
    def body(q_ref, k_ref, v_ref, s_ref, o_ref, kp, vp):
        kp[0:BLOCK, :] = jnp.zeros((BLOCK, KV_A), F32)
        vp[0:BLOCK, :] = jnp.zeros((BLOCK, KV_A), F32)
        kp[BLOCK:, :] = k_ref[...]
        vp[BLOCK:, :] = v_ref[...]
        sinks_v = s_ref[...]

        def step(n, carry):
            r = pl.multiple_of(n * BLOCK, BLOCK)
            dist, valid = _attn_consts(n)
            k2 = kp[pl.ds(r, 2 * BLOCK), :]
            v2 = vp[pl.ds(r, 2 * BLOCK), :]
            for kv in range(KV_HEADS):
                q4 = q_ref[pl.ds(r, BLOCK), kv * GROUP * HDIM:(kv + 1) * GROUP * HDIM]
                og = _attn_block(q4, k2[:, kv * HDIM:(kv + 1) * HDIM], v2[:, kv * HDIM:(kv + 1) * HDIM], sinks_v,
                                 dist, valid, kv)
                for g in range(GROUP):
                    h = kv * GROUP + g
                    o_ref[pl.ds(r, BLOCK), h * HDIM:(h + 1) * HDIM] = og[g]
            return carry

        lax.fori_loop(0, T // BLOCK, step, 0)

    return pl.pallas_call(
        body, out_shape=_sds((T, Q_A)),
        scratch_shapes=[pltpu.VMEM((T + BLOCK, KV_A), F32), pltpu.VMEM((T + BLOCK, KV_A), F32)], name=name,
        compiler_params=pltpu.CompilerParams(vmem_limit_bytes=VMEM_LIMIT),
    )(qa, ka, va, sinks)


def _attn_bwd(name, qa, ka, va, sinks, do):
    def body(q_ref, k_ref, v_ref, s_ref, do_ref, dq_ref, dk_ref, dv_ref, ds_ref, kp, vp, dkp, dvp):
        kp[0:BLOCK, :] = jnp.zeros((BLOCK, KV_A), F32)
        vp[0:BLOCK, :] = jnp.zeros((BLOCK, KV_A), F32)
        kp[BLOCK:, :] = k_ref[...]
        vp[BLOCK:, :] = v_ref[...]
        dkp[...] = jnp.zeros_like(dkp)
        dvp[...] = jnp.zeros_like(dvp)
        ds_ref[...] = jnp.zeros_like(ds_ref)
        sinks_v = s_ref[...]

        def step(n, carry):
            r = pl.multiple_of(n * BLOCK, BLOCK)
            dist, valid = _attn_consts(n)
            k2 = kp[pl.ds(r, 2 * BLOCK), :]
            v2 = vp[pl.ds(r, 2 * BLOCK), :]
            for kv in range(KV_HEADS):
                cols = slice(kv * HDIM, (kv + 1) * HDIM)
                q4 = q_ref[pl.ds(r, BLOCK), kv * GROUP * HDIM:(kv + 1) * GROUP * HDIM]
                _, vjp = jax.vjp(lambda q, k, v, s: _attn_block(q, k, v, s, dist, valid, kv),
                                 q4, k2[:, cols], v2[:, cols], sinks_v)
                cts = tuple(do_ref[pl.ds(r, BLOCK), (kv * GROUP + g) * HDIM:(kv * GROUP + g + 1) * HDIM]
                            for g in range(GROUP))
                dq4, dkk, dvv, dsk = vjp(cts)
                dq_ref[pl.ds(r, BLOCK), kv * GROUP * HDIM:(kv + 1) * GROUP * HDIM] = dq4
                dkp[pl.ds(r, 2 * BLOCK), cols] += dkk
                dvp[pl.ds(r, 2 * BLOCK), cols] += dvv
                ds_ref[...] += dsk
            return carry

        lax.fori_loop(0, T // BLOCK, step, 0)
        dk_ref[...] = dkp[BLOCK:, :]
        dv_ref[...] = dvp[BLOCK:, :]

    pad = lambda: pltpu.VMEM((T + BLOCK, KV_A), F32)
    return pl.pallas_call(
        body, out_shape=[_sds((T, Q_A)), _sds((T, KV_A)), _sds((T, KV_A)), _sds((1, HEADS))],
        scratch_shapes=[pad(), pad(), pad(), pad()], name=name,
        compiler_params=pltpu.CompilerParams(vmem_limit_bytes=VMEM_LIMIT),
    )(qa, ka, va, sinks, do)


def _dn_consts():
    i = _iota2((CHUNK, CHUNK), 0)
    j = _iota2((CHUNK, CHUNK), 1)
    return dict(causal=i >= j, strict=i > j, ltri=(i >= j).astype(F32),
                last=(_iota2((CHUNK, 1), 0) == CHUNK - 1).astype(F32))


def _l2norm(x):
    return x * lax.rsqrt(jnp.sum(x * x, axis=-1, keepdims=True) + EPS)


def _head_cols(m):
    lane = _iota2((1, HEADS), 1)
    return jnp.concatenate([jnp.sum(jnp.where(lane == h, m, 0.0), axis=1, keepdims=True)[None]
                            for h in range(HEADS)], axis=0)


@jax.custom_vjp
def _unit_lower_inverse(low, known):
    if known is not None:
        return known
    inv = (_iota2((CHUNK, CHUNK), 0) == _iota2((CHUNK, CHUNK), 1)).astype(F32) - low
    pw = low
    for _ in range(5):
        pw = _dg(pw, pw, 1, 0, True)
        inv = inv + _dg(inv, pw, 1, 0, True)
    return inv


def _unit_lower_inverse_fwd(low, known):
    inv = _unit_lower_inverse(low, known)
    return inv, (inv, known)


def _unit_lower_inverse_bwd(res, g):
    inv, known = res
    d_low = -_dg(inv, _dg(g, inv, 1, 1, True), 0, 0, True)
    return d_low, (None if known is None else jnp.zeros_like(known))


_unit_lower_inverse.defvjp(_unit_lower_inverse_fwd, _unit_lower_inverse_bwd)


def _dn_local(q3, k3, v3, braw, araw, alog, dtb, cs, known_inv=None):
    q = _l2norm(q3) * (HDIM ** -0.5)
    k = _l2norm(k3)
    g = -jnp.exp(alog) * jax.nn.softplus(araw + dtb)
    gc_all = _nn_hi(cs["ltri"], g)
    egc_all = jnp.exp(gc_all)
    beta, gc, egc = _head_cols(jax.nn.sigmoid(braw)), _head_cols(gc_all), _head_cols(egc_all)
    a = jnp.broadcast_to(gc, (HEADS, CHUNK, CHUNK))
    diff = a - jnp.swapaxes(a, 1, 2)
    decay = jnp.where(cs["causal"], jnp.exp(jnp.where(cs["causal"], diff, 0.0)), 0.0)
    kb = k * beta
    low = jnp.where(cs["strict"], _nt(kb, k) * decay, 0.0)
    inv = _unit_lower_inverse(low, known_inv)
    u = _nn_hi(inv, v3 * beta)
    w = _nn_hi(inv, kb * egc)
    attn = _nt(q, k) * decay
    gc_last = jnp.sum(gc * cs["last"], axis=1, keepdims=True)
    return u, w, attn, q * egc, k * jnp.exp(gc_last - gc), egc_all, inv


def _heads3(ref, off=0):
    return jnp.concatenate([ref[:, off + h * HDIM:off + (h + 1) * HDIM][None] for h in range(HEADS)], axis=0)


def _dn_local_fwd(name, qkv, ba, alog, dtb):
    def body(qkv_ref, ba_ref, al_ref, dt_ref, u_ref, w_ref, at_ref, qd_ref, kd_ref, eg_ref, inv_ref):
        bav = ba_ref[...]
        outs = _dn_local(_heads3(qkv_ref), _heads3(qkv_ref, 512), _heads3(qkv_ref, 1024), bav[:, :HEADS],
                         bav[:, HEADS:], al_ref[...], dt_ref[...], _dn_consts())
        for r, o in zip((u_ref, w_ref, at_ref, qd_ref, kd_ref, inv_ref), outs[:5] + outs[6:]):
            _unheads(r, o)
        eg_ref[...] = outs[5]

    row = lambda w_: pl.BlockSpec((CHUNK, w_), lambda n: (n, 0))
    return pl.pallas_call(
        body, grid=(NCHUNK,), in_specs=[row(QKV_B), row(2 * HEADS), _full((1, HEADS)), _full((1, HEADS))],
        out_specs=[row(V_B)] * 5 + [row(HEADS), row(V_B)],
        out_shape=[_sds((T, V_B))] * 5 + [_sds((T, HEADS)), _sds((T, V_B))], name=name,
        compiler_params=_cp("parallel"),
    )(qkv, ba, alog, dtb)


def _dn_local_bwd(name, qkv, ba, alog, dtb, inv, cts):
    def body(qkv_ref, ba_ref, al_ref, dt_ref, inv_ref, du_ref, dw_ref, dat_ref, dqd_ref, dkd_ref, deg_ref,
             dqkv_ref, dba_ref, dal_ref, ddt_ref):
        @pl.when(pl.program_id(0) == 0)
        def _():
            dal_ref[...] = jnp.zeros_like(dal_ref)
            ddt_ref[...] = jnp.zeros_like(ddt_ref)

        cs = _dn_consts()
        bav = ba_ref[...]
        known = _heads3(inv_ref)
        _, vjp = jax.vjp(lambda *a: _dn_local(*a, cs, known)[:6], _heads3(qkv_ref), _heads3(qkv_ref, 512),
                         _heads3(qkv_ref, 1024), bav[:, :HEADS], bav[:, HEADS:], al_ref[...], dt_ref[...])
        dq, dk, dv, dbr, dar, dal, ddt = vjp((_heads3(du_ref), _heads3(dw_ref), _heads3(dat_ref), _heads3(dqd_ref),
                                              _heads3(dkd_ref), deg_ref[...]))
        for h in range(HEADS):
            dqkv_ref[:, h * HDIM:(h + 1) * HDIM] = dq[h]
            dqkv_ref[:, 512 + h * HDIM:512 + (h + 1) * HDIM] = dk[h]
            dqkv_ref[:, 1024 + h * HDIM:1024 + (h + 1) * HDIM] = dv[h]
        dba_ref[:, :HEADS] = dbr
        dba_ref[:, HEADS:] = dar
        dal_ref[...] += dal
        ddt_ref[...] += ddt

    row = lambda w_: pl.BlockSpec((CHUNK, w_), lambda n: (n, 0))
    return pl.pallas_call(
        body, grid=(NCHUNK,),
        in_specs=[row(QKV_B), row(2 * HEADS), _full((1, HEADS)), _full((1, HEADS))] + [row(V_B)] * 6 + [row(HEADS)],
        out_specs=[row(QKV_B), row(2 * HEADS), _full((1, HEADS)), _full((1, HEADS))],
        out_shape=[_sds((T, QKV_B)), _sds((T, 2 * HEADS)), _sds((1, HEADS)), _sds((1, HEADS))], name=name,
        compiler_params=_cp("arbitrary"),
    )(qkv, ba, alog, dtb, inv, *cts)


def _dn_step(s, u, w, attn, qd, kd, egc, z, nw):
    last = (_iota2((CHUNK, 1), 0) == CHUNK - 1).astype(F32)
    gl = jnp.sum(_head_cols(egc) * last, axis=1, keepdims=True)
    v_new = u - _nn(w, s)
    o = _nn(qd, s) + _nn(attn, v_new)
    s_new = s * gl + _tn(kd, v_new)
    return s_new, _rms(o, nw) * _silu(z)


def _unheads(ref, v3):
    for h in range(HEADS):
        ref[:, h * HDIM:(h + 1) * HDIM] = v3[h]


def _dn_rec_fwd(name, u, w, attn, qd, kd, egc, z, nw):
    def body(u_ref, w_ref, at_ref, qd_ref, kd_ref, eg_ref, z_ref, nw_ref, o_ref, ss_ref, s_scr):
        @pl.when(pl.program_id(0) == 0)
        def _():
            s_scr[...] = jnp.zeros_like(s_scr)

        s = s_scr[...]
        ss_ref[...] = s
        s_new, on = _dn_step(s, _heads3(u_ref), _heads3(w_ref), _heads3(at_ref), _heads3(qd_ref), _heads3(kd_ref),
                             eg_ref[...], _heads3(z_ref), nw_ref[...])
        s_scr[...] = s_new
        _unheads(o_ref, on)

    row = lambda w_: pl.BlockSpec((CHUNK, w_), lambda n: (n, 0))
    return pl.pallas_call(
        body, grid=(NCHUNK,), in_specs=[row(V_B)] * 5 + [row(HEADS), row(V_B), _full((1, HDIM))],
        out_specs=[row(V_B), pl.BlockSpec((None, HEADS, HDIM, HDIM), lambda n: (n, 0, 0, 0))],
        out_shape=[_sds((T, V_B)), _sds((NCHUNK, HEADS, HDIM, HDIM))],
        scratch_shapes=[pltpu.VMEM((HEADS, HDIM, HDIM), F32)], name=name, compiler_params=_cp("arbitrary"),
    )(u, w, attn, qd, kd, egc, z, nw)


def _dn_rec_bwd(name, u, w, attn, qd, kd, egc, z, nw, ss, do):
    def body(u_ref, w_ref, at_ref, qd_ref, kd_ref, eg_ref, z_ref, nw_ref, ss_ref, do_ref,
             du_ref, dw_ref, dat_ref, dqd_ref, dkd_ref, deg_ref, dz_ref, dnw_ref, ds_scr):
        @pl.when(pl.program_id(0) == 0)
        def _():
            ds_scr[...] = jnp.zeros_like(ds_scr)
            dnw_ref[...] = jnp.zeros_like(dnw_ref)

        _, vjp = jax.vjp(_dn_step, ss_ref[...], _heads3(u_ref), _heads3(w_ref), _heads3(at_ref), _heads3(qd_ref),
                         _heads3(kd_ref), eg_ref[...], _heads3(z_ref), nw_ref[...])
        ds, du, dw, dat, dqd, dkd, deg, dz, dnw = vjp((ds_scr[...], _heads3(do_ref)))
        ds_scr[...] = ds
        for r, v in zip((du_ref, dw_ref, dat_ref, dqd_ref, dkd_ref, dz_ref), (du, dw, dat, dqd, dkd, dz)):
            _unheads(r, v)
        deg_ref[...] = deg
        dnw_ref[...] += dnw

    row = lambda w_: pl.BlockSpec((CHUNK, w_), lambda n: (NCHUNK - 1 - n, 0))
    return pl.pallas_call(
        body, grid=(NCHUNK,),
        in_specs=[row(V_B)] * 5 + [row(HEADS), row(V_B), _full((1, HDIM)),
                                   pl.BlockSpec((None, HEADS, HDIM, HDIM), lambda n: (NCHUNK - 1 - n, 0, 0, 0)),
                                   row(V_B)],
        out_specs=[row(V_B)] * 5 + [row(HEADS), row(V_B), _full((1, HDIM))],
        out_shape=[_sds((T, V_B))] * 5 + [_sds((T, HEADS)), _sds((T, V_B)), _sds((1, HDIM))],
        scratch_shapes=[pltpu.VMEM((HEADS, HDIM, HDIM), F32)], name=name, compiler_params=_cp("arbitrary"),
    )(u, w, attn, qd, kd, egc, z, nw, ss, do)


def _final(name, x, fw, target, tm=512):
    def body(x_ref, fw_ref, t_ref, l_ref, dx_ref, dfw_ref):
        @pl.when(pl.program_id(0) == 0)
        def _():
            l_ref[...] = jnp.zeros_like(l_ref)
            dfw_ref[...] = jnp.zeros_like(dfw_ref)

        tv = t_ref[...]

        def f(xv, fwv):
            err = _rms(xv, fwv) - tv
            per_tok = jnp.mean(err * err, axis=-1, keepdims=True)
            return 0.5 * jnp.sum(per_tok, axis=0, keepdims=True)

        loss, vjp = jax.vjp(f, x_ref[...], fw_ref[...])
        dx, dfw = vjp(jnp.ones((1, 1), F32))
        l_ref[...] += loss
        dx_ref[...] = dx
        dfw_ref[...] += dfw

    tok = pl.BlockSpec((tm, D), lambda i: (i, 0))
    return pl.pallas_call(
        body, grid=(T // tm,), in_specs=[tok, _full((1, D)), tok], out_specs=[_full((1, 1)), tok, _full((1, D))],
        out_shape=[_sds((1, 1)), _sds((T, D)), _sds((1, D))], name=name, compiler_params=_cp("arbitrary"),
    )(x, fw, target)


def _m1_pre(tv, sv):
    return [_rms(tv[0], sv[0])]


def _m1_post(ys, tv, sv):
    return (jnp.concatenate(ys, axis=1),)


def _m1_post_split(ys, tv, sv):
    proj = jnp.concatenate(ys, axis=1)
    return tuple(proj[:, a:b] for a, b in zip(IN_SPLITS[:-1], IN_SPLITS[1:]))


def _m5_pre(tv, sv):
    return [tv[1], tv[2]]


def _m5_post(ys, tv, sv):
    return (tv[0] + ys[0] + ys[1],)


def _c1_pre(tv, sv):
    return [_rms(tv[0], sv[0])]


def _c1_post(ys, tv, sv):
    return ((jnp.concatenate(ys[:2], axis=1) + sv[1]) * jax.nn.sigmoid(jnp.concatenate(ys[2:], axis=1) + sv[2]),)


def _c3_pre(tv, sv):
    return [_silu(_layernorm(tv[0], sv[0], sv[1]))]


def _c3_post(ys, tv, sv):
    return (tv[1] + ys[0] + sv[2],)


def _row(v):
    return v.reshape(1, -1)


def _mixer_fwd(tag, x, p):
    parts = _blk_fwd(f"m1_fwd_{tag}", _m1_pre, [0], _m1_post_split, [x], [p["nw"]], [p["w_in"]],
                     [(b - a, F32) for a, b in zip(IN_SPLITS[:-1], IN_SPLITS[1:])])
    qa, ka, va, qkvb, z, ba = parts
    att = _attn_fwd(f"attn_fwd_{tag}", qa, ka, va, p["sinks"])
    qkvc = _conv_fwd(f"dnconv_fwd_{tag}", qkvb, p["dn_conv_w"], None, True)
    *loc, inv = _dn_local_fwd(f"dnloc_fwd_{tag}", qkvc, ba, p["a_log"], p["dt_bias"])
    og, ss = _dn_rec_fwd(f"dnrec_fwd_{tag}", *loc, z, p["dn_norm_w"])
    (out,) = _blk_fwd(f"m5_fwd_{tag}", _m5_pre, [0, 1], _m5_post, [x, att, og], [], [p["wo_a"], p["wo_b"]],
                      [(D, F32)])
    return out, dict(x=x, qa=qa, ka=ka, va=va, qkvb=qkvb, z=z, ba=ba, att=att, qkvc=qkvc, loc=loc, inv=inv, og=og,
                     ss=ss)


def _mixer_bwd(tag, dy, p, s):
    (dxa, datt, dog), _, (dwo_a, dwo_b) = _blk_bwd(f"m5_bwd_{tag}", _m5_pre, [0, 1], _m5_post,
                                                   [s["x"], s["att"], s["og"]], [], [p["wo_a"], p["wo_b"]], [[dy]],
                                                   linear_post=True)
    rec = _dn_rec_bwd(f"dnrec_bwd_{tag}", *s["loc"], s["z"], p["dn_norm_w"], s["ss"], dog)
    dz, dnw_dn = rec[6], rec[7]
    dqkvc, dba, dalog, ddtb = _dn_local_bwd(f"dnloc_bwd_{tag}", s["qkvc"], s["ba"], p["a_log"], p["dt_bias"],
                                            s["inv"], rec[:6])
    dqkvb, dconvw, _ = _conv_bwd(f"dnconv_bwd_{tag}", s["qkvb"], p["dn_conv_w"], None, True, dqkvc)
    dqa, dka, dva, dsinks = _attn_bwd(f"attn_bwd_{tag}", s["qa"], s["ka"], s["va"], p["sinks"], datt)
    (dx,), (dnw,), (dw_in,) = _blk_bwd(f"m1_bwd_{tag}", _m1_pre, [0], _m1_post, [s["x"]], [p["nw"]], [p["w_in"]],
                                       [[dqa, dka, dva, dqkvb, dz, dba]], res=dxa, linear_post=True)
    return dx, dict(nw=dnw, w_in=dw_in, wo_a=dwo_a, wo_b=dwo_b, dn_conv_w=dconvw, sinks=dsinks, a_log=dalog,
                    dt_bias=ddtb, dn_norm_w=dnw_dn)


def _conformer_fwd(tag, x, p):
    (glu,) = _blk_fwd(f"c1_fwd_{tag}", _c1_pre, [0], _c1_post, [x], [p["nw"], p["b1a"], p["b1b"]], [p["w1"]],
                      [(D, F32)])
    cc = _conv_fwd(f"dwconv_fwd_{tag}", glu, p["w_dw"], p["b_dw"], False)
    (out,) = _blk_fwd(f"c3_fwd_{tag}", _c3_pre, [0], _c3_post, [cc, x], [p["ln_w"], p["ln_b"], p["b2"]], [p["w2"]],
                      [(D, F32)])
    return out, dict(x=x, glu=glu, cc=cc)


def _conformer_bwd(tag, dy, p, s):
    (dcc, dxa), (dlnw, dlnb, db2), (dw2,) = _blk_bwd(f"c3_bwd_{tag}", _c3_pre, [0], _c3_post, [s["cc"], s["x"]],
                                                     [p["ln_w"], p["ln_b"], p["b2"]], [p["w2"]], [[dy]],
                                                     linear_post=True)
    dglu, dwdw, dbdw = _conv_bwd(f"dwconv_bwd_{tag}", s["glu"], p["w_dw"], p["b_dw"], False, dcc)
    (dx,), (dnw, db1a, db1b), (dw1,) = _blk_bwd(f"c1_bwd_{tag}", _c1_pre, [0], _c1_post, [s["x"]],
                                                [p["nw"], p["b1a"], p["b1b"]], [p["w1"]], [[dglu]], res=dxa)
    return dx, dict(nw=dnw, b1a=db1a, b1b=db1b, w1=dw1, w_dw=dwdw, b_dw=dbdw, ln_w=dlnw, ln_b=dlnb, b2=db2, w2=dw2)


def _layer_fwd(l, x, nw, ffn_a, get_ffn_b, p):
    x1, *pre_a = _ffn_fwd(f"ffn_fwd_{l}a", x, _row(nw[0]), ffn_a, 0)
    p = dict(p, nw=_row(nw[1]))
    x2, sv = (_mixer_fwd if l % 2 == 0 else _conformer_fwd)(str(l), x1, p)
    x2, ffn_b = get_ffn_b(x2)
    out, *pre_b = _ffn_fwd(f"ffn_fwd_{l}b", x2, _row(nw[2]), ffn_b, 0)
    return out, (x, x2, p, sv, pre_a, pre_b, ffn_a, ffn_b)


def _layer_bwd(l, dx, nw, saved, after_first=lambda dx: dx):
    x0, x2, p, sv, pre_a, pre_b, ffn_a, ffn_b = saved
    dx, dn2, dffn = _ffn_bwd(f"ffn_bwd_{l}b", x2, _row(nw[2]), ffn_b, 1, pre_b, dx)
    dx = after_first(dx)
    dx, dmix = (_mixer_bwd if l % 2 == 0 else _conformer_bwd)(str(l), dx, p, sv)
    dx, dn0, dffn = _ffn_bwd(f"ffn_bwd_{l}a", x0, _row(nw[0]), ffn_a, 0, pre_a, dx, dffn)
    return dx, jnp.concatenate([dn0, dmix.pop("nw"), dn2], axis=0), dffn, dmix


def _place(staggered=False):
    x, y, c = lax.axis_index("x"), lax.axis_index("y"), lax.axis_index("c")
    s = c if staggered else 0
    first, second = (x + (1 - s) * (1 - 2 * x), y + s * (1 - 2 * y)), (x + s * (1 - 2 * x), y + (1 - s) * (1 - 2 * y))
    chips = [first, second, (1 - x, 1 - y)]
    return x, y, c, 2 * x + y, chips, [2 * px + py for px, py in chips]


def _handshake(peers):
    barrier = pltpu.get_barrier_semaphore()
    for p in peers:
        pl.semaphore_signal(barrier, inc=1, device_id=p, device_id_type=MESH)
    pl.semaphore_wait(barrier, len(peers))


def _chip_peers():
    x, y, c, _, chips, _ = _place()
    return [(*chip, c) for chip in chips] + [(x, y, 1 - c)]


def _gather_copies(ins, outs, nb, send, recv, fsend, frecv, lsem):
    n_in = len(ins)
    x, y, c, me, chips, cidx = _place(staggered=True)
    sib = (x, y, 1 - c)
    local = [pltpu.make_async_copy(ins[a], outs[a].at[me], lsem.at[a]) for a in range(n_in)]

    def region(a, k, who):
        if k < 2:
            return outs[a].at[cidx[k], pl.ds(who, 1)]
        r = ins[a].shape[1] // 2
        return outs[a].at[cidx[2], pl.ds(who, 1), pl.ds((k - 2) * r, r)]

    def hop(a, k):
        if k < 2:
            src, dst = ins[a].at[pl.ds(c, 1)], outs[a].at[me, pl.ds(c, 1)]
        else:
            r = ins[a].shape[1] // 2
            src = dst = outs[a].at[cidx[3 - k], pl.ds(c, 1), pl.ds((k - 2) * r, r)]
        return pltpu.make_async_remote_copy(src, dst, send.at[4 * a + k], recv.at[4 * a + k],
                                            device_id=(*chips[k % 2], c), device_id_type=MESH)

    def landed(a, k):
        dst = region(a, k, c)
        return pltpu.make_async_remote_copy(dst, dst, send.at[4 * a + k], recv.at[4 * a + k],
                                            device_id=(*chips[k % 2], c), device_id_type=MESH)

    def passed(a, k, who):
        part = region(a, k, who)
        return pltpu.make_async_remote_copy(part, part, fsend.at[4 * a + k], frecv.at[4 * a + k], device_id=sib,
                                            device_id_type=MESH)

    def direct(a, j):
        k = 4 * nb + 3 * (a - nb) + j
        return pltpu.make_async_remote_copy(ins[a], outs[a].at[me], send.at[k], recv.at[k],
                                            device_id=(*chips[j], c), device_id_type=MESH)

    def direct_landed(a, j):
        k = 4 * nb + 3 * (a - nb) + j
        dst = outs[a].at[cidx[j]]
        return pltpu.make_async_remote_copy(dst, dst, send.at[k], recv.at[k], device_id=(*chips[j], c),
                                            device_id_type=MESH)

    sends = [hop(a, k) for a in range(nb) for k in range(2)] + [direct(a, j) for a in range(nb, n_in) for j in range(3)]
    for cp in sends:
        cp.start()
    for cp in local:
        cp.start()
    for a in range(nb):
        for k in (1, 0):
            landed(a, k).wait_recv()
            for cp in (hop(a, 3 - k), passed(a, k, c)):
                cp.start()
                sends.append(cp)
    for a in range(nb):
        for k in (2, 3):
            landed(a, k).wait_recv()
            cp = passed(a, k, c)
            cp.start()
            sends.append(cp)
    for a in range(nb, n_in):
        for j in range(3):
            direct_landed(a, j).wait_recv()
    for a in range(nb):
        for k in range(4):
            passed(a, k, 1 - c).wait_recv()
    for cp in sends:
        cp.wait_send()
    for cp in local:
        cp.wait()


def _gather_sems(n_in, nb):
    dma = pltpu.SemaphoreType.DMA
    n_ici = 4 * nb + 3 * (n_in - nb)
    return [dma((n_ici,)), dma((n_ici,)), dma((4 * nb,)), dma((4 * nb,)), dma((n_in,))]


def _gather_async(name, halved, whole=()):
    nb, arrs = len(halved), list(halved) + list(whole)
    hbm = pltpu.MemorySpace.HBM
    ins = [jax.new_ref(a, memory_space=hbm) for a in arrs]
    outs = [jax.empty_ref(_sds((NCHIP,) + a.shape, a.dtype), memory_space=hbm) for a in arrs]

    @pl.kernel(mesh=plsc.ScalarSubcoreMesh(axis_name="seq", num_cores=1), name=name,
               scratch_types=tuple(_gather_sems(len(arrs), nb)),
               compiler_params=pltpu.CompilerParams(collective_id=2))
    def launch(send, recv, fsend, frecv, lsem):
        _handshake(_chip_peers())
        _gather_copies(ins, outs, nb, send, recv, fsend, frecv, lsem)

    launch()
    return outs


def _swap_halves(name, grads, after=None):
    n = len(grads)
    hbm = pltpu.MemorySpace.HBM
    ins = [jax.new_ref(g, memory_space=hbm) for g in grads]
    outs = [jax.empty_ref(_sds((NCHIP, g.shape[1] // 2) + g.shape[2:], g.dtype), memory_space=hbm) for g in grads]
    tile = (2 * 8, LANES)
    token = None if after is None else jax.empty_ref(_sds(tile, BF16), memory_space=hbm)

    @pl.kernel(mesh=plsc.ScalarSubcoreMesh(axis_name="seq", num_cores=1), name=name,
               scratch_types=(pltpu.SemaphoreType.DMA((n + 1,)), pltpu.SemaphoreType.DMA((n,))),
               compiler_params=pltpu.CompilerParams(collective_id=1))
    def launch(send, recv):
        x, y, c, _, _, _ = _place()
        sib = (x, y, 1 - c)
        _handshake([sib])
        if after is not None:
            tick = pltpu.make_async_copy(after.at[0, 0, 0, pl.ds(0, tile[0]), pl.ds(0, tile[1])], token, send.at[n])
            tick.start()
            tick.wait()
        cps = []
        for a in range(n):
            h = grads[a].shape[1] // 2
            cps.append(pltpu.make_async_remote_copy(ins[a].at[:, pl.ds((1 - c) * h, h)], outs[a], send.at[a],
                                                    recv.at[a], device_id=sib, device_id_type=MESH))
        for cp in cps:
            cp.start()
        for cp in cps:
            cp.wait()

    launch()
    return outs


def _row_tile(r, cap=256):
    return max(t for t in range(8, cap + 1, 8) if r % t == 0)


def _add_half(name, g, r, c_arr):
    _, l, rows, cols = g.shape
    h = l // 2
    tr = _row_tile(rows, 1056)

    def body(c_ref, g_ref, r_ref, o_ref):
        o_ref[...] = (g_ref[...].astype(F32) + r_ref[...].astype(F32)).astype(BF16)

    blk = (None, None, tr, cols)
    return pl.pallas_call(
        body,
        grid_spec=pltpu.PrefetchScalarGridSpec(
            num_scalar_prefetch=1, grid=(NCHIP, h, rows // tr),
            in_specs=[pl.BlockSpec(blk, lambda j, i, t, c_ref: (j, c_ref[0] * h + i, t, 0)),
                      pl.BlockSpec(blk, lambda j, i, t, c_ref: (j, i, t, 0))],
            out_specs=pl.BlockSpec(blk, lambda j, i, t, c_ref: (j, i, t, 0))),
        out_shape=_sds((NCHIP, h, rows, cols), BF16), name=name,
        compiler_params=_cp("parallel", "parallel", "parallel"),
    )(c_arr, g, r)


def _scatter_async(name, parts, sums, where):
    nb = len(parts)
    ins = [jax.new_ref(p, memory_space=pltpu.MemorySpace.HBM) for p in parts]
    dma = pltpu.SemaphoreType.DMA

    @pl.kernel(mesh=plsc.ScalarSubcoreMesh(axis_name="seq", num_cores=1), name=name,
               scratch_types=(dma((3 * nb,)), dma((3 * nb,)), dma((4 * nb,)), dma((4 * nb,)), dma((nb,))),
               compiler_params=pltpu.CompilerParams(collective_id=3))
    def launch(send, recv, fsend, frecv, lsem):
        _handshake(_chip_peers())
        x, y, c, me, chips, cidx = _place(staggered=True)
        sib = (x, y, 1 - c)

        def slot(a, half, chip):
            return sums[a].at[half, chip, pl.ds(where[a], 1)]

        local = [pltpu.make_async_copy(ins[a].at[me], slot(a, c, me), lsem.at[a]) for a in range(nb)]
        for cp in local:
            cp.start()

        def ici(a, j):
            return pltpu.make_async_remote_copy(ins[a].at[cidx[j]], slot(a, c, me), send.at[a * 3 + j],
                                                recv.at[a * 3 + j], device_id=(*chips[j], c), device_id_type=MESH)

        def landed(a, j):
            dst = slot(a, c, cidx[j])
            return pltpu.make_async_remote_copy(dst, dst, send.at[a * 3 + j], recv.at[a * 3 + j],
                                                device_id=(*chips[j], c), device_id_type=MESH)

        def passed(a, j, who):
            dst = slot(a, who, me if j == 3 else cidx[j])
            src = ins[a].at[me] if j == 3 else dst
            return pltpu.make_async_remote_copy(src, dst, fsend.at[a * 4 + j], frecv.at[a * 4 + j], device_id=sib,
                                                device_id_type=MESH)

        sends = [ici(a, j) for a in range(nb) for j in range(3)] + [passed(a, 3, c) for a in range(nb)]
        for cp in sends:
            cp.start()
        for a in range(nb):
            for j in range(3):
                landed(a, j).wait_recv()
                cp = passed(a, j, c)
                cp.start()
                sends.append(cp)
        for a in range(nb):
            for j in range(4):
                passed(a, j, 1 - c).wait_recv()
        for cp in sends:
            cp.wait_send()
        for cp in local:
            cp.wait()

    launch()


def _exchange_small(small, rep):
    def body(small_in, rep_in, small_out, rep_out, lsem, ssend, srecv):
        x, y, c, me, _, _ = _place()
        dev = 4 * x + 2 * y + c
        local = [pltpu.make_async_copy(small_in.at[me], small_out.at[dev], lsem.at[0]),
                 pltpu.make_async_copy(rep_in, rep_out.at[dev], lsem.at[1])]
        for cp in local:
            cp.start()

        def peer(r):
            return (1 - x if r & 4 else x), (1 - y if r & 2 else y), (1 - c if r & 1 else c)

        def tiny(r, which):
            px, py, pc = peer(r)
            k = (r - 1) * 2 + which
            if which == 0:
                return pltpu.make_async_remote_copy(small_in.at[2 * px + py], small_out.at[dev], ssend.at[k],
                                                    srecv.at[k], device_id=(px, py, pc), device_id_type=MESH)
            return pltpu.make_async_remote_copy(rep_in, rep_out.at[dev], ssend.at[k], srecv.at[k],
                                                device_id=(px, py, pc), device_id_type=MESH)

        def tiny_landed(r, which):
            px, py, pc = peer(r)
            k = (r - 1) * 2 + which
            dst = (small_out if which == 0 else rep_out).at[4 * px + 2 * py + pc]
            return pltpu.make_async_remote_copy(dst, dst, ssend.at[k], srecv.at[k], device_id=(px, py, pc),
                                                device_id_type=MESH)

        sends = [tiny(r, w) for r in range(1, NDEV) for w in range(2)]
        for cp in sends:
            cp.start()
        for r in range(1, NDEV):
            for w in range(2):
                tiny_landed(r, w).wait_recv()
        for cp in sends:
            cp.wait_send()
        for cp in local:
            cp.wait()

    dma = pltpu.SemaphoreType.DMA
    return pl.pallas_call(
        body, in_specs=[ANY] * 2, out_specs=[ANY] * 2,
        out_shape=[_sds((NDEV,) + small.shape[1:], F32), _sds((NDEV,) + rep.shape, F32)],
        scratch_shapes=[dma((2,)), dma((2 * (NDEV - 1),)), dma((2 * (NDEV - 1),))], name="exchange_small_grads",
    )(small, rep)


def _adamw_math(w, g, m, v):
    m = B1 * m + (1.0 - B1) * g
    v = B2 * v + (1.0 - B2) * (g * g)
    m_hat = m / (1.0 - B1 ** STEP)
    v_hat = v / (1.0 - B2 ** STEP)
    return -LR * (m_hat / (jnp.sqrt(v_hat) + AEPS) + WD * w), m, v


def _adamw_big(name, w, m, v, parts, row0=0, first=0, outs=None):
    _, _, rows, cols = w.shape
    n = parts.shape[2]
    tr = _row_tile(rows, 352)
    t0 = row0 // tr

    def body(w_ref, m_ref, v_ref, p_ref, *rest):
        g_ref, d_ref, nm_ref, nv_ref = rest[-4:]
        g = p_ref[0].astype(F32)
        for q in range(1, NCHIP):
            g = g + p_ref[q].astype(F32)
        d, nm, nv = _adamw_math(w_ref[...], g, m_ref[...], v_ref[...])
        g_ref[...], d_ref[...], nm_ref[...], nv_ref[...] = g, d, nm, nv

    spec = pl.BlockSpec((None, None, tr, cols), lambda i, p, t: (first + i, p, t, 0))
    na = 0 if outs is None else 4
    return pl.pallas_call(
        body, grid=(n, 2, rows // tr),
        in_specs=[spec, spec, spec,
                  pl.BlockSpec((None, NCHIP, None, tr, cols), lambda i, p, t: (p, 0, i, t0 + t, 0))] + [ANY] * na,
        out_specs=[spec] * 4, out_shape=[_sds(w.shape)] * 4, input_output_aliases={4 + k: k for k in range(na)},
        name=name, compiler_params=_cp("parallel", "parallel", "parallel"),
    )(w, m, v, parts, *(outs or ()))


def _adamw_small(name, w, m, v, parts):
    def body(w_ref, m_ref, v_ref, p_ref, g_ref, d_ref, nm_ref, nv_ref):
        g = p_ref[0]
        for q in range(1, NDEV):
            g = g + p_ref[q]
        d, nm, nv = _adamw_math(w_ref[...], g, m_ref[...], v_ref[...])
        g_ref[...], d_ref[...], nm_ref[...], nv_ref[...] = g, d, nm, nv

    return pl.pallas_call(body, out_shape=[_sds(w.shape)] * 4, name=name)(w, m, v, parts)


def _pack(arrs, rows):
    flat = jnp.concatenate([a.reshape(-1) for a in arrs])
    return jnp.pad(flat, (0, rows * LANES - flat.shape[0])).reshape(rows, LANES)


def _unpack(packed, shapes):
    flat, out, o = packed.reshape(-1), [], 0
    for s in shapes:
        n = 1
        for d in s:
            n *= d
        out.append(flat[o:o + n].reshape(s))
        o += n
    return out


SMALL_ROWS, REP_ROWS = 200, 16


def kernel(x, norm_w, ffn_w_gate, ffn_w_up, ffn_w_down, mix_w_in, dn_conv_w, attn_sinks, dn_a_log, dn_dt_bias, dn_norm_w, mix_w_out, conv_w_pw1, conv_b_pw1, conv_w_dw, conv_b_dw, conv_ln_w, conv_ln_b, conv_w_pw2, conv_b_pw2, final_norm_w, loss_target, m_norm_w, m_ffn_w_gate, m_ffn_w_up, m_ffn_w_down, m_mix_w_in, m_dn_conv_w, m_attn_sinks, m_dn_a_log, m_dn_dt_bias, m_dn_norm_w, m_mix_w_out, m_conv_w_pw1, m_conv_b_pw1, m_conv_w_dw, m_conv_b_dw, m_conv_ln_w, m_conv_ln_b, m_conv_w_pw2, m_conv_b_pw2, m_final_norm_w, v_norm_w, v_ffn_w_gate, v_ffn_w_up, v_ffn_w_down, v_mix_w_in, v_dn_conv_w, v_attn_sinks, v_dn_a_log, v_dn_dt_bias, v_dn_norm_w, v_mix_w_out, v_conv_w_pw1, v_conv_b_pw1, v_conv_w_dw, v_conv_b_dw, v_conv_ln_w, v_conv_ln_b, v_conv_w_pw2, v_conv_b_pw2, v_final_norm_w):
    small_names = ["norm_w", "dn_conv_w", "conv_b_pw1", "conv_w_dw", "conv_b_dw", "conv_ln_w", "conv_ln_b",
                   "conv_b_pw2"]
    rep_names = ["attn_sinks", "dn_a_log", "dn_dt_bias", "dn_norm_w", "final_norm_w"]
    w = dict(norm_w=norm_w, ffn_w_gate=ffn_w_gate, ffn_w_up=ffn_w_up, ffn_w_down=ffn_w_down, mix_w_in=mix_w_in, dn_conv_w=dn_conv_w, attn_sinks=attn_sinks, dn_a_log=dn_a_log, dn_dt_bias=dn_dt_bias, dn_norm_w=dn_norm_w, mix_w_out=mix_w_out, conv_w_pw1=conv_w_pw1, conv_b_pw1=conv_b_pw1, conv_w_dw=conv_w_dw, conv_b_dw=conv_b_dw, conv_ln_w=conv_ln_w, conv_ln_b=conv_ln_b, conv_w_pw2=conv_w_pw2, conv_b_pw2=conv_b_pw2, final_norm_w=final_norm_w)
    m = dict(norm_w=m_norm_w, ffn_w_gate=m_ffn_w_gate, ffn_w_up=m_ffn_w_up, ffn_w_down=m_ffn_w_down, mix_w_in=m_mix_w_in, dn_conv_w=m_dn_conv_w, attn_sinks=m_attn_sinks, dn_a_log=m_dn_a_log, dn_dt_bias=m_dn_dt_bias, dn_norm_w=m_dn_norm_w, mix_w_out=m_mix_w_out, conv_w_pw1=m_conv_w_pw1, conv_b_pw1=m_conv_b_pw1, conv_w_dw=m_conv_w_dw, conv_b_dw=m_conv_b_dw, conv_ln_w=m_conv_ln_w, conv_ln_b=m_conv_ln_b, conv_w_pw2=m_conv_w_pw2, conv_b_pw2=m_conv_b_pw2, final_norm_w=m_final_norm_w)
    v = dict(norm_w=v_norm_w, ffn_w_gate=v_ffn_w_gate, ffn_w_up=v_ffn_w_up, ffn_w_down=v_ffn_w_down, mix_w_in=v_mix_w_in, dn_conv_w=v_dn_conv_w, attn_sinks=v_attn_sinks, dn_a_log=v_dn_a_log, dn_dt_bias=v_dn_dt_bias, dn_norm_w=v_dn_norm_w, mix_w_out=v_mix_w_out, conv_w_pw1=v_conv_w_pw1, conv_b_pw1=v_conv_b_pw1, conv_w_dw=v_conv_w_dw, conv_b_dw=v_conv_b_dw, conv_ln_w=v_conv_ln_w, conv_ln_b=v_conv_ln_b, conv_w_pw2=v_conv_w_pw2, conv_b_pw2=v_conv_b_pw2, final_norm_w=v_final_norm_w)
    order = ["norm_w", "ffn_w_gate", "ffn_w_up", "ffn_w_down", "mix_w_in", "dn_conv_w", "attn_sinks", "dn_a_log",
             "dn_dt_bias", "dn_norm_w", "mix_w_out", "conv_w_pw1", "conv_b_pw1", "conv_w_dw", "conv_b_dw",
             "conv_ln_w", "conv_ln_b", "conv_w_pw2", "conv_b_pw2", "final_norm_w"]

    small_shapes = [w[n].shape for n in small_names]
    rep_shapes = [w[n].shape for n in rep_names]

    def halves(a):
        return a.reshape(a.shape[:-2] + (2, a.shape[-2] // 2, a.shape[-1]))

    tr = lambda a: jnp.swapaxes(a, -1, -2)
    gate_t, up_t = tr(ffn_w_gate), tr(ffn_w_up)

    def layer_shards(l):
        mix_in, mix_out = (mix_w_in, mix_w_out) if l % 2 == 0 else (conv_w_pw1, conv_w_pw2)
        ffn = jnp.concatenate([gate_t[l], up_t[l], ffn_w_down[l]], axis=1)
        return ([t.astype(BF16) for t in (halves(ffn[0]), halves(mix_in[l // 2]), halves(mix_out[l // 2]))],
                [halves(ffn[1]).astype(BF16)])

    first = layer_shards(0)
    first = (first[0] + [_pack([w[n] for n in small_names], SMALL_ROWS)], first[1])
    first, (gate_t, up_t, ffn_w_down, mix_w_in, mix_w_out, conv_w_pw1, conv_w_pw2) = lax.optimization_barrier(
        (first, (gate_t, up_t, ffn_w_down, mix_w_in, mix_w_out, conv_w_pw1, conv_w_pw2)))
    gathering = [(_gather_async("gather_layer0a", first[0][:3], first[0][3:]),
                  _gather_async("gather_layer0b", first[1]))]
    for l in range(1, DEPTH):
        before, after = layer_shards(l)
        gathering.append((_gather_async(f"gather_layer{l}a", before), _gather_async(f"gather_layer{l}b", after)))
    ffn_block = lambda g: g.reshape(NCHIP, 1, 3 * FS, D)

    def mixer_params(l, w_a, w_b):
        e = l // 2
        w_a = w_a.reshape(NCHIP, D, -1)
        w_b = w_b.reshape(D, D)
        if l % 2 == 0:
            return dict(w_in=w_a, dn_conv_w=sm["dn_conv_w"][e], sinks=_row(attn_sinks[e]), a_log=_row(dn_a_log[e]),
                        dt_bias=_row(dn_dt_bias[e]), dn_norm_w=_row(dn_norm_w[e]), wo_a=w_b[:Q_A], wo_b=w_b[Q_A:])
        return dict(b1a=_row(sm["conv_b_pw1"][e, :D]), b1b=_row(sm["conv_b_pw1"][e, D:]), w1=w_a,
                    w_dw=sm["conv_w_dw"][e], b_dw=_row(sm["conv_b_dw"][e]), ln_w=_row(sm["conv_ln_w"][e]),
                    ln_b=_row(sm["conv_ln_b"][e]), b2=_row(sm["conv_b_pw2"][e]), w2=w_b)

    xs, saved = x[0], []
    for l in range(DEPTH):
        got = [r[...] for r in gathering[l][0]]
        if l == 0:
            per_chip = [_unpack(got[3][q], small_shapes) for q in range(NCHIP)]
            sm = {n: jnp.concatenate([per_chip[q][i] for q in range(NCHIP)], axis=-1)
                  for i, n in enumerate(small_names)}
        else:
            xs, got = lax.optimization_barrier((xs, got))

        def second_ffn(x2, l=l):
            x2, got_b = lax.optimization_barrier((x2, gathering[l][1][0][...]))
            return x2, ffn_block(got_b)

        xs, sv = _layer_fwd(l, xs, sm["norm_w"][l], ffn_block(got[0]), second_ffn, mixer_params(l, got[1], got[2]))
        saved.append(sv)
    loss, dx, dfw = _final("final", xs, _row(final_norm_w), loss_target[0])

    hbm = pltpu.MemorySpace.HBM
    row_shapes = dict(ffn=(3 * FS, D), w_in=(D // 2, IN_COLS // NCHIP), w_out=(D // 8, D), pw1=(D // 2, D // 2),
                      pw2=(D // 8, D))
    new_sums = lambda k, n: jax.empty_ref(_sds((2, NCHIP, n) + row_shapes[k], BF16), memory_space=hbm)
    sums_0 = {k: new_sums(k, 1) for k in ("ffn", "w_in", "w_out")}
    sums = dict(ffn=new_sums("ffn", DEPTH - 1), w_in=new_sums("w_in", 1), w_out=new_sums("w_out", 1),
                pw1=new_sums("pw1", 2), pw2=new_sums("pw2", 2))
    c_arr = lax.axis_index("c").astype(jnp.int32).reshape(1)
    dnorm, gmix = [None] * DEPTH, [None] * DEPTH

    def hand_on(l, grads, swapped):
        def run(dx):
            dx, other = lax.optimization_barrier((dx, [r[...] for r in swapped]))
            parts = [_add_half(f"add_half_{l}_{k}", gg, rr, c_arr) for k, (gg, rr) in enumerate(zip(grads, other))]
            dx, parts = lax.optimization_barrier((dx, parts))
            keys = ("ffn", "w_in", "w_out") if l % 2 == 0 else ("ffn", "pw1", "pw2")
            if l == 0:
                _scatter_async("scatter_grads_0", parts, [sums_0[k] for k in keys], [0, 0, 0])
            else:
                _scatter_async(f"scatter_grads_{l}", parts, [sums[k] for k in keys],
                               [l - 1, 0, 0] if l % 2 == 0 else [l - 1, l // 2, l // 2])
            return dx
        return run

    pending = lambda dx: dx
    for l in reversed(range(DEPTH)):
        dx, dnorm[l], dffn, gmix[l] = _layer_bwd(l, dx, sm["norm_w"][l], saved[l], pending)
        if l % 2 == 0:
            g_a, g_b = gmix[l]["w_in"], jnp.concatenate([gmix[l]["wo_a"], gmix[l]["wo_b"]], axis=0)
        else:
            g_a, g_b = gmix[l]["w1"], gmix[l]["w2"]
        g_a = halves(g_a).astype(BF16)
        g_b = g_b.reshape(NCHIP, 2, D // 8, D).astype(BF16)
        dx, grads = lax.optimization_barrier((dx, [dffn, g_a, g_b]))
        pending = hand_on(l, grads, _swap_halves(f"swap_grads_{l}", grads, sums["ffn"] if l < DEPTH - 1 else None))
    gm, gc = [gmix[0], gmix[2]], [gmix[1], gmix[3]]
    small_g = dict(
        norm_w=jnp.stack(dnorm), dn_conv_w=jnp.stack([gm[e]["dn_conv_w"] for e in range(2)]),
        conv_b_pw1=jnp.stack([jnp.concatenate([gc[e]["b1a"], gc[e]["b1b"]], axis=1)[0] for e in range(2)]),
        conv_w_dw=jnp.stack([gc[e]["w_dw"] for e in range(2)]),
        conv_b_dw=jnp.stack([gc[e]["b_dw"][0] for e in range(2)]),
        conv_ln_w=jnp.stack([gc[e]["ln_w"][0] for e in range(2)]),
        conv_ln_b=jnp.stack([gc[e]["ln_b"][0] for e in range(2)]),
        conv_b_pw2=jnp.stack([gc[e]["b2"][0] for e in range(2)]))
    small_by_chip = jnp.stack([_pack([jnp.split(small_g[n], NCHIP, axis=-1)[q] for n in small_names], SMALL_ROWS)
                               for q in range(NCHIP)])
    rep_g = _pack([jnp.stack([gm[e]["sinks"][0] for e in range(2)]), jnp.stack([gm[e]["a_log"][0] for e in range(2)]),
                   jnp.stack([gm[e]["dt_bias"][0] for e in range(2)]),
                   jnp.stack([gm[e]["dn_norm_w"][0] for e in range(2)]), dfw[0]], REP_ROWS)
    small_sum, rep_sum = _exchange_small(small_by_chip, rep_g)
    dx, small_sum, rep_sum = lax.optimization_barrier((dx, small_sum, rep_sum))
    dx = pending(dx)

    big = (("ffn_w_gate", "ffn", 0), ("ffn_w_up", "ffn", FS), ("ffn_w_down", "ffn", 2 * FS), ("mix_w_in", "w_in", 0),
           ("mix_w_out", "w_out", 0), ("conv_w_pw1", "pw1", 0), ("conv_w_pw2", "pw2", 0))
    views = {n: (tr, tr) if n in ("ffn_w_gate", "ffn_w_up") else (
        (lambda a: a) if w[n].ndim == 4 else halves, lambda o, n=n: o.reshape(w[n].shape)) for n, _, _ in big}
    partial_sums = {k: r[...] for k, r in sums.items()}
    upper = {}
    for n, key, row0 in big:
        view = views[n][0]
        upper[n] = _adamw_big(f"adamw_{n}", view(w[n]), view(m[n]), view(v[n]), partial_sums[key], row0,
                              first=0 if key in ("pw1", "pw2") else 1)
    upper, partial_sums_0 = lax.optimization_barrier((upper, {k: r[...] for k, r in sums_0.items()}))
    res = {}
    for n, key, row0 in big:
        view, back = views[n]
        outs = upper[n] if key not in partial_sums_0 else _adamw_big(
            f"adamw_{n}_0", view(w[n]), view(m[n]), view(v[n]), partial_sums_0[key], row0, first=0, outs=upper[n])
        res[n] = [back(o) for o in outs]
    outs = _adamw_small("adamw_small", *[_pack([d[n] for n in small_names], SMALL_ROWS) for d in (w, m, v)],
                        small_sum)
    for i, n in enumerate(small_names):
        res[n] = [_unpack(o, small_shapes)[i] for o in outs]
    outs = _adamw_small("adamw_replicated", *[_pack([d[n] for n in rep_names], REP_ROWS) for d in (w, m, v)],
                        rep_sum)
    for i, n in enumerate(rep_names):
        res[n] = [_unpack(o, rep_shapes)[i] for o in outs]

    total = lax.psum(loss[0, 0], ("x", "y", "c"))
    return (total, dx[None], *[res[n][0] for n in order], *[res[n][1] for n in order],
            *[res[n][2] for n in order], *[res[n][3] for n in order])
```

```python
import jax
import jax.numpy as jnp
from jax import lax
from jax.experimental import pallas as pl
from jax.experimental.pallas import tpu as pltpu
from jax.experimental.pallas import tpu_sc as plsc

F32, BF16 = jnp.float32, jnp.bfloat16
MESH = pl.DeviceIdType.MESH
ANY = pl.BlockSpec(memory_space=pl.ANY)

T, D, F = 2048, 1024, 2816
DEPTH = 4
EPS = 1e-6
HEADS, HDIM, KV_HEADS, GROUP = 8, 64, 2, 4
WINDOW = BLOCK = 128
CHUNK = 64
NCHUNK = T // CHUNK
Q_A, KV_A, QKV_B, V_B = 512, 128, 1536, 512
IN_COLS = 2832
IN_SPLITS = (0, 512, 640, 768, 2304, 2816, 2832)
NCHIP, NDEV = 4, 8
FS = F // NCHIP
LR, B1, B2, AEPS, WD, STEP = 0.001, 0.9, 0.999, 1e-08, 0.01, 10
V7X_VMEM_BYTES = 64 * 1024 * 1024
VMEM_LIMIT = V7X_VMEM_BYTES * 7 // 8
LANES = 128


def _cp(*sem):
    return pltpu.CompilerParams(dimension_semantics=sem, vmem_limit_bytes=VMEM_LIMIT)


def _sds(shape, dtype=F32):
    return jax.ShapeDtypeStruct(tuple(shape), dtype)


def _full(shape):
    nd = len(shape)
    return pl.BlockSpec(tuple(shape), lambda *_: (0,) * nd)


def _split_bf16(a):
    hi = a.astype(BF16)
    return hi, (a - hi.astype(F32)).astype(BF16)


def _dg(a, b, ca, cb, hi=False):
    if a.ndim == 3 and b.ndim == 3:
        dims = (((ca + 1,), (cb + 1,)), ((0,), (0,)))
    else:
        dims = (((ca,), (cb,)), ((), ()))
    dot = lambda p, q: lax.dot_general(p, q, dims, preferred_element_type=F32)
    if hi:
        a_hi, a_lo = _split_bf16(a.astype(F32))
        b_hi, b_lo = _split_bf16(b.astype(F32))
        return dot(a_hi, b_hi) + (dot(a_hi, b_lo) + dot(a_lo, b_hi))
    return dot(a.astype(BF16), b.astype(BF16))


def _make_mm(hi):
    @jax.custom_vjp
    def nn(a, b):
        return _dg(a, b, 1, 0, hi)

    @jax.custom_vjp
    def nt(a, b):
        return _dg(a, b, 1, 1, hi)

    @jax.custom_vjp
    def tn(a, b):
        return _dg(a, b, 0, 0, hi)

    nn.defvjp(lambda a, b: (_dg(a, b, 1, 0, hi), (a, b)),
              lambda r, g: (_dg(g, r[1], 1, 1, hi).astype(r[0].dtype), _dg(r[0], g, 0, 0, hi).astype(r[1].dtype)))
    nt.defvjp(lambda a, b: (_dg(a, b, 1, 1, hi), (a, b)),
              lambda r, g: (_dg(g, r[1], 1, 0, hi).astype(r[0].dtype), _dg(g, r[0], 0, 0, hi).astype(r[1].dtype)))
    tn.defvjp(lambda a, b: (_dg(a, b, 0, 0, hi), (a, b)),
              lambda r, g: (_dg(r[1], g, 1, 1, hi).astype(r[0].dtype), _dg(r[0], g, 1, 0, hi).astype(r[1].dtype)))
    return nn, nt, tn


_nn, _nt, _tn = _make_mm(False)
_nn_hi = _make_mm(True)[0]


def _rms(x, w):
    return x * lax.rsqrt(jnp.mean(x * x, axis=-1, keepdims=True) + EPS) * w


def _layernorm(x, w, b):
    xc = x - jnp.mean(x, axis=-1, keepdims=True)
    return xc * lax.rsqrt(jnp.mean(xc * xc, axis=-1, keepdims=True) + EPS) * w + b


def _silu(x):
    return x * jax.nn.sigmoid(x)


def _iota2(shape, dim):
    return lax.broadcasted_iota(jnp.int32, shape, dim)


def _flat_weights(lhs_idx, weights):
    specs, ops, lhs_of, where = [], [], [], []
    for a, (k, w) in enumerate(zip(lhs_idx, weights)):
        for q in range(1 if w.ndim == 2 else w.shape[0]):
            specs.append(_full(w.shape) if w.ndim == 2
                         else pl.BlockSpec((None,) + w.shape[1:], lambda i, q=q: (q, 0, 0)))
            ops.append(w)
            lhs_of.append(k)
            where.append((a, None if w.ndim == 2 else q))
    return specs, ops, lhs_of, where


def _blk_fwd(name, pre, lhs_idx, post, toks, smalls, weights, outs, tm=512):
    wspecs, wops, lhs_of, _ = _flat_weights(lhs_idx, weights)
    nt_, ns, nw = len(toks), len(smalls), len(wops)

    def body(*refs):
        tv = [r[...] for r in refs[:nt_]]
        sv = [r[...] for r in refs[nt_:nt_ + ns]]
        wr = refs[nt_ + ns:nt_ + ns + nw]
        orf = refs[nt_ + ns + nw:]
        lhs = pre(tv, sv)
        ys = [_dg(lhs[i], w[...], 1, 0) for i, w in zip(lhs_of, wr)]
        for o_ref, o in zip(orf, post(ys, tv, sv)):
            o_ref[...] = o.astype(o_ref.dtype)

    in_specs = ([pl.BlockSpec((tm, a.shape[1]), lambda i: (i, 0)) for a in toks]
                + [_full(a.shape) for a in smalls] + wspecs)
    out_specs = [pl.BlockSpec((tm, w_), lambda i: (i, 0)) for w_, _ in outs]
    return pl.pallas_call(
        body, grid=(T // tm,), in_specs=in_specs, out_specs=out_specs,
        out_shape=[_sds((T, w_), dt) for w_, dt in outs], name=name, compiler_params=_cp("parallel"),
    )(*toks, *smalls, *wops)


def _blk_bwd(name, pre, lhs_idx, post, toks, smalls, weights, ct_groups, res=None, linear_post=False, tm=256,
             wchunk=512):
    wspecs, wops, lhs_of, where = _flat_weights(lhs_idx, weights)
    nt_, ns, nw, na = len(toks), len(smalls), len(wops), len(weights)
    cts = [a for g in ct_groups for a in g]
    nc = len(cts)
    widths = [sum(a.shape[1] for a in g) for g in ct_groups]
    has_res = res is not None

    def body(*refs):
        p = 0
        tr = refs[p:p + nt_]; p += nt_
        sr = refs[p:p + ns]; p += ns
        wr = refs[p:p + nw]; p += nw
        cr = refs[p:p + nc]; p += nc
        rr = refs[p:p + has_res]; p += has_res
        dtr = refs[p:p + nt_]; p += nt_
        dsr = refs[p:p + ns]; p += ns
        dwr = refs[p:p + na]; p += na
        scr = refs[p:]
        i = pl.program_id(0)

        @pl.when(i == 0)
        def _():
            for r in list(dsr) + list(dwr):
                r[...] = jnp.zeros_like(r)

        tv = [r[...] for r in tr]
        sv = [r[...] for r in sr]
        ctv, q, si = [], 0, 0
        for g in ct_groups:
            if len(g) == 1:
                ctv.append(cr[q][...].astype(F32))
            else:
                off = 0
                for j, a in enumerate(g):
                    scr[si][:, off:off + a.shape[1]] = cr[q + j][...].astype(F32)
                    off += a.shape[1]
                ctv.append(scr[si][...])
                si += 1
            q += len(g)

        lhs, vjp_pre = jax.vjp(lambda *a: tuple(pre(list(a[:nt_]), list(a[nt_:]))), *tv, *sv)
        lhs_b = [l.astype(BF16) for l in lhs]
        ys = [jnp.zeros((tm, w.shape[1]), F32) if linear_post else _dg(lhs_b[k], w[...], 1, 0)
              for k, w in zip(lhs_of, wr)]
        _, vjp_post = jax.vjp(lambda *a: tuple(post(list(a[:nw]), list(a[nw:nw + nt_]), list(a[nw + nt_:]))),
                              *ys, *tv, *sv)
        gp = vjp_post(tuple(ctv))
        dys, dt_post, ds_post = gp[:nw], gp[nw:nw + nt_], gp[nw + nt_:]
        dlhs = [None] * len(lhs)
        for k, w, dy, (a, q) in zip(lhs_of, wr, dys, where):
            dyb = dy.astype(BF16)
            n = w.shape[1]
            for c0 in range(0, n, wchunk):
                c1 = min(n, c0 + wchunk)
                part = _dg(lhs_b[k], dyb[:, c0:c1], 0, 0)
                if q is None:
                    dwr[a][:, c0:c1] += part
                else:
                    dwr[a][q, :, c0:c1] += part
            d = _dg(dyb, w[...], 1, 1)
            dlhs[k] = d if dlhs[k] is None else dlhs[k] + d
        gq = vjp_pre(tuple(d.astype(l.dtype) for d, l in zip(dlhs, lhs)))
        dt_pre, ds_pre = gq[:nt_], gq[nt_:]
        for j in range(nt_):
            d = dt_post[j] + dt_pre[j]
            if j == 0 and has_res:
                d = d + rr[0][...]
            dtr[j][...] = d
        for j in range(ns):
            dsr[j][...] += ds_post[j] + ds_pre[j]

    tok_spec = lambda a: pl.BlockSpec((tm, a.shape[1]), lambda i: (i, 0))
    in_specs = ([tok_spec(a) for a in toks] + [_full(a.shape) for a in smalls] + wspecs
                + [tok_spec(a) for a in cts] + ([tok_spec(res)] if has_res else []))
    out_specs = [tok_spec(a) for a in toks] + [_full(a.shape) for a in smalls] + [_full(w.shape) for w in weights]
    out_shape = ([_sds(a.shape) for a in toks] + [_sds(a.shape) for a in smalls] + [_sds(w.shape) for w in weights])
    scratch = [pltpu.VMEM((tm, wd), F32) for g, wd in zip(ct_groups, widths) if len(g) > 1]
    outs = pl.pallas_call(
        body, grid=(T // tm,), in_specs=in_specs, out_specs=out_specs, out_shape=out_shape,
        scratch_shapes=scratch, name=name, compiler_params=_cp("arbitrary"),
    )(*toks, *smalls, *wops, *cts, *([res] if has_res else []))
    return outs[:nt_], outs[nt_:nt_ + ns], outs[nt_ + ns:]


def _ffn_fwd(name, x, nw, ffn, idx, tm=1024):
    def body(x_ref, nw_ref, wg_ref, wu_ref, wd_ref, o_ref, g_ref, da_ref, db_ref, h_ref):
        s = pl.program_id(1)

        @pl.when(s == 0)
        def _():
            xv = x_ref[...]
            h_ref[...] = _rms(xv, nw_ref[...]).astype(BF16)
            o_ref[...] = xv

        h = h_ref[...]
        a = _dg(h, wg_ref[...], 1, 1)
        b = _dg(h, wu_ref[...], 1, 1)
        sa = jax.nn.sigmoid(a)
        act = a * sa
        gated = (act * b).astype(BF16)
        g_ref[...] = gated
        da_ref[...] = (b * (sa * (1.0 + a * (1.0 - sa)))).astype(BF16)
        db_ref[...] = act.astype(BF16)
        o_ref[...] += 0.5 * _dg(gated, wd_ref[...], 1, 0)

    wspec = lambda k: pl.BlockSpec((None, None, FS, D), lambda i, s: (s, idx, k, 0))
    act = pl.BlockSpec((None, tm, FS), lambda i, s: (s, i, 0))
    return pl.pallas_call(
        body, grid=(T // tm, NCHIP),
        in_specs=[pl.BlockSpec((tm, D), lambda i, s: (i, 0)), _full((1, D)), wspec(0), wspec(1), wspec(2)],
        out_specs=[pl.BlockSpec((tm, D), lambda i, s: (i, 0)), act, act, act,
                   pl.BlockSpec((tm, D), lambda i, s: (i, 0))],
        out_shape=[_sds((T, D))] + [_sds((NCHIP, T, FS), BF16)] * 3 + [_sds((T, D), BF16)],
        name=name, compiler_params=_cp("parallel", "arbitrary"),
    )(x, nw, ffn, ffn, ffn)


def _ffn_bwd(name, x, nw, ffn, idx, pre, dy, gbuf=None, tm=512):
    ni = T // tm

    def body(x_ref, dy_ref, nw_ref, wg_ref, wu_ref, wd_ref, g_ref, fa_ref, fb_ref, h_ref, dx_ref, dnw_ref, dffn_ref,
             dh_acc, ag, au, ad):
        s, i = pl.program_id(0), pl.program_id(1)
        rows = pl.ds(pl.multiple_of(i * tm, tm), tm)

        @pl.when((s == 0) & (i == 0))
        def _():
            dnw_ref[...] = jnp.zeros_like(dnw_ref)

        @pl.when(i == 0)
        def _():
            ag[...] = jnp.zeros_like(ag)
            au[...] = jnp.zeros_like(au)
            ad[...] = jnp.zeros_like(ad)

        hb = h_ref[...]
        dyb = (0.5 * dy_ref[...]).astype(BF16)
        ad[...] += _dg(g_ref[...], dyb, 0, 0)
        dact = _dg(dyb, wd_ref[...], 1, 1)
        da = (dact * fa_ref[...].astype(F32)).astype(BF16)
        db = (dact * fb_ref[...].astype(F32)).astype(BF16)
        ag[...] += _dg(da, hb, 0, 0)
        au[...] += _dg(db, hb, 0, 0)
        dh = _dg(da, wg_ref[...], 1, 0) + _dg(db, wu_ref[...], 1, 0)

        @pl.when(s == 0)
        def _():
            dh_acc[rows, :] = dh

        @pl.when((s > 0) & (s < NCHIP - 1))
        def _():
            dh_acc[rows, :] += dh

        @pl.when(s == NCHIP - 1)
        def _():
            _, vjp_rms = jax.vjp(_rms, x_ref[...], nw_ref[...])
            dx, dnw = vjp_rms(dh_acc[rows, :] + dh)
            dx_ref[...] = dy_ref[...] + dx
            dnw_ref[...] += dnw

        @pl.when(i == ni - 1)
        def _():
            dffn_ref[0:FS, :] = ag[...].astype(BF16)
            dffn_ref[FS:2 * FS, :] = au[...].astype(BF16)
            dffn_ref[2 * FS:, :] = ad[...].astype(BF16)

    wspec = lambda r, k, blk=0: pl.BlockSpec((None, None, r, D), lambda s, i: (s, blk, k, 0),
                                             pipeline_mode=pl.Buffered(1))
    last = lambda s, i: (jnp.where(s == NCHIP - 1, i, 0), 0)
    nb = 0 if gbuf is None else 1
    act = pl.BlockSpec((None, tm, FS), lambda s, i: (s, i, 0))
    tok = pl.BlockSpec((tm, D), lambda s, i: (i, 0))
    return pl.pallas_call(
        lambda *refs: body(*refs[:10], *refs[10 + nb:]), grid=(NCHIP, ni),
        in_specs=[pl.BlockSpec((tm, D), last), tok, _full((1, D)), wspec(FS, 0), wspec(FS, 1), wspec(FS, 2), act, act,
                  act, tok] + [ANY] * nb,
        out_specs=[pl.BlockSpec((tm, D), last), _full((1, D)), wspec(3 * FS, 0, idx)],
        out_shape=[_sds((T, D)), _sds((1, D)), _sds((NCHIP, 2, 3 * FS, D), BF16)],
        input_output_aliases={10 + k: 2 + k for k in range(nb)},
        scratch_shapes=[pltpu.VMEM((T, D), F32)] + [pltpu.VMEM((FS, D), F32)] * 3,
        name=name, compiler_params=_cp("arbitrary", "arbitrary"),
    )(x, dy, nw, ffn, ffn, ffn, *pre, *(() if gbuf is None else (gbuf,)))


CONV_ROWS = 256


def _conv_pad(k):
    return 8 * ((k - 1 + 7) // 8)


def _shifted(win, o):
    n = win.shape[0]
    return (win if o % n == 0 else pltpu.roll(win, (n - o) % n, 0))[0:CONV_ROWS, :]


def _conv_fwd(name, x, w, b, act):
    k_w, c = w.shape
    tc = 256 if c % 256 == 0 else LANES
    pad = _conv_pad(k_w)
    has_b = b is not None

    def body(*refs):
        x_ref, w_ref = refs[0], refs[1]
        b_ref = refs[2] if has_b else None
        y_ref, xp = refs[2 + has_b], refs[3 + has_b]
        xp[0:pad, :] = jnp.zeros((pad, tc), F32)
        xp[pad:, :] = x_ref[...]

        def step(t, carry):
            base = pl.multiple_of(t * CONV_ROWS, CONV_ROWS)
            win = xp[pl.ds(base, CONV_ROWS + pad), :]
            acc = jnp.zeros((CONV_ROWS, tc), F32)
            for k in range(k_w):
                o = pad - (k_w - 1) + k
                acc = acc + w_ref[k:k + 1, :] * _shifted(win, o)
            if has_b:
                acc = acc + b_ref[...]
            y_ref[pl.ds(base, CONV_ROWS), :] = _silu(acc) if act else acc
            return carry

        lax.fori_loop(0, T // CONV_ROWS, step, 0)

    col = lambda r: pl.BlockSpec((r, tc), lambda j: (0, j))
    ins = [x, w] + ([b] if has_b else [])
    return pl.pallas_call(
        body, grid=(c // tc,), in_specs=[col(T), col(k_w)] + ([col(1)] if has_b else []), out_specs=col(T),
        out_shape=_sds((T, c)), scratch_shapes=[pltpu.VMEM((T + pad, tc), F32)], name=name,
        compiler_params=_cp("parallel"),
    )(*ins)


def _conv_bwd(name, x, w, b, act, dy):
    k_w, c = w.shape
    tc = 256 if c % 256 == 0 else LANES
    pad = _conv_pad(k_w)
    has_b = b is not None

    def body(*refs):
        x_ref, w_ref, dy_ref = refs[0], refs[1], refs[2]
        b_ref = refs[3] if has_b else None
        dx_ref, dw_ref, db_ref, xp, dp = refs[3 + has_b:]
        xp[0:pad, :] = jnp.zeros((pad, tc), F32)
        xp[pad:, :] = x_ref[...]
        dp[T:, :] = jnp.zeros((pad, tc), F32)
        dw_ref[...] = jnp.zeros_like(dw_ref)
        db_ref[...] = jnp.zeros_like(db_ref)

        def step1(t, carry):
            base = pl.multiple_of(t * CONV_ROWS, CONV_ROWS)
            d = dy_ref[pl.ds(base, CONV_ROWS), :]
            win = xp[pl.ds(base, CONV_ROWS + pad), :]
            offs = [pad - (k_w - 1) + k for k in range(k_w)]
            if act:
                acc = jnp.zeros((CONV_ROWS, tc), F32)
                for k, o in enumerate(offs):
                    acc = acc + w_ref[k:k + 1, :] * _shifted(win, o)
                if has_b:
                    acc = acc + b_ref[...]
                sg = jax.nn.sigmoid(acc)
                d = d * (sg * (1.0 + acc * (1.0 - sg)))
            dp[pl.ds(base, CONV_ROWS), :] = d
            for k, o in enumerate(offs):
                dw_ref[k:k + 1, :] += jnp.sum(d * _shifted(win, o), axis=0, keepdims=True)
            db_ref[...] += jnp.sum(d, axis=0, keepdims=True)
            return carry

        lax.fori_loop(0, T // CONV_ROWS, step1, 0)

        def step2(t, carry):
            base = pl.multiple_of(t * CONV_ROWS, CONV_ROWS)
            win = dp[pl.ds(base, CONV_ROWS + pad), :]
            acc = jnp.zeros((CONV_ROWS, tc), F32)
            for k in range(k_w):
                o = (k_w - 1) - k
                acc = acc + w_ref[k:k + 1, :] * _shifted(win, o)
            dx_ref[pl.ds(base, CONV_ROWS), :] = acc
            return carry

        lax.fori_loop(0, T // CONV_ROWS, step2, 0)

    col = lambda r: pl.BlockSpec((r, tc), lambda j: (0, j))
    ins = [x, w, dy] + ([b] if has_b else [])
    return pl.pallas_call(
        body, grid=(c // tc,), in_specs=[col(T), col(k_w), col(T)] + ([col(1)] if has_b else []),
        out_specs=[col(T), col(k_w), col(1)], out_shape=[_sds((T, c)), _sds((k_w, c)), _sds((1, c))],
        scratch_shapes=[pltpu.VMEM((T + pad, tc), F32), pltpu.VMEM((T + pad, tc), F32)], name=name,
        compiler_params=_cp("parallel"),
    )(*ins)


def _attn_consts(n):
    i = _iota2((BLOCK, 2 * BLOCK), 0)
    j = _iota2((BLOCK, 2 * BLOCK), 1)
    dist = i + BLOCK - j
    valid = (dist >= 0) & (dist < WINDOW) & ((n > 0) | (j >= BLOCK))
    return dist.astype(F32), valid


def _attn_block(q4, kk, vv, sinks, dist, valid, kv):
    outs = []
    lane = _iota2((1, HEADS), 1)
    for g in range(GROUP):
        h = kv * GROUP + g
        slope = 2.0 ** (-8.0 * (h + 1) / HEADS)
        s = _nt(q4[:, g * HDIM:(g + 1) * HDIM], kk) * (HDIM ** -0.5)
        s = jnp.where(valid, s - slope * dist, -1e30)
        sink = jnp.sum(jnp.where(lane == h, sinks, 0.0), axis=1, keepdims=True)
        m = jnp.maximum(jnp.max(s, axis=-1, keepdims=True), sink)
        e = jnp.exp(s - m)
        p = e / (jnp.sum(e, axis=-1, keepdims=True) + jnp.exp(sink - m))
        outs.append(_nn(p, vv))
    return tuple(outs)


def _attn_fwd(name, qa, ka, va, sinks):
    def body(q_ref, k_ref, v_ref, s_ref, o_ref, kp, vp):
        kp[0:BLOCK, :] = jnp.zeros((BLOCK, KV_A), F32)
        vp[0:BLOCK, :] = jnp.zeros((BLOCK, KV_A), F32)
        kp[BLOCK:, :] = k_ref[...]
        vp[BLOCK:, :] = v_ref[...]
        sinks_v = s_ref[...]

        def step(n, carry):
            r = pl.multiple_of(n * BLOCK, BLOCK)
            dist, valid = _attn_consts(n)
            k2 = kp[pl.ds(r, 2 * BLOCK), :]
            v2 = vp[pl.ds(r, 2 * BLOCK), :]
            for kv in range(KV_HEADS):
                q4 = q_ref[pl.ds(r, BLOCK), kv * GROUP * HDIM:(kv + 1) * GROUP * HDIM]
                og = _attn_block(q4, k2[:, kv * HDIM:(kv + 1) * HDIM], v2[:, kv * HDIM:(kv + 1) * HDIM], sinks_v,
                                 dist, valid, kv)
                for g in range(GROUP):
                    h = kv * GROUP + g
                    o_ref[pl.ds(r, BLOCK), h * HDIM:(h + 1) * HDIM] = og[g]
            return carry

        lax.fori_loop(0, T // BLOCK, step, 0)

    return pl.pallas_call(
        body, out_shape=_sds((T, Q_A)),
        scratch_shapes=[pltpu.VMEM((T + BLOCK, KV_A), F32), pltpu.VMEM((T + BLOCK, KV_A), F32)], name=name,
        compiler_params=pltpu.CompilerParams(vmem_limit_bytes=VMEM_LIMIT),
    )(qa, ka, va, sinks)


def _attn_bwd(name, qa, ka, va, sinks, do):
    def body(q_ref, k_ref, v_ref, s_ref, do_ref, dq_ref, dk_ref, dv_ref, ds_ref, kp, vp, dkp, dvp):
        kp[0:BLOCK, :] = jnp.zeros((BLOCK, KV_A), F32)
        vp[0:BLOCK, :] = jnp.zeros((BLOCK, KV_A), F32)
        kp[BLOCK:, :] = k_ref[...]
        vp[BLOCK:, :] = v_ref[...]
        dkp[...] = jnp.zeros_like(dkp)
        dvp[...] = jnp.zeros_like(dvp)
        ds_ref[...] = jnp.zeros_like(ds_ref)
        sinks_v = s_ref[...]

        def step(n, carry):
            r = pl.multiple_of(n * BLOCK, BLOCK)
            dist, valid = _attn_consts(n)
            k2 = kp[pl.ds(r, 2 * BLOCK), :]
            v2 = vp[pl.ds(r, 2 * BLOCK), :]
            for kv in range(KV_HEADS):
                cols = slice(kv * HDIM, (kv + 1) * HDIM)
                q4 = q_ref[pl.ds(r, BLOCK), kv * GROUP * HDIM:(kv + 1) * GROUP * HDIM]
                _, vjp = jax.vjp(lambda q, k, v, s: _attn_block(q, k, v, s, dist, valid, kv),
                                 q4, k2[:, cols], v2[:, cols], sinks_v)
                cts = tuple(do_ref[pl.ds(r, BLOCK), (kv * GROUP + g) * HDIM:(kv * GROUP + g + 1) * HDIM]
                            for g in range(GROUP))
                dq4, dkk, dvv, dsk = vjp(cts)
                dq_ref[pl.ds(r, BLOCK), kv * GROUP * HDIM:(kv + 1) * GROUP * HDIM] = dq4
                dkp[pl.ds(r, 2 * BLOCK), cols] += dkk
                dvp[pl.ds(r, 2 * BLOCK), cols] += dvv
                ds_ref[...] += dsk
            return carry

        lax.fori_loop(0, T // BLOCK, step, 0)
        dk_ref[...] = dkp[BLOCK:, :]
        dv_ref[...] = dvp[BLOCK:, :]

    pad = lambda: pltpu.VMEM((T + BLOCK, KV_A), F32)
    return pl.pallas_call(
        body, out_shape=[_sds((T, Q_A)), _sds((T, KV_A)), _sds((T, KV_A)), _sds((1, HEADS))],
        scratch_shapes=[pad(), pad(), pad(), pad()], name=name,
        compiler_params=pltpu.CompilerParams(vmem_limit_bytes=VMEM_LIMIT),
    )(qa, ka, va, sinks, do)


def _dn_consts():
    i = _iota2((CHUNK, CHUNK), 0)
    j = _iota2((CHUNK, CHUNK), 1)
    return dict(causal=i >= j, strict=i > j, ltri=(i >= j).astype(F32),
                last=(_iota2((CHUNK, 1), 0) == CHUNK - 1).astype(F32))


def _l2norm(x):
    return x * lax.rsqrt(jnp.sum(x * x, axis=-1, keepdims=True) + EPS)


def _head_cols(m):
    lane = _iota2((1, HEADS), 1)
    return jnp.concatenate([jnp.sum(jnp.where(lane == h, m, 0.0), axis=1, keepdims=True)[None]
                            for h in range(HEADS)], axis=0)


@jax.custom_vjp
def _unit_lower_inverse(low, known):
    if known is not None:
        return known
    inv = (_iota2((CHUNK, CHUNK), 0) == _iota2((CHUNK, CHUNK), 1)).astype(F32) - low
    pw = low
    for _ in range(5):
        pw = _dg(pw, pw, 1, 0, True)
        inv = inv + _dg(inv, pw, 1, 0, True)
    return inv


def _unit_lower_inverse_fwd(low, known):
    inv = _unit_lower_inverse(low, known)
    return inv, (inv, known)


def _unit_lower_inverse_bwd(res, g):
    inv, known = res
    d_low = -_dg(inv, _dg(g, inv, 1, 1, True), 0, 0, True)
    return d_low, (None if known is None else jnp.zeros_like(known))


_unit_lower_inverse.defvjp(_unit_lower_inverse_fwd, _unit_lower_inverse_bwd)


def _dn_local(q3, k3, v3, braw, araw, alog, dtb, cs, known_inv=None):
    q = _l2norm(q3) * (HDIM ** -0.5)
    k = _l2norm(k3)
    g = -jnp.exp(alog) * jax.nn.softplus(araw + dtb)
    gc_all = _nn_hi(cs["ltri"], g)
    egc_all = jnp.exp(gc_all)
    beta, gc, egc = _head_cols(jax.nn.sigmoid(braw)), _head_cols(gc_all), _head_cols(egc_all)
    a = jnp.broadcast_to(gc, (HEADS, CHUNK, CHUNK))
    diff = a - jnp.swapaxes(a, 1, 2)
    decay = jnp.where(cs["causal"], jnp.exp(jnp.where(cs["causal"], diff, 0.0)), 0.0)
    kb = k * beta
    low = jnp.where(cs["strict"], _nt(kb, k) * decay, 0.0)
    inv = _unit_lower_inverse(low, known_inv)
    u = _nn_hi(inv, v3 * beta)
    w = _nn_hi(inv, kb * egc)
    attn = _nt(q, k) * decay
    gc_last = jnp.sum(gc * cs["last"], axis=1, keepdims=True)
    return u, w, attn, q * egc, k * jnp.exp(gc_last - gc), egc_all, inv


def _heads3(ref, off=0):
    return jnp.concatenate([ref[:, off + h * HDIM:off + (h + 1) * HDIM][None] for h in range(HEADS)], axis=0)


def _dn_local_fwd(name, qkv, ba, alog, dtb):
    def body(qkv_ref, ba_ref, al_ref, dt_ref, u_ref, w_ref, at_ref, qd_ref, kd_ref, eg_ref, inv_ref):
        bav = ba_ref[...]
        outs = _dn_local(_heads3(qkv_ref), _heads3(qkv_ref, 512), _heads3(qkv_ref, 1024), bav[:, :HEADS],
                         bav[:, HEADS:], al_ref[...], dt_ref[...], _dn_consts())
        for r, o in zip((u_ref, w_ref, at_ref, qd_ref, kd_ref, inv_ref), outs[:5] + outs[6:]):
            _unheads(r, o)
        eg_ref[...] = outs[5]

    row = lambda w_: pl.BlockSpec((CHUNK, w_), lambda n: (n, 0))
    return pl.pallas_call(
        body, grid=(NCHUNK,), in_specs=[row(QKV_B), row(2 * HEADS), _full((1, HEADS)), _full((1, HEADS))],
        out_specs=[row(V_B)] * 5 + [row(HEADS), row(V_B)],
        out_shape=[_sds((T, V_B))] * 5 + [_sds((T, HEADS)), _sds((T, V_B))], name=name,
        compiler_params=_cp("parallel"),
    )(qkv, ba, alog, dtb)


def _dn_local_bwd(name, qkv, ba, alog, dtb, inv, cts):
    def body(qkv_ref, ba_ref, al_ref, dt_ref, inv_ref, du_ref, dw_ref, dat_ref, dqd_ref, dkd_ref, deg_ref,
             dqkv_ref, dba_ref, dal_ref, ddt_ref):
        @pl.when(pl.program_id(0) == 0)
        def _():
            dal_ref[...] = jnp.zeros_like(dal_ref)
            ddt_ref[...] = jnp.zeros_like(ddt_ref)

        cs = _dn_consts()
        bav = ba_ref[...]
        known = _heads3(inv_ref)
        _, vjp = jax.vjp(lambda *a: _dn_local(*a, cs, known)[:6], _heads3(qkv_ref), _heads3(qkv_ref, 512),
                         _heads3(qkv_ref, 1024), bav[:, :HEADS], bav[:, HEADS:], al_ref[...], dt_ref[...])
        dq, dk, dv, dbr, dar, dal, ddt = vjp((_heads3(du_ref), _heads3(dw_ref), _heads3(dat_ref), _heads3(dqd_ref),
                                              _heads3(dkd_ref), deg_ref[...]))
        for h in range(HEADS):
            dqkv_ref[:, h * HDIM:(h + 1) * HDIM] = dq[h]
            dqkv_ref[:, 512 + h * HDIM:512 + (h + 1) * HDIM] = dk[h]
            dqkv_ref[:, 1024 + h * HDIM:1024 + (h + 1) * HDIM] = dv[h]
        dba_ref[:, :HEADS] = dbr
        dba_ref[:, HEADS:] = dar
        dal_ref[...] += dal
        ddt_ref[...] += ddt

    row = lambda w_: pl.BlockSpec((CHUNK, w_), lambda n: (n, 0))
    return pl.pallas_call(
        body, grid=(NCHUNK,),
        in_specs=[row(QKV_B), row(2 * HEADS), _full((1, HEADS)), _full((1, HEADS))] + [row(V_B)] * 6 + [row(HEADS)],
        out_specs=[row(QKV_B), row(2 * HEADS), _full((1, HEADS)), _full((1, HEADS))],
        out_shape=[_sds((T, QKV_B)), _sds((T, 2 * HEADS)), _sds((1, HEADS)), _sds((1, HEADS))], name=name,
        compiler_params=_cp("arbitrary"),
    )(qkv, ba, alog, dtb, inv, *cts)


def _dn_step(s, u, w, attn, qd, kd, egc, z, nw):
    last = (_iota2((CHUNK, 1), 0) == CHUNK - 1).astype(F32)
    gl = jnp.sum(_head_cols(egc) * last, axis=1, keepdims=True)
    v_new = u - _nn(w, s)
    o = _nn(qd, s) + _nn(attn, v_new)
    s_new = s * gl + _tn(kd, v_new)
    return s_new, _rms(o, nw) * _silu(z)


def _unheads(ref, v3):
    for h in range(HEADS):
        ref[:, h * HDIM:(h + 1) * HDIM] = v3[h]


def _dn_rec_fwd(name, u, w, attn, qd, kd, egc, z, nw):
    def body(u_ref, w_ref, at_ref, qd_ref, kd_ref, eg_ref, z_ref, nw_ref, o_ref, ss_ref, s_scr):
        @pl.when(pl.program_id(0) == 0)
        def _():
            s_scr[...] = jnp.zeros_like(s_scr)

        s = s_scr[...]
        ss_ref[...] = s
        s_new, on = _dn_step(s, _heads3(u_ref), _heads3(w_ref), _heads3(at_ref), _heads3(qd_ref), _heads3(kd_ref),
                             eg_ref[...], _heads3(z_ref), nw_ref[...])
        s_scr[...] = s_new
        _unheads(o_ref, on)

    row = lambda w_: pl.BlockSpec((CHUNK, w_), lambda n: (n, 0))
    return pl.pallas_call(
        body, grid=(NCHUNK,), in_specs=[row(V_B)] * 5 + [row(HEADS), row(V_B), _full((1, HDIM))],
        out_specs=[row(V_B), pl.BlockSpec((None, HEADS, HDIM, HDIM), lambda n: (n, 0, 0, 0))],
        out_shape=[_sds((T, V_B)), _sds((NCHUNK, HEADS, HDIM, HDIM))],
        scratch_shapes=[pltpu.VMEM((HEADS, HDIM, HDIM), F32)], name=name, compiler_params=_cp("arbitrary"),
    )(u, w, attn, qd, kd, egc, z, nw)


def _dn_rec_bwd(name, u, w, attn, qd, kd, egc, z, nw, ss, do):
    def body(u_ref, w_ref, at_ref, qd_ref, kd_ref, eg_ref, z_ref, nw_ref, ss_ref, do_ref,
             du_ref, dw_ref, dat_ref, dqd_ref, dkd_ref, deg_ref, dz_ref, dnw_ref, ds_scr):
        @pl.when(pl.program_id(0) == 0)
        def _():
            ds_scr[...] = jnp.zeros_like(ds_scr)
            dnw_ref[...] = jnp.zeros_like(dnw_ref)

        _, vjp = jax.vjp(_dn_step, ss_ref[...], _heads3(u_ref), _heads3(w_ref), _heads3(at_ref), _heads3(qd_ref),
                         _heads3(kd_ref), eg_ref[...], _heads3(z_ref), nw_ref[...])
        ds, du, dw, dat, dqd, dkd, deg, dz, dnw = vjp((ds_scr[...], _heads3(do_ref)))
        ds_scr[...] = ds
        for r, v in zip((du_ref, dw_ref, dat_ref, dqd_ref, dkd_ref, dz_ref), (du, dw, dat, dqd, dkd, dz)):
            _unheads(r, v)
        deg_ref[...] = deg
        dnw_ref[...] += dnw

    row = lambda w_: pl.BlockSpec((CHUNK, w_), lambda n: (NCHUNK - 1 - n, 0))
    return pl.pallas_call(
        body, grid=(NCHUNK,),
        in_specs=[row(V_B)] * 5 + [row(HEADS), row(V_B), _full((1, HDIM)),
                                   pl.BlockSpec((None, HEADS, HDIM, HDIM), lambda n: (NCHUNK - 1 - n, 0, 0, 0)),
                                   row(V_B)],
        out_specs=[row(V_B)] * 5 + [row(HEADS), row(V_B), _full((1, HDIM))],
        out_shape=[_sds((T, V_B))] * 5 + [_sds((T, HEADS)), _sds((T, V_B)), _sds((1, HDIM))],
        scratch_shapes=[pltpu.VMEM((HEADS, HDIM, HDIM), F32)], name=name, compiler_params=_cp("arbitrary"),
    )(u, w, attn, qd, kd, egc, z, nw, ss, do)


def _final(name, x, fw, target, tm=512):
    def body(x_ref, fw_ref, t_ref, l_ref, dx_ref, dfw_ref):
        @pl.when(pl.program_id(0) == 0)
        def _():
            l_ref[...] = jnp.zeros_like(l_ref)
            dfw_ref[...] = jnp.zeros_like(dfw_ref)

        tv = t_ref[...]

        def f(xv, fwv):
            err = _rms(xv, fwv) - tv
            per_tok = jnp.mean(err * err, axis=-1, keepdims=True)
            return 0.5 * jnp.sum(per_tok, axis=0, keepdims=True)

        loss, vjp = jax.vjp(f, x_ref[...], fw_ref[...])
        dx, dfw = vjp(jnp.ones((1, 1), F32))
        l_ref[...] += loss
        dx_ref[...] = dx
        dfw_ref[...] += dfw

    tok = pl.BlockSpec((tm, D), lambda i: (i, 0))
    return pl.pallas_call(
        body, grid=(T // tm,), in_specs=[tok, _full((1, D)), tok], out_specs=[_full((1, 1)), tok, _full((1, D))],
        out_shape=[_sds((1, 1)), _sds((T, D)), _sds((1, D))], name=name, compiler_params=_cp("arbitrary"),
    )(x, fw, target)


def _m1_pre(tv, sv):
    return [_rms(tv[0], sv[0])]


def _m1_post(ys, tv, sv):
    return (jnp.concatenate(ys, axis=1),)


def _m1_post_split(ys, tv, sv):
    proj = jnp.concatenate(ys, axis=1)
    return tuple(proj[:, a:b] for a, b in zip(IN_SPLITS[:-1], IN_SPLITS[1:]))


def _m5_pre(tv, sv):
    return [tv[1], tv[2]]


def _m5_post(ys, tv, sv):
    return (tv[0] + ys[0] + ys[1],)


def _c1_pre(tv, sv):
    return [_rms(tv[0], sv[0])]


def _c1_post(ys, tv, sv):
    return ((jnp.concatenate(ys[:2], axis=1) + sv[1]) * jax.nn.sigmoid(jnp.concatenate(ys[2:], axis=1) + sv[2]),)


def _c3_pre(tv, sv):
    return [_silu(_layernorm(tv[0], sv[0], sv[1]))]


def _c3_post(ys, tv, sv):
    return (tv[1] + ys[0] + sv[2],)


def _row(v):
    return v.reshape(1, -1)


def _mixer_fwd(tag, x, p):
    parts = _blk_fwd(f"m1_fwd_{tag}", _m1_pre, [0], _m1_post_split, [x], [p["nw"]], [p["w_in"]],
                     [(b - a, F32) for a, b in zip(IN_SPLITS[:-1], IN_SPLITS[1:])])
    qa, ka, va, qkvb, z, ba = parts
    att = _attn_fwd(f"attn_fwd_{tag}", qa, ka, va, p["sinks"])
    qkvc = _conv_fwd(f"dnconv_fwd_{tag}", qkvb, p["dn_conv_w"], None, True)
    *loc, inv = _dn_local_fwd(f"dnloc_fwd_{tag}", qkvc, ba, p["a_log"], p["dt_bias"])
    og, ss = _dn_rec_fwd(f"dnrec_fwd_{tag}", *loc, z, p["dn_norm_w"])
    (out,) = _blk_fwd(f"m5_fwd_{tag}", _m5_pre, [0, 1], _m5_post, [x, att, og], [], [p["wo_a"], p["wo_b"]],
                      [(D, F32)])
    return out, dict(x=x, qa=qa, ka=ka, va=va, qkvb=qkvb, z=z, ba=ba, att=att, qkvc=qkvc, loc=loc, inv=inv, og=og,
                     ss=ss)


def _mixer_bwd(tag, dy, p, s):
    (dxa, datt, dog), _, (dwo_a, dwo_b) = _blk_bwd(f"m5_bwd_{tag}", _m5_pre, [0, 1], _m5_post,
                                                   [s["x"], s["att"], s["og"]], [], [p["wo_a"], p["wo_b"]], [[dy]],
                                                   linear_post=True, tm=512)
    rec = _dn_rec_bwd(f"dnrec_bwd_{tag}", *s["loc"], s["z"], p["dn_norm_w"], s["ss"], dog)
    dz, dnw_dn = rec[6], rec[7]
    dqkvc, dba, dalog, ddtb = _dn_local_bwd(f"dnloc_bwd_{tag}", s["qkvc"], s["ba"], p["a_log"], p["dt_bias"],
                                            s["inv"], rec[:6])
    dqkvb, dconvw, _ = _conv_bwd(f"dnconv_bwd_{tag}", s["qkvb"], p["dn_conv_w"], None, True, dqkvc)
    dqa, dka, dva, dsinks = _attn_bwd(f"attn_bwd_{tag}", s["qa"], s["ka"], s["va"], p["sinks"], datt)
    (dx,), (dnw,), (dw_in,) = _blk_bwd(f"m1_bwd_{tag}", _m1_pre, [0], _m1_post, [s["x"]], [p["nw"]], [p["w_in"]],
                                       [[dqa, dka, dva, dqkvb, dz, dba]], res=dxa, linear_post=True)
    return dx, dict(nw=dnw, w_in=dw_in, wo_a=dwo_a, wo_b=dwo_b, dn_conv_w=dconvw, sinks=dsinks, a_log=dalog,
                    dt_bias=ddtb, dn_norm_w=dnw_dn)


def _conformer_fwd(tag, x, p):
    (glu,) = _blk_fwd(f"c1_fwd_{tag}", _c1_pre, [0], _c1_post, [x], [p["nw"], p["b1a"], p["b1b"]], [p["w1"]],
                      [(D, F32)])
    cc = _conv_fwd(f"dwconv_fwd_{tag}", glu, p["w_dw"], p["b_dw"], False)
    (out,) = _blk_fwd(f"c3_fwd_{tag}", _c3_pre, [0], _c3_post, [cc, x], [p["ln_w"], p["ln_b"], p["b2"]], [p["w2"]],
                      [(D, F32)])
    return out, dict(x=x, glu=glu, cc=cc)


def _conformer_bwd(tag, dy, p, s):
    (dcc, dxa), (dlnw, dlnb, db2), (dw2,) = _blk_bwd(f"c3_bwd_{tag}", _c3_pre, [0], _c3_post, [s["cc"], s["x"]],
                                                     [p["ln_w"], p["ln_b"], p["b2"]], [p["w2"]], [[dy]],
                                                     linear_post=True, tm=512)
    dglu, dwdw, dbdw = _conv_bwd(f"dwconv_bwd_{tag}", s["glu"], p["w_dw"], p["b_dw"], False, dcc)
    (dx,), (dnw, db1a, db1b), (dw1,) = _blk_bwd(f"c1_bwd_{tag}", _c1_pre, [0], _c1_post, [s["x"]],
                                                [p["nw"], p["b1a"], p["b1b"]], [p["w1"]], [[dglu]], res=dxa)
    return dx, dict(nw=dnw, b1a=db1a, b1b=db1b, w1=dw1, w_dw=dwdw, b_dw=dbdw, ln_w=dlnw, ln_b=dlnb, b2=db2, w2=dw2)


def _layer_fwd(l, x, nw, ffn_a, get_ffn_b, p):
    x1, *pre_a = _ffn_fwd(f"ffn_fwd_{l}a", x, _row(nw[0]), ffn_a, 0)
    p = dict(p, nw=_row(nw[1]))
    x2, sv = (_mixer_fwd if l % 2 == 0 else _conformer_fwd)(str(l), x1, p)
    x2, ffn_b = get_ffn_b(x2)
    out, *pre_b = _ffn_fwd(f"ffn_fwd_{l}b", x2, _row(nw[2]), ffn_b, 0)
    return out, (x, x2, p, sv, pre_a, pre_b, ffn_a, ffn_b)


def _layer_bwd(l, dx, nw, saved, after_first=lambda dx: dx):
    x0, x2, p, sv, pre_a, pre_b, ffn_a, ffn_b = saved
    dx, dn2, dffn = _ffn_bwd(f"ffn_bwd_{l}b", x2, _row(nw[2]), ffn_b, 1, pre_b, dx)
    dx = after_first(dx)
    dx, dmix = (_mixer_bwd if l % 2 == 0 else _conformer_bwd)(str(l), dx, p, sv)
    dx, dn0, dffn = _ffn_bwd(f"ffn_bwd_{l}a", x0, _row(nw[0]), ffn_a, 0, pre_a, dx, dffn)
    return dx, jnp.concatenate([dn0, dmix.pop("nw"), dn2], axis=0), dffn, dmix


def _place(staggered=False):
    x, y, c = lax.axis_index("x"), lax.axis_index("y"), lax.axis_index("c")
    s = c if staggered else 0
    first, second = (x + (1 - s) * (1 - 2 * x), y + s * (1 - 2 * y)), (x + s * (1 - 2 * x), y + (1 - s) * (1 - 2 * y))
    chips = [first, second, (1 - x, 1 - y)]
    return x, y, c, 2 * x + y, chips, [2 * px + py for px, py in chips]


def _handshake(peers):
    barrier = pltpu.get_barrier_semaphore()
    for p in peers:
        pl.semaphore_signal(barrier, inc=1, device_id=p, device_id_type=MESH)
    pl.semaphore_wait(barrier, len(peers))


def _chip_peers():
    x, y, c, _, chips, _ = _place()
    return [(*chip, c) for chip in chips] + [(x, y, 1 - c)]


def _gather_copies(ins, outs, nb, send, recv, fsend, frecv, lsem):
    n_in = len(ins)
    x, y, c, me, chips, cidx = _place(staggered=True)
    sib = (x, y, 1 - c)
    local = [pltpu.make_async_copy(ins[a], outs[a].at[me], lsem.at[a]) for a in range(n_in)]

    def region(a, k, who):
        if k < 2:
            return outs[a].at[cidx[k], pl.ds(who, 1)]
        r = ins[a].shape[1] // 2
        return outs[a].at[cidx[2], pl.ds(who, 1), pl.ds((k - 2) * r, r)]

    def hop(a, k):
        if k < 2:
            src, dst = ins[a].at[pl.ds(c, 1)], outs[a].at[me, pl.ds(c, 1)]
        else:
            r = ins[a].shape[1] // 2
            src = dst = outs[a].at[cidx[3 - k], pl.ds(c, 1), pl.ds((k - 2) * r, r)]
        return pltpu.make_async_remote_copy(src, dst, send.at[4 * a + k], recv.at[4 * a + k],
                                            device_id=(*chips[k % 2], c), device_id_type=MESH)

    def landed(a, k):
        dst = region(a, k, c)
        return pltpu.make_async_remote_copy(dst, dst, send.at[4 * a + k], recv.at[4 * a + k],
                                            device_id=(*chips[k % 2], c), device_id_type=MESH)

    def passed(a, k, who):
        part = region(a, k, who)
        return pltpu.make_async_remote_copy(part, part, fsend.at[4 * a + k], frecv.at[4 * a + k], device_id=sib,
                                            device_id_type=MESH)

    def direct(a, j):
        k = 4 * nb + 3 * (a - nb) + j
        return pltpu.make_async_remote_copy(ins[a], outs[a].at[me], send.at[k], recv.at[k],
                                            device_id=(*chips[j], c), device_id_type=MESH)

    def direct_landed(a, j):
        k = 4 * nb + 3 * (a - nb) + j
        dst = outs[a].at[cidx[j]]
        return pltpu.make_async_remote_copy(dst, dst, send.at[k], recv.at[k], device_id=(*chips[j], c),
                                            device_id_type=MESH)

    sends = [hop(a, k) for a in range(nb) for k in range(2)] + [direct(a, j) for a in range(nb, n_in) for j in range(3)]
    for cp in sends:
        cp.start()
    for cp in local:
        cp.start()
    for a in range(nb):
        for k in (1, 0):
            landed(a, k).wait_recv()
            for cp in (hop(a, 3 - k), passed(a, k, c)):
                cp.start()
                sends.append(cp)
    for a in range(nb):
        for k in (2, 3):
            landed(a, k).wait_recv()
            cp = passed(a, k, c)
            cp.start()
            sends.append(cp)
    for a in range(nb, n_in):
        for j in range(3):
            direct_landed(a, j).wait_recv()
    for a in range(nb):
        for k in range(4):
            passed(a, k, 1 - c).wait_recv()
    for cp in sends:
        cp.wait_send()
    for cp in local:
        cp.wait()


def _gather_sems(n_in, nb):
    dma = pltpu.SemaphoreType.DMA
    n_ici = 4 * nb + 3 * (n_in - nb)
    return [dma((n_ici,)), dma((n_ici,)), dma((4 * nb,)), dma((4 * nb,)), dma((n_in,))]


def _gather_async(name, halved, whole=()):
    nb, arrs = len(halved), list(halved) + list(whole)
    hbm = pltpu.MemorySpace.HBM
    ins = [jax.new_ref(a, memory_space=hbm) for a in arrs]
    outs = [jax.empty_ref(_sds((NCHIP,) + a.shape, a.dtype), memory_space=hbm) for a in arrs]

    @pl.kernel(mesh=plsc.ScalarSubcoreMesh(axis_name="seq", num_cores=1), name=name,
               scratch_types=tuple(_gather_sems(len(arrs), nb)),
               compiler_params=pltpu.CompilerParams(collective_id=2))
    def launch(send, recv, fsend, frecv, lsem):
        _handshake(_chip_peers())
        _gather_copies(ins, outs, nb, send, recv, fsend, frecv, lsem)

    launch()
    return outs


def _swap_halves(name, grads, after=None):
    n = len(grads)
    hbm = pltpu.MemorySpace.HBM
    ins = [jax.new_ref(g, memory_space=hbm) for g in grads]
    outs = [jax.empty_ref(_sds((NCHIP, g.shape[1] // 2) + g.shape[2:], g.dtype), memory_space=hbm) for g in grads]
    tile = (2 * 8, LANES)
    token = None if after is None else jax.empty_ref(_sds(tile, BF16), memory_space=hbm)

    @pl.kernel(mesh=plsc.ScalarSubcoreMesh(axis_name="seq", num_cores=1), name=name,
               scratch_types=(pltpu.SemaphoreType.DMA((n + 1,)), pltpu.SemaphoreType.DMA((n,))),
               compiler_params=pltpu.CompilerParams(collective_id=1))
    def launch(send, recv):
        x, y, c, _, _, _ = _place()
        sib = (x, y, 1 - c)
        _handshake([sib])
        if after is not None:
            tick = pltpu.make_async_copy(after.at[0, 0, 0, pl.ds(0, tile[0]), pl.ds(0, tile[1])], token, send.at[n])
            tick.start()
            tick.wait()
        cps = []
        for a in range(n):
            h = grads[a].shape[1] // 2
            cps.append(pltpu.make_async_remote_copy(ins[a].at[:, pl.ds((1 - c) * h, h)], outs[a], send.at[a],
                                                    recv.at[a], device_id=sib, device_id_type=MESH))
        for cp in cps:
            cp.start()
        for cp in cps:
            cp.wait()

    launch()
    return outs


def _row_tile(r, cap=256):
    return max(t for t in range(8, cap + 1, 8) if r % t == 0)


def _add_half(name, g, r, c_arr):
    _, l, rows, cols = g.shape
    h = l // 2
    tr = _row_tile(rows, 1056)

    def body(c_ref, g_ref, r_ref, o_ref):
        o_ref[...] = (g_ref[...].astype(F32) + r_ref[...].astype(F32)).astype(BF16)

    blk = (None, None, tr, cols)
    return pl.pallas_call(
        body,
        grid_spec=pltpu.PrefetchScalarGridSpec(
            num_scalar_prefetch=1, grid=(NCHIP, h, rows // tr),
            in_specs=[pl.BlockSpec(blk, lambda j, i, t, c_ref: (j, c_ref[0] * h + i, t, 0)),
                      pl.BlockSpec(blk, lambda j, i, t, c_ref: (j, i, t, 0))],
            out_specs=pl.BlockSpec(blk, lambda j, i, t, c_ref: (j, i, t, 0))),
        out_shape=_sds((NCHIP, h, rows, cols), BF16), name=name,
        compiler_params=_cp("parallel", "parallel", "parallel"),
    )(c_arr, g, r)


def _scatter_async(name, parts, sums, where):
    nb = len(parts)
    ins = [jax.new_ref(p, memory_space=pltpu.MemorySpace.HBM) for p in parts]
    dma = pltpu.SemaphoreType.DMA

    @pl.kernel(mesh=plsc.ScalarSubcoreMesh(axis_name="seq", num_cores=1), name=name,
               scratch_types=(dma((3 * nb,)), dma((3 * nb,)), dma((4 * nb,)), dma((4 * nb,)), dma((nb,))),
               compiler_params=pltpu.CompilerParams(collective_id=3))
    def launch(send, recv, fsend, frecv, lsem):
        _handshake(_chip_peers())
        x, y, c, me, chips, cidx = _place(staggered=True)
        sib = (x, y, 1 - c)

        def slot(a, half, chip):
            return sums[a].at[half, chip, pl.ds(where[a], 1)]

        local = [pltpu.make_async_copy(ins[a].at[me], slot(a, c, me), lsem.at[a]) for a in range(nb)]
        for cp in local:
            cp.start()

        def ici(a, j):
            return pltpu.make_async_remote_copy(ins[a].at[cidx[j]], slot(a, c, me), send.at[a * 3 + j],
                                                recv.at[a * 3 + j], device_id=(*chips[j], c), device_id_type=MESH)

        def landed(a, j):
            dst = slot(a, c, cidx[j])
            return pltpu.make_async_remote_copy(dst, dst, send.at[a * 3 + j], recv.at[a * 3 + j],
                                                device_id=(*chips[j], c), device_id_type=MESH)

        def passed(a, j, who):
            dst = slot(a, who, me if j == 3 else cidx[j])
            src = ins[a].at[me] if j == 3 else dst
            return pltpu.make_async_remote_copy(src, dst, fsend.at[a * 4 + j], frecv.at[a * 4 + j], device_id=sib,
                                                device_id_type=MESH)

        sends = [ici(a, j) for a in range(nb) for j in range(3)] + [passed(a, 3, c) for a in range(nb)]
        for cp in sends:
            cp.start()
        for a in range(nb):
            for j in range(3):
                landed(a, j).wait_recv()
                cp = passed(a, j, c)
                cp.start()
                sends.append(cp)
        for a in range(nb):
            for j in range(4):
                passed(a, j, 1 - c).wait_recv()
        for cp in sends:
            cp.wait_send()
        for cp in local:
            cp.wait()

    launch()


def _exchange_small(small, rep):
    def body(small_in, rep_in, small_out, rep_out, lsem, ssend, srecv):
        x, y, c, me, _, _ = _place()
        dev = 4 * x + 2 * y + c
        local = [pltpu.make_async_copy(small_in.at[me], small_out.at[dev], lsem.at[0]),
                 pltpu.make_async_copy(rep_in, rep_out.at[dev], lsem.at[1])]
        for cp in local:
            cp.start()

        def peer(r):
            return (1 - x if r & 4 else x), (1 - y if r & 2 else y), (1 - c if r & 1 else c)

        def tiny(r, which):
            px, py, pc = peer(r)
            k = (r - 1) * 2 + which
            if which == 0:
                return pltpu.make_async_remote_copy(small_in.at[2 * px + py], small_out.at[dev], ssend.at[k],
                                                    srecv.at[k], device_id=(px, py, pc), device_id_type=MESH)
            return pltpu.make_async_remote_copy(rep_in, rep_out.at[dev], ssend.at[k], srecv.at[k],
                                                device_id=(px, py, pc), device_id_type=MESH)

        def tiny_landed(r, which):
            px, py, pc = peer(r)
            k = (r - 1) * 2 + which
            dst = (small_out if which == 0 else rep_out).at[4 * px + 2 * py + pc]
            return pltpu.make_async_remote_copy(dst, dst, ssend.at[k], srecv.at[k], device_id=(px, py, pc),
                                                device_id_type=MESH)

        sends = [tiny(r, w) for r in range(1, NDEV) for w in range(2)]
        for cp in sends:
            cp.start()
        for r in range(1, NDEV):
            for w in range(2):
                tiny_landed(r, w).wait_recv()
        for cp in sends:
            cp.wait_send()
        for cp in local:
            cp.wait()

    dma = pltpu.SemaphoreType.DMA
    return pl.pallas_call(
        body, in_specs=[ANY] * 2, out_specs=[ANY] * 2,
        out_shape=[_sds((NDEV,) + small.shape[1:], F32), _sds((NDEV,) + rep.shape, F32)],
        scratch_shapes=[dma((2,)), dma((2 * (NDEV - 1),)), dma((2 * (NDEV - 1),))], name="exchange_small_grads",
    )(small, rep)


def _adamw_math(w, g, m, v):
    m = B1 * m + (1.0 - B1) * g
    v = B2 * v + (1.0 - B2) * (g * g)
    m_hat = m / (1.0 - B1 ** STEP)
    v_hat = v / (1.0 - B2 ** STEP)
    return -LR * (m_hat / (jnp.sqrt(v_hat) + AEPS) + WD * w), m, v


def _adamw_big(name, w, m, v, parts, row0=0, first=0, outs=None):
    _, _, rows, cols = w.shape
    n = parts.shape[2]
    tr = _row_tile(rows, 352)
    t0 = row0 // tr

    def body(w_ref, m_ref, v_ref, p_ref, *rest):
        g_ref, d_ref, nm_ref, nv_ref = rest[-4:]
        g = p_ref[0].astype(F32)
        for q in range(1, NCHIP):
            g = g + p_ref[q].astype(F32)
        d, nm, nv = _adamw_math(w_ref[...], g, m_ref[...], v_ref[...])
        g_ref[...], d_ref[...], nm_ref[...], nv_ref[...] = g, d, nm, nv

    spec = pl.BlockSpec((None, None, tr, cols), lambda i, p, t: (first + i, p, t, 0))
    na = 0 if outs is None else 4
    return pl.pallas_call(
        body, grid=(n, 2, rows // tr),
        in_specs=[spec, spec, spec,
                  pl.BlockSpec((None, NCHIP, None, tr, cols), lambda i, p, t: (p, 0, i, t0 + t, 0))] + [ANY] * na,
        out_specs=[spec] * 4, out_shape=[_sds(w.shape)] * 4, input_output_aliases={4 + k: k for k in range(na)},
        name=name, compiler_params=_cp("parallel", "parallel", "parallel"),
    )(w, m, v, parts, *(outs or ()))


def _adamw_small(name, w, m, v, parts):
    def body(w_ref, m_ref, v_ref, p_ref, g_ref, d_ref, nm_ref, nv_ref):
        g = p_ref[0]
        for q in range(1, NDEV):
            g = g + p_ref[q]
        d, nm, nv = _adamw_math(w_ref[...], g, m_ref[...], v_ref[...])
        g_ref[...], d_ref[...], nm_ref[...], nv_ref[...] = g, d, nm, nv

    return pl.pallas_call(body, out_shape=[_sds(w.shape)] * 4, name=name)(w, m, v, parts)


def _pack(arrs, rows):
    flat = jnp.concatenate([a.reshape(-1) for a in arrs])
    return jnp.pad(flat, (0, rows * LANES - flat.shape[0])).reshape(rows, LANES)


def _unpack(packed, shapes):
    flat, out, o = packed.reshape(-1), [], 0
    for s in shapes:
        n = 1
        for d in s:
            n *= d
        out.append(flat[o:o + n].reshape(s))
        o += n
    return out


SMALL_ROWS, REP_ROWS = 200, 16


def kernel(x, norm_w, ffn_w_gate, ffn_w_up, ffn_w_down, mix_w_in, dn_conv_w, attn_sinks, dn_a_log, dn_dt_bias, dn_norm_w, mix_w_out, conv_w_pw1, conv_b_pw1, conv_w_dw, conv_b_dw, conv_ln_w, conv_ln_b, conv_w_pw2, conv_b_pw2, final_norm_w, loss_target, m_norm_w, m_ffn_w_gate, m_ffn_w_up, m_ffn_w_down, m_mix_w_in, m_dn_conv_w, m_attn_sinks, m_dn_a_log, m_dn_dt_bias, m_dn_norm_w, m_mix_w_out, m_conv_w_pw1, m_conv_b_pw1, m_conv_w_dw, m_conv_b_dw, m_conv_ln_w, m_conv_ln_b, m_conv_w_pw2, m_conv_b_pw2, m_final_norm_w, v_norm_w, v_ffn_w_gate, v_ffn_w_up, v_ffn_w_down, v_mix_w_in, v_dn_conv_w, v_attn_sinks, v_dn_a_log, v_dn_dt_bias, v_dn_norm_w, v_mix_w_out, v_conv_w_pw1, v_conv_b_pw1, v_conv_w_dw, v_conv_b_dw, v_conv_ln_w, v_conv_ln_b, v_conv_w_pw2, v_conv_b_pw2, v_final_norm_w):
    small_names = ["norm_w", "dn_conv_w", "conv_b_pw1", "conv_w_dw", "conv_b_dw", "conv_ln_w", "conv_ln_b",
                   "conv_b_pw2"]
    rep_names = ["attn_sinks", "dn_a_log", "dn_dt_bias", "dn_norm_w", "final_norm_w"]
    w = dict(norm_w=norm_w, ffn_w_gate=ffn_w_gate, ffn_w_up=ffn_w_up, ffn_w_down=ffn_w_down, mix_w_in=mix_w_in, dn_conv_w=dn_conv_w, attn_sinks=attn_sinks, dn_a_log=dn_a_log, dn_dt_bias=dn_dt_bias, dn_norm_w=dn_norm_w, mix_w_out=mix_w_out, conv_w_pw1=conv_w_pw1, conv_b_pw1=conv_b_pw1, conv_w_dw=conv_w_dw, conv_b_dw=conv_b_dw, conv_ln_w=conv_ln_w, conv_ln_b=conv_ln_b, conv_w_pw2=conv_w_pw2, conv_b_pw2=conv_b_pw2, final_norm_w=final_norm_w)
    m = dict(norm_w=m_norm_w, ffn_w_gate=m_ffn_w_gate, ffn_w_up=m_ffn_w_up, ffn_w_down=m_ffn_w_down, mix_w_in=m_mix_w_in, dn_conv_w=m_dn_conv_w, attn_sinks=m_attn_sinks, dn_a_log=m_dn_a_log, dn_dt_bias=m_dn_dt_bias, dn_norm_w=m_dn_norm_w, mix_w_out=m_mix_w_out, conv_w_pw1=m_conv_w_pw1, conv_b_pw1=m_conv_b_pw1, conv_w_dw=m_conv_w_dw, conv_b_dw=m_conv_b_dw, conv_ln_w=m_conv_ln_w, conv_ln_b=m_conv_ln_b, conv_w_pw2=m_conv_w_pw2, conv_b_pw2=m_conv_b_pw2, final_norm_w=m_final_norm_w)
    v = dict(norm_w=v_norm_w, ffn_w_gate=v_ffn_w_gate, ffn_w_up=v_ffn_w_up, ffn_w_down=v_ffn_w_down, mix_w_in=v_mix_w_in, dn_conv_w=v_dn_conv_w, attn_sinks=v_attn_sinks, dn_a_log=v_dn_a_log, dn_dt_bias=v_dn_dt_bias, dn_norm_w=v_dn_norm_w, mix_w_out=v_mix_w_out, conv_w_pw1=v_conv_w_pw1, conv_b_pw1=v_conv_b_pw1, conv_w_dw=v_conv_w_dw, conv_b_dw=v_conv_b_dw, conv_ln_w=v_conv_ln_w, conv_ln_b=v_conv_ln_b, conv_w_pw2=v_conv_w_pw2, conv_b_pw2=v_conv_b_pw2, final_norm_w=v_final_norm_w)
    order = ["norm_w", "ffn_w_gate", "ffn_w_up", "ffn_w_down", "mix_w_in", "dn_conv_w", "attn_sinks", "dn_a_log",
             "dn_dt_bias", "dn_norm_w", "mix_w_out", "conv_w_pw1", "conv_b_pw1", "conv_w_dw", "conv_b_dw",
             "conv_ln_w", "conv_ln_b", "conv_w_pw2", "conv_b_pw2", "final_norm_w"]

    small_shapes = [w[n].shape for n in small_names]
    rep_shapes = [w[n].shape for n in rep_names]

    def halves(a):
        return a.reshape(a.shape[:-2] + (2, a.shape[-2] // 2, a.shape[-1]))

    tr = lambda a: jnp.swapaxes(a, -1, -2)
    gate_t, up_t = tr(ffn_w_gate), tr(ffn_w_up)

    def layer_shards(l):
        mix_in, mix_out = (mix_w_in, mix_w_out) if l % 2 == 0 else (conv_w_pw1, conv_w_pw2)
        ffn = jnp.concatenate([gate_t[l], up_t[l], ffn_w_down[l]], axis=1)
        return ([t.astype(BF16) for t in (halves(ffn[0]), halves(mix_in[l // 2]), halves(mix_out[l // 2]))],
                [halves(ffn[1]).astype(BF16)])

    first = layer_shards(0)
    first = (first[0] + [_pack([w[n] for n in small_names], SMALL_ROWS)], first[1])
    first, (gate_t, up_t, ffn_w_down, mix_w_in, mix_w_out, conv_w_pw1, conv_w_pw2) = lax.optimization_barrier(
        (first, (gate_t, up_t, ffn_w_down, mix_w_in, mix_w_out, conv_w_pw1, conv_w_pw2)))
    gathering = [(_gather_async("gather_layer0a", first[0][:3], first[0][3:]),
                  _gather_async("gather_layer0b", first[1]))]
    for l in range(1, DEPTH):
        before, after = layer_shards(l)
        gathering.append((_gather_async(f"gather_layer{l}a", before), _gather_async(f"gather_layer{l}b", after)))
    ffn_block = lambda g: g.reshape(NCHIP, 1, 3 * FS, D)

    def mixer_params(l, w_a, w_b):
        e = l // 2
        w_a = w_a.reshape(NCHIP, D, -1)
        w_b = w_b.reshape(D, D)
        if l % 2 == 0:
            return dict(w_in=w_a, dn_conv_w=sm["dn_conv_w"][e], sinks=_row(attn_sinks[e]), a_log=_row(dn_a_log[e]),
                        dt_bias=_row(dn_dt_bias[e]), dn_norm_w=_row(dn_norm_w[e]), wo_a=w_b[:Q_A], wo_b=w_b[Q_A:])
        return dict(b1a=_row(sm["conv_b_pw1"][e, :D]), b1b=_row(sm["conv_b_pw1"][e, D:]), w1=w_a,
                    w_dw=sm["conv_w_dw"][e], b_dw=_row(sm["conv_b_dw"][e]), ln_w=_row(sm["conv_ln_w"][e]),
                    ln_b=_row(sm["conv_ln_b"][e]), b2=_row(sm["conv_b_pw2"][e]), w2=w_b)

    xs, saved = x[0], []
    for l in range(DEPTH):
        got = [r[...] for r in gathering[l][0]]
        if l == 0:
            per_chip = [_unpack(got[3][q], small_shapes) for q in range(NCHIP)]
            sm = {n: jnp.concatenate([per_chip[q][i] for q in range(NCHIP)], axis=-1)
                  for i, n in enumerate(small_names)}
        else:
            xs, got = lax.optimization_barrier((xs, got))

        def second_ffn(x2, l=l):
            x2, got_b = lax.optimization_barrier((x2, gathering[l][1][0][...]))
            return x2, ffn_block(got_b)

        xs, sv = _layer_fwd(l, xs, sm["norm_w"][l], ffn_block(got[0]), second_ffn, mixer_params(l, got[1], got[2]))
        saved.append(sv)
    loss, dx, dfw = _final("final", xs, _row(final_norm_w), loss_target[0])

    hbm = pltpu.MemorySpace.HBM
    row_shapes = dict(ffn=(3 * FS, D), w_in=(D // 2, IN_COLS // NCHIP), w_out=(D // 8, D), pw1=(D // 2, D // 2),
                      pw2=(D // 8, D))
    new_sums = lambda k, n: jax.empty_ref(_sds((2, NCHIP, n) + row_shapes[k], BF16), memory_space=hbm)
    sums_0 = {k: new_sums(k, 1) for k in ("ffn", "w_in", "w_out")}
    sums = dict(ffn=new_sums("ffn", DEPTH - 1), w_in=new_sums("w_in", 1), w_out=new_sums("w_out", 1),
                pw1=new_sums("pw1", 2), pw2=new_sums("pw2", 2))
    c_arr = lax.axis_index("c").astype(jnp.int32).reshape(1)
    dnorm, gmix = [None] * DEPTH, [None] * DEPTH

    def hand_on(l, grads, swapped):
        def run(dx):
            dx, other = lax.optimization_barrier((dx, [r[...] for r in swapped]))
            parts = [_add_half(f"add_half_{l}_{k}", gg, rr, c_arr) for k, (gg, rr) in enumerate(zip(grads, other))]
            dx, parts = lax.optimization_barrier((dx, parts))
            keys = ("ffn", "w_in", "w_out") if l % 2 == 0 else ("ffn", "pw1", "pw2")
            if l == 0:
                _scatter_async("scatter_grads_0", parts, [sums_0[k] for k in keys], [0, 0, 0])
            else:
                _scatter_async(f"scatter_grads_{l}", parts, [sums[k] for k in keys],
                               [l - 1, 0, 0] if l % 2 == 0 else [l - 1, l // 2, l // 2])
            return dx
        return run

    pending = lambda dx: dx
    for l in reversed(range(DEPTH)):
        dx, dnorm[l], dffn, gmix[l] = _layer_bwd(l, dx, sm["norm_w"][l], saved[l], pending)
        if l % 2 == 0:
            g_a, g_b = gmix[l]["w_in"], jnp.concatenate([gmix[l]["wo_a"], gmix[l]["wo_b"]], axis=0)
        else:
            g_a, g_b = gmix[l]["w1"], gmix[l]["w2"]
        g_a = halves(g_a).astype(BF16)
        g_b = g_b.reshape(NCHIP, 2, D // 8, D).astype(BF16)
        dx, grads = lax.optimization_barrier((dx, [dffn, g_a, g_b]))
        pending = hand_on(l, grads, _swap_halves(f"swap_grads_{l}", grads, sums["ffn"] if l < DEPTH - 1 else None))
    gm, gc = [gmix[0], gmix[2]], [gmix[1], gmix[3]]
    small_g = dict(
        norm_w=jnp.stack(dnorm), dn_conv_w=jnp.stack([gm[e]["dn_conv_w"] for e in range(2)]),
        conv_b_pw1=jnp.stack([jnp.concatenate([gc[e]["b1a"], gc[e]["b1b"]], axis=1)[0] for e in range(2)]),
        conv_w_dw=jnp.stack([gc[e]["w_dw"] for e in range(2)]),
        conv_b_dw=jnp.stack([gc[e]["b_dw"][0] for e in range(2)]),
        conv_ln_w=jnp.stack([gc[e]["ln_w"][0] for e in range(2)]),
        conv_ln_b=jnp.stack([gc[e]["ln_b"][0] for e in range(2)]),
        conv_b_pw2=jnp.stack([gc[e]["b2"][0] for e in range(2)]))
    small_by_chip = jnp.stack([_pack([jnp.split(small_g[n], NCHIP, axis=-1)[q] for n in small_names], SMALL_ROWS)
                               for q in range(NCHIP)])
    rep_g = _pack([jnp.stack([gm[e]["sinks"][0] for e in range(2)]), jnp.stack([gm[e]["a_log"][0] for e in range(2)]),
                   jnp.stack([gm[e]["dt_bias"][0] for e in range(2)]),
                   jnp.stack([gm[e]["dn_norm_w"][0] for e in range(2)]), dfw[0]], REP_ROWS)
    small_sum, rep_sum = _exchange_small(small_by_chip, rep_g)
    dx, small_sum, rep_sum = lax.optimization_barrier((dx, small_sum, rep_sum))
    dx = pending(dx)

    big = (("ffn_w_gate", "ffn", 0), ("ffn_w_up", "ffn", FS), ("ffn_w_down", "ffn", 2 * FS), ("mix_w_in", "w_in", 0),
           ("mix_w_out", "w_out", 0), ("conv_w_pw1", "pw1", 0), ("conv_w_pw2", "pw2", 0))
    views = {n: (tr, tr) if n in ("ffn_w_gate", "ffn_w_up") else (
        (lambda a: a) if w[n].ndim == 4 else halves, lambda o, n=n: o.reshape(w[n].shape)) for n, _, _ in big}
    partial_sums = {k: r[...] for k, r in sums.items()}
    upper = {}
    for n, key, row0 in big:
        view = views[n][0]
        upper[n] = _adamw_big(f"adamw_{n}", view(w[n]), view(m[n]), view(v[n]), partial_sums[key], row0,
                              first=0 if key in ("pw1", "pw2") else 1)
    upper, partial_sums_0 = lax.optimization_barrier((upper, {k: r[...] for k, r in sums_0.items()}))
    res = {}
    for n, key, row0 in big:
        view, back = views[n]
        outs = upper[n] if key not in partial_sums_0 else _adamw_big(
            f"adamw_{n}_0", view(w[n]), view(m[n]), view(v[n]), partial_sums_0[key], row0, first=0, outs=upper[n])
        res[n] = [back(o) for o in outs]
    outs = _adamw_small("adamw_small", *[_pack([d[n] for n in small_names], SMALL_ROWS) for d in (w, m, v)],
                        small_sum)
    for i, n in enumerate(small_names):
        res[n] = [_unpack(o, small_shapes)[i] for o in outs]
    outs = _adamw_small("adamw_replicated", *[_pack([d[n] for n in rep_names], REP_ROWS) for d in (w, m, v)],
                        rep_sum)
    for i, n in enumerate(rep_names):
        res[n] = [_unpack(o, rep_shapes)[i] for o in outs]

    total = lax.psum(loss[0, 0], ("x", "y", "c"))
    return (total, dx[None], *[res[n][0] for n in order], *[res[n][1] for n in order],
            *[res[n][2] for n in order], *[res[n][3] for n in order])
```

```python
import jax
import jax.numpy as jnp
from jax import lax
from jax.experimental import pallas as pl
from jax.experimental.pallas import tpu as pltpu
from jax.experimental.pallas import tpu_sc as plsc

F32, BF16 = jnp.float32, jnp.bfloat16
MESH = pl.DeviceIdType.MESH
ANY = pl.BlockSpec(memory_space=pl.ANY)

T, D, F = 2048, 1024, 2816
DEPTH = 4
EPS = 1e-6
HEADS, HDIM, KV_HEADS, GROUP = 8, 64, 2, 4
WINDOW = BLOCK = 128
CHUNK = 64
NCHUNK = T // CHUNK
Q_A, KV_A, QKV_B, V_B = 512, 128, 1536, 512
IN_COLS = 2832
IN_SPLITS = (0, 512, 640, 768, 2304, 2816, 2832)
NCHIP, NDEV = 4, 8
FS = F // NCHIP
LR, B1, B2, AEPS, WD, STEP = 0.001, 0.9, 0.999, 1e-08, 0.01, 10
V7X_VMEM_BYTES = 64 * 1024 * 1024
VMEM_LIMIT = V7X_VMEM_BYTES * 7 // 8
LANES = 128


def _cp(*sem):
    return pltpu.CompilerParams(dimension_semantics=sem, vmem_limit_bytes=VMEM_LIMIT)


def _sds(shape, dtype=F32):
    return jax.ShapeDtypeStruct(tuple(shape), dtype)


def _full(shape):
    nd = len(shape)
    return pl.BlockSpec(tuple(shape), lambda *_: (0,) * nd)


def _split_bf16(a):
    hi = a.astype(BF16)
    return hi, (a - hi.astype(F32)).astype(BF16)


def _dg(a, b, ca, cb, hi=False):
    if a.ndim == 3 and b.ndim == 3:
        dims = (((ca + 1,), (cb + 1,)), ((0,), (0,)))
    else:
        dims = (((ca,), (cb,)), ((), ()))
    dot = lambda p, q: lax.dot_general(p, q, dims, preferred_element_type=F32)
    if hi:
        a_hi, a_lo = _split_bf16(a.astype(F32))
        b_hi, b_lo = _split_bf16(b.astype(F32))
        return dot(a_hi, b_hi) + (dot(a_hi, b_lo) + dot(a_lo, b_hi))
    return dot(a.astype(BF16), b.astype(BF16))


def _make_mm(hi):
    @jax.custom_vjp
    def nn(a, b):
        return _dg(a, b, 1, 0, hi)

    @jax.custom_vjp
    def nt(a, b):
        return _dg(a, b, 1, 1, hi)

    @jax.custom_vjp
    def tn(a, b):
        return _dg(a, b, 0, 0, hi)

    nn.defvjp(lambda a, b: (_dg(a, b, 1, 0, hi), (a, b)),
              lambda r, g: (_dg(g, r[1], 1, 1, hi).astype(r[0].dtype), _dg(r[0], g, 0, 0, hi).astype(r[1].dtype)))
    nt.defvjp(lambda a, b: (_dg(a, b, 1, 1, hi), (a, b)),
              lambda r, g: (_dg(g, r[1], 1, 0, hi).astype(r[0].dtype), _dg(g, r[0], 0, 0, hi).astype(r[1].dtype)))
    tn.defvjp(lambda a, b: (_dg(a, b, 0, 0, hi), (a, b)),
              lambda r, g: (_dg(r[1], g, 1, 1, hi).astype(r[0].dtype), _dg(r[0], g, 1, 0, hi).astype(r[1].dtype)))
    return nn, nt, tn


_nn, _nt, _tn = _make_mm(False)
_nn_hi = _make_mm(True)[0]


def _rms(x, w):
    return x * lax.rsqrt(jnp.mean(x * x, axis=-1, keepdims=True) + EPS) * w


def _layernorm(x, w, b):
    xc = x - jnp.mean(x, axis=-1, keepdims=True)
    return xc * lax.rsqrt(jnp.mean(xc * xc, axis=-1, keepdims=True) + EPS) * w + b


def _silu(x):
    return x * jax.nn.sigmoid(x)


def _iota2(shape, dim):
    return lax.broadcasted_iota(jnp.int32, shape, dim)


def _flat_weights(lhs_idx, weights):
    specs, ops, lhs_of, where = [], [], [], []
    for a, (k, w) in enumerate(zip(lhs_idx, weights)):
        for q in range(1 if w.ndim == 2 else w.shape[0]):
            specs.append(_full(w.shape) if w.ndim == 2
                         else pl.BlockSpec((None,) + w.shape[1:], lambda i, q=q: (q, 0, 0)))
            ops.append(w)
            lhs_of.append(k)
            where.append((a, None if w.ndim == 2 else q))
    return specs, ops, lhs_of, where


def _blk_fwd(name, pre, lhs_idx, post, toks, smalls, weights, outs, tm=512):
    wspecs, wops, lhs_of, _ = _flat_weights(lhs_idx, weights)
    nt_, ns, nw = len(toks), len(smalls), len(wops)

    def body(*refs):
        tv = [r[...] for r in refs[:nt_]]
        sv = [r[...] for r in refs[nt_:nt_ + ns]]
        wr = refs[nt_ + ns:nt_ + ns + nw]
        orf = refs[nt_ + ns + nw:]
        lhs = pre(tv, sv)
        ys = [_dg(lhs[i], w[...], 1, 0) for i, w in zip(lhs_of, wr)]
        for o_ref, o in zip(orf, post(ys, tv, sv)):
            o_ref[...] = o.astype(o_ref.dtype)

    in_specs = ([pl.BlockSpec((tm, a.shape[1]), lambda i: (i, 0)) for a in toks]
                + [_full(a.shape) for a in smalls] + wspecs)
    out_specs = [pl.BlockSpec((tm, w_), lambda i: (i, 0)) for w_, _ in outs]
    return pl.pallas_call(
        body, grid=(T // tm,), in_specs=in_specs, out_specs=out_specs,
        out_shape=[_sds((T, w_), dt) for w_, dt in outs], name=name, compiler_params=_cp("parallel"),
    )(*toks, *smalls, *wops)


def _blk_bwd(name, pre, lhs_idx, post, toks, smalls, weights, ct_groups, res=None, linear_post=False, tm=256,
             wchunk=512):
    wspecs, wops, lhs_of, where = _flat_weights(lhs_idx, weights)
    nt_, ns, nw, na = len(toks), len(smalls), len(wops), len(weights)
    cts = [a for g in ct_groups for a in g]
    nc = len(cts)
    widths = [sum(a.shape[1] for a in g) for g in ct_groups]
    has_res = res is not None

    def body(*refs):
        p = 0
        tr = refs[p:p + nt_]; p += nt_
        sr = refs[p:p + ns]; p += ns
        wr = refs[p:p + nw]; p += nw
        cr = refs[p:p + nc]; p += nc
        rr = refs[p:p + has_res]; p += has_res
        dtr = refs[p:p + nt_]; p += nt_
        dsr = refs[p:p + ns]; p += ns
        dwr = refs[p:p + na]; p += na
        scr = refs[p:]
        i = pl.program_id(0)

        @pl.when(i == 0)
        def _():
            for r in list(dsr) + list(dwr):
                r[...] = jnp.zeros_like(r)

        tv = [r[...] for r in tr]
        sv = [r[...] for r in sr]
        ctv, q, si = [], 0, 0
        for g in ct_groups:
            if len(g) == 1:
                ctv.append(cr[q][...].astype(F32))
            else:
                off = 0
                for j, a in enumerate(g):
                    scr[si][:, off:off + a.shape[1]] = cr[q + j][...].astype(F32)
                    off += a.shape[1]
                ctv.append(scr[si][...])
                si += 1
            q += len(g)

        lhs, vjp_pre = jax.vjp(lambda *a: tuple(pre(list(a[:nt_]), list(a[nt_:]))), *tv, *sv)
        lhs_b = [l.astype(BF16) for l in lhs]
        ys = [jnp.zeros((tm, w.shape[1]), F32) if linear_post else _dg(lhs_b[k], w[...], 1, 0)
              for k, w in zip(lhs_of, wr)]
        _, vjp_post = jax.vjp(lambda *a: tuple(post(list(a[:nw]), list(a[nw:nw + nt_]), list(a[nw + nt_:]))),
                              *ys, *tv, *sv)
        gp = vjp_post(tuple(ctv))
        dys, dt_post, ds_post = gp[:nw], gp[nw:nw + nt_], gp[nw + nt_:]
        dlhs = [None] * len(lhs)
        for k, w, dy, (a, q) in zip(lhs_of, wr, dys, where):
            dyb = dy.astype(BF16)
            n = w.shape[1]
            for c0 in range(0, n, wchunk):
                c1 = min(n, c0 + wchunk)
                part = _dg(lhs_b[k], dyb[:, c0:c1], 0, 0)
                if q is None:
                    dwr[a][:, c0:c1] += part
                else:
                    dwr[a][q, :, c0:c1] += part
            d = _dg(dyb, w[...], 1, 1)
            dlhs[k] = d if dlhs[k] is None else dlhs[k] + d
        gq = vjp_pre(tuple(d.astype(l.dtype) for d, l in zip(dlhs, lhs)))
        dt_pre, ds_pre = gq[:nt_], gq[nt_:]
        for j in range(nt_):
            d = dt_post[j] + dt_pre[j]
            if j == 0 and has_res:
                d = d + rr[0][...]
            dtr[j][...] = d
        for j in range(ns):
            dsr[j][...] += ds_post[j] + ds_pre[j]

    tok_spec = lambda a: pl.BlockSpec((tm, a.shape[1]), lambda i: (i, 0))
    in_specs = ([tok_spec(a) for a in toks] + [_full(a.shape) for a in smalls] + wspecs
                + [tok_spec(a) for a in cts] + ([tok_spec(res)] if has_res else []))
    out_specs = [tok_spec(a) for a in toks] + [_full(a.shape) for a in smalls] + [_full(w.shape) for w in weights]
    out_shape = ([_sds(a.shape) for a in toks] + [_sds(a.shape) for a in smalls] + [_sds(w.shape) for w in weights])
    scratch = [pltpu.VMEM((tm, wd), F32) for g, wd in zip(ct_groups, widths) if len(g) > 1]
    outs = pl.pallas_call(
        body, grid=(T // tm,), in_specs=in_specs, out_specs=out_specs, out_shape=out_shape,
        scratch_shapes=scratch, name=name, compiler_params=_cp("arbitrary"),
    )(*toks, *smalls, *wops, *cts, *([res] if has_res else []))
    return outs[:nt_], outs[nt_:nt_ + ns], outs[nt_ + ns:]


def _ffn_fwd(name, x, nw, ffn, idx, tm=1024):
    def body(x_ref, nw_ref, wg_ref, wu_ref, wd_ref, o_ref, g_ref, da_ref, db_ref, h_ref):
        s = pl.program_id(1)

        @pl.when(s == 0)
        def _():
            xv = x_ref[...]
            h_ref[...] = _rms(xv, nw_ref[...]).astype(BF16)
            o_ref[...] = xv

        h = h_ref[...]
        a = _dg(h, wg_ref[...], 1, 1)
        b = _dg(h, wu_ref[...], 1, 1)
        sa = jax.nn.sigmoid(a)
        act = a * sa
        gated = (act * b).astype(BF16)
        g_ref[...] = gated
        da_ref[...] = (b * (sa * (1.0 + a * (1.0 - sa)))).astype(BF16)
        db_ref[...] = act.astype(BF16)
        o_ref[...] += 0.5 * _dg(gated, wd_ref[...], 1, 0)

    wspec = lambda k: pl.BlockSpec((None, None, FS, D), lambda i, s: (s, idx, k, 0))
    act = pl.BlockSpec((None, tm, FS), lambda i, s: (s, i, 0))
    return pl.pallas_call(
        body, grid=(T // tm, NCHIP),
        in_specs=[pl.BlockSpec((tm, D), lambda i, s: (i, 0)), _full((1, D)), wspec(0), wspec(1), wspec(2)],
        out_specs=[pl.BlockSpec((tm, D), lambda i, s: (i, 0)), act, act, act,
                   pl.BlockSpec((tm, D), lambda i, s: (i, 0))],
        out_shape=[_sds((T, D))] + [_sds((NCHIP, T, FS), BF16)] * 3 + [_sds((T, D), BF16)],
        name=name, compiler_params=_cp("parallel", "arbitrary"),
    )(x, nw, ffn, ffn, ffn)


def _ffn_bwd(name, x, nw, ffn, idx, pre, dy, gbuf=None, tm=512):
    ni = T // tm

    def body(x_ref, dy_ref, nw_ref, wg_ref, wu_ref, wd_ref, g_ref, fa_ref, fb_ref, h_ref, dx_ref, dnw_ref, dffn_ref,
             dh_acc, ag, au, ad):
        s, i = pl.program_id(0), pl.program_id(1)
        rows = pl.ds(pl.multiple_of(i * tm, tm), tm)

        @pl.when((s == 0) & (i == 0))
        def _():
            dnw_ref[...] = jnp.zeros_like(dnw_ref)

        @pl.when(i == 0)
        def _():
            ag[...] = jnp.zeros_like(ag)
            au[...] = jnp.zeros_like(au)
            ad[...] = jnp.zeros_like(ad)

        hb = h_ref[...]
        dyb = (0.5 * dy_ref[...]).astype(BF16)
        ad[...] += _dg(g_ref[...], dyb, 0, 0)
        dact = _dg(dyb, wd_ref[...], 1, 1)
        da = (dact * fa_ref[...].astype(F32)).astype(BF16)
        db = (dact * fb_ref[...].astype(F32)).astype(BF16)
        ag[...] += _dg(da, hb, 0, 0)
        au[...] += _dg(db, hb, 0, 0)
        dh = _dg(da, wg_ref[...], 1, 0) + _dg(db, wu_ref[...], 1, 0)

        @pl.when(s == 0)
        def _():
            dh_acc[rows, :] = dh

        @pl.when((s > 0) & (s < NCHIP - 1))
        def _():
            dh_acc[rows, :] += dh

        @pl.when(s == NCHIP - 1)
        def _():
            _, vjp_rms = jax.vjp(_rms, x_ref[...], nw_ref[...])
            dx, dnw = vjp_rms(dh_acc[rows, :] + dh)
            dx_ref[...] = dy_ref[...] + dx
            dnw_ref[...] += dnw

        @pl.when(i == ni - 1)
        def _():
            dffn_ref[0:FS, :] = ag[...].astype(BF16)
            dffn_ref[FS:2 * FS, :] = au[...].astype(BF16)
            dffn_ref[2 * FS:, :] = ad[...].astype(BF16)

    wspec = lambda r, k, blk=0: pl.BlockSpec((None, None, r, D), lambda s, i: (s, blk, k, 0),
                                             pipeline_mode=pl.Buffered(1))
    last = lambda s, i: (jnp.where(s == NCHIP - 1, i, 0), 0)
    nb = 0 if gbuf is None else 1
    act = pl.BlockSpec((None, tm, FS), lambda s, i: (s, i, 0))
    tok = pl.BlockSpec((tm, D), lambda s, i: (i, 0))
    return pl.pallas_call(
        lambda *refs: body(*refs[:10], *refs[10 + nb:]), grid=(NCHIP, ni),
        in_specs=[pl.BlockSpec((tm, D), last), tok, _full((1, D)), wspec(FS, 0), wspec(FS, 1), wspec(FS, 2), act, act,
                  act, tok] + [ANY] * nb,
        out_specs=[pl.BlockSpec((tm, D), last), _full((1, D)), wspec(3 * FS, 0, idx)],
        out_shape=[_sds((T, D)), _sds((1, D)), _sds((NCHIP, 2, 3 * FS, D), BF16)],
        input_output_aliases={10 + k: 2 + k for k in range(nb)},
        scratch_shapes=[pltpu.VMEM((T, D), F32)] + [pltpu.VMEM((FS, D), F32)] * 3,
        name=name, compiler_params=_cp("arbitrary", "arbitrary"),
    )(x, dy, nw, ffn, ffn, ffn, *pre, *(() if gbuf is None else (gbuf,)))


CONV_ROWS = 256


def _conv_pad(k):
    return 8 * ((k - 1 + 7) // 8)


def _shifted(win, o):
    n = win.shape[0]
    return (win if o % n == 0 else pltpu.roll(win, (n - o) % n, 0))[0:CONV_ROWS, :]


def _conv_fwd(name, x, w, b, act):
    k_w, c = w.shape
    tc = 256 if c % 256 == 0 else LANES
    pad = _conv_pad(k_w)
    has_b = b is not None

    def body(*refs):
        x_ref, w_ref = refs[0], refs[1]
        b_ref = refs[2] if has_b else None
        y_ref, xp = refs[2 + has_b], refs[3 + has_b]
        xp[0:pad, :] = jnp.zeros((pad, tc), F32)
        xp[pad:, :] = x_ref[...]

        def step(t, carry):
            base = pl.multiple_of(t * CONV_ROWS, CONV_ROWS)
            win = xp[pl.ds(base, CONV_ROWS + pad), :]
            acc = jnp.zeros((CONV_ROWS, tc), F32)
            for k in range(k_w):
                o = pad - (k_w - 1) + k
                acc = acc + w_ref[k:k + 1, :] * _shifted(win, o)
            if has_b:
                acc = acc + b_ref[...]
            y_ref[pl.ds(base, CONV_ROWS), :] = _silu(acc) if act else acc
            return carry

        lax.fori_loop(0, T // CONV_ROWS, step, 0)

    col = lambda r: pl.BlockSpec((r, tc), lambda j: (0, j))
    ins = [x, w] + ([b] if has_b else [])
    return pl.pallas_call(
        body, grid=(c // tc,), in_specs=[col(T), col(k_w)] + ([col(1)] if has_b else []), out_specs=col(T),
        out_shape=_sds((T, c)), scratch_shapes=[pltpu.VMEM((T + pad, tc), F32)], name=name,
        compiler_params=_cp("parallel"),
    )(*ins)


def _conv_bwd(name, x, w, b, act, dy):
    k_w, c = w.shape
    tc = 256 if c % 256 == 0 else LANES
    pad = _conv_pad(k_w)
    has_b = b is not None

    def body(*refs):
        x_ref, w_ref, dy_ref = refs[0], refs[1], refs[2]
        b_ref = refs[3] if has_b else None
        dx_ref, dw_ref, db_ref, xp, dp = refs[3 + has_b:]
        xp[0:pad, :] = jnp.zeros((pad, tc), F32)
        xp[pad:, :] = x_ref[...]
        dp[T:, :] = jnp.zeros((pad, tc), F32)
        dw_ref[...] = jnp.zeros_like(dw_ref)
        db_ref[...] = jnp.zeros_like(db_ref)

        def step1(t, carry):
            base = pl.multiple_of(t * CONV_ROWS, CONV_ROWS)
            d = dy_ref[pl.ds(base, CONV_ROWS), :]
            win = xp[pl.ds(base, CONV_ROWS + pad), :]
            offs = [pad - (k_w - 1) + k for k in range(k_w)]
            if act:
                acc = jnp.zeros((CONV_ROWS, tc), F32)
                for k, o in enumerate(offs):
                    acc = acc + w_ref[k:k + 1, :] * _shifted(win, o)
                if has_b:
                    acc = acc + b_ref[...]
                sg = jax.nn.sigmoid(acc)
                d = d * (sg * (1.0 + acc * (1.0 - sg)))
            dp[pl.ds(base, CONV_ROWS), :] = d
            for k, o in enumerate(offs):
                dw_ref[k:k + 1, :] += jnp.sum(d * _shifted(win, o), axis=0, keepdims=True)
            db_ref[...] += jnp.sum(d, axis=0, keepdims=True)
            return carry

        lax.fori_loop(0, T // CONV_ROWS, step1, 0)

        def step2(t, carry):
            base = pl.multiple_of(t * CONV_ROWS, CONV_ROWS)
            win = dp[pl.ds(base, CONV_ROWS + pad), :]
            acc = jnp.zeros((CONV_ROWS, tc), F32)
            for k in range(k_w):
                o = (k_w - 1) - k
                acc = acc + w_ref[k:k + 1, :] * _shifted(win, o)
            dx_ref[pl.ds(base, CONV_ROWS), :] = acc
            return carry

        lax.fori_loop(0, T // CONV_ROWS, step2, 0)

    col = lambda r: pl.BlockSpec((r, tc), lambda j: (0, j))
    ins = [x, w, dy] + ([b] if has_b else [])
    return pl.pallas_call(
        body, grid=(c // tc,), in_specs=[col(T), col(k_w), col(T)] + ([col(1)] if has_b else []),
        out_specs=[col(T), col(k_w), col(1)], out_shape=[_sds((T, c)), _sds((k_w, c)), _sds((1, c))],
        scratch_shapes=[pltpu.VMEM((T + pad, tc), F32), pltpu.VMEM((T + pad, tc), F32)], name=name,
        compiler_params=_cp("parallel"),
    )(*ins)


def _attn_consts(n):
    i = _iota2((BLOCK, 2 * BLOCK), 0)
    j = _iota2((BLOCK, 2 * BLOCK), 1)
    dist = i + BLOCK - j
    valid = (dist >= 0) & (dist < WINDOW) & ((n > 0) | (j >= BLOCK))
    return dist.astype(F32), valid


def _attn_block(q4, kk, vv, sinks, dist, valid, kv):
    outs = []
    lane = _iota2((1, HEADS), 1)
    for g in range(GROUP):
        h = kv * GROUP + g
        slope = 2.0 ** (-8.0 * (h + 1) / HEADS)
        s = _nt(q4[:, g * HDIM:(g + 1) * HDIM], kk) * (HDIM ** -0.5)
        s = jnp.where(valid, s - slope * dist, -1e30)
        sink = jnp.sum(jnp.where(lane == h, sinks, 0.0), axis=1, keepdims=True)
        m = jnp.maximum(jnp.max(s, axis=-1, keepdims=True), sink)
        e = jnp.exp(s - m)
        p = e / (jnp.sum(e, axis=-1, keepdims=True) + jnp.exp(sink - m))
        outs.append(_nn(p, vv))
    return tuple(outs)


def _attn_fwd(name, qa, ka, va, sinks):
    def body(q_ref, k_ref, v_ref, s_ref, o_ref, kp, vp):
        kp[0:BLOCK, :] = jnp.zeros((BLOCK, KV_A), F32)
        vp[0:BLOCK, :] = jnp.zeros((BLOCK, KV_A), F32)
        kp[BLOCK:, :] = k_ref[...]
        vp[BLOCK:, :] = v_ref[...]
        sinks_v = s_ref[...]

        def step(n, carry):
            r = pl.multiple_of(n * BLOCK, BLOCK)
            dist, valid = _attn_consts(n)
            k2 = kp[pl.ds(r, 2 * BLOCK), :]
            v2 = vp[pl.ds(r, 2 * BLOCK), :]
            for kv in range(KV_HEADS):
                q4 = q_ref[pl.ds(r, BLOCK), kv * GROUP * HDIM:(kv + 1) * GROUP * HDIM]
                og = _attn_block(q4, k2[:, kv * HDIM:(kv + 1) * HDIM], v2[:, kv * HDIM:(kv + 1) * HDIM], sinks_v,
                                 dist, valid, kv)
                for g in range(GROUP):
                    h = kv * GROUP + g
                    o_ref[pl.ds(r, BLOCK), h * HDIM:(h + 1) * HDIM] = og[g]
            return carry

        lax.fori_loop(0, T // BLOCK, step, 0)

    return pl.pallas_call(
        body, out_shape=_sds((T, Q_A)),
        scratch_shapes=[pltpu.VMEM((T + BLOCK, KV_A), F32), pltpu.VMEM((T + BLOCK, KV_A), F32)], name=name,
        compiler_params=pltpu.CompilerParams(vmem_limit_bytes=VMEM_LIMIT),
    )(qa, ka, va, sinks)


def _attn_bwd(name, qa, ka, va, sinks, do):
    def body(q_ref, k_ref, v_ref, s_ref, do_ref, dq_ref, dk_ref, dv_ref, ds_ref, kp, vp, dkp, dvp):
        kp[0:BLOCK, :] = jnp.zeros((BLOCK, KV_A), F32)
        vp[0:BLOCK, :] = jnp.zeros((BLOCK, KV_A), F32)
        kp[BLOCK:, :] = k_ref[...]
        vp[BLOCK:, :] = v_ref[...]
        dkp[...] = jnp.zeros_like(dkp)
        dvp[...] = jnp.zeros_like(dvp)
        ds_ref[...] = jnp.zeros_like(ds_ref)
        sinks_v = s_ref[...]

        def step(n, carry):
            r = pl.multiple_of(n * BLOCK, BLOCK)
            dist, valid = _attn_consts(n)
            k2 = kp[pl.ds(r, 2 * BLOCK), :]
            v2 = vp[pl.ds(r, 2 * BLOCK), :]
            for kv in range(KV_HEADS):
                cols = slice(kv * HDIM, (kv + 1) * HDIM)
                q4 = q_ref[pl.ds(r, BLOCK), kv * GROUP * HDIM:(kv + 1) * GROUP * HDIM]
                _, vjp = jax.vjp(lambda q, k, v, s: _attn_block(q, k, v, s, dist, valid, kv),
                                 q4, k2[:, cols], v2[:, cols], sinks_v)
                cts = tuple(do_ref[pl.ds(r, BLOCK), (kv * GROUP + g) * HDIM:(kv * GROUP + g + 1) * HDIM]
                            for g in range(GROUP))
                dq4, dkk, dvv, dsk = vjp(cts)
                dq_ref[pl.ds(r, BLOCK), kv * GROUP * HDIM:(kv + 1) * GROUP * HDIM] = dq4
                dkp[pl.ds(r, 2 * BLOCK), cols] += dkk
                dvp[pl.ds(r, 2 * BLOCK), cols] += dvv
                ds_ref[...] += dsk
            return carry

        lax.fori_loop(0, T // BLOCK, step, 0)
        dk_ref[...] = dkp[BLOCK:, :]
        dv_ref[...] = dvp[BLOCK:, :]

    pad = lambda: pltpu.VMEM((T + BLOCK, KV_A), F32)
    return pl.pallas_call(
        body, out_shape=[_sds((T, Q_A)), _sds((T, KV_A)), _sds((T, KV_A)), _sds((1, HEADS))],
        scratch_shapes=[pad(), pad(), pad(), pad()], name=name,
        compiler_params=pltpu.CompilerParams(vmem_limit_bytes=VMEM_LIMIT),
    )(qa, ka, va, sinks, do)


def _dn_consts():
    i = _iota2((CHUNK, CHUNK), 0)
    j = _iota2((CHUNK, CHUNK), 1)
    return dict(causal=i >= j, strict=i > j, ltri=(i >= j).astype(F32),
                last=(_iota2((CHUNK, 1), 0) == CHUNK - 1).astype(F32))


def _l2norm(x):
    return x * lax.rsqrt(jnp.sum(x * x, axis=-1, keepdims=True) + EPS)


def _head_cols(m):
    lane = _iota2((1, HEADS), 1)
    return jnp.concatenate([jnp.sum(jnp.where(lane == h, m, 0.0), axis=1, keepdims=True)[None]
                            for h in range(HEADS)], axis=0)


@jax.custom_vjp
def _unit_lower_inverse(low, known):
    if known is not None:
        return known
    inv = (_iota2((CHUNK, CHUNK), 0) == _iota2((CHUNK, CHUNK), 1)).astype(F32) - low
    pw = low
    for _ in range(5):
        pw = _dg(pw, pw, 1, 0, True)
        inv = inv + _dg(inv, pw, 1, 0, True)
    return inv


def _unit_lower_inverse_fwd(low, known):
    inv = _unit_lower_inverse(low, known)
    return inv, (inv, known)


def _unit_lower_inverse_bwd(res, g):
    inv, known = res
    d_low = -_dg(inv, _dg(g, inv, 1, 1, True), 0, 0, True)
    return d_low, (None if known is None else jnp.zeros_like(known))


_unit_lower_inverse.defvjp(_unit_lower_inverse_fwd, _unit_lower_inverse_bwd)


def _dn_local(q3, k3, v3, braw, araw, alog, dtb, cs, known_inv=None):
    q = _l2norm(q3) * (HDIM ** -0.5)
    k = _l2norm(k3)
    g = -jnp.exp(alog) * jax.nn.softplus(araw + dtb)
    gc_all = _nn_hi(cs["ltri"], g)
    egc_all = jnp.exp(gc_all)
    beta, gc, egc = _head_cols(jax.nn.sigmoid(braw)), _head_cols(gc_all), _head_cols(egc_all)
    a = jnp.broadcast_to(gc, (HEADS, CHUNK, CHUNK))
    diff = a - jnp.swapaxes(a, 1, 2)
    decay = jnp.where(cs["causal"], jnp.exp(jnp.where(cs["causal"], diff, 0.0)), 0.0)
    kb = k * beta
    low = jnp.where(cs["strict"], _nt(kb, k) * decay, 0.0)
    inv = _unit_lower_inverse(low, known_inv)
    u = _nn_hi(inv, v3 * beta)
    w = _nn_hi(inv, kb * egc)
    attn = _nt(q, k) * decay
    gc_last = jnp.sum(gc * cs["last"], axis=1, keepdims=True)
    return u, w, attn, q * egc, k * jnp.exp(gc_last - gc), egc_all, inv


def _heads3(ref, off=0):
    return jnp.concatenate([ref[:, off + h * HDIM:off + (h + 1) * HDIM][None] for h in range(HEADS)], axis=0)


def _dn_local_fwd(name, qkv, ba, alog, dtb):
    def body(qkv_ref, ba_ref, al_ref, dt_ref, u_ref, w_ref, at_ref, qd_ref, kd_ref, eg_ref, inv_ref):
        bav = ba_ref[...]
        outs = _dn_local(_heads3(qkv_ref), _heads3(qkv_ref, 512), _heads3(qkv_ref, 1024), bav[:, :HEADS],
                         bav[:, HEADS:], al_ref[...], dt_ref[...], _dn_consts())
        for r, o in zip((u_ref, w_ref, at_ref, qd_ref, kd_ref, inv_ref), outs[:5] + outs[6:]):
            _unheads(r, o)
        eg_ref[...] = outs[5]

    row = lambda w_: pl.BlockSpec((CHUNK, w_), lambda n: (n, 0))
    return pl.pallas_call(
        body, grid=(NCHUNK,), in_specs=[row(QKV_B), row(2 * HEADS), _full((1, HEADS)), _full((1, HEADS))],
        out_specs=[row(V_B)] * 5 + [row(HEADS), row(V_B)],
        out_shape=[_sds((T, V_B))] * 5 + [_sds((T, HEADS)), _sds((T, V_B))], name=name,
        compiler_params=_cp("parallel"),
    )(qkv, ba, alog, dtb)


def _dn_local_bwd(name, qkv, ba, alog, dtb, inv, cts):
    def body(qkv_ref, ba_ref, al_ref, dt_ref, inv_ref, du_ref, dw_ref, dat_ref, dqd_ref, dkd_ref, deg_ref,
             dqkv_ref, dba_ref, dal_ref, ddt_ref):
        @pl.when(pl.program_id(0) == 0)
        def _():
            dal_ref[...] = jnp.zeros_like(dal_ref)
            ddt_ref[...] = jnp.zeros_like(ddt_ref)

        cs = _dn_consts()
        bav = ba_ref[...]
        known = _heads3(inv_ref)
        _, vjp = jax.vjp(lambda *a: _dn_local(*a, cs, known)[:6], _heads3(qkv_ref), _heads3(qkv_ref, 512),
                         _heads3(qkv_ref, 1024), bav[:, :HEADS], bav[:, HEADS:], al_ref[...], dt_ref[...])
        dq, dk, dv, dbr, dar, dal, ddt = vjp((_heads3(du_ref), _heads3(dw_ref), _heads3(dat_ref), _heads3(dqd_ref),
                                              _heads3(dkd_ref), deg_ref[...]))
        for h in range(HEADS):
            dqkv_ref[:, h * HDIM:(h + 1) * HDIM] = dq[h]
            dqkv_ref[:, 512 + h * HDIM:512 + (h + 1) * HDIM] = dk[h]
            dqkv_ref[:, 1024 + h * HDIM:1024 + (h + 1) * HDIM] = dv[h]
        dba_ref[:, :HEADS] = dbr
        dba_ref[:, HEADS:] = dar
        dal_ref[...] += dal
        ddt_ref[...] += ddt

    row = lambda w_: pl.BlockSpec((CHUNK, w_), lambda n: (n, 0))
    return pl.pallas_call(
        body, grid=(NCHUNK,),
        in_specs=[row(QKV_B), row(2 * HEADS), _full((1, HEADS)), _full((1, HEADS))] + [row(V_B)] * 6 + [row(HEADS)],
        out_specs=[row(QKV_B), row(2 * HEADS), _full((1, HEADS)), _full((1, HEADS))],
        out_shape=[_sds((T, QKV_B)), _sds((T, 2 * HEADS)), _sds((1, HEADS)), _sds((1, HEADS))], name=name,
        compiler_params=_cp("arbitrary"),
    )(qkv, ba, alog, dtb, inv, *cts)


def _dn_step(s, u, w, attn, qd, kd, egc, z, nw):
    last = (_iota2((CHUNK, 1), 0) == CHUNK - 1).astype(F32)
    gl = jnp.sum(_head_cols(egc) * last, axis=1, keepdims=True)
    v_new = u - _nn(w, s)
    o = _nn(qd, s) + _nn(attn, v_new)
    s_new = s * gl + _tn(kd, v_new)
    return s_new, _rms(o, nw) * _silu(z)


def _unheads(ref, v3):
    for h in range(HEADS):
        ref[:, h * HDIM:(h + 1) * HDIM] = v3[h]


def _dn_rec_fwd(name, u, w, attn, qd, kd, egc, z, nw):
    def body(u_ref, w_ref, at_ref, qd_ref, kd_ref, eg_ref, z_ref, nw_ref, o_ref, ss_ref, s_scr):
        @pl.when(pl.program_id(0) == 0)
        def _():
            s_scr[...] = jnp.zeros_like(s_scr)

        s = s_scr[...]
        ss_ref[...] = s
        s_new, on = _dn_step(s, _heads3(u_ref), _heads3(w_ref), _heads3(at_ref), _heads3(qd_ref), _heads3(kd_ref),
                             eg_ref[...], _heads3(z_ref), nw_ref[...])
        s_scr[...] = s_new
        _unheads(o_ref, on)

    row = lambda w_: pl.BlockSpec((CHUNK, w_), lambda n: (n, 0))
    return pl.pallas_call(
        body, grid=(NCHUNK,), in_specs=[row(V_B)] * 5 + [row(HEADS), row(V_B), _full((1, HDIM))],
        out_specs=[row(V_B), pl.BlockSpec((None, HEADS, HDIM, HDIM), lambda n: (n, 0, 0, 0))],
        out_shape=[_sds((T, V_B)), _sds((NCHUNK, HEADS, HDIM, HDIM))],
        scratch_shapes=[pltpu.VMEM((HEADS, HDIM, HDIM), F32)], name=name, compiler_params=_cp("arbitrary"),
    )(u, w, attn, qd, kd, egc, z, nw)


def _dn_rec_bwd(name, u, w, attn, qd, kd, egc, z, nw, ss, do):
    def body(u_ref, w_ref, at_ref, qd_ref, kd_ref, eg_ref, z_ref, nw_ref, ss_ref, do_ref,
             du_ref, dw_ref, dat_ref, dqd_ref, dkd_ref, deg_ref, dz_ref, dnw_ref, ds_scr):
        @pl.when(pl.program_id(0) == 0)
        def _():
            ds_scr[...] = jnp.zeros_like(ds_scr)
            dnw_ref[...] = jnp.zeros_like(dnw_ref)

        _, vjp = jax.vjp(_dn_step, ss_ref[...], _heads3(u_ref), _heads3(w_ref), _heads3(at_ref), _heads3(qd_ref),
                         _heads3(kd_ref), eg_ref[...], _heads3(z_ref), nw_ref[...])
        ds, du, dw, dat, dqd, dkd, deg, dz, dnw = vjp((ds_scr[...], _heads3(do_ref)))
        ds_scr[...] = ds
        for r, v in zip((du_ref, dw_ref, dat_ref, dqd_ref, dkd_ref, dz_ref), (du, dw, dat, dqd, dkd, dz)):
            _unheads(r, v)
        deg_ref[...] = deg
        dnw_ref[...] += dnw

    row = lambda w_: pl.BlockSpec((CHUNK, w_), lambda n: (NCHUNK - 1 - n, 0))
    return pl.pallas_call(
        body, grid=(NCHUNK,),
        in_specs=[row(V_B)] * 5 + [row(HEADS), row(V_B), _full((1, HDIM)),
                                   pl.BlockSpec((None, HEADS, HDIM, HDIM), lambda n: (NCHUNK - 1 - n, 0, 0, 0)),
                                   row(V_B)],
        out_specs=[row(V_B)] * 5 + [row(HEADS), row(V_B), _full((1, HDIM))],
        out_shape=[_sds((T, V_B))] * 5 + [_sds((T, HEADS)), _sds((T, V_B)), _sds((1, HDIM))],
        scratch_shapes=[pltpu.VMEM((HEADS, HDIM, HDIM), F32)], name=name, compiler_params=_cp("arbitrary"),
    )(u, w, attn, qd, kd, egc, z, nw, ss, do)


def _final(name, x, fw, target, tm=512):
    def body(x_ref, fw_ref, t_ref, l_ref, dx_ref, dfw_ref):
        @pl.when(pl.program_id(0) == 0)
        def _():
            l_ref[...] = jnp.zeros_like(l_ref)
            dfw_ref[...] = jnp.zeros_like(dfw_ref)

        tv = t_ref[...]

        def f(xv, fwv):
            err = _rms(xv, fwv) - tv
            per_tok = jnp.mean(err * err, axis=-1, keepdims=True)
            return 0.5 * jnp.sum(per_tok, axis=0, keepdims=True)

        loss, vjp = jax.vjp(f, x_ref[...], fw_ref[...])
        dx, dfw = vjp(jnp.ones((1, 1), F32))
        l_ref[...] += loss
        dx_ref[...] = dx
        dfw_ref[...] += dfw

    tok = pl.BlockSpec((tm, D), lambda i: (i, 0))
    return pl.pallas_call(
        body, grid=(T // tm,), in_specs=[tok, _full((1, D)), tok], out_specs=[_full((1, 1)), tok, _full((1, D))],
        out_shape=[_sds((1, 1)), _sds((T, D)), _sds((1, D))], name=name, compiler_params=_cp("arbitrary"),
    )(x, fw, target)


def _m1_pre(tv, sv):
    return [_rms(tv[0], sv[0])]


def _m1_post(ys, tv, sv):
    return (jnp.concatenate(ys, axis=1),)


def _m1_post_split(ys, tv, sv):
    proj = jnp.concatenate(ys, axis=1)
    return tuple(proj[:, a:b] for a, b in zip(IN_SPLITS[:-1], IN_SPLITS[1:]))


def _m5_pre(tv, sv):
    return [tv[1], tv[2]]


def _m5_post(ys, tv, sv):
    return (tv[0] + ys[0] + ys[1],)


def _c1_pre(tv, sv):
    return [_rms(tv[0], sv[0])]


def _c1_post(ys, tv, sv):
    return ((jnp.concatenate(ys[:2], axis=1) + sv[1]) * jax.nn.sigmoid(jnp.concatenate(ys[2:], axis=1) + sv[2]),)


def _c3_pre(tv, sv):
    return [_silu(_layernorm(tv[0], sv[0], sv[1]))]


def _c3_post(ys, tv, sv):
    return (tv[1] + ys[0] + sv[2],)


def _row(v):
    return v.reshape(1, -1)


def _mixer_fwd(tag, x, p):
    parts = _blk_fwd(f"m1_fwd_{tag}", _m1_pre, [0], _m1_post_split, [x], [p["nw"]], [p["w_in"]],
                     [(b - a, F32) for a, b in zip(IN_SPLITS[:-1], IN_SPLITS[1:])])
    qa, ka, va, qkvb, z, ba = parts
    att = _attn_fwd(f"attn_fwd_{tag}", qa, ka, va, p["sinks"])
    qkvc = _conv_fwd(f"dnconv_fwd_{tag}", qkvb, p["dn_conv_w"], None, True)
    *loc, inv = _dn_local_fwd(f"dnloc_fwd_{tag}", qkvc, ba, p["a_log"], p["dt_bias"])
    og, ss = _dn_rec_fwd(f"dnrec_fwd_{tag}", *loc, z, p["dn_norm_w"])
    (out,) = _blk_fwd(f"m5_fwd_{tag}", _m5_pre, [0, 1], _m5_post, [x, att, og], [], [p["wo_a"], p["wo_b"]],
                      [(D, F32)])
    return out, dict(x=x, qa=qa, ka=ka, va=va, qkvb=qkvb, z=z, ba=ba, att=att, qkvc=qkvc, loc=loc, inv=inv, og=og,
                     ss=ss)


def _mixer_bwd(tag, dy, p, s):
    (dxa, datt, dog), _, (dwo_a, dwo_b) = _blk_bwd(f"m5_bwd_{tag}", _m5_pre, [0, 1], _m5_post,
                                                   [s["x"], s["att"], s["og"]], [], [p["wo_a"], p["wo_b"]], [[dy]],
                                                   linear_post=True)
    rec = _dn_rec_bwd(f"dnrec_bwd_{tag}", *s["loc"], s["z"], p["dn_norm_w"], s["ss"], dog)
    dz, dnw_dn = rec[6], rec[7]
    dqkvc, dba, dalog, ddtb = _dn_local_bwd(f"dnloc_bwd_{tag}", s["qkvc"], s["ba"], p["a_log"], p["dt_bias"],
                                            s["inv"], rec[:6])
    dqkvb, dconvw, _ = _conv_bwd(f"dnconv_bwd_{tag}", s["qkvb"], p["dn_conv_w"], None, True, dqkvc)
    dqa, dka, dva, dsinks = _attn_bwd(f"attn_bwd_{tag}", s["qa"], s["ka"], s["va"], p["sinks"], datt)
    (dx,), (dnw,), (dw_in,) = _blk_bwd(f"m1_bwd_{tag}", _m1_pre, [0], _m1_post, [s["x"]], [p["nw"]], [p["w_in"]],
                                       [[dqa, dka, dva, dqkvb, dz, dba]], res=dxa, linear_post=True)
    return dx, dict(nw=dnw, w_in=dw_in, wo_a=dwo_a, wo_b=dwo_b, dn_conv_w=dconvw, sinks=dsinks, a_log=dalog,
                    dt_bias=ddtb, dn_norm_w=dnw_dn)


def _conformer_fwd(tag, x, p):
    (glu,) = _blk_fwd(f"c1_fwd_{tag}", _c1_pre, [0], _c1_post, [x], [p["nw"], p["b1a"], p["b1b"]], [p["w1"]],
                      [(D, F32)])
    cc = _conv_fwd(f"dwconv_fwd_{tag}", glu, p["w_dw"], p["b_dw"], False)
    (out,) = _blk_fwd(f"c3_fwd_{tag}", _c3_pre, [0], _c3_post, [cc, x], [p["ln_w"], p["ln_b"], p["b2"]], [p["w2"]],
                      [(D, F32)])
    return out, dict(x=x, glu=glu, cc=cc)


def _conformer_bwd(tag, dy, p, s):
    (dcc, dxa), (dlnw, dlnb, db2), (dw2,) = _blk_bwd(f"c3_bwd_{tag}", _c3_pre, [0], _c3_post, [s["cc"], s["x"]],
                                                     [p["ln_w"], p["ln_b"], p["b2"]], [p["w2"]], [[dy]],
                                                     linear_post=True)
    dglu, dwdw, dbdw = _conv_bwd(f"dwconv_bwd_{tag}", s["glu"], p["w_dw"], p["b_dw"], False, dcc)
    (dx,), (dnw, db1a, db1b), (dw1,) = _blk_bwd(f"c1_bwd_{tag}", _c1_pre, [0], _c1_post, [s["x"]],
                                                [p["nw"], p["b1a"], p["b1b"]], [p["w1"]], [[dglu]], res=dxa)
    return dx, dict(nw=dnw, b1a=db1a, b1b=db1b, w1=dw1, w_dw=dwdw, b_dw=dbdw, ln_w=dlnw, ln_b=dlnb, b2=db2, w2=dw2)


def _layer_fwd(l, x, nw, ffn_a, get_ffn_b, p):
    x1, *pre_a = _ffn_fwd(f"ffn_fwd_{l}a", x, _row(nw[0]), ffn_a, 0)
    p = dict(p, nw=_row(nw[1]))
    x2, sv = (_mixer_fwd if l % 2 == 0 else _conformer_fwd)(str(l), x1, p)
    x2, ffn_b = get_ffn_b(x2)
    out, *pre_b = _ffn_fwd(f"ffn_fwd_{l}b", x2, _row(nw[2]), ffn_b, 0)
    return out, (x, x2, p, sv, pre_a, pre_b, ffn_a, ffn_b)


def _layer_bwd(l, dx, nw, saved, after_first=lambda dx: dx):
    x0, x2, p, sv, pre_a, pre_b, ffn_a, ffn_b = saved
    dx, dn2, dffn = _ffn_bwd(f"ffn_bwd_{l}b", x2, _row(nw[2]), ffn_b, 1, pre_b, dx)
    dx = after_first(dx)
    dx, dmix = (_mixer_bwd if l % 2 == 0 else _conformer_bwd)(str(l), dx, p, sv)
    dx, dn0, dffn = _ffn_bwd(f"ffn_bwd_{l}a", x0, _row(nw[0]), ffn_a, 0, pre_a, dx, dffn)
    return dx, jnp.concatenate([dn0, dmix.pop("nw"), dn2], axis=0), dffn, dmix


def _place(staggered=False):
    x, y, c = lax.axis_index("x"), lax.axis_index("y"), lax.axis_index("c")
    s = c if staggered else 0
    first, second = (x + (1 - s) * (1 - 2 * x), y + s * (1 - 2 * y)), (x + s * (1 - 2 * x), y + (1 - s) * (1 - 2 * y))
    chips = [first, second, (1 - x, 1 - y)]
    return x, y, c, 2 * x + y, chips, [2 * px + py for px, py in chips]


def _handshake(peers):
    barrier = pltpu.get_barrier_semaphore()
    for p in peers:
        pl.semaphore_signal(barrier, inc=1, device_id=p, device_id_type=MESH)
    pl.semaphore_wait(barrier, len(peers))


def _chip_peers():
    x, y, c, _, chips, _ = _place()
    return [(*chip, c) for chip in chips] + [(x, y, 1 - c)]


def _gather_copies(ins, outs, nb, send, recv, fsend, frecv, lsem):
    n_in = len(ins)
    x, y, c, me, chips, cidx = _place(staggered=True)
    sib = (x, y, 1 - c)
    local = [pltpu.make_async_copy(ins[a], outs[a].at[me], lsem.at[a]) for a in range(nb, n_in)]

    def region(a, k, who):
        if k < 2:
            return outs[a].at[cidx[k], pl.ds(who, 1)]
        r = ins[a].shape[1] // 2
        return outs[a].at[cidx[2], pl.ds(who, 1), pl.ds((k - 2) * r, r)]

    def hop(a, k):
        if k < 2:
            src, dst = ins[a].at[pl.ds(c, 1)], outs[a].at[me, pl.ds(c, 1)]
        else:
            r = ins[a].shape[1] // 2
            src = dst = outs[a].at[cidx[3 - k], pl.ds(c, 1), pl.ds((k - 2) * r, r)]
        return pltpu.make_async_remote_copy(src, dst, send.at[4 * a + k], recv.at[4 * a + k],
                                            device_id=(*chips[k % 2], c), device_id_type=MESH)

    def landed(a, k):
        dst = region(a, k, c)
        return pltpu.make_async_remote_copy(dst, dst, send.at[4 * a + k], recv.at[4 * a + k],
                                            device_id=(*chips[k % 2], c), device_id_type=MESH)

    def passed(a, k, who):
        part = region(a, k, who)
        return pltpu.make_async_remote_copy(part, part, fsend.at[4 * a + k], frecv.at[4 * a + k], device_id=sib,
                                            device_id_type=MESH)

    def direct(a, j):
        k = 4 * nb + 3 * (a - nb) + j
        return pltpu.make_async_remote_copy(ins[a], outs[a].at[me], send.at[k], recv.at[k],
                                            device_id=(*chips[j], c), device_id_type=MESH)

    def direct_landed(a, j):
        k = 4 * nb + 3 * (a - nb) + j
        dst = outs[a].at[cidx[j]]
        return pltpu.make_async_remote_copy(dst, dst, send.at[k], recv.at[k], device_id=(*chips[j], c),
                                            device_id_type=MESH)

    sends = [hop(a, k) for a in range(nb) for k in range(2)] + [direct(a, j) for a in range(nb, n_in) for j in range(3)]
    for cp in sends:
        cp.start()
    for cp in local:
        cp.start()
    for a in range(nb):
        for k in (1, 0):
            landed(a, k).wait_recv()
            for cp in (hop(a, 3 - k), passed(a, k, c)):
                cp.start()
                sends.append(cp)
    for a in range(nb):
        for k in (2, 3):
            landed(a, k).wait_recv()
            cp = passed(a, k, c)
            cp.start()
            sends.append(cp)
    for a in range(nb, n_in):
        for j in range(3):
            direct_landed(a, j).wait_recv()
    for a in range(nb):
        for k in range(4):
            passed(a, k, 1 - c).wait_recv()
    for cp in sends:
        cp.wait_send()
    for cp in local:
        cp.wait()


def _gather_sems(n_in, nb):
    dma = pltpu.SemaphoreType.DMA
    n_ici = 4 * nb + 3 * (n_in - nb)
    return [dma((n_ici,)), dma((n_ici,)), dma((4 * nb,)), dma((4 * nb,)), dma((n_in,))]


def _gather_async(name, halved, whole=()):
    nb, arrs = len(halved), list(halved) + list(whole)
    hbm = pltpu.MemorySpace.HBM
    ins = [jax.new_ref(a, memory_space=hbm) for a in arrs]
    outs = [jax.empty_ref(_sds((NCHIP,) + a.shape, a.dtype), memory_space=hbm) for a in arrs]

    @pl.kernel(mesh=plsc.ScalarSubcoreMesh(axis_name="seq", num_cores=1), name=name,
               scratch_types=tuple(_gather_sems(len(arrs), nb)),
               compiler_params=pltpu.CompilerParams(collective_id=2))
    def launch(send, recv, fsend, frecv, lsem):
        _handshake(_chip_peers())
        _gather_copies(ins, outs, nb, send, recv, fsend, frecv, lsem)

    launch()
    return outs


def _swap_halves(name, grads, after=None):
    n = len(grads)
    hbm = pltpu.MemorySpace.HBM
    ins = [jax.new_ref(g, memory_space=hbm) for g in grads]
    outs = [jax.empty_ref(_sds((NCHIP, g.shape[1] // 2) + g.shape[2:], g.dtype), memory_space=hbm) for g in grads]
    tile = (2 * 8, LANES)
    token = None if after is None else jax.empty_ref(_sds(tile, BF16), memory_space=hbm)

    @pl.kernel(mesh=plsc.ScalarSubcoreMesh(axis_name="seq", num_cores=1), name=name,
               scratch_types=(pltpu.SemaphoreType.DMA((n + 1,)), pltpu.SemaphoreType.DMA((n,))),
               compiler_params=pltpu.CompilerParams(collective_id=1))
    def launch(send, recv):
        x, y, c, _, _, _ = _place()
        sib = (x, y, 1 - c)
        _handshake([sib])
        if after is not None:
            tick = pltpu.make_async_copy(after.at[0, 0, 0, pl.ds(0, tile[0]), pl.ds(0, tile[1])], token, send.at[n])
            tick.start()
            tick.wait()
        cps = []
        for a in range(n):
            h = grads[a].shape[1] // 2
            cps.append(pltpu.make_async_remote_copy(ins[a].at[:, pl.ds((1 - c) * h, h)], outs[a], send.at[a],
                                                    recv.at[a], device_id=sib, device_id_type=MESH))
        for cp in cps:
            cp.start()
        for cp in cps:
            cp.wait()

    launch()
    return outs


def _row_tile(r, cap=256):
    return max(t for t in range(8, cap + 1, 8) if r % t == 0)


def _add_half(name, g, r, c_arr):
    _, l, rows, cols = g.shape
    h = l // 2
    tr = _row_tile(rows, 1056)

    def body(c_ref, g_ref, r_ref, o_ref):
        o_ref[...] = (g_ref[...].astype(F32) + r_ref[...].astype(F32)).astype(BF16)

    blk = (None, None, tr, cols)
    return pl.pallas_call(
        body,
        grid_spec=pltpu.PrefetchScalarGridSpec(
            num_scalar_prefetch=1, grid=(NCHIP, h, rows // tr),
            in_specs=[pl.BlockSpec(blk, lambda j, i, t, c_ref: (j, c_ref[0] * h + i, t, 0)),
                      pl.BlockSpec(blk, lambda j, i, t, c_ref: (j, i, t, 0))],
            out_specs=pl.BlockSpec(blk, lambda j, i, t, c_ref: (j, i, t, 0))),
        out_shape=_sds((NCHIP, h, rows, cols), BF16), name=name,
        compiler_params=_cp("parallel", "parallel", "parallel"),
    )(c_arr, g, r)


def _scatter_async(name, parts, sums, where):
    nb = len(parts)
    ins = [jax.new_ref(p, memory_space=pltpu.MemorySpace.HBM) for p in parts]
    dma = pltpu.SemaphoreType.DMA

    @pl.kernel(mesh=plsc.ScalarSubcoreMesh(axis_name="seq", num_cores=1), name=name,
               scratch_types=(dma((3 * nb,)), dma((3 * nb,)), dma((4 * nb,)), dma((4 * nb,)), dma((nb,))),
               compiler_params=pltpu.CompilerParams(collective_id=3))
    def launch(send, recv, fsend, frecv, lsem):
        _handshake(_chip_peers())
        x, y, c, me, chips, cidx = _place(staggered=True)
        sib = (x, y, 1 - c)

        def slot(a, half, chip):
            return sums[a].at[half, chip, pl.ds(where[a], 1)]

        local = [pltpu.make_async_copy(ins[a].at[me], slot(a, c, me), lsem.at[a]) for a in range(nb)]
        for cp in local:
            cp.start()

        def ici(a, j):
            return pltpu.make_async_remote_copy(ins[a].at[cidx[j]], slot(a, c, me), send.at[a * 3 + j],
                                                recv.at[a * 3 + j], device_id=(*chips[j], c), device_id_type=MESH)

        def landed(a, j):
            dst = slot(a, c, cidx[j])
            return pltpu.make_async_remote_copy(dst, dst, send.at[a * 3 + j], recv.at[a * 3 + j],
                                                device_id=(*chips[j], c), device_id_type=MESH)

        def passed(a, j, who):
            dst = slot(a, who, me if j == 3 else cidx[j])
            src = ins[a].at[me] if j == 3 else dst
            return pltpu.make_async_remote_copy(src, dst, fsend.at[a * 4 + j], frecv.at[a * 4 + j], device_id=sib,
                                                device_id_type=MESH)

        sends = [ici(a, j) for a in range(nb) for j in range(3)] + [passed(a, 3, c) for a in range(nb)]
        for cp in sends:
            cp.start()
        for a in range(nb):
            for j in range(3):
                landed(a, j).wait_recv()
                cp = passed(a, j, c)
                cp.start()
                sends.append(cp)
        for a in range(nb):
            for j in range(4):
                passed(a, j, 1 - c).wait_recv()
        for cp in sends:
            cp.wait_send()
        for cp in local:
            cp.wait()

    launch()


def _exchange_small(small, rep):
    def body(small_in, rep_in, small_out, rep_out, lsem, ssend, srecv):
        x, y, c, me, _, _ = _place()
        dev = 4 * x + 2 * y + c
        local = [pltpu.make_async_copy(small_in.at[me], small_out.at[dev], lsem.at[0]),
                 pltpu.make_async_copy(rep_in, rep_out.at[dev], lsem.at[1])]
        for cp in local:
            cp.start()

        def peer(r):
            return (1 - x if r & 4 else x), (1 - y if r & 2 else y), (1 - c if r & 1 else c)

        def tiny(r, which):
            px, py, pc = peer(r)
            k = (r - 1) * 2 + which
            if which == 0:
                return pltpu.make_async_remote_copy(small_in.at[2 * px + py], small_out.at[dev], ssend.at[k],
                                                    srecv.at[k], device_id=(px, py, pc), device_id_type=MESH)
            return pltpu.make_async_remote_copy(rep_in, rep_out.at[dev], ssend.at[k], srecv.at[k],
                                                device_id=(px, py, pc), device_id_type=MESH)

        def tiny_landed(r, which):
            px, py, pc = peer(r)
            k = (r - 1) * 2 + which
            dst = (small_out if which == 0 else rep_out).at[4 * px + 2 * py + pc]
            return pltpu.make_async_remote_copy(dst, dst, ssend.at[k], srecv.at[k], device_id=(px, py, pc),
                                                device_id_type=MESH)

        sends = [tiny(r, w) for r in range(1, NDEV) for w in range(2)]
        for cp in sends:
            cp.start()
        for r in range(1, NDEV):
            for w in range(2):
                tiny_landed(r, w).wait_recv()
        for cp in sends:
            cp.wait_send()
        for cp in local:
            cp.wait()

    dma = pltpu.SemaphoreType.DMA
    return pl.pallas_call(
        body, in_specs=[ANY] * 2, out_specs=[ANY] * 2,
        out_shape=[_sds((NDEV,) + small.shape[1:], F32), _sds((NDEV,) + rep.shape, F32)],
        scratch_shapes=[dma((2,)), dma((2 * (NDEV - 1),)), dma((2 * (NDEV - 1),))], name="exchange_small_grads",
    )(small, rep)


def _adamw_math(w, g, m, v):
    m = B1 * m + (1.0 - B1) * g
    v = B2 * v + (1.0 - B2) * (g * g)
    m_hat = m / (1.0 - B1 ** STEP)
    v_hat = v / (1.0 - B2 ** STEP)
    return -LR * (m_hat / (jnp.sqrt(v_hat) + AEPS) + WD * w), m, v


def _adamw_big(name, w, m, v, parts, row0=0, first=0, outs=None):
    _, _, rows, cols = w.shape
    n = parts.shape[2]
    tr = _row_tile(rows, 352)
    t0 = row0 // tr

    def body(w_ref, m_ref, v_ref, p_ref, *rest):
        g_ref, d_ref, nm_ref, nv_ref = rest[-4:]
        g = p_ref[0].astype(F32)
        for q in range(1, NCHIP):
            g = g + p_ref[q].astype(F32)
        d, nm, nv = _adamw_math(w_ref[...], g, m_ref[...], v_ref[...])
        g_ref[...], d_ref[...], nm_ref[...], nv_ref[...] = g, d, nm, nv

    spec = pl.BlockSpec((None, None, tr, cols), lambda i, p, t: (first + i, p, t, 0))
    na = 0 if outs is None else 4
    return pl.pallas_call(
        body, grid=(n, 2, rows // tr),
        in_specs=[spec, spec, spec,
                  pl.BlockSpec((None, NCHIP, None, tr, cols), lambda i, p, t: (p, 0, i, t0 + t, 0))] + [ANY] * na,
        out_specs=[spec] * 4, out_shape=[_sds(w.shape)] * 4, input_output_aliases={4 + k: k for k in range(na)},
        name=name, compiler_params=_cp("parallel", "parallel", "parallel"),
    )(w, m, v, parts, *(outs or ()))


def _adamw_small(name, w, m, v, parts):
    def body(w_ref, m_ref, v_ref, p_ref, g_ref, d_ref, nm_ref, nv_ref):
        g = p_ref[0]
        for q in range(1, NDEV):
            g = g + p_ref[q]
        d, nm, nv = _adamw_math(w_ref[...], g, m_ref[...], v_ref[...])
        g_ref[...], d_ref[...], nm_ref[...], nv_ref[...] = g, d, nm, nv

    return pl.pallas_call(body, out_shape=[_sds(w.shape)] * 4, name=name)(w, m, v, parts)


def _pack(arrs, rows):
    flat = jnp.concatenate([a.reshape(-1) for a in arrs])
    return jnp.pad(flat, (0, rows * LANES - flat.shape[0])).reshape(rows, LANES)


def _unpack(packed, shapes):
    flat, out, o = packed.reshape(-1), [], 0
    for s in shapes:
        n = 1
        for d in s:
            n *= d
        out.append(flat[o:o + n].reshape(s))
        o += n
    return out


SMALL_ROWS, REP_ROWS = 200, 16


def kernel(x, norm_w, ffn_w_gate, ffn_w_up, ffn_w_down, mix_w_in, dn_conv_w, attn_sinks, dn_a_log, dn_dt_bias, dn_norm_w, mix_w_out, conv_w_pw1, conv_b_pw1, conv_w_dw, conv_b_dw, conv_ln_w, conv_ln_b, conv_w_pw2, conv_b_pw2, final_norm_w, loss_target, m_norm_w, m_ffn_w_gate, m_ffn_w_up, m_ffn_w_down, m_mix_w_in, m_dn_conv_w, m_attn_sinks, m_dn_a_log, m_dn_dt_bias, m_dn_norm_w, m_mix_w_out, m_conv_w_pw1, m_conv_b_pw1, m_conv_w_dw, m_conv_b_dw, m_conv_ln_w, m_conv_ln_b, m_conv_w_pw2, m_conv_b_pw2, m_final_norm_w, v_norm_w, v_ffn_w_gate, v_ffn_w_up, v_ffn_w_down, v_mix_w_in, v_dn_conv_w, v_attn_sinks, v_dn_a_log, v_dn_dt_bias, v_dn_norm_w, v_mix_w_out, v_conv_w_pw1, v_conv_b_pw1, v_conv_w_dw, v_conv_b_dw, v_conv_ln_w, v_conv_ln_b, v_conv_w_pw2, v_conv_b_pw2, v_final_norm_w):
    small_names = ["norm_w", "dn_conv_w", "conv_b_pw1", "conv_w_dw", "conv_b_dw", "conv_ln_w", "conv_ln_b",
                   "conv_b_pw2"]
    rep_names = ["attn_sinks", "dn_a_log", "dn_dt_bias", "dn_norm_w", "final_norm_w"]
    w = dict(norm_w=norm_w, ffn_w_gate=ffn_w_gate, ffn_w_up=ffn_w_up, ffn_w_down=ffn_w_down, mix_w_in=mix_w_in, dn_conv_w=dn_conv_w, attn_sinks=attn_sinks, dn_a_log=dn_a_log, dn_dt_bias=dn_dt_bias, dn_norm_w=dn_norm_w, mix_w_out=mix_w_out, conv_w_pw1=conv_w_pw1, conv_b_pw1=conv_b_pw1, conv_w_dw=conv_w_dw, conv_b_dw=conv_b_dw, conv_ln_w=conv_ln_w, conv_ln_b=conv_ln_b, conv_w_pw2=conv_w_pw2, conv_b_pw2=conv_b_pw2, final_norm_w=final_norm_w)
    m = dict(norm_w=m_norm_w, ffn_w_gate=m_ffn_w_gate, ffn_w_up=m_ffn_w_up, ffn_w_down=m_ffn_w_down, mix_w_in=m_mix_w_in, dn_conv_w=m_dn_conv_w, attn_sinks=m_attn_sinks, dn_a_log=m_dn_a_log, dn_dt_bias=m_dn_dt_bias, dn_norm_w=m_dn_norm_w, mix_w_out=m_mix_w_out, conv_w_pw1=m_conv_w_pw1, conv_b_pw1=m_conv_b_pw1, conv_w_dw=m_conv_w_dw, conv_b_dw=m_conv_b_dw, conv_ln_w=m_conv_ln_w, conv_ln_b=m_conv_ln_b, conv_w_pw2=m_conv_w_pw2, conv_b_pw2=m_conv_b_pw2, final_norm_w=m_final_norm_w)
    v = dict(norm_w=v_norm_w, ffn_w_gate=v_ffn_w_gate, ffn_w_up=v_ffn_w_up, ffn_w_down=v_ffn_w_down, mix_w_in=v_mix_w_in, dn_conv_w=v_dn_conv_w, attn_sinks=v_attn_sinks, dn_a_log=v_dn_a_log, dn_dt_bias=v_dn_dt_bias, dn_norm_w=v_dn_norm_w, mix_w_out=v_mix_w_out, conv_w_pw1=v_conv_w_pw1, conv_b_pw1=v_conv_b_pw1, conv_w_dw=v_conv_w_dw, conv_b_dw=v_conv_b_dw, conv_ln_w=v_conv_ln_w, conv_ln_b=v_conv_ln_b, conv_w_pw2=v_conv_w_pw2, conv_b_pw2=v_conv_b_pw2, final_norm_w=v_final_norm_w)
    order = ["norm_w", "ffn_w_gate", "ffn_w_up", "ffn_w_down", "mix_w_in", "dn_conv_w", "attn_sinks", "dn_a_log",
             "dn_dt_bias", "dn_norm_w", "mix_w_out", "conv_w_pw1", "conv_b_pw1", "conv_w_dw", "conv_b_dw",
             "conv_ln_w", "conv_ln_b", "conv_w_pw2", "conv_b_pw2", "final_norm_w"]

    small_shapes = [w[n].shape for n in small_names]
    rep_shapes = [w[n].shape for n in rep_names]

    def halves(a):
        return a.reshape(a.shape[:-2] + (2, a.shape[-2] // 2, a.shape[-1]))

    tr = lambda a: jnp.swapaxes(a, -1, -2)
    gate_t, up_t = tr(ffn_w_gate), tr(ffn_w_up)

    def layer_shards(l):
        mix_in, mix_out = (mix_w_in, mix_w_out) if l % 2 == 0 else (conv_w_pw1, conv_w_pw2)
        ffn = jnp.concatenate([gate_t[l], up_t[l], ffn_w_down[l]], axis=1)
        return ([t.astype(BF16) for t in (halves(ffn[0]), halves(mix_in[l // 2]), halves(mix_out[l // 2]))],
                [halves(ffn[1]).astype(BF16)])

    first = layer_shards(0)
    first = (first[0] + [_pack([w[n] for n in small_names], SMALL_ROWS)], first[1])
    first, (gate_t, up_t, ffn_w_down, mix_w_in, mix_w_out, conv_w_pw1, conv_w_pw2) = lax.optimization_barrier(
        (first, (gate_t, up_t, ffn_w_down, mix_w_in, mix_w_out, conv_w_pw1, conv_w_pw2)))
    gathering = [(_gather_async("gather_layer0a", first[0][:3], first[0][3:]),
                  _gather_async("gather_layer0b", first[1]))]
    own = [(first[0][:3], first[1])]
    for l in range(1, DEPTH):
        before, after = layer_shards(l)
        gathering.append((_gather_async(f"gather_layer{l}a", before), _gather_async(f"gather_layer{l}b", after)))
        own.append((before, after))
    ffn_block = lambda g: g.reshape(NCHIP, 1, 3 * FS, D)
    me = 2 * lax.axis_index("x") + lax.axis_index("y")
    with_own = lambda g, shard: lax.dynamic_update_slice(g, shard[None], (me, 0, 0, 0))

    def mixer_params(l, w_a, w_b):
        e = l // 2
        w_a = w_a.reshape(NCHIP, D, -1)
        w_b = w_b.reshape(D, D)
        if l % 2 == 0:
            return dict(w_in=w_a, dn_conv_w=sm["dn_conv_w"][e], sinks=_row(attn_sinks[e]), a_log=_row(dn_a_log[e]),
                        dt_bias=_row(dn_dt_bias[e]), dn_norm_w=_row(dn_norm_w[e]), wo_a=w_b[:Q_A], wo_b=w_b[Q_A:])
        return dict(b1a=_row(sm["conv_b_pw1"][e, :D]), b1b=_row(sm["conv_b_pw1"][e, D:]), w1=w_a,
                    w_dw=sm["conv_w_dw"][e], b_dw=_row(sm["conv_b_dw"][e]), ln_w=_row(sm["conv_ln_w"][e]),
                    ln_b=_row(sm["conv_ln_b"][e]), b2=_row(sm["conv_b_pw2"][e]), w2=w_b)

    xs, saved = x[0], []
    for l in range(DEPTH):
        got = [r[...] for r in gathering[l][0]]
        if l == 0:
            per_chip = [_unpack(got[3][q], small_shapes) for q in range(NCHIP)]
            sm = {n: jnp.concatenate([per_chip[q][i] for q in range(NCHIP)], axis=-1)
                  for i, n in enumerate(small_names)}
        else:
            xs, got = lax.optimization_barrier((xs, got))
        got[:3] = [with_own(g, s) for g, s in zip(got[:3], own[l][0])]

        def second_ffn(x2, l=l):
            x2, got_b = lax.optimization_barrier((x2, gathering[l][1][0][...]))
            return x2, ffn_block(with_own(got_b, own[l][1][0]))

        xs, sv = _layer_fwd(l, xs, sm["norm_w"][l], ffn_block(got[0]), second_ffn, mixer_params(l, got[1], got[2]))
        saved.append(sv)
    loss, dx, dfw = _final("final", xs, _row(final_norm_w), loss_target[0])

    hbm = pltpu.MemorySpace.HBM
    row_shapes = dict(ffn=(3 * FS, D), w_in=(D // 2, IN_COLS // NCHIP), w_out=(D // 8, D), pw1=(D // 2, D // 2),
                      pw2=(D // 8, D))
    new_sums = lambda k, n: jax.empty_ref(_sds((2, NCHIP, n) + row_shapes[k], BF16), memory_space=hbm)
    sums_0 = {k: new_sums(k, 1) for k in ("ffn", "w_in", "w_out")}
    sums = dict(ffn=new_sums("ffn", DEPTH - 1), w_in=new_sums("w_in", 1), w_out=new_sums("w_out", 1),
                pw1=new_sums("pw1", 2), pw2=new_sums("pw2", 2))
    c_arr = lax.axis_index("c").astype(jnp.int32).reshape(1)
    dnorm, gmix = [None] * DEPTH, [None] * DEPTH

    def hand_on(l, grads, swapped):
        def run(dx):
            dx, other = lax.optimization_barrier((dx, [r[...] for r in swapped]))
            parts = [_add_half(f"add_half_{l}_{k}", gg, rr, c_arr) for k, (gg, rr) in enumerate(zip(grads, other))]
            dx, parts = lax.optimization_barrier((dx, parts))
            keys = ("ffn", "w_in", "w_out") if l % 2 == 0 else ("ffn", "pw1", "pw2")
            if l == 0:
                _scatter_async("scatter_grads_0", parts, [sums_0[k] for k in keys], [0, 0, 0])
            else:
                _scatter_async(f"scatter_grads_{l}", parts, [sums[k] for k in keys],
                               [l - 1, 0, 0] if l % 2 == 0 else [l - 1, l // 2, l // 2])
            return dx
        return run

    pending = lambda dx: dx
    for l in reversed(range(DEPTH)):
        dx, dnorm[l], dffn, gmix[l] = _layer_bwd(l, dx, sm["norm_w"][l], saved[l], pending)
        if l % 2 == 0:
            g_a, g_b = gmix[l]["w_in"], jnp.concatenate([gmix[l]["wo_a"], gmix[l]["wo_b"]], axis=0)
        else:
            g_a, g_b = gmix[l]["w1"], gmix[l]["w2"]
        g_a = halves(g_a).astype(BF16)
        g_b = g_b.reshape(NCHIP, 2, D // 8, D).astype(BF16)
        dx, grads = lax.optimization_barrier((dx, [dffn, g_a, g_b]))
        pending = hand_on(l, grads, _swap_halves(f"swap_grads_{l}", grads, sums["ffn"] if l < DEPTH - 1 else None))
    gm, gc = [gmix[0], gmix[2]], [gmix[1], gmix[3]]
    small_g = dict(
        norm_w=jnp.stack(dnorm), dn_conv_w=jnp.stack([gm[e]["dn_conv_w"] for e in range(2)]),
        conv_b_pw1=jnp.stack([jnp.concatenate([gc[e]["b1a"], gc[e]["b1b"]], axis=1)[0] for e in range(2)]),
        conv_w_dw=jnp.stack([gc[e]["w_dw"] for e in range(2)]),
        conv_b_dw=jnp.stack([gc[e]["b_dw"][0] for e in range(2)]),
        conv_ln_w=jnp.stack([gc[e]["ln_w"][0] for e in range(2)]),
        conv_ln_b=jnp.stack([gc[e]["ln_b"][0] for e in range(2)]),
        conv_b_pw2=jnp.stack([gc[e]["b2"][0] for e in range(2)]))
    small_by_chip = jnp.stack([_pack([jnp.split(small_g[n], NCHIP, axis=-1)[q] for n in small_names], SMALL_ROWS)
                               for q in range(NCHIP)])
    rep_g = _pack([jnp.stack([gm[e]["sinks"][0] for e in range(2)]), jnp.stack([gm[e]["a_log"][0] for e in range(2)]),
                   jnp.stack([gm[e]["dt_bias"][0] for e in range(2)]),
                   jnp.stack([gm[e]["dn_norm_w"][0] for e in range(2)]), dfw[0]], REP_ROWS)
    small_sum, rep_sum = _exchange_small(small_by_chip, rep_g)
    dx, small_sum, rep_sum = lax.optimization_barrier((dx, small_sum, rep_sum))
    dx = pending(dx)

    big = (("ffn_w_gate", "ffn", 0), ("ffn_w_up", "ffn", FS), ("ffn_w_down", "ffn", 2 * FS), ("mix_w_in", "w_in", 0),
           ("mix_w_out", "w_out", 0), ("conv_w_pw1", "pw1", 0), ("conv_w_pw2", "pw2", 0))
    views = {n: (tr, tr) if n in ("ffn_w_gate", "ffn_w_up") else (
        (lambda a: a) if w[n].ndim == 4 else halves, lambda o, n=n: o.reshape(w[n].shape)) for n, _, _ in big}
    partial_sums = {k: r[...] for k, r in sums.items()}
    upper = {}
    for n, key, row0 in big:
        view = views[n][0]
        upper[n] = _adamw_big(f"adamw_{n}", view(w[n]), view(m[n]), view(v[n]), partial_sums[key], row0,
                              first=0 if key in ("pw1", "pw2") else 1)
    upper, partial_sums_0 = lax.optimization_barrier((upper, {k: r[...] for k, r in sums_0.items()}))
    res = {}
    for n, key, row0 in big:
        view, back = views[n]
        outs = upper[n] if key not in partial_sums_0 else _adamw_big(
            f"adamw_{n}_0", view(w[n]), view(m[n]), view(v[n]), partial_sums_0[key], row0, first=0, outs=upper[n])
        res[n] = [back(o) for o in outs]
    outs = _adamw_small("adamw_small", *[_pack([d[n] for n in small_names], SMALL_ROWS) for d in (w, m, v)],
                        small_sum)
    for i, n in enumerate(small_names):
        res[n] = [_unpack(o, small_shapes)[i] for o in outs]
    outs = _adamw_small("adamw_replicated", *[_pack([d[n] for n in rep_names], REP_ROWS) for d in (w, m, v)],
                        rep_sum)
    for i, n in enumerate(rep_names):
        res[n] = [_unpack(o, rep_shapes)[i] for o in outs]

    total = lax.psum(loss[0, 0], ("x", "y", "c"))
    return (total, dx[None], *[res[n][0] for n in order], *[res[n][1] for n in order],
            *[res[n][2] for n in order], *[res[n][3] for n in order])
```

```python
import jax
import jax.numpy as jnp
from jax import lax
from jax.experimental import pallas as pl
from jax.experimental.pallas import tpu as pltpu
from jax.experimental.pallas import tpu_sc as plsc

F32, BF16 = jnp.float32, jnp.bfloat16
MESH = pl.DeviceIdType.MESH
ANY = pl.BlockSpec(memory_space=pl.ANY)

T, D, F = 2048, 1024, 2816
DEPTH = 4
EPS = 1e-6
HEADS, HDIM, KV_HEADS, GROUP = 8, 64, 2, 4
WINDOW = BLOCK = 128
CHUNK = 64
NCHUNK = T // CHUNK
Q_A, KV_A, QKV_B, V_B = 512, 128, 1536, 512
IN_COLS = 2832
IN_SPLITS = (0, 512, 640, 768, 2304, 2816, 2832)
NCHIP, NDEV = 4, 8
FS = F // NCHIP
LR, B1, B2, AEPS, WD, STEP = 0.001, 0.9, 0.999, 1e-08, 0.01, 10
V7X_VMEM_BYTES = 64 * 1024 * 1024
VMEM_LIMIT = V7X_VMEM_BYTES * 7 // 8
LANES = 128


def _cp(*sem):
    return pltpu.CompilerParams(dimension_semantics=sem, vmem_limit_bytes=VMEM_LIMIT)


def _sds(shape, dtype=F32):
    return jax.ShapeDtypeStruct(tuple(shape), dtype)


def _full(shape):
    nd = len(shape)
    return pl.BlockSpec(tuple(shape), lambda *_: (0,) * nd)


def _split_bf16(a):
    hi = a.astype(BF16)
    return hi, (a - hi.astype(F32)).astype(BF16)


def _dg(a, b, ca, cb, hi=False):
    if a.ndim == 3 and b.ndim == 3:
        dims = (((ca + 1,), (cb + 1,)), ((0,), (0,)))
    else:
        dims = (((ca,), (cb,)), ((), ()))
    dot = lambda p, q: lax.dot_general(p, q, dims, preferred_element_type=F32)
    if hi:
        a_hi, a_lo = _split_bf16(a.astype(F32))
        b_hi, b_lo = _split_bf16(b.astype(F32))
        return dot(a_hi, b_hi) + (dot(a_hi, b_lo) + dot(a_lo, b_hi))
    return dot(a.astype(BF16), b.astype(BF16))


def _make_mm(hi):
    @jax.custom_vjp
    def nn(a, b):
        return _dg(a, b, 1, 0, hi)

    @jax.custom_vjp
    def nt(a, b):
        return _dg(a, b, 1, 1, hi)

    @jax.custom_vjp
    def tn(a, b):
        return _dg(a, b, 0, 0, hi)

    nn.defvjp(lambda a, b: (_dg(a, b, 1, 0, hi), (a, b)),
              lambda r, g: (_dg(g, r[1], 1, 1, hi).astype(r[0].dtype), _dg(r[0], g, 0, 0, hi).astype(r[1].dtype)))
    nt.defvjp(lambda a, b: (_dg(a, b, 1, 1, hi), (a, b)),
              lambda r, g: (_dg(g, r[1], 1, 0, hi).astype(r[0].dtype), _dg(g, r[0], 0, 0, hi).astype(r[1].dtype)))
    tn.defvjp(lambda a, b: (_dg(a, b, 0, 0, hi), (a, b)),
              lambda r, g: (_dg(r[1], g, 1, 1, hi).astype(r[0].dtype), _dg(r[0], g, 1, 0, hi).astype(r[1].dtype)))
    return nn, nt, tn


_nn, _nt, _tn = _make_mm(False)
_nn_hi = _make_mm(True)[0]


def _rms(x, w):
    return x * lax.rsqrt(jnp.mean(x * x, axis=-1, keepdims=True) + EPS) * w


def _layernorm(x, w, b):
    xc = x - jnp.mean(x, axis=-1, keepdims=True)
    return xc * lax.rsqrt(jnp.mean(xc * xc, axis=-1, keepdims=True) + EPS) * w + b


def _silu(x):
    return x * jax.nn.sigmoid(x)


def _iota2(shape, dim):
    return lax.broadcasted_iota(jnp.int32, shape, dim)


def _flat_weights(lhs_idx, weights):
    specs, ops, lhs_of, where = [], [], [], []
    for a, (k, w) in enumerate(zip(lhs_idx, weights)):
        for q in range(1 if w.ndim == 2 else w.shape[0]):
            specs.append(_full(w.shape) if w.ndim == 2
                         else pl.BlockSpec((None,) + w.shape[1:], lambda i, q=q: (q, 0, 0)))
            ops.append(w)
            lhs_of.append(k)
            where.append((a, None if w.ndim == 2 else q))
    return specs, ops, lhs_of, where


def _blk_fwd(name, pre, lhs_idx, post, toks, smalls, weights, outs, tm=512):
    wspecs, wops, lhs_of, _ = _flat_weights(lhs_idx, weights)
    nt_, ns, nw = len(toks), len(smalls), len(wops)

    def body(*refs):
        tv = [r[...] for r in refs[:nt_]]
        sv = [r[...] for r in refs[nt_:nt_ + ns]]
        wr = refs[nt_ + ns:nt_ + ns + nw]
        orf = refs[nt_ + ns + nw:]
        lhs = pre(tv, sv)
        ys = [_dg(lhs[i], w[...], 1, 0) for i, w in zip(lhs_of, wr)]
        for o_ref, o in zip(orf, post(ys, tv, sv)):
            o_ref[...] = o.astype(o_ref.dtype)

    in_specs = ([pl.BlockSpec((tm, a.shape[1]), lambda i: (i, 0)) for a in toks]
                + [_full(a.shape) for a in smalls] + wspecs)
    out_specs = [pl.BlockSpec((tm, w_), lambda i: (i, 0)) for w_, _ in outs]
    return pl.pallas_call(
        body, grid=(T // tm,), in_specs=in_specs, out_specs=out_specs,
        out_shape=[_sds((T, w_), dt) for w_, dt in outs], name=name, compiler_params=_cp("parallel"),
    )(*toks, *smalls, *wops)


def _blk_bwd(name, pre, lhs_idx, post, toks, smalls, weights, ct_groups, res=None, linear_post=False, tm=256,
             wchunk=512):
    wspecs, wops, lhs_of, where = _flat_weights(lhs_idx, weights)
    nt_, ns, nw, na = len(toks), len(smalls), len(wops), len(weights)
    cts = [a for g in ct_groups for a in g]
    nc = len(cts)
    widths = [sum(a.shape[1] for a in g) for g in ct_groups]
    has_res = res is not None

    def body(*refs):
        p = 0
        tr = refs[p:p + nt_]; p += nt_
        sr = refs[p:p + ns]; p += ns
        wr = refs[p:p + nw]; p += nw
        cr = refs[p:p + nc]; p += nc
        rr = refs[p:p + has_res]; p += has_res
        dtr = refs[p:p + nt_]; p += nt_
        dsr = refs[p:p + ns]; p += ns
        dwr = refs[p:p + na]; p += na
        scr = refs[p:]
        i = pl.program_id(0)

        @pl.when(i == 0)
        def _():
            for r in list(dsr) + list(dwr):
                r[...] = jnp.zeros_like(r)

        tv = [r[...] for r in tr]
        sv = [r[...] for r in sr]
        ctv, q, si = [], 0, 0
        for g in ct_groups:
            if len(g) == 1:
                ctv.append(cr[q][...].astype(F32))
            else:
                off = 0
                for j, a in enumerate(g):
                    scr[si][:, off:off + a.shape[1]] = cr[q + j][...].astype(F32)
                    off += a.shape[1]
                ctv.append(scr[si][...])
                si += 1
            q += len(g)

        lhs, vjp_pre = jax.vjp(lambda *a: tuple(pre(list(a[:nt_]), list(a[nt_:]))), *tv, *sv)
        lhs_b = [l.astype(BF16) for l in lhs]
        ys = [jnp.zeros((tm, w.shape[1]), F32) if linear_post else _dg(lhs_b[k], w[...], 1, 0)
              for k, w in zip(lhs_of, wr)]
        _, vjp_post = jax.vjp(lambda *a: tuple(post(list(a[:nw]), list(a[nw:nw + nt_]), list(a[nw + nt_:]))),
                              *ys, *tv, *sv)
        gp = vjp_post(tuple(ctv))
        dys, dt_post, ds_post = gp[:nw], gp[nw:nw + nt_], gp[nw + nt_:]
        dlhs = [None] * len(lhs)
        for k, w, dy, (a, q) in zip(lhs_of, wr, dys, where):
            dyb = dy.astype(BF16)
            n = w.shape[1]
            for c0 in range(0, n, wchunk):
                c1 = min(n, c0 + wchunk)
                part = _dg(lhs_b[k], dyb[:, c0:c1], 0, 0)
                if q is None:
                    dwr[a][:, c0:c1] += part
                else:
                    dwr[a][q, :, c0:c1] += part
            d = _dg(dyb, w[...], 1, 1)
            dlhs[k] = d if dlhs[k] is None else dlhs[k] + d
        gq = vjp_pre(tuple(d.astype(l.dtype) for d, l in zip(dlhs, lhs)))
        dt_pre, ds_pre = gq[:nt_], gq[nt_:]
        for j in range(nt_):
            d = dt_post[j] + dt_pre[j]
            if j == 0 and has_res:
                d = d + rr[0][...]
            dtr[j][...] = d
        for j in range(ns):
            dsr[j][...] += ds_post[j] + ds_pre[j]

    tok_spec = lambda a: pl.BlockSpec((tm, a.shape[1]), lambda i: (i, 0))
    in_specs = ([tok_spec(a) for a in toks] + [_full(a.shape) for a in smalls] + wspecs
                + [tok_spec(a) for a in cts] + ([tok_spec(res)] if has_res else []))
    out_specs = [tok_spec(a) for a in toks] + [_full(a.shape) for a in smalls] + [_full(w.shape) for w in weights]
    out_shape = ([_sds(a.shape) for a in toks] + [_sds(a.shape) for a in smalls] + [_sds(w.shape) for w in weights])
    scratch = [pltpu.VMEM((tm, wd), F32) for g, wd in zip(ct_groups, widths) if len(g) > 1]
    outs = pl.pallas_call(
        body, grid=(T // tm,), in_specs=in_specs, out_specs=out_specs, out_shape=out_shape,
        scratch_shapes=scratch, name=name, compiler_params=_cp("arbitrary"),
    )(*toks, *smalls, *wops, *cts, *([res] if has_res else []))
    return outs[:nt_], outs[nt_:nt_ + ns], outs[nt_ + ns:]


def _ffn_fwd(name, x, nw, ffn, idx, tm=1024):
    def body(x_ref, nw_ref, wg_ref, wu_ref, wd_ref, o_ref, g_ref, da_ref, db_ref, h_ref):
        s = pl.program_id(1)

        @pl.when(s == 0)
        def _():
            xv = x_ref[...]
            h_ref[...] = _rms(xv, nw_ref[...]).astype(BF16)
            o_ref[...] = xv

        h = h_ref[...]
        a = _dg(h, wg_ref[...], 1, 1)
        b = _dg(h, wu_ref[...], 1, 1)
        sa = jax.nn.sigmoid(a)
        act = a * sa
        gated = (act * b).astype(BF16)
        g_ref[...] = gated
        da_ref[...] = (b * (sa * (1.0 + a * (1.0 - sa)))).astype(BF16)
        db_ref[...] = act.astype(BF16)
        o_ref[...] += 0.5 * _dg(gated, wd_ref[...], 1, 0)

    wspec = lambda k: pl.BlockSpec((None, None, FS, D), lambda i, s: (s, idx, k, 0))
    act = pl.BlockSpec((None, tm, FS), lambda i, s: (s, i, 0))
    return pl.pallas_call(
        body, grid=(T // tm, NCHIP),
        in_specs=[pl.BlockSpec((tm, D), lambda i, s: (i, 0)), _full((1, D)), wspec(0), wspec(1), wspec(2)],
        out_specs=[pl.BlockSpec((tm, D), lambda i, s: (i, 0)), act, act, act,
                   pl.BlockSpec((tm, D), lambda i, s: (i, 0))],
        out_shape=[_sds((T, D))] + [_sds((NCHIP, T, FS), BF16)] * 3 + [_sds((T, D), BF16)],
        name=name, compiler_params=_cp("parallel", "arbitrary"),
    )(x, nw, ffn, ffn, ffn)


def _ffn_bwd(name, x, nw, ffn, idx, pre, dy, gbuf=None, tm=512):
    ni = T // tm

    def body(x_ref, dy_ref, nw_ref, wg_ref, wu_ref, wd_ref, g_ref, fa_ref, fb_ref, h_ref, dx_ref, dnw_ref, dffn_ref,
             dh_acc, ag, au, ad):
        s, i = pl.program_id(0), pl.program_id(1)
        rows = pl.ds(pl.multiple_of(i * tm, tm), tm)

        @pl.when((s == 0) & (i == 0))
        def _():
            dnw_ref[...] = jnp.zeros_like(dnw_ref)

        @pl.when(i == 0)
        def _():
            ag[...] = jnp.zeros_like(ag)
            au[...] = jnp.zeros_like(au)
            ad[...] = jnp.zeros_like(ad)

        hb = h_ref[...]
        dyb = (0.5 * dy_ref[...]).astype(BF16)
        ad[...] += _dg(g_ref[...], dyb, 0, 0)
        dact = _dg(dyb, wd_ref[...], 1, 1)
        da = (dact * fa_ref[...].astype(F32)).astype(BF16)
        db = (dact * fb_ref[...].astype(F32)).astype(BF16)
        ag[...] += _dg(da, hb, 0, 0)
        au[...] += _dg(db, hb, 0, 0)
        dh = _dg(da, wg_ref[...], 1, 0) + _dg(db, wu_ref[...], 1, 0)

        @pl.when(s == 0)
        def _():
            dh_acc[rows, :] = dh

        @pl.when((s > 0) & (s < NCHIP - 1))
        def _():
            dh_acc[rows, :] += dh

        @pl.when(s == NCHIP - 1)
        def _():
            _, vjp_rms = jax.vjp(_rms, x_ref[...], nw_ref[...])
            dx, dnw = vjp_rms(dh_acc[rows, :] + dh)
            dx_ref[...] = dy_ref[...] + dx
            dnw_ref[...] += dnw

        @pl.when(i == ni - 1)
        def _():
            dffn_ref[0:FS, :] = ag[...].astype(BF16)
            dffn_ref[FS:2 * FS, :] = au[...].astype(BF16)
            dffn_ref[2 * FS:, :] = ad[...].astype(BF16)

    wspec = lambda r, k, blk=0: pl.BlockSpec((None, None, r, D), lambda s, i: (s, blk, k, 0),
                                             pipeline_mode=pl.Buffered(1))
    last = lambda s, i: (jnp.where(s == NCHIP - 1, i, 0), 0)
    nb = 0 if gbuf is None else 1
    act = pl.BlockSpec((None, tm, FS), lambda s, i: (s, i, 0))
    tok = pl.BlockSpec((tm, D), lambda s, i: (i, 0))
    return pl.pallas_call(
        lambda *refs: body(*refs[:10], *refs[10 + nb:]), grid=(NCHIP, ni),
        in_specs=[pl.BlockSpec((tm, D), last), tok, _full((1, D)), wspec(FS, 0), wspec(FS, 1), wspec(FS, 2), act, act,
                  act, tok] + [ANY] * nb,
        out_specs=[pl.BlockSpec((tm, D), last), _full((1, D)), wspec(3 * FS, 0, idx)],
        out_shape=[_sds((T, D)), _sds((1, D)), _sds((NCHIP, 2, 3 * FS, D), BF16)],
        input_output_aliases={10 + k: 2 + k for k in range(nb)},
        scratch_shapes=[pltpu.VMEM((T, D), F32)] + [pltpu.VMEM((FS, D), F32)] * 3,
        name=name, compiler_params=_cp("arbitrary", "arbitrary"),
    )(x, dy, nw, ffn, ffn, ffn, *pre, *(() if gbuf is None else (gbuf,)))


CONV_ROWS = 256


def _conv_pad(k):
    return 8 * ((k - 1 + 7) // 8)


def _shifted(win, o):
    n = win.shape[0]
    return (win if o % n == 0 else pltpu.roll(win, (n - o) % n, 0))[0:CONV_ROWS, :]


def _conv_fwd(name, x, w, b, act):
    k_w, c = w.shape
    tc = 256 if c % 256 == 0 else LANES
    pad = _conv_pad(k_w)
    has_b = b is not None

    def body(*refs):
        x_ref, w_ref = refs[0], refs[1]
        b_ref = refs[2] if has_b else None
        y_ref, xp = refs[2 + has_b], refs[3 + has_b]
        xp[0:pad, :] = jnp.zeros((pad, tc), F32)
        xp[pad:, :] = x_ref[...]

        def step(t, carry):
            base = pl.multiple_of(t * CONV_ROWS, CONV_ROWS)
            win = xp[pl.ds(base, CONV_ROWS + pad), :]
            acc = jnp.zeros((CONV_ROWS, tc), F32)
            for k in range(k_w):
                o = pad - (k_w - 1) + k
                acc = acc + w_ref[k:k + 1, :] * _shifted(win, o)
            if has_b:
                acc = acc + b_ref[...]
            y_ref[pl.ds(base, CONV_ROWS), :] = _silu(acc) if act else acc
            return carry

        lax.fori_loop(0, T // CONV_ROWS, step, 0)

    col = lambda r: pl.BlockSpec((r, tc), lambda j: (0, j))
    ins = [x, w] + ([b] if has_b else [])
    return pl.pallas_call(
        body, grid=(c // tc,), in_specs=[col(T), col(k_w)] + ([col(1)] if has_b else []), out_specs=col(T),
        out_shape=_sds((T, c)), scratch_shapes=[pltpu.VMEM((T + pad, tc), F32)], name=name,
        compiler_params=_cp("parallel"),
    )(*ins)


def _conv_bwd(name, x, w, b, act, dy):
    k_w, c = w.shape
    tc = 256 if c % 256 == 0 else LANES
    pad = _conv_pad(k_w)
    has_b = b is not None

    def body(*refs):
        x_ref, w_ref, dy_ref = refs[0], refs[1], refs[2]
        b_ref = refs[3] if has_b else None
        dx_ref, dw_ref, db_ref, xp, dp = refs[3 + has_b:]
        xp[0:pad, :] = jnp.zeros((pad, tc), F32)
        xp[pad:, :] = x_ref[...]
        dp[T:, :] = jnp.zeros((pad, tc), F32)
        dw_ref[...] = jnp.zeros_like(dw_ref)
        db_ref[...] = jnp.zeros_like(db_ref)

        def step1(t, carry):
            base = pl.multiple_of(t * CONV_ROWS, CONV_ROWS)
            d = dy_ref[pl.ds(base, CONV_ROWS), :]
            win = xp[pl.ds(base, CONV_ROWS + pad), :]
            offs = [pad - (k_w - 1) + k for k in range(k_w)]
            if act:
                acc = jnp.zeros((CONV_ROWS, tc), F32)
                for k, o in enumerate(offs):
                    acc = acc + w_ref[k:k + 1, :] * _shifted(win, o)
                if has_b:
                    acc = acc + b_ref[...]
                sg = jax.nn.sigmoid(acc)
                d = d * (sg * (1.0 + acc * (1.0 - sg)))
            dp[pl.ds(base, CONV_ROWS), :] = d
            for k, o in enumerate(offs):
                dw_ref[k:k + 1, :] += jnp.sum(d * _shifted(win, o), axis=0, keepdims=True)
            db_ref[...] += jnp.sum(d, axis=0, keepdims=True)
            return carry

        lax.fori_loop(0, T // CONV_ROWS, step1, 0)

        def step2(t, carry):
            base = pl.multiple_of(t * CONV_ROWS, CONV_ROWS)
            win = dp[pl.ds(base, CONV_ROWS + pad), :]
            acc = jnp.zeros((CONV_ROWS, tc), F32)
            for k in range(k_w):
                o = (k_w - 1) - k
                acc = acc + w_ref[k:k + 1, :] * _shifted(win, o)
            dx_ref[pl.ds(base, CONV_ROWS), :] = acc
            return carry

        lax.fori_loop(0, T // CONV_ROWS, step2, 0)

    col = lambda r: pl.BlockSpec((r, tc), lambda j: (0, j))
    ins = [x, w, dy] + ([b] if has_b else [])
    return pl.pallas_call(
        body, grid=(c // tc,), in_specs=[col(T), col(k_w), col(T)] + ([col(1)] if has_b else []),
        out_specs=[col(T), col(k_w), col(1)], out_shape=[_sds((T, c)), _sds((k_w, c)), _sds((1, c))],
        scratch_shapes=[pltpu.VMEM((T + pad, tc), F32), pltpu.VMEM((T + pad, tc), F32)], name=name,
        compiler_params=_cp("parallel"),
    )(*ins)


def _attn_consts(n):
    i = _iota2((BLOCK, 2 * BLOCK), 0)
    j = _iota2((BLOCK, 2 * BLOCK), 1)
    dist = i + BLOCK - j
    valid = (dist >= 0) & (dist < WINDOW) & ((n > 0) | (j >= BLOCK))
    return dist.astype(F32), valid


def _attn_block(q4, kk, vv, sinks, dist, valid, kv):
    outs = []
    lane = _iota2((1, HEADS), 1)
    for g in range(GROUP):
        h = kv * GROUP + g
        slope = 2.0 ** (-8.0 * (h + 1) / HEADS)
        s = _nt(q4[:, g * HDIM:(g + 1) * HDIM], kk) * (HDIM ** -0.5)
        s = jnp.where(valid, s - slope * dist, -1e30)
        sink = jnp.sum(jnp.where(lane == h, sinks, 0.0), axis=1, keepdims=True)
        m = jnp.maximum(jnp.max(s, axis=-1, keepdims=True), sink)
        e = jnp.exp(s - m)
        p = e / (jnp.sum(e, axis=-1, keepdims=True) + jnp.exp(sink - m))
        outs.append(_nn(p, vv))
    return tuple(outs)


def _attn_fwd(name, qa, ka, va, sinks):
    def body(q_ref, k_ref, v_ref, s_ref, o_ref, kp, vp):
        kp[0:BLOCK, :] = jnp.zeros((BLOCK, KV_A), F32)
        vp[0:BLOCK, :] = jnp.zeros((BLOCK, KV_A), F32)
        kp[BLOCK:, :] = k_ref[...]
        vp[BLOCK:, :] = v_ref[...]
        sinks_v = s_ref[...]

        def step(n, carry):
            r = pl.multiple_of(n * BLOCK, BLOCK)
            dist, valid = _attn_consts(n)
            k2 = kp[pl.ds(r, 2 * BLOCK), :]
            v2 = vp[pl.ds(r, 2 * BLOCK), :]
            for kv in range(KV_HEADS):
                q4 = q_ref[pl.ds(r, BLOCK), kv * GROUP * HDIM:(kv + 1) * GROUP * HDIM]
                og = _attn_block(q4, k2[:, kv * HDIM:(kv + 1) * HDIM], v2[:, kv * HDIM:(kv + 1) * HDIM], sinks_v,
                                 dist, valid, kv)
                for g in range(GROUP):
                    h = kv * GROUP + g
                    o_ref[pl.ds(r, BLOCK), h * HDIM:(h + 1) * HDIM] = og[g]
            return carry

        lax.fori_loop(0, T // BLOCK, step, 0)

    return pl.pallas_call(
        body, out_shape=_sds((T, Q_A)),
        scratch_shapes=[pltpu.VMEM((T + BLOCK, KV_A), F32), pltpu.VMEM((T + BLOCK, KV_A), F32)], name=name,
        compiler_params=pltpu.CompilerParams(vmem_limit_bytes=VMEM_LIMIT),
    )(qa, ka, va, sinks)


def _attn_bwd(name, qa, ka, va, sinks, do):
    def body(q_ref, k_ref, v_ref, s_ref, do_ref, dq_ref, dk_ref, dv_ref, ds_ref, kp, vp, dkp, dvp):
        kp[0:BLOCK, :] = jnp.zeros((BLOCK, KV_A), F32)
        vp[0:BLOCK, :] = jnp.zeros((BLOCK, KV_A), F32)
        kp[BLOCK:, :] = k_ref[...]
        vp[BLOCK:, :] = v_ref[...]
        dkp[...] = jnp.zeros_like(dkp)
        dvp[...] = jnp.zeros_like(dvp)
        ds_ref[...] = jnp.zeros_like(ds_ref)
        sinks_v = s_ref[...]

        def step(n, carry):
            r = pl.multiple_of(n * BLOCK, BLOCK)
            dist, valid = _attn_consts(n)
            k2 = kp[pl.ds(r, 2 * BLOCK), :]
            v2 = vp[pl.ds(r, 2 * BLOCK), :]
            for kv in range(KV_HEADS):
                cols = slice(kv * HDIM, (kv + 1) * HDIM)
                q4 = q_ref[pl.ds(r, BLOCK), kv * GROUP * HDIM:(kv + 1) * GROUP * HDIM]
                _, vjp = jax.vjp(lambda q, k, v, s: _attn_block(q, k, v, s, dist, valid, kv),
                                 q4, k2[:, cols], v2[:, cols], sinks_v)
                cts = tuple(do_ref[pl.ds(r, BLOCK), (kv * GROUP + g) * HDIM:(kv * GROUP + g + 1) * HDIM]
                            for g in range(GROUP))
                dq4, dkk, dvv, dsk = vjp(cts)
                dq_ref[pl.ds(r, BLOCK), kv * GROUP * HDIM:(kv + 1) * GROUP * HDIM] = dq4
                dkp[pl.ds(r, 2 * BLOCK), cols] += dkk
                dvp[pl.ds(r, 2 * BLOCK), cols] += dvv
                ds_ref[...] += dsk
            return carry

        lax.fori_loop(0, T // BLOCK, step, 0)
        dk_ref[...] = dkp[BLOCK:, :]
        dv_ref[...] = dvp[BLOCK:, :]

    pad = lambda: pltpu.VMEM((T + BLOCK, KV_A), F32)
    return pl.pallas_call(
        body, out_shape=[_sds((T, Q_A)), _sds((T, KV_A)), _sds((T, KV_A)), _sds((1, HEADS))],
        scratch_shapes=[pad(), pad(), pad(), pad()], name=name,
        compiler_params=pltpu.CompilerParams(vmem_limit_bytes=VMEM_LIMIT),
    )(qa, ka, va, sinks, do)


def _dn_consts():
    i = _iota2((CHUNK, CHUNK), 0)
    j = _iota2((CHUNK, CHUNK), 1)
    return dict(causal=i >= j, strict=i > j, ltri=(i >= j).astype(F32),
                last=(_iota2((CHUNK, 1), 0) == CHUNK - 1).astype(F32))


def _l2norm(x):
    return x * lax.rsqrt(jnp.sum(x * x, axis=-1, keepdims=True) + EPS)


def _head_cols(m):
    lane = _iota2((1, HEADS), 1)
    return jnp.concatenate([jnp.sum(jnp.where(lane == h, m, 0.0), axis=1, keepdims=True)[None]
                            for h in range(HEADS)], axis=0)


@jax.custom_vjp
def _unit_lower_inverse(low, known):
    if known is not None:
        return known
    inv = (_iota2((CHUNK, CHUNK), 0) == _iota2((CHUNK, CHUNK), 1)).astype(F32) - low
    pw = low
    for _ in range(5):
        pw = _dg(pw, pw, 1, 0, True)
        inv = inv + _dg(inv, pw, 1, 0, True)
    return inv


def _unit_lower_inverse_fwd(low, known):
    inv = _unit_lower_inverse(low, known)
    return inv, (inv, known)


def _unit_lower_inverse_bwd(res, g):
    inv, known = res
    d_low = -_dg(inv, _dg(g, inv, 1, 1, True), 0, 0, True)
    return d_low, (None if known is None else jnp.zeros_like(known))


_unit_lower_inverse.defvjp(_unit_lower_inverse_fwd, _unit_lower_inverse_bwd)


def _dn_local(q3, k3, v3, braw, araw, alog, dtb, cs, known_inv=None):
    q = _l2norm(q3) * (HDIM ** -0.5)
    k = _l2norm(k3)
    g = -jnp.exp(alog) * jax.nn.softplus(araw + dtb)
    gc_all = _nn_hi(cs["ltri"], g)
    egc_all = jnp.exp(gc_all)
    beta, gc, egc = _head_cols(jax.nn.sigmoid(braw)), _head_cols(gc_all), _head_cols(egc_all)
    a = jnp.broadcast_to(gc, (HEADS, CHUNK, CHUNK))
    diff = a - jnp.swapaxes(a, 1, 2)
    decay = jnp.where(cs["causal"], jnp.exp(jnp.where(cs["causal"], diff, 0.0)), 0.0)
    kb = k * beta
    low = jnp.where(cs["strict"], _nt(kb, k) * decay, 0.0)
    inv = _unit_lower_inverse(low, known_inv)
    u = _nn_hi(inv, v3 * beta)
    w = _nn_hi(inv, kb * egc)
    attn = _nt(q, k) * decay
    gc_last = jnp.sum(gc * cs["last"], axis=1, keepdims=True)
    return u, w, attn, q * egc, k * jnp.exp(gc_last - gc), egc_all, inv


def _heads3(ref, off=0):
    return jnp.concatenate([ref[:, off + h * HDIM:off + (h + 1) * HDIM][None] for h in range(HEADS)], axis=0)


def _dn_local_fwd(name, qkv, ba, alog, dtb):
    def body(qkv_ref, ba_ref, al_ref, dt_ref, u_ref, w_ref, at_ref, qd_ref, kd_ref, eg_ref, inv_ref):
        bav = ba_ref[...]
        outs = _dn_local(_heads3(qkv_ref), _heads3(qkv_ref, 512), _heads3(qkv_ref, 1024), bav[:, :HEADS],
                         bav[:, HEADS:], al_ref[...], dt_ref[...], _dn_consts())
        for r, o in zip((u_ref, w_ref, at_ref, qd_ref, kd_ref, inv_ref), outs[:5] + outs[6:]):
            _unheads(r, o)
        eg_ref[...] = outs[5]

    row = lambda w_: pl.BlockSpec((CHUNK, w_), lambda n: (n, 0))
    return pl.pallas_call(
        body, grid=(NCHUNK,), in_specs=[row(QKV_B), row(2 * HEADS), _full((1, HEADS)), _full((1, HEADS))],
        out_specs=[row(V_B)] * 5 + [row(HEADS), row(V_B)],
        out_shape=[_sds((T, V_B))] * 5 + [_sds((T, HEADS)), _sds((T, V_B))], name=name,
        compiler_params=_cp("parallel"),
    )(qkv, ba, alog, dtb)


def _dn_local_bwd(name, qkv, ba, alog, dtb, inv, cts):
    def body(qkv_ref, ba_ref, al_ref, dt_ref, inv_ref, du_ref, dw_ref, dat_ref, dqd_ref, dkd_ref, deg_ref,
             dqkv_ref, dba_ref, dal_ref, ddt_ref):
        @pl.when(pl.program_id(0) == 0)
        def _():
            dal_ref[...] = jnp.zeros_like(dal_ref)
            ddt_ref[...] = jnp.zeros_like(ddt_ref)

        cs = _dn_consts()
        bav = ba_ref[...]
        known = _heads3(inv_ref)
        _, vjp = jax.vjp(lambda *a: _dn_local(*a, cs, known)[:6], _heads3(qkv_ref), _heads3(qkv_ref, 512),
                         _heads3(qkv_ref, 1024), bav[:, :HEADS], bav[:, HEADS:], al_ref[...], dt_ref[...])
        dq, dk, dv, dbr, dar, dal, ddt = vjp((_heads3(du_ref), _heads3(dw_ref), _heads3(dat_ref), _heads3(dqd_ref),
                                              _heads3(dkd_ref), deg_ref[...]))
        for h in range(HEADS):
            dqkv_ref[:, h * HDIM:(h + 1) * HDIM] = dq[h]
            dqkv_ref[:, 512 + h * HDIM:512 + (h + 1) * HDIM] = dk[h]
            dqkv_ref[:, 1024 + h * HDIM:1024 + (h + 1) * HDIM] = dv[h]
        dba_ref[:, :HEADS] = dbr
        dba_ref[:, HEADS:] = dar
        dal_ref[...] += dal
        ddt_ref[...] += ddt

    row = lambda w_: pl.BlockSpec((CHUNK, w_), lambda n: (n, 0))
    return pl.pallas_call(
        body, grid=(NCHUNK,),
        in_specs=[row(QKV_B), row(2 * HEADS), _full((1, HEADS)), _full((1, HEADS))] + [row(V_B)] * 6 + [row(HEADS)],
        out_specs=[row(QKV_B), row(2 * HEADS), _full((1, HEADS)), _full((1, HEADS))],
        out_shape=[_sds((T, QKV_B)), _sds((T, 2 * HEADS)), _sds((1, HEADS)), _sds((1, HEADS))], name=name,
        compiler_params=_cp("arbitrary"),
    )(qkv, ba, alog, dtb, inv, *cts)


def _dn_step(s, u, w, attn, qd, kd, egc, z, nw):
    last = (_iota2((CHUNK, 1), 0) == CHUNK - 1).astype(F32)
    gl = jnp.sum(_head_cols(egc) * last, axis=1, keepdims=True)
    v_new = u - _nn(w, s)
    o = _nn(qd, s) + _nn(attn, v_new)
    s_new = s * gl + _tn(kd, v_new)
    return s_new, _rms(o, nw) * _silu(z)


def _unheads(ref, v3):
    for h in range(HEADS):
        ref[:, h * HDIM:(h + 1) * HDIM] = v3[h]


def _dn_rec_fwd(name, u, w, attn, qd, kd, egc, z, nw):
    def body(u_ref, w_ref, at_ref, qd_ref, kd_ref, eg_ref, z_ref, nw_ref, o_ref, ss_ref, s_scr):
        @pl.when(pl.program_id(0) == 0)
        def _():
            s_scr[...] = jnp.zeros_like(s_scr)

        s = s_scr[...]
        ss_ref[...] = s
        s_new, on = _dn_step(s, _heads3(u_ref), _heads3(w_ref), _heads3(at_ref), _heads3(qd_ref), _heads3(kd_ref),
                             eg_ref[...], _heads3(z_ref), nw_ref[...])
        s_scr[...] = s_new
        _unheads(o_ref, on)

    row = lambda w_: pl.BlockSpec((CHUNK, w_), lambda n: (n, 0))
    return pl.pallas_call(
        body, grid=(NCHUNK,), in_specs=[row(V_B)] * 5 + [row(HEADS), row(V_B), _full((1, HDIM))],
        out_specs=[row(V_B), pl.BlockSpec((None, HEADS, HDIM, HDIM), lambda n: (n, 0, 0, 0))],
        out_shape=[_sds((T, V_B)), _sds((NCHUNK, HEADS, HDIM, HDIM))],
        scratch_shapes=[pltpu.VMEM((HEADS, HDIM, HDIM), F32)], name=name, compiler_params=_cp("arbitrary"),
    )(u, w, attn, qd, kd, egc, z, nw)


def _dn_rec_bwd(name, u, w, attn, qd, kd, egc, z, nw, ss, do):
    def body(u_ref, w_ref, at_ref, qd_ref, kd_ref, eg_ref, z_ref, nw_ref, ss_ref, do_ref,
             du_ref, dw_ref, dat_ref, dqd_ref, dkd_ref, deg_ref, dz_ref, dnw_ref, ds_scr):
        @pl.when(pl.program_id(0) == 0)
        def _():
            ds_scr[...] = jnp.zeros_like(ds_scr)
            dnw_ref[...] = jnp.zeros_like(dnw_ref)

        _, vjp = jax.vjp(_dn_step, ss_ref[...], _heads3(u_ref), _heads3(w_ref), _heads3(at_ref), _heads3(qd_ref),
                         _heads3(kd_ref), eg_ref[...], _heads3(z_ref), nw_ref[...])
        ds, du, dw, dat, dqd, dkd, deg, dz, dnw = vjp((ds_scr[...], _heads3(do_ref)))
        ds_scr[...] = ds
        for r, v in zip((du_ref, dw_ref, dat_ref, dqd_ref, dkd_ref, dz_ref), (du, dw, dat, dqd, dkd, dz)):
            _unheads(r, v)
        deg_ref[...] = deg
        dnw_ref[...] += dnw

    row = lambda w_: pl.BlockSpec((CHUNK, w_), lambda n: (NCHUNK - 1 - n, 0))
    return pl.pallas_call(
        body, grid=(NCHUNK,),
        in_specs=[row(V_B)] * 5 + [row(HEADS), row(V_B), _full((1, HDIM)),
                                   pl.BlockSpec((None, HEADS, HDIM, HDIM), lambda n: (NCHUNK - 1 - n, 0, 0, 0)),
                                   row(V_B)],
        out_specs=[row(V_B)] * 5 + [row(HEADS), row(V_B), _full((1, HDIM))],
        out_shape=[_sds((T, V_B))] * 5 + [_sds((T, HEADS)), _sds((T, V_B)), _sds((1, HDIM))],
        scratch_shapes=[pltpu.VMEM((HEADS, HDIM, HDIM), F32)], name=name, compiler_params=_cp("arbitrary"),
    )(u, w, attn, qd, kd, egc, z, nw, ss, do)


def _final(name, x, fw, target, tm=512):
    def body(x_ref, fw_ref, t_ref, l_ref, dx_ref, dfw_ref):
        @pl.when(pl.program_id(0) == 0)
        def _():
            l_ref[...] = jnp.zeros_like(l_ref)
            dfw_ref[...] = jnp.zeros_like(dfw_ref)

        tv = t_ref[...]

        def f(xv, fwv):
            err = _rms(xv, fwv) - tv
            per_tok = jnp.mean(err * err, axis=-1, keepdims=True)
            return 0.5 * jnp.sum(per_tok, axis=0, keepdims=True)

        loss, vjp = jax.vjp(f, x_ref[...], fw_ref[...])
        dx, dfw = vjp(jnp.ones((1, 1), F32))
        l_ref[...] += loss
        dx_ref[...] = dx
        dfw_ref[...] += dfw

    tok = pl.BlockSpec((tm, D), lambda i: (i, 0))
    return pl.pallas_call(
        body, grid=(T // tm,), in_specs=[tok, _full((1, D)), tok], out_specs=[_full((1, 1)), tok, _full((1, D))],
        out_shape=[_sds((1, 1)), _sds((T, D)), _sds((1, D))], name=name, compiler_params=_cp("arbitrary"),
    )(x, fw, target)


def _m1_pre(tv, sv):
    return [_rms(tv[0], sv[0])]


def _m1_post(ys, tv, sv):
    return (jnp.concatenate(ys, axis=1),)


def _m1_post_split(ys, tv, sv):
    proj = jnp.concatenate(ys, axis=1)
    return tuple(proj[:, a:b] for a, b in zip(IN_SPLITS[:-1], IN_SPLITS[1:]))


def _m5_pre(tv, sv):
    return [tv[1], tv[2]]


def _m5_post(ys, tv, sv):
    return (tv[0] + ys[0] + ys[1],)


def _c1_pre(tv, sv):
    return [_rms(tv[0], sv[0])]


def _c1_post(ys, tv, sv):
    return ((jnp.concatenate(ys[:2], axis=1) + sv[1]) * jax.nn.sigmoid(jnp.concatenate(ys[2:], axis=1) + sv[2]),)


def _c3_pre(tv, sv):
    return [_silu(_layernorm(tv[0], sv[0], sv[1]))]


def _c3_post(ys, tv, sv):
    return (tv[1] + ys[0] + sv[2],)


def _row(v):
    return v.reshape(1, -1)


def _mixer_fwd(tag, x, p):
    parts = _blk_fwd(f"m1_fwd_{tag}", _m1_pre, [0], _m1_post_split, [x], [p["nw"]], [p["w_in"]],
                     [(b - a, F32) for a, b in zip(IN_SPLITS[:-1], IN_SPLITS[1:])])
    qa, ka, va, qkvb, z, ba = parts
    att = _attn_fwd(f"attn_fwd_{tag}", qa, ka, va, p["sinks"])
    qkvc = _conv_fwd(f"dnconv_fwd_{tag}", qkvb, p["dn_conv_w"], None, True)
    *loc, inv = _dn_local_fwd(f"dnloc_fwd_{tag}", qkvc, ba, p["a_log"], p["dt_bias"])
    og, ss = _dn_rec_fwd(f"dnrec_fwd_{tag}", *loc, z, p["dn_norm_w"])
    (out,) = _blk_fwd(f"m5_fwd_{tag}", _m5_pre, [0, 1], _m5_post, [x, att, og], [], [p["wo_a"], p["wo_b"]],
                      [(D, F32)])
    return out, dict(x=x, qa=qa, ka=ka, va=va, qkvb=qkvb, z=z, ba=ba, att=att, qkvc=qkvc, loc=loc, inv=inv, og=og,
                     ss=ss)


def _mixer_bwd(tag, dy, p, s):
    (dxa, datt, dog), _, (dwo_a, dwo_b) = _blk_bwd(f"m5_bwd_{tag}", _m5_pre, [0, 1], _m5_post,
                                                   [s["x"], s["att"], s["og"]], [], [p["wo_a"], p["wo_b"]], [[dy]],
                                                   linear_post=True)
    rec = _dn_rec_bwd(f"dnrec_bwd_{tag}", *s["loc"], s["z"], p["dn_norm_w"], s["ss"], dog)
    dz, dnw_dn = rec[6], rec[7]
    dqkvc, dba, dalog, ddtb = _dn_local_bwd(f"dnloc_bwd_{tag}", s["qkvc"], s["ba"], p["a_log"], p["dt_bias"],
                                            s["inv"], rec[:6])
    dqkvb, dconvw, _ = _conv_bwd(f"dnconv_bwd_{tag}", s["qkvb"], p["dn_conv_w"], None, True, dqkvc)
    dqa, dka, dva, dsinks = _attn_bwd(f"attn_bwd_{tag}", s["qa"], s["ka"], s["va"], p["sinks"], datt)
    (dx,), (dnw,), (dw_in,) = _blk_bwd(f"m1_bwd_{tag}", _m1_pre, [0], _m1_post, [s["x"]], [p["nw"]], [p["w_in"]],
                                       [[dqa, dka, dva, dqkvb, dz, dba]], res=dxa, linear_post=True)
    return dx, dict(nw=dnw, w_in=dw_in, wo_a=dwo_a, wo_b=dwo_b, dn_conv_w=dconvw, sinks=dsinks, a_log=dalog,
                    dt_bias=ddtb, dn_norm_w=dnw_dn)


def _conformer_fwd(tag, x, p):
    (glu,) = _blk_fwd(f"c1_fwd_{tag}", _c1_pre, [0], _c1_post, [x], [p["nw"], p["b1a"], p["b1b"]], [p["w1"]],
                      [(D, F32)])
    cc = _conv_fwd(f"dwconv_fwd_{tag}", glu, p["w_dw"], p["b_dw"], False)
    (out,) = _blk_fwd(f"c3_fwd_{tag}", _c3_pre, [0], _c3_post, [cc, x], [p["ln_w"], p["ln_b"], p["b2"]], [p["w2"]],
                      [(D, F32)])
    return out, dict(x=x, glu=glu, cc=cc)


def _conformer_bwd(tag, dy, p, s):
    (dcc, dxa), (dlnw, dlnb, db2), (dw2,) = _blk_bwd(f"c3_bwd_{tag}", _c3_pre, [0], _c3_post, [s["cc"], s["x"]],
                                                     [p["ln_w"], p["ln_b"], p["b2"]], [p["w2"]], [[dy]],
                                                     linear_post=True)
    dglu, dwdw, dbdw = _conv_bwd(f"dwconv_bwd_{tag}", s["glu"], p["w_dw"], p["b_dw"], False, dcc)
    (dx,), (dnw, db1a, db1b), (dw1,) = _blk_bwd(f"c1_bwd_{tag}", _c1_pre, [0], _c1_post, [s["x"]],
                                                [p["nw"], p["b1a"], p["b1b"]], [p["w1"]], [[dglu]], res=dxa)
    return dx, dict(nw=dnw, b1a=db1a, b1b=db1b, w1=dw1, w_dw=dwdw, b_dw=dbdw, ln_w=dlnw, ln_b=dlnb, b2=db2, w2=dw2)


def _layer_fwd(l, x, nw, ffn_a, get_ffn_b, p):
    x1, *pre_a = _ffn_fwd(f"ffn_fwd_{l}a", x, _row(nw[0]), ffn_a, 0)
    p = dict(p, nw=_row(nw[1]))
    x2, sv = (_mixer_fwd if l % 2 == 0 else _conformer_fwd)(str(l), x1, p)
    x2, ffn_b = get_ffn_b(x2)
    out, *pre_b = _ffn_fwd(f"ffn_fwd_{l}b", x2, _row(nw[2]), ffn_b, 0)
    return out, (x, x2, p, sv, pre_a, pre_b, ffn_a, ffn_b)


def _layer_bwd(l, dx, nw, saved, after_first=lambda dx: dx):
    x0, x2, p, sv, pre_a, pre_b, ffn_a, ffn_b = saved
    dx, dn2, dffn = _ffn_bwd(f"ffn_bwd_{l}b", x2, _row(nw[2]), ffn_b, 1, pre_b, dx)
    dx = after_first(dx)
    dx, dmix = (_mixer_bwd if l % 2 == 0 else _conformer_bwd)(str(l), dx, p, sv)
    dx, dn0, dffn = _ffn_bwd(f"ffn_bwd_{l}a", x0, _row(nw[0]), ffn_a, 0, pre_a, dx, dffn)
    return dx, jnp.concatenate([dn0, dmix.pop("nw"), dn2], axis=0), dffn, dmix


def _place(staggered=False):
    x, y, c = lax.axis_index("x"), lax.axis_index("y"), lax.axis_index("c")
    s = c if staggered else 0
    first, second = (x + (1 - s) * (1 - 2 * x), y + s * (1 - 2 * y)), (x + s * (1 - 2 * x), y + (1 - s) * (1 - 2 * y))
    chips = [first, second, (1 - x, 1 - y)]
    return x, y, c, 2 * x + y, chips, [2 * px + py for px, py in chips]


def _handshake(peers):
    barrier = pltpu.get_barrier_semaphore()
    for p in peers:
        pl.semaphore_signal(barrier, inc=1, device_id=p, device_id_type=MESH)
    pl.semaphore_wait(barrier, len(peers))


def _chip_peers():
    x, y, c, _, chips, _ = _place()
    return [(*chip, c) for chip in chips] + [(x, y, 1 - c)]


def _gather_copies(ins, outs, nb, send, recv, fsend, frecv, lsem):
    n_in = len(ins)
    x, y, c, me, chips, cidx = _place(staggered=True)
    sib = (x, y, 1 - c)
    local = [pltpu.make_async_copy(ins[a], outs[a].at[me], lsem.at[a]) for a in range(nb, n_in)]

    def region(a, k, who):
        if k < 2:
            return outs[a].at[cidx[k], pl.ds(who, 1)]
        r = ins[a].shape[1] // 2
        return outs[a].at[cidx[2], pl.ds(who, 1), pl.ds((k - 2) * r, r)]

    def hop(a, k):
        if k < 2:
            src, dst = ins[a].at[pl.ds(c, 1)], outs[a].at[me, pl.ds(c, 1)]
        else:
            r = ins[a].shape[1] // 2
            src = dst = outs[a].at[cidx[3 - k], pl.ds(c, 1), pl.ds((k - 2) * r, r)]
        return pltpu.make_async_remote_copy(src, dst, send.at[4 * a + k], recv.at[4 * a + k],
                                            device_id=(*chips[k % 2], c), device_id_type=MESH)

    def landed(a, k):
        dst = region(a, k, c)
        return pltpu.make_async_remote_copy(dst, dst, send.at[4 * a + k], recv.at[4 * a + k],
                                            device_id=(*chips[k % 2], c), device_id_type=MESH)

    def passed(a, k, who):
        part = region(a, k, who)
        return pltpu.make_async_remote_copy(part, part, fsend.at[4 * a + k], frecv.at[4 * a + k], device_id=sib,
                                            device_id_type=MESH)

    def direct(a, j):
        k = 4 * nb + 3 * (a - nb) + j
        return pltpu.make_async_remote_copy(ins[a], outs[a].at[me], send.at[k], recv.at[k],
                                            device_id=(*chips[j], c), device_id_type=MESH)

    def direct_landed(a, j):
        k = 4 * nb + 3 * (a - nb) + j
        dst = outs[a].at[cidx[j]]
        return pltpu.make_async_remote_copy(dst, dst, send.at[k], recv.at[k], device_id=(*chips[j], c),
                                            device_id_type=MESH)

    sends = [hop(a, k) for a in range(nb) for k in range(2)] + [direct(a, j) for a in range(nb, n_in) for j in range(3)]
    for cp in sends:
        cp.start()
    for cp in local:
        cp.start()
    for a in range(nb):
        for k in (1, 0):
            landed(a, k).wait_recv()
            for cp in (hop(a, 3 - k), passed(a, k, c)):
                cp.start()
                sends.append(cp)
    for a in range(nb):
        for k in (2, 3):
            landed(a, k).wait_recv()
            cp = passed(a, k, c)
            cp.start()
            sends.append(cp)
    for a in range(nb, n_in):
        for j in range(3):
            direct_landed(a, j).wait_recv()
    for a in range(nb):
        for k in range(4):
            passed(a, k, 1 - c).wait_recv()
    for cp in sends:
        cp.wait_send()
    for cp in local:
        cp.wait()


def _gather_sems(n_in, nb):
    dma = pltpu.SemaphoreType.DMA
    n_ici = 4 * nb + 3 * (n_in - nb)
    return [dma((n_ici,)), dma((n_ici,)), dma((4 * nb,)), dma((4 * nb,)), dma((n_in,))]


def _gather_async(name, halved, whole=()):
    nb, arrs = len(halved), list(halved) + list(whole)
    hbm = pltpu.MemorySpace.HBM
    ins = [jax.new_ref(a, memory_space=hbm) for a in arrs]
    outs = [jax.empty_ref(_sds((NCHIP,) + a.shape, a.dtype), memory_space=hbm) for a in arrs]

    @pl.kernel(mesh=plsc.ScalarSubcoreMesh(axis_name="seq", num_cores=1), name=name,
               scratch_types=tuple(_gather_sems(len(arrs), nb)),
               compiler_params=pltpu.CompilerParams(collective_id=2))
    def launch(send, recv, fsend, frecv, lsem):
        _handshake(_chip_peers())
        _gather_copies(ins, outs, nb, send, recv, fsend, frecv, lsem)

    launch()
    return outs


def _swap_halves(name, grads, after=None):
    n = len(grads)
    hbm = pltpu.MemorySpace.HBM
    ins = [jax.new_ref(g, memory_space=hbm) for g in grads]
    outs = [jax.empty_ref(_sds((NCHIP, g.shape[1] // 2) + g.shape[2:], g.dtype), memory_space=hbm) for g in grads]
    tile = (2 * 8, LANES)
    token = None if after is None else jax.empty_ref(_sds(tile, BF16), memory_space=hbm)

    @pl.kernel(mesh=plsc.ScalarSubcoreMesh(axis_name="seq", num_cores=1), name=name,
               scratch_types=(pltpu.SemaphoreType.DMA((n + 1,)), pltpu.SemaphoreType.DMA((n,))),
               compiler_params=pltpu.CompilerParams(collective_id=1))
    def launch(send, recv):
        x, y, c, _, _, _ = _place()
        sib = (x, y, 1 - c)
        _handshake([sib])
        if after is not None:
            tick = pltpu.make_async_copy(after.at[0, 0, 0, pl.ds(0, tile[0]), pl.ds(0, tile[1])], token, send.at[n])
            tick.start()
            tick.wait()
        cps = []
        for a in range(n):
            h = grads[a].shape[1] // 2
            cps.append(pltpu.make_async_remote_copy(ins[a].at[:, pl.ds((1 - c) * h, h)], outs[a], send.at[a],
                                                    recv.at[a], device_id=sib, device_id_type=MESH))
        for cp in cps:
            cp.start()
        for cp in cps:
            cp.wait()

    launch()
    return outs


def _row_tile(r, cap=256):
    return max(t for t in range(8, cap + 1, 8) if r % t == 0)


def _add_half(name, g, r, c_arr):
    _, l, rows, cols = g.shape
    h = l // 2
    tr = _row_tile(rows, 1056)

    def body(c_ref, g_ref, r_ref, o_ref):
        o_ref[...] = (g_ref[...].astype(F32) + r_ref[...].astype(F32)).astype(BF16)

    blk = (None, None, tr, cols)
    return pl.pallas_call(
        body,
        grid_spec=pltpu.PrefetchScalarGridSpec(
            num_scalar_prefetch=1, grid=(NCHIP, h, rows // tr),
            in_specs=[pl.BlockSpec(blk, lambda j, i, t, c_ref: (j, c_ref[0] * h + i, t, 0)),
                      pl.BlockSpec(blk, lambda j, i, t, c_ref: (j, i, t, 0))],
            out_specs=pl.BlockSpec(blk, lambda j, i, t, c_ref: (j, i, t, 0))),
        out_shape=_sds((NCHIP, h, rows, cols), BF16), name=name,
        compiler_params=_cp("parallel", "parallel", "parallel"),
    )(c_arr, g, r)


def _scatter_async(name, parts, sums, where):
    nb = len(parts)
    ins = [jax.new_ref(p, memory_space=pltpu.MemorySpace.HBM) for p in parts]
    dma = pltpu.SemaphoreType.DMA

    @pl.kernel(mesh=plsc.ScalarSubcoreMesh(axis_name="seq", num_cores=1), name=name,
               scratch_types=(dma((3 * nb,)), dma((3 * nb,)), dma((4 * nb,)), dma((4 * nb,)), dma((nb,))),
               compiler_params=pltpu.CompilerParams(collective_id=3))
    def launch(send, recv, fsend, frecv, lsem):
        _handshake(_chip_peers())
        x, y, c, me, chips, cidx = _place(staggered=True)
        sib = (x, y, 1 - c)

        def slot(a, half, chip):
            return sums[a].at[half, chip, pl.ds(where[a], 1)]

        local = [pltpu.make_async_copy(ins[a].at[me], slot(a, c, me), lsem.at[a]) for a in range(nb)]
        for cp in local:
            cp.start()

        def ici(a, j):
            return pltpu.make_async_remote_copy(ins[a].at[cidx[j]], slot(a, c, me), send.at[a * 3 + j],
                                                recv.at[a * 3 + j], device_id=(*chips[j], c), device_id_type=MESH)

        def landed(a, j):
            dst = slot(a, c, cidx[j])
            return pltpu.make_async_remote_copy(dst, dst, send.at[a * 3 + j], recv.at[a * 3 + j],
                                                device_id=(*chips[j], c), device_id_type=MESH)

        def passed(a, j, who):
            dst = slot(a, who, me if j == 3 else cidx[j])
            src = ins[a].at[me] if j == 3 else dst
            return pltpu.make_async_remote_copy(src, dst, fsend.at[a * 4 + j], frecv.at[a * 4 + j], device_id=sib,
                                                device_id_type=MESH)

        sends = [ici(a, j) for a in range(nb) for j in range(3)] + [passed(a, 3, c) for a in range(nb)]
        for cp in sends:
            cp.start()
        for a in range(nb):
            for j in range(3):
                landed(a, j).wait_recv()
                cp = passed(a, j, c)
                cp.start()
                sends.append(cp)
        for a in range(nb):
            for j in range(4):
                passed(a, j, 1 - c).wait_recv()
        for cp in sends:
            cp.wait_send()
        for cp in local:
            cp.wait()

    launch()


def _exchange_small(small, rep):
    hbm = pltpu.MemorySpace.HBM
    small_in, rep_in = jax.new_ref(small, memory_space=hbm), jax.new_ref(rep, memory_space=hbm)
    small_out = jax.empty_ref(_sds((NDEV,) + small.shape[1:], F32), memory_space=hbm)
    rep_out = jax.empty_ref(_sds((NDEV,) + rep.shape, F32), memory_space=hbm)
    dma = pltpu.SemaphoreType.DMA

    @pl.kernel(mesh=plsc.ScalarSubcoreMesh(axis_name="seq", num_cores=1), name="exchange_small_grads",
               scratch_types=(dma((2,)), dma((2 * (NDEV - 1),)), dma((2 * (NDEV - 1),))),
               compiler_params=pltpu.CompilerParams(collective_id=4))
    def launch(lsem, ssend, srecv):
        x, y, c, me, _, _ = _place()
        dev = 4 * x + 2 * y + c
        _handshake([((1 - x if r & 4 else x), (1 - y if r & 2 else y), (1 - c if r & 1 else c))
                    for r in range(1, NDEV)])
        local = [pltpu.make_async_copy(small_in.at[me], small_out.at[dev], lsem.at[0]),
                 pltpu.make_async_copy(rep_in, rep_out.at[dev], lsem.at[1])]
        for cp in local:
            cp.start()

        def peer(r):
            return (1 - x if r & 4 else x), (1 - y if r & 2 else y), (1 - c if r & 1 else c)

        def tiny(r, which):
            px, py, pc = peer(r)
            k = (r - 1) * 2 + which
            if which == 0:
                return pltpu.make_async_remote_copy(small_in.at[2 * px + py], small_out.at[dev], ssend.at[k],
                                                    srecv.at[k], device_id=(px, py, pc), device_id_type=MESH)
            return pltpu.make_async_remote_copy(rep_in, rep_out.at[dev], ssend.at[k], srecv.at[k],
                                                device_id=(px, py, pc), device_id_type=MESH)

        def tiny_landed(r, which):
            px, py, pc = peer(r)
            k = (r - 1) * 2 + which
            dst = (small_out if which == 0 else rep_out).at[4 * px + 2 * py + pc]
            return pltpu.make_async_remote_copy(dst, dst, ssend.at[k], srecv.at[k], device_id=(px, py, pc),
                                                device_id_type=MESH)

        sends = [tiny(r, w) for r in range(1, NDEV) for w in range(2)]
        for cp in sends:
            cp.start()
        for r in range(1, NDEV):
            for w in range(2):
                tiny_landed(r, w).wait_recv()
        for cp in sends:
            cp.wait_send()
        for cp in local:
            cp.wait()

    launch()
    return small_out, rep_out


def _adamw_math(w, g, m, v):
    m = B1 * m + (1.0 - B1) * g
    v = B2 * v + (1.0 - B2) * (g * g)
    m_hat = m / (1.0 - B1 ** STEP)
    v_hat = v / (1.0 - B2 ** STEP)
    return -LR * (m_hat / (jnp.sqrt(v_hat) + AEPS) + WD * w), m, v


def _adamw_big(name, w, m, v, parts, row0=0, first=0, outs=None):
    _, _, rows, cols = w.shape
    n = parts.shape[2]
    tr = _row_tile(rows, 352)
    t0 = row0 // tr

    def body(w_ref, m_ref, v_ref, p_ref, *rest):
        g_ref, d_ref, nm_ref, nv_ref = rest[-4:]
        g = p_ref[0].astype(F32)
        for q in range(1, NCHIP):
            g = g + p_ref[q].astype(F32)
        d, nm, nv = _adamw_math(w_ref[...], g, m_ref[...], v_ref[...])
        g_ref[...], d_ref[...], nm_ref[...], nv_ref[...] = g, d, nm, nv

    spec = pl.BlockSpec((None, None, tr, cols), lambda i, p, t: (first + i, p, t, 0))
    na = 0 if outs is None else 4
    return pl.pallas_call(
        body, grid=(n, 2, rows // tr),
        in_specs=[spec, spec, spec,
                  pl.BlockSpec((None, NCHIP, None, tr, cols), lambda i, p, t: (p, 0, i, t0 + t, 0))] + [ANY] * na,
        out_specs=[spec] * 4, out_shape=[_sds(w.shape)] * 4, input_output_aliases={4 + k: k for k in range(na)},
        name=name, compiler_params=_cp("parallel", "parallel", "parallel"),
    )(w, m, v, parts, *(outs or ()))


def _adamw_small(name, w, m, v, parts):
    def body(w_ref, m_ref, v_ref, p_ref, g_ref, d_ref, nm_ref, nv_ref):
        g = p_ref[0]
        for q in range(1, NDEV):
            g = g + p_ref[q]
        d, nm, nv = _adamw_math(w_ref[...], g, m_ref[...], v_ref[...])
        g_ref[...], d_ref[...], nm_ref[...], nv_ref[...] = g, d, nm, nv

    return pl.pallas_call(body, out_shape=[_sds(w.shape)] * 4, name=name)(w, m, v, parts)


def _pack(arrs, rows):
    flat = jnp.concatenate([a.reshape(-1) for a in arrs])
    return jnp.pad(flat, (0, rows * LANES - flat.shape[0])).reshape(rows, LANES)


def _unpack(packed, shapes):
    flat, out, o = packed.reshape(-1), [], 0
    for s in shapes:
        n = 1
        for d in s:
            n *= d
        out.append(flat[o:o + n].reshape(s))
        o += n
    return out


SMALL_ROWS, REP_ROWS = 200, 16


def kernel(x, norm_w, ffn_w_gate, ffn_w_up, ffn_w_down, mix_w_in, dn_conv_w, attn_sinks, dn_a_log, dn_dt_bias, dn_norm_w, mix_w_out, conv_w_pw1, conv_b_pw1, conv_w_dw, conv_b_dw, conv_ln_w, conv_ln_b, conv_w_pw2, conv_b_pw2, final_norm_w, loss_target, m_norm_w, m_ffn_w_gate, m_ffn_w_up, m_ffn_w_down, m_mix_w_in, m_dn_conv_w, m_attn_sinks, m_dn_a_log, m_dn_dt_bias, m_dn_norm_w, m_mix_w_out, m_conv_w_pw1, m_conv_b_pw1, m_conv_w_dw, m_conv_b_dw, m_conv_ln_w, m_conv_ln_b, m_conv_w_pw2, m_conv_b_pw2, m_final_norm_w, v_norm_w, v_ffn_w_gate, v_ffn_w_up, v_ffn_w_down, v_mix_w_in, v_dn_conv_w, v_attn_sinks, v_dn_a_log, v_dn_dt_bias, v_dn_norm_w, v_mix_w_out, v_conv_w_pw1, v_conv_b_pw1, v_conv_w_dw, v_conv_b_dw, v_conv_ln_w, v_conv_ln_b, v_conv_w_pw2, v_conv_b_pw2, v_final_norm_w):
    small_names = ["norm_w", "dn_conv_w", "conv_b_pw1", "conv_w_dw", "conv_b_dw", "conv_ln_w", "conv_ln_b",
                   "conv_b_pw2"]
    rep_names = ["attn_sinks", "dn_a_log", "dn_dt_bias", "dn_norm_w", "final_norm_w"]
    w = dict(norm_w=norm_w, ffn_w_gate=ffn_w_gate, ffn_w_up=ffn_w_up, ffn_w_down=ffn_w_down, mix_w_in=mix_w_in, dn_conv_w=dn_conv_w, attn_sinks=attn_sinks, dn_a_log=dn_a_log, dn_dt_bias=dn_dt_bias, dn_norm_w=dn_norm_w, mix_w_out=mix_w_out, conv_w_pw1=conv_w_pw1, conv_b_pw1=conv_b_pw1, conv_w_dw=conv_w_dw, conv_b_dw=conv_b_dw, conv_ln_w=conv_ln_w, conv_ln_b=conv_ln_b, conv_w_pw2=conv_w_pw2, conv_b_pw2=conv_b_pw2, final_norm_w=final_norm_w)
    m = dict(norm_w=m_norm_w, ffn_w_gate=m_ffn_w_gate, ffn_w_up=m_ffn_w_up, ffn_w_down=m_ffn_w_down, mix_w_in=m_mix_w_in, dn_conv_w=m_dn_conv_w, attn_sinks=m_attn_sinks, dn_a_log=m_dn_a_log, dn_dt_bias=m_dn_dt_bias, dn_norm_w=m_dn_norm_w, mix_w_out=m_mix_w_out, conv_w_pw1=m_conv_w_pw1, conv_b_pw1=m_conv_b_pw1, conv_w_dw=m_conv_w_dw, conv_b_dw=m_conv_b_dw, conv_ln_w=m_conv_ln_w, conv_ln_b=m_conv_ln_b, conv_w_pw2=m_conv_w_pw2, conv_b_pw2=m_conv_b_pw2, final_norm_w=m_final_norm_w)
    v = dict(norm_w=v_norm_w, ffn_w_gate=v_ffn_w_gate, ffn_w_up=v_ffn_w_up, ffn_w_down=v_ffn_w_down, mix_w_in=v_mix_w_in, dn_conv_w=v_dn_conv_w, attn_sinks=v_attn_sinks, dn_a_log=v_dn_a_log, dn_dt_bias=v_dn_dt_bias, dn_norm_w=v_dn_norm_w, mix_w_out=v_mix_w_out, conv_w_pw1=v_conv_w_pw1, conv_b_pw1=v_conv_b_pw1, conv_w_dw=v_conv_w_dw, conv_b_dw=v_conv_b_dw, conv_ln_w=v_conv_ln_w, conv_ln_b=v_conv_ln_b, conv_w_pw2=v_conv_w_pw2, conv_b_pw2=v_conv_b_pw2, final_norm_w=v_final_norm_w)
    order = ["norm_w", "ffn_w_gate", "ffn_w_up", "ffn_w_down", "mix_w_in", "dn_conv_w", "attn_sinks", "dn_a_log",
             "dn_dt_bias", "dn_norm_w", "mix_w_out", "conv_w_pw1", "conv_b_pw1", "conv_w_dw", "conv_b_dw",
             "conv_ln_w", "conv_ln_b", "conv_w_pw2", "conv_b_pw2", "final_norm_w"]

    small_shapes = [w[n].shape for n in small_names]
    rep_shapes = [w[n].shape for n in rep_names]

    def halves(a):
        return a.reshape(a.shape[:-2] + (2, a.shape[-2] // 2, a.shape[-1]))

    tr = lambda a: jnp.swapaxes(a, -1, -2)
    gate_t, up_t = tr(ffn_w_gate), tr(ffn_w_up)

    def layer_shards(l):
        mix_in, mix_out = (mix_w_in, mix_w_out) if l % 2 == 0 else (conv_w_pw1, conv_w_pw2)
        ffn = jnp.concatenate([gate_t[l], up_t[l], ffn_w_down[l]], axis=1)
        return ([t.astype(BF16) for t in (halves(ffn[0]), halves(mix_in[l // 2]), halves(mix_out[l // 2]))],
                [halves(ffn[1]).astype(BF16)])

    first = layer_shards(0)
    first = (first[0] + [_pack([w[n] for n in small_names], SMALL_ROWS)], first[1])
    first, (gate_t, up_t, ffn_w_down, mix_w_in, mix_w_out, conv_w_pw1, conv_w_pw2) = lax.optimization_barrier(
        (first, (gate_t, up_t, ffn_w_down, mix_w_in, mix_w_out, conv_w_pw1, conv_w_pw2)))
    gathering = [(_gather_async("gather_layer0a", first[0][:3], first[0][3:]),
                  _gather_async("gather_layer0b", first[1]))]
    own = [(first[0][:3], first[1])]
    for l in range(1, DEPTH):
        before, after = layer_shards(l)
        gathering.append((_gather_async(f"gather_layer{l}a", before), _gather_async(f"gather_layer{l}b", after)))
        own.append((before, after))
    ffn_block = lambda g: g.reshape(NCHIP, 1, 3 * FS, D)
    me = 2 * lax.axis_index("x") + lax.axis_index("y")
    with_own = lambda g, shard: lax.dynamic_update_slice(g, shard[None], (me, 0, 0, 0))

    def mixer_params(l, w_a, w_b):
        e = l // 2
        w_a = w_a.reshape(NCHIP, D, -1)
        w_b = w_b.reshape(D, D)
        if l % 2 == 0:
            return dict(w_in=w_a, dn_conv_w=sm["dn_conv_w"][e], sinks=_row(attn_sinks[e]), a_log=_row(dn_a_log[e]),
                        dt_bias=_row(dn_dt_bias[e]), dn_norm_w=_row(dn_norm_w[e]), wo_a=w_b[:Q_A], wo_b=w_b[Q_A:])
        return dict(b1a=_row(sm["conv_b_pw1"][e, :D]), b1b=_row(sm["conv_b_pw1"][e, D:]), w1=w_a,
                    w_dw=sm["conv_w_dw"][e], b_dw=_row(sm["conv_b_dw"][e]), ln_w=_row(sm["conv_ln_w"][e]),
                    ln_b=_row(sm["conv_ln_b"][e]), b2=_row(sm["conv_b_pw2"][e]), w2=w_b)

    xs, saved = x[0], []
    for l in range(DEPTH):
        got = [r[...] for r in gathering[l][0]]
        if l == 0:
            per_chip = [_unpack(got[3][q], small_shapes) for q in range(NCHIP)]
            sm = {n: jnp.concatenate([per_chip[q][i] for q in range(NCHIP)], axis=-1)
                  for i, n in enumerate(small_names)}
        else:
            xs, got = lax.optimization_barrier((xs, got))
        got[:3] = [with_own(g, s) for g, s in zip(got[:3], own[l][0])]

        def second_ffn(x2, l=l):
            x2, got_b = lax.optimization_barrier((x2, gathering[l][1][0][...]))
            return x2, ffn_block(with_own(got_b, own[l][1][0]))

        xs, sv = _layer_fwd(l, xs, sm["norm_w"][l], ffn_block(got[0]), second_ffn, mixer_params(l, got[1], got[2]))
        saved.append(sv)
    loss, dx, dfw = _final("final", xs, _row(final_norm_w), loss_target[0])

    hbm = pltpu.MemorySpace.HBM
    row_shapes = dict(ffn=(3 * FS, D), w_in=(D // 2, IN_COLS // NCHIP), w_out=(D // 8, D), pw1=(D // 2, D // 2),
                      pw2=(D // 8, D))
    new_sums = lambda k, n: jax.empty_ref(_sds((2, NCHIP, n) + row_shapes[k], BF16), memory_space=hbm)
    sums_0 = {k: new_sums(k, 1) for k in ("ffn", "w_in", "w_out")}
    sums = dict(ffn=new_sums("ffn", DEPTH - 1), w_in=new_sums("w_in", 1), w_out=new_sums("w_out", 1),
                pw1=new_sums("pw1", 2), pw2=new_sums("pw2", 2))
    c_arr = lax.axis_index("c").astype(jnp.int32).reshape(1)
    dnorm, gmix = [None] * DEPTH, [None] * DEPTH

    def hand_on(l, grads, swapped):
        def run(dx):
            dx, other = lax.optimization_barrier((dx, [r[...] for r in swapped]))
            parts = [_add_half(f"add_half_{l}_{k}", gg, rr, c_arr) for k, (gg, rr) in enumerate(zip(grads, other))]
            dx, parts = lax.optimization_barrier((dx, parts))
            keys = ("ffn", "w_in", "w_out") if l % 2 == 0 else ("ffn", "pw1", "pw2")
            if l == 0:
                _scatter_async("scatter_grads_0", parts, [sums_0[k] for k in keys], [0, 0, 0])
            else:
                _scatter_async(f"scatter_grads_{l}", parts, [sums[k] for k in keys],
                               [l - 1, 0, 0] if l % 2 == 0 else [l - 1, l // 2, l // 2])
            return dx
        return run

    pending = lambda dx: dx
    for l in reversed(range(DEPTH)):
        dx, dnorm[l], dffn, gmix[l] = _layer_bwd(l, dx, sm["norm_w"][l], saved[l], pending)
        if l % 2 == 0:
            g_a, g_b = gmix[l]["w_in"], jnp.concatenate([gmix[l]["wo_a"], gmix[l]["wo_b"]], axis=0)
        else:
            g_a, g_b = gmix[l]["w1"], gmix[l]["w2"]
        g_a = halves(g_a).astype(BF16)
        g_b = g_b.reshape(NCHIP, 2, D // 8, D).astype(BF16)
        dx, grads = lax.optimization_barrier((dx, [dffn, g_a, g_b]))
        pending = hand_on(l, grads, _swap_halves(f"swap_grads_{l}", grads, sums["ffn"] if l < DEPTH - 1 else None))
    gm, gc = [gmix[0], gmix[2]], [gmix[1], gmix[3]]
    small_g = dict(
        norm_w=jnp.stack(dnorm), dn_conv_w=jnp.stack([gm[e]["dn_conv_w"] for e in range(2)]),
        conv_b_pw1=jnp.stack([jnp.concatenate([gc[e]["b1a"], gc[e]["b1b"]], axis=1)[0] for e in range(2)]),
        conv_w_dw=jnp.stack([gc[e]["w_dw"] for e in range(2)]),
        conv_b_dw=jnp.stack([gc[e]["b_dw"][0] for e in range(2)]),
        conv_ln_w=jnp.stack([gc[e]["ln_w"][0] for e in range(2)]),
        conv_ln_b=jnp.stack([gc[e]["ln_b"][0] for e in range(2)]),
        conv_b_pw2=jnp.stack([gc[e]["b2"][0] for e in range(2)]))
    small_by_chip = jnp.stack([_pack([jnp.split(small_g[n], NCHIP, axis=-1)[q] for n in small_names], SMALL_ROWS)
                               for q in range(NCHIP)])
    rep_g = _pack([jnp.stack([gm[e]["sinks"][0] for e in range(2)]), jnp.stack([gm[e]["a_log"][0] for e in range(2)]),
                   jnp.stack([gm[e]["dt_bias"][0] for e in range(2)]),
                   jnp.stack([gm[e]["dn_norm_w"][0] for e in range(2)]), dfw[0]], REP_ROWS)
    dx = pending(dx)
    small_ref, rep_ref = _exchange_small(small_by_chip, rep_g)

    big = (("ffn_w_gate", "ffn", 0), ("ffn_w_up", "ffn", FS), ("ffn_w_down", "ffn", 2 * FS), ("mix_w_in", "w_in", 0),
           ("mix_w_out", "w_out", 0), ("conv_w_pw1", "pw1", 0), ("conv_w_pw2", "pw2", 0))
    views = {n: (tr, tr) if n in ("ffn_w_gate", "ffn_w_up") else (
        (lambda a: a) if w[n].ndim == 4 else halves, lambda o, n=n: o.reshape(w[n].shape)) for n, _, _ in big}
    partial_sums = {k: r[...] for k, r in sums.items()}
    upper = {}
    for n, key, row0 in big:
        view = views[n][0]
        upper[n] = _adamw_big(f"adamw_{n}", view(w[n]), view(m[n]), view(v[n]), partial_sums[key], row0,
                              first=0 if key in ("pw1", "pw2") else 1)
    upper, partial_sums_0 = lax.optimization_barrier((upper, {k: r[...] for k, r in sums_0.items()}))
    res = {}
    for n, key, row0 in big:
        view, back = views[n]
        outs = upper[n] if key not in partial_sums_0 else _adamw_big(
            f"adamw_{n}_0", view(w[n]), view(m[n]), view(v[n]), partial_sums_0[key], row0, first=0, outs=upper[n])
        res[n] = [back(o) for o in outs]
    outs = _adamw_small("adamw_small", *[_pack([d[n] for n in small_names], SMALL_ROWS) for d in (w, m, v)],
                        small_ref[...])
    for i, n in enumerate(small_names):
        res[n] = [_unpack(o, small_shapes)[i] for o in outs]
    outs = _adamw_small("adamw_replicated", *[_pack([d[n] for n in rep_names], REP_ROWS) for d in (w, m, v)],
                        rep_ref[...])
    for i, n in enumerate(rep_names):
        res[n] = [_unpack(o, rep_shapes)[i] for o in outs]

    total = lax.psum(loss[0, 0], ("x", "y", "c"))
    return (total, dx[None], *[res[n][0] for n in order], *[res[n][1] for n in order],
            *[res[n][2] for n in order], *[res[n][3] for n in order])
```
